```python
import math
import jax, jax.numpy as jnp
from jax import lax
import numpy as np

D_MODEL = 1024
BATCH = 8
SEQ = 2048
DEPTH = 4

EPS = 1e-6
NEG = -1e30
FORCE = 1e4

GDN_DK = 128
GDN_DV = 128
GDN_HEADS = (D_MODEL // 2) // GDN_DV
GDN_QK = GDN_HEADS * GDN_DK
GDN_V = GDN_HEADS * GDN_DV
GDN_CONV = 4
GDN_CHUNK = 64

NSA_DH = 64
NSA_HEADS = (D_MODEL // 2) // NSA_DH
NSA_GROUPS = 2
NSA_Q = NSA_HEADS * NSA_DH
NSA_KV = NSA_GROUPS * NSA_DH
CMP_LEN = 32
CMP_STRIDE = 16
CMP_HIDDEN = 2 * NSA_DH
SLC_LEN = 64
SLC_TOPK = 8
WINDOW = 512
Q_BLOCK = 128

GLA_HEADS = 4
GLA_DK = (D_MODEL // 2) // GLA_HEADS
GLA_DV = D_MODEL // GLA_HEADS
GLA_QK = GLA_HEADS * GLA_DK
GLA_V = GLA_HEADS * GLA_DV
GLA_RANK = 16
GLA_TAU = 16.0
GLA_CHUNK = 64

D_FF = ((8 * D_MODEL // 3 + 127) // 128) * 128
FFN_CONV = 3

EVEN_SPLITS = (2 * GDN_QK + GDN_V, GDN_HEADS, GDN_HEADS, GDN_V, NSA_Q, 6 * NSA_KV, 3 * NSA_HEADS)
EVEN_IN = 2 * GDN_QK + GDN_V + 2 * GDN_HEADS + GDN_V + NSA_Q + 6 * NSA_KV + 3 * NSA_HEADS
EVEN_OUT = GDN_V + NSA_Q
ODD_SPLITS = (GLA_QK, GLA_QK, GLA_V, GLA_V, GLA_RANK)
ODD_IN = 2 * GLA_QK + 2 * GLA_V + GLA_RANK

kernel_name = "hybrid_gdn_nsa_gla_convffn"


def split_last(x, sizes):
    idx = np.cumsum(sizes)[:-1].tolist()
    return jnp.split(x, idx, axis=-1)


def rms_norm(x, g):
    xf = x.astype(jnp.float32)
    y = xf * lax.rsqrt(jnp.mean(xf * xf, axis=-1, keepdims=True) + EPS)
    return (y * g.astype(jnp.float32)).astype(x.dtype)


def l2norm(x):
    xf = x.astype(jnp.float32)
    return xf * lax.rsqrt(jnp.sum(xf * xf, axis=-1, keepdims=True) + EPS)


def causal_dwconv(x, w):
    K, C = w.shape
    xp = jnp.pad(x, ((0, 0), (K - 1, 0), (0, 0)))
    return lax.conv_general_dilated(xp, w.astype(x.dtype)[:, None, :], window_strides=(1,), padding='VALID',
                                    dimension_numbers=('NWC', 'WIO', 'NWC'), feature_group_count=C)


def gated_delta_rule(q, k, v, beta, g):
    Bn, S, H, Dk = q.shape
    Dv = v.shape[-1]
    C = GDN_CHUNK
    N = S // C

    def chunks(a):
        a = a.astype(jnp.float32).reshape((Bn, N, C, H) + a.shape[3:])
        return jnp.moveaxis(a, 3, 1)

    q = chunks(q) * (Dk ** -0.5)
    k, v, beta, g = chunks(k), chunks(v), chunks(beta), chunks(g)
    G = jnp.cumsum(g, axis=-1)
    causal = jnp.tril(jnp.ones((C, C), dtype=bool))
    strict = jnp.tril(jnp.ones((C, C), dtype=bool), -1)
    decay = jnp.exp(jnp.where(causal, G[..., :, None] - G[..., None, :], -jnp.inf))
    kb = k * beta[..., None]
    a_kk = jnp.where(strict, jnp.einsum('bhncd,bhnmd->bhncm', kb, k) * decay, 0.0)
    rhs = jnp.concatenate([v * beta[..., None], kb * jnp.exp(G)[..., None]], axis=-1)
    sol = lax.linalg.triangular_solve(a_kk, rhs, left_side=True, lower=True, unit_diagonal=True)
    u, w = sol[..., :Dv], sol[..., Dv:]
    a_qk = jnp.einsum('bhncd,bhnmd->bhncm', q, k) * decay
    qg = q * jnp.exp(G)[..., None]
    g_last = G[..., -1]
    kd = k * jnp.exp(g_last[..., None] - G)[..., None]

    def step(state, inp):
        qg_n, kd_n, u_n, w_n, aqk_n, gl_n = inp
        v_new = u_n - jnp.einsum('bhcd,bhde->bhce', w_n, state)
        o = jnp.einsum('bhcd,bhde->bhce', qg_n, state) + jnp.einsum('bhcm,bhme->bhce', aqk_n, v_new)
        state = state * jnp.exp(gl_n)[..., None, None] + jnp.einsum('bhcd,bhce->bhde', kd_n, v_new)
        return state, o

    xs = tuple(jnp.moveaxis(a, 2, 0) for a in (qg, kd, u, w, a_qk, g_last))
    s0 = jnp.zeros((Bn, H, Dk, Dv), jnp.float32)
    _, o = lax.scan(step, s0, xs)
    return o.transpose(1, 0, 3, 2, 4).reshape(Bn, S, H, Dv)


def compress_blocks(x, pe, w1, b1, w2):
    Bn, G, S, dh = x.shape
    R = CMP_LEN // CMP_STRIDE
    P = x.reshape(Bn, G, S // CMP_STRIDE, CMP_STRIDE, dh)
    Nc = P.shape[2] - R + 1
    blocks = jnp.concatenate([P[:, :, i:i + Nc] for i in range(R)], axis=3)
    flat = (blocks + pe).reshape(Bn, G, Nc, CMP_LEN * dh)
    return jax.nn.silu(flat @ w1 + b1) @ w2


def nsa_attention(q, kc, vc, ks, vs, kw, vw, gate_logits, cmp_pe, cmp_w1, cmp_b1, cmp_w2):
    Bn, S, _ = q.shape
    G, Hg, dh = NSA_GROUPS, NSA_HEADS // NSA_GROUPS, NSA_DH
    scale = dh ** -0.5
    f32 = jnp.float32
    qh = q.astype(f32).reshape(Bn, S, G, Hg, dh).transpose(0, 2, 3, 1, 4)

    def kvh(a):
        return a.astype(f32).reshape(Bn, S, G, dh).transpose(0, 2, 1, 3)

    kc, vc, ks, vs, kw, vw = kvh(kc), kvh(vc), kvh(ks), kvh(vs), kvh(kw), kvh(vw)
    t = jnp.arange(S)

    k_cmp = compress_blocks(kc, cmp_pe[0], cmp_w1[0], cmp_b1[0], cmp_w2[0])
    v_cmp = compress_blocks(vc, cmp_pe[1], cmp_w1[1], cmp_b1[1], cmp_w2[1])
    Nc = k_cmp.shape[2]
    cmp_end = jnp.arange(Nc) * CMP_STRIDE + CMP_LEN - 1
    cmp_valid = cmp_end[None, :] <= t[:, None]
    s = jnp.einsum('bghsd,bgcd->bghsc', qh, k_cmp) * scale
    p_cmp = jnp.where(cmp_valid, jax.nn.softmax(jnp.where(cmp_valid, s, NEG), axis=-1), 0.0)
    o_cmp = jnp.einsum('bghsc,bgcd->bghsd', p_cmp, v_cmp)

    Ns = S // SLC_LEN
    c0 = np.arange(Nc)[:, None] * CMP_STRIDE
    s0 = np.arange(Ns)[None, :] * SLC_LEN
    overlap = np.clip(np.minimum(c0 + CMP_LEN, s0 + SLC_LEN) - np.maximum(c0, s0), 0, None) / CMP_STRIDE
    imp = jnp.einsum('bgsc,cj->bgsj', p_cmp.sum(axis=2), jnp.asarray(overlap, f32))
    blk = jnp.arange(Ns)[None, :]
    cur = (t // SLC_LEN)[:, None]
    slc_valid = blk <= cur
    forced = (blk == 0) | (blk == cur) | (blk == cur - 1)
    score = jnp.where(slc_valid, imp + jnp.where(forced, FORCE, 0.0), NEG)
    n_top = min(SLC_TOPK, Ns)
    _, sel = lax.top_k(score, n_top)

    QB = Q_BLOCK
    NQ = S // QB
    ksb = ks.reshape(Bn, G, Ns, SLC_LEN, dh)
    vsb = vs.reshape(Bn, G, Ns, SLC_LEN, dh)
    kw_pad = jnp.pad(kw, ((0, 0), (0, 0), (WINDOW, 0), (0, 0)))
    vw_pad = jnp.pad(vw, ((0, 0), (0, 0), (WINDOW, 0), (0, 0)))
    take = jax.vmap(jax.vmap(lambda blocks, ix: blocks[ix]))

    def block(args):
        i, q_i, sel_i = args
        t_i = i * QB + jnp.arange(QB)
        kg = take(ksb, sel_i).reshape(Bn, G, QB, n_top * SLC_LEN, dh)
        vg = take(vsb, sel_i).reshape(Bn, G, QB, n_top * SLC_LEN, dh)
        pos = (sel_i[..., None] * SLC_LEN + jnp.arange(SLC_LEN)).reshape(Bn, G, QB, n_top * SLC_LEN)
        m = (pos <= t_i[:, None])[:, :, None]
        s_sel = jnp.einsum('bghqd,bgqnd->bghqn', q_i, kg) * scale
        o_s = jnp.einsum('bghqn,bgqnd->bghqd', jax.nn.softmax(jnp.where(m, s_sel, NEG), axis=-1), vg)
        kwin = lax.dynamic_slice_in_dim(kw_pad, i * QB, WINDOW + QB, axis=2)
        vwin = lax.dynamic_slice_in_dim(vw_pad, i * QB, WINDOW + QB, axis=2)
        kpos = i * QB - WINDOW + jnp.arange(WINDOW + QB)
        mw = (kpos[None, :] <= t_i[:, None]) & (kpos[None, :] > t_i[:, None] - WINDOW) & (kpos[None, :] >= 0)
        s_win = jnp.einsum('bghqd,bgnd->bghqn', q_i, kwin) * scale
        o_w = jnp.einsum('bghqn,bgnd->bghqd', jax.nn.softmax(jnp.where(mw, s_win, NEG), axis=-1), vwin)
        return o_s, o_w

    q_blocks = jnp.moveaxis(qh.reshape(Bn, G, Hg, NQ, QB, dh), 3, 0)
    sel_blocks = jnp.moveaxis(sel.reshape(Bn, G, NQ, QB, n_top), 2, 0)
    o_s, o_w = lax.map(block, (jnp.arange(NQ), q_blocks, sel_blocks))
    o_s = jnp.moveaxis(o_s, 0, 3).reshape(Bn, G, Hg, S, dh)
    o_w = jnp.moveaxis(o_w, 0, 3).reshape(Bn, G, Hg, S, dh)

    gates = jax.nn.sigmoid(gate_logits.astype(f32)).reshape(Bn, S, 3, G, Hg).transpose(2, 0, 3, 4, 1)[..., None]
    o = gates[0] * o_cmp + gates[1] * o_s + gates[2] * o_w
    return o.transpose(0, 3, 1, 2, 4).reshape(Bn, S, NSA_Q)


def even_mixer(h, w_in, w_out, conv_w, A_log, dt_bias, o_norm, cmp_pe, cmp_w1, cmp_b1, cmp_w2):
    Bn, S, _ = h.shape
    qkv_a, beta_l, a_l, gate_a, q_b, kv_b, gate_b = split_last(h @ w_in, EVEN_SPLITS)
    qkv = jax.nn.silu(causal_dwconv(qkv_a, conv_w))
    qa, ka, va = split_last(qkv, (GDN_QK, GDN_QK, GDN_V))
    qa = l2norm(qa.reshape(Bn, S, GDN_HEADS, GDN_DK))
    ka = l2norm(ka.reshape(Bn, S, GDN_HEADS, GDN_DK))
    va = va.reshape(Bn, S, GDN_HEADS, GDN_DV)
    beta = jax.nn.sigmoid(beta_l.astype(jnp.float32))
    g = -jnp.exp(A_log.astype(jnp.float32)) * jax.nn.softplus(a_l.astype(jnp.float32) + dt_bias.astype(jnp.float32))
    oa = gated_delta_rule(qa, ka, va, beta, g)
    oa = rms_norm(oa, o_norm) * jax.nn.silu(gate_a.astype(jnp.float32).reshape(Bn, S, GDN_HEADS, GDN_DV))
    kc, vc, ks, vs, kw, vw = split_last(kv_b, (NSA_KV,) * 6)
    ob = nsa_attention(q_b, kc, vc, ks, vs, kw, vw, gate_b, cmp_pe, cmp_w1, cmp_b1, cmp_w2)
    o = jnp.concatenate([oa.reshape(Bn, S, GDN_V), ob], axis=-1).astype(h.dtype)
    return o @ w_out


def gla_chunked(q, k, v, log_a):
    Bn, S, H, Dk = q.shape
    Dv = v.shape[-1]
    C = GLA_CHUNK
    N = S // C

    def chunks(a):
        a = a.astype(jnp.float32).reshape(Bn, N, C, H, a.shape[-1])
        return jnp.moveaxis(a, 3, 1)

    q = chunks(q) * (Dk ** -0.5)
    k, v, log_a = chunks(k), chunks(v), chunks(log_a)
    b = jnp.cumsum(log_a, axis=3)
    b_last = b[..., -1:, :]
    qt = q * jnp.exp(b)
    kt = k * jnp.exp(-b)
    kh = k * jnp.exp(b_last - b)
    causal = jnp.tril(jnp.ones((C, C), dtype=bool))
    intra = jnp.where(causal, jnp.einsum('bhncd,bhnmd->bhncm', qt, kt), 0.0)
    o_intra = jnp.einsum('bhncm,bhnme->bhnce', intra, v)

    def step(state, inp):
        qt_n, kh_n, v_n, dl_n = inp
        o = jnp.einsum('bhcd,bhde->bhce', qt_n, state)
        state = state * jnp.exp(dl_n)[..., None] + jnp.einsum('bhcd,bhce->bhde', kh_n, v_n)
        return state, o

    xs = tuple(jnp.moveaxis(a, 2, 0) for a in (qt, kh, v, b_last[..., 0, :]))
    s0 = jnp.zeros((Bn, H, Dk, Dv), jnp.float32)
    _, o_inter = lax.scan(step, s0, xs)
    o = o_intra + jnp.moveaxis(o_inter, 0, 2)
    return o.transpose(0, 2, 3, 1, 4).reshape(Bn, S, H, Dv)


def odd_mixer(h, w_in, a_w2, a_b, o_norm, w_out):
    Bn, S, _ = h.shape
    q, k, v, r, a_lr = split_last(h @ w_in, ODD_SPLITS)
    log_a = jax.nn.log_sigmoid((a_lr @ a_w2 + a_b).astype(jnp.float32)) / GLA_TAU
    hd = lambda a, d: a.reshape(Bn, S, GLA_HEADS, d)
    o = gla_chunked(hd(q, GLA_DK), hd(k, GLA_DK), hd(v, GLA_DV), hd(log_a, GLA_DK))
    o = rms_norm(o, o_norm) * jax.nn.silu(hd(r, GLA_DV).astype(jnp.float32))
    return o.reshape(Bn, S, GLA_V).astype(h.dtype) @ w_out


def conv_ffn(h, w_in, conv_w, conv_b, w_out):
    u = causal_dwconv(h @ w_in, conv_w) + conv_b
    gate, val = jnp.split(u, 2, axis=-1)
    return (jax.nn.silu(gate) * val) @ w_out


def setup_inputs(seed: int = 0) -> dict:
    key = jax.random.key(seed)
    ks = jax.random.split(key, 24)
    NE = (DEPTH + 1) // 2
    NO = DEPTH // 2
    f32 = jnp.float32

    def dense(k, shape, fan_in):
        return jax.random.normal(k, shape, f32) * (fan_in ** -0.5)

    def gain(k, shape):
        return 1.0 + 0.05 * jax.random.normal(k, shape, f32)

    def small(k, shape, s=0.02):
        return s * jax.random.normal(k, shape, f32)

    dt = jnp.exp(jax.random.uniform(ks[10], (NE, GDN_HEADS), f32, math.log(1e-3), math.log(1e-1)))
    return {
        "x": jax.random.normal(ks[0], (BATCH, SEQ, D_MODEL), f32),
        "norm_gains": gain(ks[1], (DEPTH, 4, D_MODEL)),
        "ffn_w_in": dense(ks[2], (DEPTH, D_MODEL, 2 * D_FF), D_MODEL),
        "ffn_conv_w": dense(ks[3], (DEPTH, FFN_CONV, 2 * D_FF), FFN_CONV),
        "ffn_conv_b": small(ks[4], (DEPTH, 2 * D_FF)),
        "ffn_w_out": dense(ks[5], (DEPTH, D_FF, D_MODEL), D_FF),
        "hyb_w_in": dense(ks[6], (NE, D_MODEL, EVEN_IN), D_MODEL),
        "hyb_w_out": dense(ks[7], (NE, EVEN_OUT, D_MODEL), EVEN_OUT),
        "gdn_conv_w": dense(ks[8], (NE, GDN_CONV, 2 * GDN_QK + GDN_V), GDN_CONV),
        "gdn_A_log": jnp.log(jax.random.uniform(ks[9], (NE, GDN_HEADS), f32, 1.0, 16.0)),
        "gdn_dt_bias": dt + jnp.log(-jnp.expm1(-dt)),
        "gdn_o_norm": gain(ks[11], (NE, GDN_DV)),
        "nsa_cmp_pe": small(ks[12], (NE, 2, CMP_LEN, NSA_DH)),
        "nsa_cmp_w1": dense(ks[13], (NE, 2, CMP_LEN * NSA_DH, CMP_HIDDEN), CMP_LEN * NSA_DH),
        "nsa_cmp_b1": small(ks[14], (NE, 2, CMP_HIDDEN)),
        "nsa_cmp_w2": dense(ks[15], (NE, 2, CMP_HIDDEN, NSA_DH), CMP_HIDDEN),
        "gla_w_in": dense(ks[16], (NO, D_MODEL, ODD_IN), D_MODEL),
        "gla_a_w2": dense(ks[17], (NO, GLA_RANK, GLA_QK), GLA_RANK),
        "gla_a_b": small(ks[18], (NO, GLA_QK), 0.1),
        "gla_o_norm": gain(ks[19], (NO, GLA_DV)),
        "gla_w_out": dense(ks[20], (NO, GLA_V, D_MODEL), GLA_V),
    }


def reference(x, norm_gains, ffn_w_in, ffn_conv_w, ffn_conv_b, ffn_w_out,
              hyb_w_in, hyb_w_out, gdn_conv_w, gdn_A_log, gdn_dt_bias, gdn_o_norm,
              nsa_cmp_pe, nsa_cmp_w1, nsa_cmp_b1, nsa_cmp_w2,
              gla_w_in, gla_a_w2, gla_a_b, gla_o_norm, gla_w_out):
    for layer in range(DEPTH):
        h = rms_norm(x, norm_gains[layer, 0])
        if layer % 2 == 0:
            e = layer // 2
            mix = even_mixer(h, hyb_w_in[e], hyb_w_out[e], gdn_conv_w[e], gdn_A_log[e], gdn_dt_bias[e],
                             gdn_o_norm[e], nsa_cmp_pe[e], nsa_cmp_w1[e], nsa_cmp_b1[e], nsa_cmp_w2[e])
        else:
            o = layer // 2
            mix = odd_mixer(h, gla_w_in[o], gla_a_w2[o], gla_a_b[o], gla_o_norm[o], gla_w_out[o])
        x = x + rms_norm(mix, norm_gains[layer, 1])
        h = rms_norm(x, norm_gains[layer, 2])
        x = x + rms_norm(conv_ffn(h, ffn_w_in[layer], ffn_conv_w[layer], ffn_conv_b[layer], ffn_w_out[layer]),
                         norm_gains[layer, 3])
    return x
```

```python
import functools
import math

import jax
import jax.numpy as jnp
import numpy as np
from jax import lax
from jax.experimental import pallas as pl
from jax.experimental.pallas import tpu as pltpu

D_MODEL = 1024
BATCH = 8
SEQ = 2048
DEPTH = 4
TOKENS = BATCH * SEQ

EPS = 1e-6
NEG = -1e30
FORCE = 1e4

GDN_DK = 128
GDN_DV = 128
GDN_HEADS = 4
GDN_QK = GDN_HEADS * GDN_DK
GDN_V = GDN_HEADS * GDN_DV
GDN_CONV = 4
GDN_CHUNK = 64

NSA_DH = 64
NSA_HEADS = 8
NSA_GROUPS = 2
NSA_Q = NSA_HEADS * NSA_DH
NSA_KV = NSA_GROUPS * NSA_DH
CMP_LEN = 32
CMP_STRIDE = 16
CMP_HIDDEN = 2 * NSA_DH
SLC_LEN = 64
SLC_TOPK = 8
WINDOW = 512
Q_BLOCK = 128

GLA_HEADS = 4
GLA_DK = 128
GLA_DV = 256
GLA_QK = GLA_HEADS * GLA_DK
GLA_V = GLA_HEADS * GLA_DV
GLA_RANK = 16
GLA_TAU = 16.0
GLA_CHUNK = 64

D_FF = 2816
FFN_CONV = 3

EVEN_SPLITS = (2 * GDN_QK + GDN_V, GDN_HEADS, GDN_HEADS, GDN_V, NSA_Q, 6 * NSA_KV, 3 * NSA_HEADS)
ODD_SPLITS = (GLA_QK, GLA_QK, GLA_V, GLA_V, GLA_RANK)

LANE = 128
VMEM_LIMIT = 56 * 1024 * 1024

F32 = jnp.float32
BF16 = jnp.bfloat16


def _round_up(n, m):
    return (n + m - 1) // m * m


def _resident(shape):
    nd = len(shape)
    return pl.BlockSpec(shape, lambda *_: (0,) * nd, pipeline_mode=pl.Buffered(1))


def _rms(x, g):
    ms = jnp.mean(x * x, axis=-1, keepdims=True)
    return x * lax.rsqrt(ms + EPS) * g


PROJ_TM = 256
PROJ_CH = 512


def _proj_kernel(x_ref, g_ref, w_ref, o_ref):
    h = _rms(x_ref[...], g_ref[...]).astype(BF16)
    n = o_ref.shape[1]
    for c0 in range(0, n, PROJ_CH):
        cw = min(PROJ_CH, n - c0)
        o_ref[:, c0:c0 + cw] = jnp.dot(h, w_ref[:, c0:c0 + cw], preferred_element_type=F32)


def norm_proj(x2, gain, w):
    T, D = x2.shape
    N = w.shape[1]
    return pl.pallas_call(
        _proj_kernel,
        grid=(T // PROJ_TM,),
        in_specs=[pl.BlockSpec((PROJ_TM, D), lambda i: (i, 0)), _resident((1, D)), _resident((D, N))],
        out_specs=pl.BlockSpec((PROJ_TM, N), lambda i: (i, 0)),
        out_shape=jax.ShapeDtypeStruct((T, N), F32),
        compiler_params=pltpu.CompilerParams(dimension_semantics=("arbitrary",), vmem_limit_bytes=VMEM_LIMIT),
        name="norm_proj",
    )(x2, gain.reshape(1, D), w)


def _outproj_kernel(o_ref, x_ref, g_ref, w_ref, y_ref):
    mix = jnp.dot(o_ref[...].astype(BF16), w_ref[...], preferred_element_type=F32)
    y_ref[...] = x_ref[...] + _rms(mix, g_ref[...])


def out_proj_residual(o2, x2, gain, w):
    T, K = o2.shape
    D = x2.shape[1]
    return pl.pallas_call(
        _outproj_kernel,
        grid=(T // PROJ_TM,),
        in_specs=[pl.BlockSpec((PROJ_TM, K), lambda i: (i, 0)), pl.BlockSpec((PROJ_TM, D), lambda i: (i, 0)),
                  _resident((1, D)), _resident((K, D))],
        out_specs=pl.BlockSpec((PROJ_TM, D), lambda i: (i, 0)),
        out_shape=jax.ShapeDtypeStruct((T, D), F32),
        compiler_params=pltpu.CompilerParams(dimension_semantics=("arbitrary",), vmem_limit_bytes=VMEM_LIMIT),
        name="out_proj_residual",
    )(o2, x2, gain.reshape(1, D), w)


FFN_TM = 256
FFN_HALO = 16
FFN_CH = 256
assert D_FF % FFN_CH == 0


def _ffn_kernel(x_ref, g_in_ref, w_in_ref, cw_ref, cb_ref, w_out_ref, g_out_ref, y_ref, hext_ref):
    t = pl.program_id(1)

    @pl.when(t == 0)
    def _():
        hext_ref[0:FFN_HALO, :] = jnp.zeros((FFN_HALO, D_MODEL), BF16)

    x = x_ref[...]
    hext_ref[FFN_HALO:, :] = _rms(x, g_in_ref[...]).astype(BF16)
    hext = hext_ref[...]

    def conv_half(c0):
        u = jnp.dot(hext, w_in_ref[:, c0:c0 + FFN_CH], preferred_element_type=F32)
        u1 = pltpu.roll(u, 1, 0)
        u2 = pltpu.roll(u, 2, 0)
        w = cw_ref[:, c0:c0 + FFN_CH]
        y = u2 * w[0:1] + u1 * w[1:2] + u * w[2:3] + cb_ref[:, c0:c0 + FFN_CH]
        return y[FFN_HALO:]

    acc = jnp.zeros((FFN_TM, D_MODEL), F32)
    for c0 in range(0, D_FF, FFN_CH):
        gate = conv_half(c0)
        val = conv_half(D_FF + c0)
        act = (gate * jax.nn.sigmoid(gate) * val).astype(BF16)
        acc = acc + jnp.dot(act, w_out_ref[c0:c0 + FFN_CH, :], preferred_element_type=F32)
    y_ref[...] = x + _rms(acc, g_out_ref[...])
    hext_ref[0:FFN_HALO, :] = hext_ref[FFN_TM:FFN_TM + FFN_HALO, :]


def conv_ffn_residual(x2, g_in, w_in, conv_w, conv_b, w_out, g_out):
    T, D = x2.shape
    nt = SEQ // FFN_TM
    tok = lambda b, t: (b * nt + t, 0)
    return pl.pallas_call(
        _ffn_kernel,
        grid=(BATCH, nt),
        in_specs=[pl.BlockSpec((FFN_TM, D), tok), _resident((1, D)), _resident((D, 2 * D_FF)),
                  _resident((FFN_CONV, 2 * D_FF)), _resident((1, 2 * D_FF)), _resident((D_FF, D)), _resident((1, D))],
        out_specs=pl.BlockSpec((FFN_TM, D), tok),
        out_shape=jax.ShapeDtypeStruct((T, D), F32),
        scratch_shapes=[pltpu.VMEM((FFN_HALO + FFN_TM, D), BF16)],
        compiler_params=pltpu.CompilerParams(dimension_semantics=("arbitrary", "arbitrary"),
                                             vmem_limit_bytes=VMEM_LIMIT),
        name="conv_ffn",
    )(x2, g_in.reshape(1, D), w_in, conv_w, conv_b.reshape(1, -1), w_out, g_out.reshape(1, D))


def split_last(x, sizes):
    idx = np.cumsum(sizes)[:-1].tolist()
    return jnp.split(x, idx, axis=-1)


def rms_norm(x, g):
    xf = x.astype(jnp.float32)
    y = xf * lax.rsqrt(jnp.mean(xf * xf, axis=-1, keepdims=True) + EPS)
    return (y * g.astype(jnp.float32)).astype(x.dtype)


def l2norm(x):
    xf = x.astype(jnp.float32)
    return xf * lax.rsqrt(jnp.sum(xf * xf, axis=-1, keepdims=True) + EPS)


def causal_dwconv(x, w):
    K, C = w.shape
    xp = jnp.pad(x, ((0, 0), (K - 1, 0), (0, 0)))
    return lax.conv_general_dilated(xp, w.astype(x.dtype)[:, None, :], window_strides=(1,), padding='VALID',
                                    dimension_numbers=('NWC', 'WIO', 'NWC'), feature_group_count=C)


def gated_delta_rule(q, k, v, beta, g):
    Bn, S, H, Dk = q.shape
    Dv = v.shape[-1]
    C = GDN_CHUNK
    N = S // C

    def chunks(a):
        a = a.astype(jnp.float32).reshape((Bn, N, C, H) + a.shape[3:])
        return jnp.moveaxis(a, 3, 1)

    q = chunks(q) * (Dk ** -0.5)
    k, v, beta, g = chunks(k), chunks(v), chunks(beta), chunks(g)
    G = jnp.cumsum(g, axis=-1)
    causal = jnp.tril(jnp.ones((C, C), dtype=bool))
    strict = jnp.tril(jnp.ones((C, C), dtype=bool), -1)
    decay = jnp.exp(jnp.where(causal, G[..., :, None] - G[..., None, :], -jnp.inf))
    kb = k * beta[..., None]
    a_kk = jnp.where(strict, jnp.einsum('bhncd,bhnmd->bhncm', kb, k) * decay, 0.0)
    rhs = jnp.concatenate([v * beta[..., None], kb * jnp.exp(G)[..., None]], axis=-1)
    sol = lax.linalg.triangular_solve(a_kk, rhs, left_side=True, lower=True, unit_diagonal=True)
    u, w = sol[..., :Dv], sol[..., Dv:]
    a_qk = jnp.einsum('bhncd,bhnmd->bhncm', q, k) * decay
    qg = q * jnp.exp(G)[..., None]
    g_last = G[..., -1]
    kd = k * jnp.exp(g_last[..., None] - G)[..., None]

    def step(state, inp):
        qg_n, kd_n, u_n, w_n, aqk_n, gl_n = inp
        v_new = u_n - jnp.einsum('bhcd,bhde->bhce', w_n, state)
        o = jnp.einsum('bhcd,bhde->bhce', qg_n, state) + jnp.einsum('bhcm,bhme->bhce', aqk_n, v_new)
        state = state * jnp.exp(gl_n)[..., None, None] + jnp.einsum('bhcd,bhce->bhde', kd_n, v_new)
        return state, o

    xs = tuple(jnp.moveaxis(a, 2, 0) for a in (qg, kd, u, w, a_qk, g_last))
    s0 = jnp.zeros((Bn, H, Dk, Dv), jnp.float32)
    _, o = lax.scan(step, s0, xs)
    return o.transpose(1, 0, 3, 2, 4).reshape(Bn, S, H, Dv)


def compress_blocks(x, pe, w1, b1, w2):
    Bn, G, S, dh = x.shape
    R = CMP_LEN // CMP_STRIDE
    P = x.reshape(Bn, G, S // CMP_STRIDE, CMP_STRIDE, dh)
    Nc = P.shape[2] - R + 1
    blocks = jnp.concatenate([P[:, :, i:i + Nc] for i in range(R)], axis=3)
    flat = (blocks + pe).reshape(Bn, G, Nc, CMP_LEN * dh)
    return jax.nn.silu(flat @ w1 + b1) @ w2


def nsa_attention(q, kc, vc, ks, vs, kw, vw, gate_logits, cmp_pe, cmp_w1, cmp_b1, cmp_w2):
    Bn, S, _ = q.shape
    G, Hg, dh = NSA_GROUPS, NSA_HEADS // NSA_GROUPS, NSA_DH
    scale = dh ** -0.5
    f32 = jnp.float32
    qh = q.astype(f32).reshape(Bn, S, G, Hg, dh).transpose(0, 2, 3, 1, 4)

    def kvh(a):
        return a.astype(f32).reshape(Bn, S, G, dh).transpose(0, 2, 1, 3)

    kc, vc, ks, vs, kw, vw = kvh(kc), kvh(vc), kvh(ks), kvh(vs), kvh(kw), kvh(vw)
    t = jnp.arange(S)
    k_cmp = compress_blocks(kc, cmp_pe[0], cmp_w1[0], cmp_b1[0], cmp_w2[0])
    v_cmp = compress_blocks(vc, cmp_pe[1], cmp_w1[1], cmp_b1[1], cmp_w2[1])
    Nc = k_cmp.shape[2]
    cmp_end = jnp.arange(Nc) * CMP_STRIDE + CMP_LEN - 1
    cmp_valid = cmp_end[None, :] <= t[:, None]
    s = jnp.einsum('bghsd,bgcd->bghsc', qh, k_cmp) * scale
    p_cmp = jnp.where(cmp_valid, jax.nn.softmax(jnp.where(cmp_valid, s, NEG), axis=-1), 0.0)
    o_cmp = jnp.einsum('bghsc,bgcd->bghsd', p_cmp, v_cmp)
    Ns = S // SLC_LEN
    c0 = np.arange(Nc)[:, None] * CMP_STRIDE
    s0 = np.arange(Ns)[None, :] * SLC_LEN
    overlap = np.clip(np.minimum(c0 + CMP_LEN, s0 + SLC_LEN) - np.maximum(c0, s0), 0, None) / CMP_STRIDE
    imp = jnp.einsum('bgsc,cj->bgsj', p_cmp.sum(axis=2), jnp.asarray(overlap, f32))
    blk = jnp.arange(Ns)[None, :]
    cur = (t // SLC_LEN)[:, None]
    slc_valid = blk <= cur
    forced = (blk == 0) | (blk == cur) | (blk == cur - 1)
    score = jnp.where(slc_valid, imp + jnp.where(forced, FORCE, 0.0), NEG)
    n_top = min(SLC_TOPK, Ns)
    _, sel = lax.top_k(score, n_top)
    QB = Q_BLOCK
    NQ = S // QB
    ksb = ks.reshape(Bn, G, Ns, SLC_LEN, dh)
    vsb = vs.reshape(Bn, G, Ns, SLC_LEN, dh)
    kw_pad = jnp.pad(kw, ((0, 0), (0, 0), (WINDOW, 0), (0, 0)))
    vw_pad = jnp.pad(vw, ((0, 0), (0, 0), (WINDOW, 0), (0, 0)))
    take = jax.vmap(jax.vmap(lambda blocks, ix: blocks[ix]))

    def block(args):
        i, q_i, sel_i = args
        t_i = i * QB + jnp.arange(QB)
        kg = take(ksb, sel_i).reshape(Bn, G, QB, n_top * SLC_LEN, dh)
        vg = take(vsb, sel_i).reshape(Bn, G, QB, n_top * SLC_LEN, dh)
        pos = (sel_i[..., None] * SLC_LEN + jnp.arange(SLC_LEN)).reshape(Bn, G, QB, n_top * SLC_LEN)
        m = (pos <= t_i[:, None])[:, :, None]
        s_sel = jnp.einsum('bghqd,bgqnd->bghqn', q_i, kg) * scale
        o_s = jnp.einsum('bghqn,bgqnd->bghqd', jax.nn.softmax(jnp.where(m, s_sel, NEG), axis=-1), vg)
        kwin = lax.dynamic_slice_in_dim(kw_pad, i * QB, WINDOW + QB, axis=2)
        vwin = lax.dynamic_slice_in_dim(vw_pad, i * QB, WINDOW + QB, axis=2)
        kpos = i * QB - WINDOW + jnp.arange(WINDOW + QB)
        mw = (kpos[None, :] <= t_i[:, None]) & (kpos[None, :] > t_i[:, None] - WINDOW) & (kpos[None, :] >= 0)
        s_win = jnp.einsum('bghqd,bgnd->bghqn', q_i, kwin) * scale
        o_w = jnp.einsum('bghqn,bgnd->bghqd', jax.nn.softmax(jnp.where(mw, s_win, NEG), axis=-1), vwin)
        return o_s, o_w

    q_blocks = jnp.moveaxis(qh.reshape(Bn, G, Hg, NQ, QB, dh), 3, 0)
    sel_blocks = jnp.moveaxis(sel.reshape(Bn, G, NQ, QB, n_top), 2, 0)
    o_s, o_w = lax.map(block, (jnp.arange(NQ), q_blocks, sel_blocks))
    o_s = jnp.moveaxis(o_s, 0, 3).reshape(Bn, G, Hg, S, dh)
    o_w = jnp.moveaxis(o_w, 0, 3).reshape(Bn, G, Hg, S, dh)
    gates = jax.nn.sigmoid(gate_logits.astype(f32)).reshape(Bn, S, 3, G, Hg).transpose(2, 0, 3, 4, 1)[..., None]
    o = gates[0] * o_cmp + gates[1] * o_s + gates[2] * o_w
    return o.transpose(0, 3, 1, 2, 4).reshape(Bn, S, NSA_Q)


def even_mixer_core(proj, conv_w, A_log, dt_bias, o_norm, cmp_pe, cmp_w1, cmp_b1, cmp_w2):
    Bn, S, _ = proj.shape
    qkv_a, beta_l, a_l, gate_a, q_b, kv_b, gate_b = split_last(proj, EVEN_SPLITS)
    qkv = jax.nn.silu(causal_dwconv(qkv_a, conv_w))
    qa, ka, va = split_last(qkv, (GDN_QK, GDN_QK, GDN_V))
    qa = l2norm(qa.reshape(Bn, S, GDN_HEADS, GDN_DK))
    ka = l2norm(ka.reshape(Bn, S, GDN_HEADS, GDN_DK))
    va = va.reshape(Bn, S, GDN_HEADS, GDN_DV)
    beta = jax.nn.sigmoid(beta_l.astype(jnp.float32))
    g = -jnp.exp(A_log.astype(jnp.float32)) * jax.nn.softplus(a_l.astype(jnp.float32) + dt_bias.astype(jnp.float32))
    oa = gated_delta_rule(qa, ka, va, beta, g)
    oa = rms_norm(oa, o_norm) * jax.nn.silu(gate_a.astype(jnp.float32).reshape(Bn, S, GDN_HEADS, GDN_DV))
    kc, vc, ks, vs, kw, vw = split_last(kv_b, (NSA_KV,) * 6)
    ob = nsa_attention(q_b, kc, vc, ks, vs, kw, vw, gate_b, cmp_pe, cmp_w1, cmp_b1, cmp_w2)
    return jnp.concatenate([oa.reshape(Bn, S, GDN_V), ob], axis=-1)


def gla_chunked(q, k, v, log_a):
    Bn, S, H, Dk = q.shape
    Dv = v.shape[-1]
    C = GLA_CHUNK
    N = S // C

    def chunks(a):
        a = a.astype(jnp.float32).reshape(Bn, N, C, H, a.shape[-1])
        return jnp.moveaxis(a, 3, 1)

    q = chunks(q) * (Dk ** -0.5)
    k, v, log_a = chunks(k), chunks(v), chunks(log_a)
    b = jnp.cumsum(log_a, axis=3)
    b_last = b[..., -1:, :]
    qt = q * jnp.exp(b)
    kt = k * jnp.exp(-b)
    kh = k * jnp.exp(b_last - b)
    causal = jnp.tril(jnp.ones((C, C), dtype=bool))
    intra = jnp.where(causal, jnp.einsum('bhncd,bhnmd->bhncm', qt, kt), 0.0)
    o_intra = jnp.einsum('bhncm,bhnme->bhnce', intra, v)

    def step(state, inp):
        qt_n, kh_n, v_n, dl_n = inp
        o = jnp.einsum('bhcd,bhde->bhce', qt_n, state)
        state = state * jnp.exp(dl_n)[..., None] + jnp.einsum('bhcd,bhce->bhde', kh_n, v_n)
        return state, o

    xs = tuple(jnp.moveaxis(a, 2, 0) for a in (qt, kh, v, b_last[..., 0, :]))
    s0 = jnp.zeros((Bn, H, Dk, Dv), jnp.float32)
    _, o_inter = lax.scan(step, s0, xs)
    o = o_intra + jnp.moveaxis(o_inter, 0, 2)
    return o.transpose(0, 2, 3, 1, 4).reshape(Bn, S, H, Dv)


def odd_mixer_core(proj, a_w2, a_b, o_norm):
    Bn, S, _ = proj.shape
    q, k, v, r, a_lr = split_last(proj, ODD_SPLITS)
    log_a = jax.nn.log_sigmoid((a_lr @ a_w2 + a_b).astype(jnp.float32)) / GLA_TAU
    hd = lambda a, d: a.reshape(Bn, S, GLA_HEADS, d)
    o = gla_chunked(hd(q, GLA_DK), hd(k, GLA_DK), hd(v, GLA_DV), hd(log_a, GLA_DK))
    o = rms_norm(o, o_norm) * jax.nn.silu(hd(r, GLA_DV).astype(jnp.float32))
    return o.reshape(Bn, S, GLA_V)


def _pad_cols(w, n):
    return jnp.pad(w, ((0, 0), (0, n - w.shape[1])))


def kernel(x, norm_gains, ffn_w_in, ffn_conv_w, ffn_conv_b, ffn_w_out, hyb_w_in, hyb_w_out, gdn_conv_w, gdn_A_log,
           gdn_dt_bias, gdn_o_norm, nsa_cmp_pe, nsa_cmp_w1, nsa_cmp_b1, nsa_cmp_w2, gla_w_in, gla_a_w2, gla_a_b,
           gla_o_norm, gla_w_out):
    x2 = x.reshape(TOKENS, D_MODEL)
    for layer in range(DEPTH):
        if layer % 2 == 0:
            e = layer // 2
            n_in = hyb_w_in.shape[2]
            w = _pad_cols(hyb_w_in[e], _round_up(n_in, LANE)).astype(BF16)
            proj = norm_proj(x2, norm_gains[layer, 0], w)[:, :n_in].reshape(BATCH, SEQ, n_in)
            o = even_mixer_core(proj, gdn_conv_w[e], gdn_A_log[e], gdn_dt_bias[e], gdn_o_norm[e], nsa_cmp_pe[e],
                                nsa_cmp_w1[e], nsa_cmp_b1[e], nsa_cmp_w2[e])
            w_out = hyb_w_out[e]
        else:
            o_ = layer // 2
            n_in = gla_w_in.shape[2]
            w = _pad_cols(gla_w_in[o_], _round_up(n_in, LANE)).astype(BF16)
            proj = norm_proj(x2, norm_gains[layer, 0], w)[:, :n_in].reshape(BATCH, SEQ, n_in)
            o = odd_mixer_core(proj, gla_a_w2[o_], gla_a_b[o_], gla_o_norm[o_])
            w_out = gla_w_out[o_]
        x2 = out_proj_residual(o.reshape(TOKENS, -1), x2, norm_gains[layer, 1], w_out.astype(BF16))
        x2 = conv_ffn_residual(x2, norm_gains[layer, 2], ffn_w_in[layer].astype(BF16), ffn_conv_w[layer],
                               ffn_conv_b[layer], ffn_w_out[layer].astype(BF16), norm_gains[layer, 3])
    return x2.reshape(BATCH, SEQ, D_MODEL)
```

```python
import functools
import math

import jax
import jax.numpy as jnp
import numpy as np
from jax import lax
from jax.experimental import pallas as pl
from jax.experimental.pallas import tpu as pltpu

D_MODEL = 1024
BATCH = 8
SEQ = 2048
DEPTH = 4
TOKENS = BATCH * SEQ

EPS = 1e-6
NEG = -1e30
FORCE = 1e4

GDN_DK = 128
GDN_DV = 128
GDN_HEADS = 4
GDN_QK = GDN_HEADS * GDN_DK
GDN_V = GDN_HEADS * GDN_DV
GDN_CONV = 4
GDN_CHUNK = 64

NSA_DH = 64
NSA_HEADS = 8
NSA_GROUPS = 2
NSA_Q = NSA_HEADS * NSA_DH
NSA_KV = NSA_GROUPS * NSA_DH
CMP_LEN = 32
CMP_STRIDE = 16
CMP_HIDDEN = 2 * NSA_DH
SLC_LEN = 64
SLC_TOPK = 8
WINDOW = 512
Q_BLOCK = 128

GLA_HEADS = 4
GLA_DK = 128
GLA_DV = 256
GLA_QK = GLA_HEADS * GLA_DK
GLA_V = GLA_HEADS * GLA_DV
GLA_RANK = 16
GLA_TAU = 16.0
GLA_CHUNK = 64

D_FF = 2816
FFN_CONV = 3

EVEN_SPLITS = (2 * GDN_QK + GDN_V, GDN_HEADS, GDN_HEADS, GDN_V, NSA_Q, 6 * NSA_KV, 3 * NSA_HEADS)
ODD_SPLITS = (GLA_QK, GLA_QK, GLA_V, GLA_V, GLA_RANK)

LANE = 128
VMEM_LIMIT = 56 * 1024 * 1024

F32 = jnp.float32
BF16 = jnp.bfloat16


def _round_up(n, m):
    return (n + m - 1) // m * m


def _resident(shape):
    nd = len(shape)
    return pl.BlockSpec(shape, lambda *_: (0,) * nd, pipeline_mode=pl.Buffered(1))


def _rms(x, g):
    ms = jnp.mean(x * x, axis=-1, keepdims=True)
    return x * lax.rsqrt(ms + EPS) * g


PROJ_TM = 256
PROJ_CH = 512


def _proj_kernel(x_ref, g_ref, w_ref, o_ref):
    h = _rms(x_ref[...], g_ref[...]).astype(BF16)
    n = o_ref.shape[1]
    for c0 in range(0, n, PROJ_CH):
        cw = min(PROJ_CH, n - c0)
        o_ref[:, c0:c0 + cw] = jnp.dot(h, w_ref[:, c0:c0 + cw], preferred_element_type=F32)


def norm_proj(x2, gain, w):
    T, D = x2.shape
    N = w.shape[1]
    return pl.pallas_call(
        _proj_kernel,
        grid=(T // PROJ_TM,),
        in_specs=[pl.BlockSpec((PROJ_TM, D), lambda i: (i, 0)), _resident((1, D)), _resident((D, N))],
        out_specs=pl.BlockSpec((PROJ_TM, N), lambda i: (i, 0)),
        out_shape=jax.ShapeDtypeStruct((T, N), F32),
        compiler_params=pltpu.CompilerParams(dimension_semantics=("arbitrary",), vmem_limit_bytes=VMEM_LIMIT),
        name="norm_proj",
    )(x2, gain.reshape(1, D), w)


def _outproj_kernel(o_ref, x_ref, g_ref, w_ref, y_ref):
    mix = jnp.dot(o_ref[...].astype(BF16), w_ref[...], preferred_element_type=F32)
    y_ref[...] = x_ref[...] + _rms(mix, g_ref[...])


def out_proj_residual(o2, x2, gain, w):
    T, K = o2.shape
    D = x2.shape[1]
    return pl.pallas_call(
        _outproj_kernel,
        grid=(T // PROJ_TM,),
        in_specs=[pl.BlockSpec((PROJ_TM, K), lambda i: (i, 0)), pl.BlockSpec((PROJ_TM, D), lambda i: (i, 0)),
                  _resident((1, D)), _resident((K, D))],
        out_specs=pl.BlockSpec((PROJ_TM, D), lambda i: (i, 0)),
        out_shape=jax.ShapeDtypeStruct((T, D), F32),
        compiler_params=pltpu.CompilerParams(dimension_semantics=("arbitrary",), vmem_limit_bytes=VMEM_LIMIT),
        name="out_proj_residual",
    )(o2, x2, gain.reshape(1, D), w)


FFN_TM = 256
FFN_HALO = 16
FFN_CH = 256
assert D_FF % FFN_CH == 0


def _ffn_kernel(x_ref, g_in_ref, w_in_ref, cw_ref, cb_ref, w_out_ref, g_out_ref, y_ref, hext_ref):
    t = pl.program_id(1)

    @pl.when(t == 0)
    def _():
        hext_ref[0:FFN_HALO, :] = jnp.zeros((FFN_HALO, D_MODEL), BF16)

    x = x_ref[...]
    hext_ref[FFN_HALO:, :] = _rms(x, g_in_ref[...]).astype(BF16)
    hext = hext_ref[...]

    def conv_half(c0):
        u = jnp.dot(hext, w_in_ref[:, c0:c0 + FFN_CH], preferred_element_type=F32)
        u1 = pltpu.roll(u, 1, 0)
        u2 = pltpu.roll(u, 2, 0)
        w = cw_ref[:, c0:c0 + FFN_CH]
        y = u2 * w[0:1] + u1 * w[1:2] + u * w[2:3] + cb_ref[:, c0:c0 + FFN_CH]
        return y[FFN_HALO:]

    acc = jnp.zeros((FFN_TM, D_MODEL), F32)
    for c0 in range(0, D_FF, FFN_CH):
        gate = conv_half(c0)
        val = conv_half(D_FF + c0)
        act = (gate * jax.nn.sigmoid(gate) * val).astype(BF16)
        acc = acc + jnp.dot(act, w_out_ref[c0:c0 + FFN_CH, :], preferred_element_type=F32)
    y_ref[...] = x + _rms(acc, g_out_ref[...])
    hext_ref[0:FFN_HALO, :] = hext_ref[FFN_TM:FFN_TM + FFN_HALO, :]


def conv_ffn_residual(x2, g_in, w_in, conv_w, conv_b, w_out, g_out):
    T, D = x2.shape
    nt = SEQ // FFN_TM
    tok = lambda b, t: (b * nt + t, 0)
    return pl.pallas_call(
        _ffn_kernel,
        grid=(BATCH, nt),
        in_specs=[pl.BlockSpec((FFN_TM, D), tok), _resident((1, D)), _resident((D, 2 * D_FF)),
                  _resident((FFN_CONV, 2 * D_FF)), _resident((1, 2 * D_FF)), _resident((D_FF, D)), _resident((1, D))],
        out_specs=pl.BlockSpec((FFN_TM, D), tok),
        out_shape=jax.ShapeDtypeStruct((T, D), F32),
        scratch_shapes=[pltpu.VMEM((FFN_HALO + FFN_TM, D), BF16)],
        compiler_params=pltpu.CompilerParams(dimension_semantics=("arbitrary", "arbitrary"),
                                             vmem_limit_bytes=VMEM_LIMIT),
        name="conv_ffn",
    )(x2, g_in.reshape(1, D), w_in, conv_w, conv_b.reshape(1, -1), w_out, g_out.reshape(1, D))


def split_last(x, sizes):
    idx = np.cumsum(sizes)[:-1].tolist()
    return jnp.split(x, idx, axis=-1)


def rms_norm(x, g):
    xf = x.astype(jnp.float32)
    y = xf * lax.rsqrt(jnp.mean(xf * xf, axis=-1, keepdims=True) + EPS)
    return (y * g.astype(jnp.float32)).astype(x.dtype)


def l2norm(x):
    xf = x.astype(jnp.float32)
    return xf * lax.rsqrt(jnp.sum(xf * xf, axis=-1, keepdims=True) + EPS)


def causal_dwconv(x, w):
    K, C = w.shape
    xp = jnp.pad(x, ((0, 0), (K - 1, 0), (0, 0)))
    return lax.conv_general_dilated(xp, w.astype(x.dtype)[:, None, :], window_strides=(1,), padding='VALID',
                                    dimension_numbers=('NWC', 'WIO', 'NWC'), feature_group_count=C)


def gated_delta_rule(q, k, v, beta, g):
    Bn, S, H, Dk = q.shape
    Dv = v.shape[-1]
    C = GDN_CHUNK
    N = S // C

    def chunks(a):
        a = a.astype(jnp.float32).reshape((Bn, N, C, H) + a.shape[3:])
        return jnp.moveaxis(a, 3, 1)

    q = chunks(q) * (Dk ** -0.5)
    k, v, beta, g = chunks(k), chunks(v), chunks(beta), chunks(g)
    G = jnp.cumsum(g, axis=-1)
    causal = jnp.tril(jnp.ones((C, C), dtype=bool))
    strict = jnp.tril(jnp.ones((C, C), dtype=bool), -1)
    decay = jnp.exp(jnp.where(causal, G[..., :, None] - G[..., None, :], -jnp.inf))
    kb = k * beta[..., None]
    a_kk = jnp.where(strict, jnp.einsum('bhncd,bhnmd->bhncm', kb, k) * decay, 0.0)
    rhs = jnp.concatenate([v * beta[..., None], kb * jnp.exp(G)[..., None]], axis=-1)
    sol = lax.linalg.triangular_solve(a_kk, rhs, left_side=True, lower=True, unit_diagonal=True)
    u, w = sol[..., :Dv], sol[..., Dv:]
    a_qk = jnp.einsum('bhncd,bhnmd->bhncm', q, k) * decay
    qg = q * jnp.exp(G)[..., None]
    g_last = G[..., -1]
    kd = k * jnp.exp(g_last[..., None] - G)[..., None]

    def step(state, inp):
        qg_n, kd_n, u_n, w_n, aqk_n, gl_n = inp
        v_new = u_n - jnp.einsum('bhcd,bhde->bhce', w_n, state)
        o = jnp.einsum('bhcd,bhde->bhce', qg_n, state) + jnp.einsum('bhcm,bhme->bhce', aqk_n, v_new)
        state = state * jnp.exp(gl_n)[..., None, None] + jnp.einsum('bhcd,bhce->bhde', kd_n, v_new)
        return state, o

    xs = tuple(jnp.moveaxis(a, 2, 0) for a in (qg, kd, u, w, a_qk, g_last))
    s0 = jnp.zeros((Bn, H, Dk, Dv), jnp.float32)
    _, o = lax.scan(step, s0, xs)
    return o.transpose(1, 0, 3, 2, 4).reshape(Bn, S, H, Dv)


N_SLC = SEQ // SLC_LEN
N_CMP = SEQ // CMP_STRIDE
NSA_HG = NSA_HEADS // NSA_GROUPS
NSA_TQ = 128
SEL_KC = 256
WIN_KC = 128
N_WIN_CH = WINDOW // WIN_KC + 1
NSA_SCALE = NSA_DH ** -0.5


def _dot_nt(a, b):
    return lax.dot_general(a, b, (((1,), (1,)), ((), ())), preferred_element_type=F32)


def _cmp_kernel(p_ref, pe_ref, w1_ref, b1_ref, w2_ref, o_ref):
    half = CMP_STRIDE * NSA_DH
    for kv in range(2):
        for g in range(NSA_GROUPS):
            rows = p_ref[0, kv * NSA_GROUPS + g]
            lo = (rows + pe_ref[kv, :, :half]).astype(BF16)
            hi = (rows + pe_ref[kv, :, half:]).astype(BF16)
            a = jnp.dot(lo, w1_ref[kv, :half, :], preferred_element_type=F32)
            b = jnp.dot(hi, w1_ref[kv, half:, :], preferred_element_type=F32)
            hid = a + pltpu.roll(b, N_CMP - 1, 0) + b1_ref[kv]
            act = (hid * jax.nn.sigmoid(hid)).astype(BF16)
            o_ref[0, kv * NSA_GROUPS + g] = jnp.dot(act, w2_ref[kv], preferred_element_type=F32).astype(BF16)


def nsa_compress(strips, pe, w1, b1, w2):
    n = 2 * NSA_GROUPS
    return pl.pallas_call(
        _cmp_kernel,
        grid=(BATCH,),
        in_specs=[pl.BlockSpec((1, n, N_CMP, CMP_STRIDE * NSA_DH), lambda b: (b, 0, 0, 0)),
                  _resident((2, 1, CMP_LEN * NSA_DH)), _resident((2, CMP_LEN * NSA_DH, CMP_HIDDEN)),
                  _resident((2, 1, CMP_HIDDEN)), _resident((2, CMP_HIDDEN, NSA_DH))],
        out_specs=pl.BlockSpec((1, n, N_CMP, NSA_DH), lambda b: (b, 0, 0, 0)),
        out_shape=jax.ShapeDtypeStruct((BATCH, n, N_CMP, NSA_DH), BF16),
        compiler_params=pltpu.CompilerParams(dimension_semantics=("arbitrary",), vmem_limit_bytes=VMEM_LIMIT),
        name="nsa_compress",
    )(strips, pe.reshape(2, 1, -1), w1.astype(BF16), b1.reshape(2, 1, -1), w2.astype(BF16))


def _flash_step(q, k, v, bias, carry):
    m, l, acc = carry
    kc = k.shape[0]
    s = _dot_nt(q, k).reshape(NSA_HG, NSA_TQ, kc) + bias[None]
    s = s.reshape(NSA_HG * NSA_TQ, kc)
    m_new = jnp.maximum(m, jnp.max(s, axis=-1, keepdims=True))
    alpha = jnp.exp(m - m_new)
    p = jnp.exp(s - m_new)
    l = alpha * l + jnp.sum(p, axis=-1, keepdims=True)
    acc = alpha * acc + jnp.dot(p.astype(BF16), v, preferred_element_type=F32)
    return m_new, l, acc


def _nsa_kernel(q_ref, ks_ref, vs_ref, kw_ref, vw_ref, cmp_ref, gate_ref, ovl_ref, eye_ref, expand_ref, wbias_ref,
                o_ref, sbias_ref):
    i = pl.program_id(1)
    q0 = i * NSA_TQ
    rows = NSA_HG * NSA_TQ
    gates = jax.nn.sigmoid(gate_ref[...])
    tok = q0 + lax.broadcasted_iota(jnp.int32, (NSA_TQ, 1), 0)
    tok_l = q0 + lax.broadcasted_iota(jnp.int32, (1, NSA_TQ), 1)
    init = (jnp.full((rows, 1), NEG, F32), jnp.zeros((rows, 1), F32), jnp.zeros((rows, NSA_DH), F32))

    for g in range(NSA_GROUPS):
        lanes = slice(g * NSA_DH, (g + 1) * NSA_DH)
        heads = [g * NSA_HG + h for h in range(NSA_HG)]
        q = jnp.concatenate([q_ref[:, h * NSA_DH:(h + 1) * NSA_DH] for h in heads], axis=0)
        q = q * jnp.asarray(NSA_SCALE, BF16)

        s = _dot_nt(q, cmp_ref[0, g]).reshape(NSA_HG, NSA_TQ, N_CMP)
        cmp_end = lax.broadcasted_iota(jnp.int32, (1, N_CMP), 1) * CMP_STRIDE + (CMP_LEN - 1)
        cvalid = (cmp_end <= tok)[None]
        sm = jnp.where(cvalid, s, NEG)
        e = jnp.exp(sm - jnp.max(sm, axis=-1, keepdims=True))
        p = jnp.where(cvalid, e / jnp.sum(e, axis=-1, keepdims=True), 0.0)
        o_cmp = jnp.dot(p.reshape(rows, N_CMP).astype(BF16), cmp_ref[0, NSA_GROUPS + g], preferred_element_type=F32)

        imp = _dot_nt(ovl_ref[...], jnp.sum(p, axis=0).astype(BF16))
        blk = lax.broadcasted_iota(jnp.int32, (N_SLC, NSA_TQ), 0)
        cur = tok_l // SLC_LEN
        bvalid = blk <= cur
        forced = (blk == 0) | (blk == cur) | (blk == cur - 1)
        score = jnp.where(bvalid, imp + jnp.where(forced, FORCE, 0.0), NEG)
        rank = jnp.zeros((N_SLC, NSA_TQ), F32)
        for j in range(N_SLC):
            sj = score[j:j + 1, :]
            tie = jnp.where((sj == score) & (blk > j), 1.0, 0.0)
            rank = rank + jnp.where(sj > score, 1.0, tie)
        sel_t = jnp.where((rank < SLC_TOPK) & bvalid, 1.0, 0.0).astype(BF16)
        sel = _dot_nt(eye_ref[...], sel_t).astype(BF16)
        hit = jnp.dot(sel, expand_ref[...], preferred_element_type=F32)
        kpos = lax.broadcasted_iota(jnp.int32, (1, SEQ), 1)
        sbias = jnp.where((hit > 0.5) & (kpos <= tok), 0.0, NEG)
        for c in range(SEQ // SEL_KC):
            sbias_ref[c] = sbias[:, c * SEL_KC:(c + 1) * SEL_KC]

        def sel_body(c, carry):
            r = pl.ds(pl.multiple_of(c * SEL_KC, SEL_KC), SEL_KC)
            return _flash_step(q, ks_ref[r, lanes], vs_ref[r, lanes], sbias_ref[c], carry)

        _, l_s, acc_s = lax.fori_loop(0, (q0 + NSA_TQ + SEL_KC - 1) // SEL_KC, sel_body, init)

        def win_body(c, carry):
            r = pl.ds(pl.multiple_of(q0 - WINDOW + c * WIN_KC, WIN_KC), WIN_KC)
            return _flash_step(q, kw_ref[r, lanes], vw_ref[r, lanes], wbias_ref[c], carry)

        _, l_w, acc_w = lax.fori_loop(jnp.maximum(0, N_WIN_CH - 1 - i), N_WIN_CH, win_body, init)

        def gate(branch):
            return jnp.concatenate([gates[:, branch * NSA_HEADS + h:branch * NSA_HEADS + h + 1] for h in heads], axis=0)

        o = gate(0) * o_cmp + gate(1) * (acc_s / l_s) + gate(2) * (acc_w / l_w)
        for n, h in enumerate(heads):
            o_ref[:, h * NSA_DH:(h + 1) * NSA_DH] = o[n * NSA_TQ:(n + 1) * NSA_TQ].astype(o_ref.dtype)


def _nsa_constants():
    c0 = np.arange(N_CMP)[:, None] * CMP_STRIDE
    s0 = np.arange(N_SLC)[None, :] * SLC_LEN
    overlap = np.clip(np.minimum(c0 + CMP_LEN, s0 + SLC_LEN) - np.maximum(c0, s0), 0, None) / CMP_STRIDE
    overlap[N_CMP - 1] = 0.0
    expand = (np.arange(SEQ)[None, :] // SLC_LEN == np.arange(N_SLC)[:, None]).astype(np.float32)
    a = np.arange(NSA_TQ)[:, None]
    r = np.arange(N_WIN_CH * WIN_KC)[None, :]
    wbias = np.where((r > a) & (r <= a + WINDOW), 0.0, NEG).astype(np.float32)
    wbias = wbias.reshape(NSA_TQ, N_WIN_CH, WIN_KC).transpose(1, 0, 2)
    return (jnp.asarray(overlap.T, BF16), jnp.asarray(np.eye(NSA_TQ), BF16), jnp.asarray(expand, BF16),
            jnp.asarray(wbias, F32))


def nsa_attention(q, kv, cmp_kv, gate_logits):
    nq = SEQ // NSA_TQ
    ovl_t, eye, expand, wbias = _nsa_constants()
    kv_spec = lambda col: pl.BlockSpec((SEQ, NSA_KV), lambda b, i: (b, col))
    return pl.pallas_call(
        _nsa_kernel,
        grid=(BATCH, nq),
        in_specs=[pl.BlockSpec((NSA_TQ, NSA_Q), lambda b, i: (b * nq + i, 0)),
                  kv_spec(2), kv_spec(3), kv_spec(4), kv_spec(5),
                  pl.BlockSpec((1, 2 * NSA_GROUPS, N_CMP, NSA_DH), lambda b, i: (b, 0, 0, 0)),
                  pl.BlockSpec((NSA_TQ, LANE), lambda b, i: (b * nq + i, 0)),
                  _resident(ovl_t.shape), _resident(eye.shape), _resident(expand.shape), _resident(wbias.shape)],
        out_specs=pl.BlockSpec((NSA_TQ, NSA_Q), lambda b, i: (b * nq + i, 0)),
        out_shape=jax.ShapeDtypeStruct((TOKENS, NSA_Q), BF16),
        scratch_shapes=[pltpu.VMEM((SEQ // SEL_KC, NSA_TQ, SEL_KC), F32)],
        compiler_params=pltpu.CompilerParams(dimension_semantics=("arbitrary", "arbitrary"),
                                             vmem_limit_bytes=VMEM_LIMIT),
        name="nsa_attention",
    )(q, kv, kv, kv, kv, cmp_kv, gate_logits, ovl_t, eye, expand, wbias)


def even_mixer_core(proj, conv_w, A_log, dt_bias, o_norm, cmp_pe, cmp_w1, cmp_b1, cmp_w2):
    Bn, S = BATCH, SEQ
    qkv_a, beta_l, a_l, gate_a, q_b, kv_b, gate_b = split_last(proj.reshape(Bn, S, -1), EVEN_SPLITS)
    qkv = jax.nn.silu(causal_dwconv(qkv_a, conv_w))
    qa, ka, va = split_last(qkv, (GDN_QK, GDN_QK, GDN_V))
    qa = l2norm(qa.reshape(Bn, S, GDN_HEADS, GDN_DK))
    ka = l2norm(ka.reshape(Bn, S, GDN_HEADS, GDN_DK))
    va = va.reshape(Bn, S, GDN_HEADS, GDN_DV)
    beta = jax.nn.sigmoid(beta_l.astype(jnp.float32))
    g = -jnp.exp(A_log.astype(jnp.float32)) * jax.nn.softplus(a_l.astype(jnp.float32) + dt_bias.astype(jnp.float32))
    oa = gated_delta_rule(qa, ka, va, beta, g)
    oa = rms_norm(oa, o_norm) * jax.nn.silu(gate_a.astype(jnp.float32).reshape(Bn, S, GDN_HEADS, GDN_DV))
    strips = kv_b[..., :2 * NSA_KV].reshape(Bn, N_CMP, CMP_STRIDE, 2 * NSA_GROUPS, NSA_DH)
    strips = strips.transpose(0, 3, 1, 2, 4).reshape(Bn, 2 * NSA_GROUPS, N_CMP, CMP_STRIDE * NSA_DH)
    cmp_kv = nsa_compress(strips, cmp_pe, cmp_w1, cmp_b1, cmp_w2)
    gate_pad = jnp.pad(gate_b.reshape(TOKENS, -1), ((0, 0), (0, LANE - 3 * NSA_HEADS)))
    ob = nsa_attention(q_b.reshape(TOKENS, NSA_Q).astype(BF16), kv_b.reshape(TOKENS, -1).astype(BF16), cmp_kv,
                       gate_pad)
    return jnp.concatenate([oa.reshape(TOKENS, GDN_V).astype(BF16), ob], axis=-1)


def gla_chunked(q, k, v, log_a):
    Bn, S, H, Dk = q.shape
    Dv = v.shape[-1]
    C = GLA_CHUNK
    N = S // C

    def chunks(a):
        a = a.astype(jnp.float32).reshape(Bn, N, C, H, a.shape[-1])
        return jnp.moveaxis(a, 3, 1)

    q = chunks(q) * (Dk ** -0.5)
    k, v, log_a = chunks(k), chunks(v), chunks(log_a)
    b = jnp.cumsum(log_a, axis=3)
    b_last = b[..., -1:, :]
    qt = q * jnp.exp(b)
    kt = k * jnp.exp(-b)
    kh = k * jnp.exp(b_last - b)
    causal = jnp.tril(jnp.ones((C, C), dtype=bool))
    intra = jnp.where(causal, jnp.einsum('bhncd,bhnmd->bhncm', qt, kt), 0.0)
    o_intra = jnp.einsum('bhncm,bhnme->bhnce', intra, v)

    def step(state, inp):
        qt_n, kh_n, v_n, dl_n = inp
        o = jnp.einsum('bhcd,bhde->bhce', qt_n, state)
        state = state * jnp.exp(dl_n)[..., None] + jnp.einsum('bhcd,bhce->bhde', kh_n, v_n)
        return state, o

    xs = tuple(jnp.moveaxis(a, 2, 0) for a in (qt, kh, v, b_last[..., 0, :]))
    s0 = jnp.zeros((Bn, H, Dk, Dv), jnp.float32)
    _, o_inter = lax.scan(step, s0, xs)
    o = o_intra + jnp.moveaxis(o_inter, 0, 2)
    return o.transpose(0, 2, 3, 1, 4).reshape(Bn, S, H, Dv)


def odd_mixer_core(proj, a_w2, a_b, o_norm):
    Bn, S, _ = proj.shape
    q, k, v, r, a_lr = split_last(proj, ODD_SPLITS)
    log_a = jax.nn.log_sigmoid((a_lr @ a_w2 + a_b).astype(jnp.float32)) / GLA_TAU
    hd = lambda a, d: a.reshape(Bn, S, GLA_HEADS, d)
    o = gla_chunked(hd(q, GLA_DK), hd(k, GLA_DK), hd(v, GLA_DV), hd(log_a, GLA_DK))
    o = rms_norm(o, o_norm) * jax.nn.silu(hd(r, GLA_DV).astype(jnp.float32))
    return o.reshape(Bn, S, GLA_V)


def _pad_cols(w, n):
    return jnp.pad(w, ((0, 0), (0, n - w.shape[1])))


def kernel(x, norm_gains, ffn_w_in, ffn_conv_w, ffn_conv_b, ffn_w_out, hyb_w_in, hyb_w_out, gdn_conv_w, gdn_A_log,
           gdn_dt_bias, gdn_o_norm, nsa_cmp_pe, nsa_cmp_w1, nsa_cmp_b1, nsa_cmp_w2, gla_w_in, gla_a_w2, gla_a_b,
           gla_o_norm, gla_w_out):
    x2 = x.reshape(TOKENS, D_MODEL)
    for layer in range(DEPTH):
        if layer % 2 == 0:
            e = layer // 2
            n_in = hyb_w_in.shape[2]
            w = _pad_cols(hyb_w_in[e], _round_up(n_in, LANE)).astype(BF16)
            proj = norm_proj(x2, norm_gains[layer, 0], w)[:, :n_in]
            o = even_mixer_core(proj, gdn_conv_w[e], gdn_A_log[e], gdn_dt_bias[e], gdn_o_norm[e], nsa_cmp_pe[e],
                                nsa_cmp_w1[e], nsa_cmp_b1[e], nsa_cmp_w2[e])
            w_out = hyb_w_out[e]
        else:
            o_ = layer // 2
            n_in = gla_w_in.shape[2]
            w = _pad_cols(gla_w_in[o_], _round_up(n_in, LANE)).astype(BF16)
            proj = norm_proj(x2, norm_gains[layer, 0], w)[:, :n_in].reshape(BATCH, SEQ, n_in)
            o = odd_mixer_core(proj, gla_a_w2[o_], gla_a_b[o_], gla_o_norm[o_])
            w_out = gla_w_out[o_]
        x2 = out_proj_residual(o.reshape(TOKENS, -1), x2, norm_gains[layer, 1], w_out.astype(BF16))
        x2 = conv_ffn_residual(x2, norm_gains[layer, 2], ffn_w_in[layer].astype(BF16), ffn_conv_w[layer],
                               ffn_conv_b[layer], ffn_w_out[layer].astype(BF16), norm_gains[layer, 3])
    return x2.reshape(BATCH, SEQ, D_MODEL)
```

```python
import functools
import math

import jax
import jax.numpy as jnp
import numpy as np
from jax import lax
from jax.experimental import pallas as pl
from jax.experimental.pallas import tpu as pltpu

D_MODEL = 1024
BATCH = 8
SEQ = 2048
DEPTH = 4
TOKENS = BATCH * SEQ

EPS = 1e-6
NEG = -1e30
FORCE = 1e4

GDN_DK = 128
GDN_DV = 128
GDN_HEADS = 4
GDN_QK = GDN_HEADS * GDN_DK
GDN_V = GDN_HEADS * GDN_DV
GDN_CONV = 4
GDN_CHUNK = 64

NSA_DH = 64
NSA_HEADS = 8
NSA_GROUPS = 2
NSA_Q = NSA_HEADS * NSA_DH
NSA_KV = NSA_GROUPS * NSA_DH
CMP_LEN = 32
CMP_STRIDE = 16
CMP_HIDDEN = 2 * NSA_DH
SLC_LEN = 64
SLC_TOPK = 8
WINDOW = 512
Q_BLOCK = 128

GLA_HEADS = 4
GLA_DK = 128
GLA_DV = 256
GLA_QK = GLA_HEADS * GLA_DK
GLA_V = GLA_HEADS * GLA_DV
GLA_RANK = 16
GLA_TAU = 16.0
GLA_CHUNK = 64

D_FF = 2816
FFN_CONV = 3

EVEN_SPLITS = (2 * GDN_QK + GDN_V, GDN_HEADS, GDN_HEADS, GDN_V, NSA_Q, 6 * NSA_KV, 3 * NSA_HEADS)
ODD_SPLITS = (GLA_QK, GLA_QK, GLA_V, GLA_V, GLA_RANK)

LANE = 128
VMEM_LIMIT = 56 * 1024 * 1024

F32 = jnp.float32
BF16 = jnp.bfloat16


def _round_up(n, m):
    return (n + m - 1) // m * m


def _resident(shape):
    nd = len(shape)
    return pl.BlockSpec(shape, lambda *_: (0,) * nd, pipeline_mode=pl.Buffered(1))


def _rms(x, g):
    ms = jnp.mean(x * x, axis=-1, keepdims=True)
    return x * lax.rsqrt(ms + EPS) * g


PROJ_TM = 256
PROJ_CH = 512


def _proj_kernel(x_ref, g_ref, w_ref, *o_refs):
    h = _rms(x_ref[...], g_ref[...]).astype(BF16)
    col = 0
    for o_ref in o_refs:
        n = o_ref.shape[1]
        for c0 in range(0, n, PROJ_CH):
            cw = min(PROJ_CH, n - c0)
            y = jnp.dot(h, w_ref[:, col + c0:col + c0 + cw], preferred_element_type=F32)
            o_ref[:, c0:c0 + cw] = y.astype(o_ref.dtype)
        col += n


def norm_proj(x2, gain, w, sections):
    T, D = x2.shape
    N = w.shape[1]
    assert N == sum(n for n, _ in sections) and all(n % LANE == 0 for n, _ in sections)
    return pl.pallas_call(
        _proj_kernel,
        grid=(T // PROJ_TM,),
        in_specs=[pl.BlockSpec((PROJ_TM, D), lambda i: (i, 0)), _resident((1, D)), _resident((D, N))],
        out_specs=[pl.BlockSpec((PROJ_TM, n), lambda i: (i, 0)) for n, _ in sections],
        out_shape=[jax.ShapeDtypeStruct((T, n), dt) for n, dt in sections],
        compiler_params=pltpu.CompilerParams(dimension_semantics=("arbitrary",), vmem_limit_bytes=VMEM_LIMIT),
        name="norm_proj",
    )(x2, gain.reshape(1, D), w)


def _outproj_kernel(o_ref, x_ref, g_ref, w_ref, y_ref):
    mix = jnp.dot(o_ref[...].astype(BF16), w_ref[...], preferred_element_type=F32)
    y_ref[...] = x_ref[...] + _rms(mix, g_ref[...])


def out_proj_residual(o2, x2, gain, w):
    T, K = o2.shape
    D = x2.shape[1]
    return pl.pallas_call(
        _outproj_kernel,
        grid=(T // PROJ_TM,),
        in_specs=[pl.BlockSpec((PROJ_TM, K), lambda i: (i, 0)), pl.BlockSpec((PROJ_TM, D), lambda i: (i, 0)),
                  _resident((1, D)), _resident((K, D))],
        out_specs=pl.BlockSpec((PROJ_TM, D), lambda i: (i, 0)),
        out_shape=jax.ShapeDtypeStruct((T, D), F32),
        compiler_params=pltpu.CompilerParams(dimension_semantics=("arbitrary",), vmem_limit_bytes=VMEM_LIMIT),
        name="out_proj_residual",
    )(o2, x2, gain.reshape(1, D), w)


FFN_TM = 256
FFN_HALO = 16
FFN_CH = 256
assert D_FF % FFN_CH == 0


def _ffn_kernel(x_ref, g_in_ref, w_in_ref, cw_ref, cb_ref, w_out_ref, g_out_ref, y_ref, hext_ref):
    t = pl.program_id(1)

    @pl.when(t == 0)
    def _():
        hext_ref[0:FFN_HALO, :] = jnp.zeros((FFN_HALO, D_MODEL), BF16)

    x = x_ref[...]
    hext_ref[FFN_HALO:, :] = _rms(x, g_in_ref[...]).astype(BF16)
    hext = hext_ref[...]

    def conv_half(c0):
        u = jnp.dot(hext, w_in_ref[:, c0:c0 + FFN_CH], preferred_element_type=F32)
        u1 = pltpu.roll(u, 1, 0)
        u2 = pltpu.roll(u, 2, 0)
        w = cw_ref[:, c0:c0 + FFN_CH]
        y = u2 * w[0:1] + u1 * w[1:2] + u * w[2:3] + cb_ref[:, c0:c0 + FFN_CH]
        return y[FFN_HALO:]

    acc = jnp.zeros((FFN_TM, D_MODEL), F32)
    for c0 in range(0, D_FF, FFN_CH):
        gate = conv_half(c0)
        val = conv_half(D_FF + c0)
        act = (gate * jax.nn.sigmoid(gate) * val).astype(BF16)
        acc = acc + jnp.dot(act, w_out_ref[c0:c0 + FFN_CH, :], preferred_element_type=F32)
    y_ref[...] = x + _rms(acc, g_out_ref[...])
    hext_ref[0:FFN_HALO, :] = hext_ref[FFN_TM:FFN_TM + FFN_HALO, :]


def conv_ffn_residual(x2, g_in, w_in, conv_w, conv_b, w_out, g_out):
    T, D = x2.shape
    nt = SEQ // FFN_TM
    tok = lambda b, t: (b * nt + t, 0)
    return pl.pallas_call(
        _ffn_kernel,
        grid=(BATCH, nt),
        in_specs=[pl.BlockSpec((FFN_TM, D), tok), _resident((1, D)), _resident((D, 2 * D_FF)),
                  _resident((FFN_CONV, 2 * D_FF)), _resident((1, 2 * D_FF)), _resident((D_FF, D)), _resident((1, D))],
        out_specs=pl.BlockSpec((FFN_TM, D), tok),
        out_shape=jax.ShapeDtypeStruct((T, D), F32),
        scratch_shapes=[pltpu.VMEM((FFN_HALO + FFN_TM, D), BF16)],
        compiler_params=pltpu.CompilerParams(dimension_semantics=("arbitrary", "arbitrary"),
                                             vmem_limit_bytes=VMEM_LIMIT),
        name="conv_ffn",
    )(x2, g_in.reshape(1, D), w_in, conv_w, conv_b.reshape(1, -1), w_out, g_out.reshape(1, D))


def split_last(x, sizes):
    idx = np.cumsum(sizes)[:-1].tolist()
    return jnp.split(x, idx, axis=-1)


def rms_norm(x, g):
    xf = x.astype(jnp.float32)
    y = xf * lax.rsqrt(jnp.mean(xf * xf, axis=-1, keepdims=True) + EPS)
    return (y * g.astype(jnp.float32)).astype(x.dtype)


def l2norm(x):
    xf = x.astype(jnp.float32)
    return xf * lax.rsqrt(jnp.sum(xf * xf, axis=-1, keepdims=True) + EPS)


def causal_dwconv(x, w):
    K, C = w.shape
    xp = jnp.pad(x, ((0, 0), (K - 1, 0), (0, 0)))
    return lax.conv_general_dilated(xp, w.astype(x.dtype)[:, None, :], window_strides=(1,), padding='VALID',
                                    dimension_numbers=('NWC', 'WIO', 'NWC'), feature_group_count=C)


def _dot_nt(a, b):
    return lax.dot_general(a, b, (((1,), (1,)), ((), ())), preferred_element_type=F32)


def _dot_tn(a, b):
    return lax.dot_general(a, b, (((0,), (0,)), ((), ())), preferred_element_type=F32)


def _split3(a):
    hi = a.astype(BF16)
    r = a - hi.astype(F32)
    mid = r.astype(BF16)
    return hi, mid, (r - mid.astype(F32)).astype(BF16)


def _dot_x3(a, b):
    a_hi, a_lo, _ = _split3(a)
    b_hi, b_lo, _ = _split3(b)
    d = lambda p, q: jnp.dot(p, q, preferred_element_type=F32)
    return d(a_hi, b_hi) + (d(a_lo, b_hi) + d(a_hi, b_lo))


def _cumsum_rows(tri, g):
    return sum(jnp.dot(tri, t, preferred_element_type=F32) for t in reversed(_split3(g)))


def _unit_lower_inverse(a):
    c = a.shape[0]
    eye = jnp.where(lax.broadcasted_iota(jnp.int32, (c, c), 0) == lax.broadcasted_iota(jnp.int32, (c, c), 1), 1.0, 0.0)
    p = -a
    t = eye + p
    k = 2
    while k < c:
        p = _dot_x3(p, p)
        t = t + _dot_x3(t, p)
        k *= 2
    return t


GDN_TC = 256
GDN_HALO = 8
GDN_SMALL_BETA = 0
GDN_SMALL_DECAY = GDN_HEADS
NSA_SMALL_GATE = 2 * GDN_HEADS


def _gdn_kernel(x_ref, gate_ref, small_ref, cw_ref, alog_ref, dtb_ref, onorm_ref, tri_ref, o_ref, xext_ref, state_ref):
    t = pl.program_id(1)
    C = GDN_CHUNK

    @pl.when(t == 0)
    def _():
        xext_ref[0:GDN_HALO, :] = jnp.zeros((GDN_HALO, xext_ref.shape[1]), F32)
        state_ref[...] = jnp.zeros(state_ref.shape, F32)

    xext_ref[GDN_HALO:, :] = x_ref[...]
    small = small_ref[...]
    beta_all = jax.nn.sigmoid(small)
    z = small + dtb_ref[...]
    softplus = jnp.maximum(z, 0.0) + jnp.log(1.0 + jnp.exp(-jnp.abs(z)))
    g_all = -jnp.exp(alog_ref[...]) * softplus
    tri = tri_ref[...]
    row = lax.broadcasted_iota(jnp.int32, (C, C), 0)
    col = lax.broadcasted_iota(jnp.int32, (C, C), 1)

    def conv_silu(c0, width):
        xe = xext_ref[:, c0:c0 + width]
        w = cw_ref[:, c0:c0 + width]
        y = xe * w[GDN_CONV - 1:GDN_CONV]
        for s in range(1, GDN_CONV):
            y = y + pltpu.roll(xe, s, 0) * w[GDN_CONV - 1 - s:GDN_CONV - s]
        y = y[GDN_HALO:]
        return y * jax.nn.sigmoid(y)

    for h in range(GDN_HEADS):
        q_all = conv_silu(h * GDN_DK, GDN_DK)
        k_all = conv_silu(GDN_QK + h * GDN_DK, GDN_DK)
        v_all = conv_silu(2 * GDN_QK + h * GDN_DV, GDN_DV)
        q_all = q_all * (lax.rsqrt(jnp.sum(q_all * q_all, axis=-1, keepdims=True) + EPS) * GDN_DK ** -0.5)
        k_all = k_all * lax.rsqrt(jnp.sum(k_all * k_all, axis=-1, keepdims=True) + EPS)
        state = state_ref[h]
        for c in range(GDN_TC // C):
            rows = slice(c * C, (c + 1) * C)
            q, k, v = q_all[rows], k_all[rows], v_all[rows]
            beta = beta_all[rows, GDN_SMALL_BETA + h:GDN_SMALL_BETA + h + 1]
            gsum = _cumsum_rows(tri, g_all[rows])
            gcol = gsum[:, GDN_SMALL_DECAY + h:GDN_SMALL_DECAY + h + 1]
            grow = jnp.transpose(gsum)[GDN_SMALL_DECAY + h:GDN_SMALL_DECAY + h + 1, :]
            glast = gcol[C - 1:C]
            decay = jnp.exp(jnp.where(row >= col, gcol - grow, -jnp.inf))
            kb = k * beta
            kbf = k.astype(BF16)
            a_kk = jnp.where(row > col, _dot_nt(kb.astype(BF16), kbf) * decay, 0.0)
            a_qk = _dot_nt(q.astype(BF16), kbf) * decay
            eg = jnp.exp(gcol)
            rhs = jnp.concatenate([v * beta, kb * eg], axis=-1)
            sol = _dot_x3(_unit_lower_inverse(a_kk), rhs)
            u, w = sol[:, :GDN_DV], sol[:, GDN_DV:]
            sb = state.astype(BF16)
            v_new = u - jnp.dot(w.astype(BF16), sb, preferred_element_type=F32)
            o = (jnp.dot((q * eg).astype(BF16), sb, preferred_element_type=F32)
                 + jnp.dot(a_qk.astype(BF16), v_new.astype(BF16), preferred_element_type=F32))
            kd = k * jnp.exp(glast - gcol)
            state = state * jnp.exp(glast) + _dot_tn(kd.astype(BF16), v_new.astype(BF16))
            gate = gate_ref[rows, h * GDN_DV:(h + 1) * GDN_DV]
            on = o * lax.rsqrt(jnp.mean(o * o, axis=-1, keepdims=True) + EPS) * onorm_ref[...]
            o_ref[rows, h * GDN_DV:(h + 1) * GDN_DV] = (on * (gate * jax.nn.sigmoid(gate))).astype(o_ref.dtype)
        state_ref[h] = state
    xext_ref[0:GDN_HALO, :] = xext_ref[GDN_TC:GDN_TC + GDN_HALO, :]


def gated_delta_net(p32, conv_w, a_log, dt_bias, o_norm, qkv_col, gate_col, small_col):
    nt = SEQ // GDN_TC
    width = 2 * GDN_QK + GDN_V
    lane_vec = lambda a, off: jnp.zeros((1, LANE), F32).at[0, off:off + GDN_HEADS].set(a.astype(F32))
    tri = jnp.asarray(np.tril(np.ones((GDN_CHUNK, GDN_CHUNK))), BF16)
    tok = lambda blk: (lambda b, t: (b * nt + t, blk))
    return pl.pallas_call(
        _gdn_kernel,
        grid=(BATCH, nt),
        in_specs=[pl.BlockSpec((GDN_TC, width), tok(qkv_col // width)),
                  pl.BlockSpec((GDN_TC, GDN_V), tok(gate_col // GDN_V)),
                  pl.BlockSpec((GDN_TC, LANE), tok(small_col // LANE)),
                  _resident((GDN_CONV, width)), _resident((1, LANE)), _resident((1, LANE)), _resident((1, GDN_DV)),
                  _resident((GDN_CHUNK, GDN_CHUNK))],
        out_specs=pl.BlockSpec((GDN_TC, GDN_V), tok(0)),
        out_shape=jax.ShapeDtypeStruct((TOKENS, GDN_V), BF16),
        scratch_shapes=[pltpu.VMEM((GDN_HALO + GDN_TC, width), F32), pltpu.VMEM((GDN_HEADS, GDN_DK, GDN_DV), F32)],
        compiler_params=pltpu.CompilerParams(dimension_semantics=("arbitrary", "arbitrary"),
                                             vmem_limit_bytes=VMEM_LIMIT),
        name="gated_delta_net",
    )(p32, p32, p32, conv_w, lane_vec(a_log, GDN_SMALL_DECAY), lane_vec(dt_bias, GDN_SMALL_DECAY),
      o_norm.reshape(1, GDN_DV), tri)


N_SLC = SEQ // SLC_LEN
N_CMP = SEQ // CMP_STRIDE
NSA_HG = NSA_HEADS // NSA_GROUPS
NSA_TQ = 128
SEL_KC = 256
WIN_KC = 128
N_WIN_CH = WINDOW // WIN_KC + 1
NSA_SCALE = NSA_DH ** -0.5


def _cmp_kernel(p_ref, pe_ref, w1_ref, b1_ref, w2_ref, o_ref):
    half = CMP_STRIDE * NSA_DH
    for kv in range(2):
        for g in range(NSA_GROUPS):
            rows = p_ref[0, kv * NSA_GROUPS + g]
            lo = (rows + pe_ref[kv, :, :half]).astype(BF16)
            hi = (rows + pe_ref[kv, :, half:]).astype(BF16)
            a = jnp.dot(lo, w1_ref[kv, :half, :], preferred_element_type=F32)
            b = jnp.dot(hi, w1_ref[kv, half:, :], preferred_element_type=F32)
            hid = a + pltpu.roll(b, N_CMP - 1, 0) + b1_ref[kv]
            act = (hid * jax.nn.sigmoid(hid)).astype(BF16)
            o_ref[0, kv * NSA_GROUPS + g] = jnp.dot(act, w2_ref[kv], preferred_element_type=F32).astype(BF16)


def nsa_compress(strips, pe, w1, b1, w2):
    n = 2 * NSA_GROUPS
    return pl.pallas_call(
        _cmp_kernel,
        grid=(BATCH,),
        in_specs=[pl.BlockSpec((1, n, N_CMP, CMP_STRIDE * NSA_DH), lambda b: (b, 0, 0, 0)),
                  _resident((2, 1, CMP_LEN * NSA_DH)), _resident((2, CMP_LEN * NSA_DH, CMP_HIDDEN)),
                  _resident((2, 1, CMP_HIDDEN)), _resident((2, CMP_HIDDEN, NSA_DH))],
        out_specs=pl.BlockSpec((1, n, N_CMP, NSA_DH), lambda b: (b, 0, 0, 0)),
        out_shape=jax.ShapeDtypeStruct((BATCH, n, N_CMP, NSA_DH), BF16),
        compiler_params=pltpu.CompilerParams(dimension_semantics=("arbitrary",), vmem_limit_bytes=VMEM_LIMIT),
        name="nsa_compress",
    )(strips, pe.reshape(2, 1, -1), w1.astype(BF16), b1.reshape(2, 1, -1), w2.astype(BF16))


def _flash_step(q, k, v, bias, carry):
    m, l, acc = carry
    kc = k.shape[0]
    s = _dot_nt(q, k).reshape(NSA_HG, NSA_TQ, kc) + bias[None]
    s = s.reshape(NSA_HG * NSA_TQ, kc)
    m_new = jnp.maximum(m, jnp.max(s, axis=-1, keepdims=True))
    alpha = jnp.exp(m - m_new)
    p = jnp.exp(s - m_new)
    l = alpha * l + jnp.sum(p, axis=-1, keepdims=True)
    acc = alpha * acc + jnp.dot(p.astype(BF16), v, preferred_element_type=F32)
    return m_new, l, acc


def _nsa_kernel(q_ref, ks_ref, vs_ref, kw_ref, vw_ref, cmp_ref, gate_ref, ovl_ref, eye_ref, expand_ref, wbias_ref,
                o_ref, sbias_ref):
    i = pl.program_id(1)
    q0 = i * NSA_TQ
    rows = NSA_HG * NSA_TQ
    gates = jax.nn.sigmoid(gate_ref[...])
    tok = q0 + lax.broadcasted_iota(jnp.int32, (NSA_TQ, 1), 0)
    tok_l = q0 + lax.broadcasted_iota(jnp.int32, (1, NSA_TQ), 1)
    init = (jnp.full((rows, 1), NEG, F32), jnp.zeros((rows, 1), F32), jnp.zeros((rows, NSA_DH), F32))

    for g in range(NSA_GROUPS):
        lanes = slice(g * NSA_DH, (g + 1) * NSA_DH)
        heads = [g * NSA_HG + h for h in range(NSA_HG)]
        q = jnp.concatenate([q_ref[:, h * NSA_DH:(h + 1) * NSA_DH] for h in heads], axis=0)
        q = q * jnp.asarray(NSA_SCALE, BF16)

        s = _dot_nt(q, cmp_ref[0, g]).reshape(NSA_HG, NSA_TQ, N_CMP)
        cmp_end = lax.broadcasted_iota(jnp.int32, (1, N_CMP), 1) * CMP_STRIDE + (CMP_LEN - 1)
        cvalid = (cmp_end <= tok)[None]
        sm = jnp.where(cvalid, s, NEG)
        e = jnp.exp(sm - jnp.max(sm, axis=-1, keepdims=True))
        p = jnp.where(cvalid, e / jnp.sum(e, axis=-1, keepdims=True), 0.0)
        o_cmp = jnp.dot(p.reshape(rows, N_CMP).astype(BF16), cmp_ref[0, NSA_GROUPS + g], preferred_element_type=F32)

        imp = _dot_nt(ovl_ref[...], jnp.sum(p, axis=0).astype(BF16))
        blk = lax.broadcasted_iota(jnp.int32, (N_SLC, NSA_TQ), 0)
        cur = tok_l // SLC_LEN
        bvalid = blk <= cur
        forced = (blk == 0) | (blk == cur) | (blk == cur - 1)
        score = jnp.where(bvalid, imp + jnp.where(forced, FORCE, 0.0), NEG)
        rank = jnp.zeros((N_SLC, NSA_TQ), F32)
        for j in range(N_SLC):
            sj = score[j:j + 1, :]
            tie = jnp.where((sj == score) & (blk > j), 1.0, 0.0)
            rank = rank + jnp.where(sj > score, 1.0, tie)
        sel_t = jnp.where((rank < SLC_TOPK) & bvalid, 1.0, 0.0).astype(BF16)
        sel = _dot_nt(eye_ref[...], sel_t).astype(BF16)
        hit = jnp.dot(sel, expand_ref[...], preferred_element_type=F32)
        kpos = lax.broadcasted_iota(jnp.int32, (1, SEQ), 1)
        sbias = jnp.where((hit > 0.5) & (kpos <= tok), 0.0, NEG)
        for c in range(SEQ // SEL_KC):
            sbias_ref[c] = sbias[:, c * SEL_KC:(c + 1) * SEL_KC]

        def sel_body(c, carry):
            r = pl.ds(pl.multiple_of(c * SEL_KC, SEL_KC), SEL_KC)
            return _flash_step(q, ks_ref[r, lanes], vs_ref[r, lanes], sbias_ref[c], carry)

        _, l_s, acc_s = lax.fori_loop(0, (q0 + NSA_TQ + SEL_KC - 1) // SEL_KC, sel_body, init)

        def win_body(c, carry):
            r = pl.ds(pl.multiple_of(q0 - WINDOW + c * WIN_KC, WIN_KC), WIN_KC)
            return _flash_step(q, kw_ref[r, lanes], vw_ref[r, lanes], wbias_ref[c], carry)

        _, l_w, acc_w = lax.fori_loop(jnp.maximum(0, N_WIN_CH - 1 - i), N_WIN_CH, win_body, init)

        def gate(branch):
            lane = NSA_SMALL_GATE + branch * NSA_HEADS
            return jnp.concatenate([gates[:, lane + h:lane + h + 1] for h in heads], axis=0)

        o = gate(0) * o_cmp + gate(1) * (acc_s / l_s) + gate(2) * (acc_w / l_w)
        for n, h in enumerate(heads):
            o_ref[:, h * NSA_DH:(h + 1) * NSA_DH] = o[n * NSA_TQ:(n + 1) * NSA_TQ].astype(o_ref.dtype)


def _nsa_constants():
    c0 = np.arange(N_CMP)[:, None] * CMP_STRIDE
    s0 = np.arange(N_SLC)[None, :] * SLC_LEN
    overlap = np.clip(np.minimum(c0 + CMP_LEN, s0 + SLC_LEN) - np.maximum(c0, s0), 0, None) / CMP_STRIDE
    overlap[N_CMP - 1] = 0.0
    expand = (np.arange(SEQ)[None, :] // SLC_LEN == np.arange(N_SLC)[:, None]).astype(np.float32)
    a = np.arange(NSA_TQ)[:, None]
    r = np.arange(N_WIN_CH * WIN_KC)[None, :]
    wbias = np.where((r > a) & (r <= a + WINDOW), 0.0, NEG).astype(np.float32)
    wbias = wbias.reshape(NSA_TQ, N_WIN_CH, WIN_KC).transpose(1, 0, 2)
    return (jnp.asarray(overlap.T, BF16), jnp.asarray(np.eye(NSA_TQ), BF16), jnp.asarray(expand, BF16),
            jnp.asarray(wbias, F32))


def nsa_attention(p16, cmp_kv, p32, q_col, kv_col, small_col):
    nq = SEQ // NSA_TQ
    ovl_t, eye, expand, wbias = _nsa_constants()
    kv_spec = lambda n: pl.BlockSpec((SEQ, NSA_KV), lambda b, i: (b, kv_col // NSA_KV + n))
    return pl.pallas_call(
        _nsa_kernel,
        grid=(BATCH, nq),
        in_specs=[pl.BlockSpec((NSA_TQ, NSA_Q), lambda b, i: (b * nq + i, q_col // NSA_Q)),
                  kv_spec(0), kv_spec(1), kv_spec(2), kv_spec(3),
                  pl.BlockSpec((1, 2 * NSA_GROUPS, N_CMP, NSA_DH), lambda b, i: (b, 0, 0, 0)),
                  pl.BlockSpec((NSA_TQ, LANE), lambda b, i: (b * nq + i, small_col // LANE)),
                  _resident(ovl_t.shape), _resident(eye.shape), _resident(expand.shape), _resident(wbias.shape)],
        out_specs=pl.BlockSpec((NSA_TQ, NSA_Q), lambda b, i: (b * nq + i, 0)),
        out_shape=jax.ShapeDtypeStruct((TOKENS, NSA_Q), BF16),
        scratch_shapes=[pltpu.VMEM((SEQ // SEL_KC, NSA_TQ, SEL_KC), F32)],
        compiler_params=pltpu.CompilerParams(dimension_semantics=("arbitrary", "arbitrary"),
                                             vmem_limit_bytes=VMEM_LIMIT),
        name="nsa_attention",
    )(p16, p16, p16, p16, p16, cmp_kv, p32, ovl_t, eye, expand, wbias)


EVEN_QKV_COL = 0
EVEN_GATE_COL = 2 * GDN_QK + GDN_V
EVEN_CMP_COL = EVEN_GATE_COL + GDN_V
EVEN_SMALL_COL = EVEN_CMP_COL + 2 * NSA_KV
EVEN_F32_W = EVEN_SMALL_COL + LANE
EVEN_Q_COL = 0
EVEN_KV_COL = NSA_Q
EVEN_BF16_W = NSA_Q + 4 * NSA_KV
assert NSA_SMALL_GATE + 3 * NSA_HEADS <= LANE


def even_proj_weight(w_in):
    qkv_a, beta, dec, gate_a, q_b, kv_b, gate_b = split_last(w_in, EVEN_SPLITS)
    pad = jnp.zeros((w_in.shape[0], LANE - 2 * GDN_HEADS - 3 * NSA_HEADS), w_in.dtype)
    return jnp.concatenate([qkv_a, gate_a, kv_b[:, :2 * NSA_KV], beta, dec, gate_b, pad, q_b, kv_b[:, 2 * NSA_KV:]],
                           axis=1).astype(BF16)


def even_mixer_core(p32, p16, conv_w, A_log, dt_bias, o_norm, cmp_pe, cmp_w1, cmp_b1, cmp_w2):
    oa = gated_delta_net(p32, conv_w, A_log, dt_bias, o_norm, EVEN_QKV_COL, EVEN_GATE_COL, EVEN_SMALL_COL)
    strips = p32[:, EVEN_CMP_COL:EVEN_CMP_COL + 2 * NSA_KV].reshape(BATCH, N_CMP, CMP_STRIDE, 2 * NSA_GROUPS, NSA_DH)
    strips = strips.transpose(0, 3, 1, 2, 4).reshape(BATCH, 2 * NSA_GROUPS, N_CMP, CMP_STRIDE * NSA_DH)
    cmp_kv = nsa_compress(strips, cmp_pe, cmp_w1, cmp_b1, cmp_w2)
    ob = nsa_attention(p16, cmp_kv, p32, EVEN_Q_COL, EVEN_KV_COL, EVEN_SMALL_COL)
    return jnp.concatenate([oa, ob], axis=-1)


def gla_chunked(q, k, v, log_a):
    Bn, S, H, Dk = q.shape
    Dv = v.shape[-1]
    C = GLA_CHUNK
    N = S // C

    def chunks(a):
        a = a.astype(jnp.float32).reshape(Bn, N, C, H, a.shape[-1])
        return jnp.moveaxis(a, 3, 1)

    q = chunks(q) * (Dk ** -0.5)
    k, v, log_a = chunks(k), chunks(v), chunks(log_a)
    b = jnp.cumsum(log_a, axis=3)
    b_last = b[..., -1:, :]
    qt = q * jnp.exp(b)
    kt = k * jnp.exp(-b)
    kh = k * jnp.exp(b_last - b)
    causal = jnp.tril(jnp.ones((C, C), dtype=bool))
    intra = jnp.where(causal, jnp.einsum('bhncd,bhnmd->bhncm', qt, kt), 0.0)
    o_intra = jnp.einsum('bhncm,bhnme->bhnce', intra, v)

    def step(state, inp):
        qt_n, kh_n, v_n, dl_n = inp
        o = jnp.einsum('bhcd,bhde->bhce', qt_n, state)
        state = state * jnp.exp(dl_n)[..., None] + jnp.einsum('bhcd,bhce->bhde', kh_n, v_n)
        return state, o

    xs = tuple(jnp.moveaxis(a, 2, 0) for a in (qt, kh, v, b_last[..., 0, :]))
    s0 = jnp.zeros((Bn, H, Dk, Dv), jnp.float32)
    _, o_inter = lax.scan(step, s0, xs)
    o = o_intra + jnp.moveaxis(o_inter, 0, 2)
    return o.transpose(0, 2, 3, 1, 4).reshape(Bn, S, H, Dv)


def odd_mixer_core(proj, a_w2, a_b, o_norm):
    Bn, S, _ = proj.shape
    q, k, v, r, a_lr = split_last(proj, ODD_SPLITS)
    log_a = jax.nn.log_sigmoid((a_lr @ a_w2 + a_b).astype(jnp.float32)) / GLA_TAU
    hd = lambda a, d: a.reshape(Bn, S, GLA_HEADS, d)
    o = gla_chunked(hd(q, GLA_DK), hd(k, GLA_DK), hd(v, GLA_DV), hd(log_a, GLA_DK))
    o = rms_norm(o, o_norm) * jax.nn.silu(hd(r, GLA_DV).astype(jnp.float32))
    return o.reshape(Bn, S, GLA_V)


def _pad_cols(w, n):
    return jnp.pad(w, ((0, 0), (0, n - w.shape[1])))


def kernel(x, norm_gains, ffn_w_in, ffn_conv_w, ffn_conv_b, ffn_w_out, hyb_w_in, hyb_w_out, gdn_conv_w, gdn_A_log,
           gdn_dt_bias, gdn_o_norm, nsa_cmp_pe, nsa_cmp_w1, nsa_cmp_b1, nsa_cmp_w2, gla_w_in, gla_a_w2, gla_a_b,
           gla_o_norm, gla_w_out):
    x2 = x.reshape(TOKENS, D_MODEL)
    for layer in range(DEPTH):
        if layer % 2 == 0:
            e = layer // 2
            p32, p16 = norm_proj(x2, norm_gains[layer, 0], even_proj_weight(hyb_w_in[e]),
                                 ((EVEN_F32_W, F32), (EVEN_BF16_W, BF16)))
            o = even_mixer_core(p32, p16, gdn_conv_w[e], gdn_A_log[e], gdn_dt_bias[e], gdn_o_norm[e], nsa_cmp_pe[e],
                                nsa_cmp_w1[e], nsa_cmp_b1[e], nsa_cmp_w2[e])
            w_out = hyb_w_out[e]
        else:
            o_ = layer // 2
            n_in = gla_w_in.shape[2]
            w = _pad_cols(gla_w_in[o_], _round_up(n_in, LANE)).astype(BF16)
            (proj,) = norm_proj(x2, norm_gains[layer, 0], w, ((w.shape[1], F32),))
            proj = proj[:, :n_in].reshape(BATCH, SEQ, n_in)
            o = odd_mixer_core(proj, gla_a_w2[o_], gla_a_b[o_], gla_o_norm[o_])
            w_out = gla_w_out[o_]
        x2 = out_proj_residual(o.reshape(TOKENS, -1), x2, norm_gains[layer, 1], w_out.astype(BF16))
        x2 = conv_ffn_residual(x2, norm_gains[layer, 2], ffn_w_in[layer].astype(BF16), ffn_conv_w[layer],
                               ffn_conv_b[layer], ffn_w_out[layer].astype(BF16), norm_gains[layer, 3])
    return x2.reshape(BATCH, SEQ, D_MODEL)
```

```python
import functools
import math

import jax
import jax.numpy as jnp
import numpy as np
from jax import lax
from jax.experimental import pallas as pl
from jax.experimental.pallas import tpu as pltpu

D_MODEL = 1024
BATCH = 8
SEQ = 2048
DEPTH = 4
TOKENS = BATCH * SEQ

EPS = 1e-6
NEG = -1e30
FORCE = 1e4

GDN_DK = 128
GDN_DV = 128
GDN_HEADS = 4
GDN_QK = GDN_HEADS * GDN_DK
GDN_V = GDN_HEADS * GDN_DV
GDN_CONV = 4
GDN_CHUNK = 64

NSA_DH = 64
NSA_HEADS = 8
NSA_GROUPS = 2
NSA_Q = NSA_HEADS * NSA_DH
NSA_KV = NSA_GROUPS * NSA_DH
CMP_LEN = 32
CMP_STRIDE = 16
CMP_HIDDEN = 2 * NSA_DH
SLC_LEN = 64
SLC_TOPK = 8
WINDOW = 512
Q_BLOCK = 128

GLA_HEADS = 4
GLA_DK = 128
GLA_DV = 256
GLA_QK = GLA_HEADS * GLA_DK
GLA_V = GLA_HEADS * GLA_DV
GLA_RANK = 16
GLA_TAU = 16.0
GLA_CHUNK = 64

D_FF = 2816
FFN_CONV = 3

EVEN_SPLITS = (2 * GDN_QK + GDN_V, GDN_HEADS, GDN_HEADS, GDN_V, NSA_Q, 6 * NSA_KV, 3 * NSA_HEADS)
ODD_SPLITS = (GLA_QK, GLA_QK, GLA_V, GLA_V, GLA_RANK)

LANE = 128
VMEM_LIMIT = 56 * 1024 * 1024

F32 = jnp.float32
BF16 = jnp.bfloat16


def _round_up(n, m):
    return (n + m - 1) // m * m


def _resident(shape):
    nd = len(shape)
    return pl.BlockSpec(shape, lambda *_: (0,) * nd, pipeline_mode=pl.Buffered(1))


def _rms(x, g):
    ms = jnp.mean(x * x, axis=-1, keepdims=True)
    return x * lax.rsqrt(ms + EPS) * g


PROJ_TM = 256
PROJ_CH = 512


def _proj_kernel(x_ref, g_ref, w_ref, *o_refs):
    h = _rms(x_ref[...], g_ref[...]).astype(BF16)
    col = 0
    for o_ref in o_refs:
        n = o_ref.shape[1]
        for c0 in range(0, n, PROJ_CH):
            cw = min(PROJ_CH, n - c0)
            y = jnp.dot(h, w_ref[:, col + c0:col + c0 + cw], preferred_element_type=F32)
            o_ref[:, c0:c0 + cw] = y.astype(o_ref.dtype)
        col += n


def norm_proj(x2, gain, w, sections):
    T, D = x2.shape
    N = w.shape[1]
    assert N == sum(n for n, _ in sections) and all(n % LANE == 0 for n, _ in sections)
    return pl.pallas_call(
        _proj_kernel,
        grid=(T // PROJ_TM,),
        in_specs=[pl.BlockSpec((PROJ_TM, D), lambda i: (i, 0)), _resident((1, D)), _resident((D, N))],
        out_specs=[pl.BlockSpec((PROJ_TM, n), lambda i: (i, 0)) for n, _ in sections],
        out_shape=[jax.ShapeDtypeStruct((T, n), dt) for n, dt in sections],
        compiler_params=pltpu.CompilerParams(dimension_semantics=("arbitrary",), vmem_limit_bytes=VMEM_LIMIT),
        name="norm_proj",
    )(x2, gain.reshape(1, D), w)


def _outproj_kernel(o_ref, x_ref, g_ref, w_ref, y_ref):
    mix = jnp.dot(o_ref[...].astype(BF16), w_ref[...], preferred_element_type=F32)
    y_ref[...] = x_ref[...] + _rms(mix, g_ref[...])


def out_proj_residual(o2, x2, gain, w):
    T, K = o2.shape
    D = x2.shape[1]
    return pl.pallas_call(
        _outproj_kernel,
        grid=(T // PROJ_TM,),
        in_specs=[pl.BlockSpec((PROJ_TM, K), lambda i: (i, 0)), pl.BlockSpec((PROJ_TM, D), lambda i: (i, 0)),
                  _resident((1, D)), _resident((K, D))],
        out_specs=pl.BlockSpec((PROJ_TM, D), lambda i: (i, 0)),
        out_shape=jax.ShapeDtypeStruct((T, D), F32),
        compiler_params=pltpu.CompilerParams(dimension_semantics=("arbitrary",), vmem_limit_bytes=VMEM_LIMIT),
        name="out_proj_residual",
    )(o2, x2, gain.reshape(1, D), w)


FFN_TM = 256
FFN_HALO = 16
FFN_CH = 256
assert D_FF % FFN_CH == 0


def _ffn_kernel(x_ref, g_in_ref, w_in_ref, cw_ref, cb_ref, w_out_ref, g_out_ref, y_ref, hext_ref):
    t = pl.program_id(1)

    @pl.when(t == 0)
    def _():
        hext_ref[0:FFN_HALO, :] = jnp.zeros((FFN_HALO, D_MODEL), BF16)

    x = x_ref[...]
    hext_ref[FFN_HALO:, :] = _rms(x, g_in_ref[...]).astype(BF16)
    hext = hext_ref[...]

    def conv_half(c0):
        u = jnp.dot(hext, w_in_ref[:, c0:c0 + FFN_CH], preferred_element_type=F32)
        u1 = pltpu.roll(u, 1, 0)
        u2 = pltpu.roll(u, 2, 0)
        w = cw_ref[:, c0:c0 + FFN_CH]
        y = u2 * w[0:1] + u1 * w[1:2] + u * w[2:3] + cb_ref[:, c0:c0 + FFN_CH]
        return y[FFN_HALO:]

    acc = jnp.zeros((FFN_TM, D_MODEL), F32)
    for c0 in range(0, D_FF, FFN_CH):
        gate = conv_half(c0)
        val = conv_half(D_FF + c0)
        act = (gate * jax.nn.sigmoid(gate) * val).astype(BF16)
        acc = acc + jnp.dot(act, w_out_ref[c0:c0 + FFN_CH, :], preferred_element_type=F32)
    y_ref[...] = x + _rms(acc, g_out_ref[...])
    hext_ref[0:FFN_HALO, :] = hext_ref[FFN_TM:FFN_TM + FFN_HALO, :]


def conv_ffn_residual(x2, g_in, w_in, conv_w, conv_b, w_out, g_out):
    T, D = x2.shape
    nt = SEQ // FFN_TM
    tok = lambda b, t: (b * nt + t, 0)
    return pl.pallas_call(
        _ffn_kernel,
        grid=(BATCH, nt),
        in_specs=[pl.BlockSpec((FFN_TM, D), tok), _resident((1, D)), _resident((D, 2 * D_FF)),
                  _resident((FFN_CONV, 2 * D_FF)), _resident((1, 2 * D_FF)), _resident((D_FF, D)), _resident((1, D))],
        out_specs=pl.BlockSpec((FFN_TM, D), tok),
        out_shape=jax.ShapeDtypeStruct((T, D), F32),
        scratch_shapes=[pltpu.VMEM((FFN_HALO + FFN_TM, D), BF16)],
        compiler_params=pltpu.CompilerParams(dimension_semantics=("arbitrary", "arbitrary"),
                                             vmem_limit_bytes=VMEM_LIMIT),
        name="conv_ffn",
    )(x2, g_in.reshape(1, D), w_in, conv_w, conv_b.reshape(1, -1), w_out, g_out.reshape(1, D))


def split_last(x, sizes):
    idx = np.cumsum(sizes)[:-1].tolist()
    return jnp.split(x, idx, axis=-1)


def _dot_nt(a, b):
    return lax.dot_general(a, b, (((1,), (1,)), ((), ())), preferred_element_type=F32)


def _dot_tn(a, b):
    return lax.dot_general(a, b, (((0,), (0,)), ((), ())), preferred_element_type=F32)


def _split3(a):
    hi = a.astype(BF16)
    r = a - hi.astype(F32)
    mid = r.astype(BF16)
    return hi, mid, (r - mid.astype(F32)).astype(BF16)


def _dot_x3(a, b):
    a_hi, a_lo, _ = _split3(a)
    b_hi, b_lo, _ = _split3(b)
    d = lambda p, q: jnp.dot(p, q, preferred_element_type=F32)
    return d(a_hi, b_hi) + (d(a_lo, b_hi) + d(a_hi, b_lo))


def _cumsum_rows(tri, g):
    return sum(jnp.dot(tri, t, preferred_element_type=F32) for t in reversed(_split3(g)))


def _unit_lower_inverse(a):
    c = a.shape[0]
    eye = jnp.where(lax.broadcasted_iota(jnp.int32, (c, c), 0) == lax.broadcasted_iota(jnp.int32, (c, c), 1), 1.0, 0.0)
    p = -a
    t = eye + p
    k = 2
    while k < c:
        p = _dot_x3(p, p)
        t = t + _dot_x3(t, p)
        k *= 2
    return t


GDN_TC = 256
GDN_HALO = 8
GDN_SMALL_BETA = 0
GDN_SMALL_DECAY = GDN_HEADS
NSA_SMALL_GATE = 2 * GDN_HEADS


def _gdn_kernel(x_ref, gate_ref, small_ref, cw_ref, alog_ref, dtb_ref, onorm_ref, tri_ref, o_ref, xext_ref, state_ref):
    t = pl.program_id(1)
    C = GDN_CHUNK

    @pl.when(t == 0)
    def _():
        xext_ref[0:GDN_HALO, :] = jnp.zeros((GDN_HALO, xext_ref.shape[1]), F32)
        state_ref[...] = jnp.zeros(state_ref.shape, F32)

    xext_ref[GDN_HALO:, :] = x_ref[...]
    small = small_ref[...]
    beta_all = jax.nn.sigmoid(small)
    z = small + dtb_ref[...]
    softplus = jnp.maximum(z, 0.0) + jnp.log(1.0 + jnp.exp(-jnp.abs(z)))
    g_all = -jnp.exp(alog_ref[...]) * softplus
    tri = tri_ref[...]
    row = lax.broadcasted_iota(jnp.int32, (C, C), 0)
    col = lax.broadcasted_iota(jnp.int32, (C, C), 1)

    def conv_silu(c0, width):
        xe = xext_ref[:, c0:c0 + width]
        w = cw_ref[:, c0:c0 + width]
        y = xe * w[GDN_CONV - 1:GDN_CONV]
        for s in range(1, GDN_CONV):
            y = y + pltpu.roll(xe, s, 0) * w[GDN_CONV - 1 - s:GDN_CONV - s]
        y = y[GDN_HALO:]
        return y * jax.nn.sigmoid(y)

    for h in range(GDN_HEADS):
        q_all = conv_silu(h * GDN_DK, GDN_DK)
        k_all = conv_silu(GDN_QK + h * GDN_DK, GDN_DK)
        v_all = conv_silu(2 * GDN_QK + h * GDN_DV, GDN_DV)
        q_all = q_all * (lax.rsqrt(jnp.sum(q_all * q_all, axis=-1, keepdims=True) + EPS) * GDN_DK ** -0.5)
        k_all = k_all * lax.rsqrt(jnp.sum(k_all * k_all, axis=-1, keepdims=True) + EPS)
        state = state_ref[h]
        for c in range(GDN_TC // C):
            rows = slice(c * C, (c + 1) * C)
            q, k, v = q_all[rows], k_all[rows], v_all[rows]
            beta = beta_all[rows, GDN_SMALL_BETA + h:GDN_SMALL_BETA + h + 1]
            gsum = _cumsum_rows(tri, g_all[rows])
            gcol = gsum[:, GDN_SMALL_DECAY + h:GDN_SMALL_DECAY + h + 1]
            grow = jnp.transpose(gsum)[GDN_SMALL_DECAY + h:GDN_SMALL_DECAY + h + 1, :]
            glast = gcol[C - 1:C]
            decay = jnp.exp(jnp.where(row >= col, gcol - grow, -jnp.inf))
            kb = k * beta
            kbf = k.astype(BF16)
            a_kk = jnp.where(row > col, _dot_nt(kb.astype(BF16), kbf) * decay, 0.0)
            a_qk = _dot_nt(q.astype(BF16), kbf) * decay
            eg = jnp.exp(gcol)
            rhs = jnp.concatenate([v * beta, kb * eg], axis=-1)
            sol = _dot_x3(_unit_lower_inverse(a_kk), rhs)
            u, w = sol[:, :GDN_DV], sol[:, GDN_DV:]
            sb = state.astype(BF16)
            v_new = u - jnp.dot(w.astype(BF16), sb, preferred_element_type=F32)
            o = (jnp.dot((q * eg).astype(BF16), sb, preferred_element_type=F32)
                 + jnp.dot(a_qk.astype(BF16), v_new.astype(BF16), preferred_element_type=F32))
            kd = k * jnp.exp(glast - gcol)
            state = state * jnp.exp(glast) + _dot_tn(kd.astype(BF16), v_new.astype(BF16))
            gate = gate_ref[rows, h * GDN_DV:(h + 1) * GDN_DV]
            on = o * lax.rsqrt(jnp.mean(o * o, axis=-1, keepdims=True) + EPS) * onorm_ref[...]
            o_ref[rows, h * GDN_DV:(h + 1) * GDN_DV] = (on * (gate * jax.nn.sigmoid(gate))).astype(o_ref.dtype)
        state_ref[h] = state
    xext_ref[0:GDN_HALO, :] = xext_ref[GDN_TC:GDN_TC + GDN_HALO, :]


def gated_delta_net(p32, conv_w, a_log, dt_bias, o_norm, qkv_col, gate_col, small_col):
    nt = SEQ // GDN_TC
    width = 2 * GDN_QK + GDN_V
    lane_vec = lambda a, off: jnp.zeros((1, LANE), F32).at[0, off:off + GDN_HEADS].set(a.astype(F32))
    tri = jnp.asarray(np.tril(np.ones((GDN_CHUNK, GDN_CHUNK))), BF16)
    tok = lambda blk: (lambda b, t: (b * nt + t, blk))
    return pl.pallas_call(
        _gdn_kernel,
        grid=(BATCH, nt),
        in_specs=[pl.BlockSpec((GDN_TC, width), tok(qkv_col // width)),
                  pl.BlockSpec((GDN_TC, GDN_V), tok(gate_col // GDN_V)),
                  pl.BlockSpec((GDN_TC, LANE), tok(small_col // LANE)),
                  _resident((GDN_CONV, width)), _resident((1, LANE)), _resident((1, LANE)), _resident((1, GDN_DV)),
                  _resident((GDN_CHUNK, GDN_CHUNK))],
        out_specs=pl.BlockSpec((GDN_TC, GDN_V), tok(0)),
        out_shape=jax.ShapeDtypeStruct((TOKENS, GDN_V), BF16),
        scratch_shapes=[pltpu.VMEM((GDN_HALO + GDN_TC, width), F32), pltpu.VMEM((GDN_HEADS, GDN_DK, GDN_DV), F32)],
        compiler_params=pltpu.CompilerParams(dimension_semantics=("arbitrary", "arbitrary"),
                                             vmem_limit_bytes=VMEM_LIMIT),
        name="gated_delta_net",
    )(p32, p32, p32, conv_w, lane_vec(a_log, GDN_SMALL_DECAY), lane_vec(dt_bias, GDN_SMALL_DECAY),
      o_norm.reshape(1, GDN_DV), tri)


N_SLC = SEQ // SLC_LEN
N_CMP = SEQ // CMP_STRIDE
NSA_HG = NSA_HEADS // NSA_GROUPS
NSA_TQ = 128
SEL_KC = 256
WIN_KC = 128
N_WIN_CH = WINDOW // WIN_KC + 1
NSA_SCALE = NSA_DH ** -0.5


def _cmp_kernel(p_ref, pe_ref, w1_ref, b1_ref, w2_ref, o_ref):
    half = CMP_STRIDE * NSA_DH
    for kv in range(2):
        for g in range(NSA_GROUPS):
            rows = p_ref[0, kv * NSA_GROUPS + g]
            lo = (rows + pe_ref[kv, :, :half]).astype(BF16)
            hi = (rows + pe_ref[kv, :, half:]).astype(BF16)
            a = jnp.dot(lo, w1_ref[kv, :half, :], preferred_element_type=F32)
            b = jnp.dot(hi, w1_ref[kv, half:, :], preferred_element_type=F32)
            hid = a + pltpu.roll(b, N_CMP - 1, 0) + b1_ref[kv]
            act = (hid * jax.nn.sigmoid(hid)).astype(BF16)
            o_ref[0, kv * NSA_GROUPS + g] = jnp.dot(act, w2_ref[kv], preferred_element_type=F32).astype(BF16)


def nsa_compress(strips, pe, w1, b1, w2):
    n = 2 * NSA_GROUPS
    return pl.pallas_call(
        _cmp_kernel,
        grid=(BATCH,),
        in_specs=[pl.BlockSpec((1, n, N_CMP, CMP_STRIDE * NSA_DH), lambda b: (b, 0, 0, 0)),
                  _resident((2, 1, CMP_LEN * NSA_DH)), _resident((2, CMP_LEN * NSA_DH, CMP_HIDDEN)),
                  _resident((2, 1, CMP_HIDDEN)), _resident((2, CMP_HIDDEN, NSA_DH))],
        out_specs=pl.BlockSpec((1, n, N_CMP, NSA_DH), lambda b: (b, 0, 0, 0)),
        out_shape=jax.ShapeDtypeStruct((BATCH, n, N_CMP, NSA_DH), BF16),
        compiler_params=pltpu.CompilerParams(dimension_semantics=("arbitrary",), vmem_limit_bytes=VMEM_LIMIT),
        name="nsa_compress",
    )(strips, pe.reshape(2, 1, -1), w1.astype(BF16), b1.reshape(2, 1, -1), w2.astype(BF16))


def _flash_step(q, k, v, bias, carry):
    m, l, acc = carry
    kc = k.shape[0]
    s = _dot_nt(q, k).reshape(NSA_HG, NSA_TQ, kc) + bias[None]
    s = s.reshape(NSA_HG * NSA_TQ, kc)
    m_new = jnp.maximum(m, jnp.max(s, axis=-1, keepdims=True))
    alpha = jnp.exp(m - m_new)
    p = jnp.exp(s - m_new)
    l = alpha * l + jnp.sum(p, axis=-1, keepdims=True)
    acc = alpha * acc + jnp.dot(p.astype(BF16), v, preferred_element_type=F32)
    return m_new, l, acc


def _nsa_kernel(q_ref, ks_ref, vs_ref, kw_ref, vw_ref, cmp_ref, gate_ref, ovl_ref, eye_ref, expand_ref, wbias_ref,
                o_ref, sbias_ref):
    i = pl.program_id(1)
    q0 = i * NSA_TQ
    rows = NSA_HG * NSA_TQ
    gates = jax.nn.sigmoid(gate_ref[...])
    tok = q0 + lax.broadcasted_iota(jnp.int32, (NSA_TQ, 1), 0)
    tok_l = q0 + lax.broadcasted_iota(jnp.int32, (1, NSA_TQ), 1)
    init = (jnp.full((rows, 1), NEG, F32), jnp.zeros((rows, 1), F32), jnp.zeros((rows, NSA_DH), F32))

    for g in range(NSA_GROUPS):
        lanes = slice(g * NSA_DH, (g + 1) * NSA_DH)
        heads = [g * NSA_HG + h for h in range(NSA_HG)]
        q = jnp.concatenate([q_ref[:, h * NSA_DH:(h + 1) * NSA_DH] for h in heads], axis=0)
        q = q * jnp.asarray(NSA_SCALE, BF16)

        s = _dot_nt(q, cmp_ref[0, g]).reshape(NSA_HG, NSA_TQ, N_CMP)
        cmp_end = lax.broadcasted_iota(jnp.int32, (1, N_CMP), 1) * CMP_STRIDE + (CMP_LEN - 1)
        cvalid = (cmp_end <= tok)[None]
        sm = jnp.where(cvalid, s, NEG)
        e = jnp.exp(sm - jnp.max(sm, axis=-1, keepdims=True))
        p = jnp.where(cvalid, e / jnp.sum(e, axis=-1, keepdims=True), 0.0)
        o_cmp = jnp.dot(p.reshape(rows, N_CMP).astype(BF16), cmp_ref[0, NSA_GROUPS + g], preferred_element_type=F32)

        imp = _dot_nt(ovl_ref[...], jnp.sum(p, axis=0).astype(BF16))
        blk = lax.broadcasted_iota(jnp.int32, (N_SLC, NSA_TQ), 0)
        cur = tok_l // SLC_LEN
        bvalid = blk <= cur
        forced = (blk == 0) | (blk == cur) | (blk == cur - 1)
        score = jnp.where(bvalid, imp + jnp.where(forced, FORCE, 0.0), NEG)
        rank = jnp.zeros((N_SLC, NSA_TQ), F32)
        for j in range(N_SLC):
            sj = score[j:j + 1, :]
            tie = jnp.where((sj == score) & (blk > j), 1.0, 0.0)
            rank = rank + jnp.where(sj > score, 1.0, tie)
        sel_t = jnp.where((rank < SLC_TOPK) & bvalid, 1.0, 0.0).astype(BF16)
        sel = _dot_nt(eye_ref[...], sel_t).astype(BF16)
        hit = jnp.dot(sel, expand_ref[...], preferred_element_type=F32)
        kpos = lax.broadcasted_iota(jnp.int32, (1, SEQ), 1)
        sbias = jnp.where((hit > 0.5) & (kpos <= tok), 0.0, NEG)
        for c in range(SEQ // SEL_KC):
            sbias_ref[c] = sbias[:, c * SEL_KC:(c + 1) * SEL_KC]

        def sel_body(c, carry):
            r = pl.ds(pl.multiple_of(c * SEL_KC, SEL_KC), SEL_KC)
            return _flash_step(q, ks_ref[r, lanes], vs_ref[r, lanes], sbias_ref[c], carry)

        _, l_s, acc_s = lax.fori_loop(0, (q0 + NSA_TQ + SEL_KC - 1) // SEL_KC, sel_body, init)

        def win_body(c, carry):
            r = pl.ds(pl.multiple_of(q0 - WINDOW + c * WIN_KC, WIN_KC), WIN_KC)
            return _flash_step(q, kw_ref[r, lanes], vw_ref[r, lanes], wbias_ref[c], carry)

        _, l_w, acc_w = lax.fori_loop(jnp.maximum(0, N_WIN_CH - 1 - i), N_WIN_CH, win_body, init)

        def gate(branch):
            lane = NSA_SMALL_GATE + branch * NSA_HEADS
            return jnp.concatenate([gates[:, lane + h:lane + h + 1] for h in heads], axis=0)

        o = gate(0) * o_cmp + gate(1) * (acc_s / l_s) + gate(2) * (acc_w / l_w)
        for n, h in enumerate(heads):
            o_ref[:, h * NSA_DH:(h + 1) * NSA_DH] = o[n * NSA_TQ:(n + 1) * NSA_TQ].astype(o_ref.dtype)


def _nsa_constants():
    c0 = np.arange(N_CMP)[:, None] * CMP_STRIDE
    s0 = np.arange(N_SLC)[None, :] * SLC_LEN
    overlap = np.clip(np.minimum(c0 + CMP_LEN, s0 + SLC_LEN) - np.maximum(c0, s0), 0, None) / CMP_STRIDE
    overlap[N_CMP - 1] = 0.0
    expand = (np.arange(SEQ)[None, :] // SLC_LEN == np.arange(N_SLC)[:, None]).astype(np.float32)
    a = np.arange(NSA_TQ)[:, None]
    r = np.arange(N_WIN_CH * WIN_KC)[None, :]
    wbias = np.where((r > a) & (r <= a + WINDOW), 0.0, NEG).astype(np.float32)
    wbias = wbias.reshape(NSA_TQ, N_WIN_CH, WIN_KC).transpose(1, 0, 2)
    return (jnp.asarray(overlap.T, BF16), jnp.asarray(np.eye(NSA_TQ), BF16), jnp.asarray(expand, BF16),
            jnp.asarray(wbias, F32))


def nsa_attention(p16, cmp_kv, p32, q_col, kv_col, small_col):
    nq = SEQ // NSA_TQ
    ovl_t, eye, expand, wbias = _nsa_constants()
    kv_spec = lambda n: pl.BlockSpec((SEQ, NSA_KV), lambda b, i: (b, kv_col // NSA_KV + n))
    return pl.pallas_call(
        _nsa_kernel,
        grid=(BATCH, nq),
        in_specs=[pl.BlockSpec((NSA_TQ, NSA_Q), lambda b, i: (b * nq + i, q_col // NSA_Q)),
                  kv_spec(0), kv_spec(1), kv_spec(2), kv_spec(3),
                  pl.BlockSpec((1, 2 * NSA_GROUPS, N_CMP, NSA_DH), lambda b, i: (b, 0, 0, 0)),
                  pl.BlockSpec((NSA_TQ, LANE), lambda b, i: (b * nq + i, small_col // LANE)),
                  _resident(ovl_t.shape), _resident(eye.shape), _resident(expand.shape), _resident(wbias.shape)],
        out_specs=pl.BlockSpec((NSA_TQ, NSA_Q), lambda b, i: (b * nq + i, 0)),
        out_shape=jax.ShapeDtypeStruct((TOKENS, NSA_Q), BF16),
        scratch_shapes=[pltpu.VMEM((SEQ // SEL_KC, NSA_TQ, SEL_KC), F32)],
        compiler_params=pltpu.CompilerParams(dimension_semantics=("arbitrary", "arbitrary"),
                                             vmem_limit_bytes=VMEM_LIMIT),
        name="nsa_attention",
    )(p16, p16, p16, p16, p16, cmp_kv, p32, ovl_t, eye, expand, wbias)


EVEN_QKV_COL = 0
EVEN_GATE_COL = 2 * GDN_QK + GDN_V
EVEN_CMP_COL = EVEN_GATE_COL + GDN_V
EVEN_SMALL_COL = EVEN_CMP_COL + 2 * NSA_KV
EVEN_F32_W = EVEN_SMALL_COL + LANE
EVEN_Q_COL = 0
EVEN_KV_COL = NSA_Q
EVEN_BF16_W = NSA_Q + 4 * NSA_KV
assert NSA_SMALL_GATE + 3 * NSA_HEADS <= LANE


def even_proj_weight(w_in):
    qkv_a, beta, dec, gate_a, q_b, kv_b, gate_b = split_last(w_in, EVEN_SPLITS)
    pad = jnp.zeros((w_in.shape[0], LANE - 2 * GDN_HEADS - 3 * NSA_HEADS), w_in.dtype)
    return jnp.concatenate([qkv_a, gate_a, kv_b[:, :2 * NSA_KV], beta, dec, gate_b, pad, q_b, kv_b[:, 2 * NSA_KV:]],
                           axis=1).astype(BF16)


def even_mixer_core(p32, p16, conv_w, A_log, dt_bias, o_norm, cmp_pe, cmp_w1, cmp_b1, cmp_w2):
    oa = gated_delta_net(p32, conv_w, A_log, dt_bias, o_norm, EVEN_QKV_COL, EVEN_GATE_COL, EVEN_SMALL_COL)
    strips = p32[:, EVEN_CMP_COL:EVEN_CMP_COL + 2 * NSA_KV].reshape(BATCH, N_CMP, CMP_STRIDE, 2 * NSA_GROUPS, NSA_DH)
    strips = strips.transpose(0, 3, 1, 2, 4).reshape(BATCH, 2 * NSA_GROUPS, N_CMP, CMP_STRIDE * NSA_DH)
    cmp_kv = nsa_compress(strips, cmp_pe, cmp_w1, cmp_b1, cmp_w2)
    ob = nsa_attention(p16, cmp_kv, p32, EVEN_Q_COL, EVEN_KV_COL, EVEN_SMALL_COL)
    return jnp.concatenate([oa, ob], axis=-1)


GLA_TC = 256
ODD_Q_COL = 0
ODD_K_COL = GLA_QK
ODD_V_COL = 2 * GLA_QK
ODD_R_COL = ODD_V_COL + GLA_V
ODD_LR_COL = ODD_R_COL + GLA_V
ODD_F32_W = ODD_LR_COL + LANE


def _gla_kernel(q_ref, k_ref, v_ref, r_ref, lr_ref, aw_ref, ab_ref, onorm_ref, tri_ref, o_ref, state_ref):
    t = pl.program_id(1)
    C = GLA_CHUNK

    @pl.when(t == 0)
    def _():
        state_ref[...] = jnp.zeros(state_ref.shape, F32)

    z = jnp.dot(lr_ref[...].astype(BF16), aw_ref[...], preferred_element_type=F32) + ab_ref[...]
    log_a = (jnp.minimum(z, 0.0) - jnp.log(1.0 + jnp.exp(-jnp.abs(z)))) * (1.0 / GLA_TAU)
    tri = tri_ref[...]
    causal = lax.broadcasted_iota(jnp.int32, (C, C), 0) >= lax.broadcasted_iota(jnp.int32, (C, C), 1)

    for h in range(GLA_HEADS):
        kl = slice(h * GLA_DK, (h + 1) * GLA_DK)
        vl = slice(h * GLA_DV, (h + 1) * GLA_DV)
        state_t = state_ref[h]
        for c in range(GLA_TC // C):
            rows = slice(c * C, (c + 1) * C)
            b = _cumsum_rows(tri, log_a[rows, kl])
            b_last = b[C - 1:C]
            q = q_ref[rows, kl] * GLA_DK ** -0.5
            k = k_ref[rows, kl]
            v = v_ref[rows, vl].astype(BF16)
            qt = (q * jnp.exp(b)).astype(BF16)
            kt = (k * jnp.exp(-b)).astype(BF16)
            kh = (k * jnp.exp(b_last - b)).astype(BF16)
            intra = jnp.where(causal, _dot_nt(qt, kt), 0.0).astype(BF16)
            o = jnp.dot(intra, v, preferred_element_type=F32) + _dot_nt(qt, state_t.astype(BF16))
            state_t = state_t * jnp.exp(b_last) + _dot_tn(v, kh)
            r = r_ref[rows, vl]
            on = o * lax.rsqrt(jnp.mean(o * o, axis=-1, keepdims=True) + EPS) * onorm_ref[...]
            o_ref[rows, vl] = (on * (r * jax.nn.sigmoid(r))).astype(o_ref.dtype)
        state_ref[h] = state_t


def gated_linear_attention(p32, a_w2, a_b, o_norm):
    nt = SEQ // GLA_TC
    tok = lambda blk: (lambda b, t: (b * nt + t, blk))
    aw = jnp.pad(a_w2, ((0, LANE - GLA_RANK), (0, 0))).astype(BF16)
    tri = jnp.asarray(np.tril(np.ones((GLA_CHUNK, GLA_CHUNK))), BF16)
    return pl.pallas_call(
        _gla_kernel,
        grid=(BATCH, nt),
        in_specs=[pl.BlockSpec((GLA_TC, GLA_QK), tok(ODD_Q_COL // GLA_QK)),
                  pl.BlockSpec((GLA_TC, GLA_QK), tok(ODD_K_COL // GLA_QK)),
                  pl.BlockSpec((GLA_TC, GLA_V), tok(ODD_V_COL // GLA_V)),
                  pl.BlockSpec((GLA_TC, GLA_V), tok(ODD_R_COL // GLA_V)),
                  pl.BlockSpec((GLA_TC, LANE), tok(ODD_LR_COL // LANE)),
                  _resident((LANE, GLA_QK)), _resident((1, GLA_QK)), _resident((1, GLA_DV)),
                  _resident((GLA_CHUNK, GLA_CHUNK))],
        out_specs=pl.BlockSpec((GLA_TC, GLA_V), tok(0)),
        out_shape=jax.ShapeDtypeStruct((TOKENS, GLA_V), BF16),
        scratch_shapes=[pltpu.VMEM((GLA_HEADS, GLA_DV, GLA_DK), F32)],
        compiler_params=pltpu.CompilerParams(dimension_semantics=("arbitrary", "arbitrary"),
                                             vmem_limit_bytes=VMEM_LIMIT),
        name="gated_linear_attention",
    )(p32, p32, p32, p32, p32, aw, a_b.reshape(1, GLA_QK), o_norm.reshape(1, GLA_DV), tri)


def _pad_cols(w, n):
    return jnp.pad(w, ((0, 0), (0, n - w.shape[1])))


def kernel(x, norm_gains, ffn_w_in, ffn_conv_w, ffn_conv_b, ffn_w_out, hyb_w_in, hyb_w_out, gdn_conv_w, gdn_A_log,
           gdn_dt_bias, gdn_o_norm, nsa_cmp_pe, nsa_cmp_w1, nsa_cmp_b1, nsa_cmp_w2, gla_w_in, gla_a_w2, gla_a_b,
           gla_o_norm, gla_w_out):
    x2 = x.reshape(TOKENS, D_MODEL)
    for layer in range(DEPTH):
        if layer % 2 == 0:
            e = layer // 2
            p32, p16 = norm_proj(x2, norm_gains[layer, 0], even_proj_weight(hyb_w_in[e]),
                                 ((EVEN_F32_W, F32), (EVEN_BF16_W, BF16)))
            o = even_mixer_core(p32, p16, gdn_conv_w[e], gdn_A_log[e], gdn_dt_bias[e], gdn_o_norm[e], nsa_cmp_pe[e],
                                nsa_cmp_w1[e], nsa_cmp_b1[e], nsa_cmp_w2[e])
            w_out = hyb_w_out[e]
        else:
            o_ = layer // 2
            w = _pad_cols(gla_w_in[o_], ODD_F32_W).astype(BF16)
            (p32,) = norm_proj(x2, norm_gains[layer, 0], w, ((ODD_F32_W, F32),))
            o = gated_linear_attention(p32, gla_a_w2[o_], gla_a_b[o_], gla_o_norm[o_])
            w_out = gla_w_out[o_]
        x2 = out_proj_residual(o.reshape(TOKENS, -1), x2, norm_gains[layer, 1], w_out.astype(BF16))
        x2 = conv_ffn_residual(x2, norm_gains[layer, 2], ffn_w_in[layer].astype(BF16), ffn_conv_w[layer],
                               ffn_conv_b[layer], ffn_w_out[layer].astype(BF16), norm_gains[layer, 3])
    return x2.reshape(BATCH, SEQ, D_MODEL)
```

```python
import functools
import math

import jax
import jax.numpy as jnp
import numpy as np
from jax import lax
from jax.experimental import pallas as pl
from jax.experimental.pallas import tpu as pltpu

D_MODEL = 1024
BATCH = 8
SEQ = 2048
DEPTH = 4
TOKENS = BATCH * SEQ

EPS = 1e-6
NEG = -1e30
FORCE = 1e4

GDN_DK = 128
GDN_DV = 128
GDN_HEADS = 4
GDN_QK = GDN_HEADS * GDN_DK
GDN_V = GDN_HEADS * GDN_DV
GDN_CONV = 4
GDN_CHUNK = 64

NSA_DH = 64
NSA_HEADS = 8
NSA_GROUPS = 2
NSA_Q = NSA_HEADS * NSA_DH
NSA_KV = NSA_GROUPS * NSA_DH
CMP_LEN = 32
CMP_STRIDE = 16
CMP_HIDDEN = 2 * NSA_DH
SLC_LEN = 64
SLC_TOPK = 8
WINDOW = 512
Q_BLOCK = 128

GLA_HEADS = 4
GLA_DK = 128
GLA_DV = 256
GLA_QK = GLA_HEADS * GLA_DK
GLA_V = GLA_HEADS * GLA_DV
GLA_RANK = 16
GLA_TAU = 16.0
GLA_CHUNK = 64

D_FF = 2816
FFN_CONV = 3

EVEN_SPLITS = (2 * GDN_QK + GDN_V, GDN_HEADS, GDN_HEADS, GDN_V, NSA_Q, 6 * NSA_KV, 3 * NSA_HEADS)
ODD_SPLITS = (GLA_QK, GLA_QK, GLA_V, GLA_V, GLA_RANK)

LANE = 128
VMEM_LIMIT = 56 * 1024 * 1024

F32 = jnp.float32
BF16 = jnp.bfloat16


def _round_up(n, m):
    return (n + m - 1) // m * m


def _resident(shape):
    nd = len(shape)
    return pl.BlockSpec(shape, lambda *_: (0,) * nd, pipeline_mode=pl.Buffered(1))


def _rms(x, g):
    ms = jnp.mean(x * x, axis=-1, keepdims=True)
    return x * lax.rsqrt(ms + EPS) * g


PROJ_TM = 256
PROJ_CH = 512


def _proj_kernel(x_ref, g_ref, w_ref, *o_refs):
    h = _rms(x_ref[...], g_ref[...]).astype(BF16)
    col = 0
    for o_ref in o_refs:
        n = o_ref.shape[1]
        for c0 in range(0, n, PROJ_CH):
            cw = min(PROJ_CH, n - c0)
            y = jnp.dot(h, w_ref[:, col + c0:col + c0 + cw], preferred_element_type=F32)
            o_ref[:, c0:c0 + cw] = y.astype(o_ref.dtype)
        col += n


def norm_proj(x2, gain, w, sections):
    T, D = x2.shape
    N = w.shape[1]
    assert N == sum(n for n, _ in sections) and all(n % LANE == 0 for n, _ in sections)
    return pl.pallas_call(
        _proj_kernel,
        grid=(T // PROJ_TM,),
        in_specs=[pl.BlockSpec((PROJ_TM, D), lambda i: (i, 0)), _resident((1, D)), _resident((D, N))],
        out_specs=[pl.BlockSpec((PROJ_TM, n), lambda i: (i, 0)) for n, _ in sections],
        out_shape=[jax.ShapeDtypeStruct((T, n), dt) for n, dt in sections],
        compiler_params=pltpu.CompilerParams(dimension_semantics=("arbitrary",), vmem_limit_bytes=VMEM_LIMIT),
        name="norm_proj",
    )(x2, gain.reshape(1, D), w)


def _outproj_kernel(o_ref, x_ref, g_ref, w_ref, y_ref):
    mix = jnp.dot(o_ref[...].astype(BF16), w_ref[...], preferred_element_type=F32)
    y_ref[...] = x_ref[...] + _rms(mix, g_ref[...])


def out_proj_residual(o2, x2, gain, w):
    T, K = o2.shape
    D = x2.shape[1]
    return pl.pallas_call(
        _outproj_kernel,
        grid=(T // PROJ_TM,),
        in_specs=[pl.BlockSpec((PROJ_TM, K), lambda i: (i, 0)), pl.BlockSpec((PROJ_TM, D), lambda i: (i, 0)),
                  _resident((1, D)), _resident((K, D))],
        out_specs=pl.BlockSpec((PROJ_TM, D), lambda i: (i, 0)),
        out_shape=jax.ShapeDtypeStruct((T, D), F32),
        compiler_params=pltpu.CompilerParams(dimension_semantics=("arbitrary",), vmem_limit_bytes=VMEM_LIMIT),
        name="out_proj_residual",
    )(o2, x2, gain.reshape(1, D), w)


FFN_TM = 256
FFN_HALO = 16
FFN_CH = 256
assert D_FF % FFN_CH == 0


def _ffn_kernel(x_ref, g_in_ref, w_in_ref, cw_ref, cb_ref, w_out_ref, g_out_ref, y_ref, hext_ref):
    t = pl.program_id(1)

    @pl.when(t == 0)
    def _():
        hext_ref[0:FFN_HALO, :] = jnp.zeros((FFN_HALO, D_MODEL), BF16)

    x = x_ref[...]
    hext_ref[FFN_HALO:, :] = _rms(x, g_in_ref[...]).astype(BF16)
    hext = hext_ref[...]

    def conv_half(c0):
        u = jnp.dot(hext, w_in_ref[:, c0:c0 + FFN_CH], preferred_element_type=F32)
        u1 = pltpu.roll(u, 1, 0)
        u2 = pltpu.roll(u, 2, 0)
        w = cw_ref[:, c0:c0 + FFN_CH]
        y = u2 * w[0:1] + u1 * w[1:2] + u * w[2:3] + cb_ref[:, c0:c0 + FFN_CH]
        return y[FFN_HALO:]

    acc = jnp.zeros((FFN_TM, D_MODEL), F32)
    for c0 in range(0, D_FF, FFN_CH):
        gate = conv_half(c0)
        val = conv_half(D_FF + c0)
        act = (gate * jax.nn.sigmoid(gate) * val).astype(BF16)
        acc = acc + jnp.dot(act, w_out_ref[c0:c0 + FFN_CH, :], preferred_element_type=F32)
    y_ref[...] = x + _rms(acc, g_out_ref[...])
    hext_ref[0:FFN_HALO, :] = hext_ref[FFN_TM:FFN_TM + FFN_HALO, :]


def conv_ffn_residual(x2, g_in, w_in, conv_w, conv_b, w_out, g_out):
    T, D = x2.shape
    nt = SEQ // FFN_TM
    tok = lambda b, t: (b * nt + t, 0)
    return pl.pallas_call(
        _ffn_kernel,
        grid=(BATCH, nt),
        in_specs=[pl.BlockSpec((FFN_TM, D), tok), _resident((1, D)), _resident((D, 2 * D_FF)),
                  _resident((FFN_CONV, 2 * D_FF)), _resident((1, 2 * D_FF)), _resident((D_FF, D)), _resident((1, D))],
        out_specs=pl.BlockSpec((FFN_TM, D), tok),
        out_shape=jax.ShapeDtypeStruct((T, D), F32),
        scratch_shapes=[pltpu.VMEM((FFN_HALO + FFN_TM, D), BF16)],
        compiler_params=pltpu.CompilerParams(dimension_semantics=("arbitrary", "arbitrary"),
                                             vmem_limit_bytes=VMEM_LIMIT),
        name="conv_ffn",
    )(x2, g_in.reshape(1, D), w_in, conv_w, conv_b.reshape(1, -1), w_out, g_out.reshape(1, D))


def split_last(x, sizes):
    idx = np.cumsum(sizes)[:-1].tolist()
    return jnp.split(x, idx, axis=-1)


def _dot_nt(a, b):
    return lax.dot_general(a, b, (((1,), (1,)), ((), ())), preferred_element_type=F32)


def _dot_tn(a, b):
    return lax.dot_general(a, b, (((0,), (0,)), ((), ())), preferred_element_type=F32)


def _split3(a):
    hi = a.astype(BF16)
    r = a - hi.astype(F32)
    mid = r.astype(BF16)
    return hi, mid, (r - mid.astype(F32)).astype(BF16)


def _dot_x3(a, b):
    a_hi, a_lo, _ = _split3(a)
    b_hi, b_lo, _ = _split3(b)
    d = lambda p, q: jnp.dot(p, q, preferred_element_type=F32)
    return d(a_hi, b_hi) + (d(a_lo, b_hi) + d(a_hi, b_lo))


def _cumsum_rows(tri, g):
    return sum(jnp.dot(tri, t, preferred_element_type=F32) for t in reversed(_split3(g)))


def _unit_lower_inverse(a):
    c = a.shape[0]
    eye = jnp.where(lax.broadcasted_iota(jnp.int32, (c, c), 0) == lax.broadcasted_iota(jnp.int32, (c, c), 1), 1.0, 0.0)
    p = -a
    t = eye + p
    k = 2
    while k < c:
        p = _dot_x3(p, p)
        t = t + _dot_x3(t, p)
        k *= 2
    return t


GDN_TC = 256
GDN_HALO = 8
GDN_SMALL_BETA = 0
GDN_SMALL_DECAY = GDN_HEADS
NSA_SMALL_GATE = 2 * GDN_HEADS


def _gdn_kernel(x_ref, gate_ref, small_ref, cw_ref, alog_ref, dtb_ref, onorm_ref, tri_ref, o_ref, xext_ref, state_ref):
    t = pl.program_id(1)
    C = GDN_CHUNK

    @pl.when(t == 0)
    def _():
        xext_ref[0:GDN_HALO, :] = jnp.zeros((GDN_HALO, xext_ref.shape[1]), F32)
        state_ref[...] = jnp.zeros(state_ref.shape, F32)

    xext_ref[GDN_HALO:, :] = x_ref[...]
    small = small_ref[...]
    beta_all = jax.nn.sigmoid(small)
    z = small + dtb_ref[...]
    softplus = jnp.maximum(z, 0.0) + jnp.log(1.0 + jnp.exp(-jnp.abs(z)))
    g_all = -jnp.exp(alog_ref[...]) * softplus
    tri = tri_ref[...]
    row = lax.broadcasted_iota(jnp.int32, (C, C), 0)
    col = lax.broadcasted_iota(jnp.int32, (C, C), 1)

    def conv_silu(c0, width):
        xe = xext_ref[:, c0:c0 + width]
        w = cw_ref[:, c0:c0 + width]
        y = xe * w[GDN_CONV - 1:GDN_CONV]
        for s in range(1, GDN_CONV):
            y = y + pltpu.roll(xe, s, 0) * w[GDN_CONV - 1 - s:GDN_CONV - s]
        y = y[GDN_HALO:]
        return y * jax.nn.sigmoid(y)

    for h in range(GDN_HEADS):
        q_all = conv_silu(h * GDN_DK, GDN_DK)
        k_all = conv_silu(GDN_QK + h * GDN_DK, GDN_DK)
        v_all = conv_silu(2 * GDN_QK + h * GDN_DV, GDN_DV)
        q_all = q_all * (lax.rsqrt(jnp.sum(q_all * q_all, axis=-1, keepdims=True) + EPS) * GDN_DK ** -0.5)
        k_all = k_all * lax.rsqrt(jnp.sum(k_all * k_all, axis=-1, keepdims=True) + EPS)
        state = state_ref[h]
        for c in range(GDN_TC // C):
            rows = slice(c * C, (c + 1) * C)
            q, k, v = q_all[rows], k_all[rows], v_all[rows]
            beta = beta_all[rows, GDN_SMALL_BETA + h:GDN_SMALL_BETA + h + 1]
            gsum = _cumsum_rows(tri, g_all[rows])
            gcol = gsum[:, GDN_SMALL_DECAY + h:GDN_SMALL_DECAY + h + 1]
            grow = jnp.transpose(gsum)[GDN_SMALL_DECAY + h:GDN_SMALL_DECAY + h + 1, :]
            glast = gcol[C - 1:C]
            decay = jnp.exp(jnp.where(row >= col, gcol - grow, -jnp.inf))
            kb = k * beta
            kbf = k.astype(BF16)
            a_kk = jnp.where(row > col, _dot_nt(kb.astype(BF16), kbf) * decay, 0.0)
            a_qk = _dot_nt(q.astype(BF16), kbf) * decay
            eg = jnp.exp(gcol)
            rhs = jnp.concatenate([v * beta, kb * eg], axis=-1)
            sol = _dot_x3(_unit_lower_inverse(a_kk), rhs)
            u, w = sol[:, :GDN_DV], sol[:, GDN_DV:]
            sb = state.astype(BF16)
            v_new = u - jnp.dot(w.astype(BF16), sb, preferred_element_type=F32)
            o = (jnp.dot((q * eg).astype(BF16), sb, preferred_element_type=F32)
                 + jnp.dot(a_qk.astype(BF16), v_new.astype(BF16), preferred_element_type=F32))
            kd = k * jnp.exp(glast - gcol)
            state = state * jnp.exp(glast) + _dot_tn(kd.astype(BF16), v_new.astype(BF16))
            gate = gate_ref[rows, h * GDN_DV:(h + 1) * GDN_DV]
            on = o * lax.rsqrt(jnp.mean(o * o, axis=-1, keepdims=True) + EPS) * onorm_ref[...]
            o_ref[rows, h * GDN_DV:(h + 1) * GDN_DV] = (on * (gate * jax.nn.sigmoid(gate))).astype(o_ref.dtype)
        state_ref[h] = state
    xext_ref[0:GDN_HALO, :] = xext_ref[GDN_TC:GDN_TC + GDN_HALO, :]


def gated_delta_net(p32, conv_w, a_log, dt_bias, o_norm, qkv_col, gate_col, small_col):
    nt = SEQ // GDN_TC
    width = 2 * GDN_QK + GDN_V
    lane_vec = lambda a, off: jnp.zeros((1, LANE), F32).at[0, off:off + GDN_HEADS].set(a.astype(F32))
    tri = jnp.asarray(np.tril(np.ones((GDN_CHUNK, GDN_CHUNK))), BF16)
    tok = lambda blk: (lambda b, t: (b * nt + t, blk))
    return pl.pallas_call(
        _gdn_kernel,
        grid=(BATCH, nt),
        in_specs=[pl.BlockSpec((GDN_TC, width), tok(qkv_col // width)),
                  pl.BlockSpec((GDN_TC, GDN_V), tok(gate_col // GDN_V)),
                  pl.BlockSpec((GDN_TC, LANE), tok(small_col // LANE)),
                  _resident((GDN_CONV, width)), _resident((1, LANE)), _resident((1, LANE)), _resident((1, GDN_DV)),
                  _resident((GDN_CHUNK, GDN_CHUNK))],
        out_specs=pl.BlockSpec((GDN_TC, GDN_V), tok(0)),
        out_shape=jax.ShapeDtypeStruct((TOKENS, GDN_V), BF16),
        scratch_shapes=[pltpu.VMEM((GDN_HALO + GDN_TC, width), F32), pltpu.VMEM((GDN_HEADS, GDN_DK, GDN_DV), F32)],
        compiler_params=pltpu.CompilerParams(dimension_semantics=("arbitrary", "arbitrary"),
                                             vmem_limit_bytes=VMEM_LIMIT),
        name="gated_delta_net",
    )(p32, p32, p32, conv_w, lane_vec(a_log, GDN_SMALL_DECAY), lane_vec(dt_bias, GDN_SMALL_DECAY),
      o_norm.reshape(1, GDN_DV), tri)


N_SLC = SEQ // SLC_LEN
N_CMP = SEQ // CMP_STRIDE
NSA_HG = NSA_HEADS // NSA_GROUPS
NSA_TQ = 128
SEL_STEP = 512
WIN_W = WINDOW + NSA_TQ
NSA_SCALE = NSA_DH ** -0.5


def _cmp_kernel(p_ref, pe_ref, w1_ref, b1_ref, w2_ref, o_ref):
    half = CMP_STRIDE * NSA_DH
    for kv in range(2):
        for g in range(NSA_GROUPS):
            rows = p_ref[0, kv * NSA_GROUPS + g]
            lo = (rows + pe_ref[kv, :, :half]).astype(BF16)
            hi = (rows + pe_ref[kv, :, half:]).astype(BF16)
            a = jnp.dot(lo, w1_ref[kv, :half, :], preferred_element_type=F32)
            b = jnp.dot(hi, w1_ref[kv, half:, :], preferred_element_type=F32)
            hid = a + pltpu.roll(b, N_CMP - 1, 0) + b1_ref[kv]
            act = (hid * jax.nn.sigmoid(hid)).astype(BF16)
            o_ref[0, kv * NSA_GROUPS + g] = jnp.dot(act, w2_ref[kv], preferred_element_type=F32).astype(BF16)


def nsa_compress(strips, pe, w1, b1, w2):
    n = 2 * NSA_GROUPS
    return pl.pallas_call(
        _cmp_kernel,
        grid=(BATCH,),
        in_specs=[pl.BlockSpec((1, n, N_CMP, CMP_STRIDE * NSA_DH), lambda b: (b, 0, 0, 0)),
                  _resident((2, 1, CMP_LEN * NSA_DH)), _resident((2, CMP_LEN * NSA_DH, CMP_HIDDEN)),
                  _resident((2, 1, CMP_HIDDEN)), _resident((2, CMP_HIDDEN, NSA_DH))],
        out_specs=pl.BlockSpec((1, n, N_CMP, NSA_DH), lambda b: (b, 0, 0, 0)),
        out_shape=jax.ShapeDtypeStruct((BATCH, n, N_CMP, NSA_DH), BF16),
        compiler_params=pltpu.CompilerParams(dimension_semantics=("arbitrary",), vmem_limit_bytes=VMEM_LIMIT),
        name="nsa_compress",
    )(strips, pe.reshape(2, 1, -1), w1.astype(BF16), b1.reshape(2, 1, -1), w2.astype(BF16))


def _masked_attention(q, k, v, bias):
    w = k.shape[0]
    s = _dot_nt(q, k).reshape(NSA_HG, NSA_TQ, w) + bias[None]
    s = s.reshape(NSA_HG * NSA_TQ, w)
    p = jnp.exp(s - jnp.max(s, axis=-1, keepdims=True))
    l = jnp.sum(p, axis=-1, keepdims=True)
    return jnp.dot(p.astype(BF16), v, preferred_element_type=F32) / l


def _nsa_kernel(q_ref, ks_ref, vs_ref, kw_ref, vw_ref, cmp_ref, gate_ref, ovl_ref, eye_ref, expand_ref,
                o_ref, sbias_ref, osel_ref):
    i = pl.program_id(1)
    q0 = i * NSA_TQ
    rows = NSA_HG * NSA_TQ
    gates = jax.nn.sigmoid(gate_ref[...])
    tok = q0 + lax.broadcasted_iota(jnp.int32, (NSA_TQ, 1), 0)
    tok_l = q0 + lax.broadcasted_iota(jnp.int32, (1, NSA_TQ), 1)
    w0 = pl.multiple_of(jnp.maximum(q0 + NSA_TQ - WIN_W, 0), NSA_TQ)
    wpos = w0 + lax.broadcasted_iota(jnp.int32, (1, WIN_W), 1)
    wbias = jnp.where((wpos <= tok) & (wpos > tok - WINDOW), 0.0, NEG)

    for g in range(NSA_GROUPS):
        lanes = slice(g * NSA_DH, (g + 1) * NSA_DH)
        heads = [g * NSA_HG + h for h in range(NSA_HG)]
        q = jnp.concatenate([q_ref[:, h * NSA_DH:(h + 1) * NSA_DH] for h in heads], axis=0)
        q = q * jnp.asarray(NSA_SCALE, BF16)

        s = _dot_nt(q, cmp_ref[0, g]).reshape(NSA_HG, NSA_TQ, N_CMP)
        cmp_end = lax.broadcasted_iota(jnp.int32, (1, N_CMP), 1) * CMP_STRIDE + (CMP_LEN - 1)
        cvalid = (cmp_end <= tok)[None]
        sm = jnp.where(cvalid, s, NEG)
        e = jnp.exp(sm - jnp.max(sm, axis=-1, keepdims=True))
        p = jnp.where(cvalid, e / jnp.sum(e, axis=-1, keepdims=True), 0.0)
        o_cmp = jnp.dot(p.reshape(rows, N_CMP).astype(BF16), cmp_ref[0, NSA_GROUPS + g], preferred_element_type=F32)

        imp = _dot_nt(ovl_ref[...], jnp.sum(p, axis=0).astype(BF16))
        blk = lax.broadcasted_iota(jnp.int32, (N_SLC, NSA_TQ), 0)
        cur = tok_l // SLC_LEN
        bvalid = blk <= cur
        forced = (blk == 0) | (blk == cur) | (blk == cur - 1)
        score = jnp.where(bvalid, imp + jnp.where(forced, FORCE, 0.0), NEG)
        rank = jnp.zeros((N_SLC, NSA_TQ), F32)
        for j in range(N_SLC):
            sj = score[j:j + 1, :]
            tie = jnp.where((sj == score) & (blk > j), 1.0, 0.0)
            rank = rank + jnp.where(sj > score, 1.0, tie)
        sel_t = jnp.where((rank < SLC_TOPK) & bvalid, 1.0, 0.0).astype(BF16)
        sel = _dot_nt(eye_ref[...], sel_t).astype(BF16)
        hit = jnp.dot(sel, expand_ref[...], preferred_element_type=F32)
        kpos = lax.broadcasted_iota(jnp.int32, (1, SEQ), 1)
        sbias_ref[...] = jnp.where((hit > 0.5) & (kpos <= tok), 0.0, NEG)

        for n in range(SEQ // SEL_STEP):
            @pl.when(i // (SEL_STEP // NSA_TQ) == n)
            def _():
                w = (n + 1) * SEL_STEP
                osel_ref[...] = _masked_attention(q, ks_ref[0:w, lanes], vs_ref[0:w, lanes], sbias_ref[:, 0:w])

        r = pl.ds(w0, WIN_W)
        o_win = _masked_attention(q, kw_ref[r, lanes], vw_ref[r, lanes], wbias)

        def gate(branch):
            lane = NSA_SMALL_GATE + branch * NSA_HEADS
            return jnp.concatenate([gates[:, lane + h:lane + h + 1] for h in heads], axis=0)

        o = gate(0) * o_cmp + gate(1) * osel_ref[...] + gate(2) * o_win
        for n, h in enumerate(heads):
            o_ref[:, h * NSA_DH:(h + 1) * NSA_DH] = o[n * NSA_TQ:(n + 1) * NSA_TQ].astype(o_ref.dtype)


def _nsa_constants():
    c0 = np.arange(N_CMP)[:, None] * CMP_STRIDE
    s0 = np.arange(N_SLC)[None, :] * SLC_LEN
    overlap = np.clip(np.minimum(c0 + CMP_LEN, s0 + SLC_LEN) - np.maximum(c0, s0), 0, None) / CMP_STRIDE
    overlap[N_CMP - 1] = 0.0
    expand = (np.arange(SEQ)[None, :] // SLC_LEN == np.arange(N_SLC)[:, None]).astype(np.float32)
    return jnp.asarray(overlap.T, BF16), jnp.asarray(np.eye(NSA_TQ), BF16), jnp.asarray(expand, BF16)


def nsa_attention(p16, cmp_kv, p32, q_col, kv_col, small_col):
    nq = SEQ // NSA_TQ
    ovl_t, eye, expand = _nsa_constants()
    kv_spec = lambda n: pl.BlockSpec((SEQ, NSA_KV), lambda b, i: (b, kv_col // NSA_KV + n))
    return pl.pallas_call(
        _nsa_kernel,
        grid=(BATCH, nq),
        in_specs=[pl.BlockSpec((NSA_TQ, NSA_Q), lambda b, i: (b * nq + i, q_col // NSA_Q)),
                  kv_spec(0), kv_spec(1), kv_spec(2), kv_spec(3),
                  pl.BlockSpec((1, 2 * NSA_GROUPS, N_CMP, NSA_DH), lambda b, i: (b, 0, 0, 0)),
                  pl.BlockSpec((NSA_TQ, LANE), lambda b, i: (b * nq + i, small_col // LANE)),
                  _resident(ovl_t.shape), _resident(eye.shape), _resident(expand.shape)],
        out_specs=pl.BlockSpec((NSA_TQ, NSA_Q), lambda b, i: (b * nq + i, 0)),
        out_shape=jax.ShapeDtypeStruct((TOKENS, NSA_Q), BF16),
        scratch_shapes=[pltpu.VMEM((NSA_TQ, SEQ), F32), pltpu.VMEM((NSA_HG * NSA_TQ, NSA_DH), F32)],
        compiler_params=pltpu.CompilerParams(dimension_semantics=("arbitrary", "arbitrary"),
                                             vmem_limit_bytes=VMEM_LIMIT),
        name="nsa_attention",
    )(p16, p16, p16, p16, p16, cmp_kv, p32, ovl_t, eye, expand)


EVEN_QKV_COL = 0
EVEN_GATE_COL = 2 * GDN_QK + GDN_V
EVEN_CMP_COL = EVEN_GATE_COL + GDN_V
EVEN_SMALL_COL = EVEN_CMP_COL + 2 * NSA_KV
EVEN_F32_W = EVEN_SMALL_COL + LANE
EVEN_Q_COL = 0
EVEN_KV_COL = NSA_Q
EVEN_BF16_W = NSA_Q + 4 * NSA_KV
assert NSA_SMALL_GATE + 3 * NSA_HEADS <= LANE


def even_proj_weight(w_in):
    qkv_a, beta, dec, gate_a, q_b, kv_b, gate_b = split_last(w_in, EVEN_SPLITS)
    pad = jnp.zeros((w_in.shape[0], LANE - 2 * GDN_HEADS - 3 * NSA_HEADS), w_in.dtype)
    return jnp.concatenate([qkv_a, gate_a, kv_b[:, :2 * NSA_KV], beta, dec, gate_b, pad, q_b, kv_b[:, 2 * NSA_KV:]],
                           axis=1).astype(BF16)


def even_mixer_core(p32, p16, conv_w, A_log, dt_bias, o_norm, cmp_pe, cmp_w1, cmp_b1, cmp_w2):
    oa = gated_delta_net(p32, conv_w, A_log, dt_bias, o_norm, EVEN_QKV_COL, EVEN_GATE_COL, EVEN_SMALL_COL)
    strips = p32[:, EVEN_CMP_COL:EVEN_CMP_COL + 2 * NSA_KV].reshape(BATCH, N_CMP, CMP_STRIDE, 2 * NSA_GROUPS, NSA_DH)
    strips = strips.transpose(0, 3, 1, 2, 4).reshape(BATCH, 2 * NSA_GROUPS, N_CMP, CMP_STRIDE * NSA_DH)
    cmp_kv = nsa_compress(strips, cmp_pe, cmp_w1, cmp_b1, cmp_w2)
    ob = nsa_attention(p16, cmp_kv, p32, EVEN_Q_COL, EVEN_KV_COL, EVEN_SMALL_COL)
    return jnp.concatenate([oa, ob], axis=-1)


GLA_TC = 256
ODD_Q_COL = 0
ODD_K_COL = GLA_QK
ODD_V_COL = 2 * GLA_QK
ODD_R_COL = ODD_V_COL + GLA_V
ODD_LR_COL = ODD_R_COL + GLA_V
ODD_F32_W = ODD_LR_COL + LANE


def _gla_kernel(q_ref, k_ref, v_ref, r_ref, lr_ref, aw_ref, ab_ref, onorm_ref, tri_ref, o_ref, state_ref):
    t = pl.program_id(1)
    C = GLA_CHUNK

    @pl.when(t == 0)
    def _():
        state_ref[...] = jnp.zeros(state_ref.shape, F32)

    z = jnp.dot(lr_ref[...].astype(BF16), aw_ref[...], preferred_element_type=F32) + ab_ref[...]
    log_a = (jnp.minimum(z, 0.0) - jnp.log(1.0 + jnp.exp(-jnp.abs(z)))) * (1.0 / GLA_TAU)
    tri = tri_ref[...]
    causal = lax.broadcasted_iota(jnp.int32, (C, C), 0) >= lax.broadcasted_iota(jnp.int32, (C, C), 1)

    for h in range(GLA_HEADS):
        kl = slice(h * GLA_DK, (h + 1) * GLA_DK)
        vl = slice(h * GLA_DV, (h + 1) * GLA_DV)
        state_t = state_ref[h]
        for c in range(GLA_TC // C):
            rows = slice(c * C, (c + 1) * C)
            b = _cumsum_rows(tri, log_a[rows, kl])
            b_last = b[C - 1:C]
            q = q_ref[rows, kl] * GLA_DK ** -0.5
            k = k_ref[rows, kl]
            v = v_ref[rows, vl].astype(BF16)
            qt = (q * jnp.exp(b)).astype(BF16)
            kt = (k * jnp.exp(-b)).astype(BF16)
            kh = (k * jnp.exp(b_last - b)).astype(BF16)
            intra = jnp.where(causal, _dot_nt(qt, kt), 0.0).astype(BF16)
            o = jnp.dot(intra, v, preferred_element_type=F32) + _dot_nt(qt, state_t.astype(BF16))
            state_t = state_t * jnp.exp(b_last) + _dot_tn(v, kh)
            r = r_ref[rows, vl]
            on = o * lax.rsqrt(jnp.mean(o * o, axis=-1, keepdims=True) + EPS) * onorm_ref[...]
            o_ref[rows, vl] = (on * (r * jax.nn.sigmoid(r))).astype(o_ref.dtype)
        state_ref[h] = state_t


def gated_linear_attention(p32, a_w2, a_b, o_norm):
    nt = SEQ // GLA_TC
    tok = lambda blk: (lambda b, t: (b * nt + t, blk))
    aw = jnp.pad(a_w2, ((0, LANE - GLA_RANK), (0, 0))).astype(BF16)
    tri = jnp.asarray(np.tril(np.ones((GLA_CHUNK, GLA_CHUNK))), BF16)
    return pl.pallas_call(
        _gla_kernel,
        grid=(BATCH, nt),
        in_specs=[pl.BlockSpec((GLA_TC, GLA_QK), tok(ODD_Q_COL // GLA_QK)),
                  pl.BlockSpec((GLA_TC, GLA_QK), tok(ODD_K_COL // GLA_QK)),
                  pl.BlockSpec((GLA_TC, GLA_V), tok(ODD_V_COL // GLA_V)),
                  pl.BlockSpec((GLA_TC, GLA_V), tok(ODD_R_COL // GLA_V)),
                  pl.BlockSpec((GLA_TC, LANE), tok(ODD_LR_COL // LANE)),
                  _resident((LANE, GLA_QK)), _resident((1, GLA_QK)), _resident((1, GLA_DV)),
                  _resident((GLA_CHUNK, GLA_CHUNK))],
        out_specs=pl.BlockSpec((GLA_TC, GLA_V), tok(0)),
        out_shape=jax.ShapeDtypeStruct((TOKENS, GLA_V), BF16),
        scratch_shapes=[pltpu.VMEM((GLA_HEADS, GLA_DV, GLA_DK), F32)],
        compiler_params=pltpu.CompilerParams(dimension_semantics=("arbitrary", "arbitrary"),
                                             vmem_limit_bytes=VMEM_LIMIT),
        name="gated_linear_attention",
    )(p32, p32, p32, p32, p32, aw, a_b.reshape(1, GLA_QK), o_norm.reshape(1, GLA_DV), tri)


def _pad_cols(w, n):
    return jnp.pad(w, ((0, 0), (0, n - w.shape[1])))


def kernel(x, norm_gains, ffn_w_in, ffn_conv_w, ffn_conv_b, ffn_w_out, hyb_w_in, hyb_w_out, gdn_conv_w, gdn_A_log,
           gdn_dt_bias, gdn_o_norm, nsa_cmp_pe, nsa_cmp_w1, nsa_cmp_b1, nsa_cmp_w2, gla_w_in, gla_a_w2, gla_a_b,
           gla_o_norm, gla_w_out):
    x2 = x.reshape(TOKENS, D_MODEL)
    for layer in range(DEPTH):
        if layer % 2 == 0:
            e = layer // 2
            p32, p16 = norm_proj(x2, norm_gains[layer, 0], even_proj_weight(hyb_w_in[e]),
                                 ((EVEN_F32_W, F32), (EVEN_BF16_W, BF16)))
            o = even_mixer_core(p32, p16, gdn_conv_w[e], gdn_A_log[e], gdn_dt_bias[e], gdn_o_norm[e], nsa_cmp_pe[e],
                                nsa_cmp_w1[e], nsa_cmp_b1[e], nsa_cmp_w2[e])
            w_out = hyb_w_out[e]
        else:
            o_ = layer // 2
            w = _pad_cols(gla_w_in[o_], ODD_F32_W).astype(BF16)
            (p32,) = norm_proj(x2, norm_gains[layer, 0], w, ((ODD_F32_W, F32),))
            o = gated_linear_attention(p32, gla_a_w2[o_], gla_a_b[o_], gla_o_norm[o_])
            w_out = gla_w_out[o_]
        x2 = out_proj_residual(o.reshape(TOKENS, -1), x2, norm_gains[layer, 1], w_out.astype(BF16))
        x2 = conv_ffn_residual(x2, norm_gains[layer, 2], ffn_w_in[layer].astype(BF16), ffn_conv_w[layer],
                               ffn_conv_b[layer], ffn_w_out[layer].astype(BF16), norm_gains[layer, 3])
    return x2.reshape(BATCH, SEQ, D_MODEL)
```

```python
import functools
import math

import jax
import jax.numpy as jnp
import numpy as np
from jax import lax
from jax.experimental import pallas as pl
from jax.experimental.pallas import tpu as pltpu

D_MODEL = 1024
BATCH = 8
SEQ = 2048
DEPTH = 4
TOKENS = BATCH * SEQ

EPS = 1e-6
NEG = -1e30
FORCE = 1e4

GDN_DK = 128
GDN_DV = 128
GDN_HEADS = 4
GDN_QK = GDN_HEADS * GDN_DK
GDN_V = GDN_HEADS * GDN_DV
GDN_CONV = 4
GDN_CHUNK = 64

NSA_DH = 64
NSA_HEADS = 8
NSA_GROUPS = 2
NSA_Q = NSA_HEADS * NSA_DH
NSA_KV = NSA_GROUPS * NSA_DH
CMP_LEN = 32
CMP_STRIDE = 16
CMP_HIDDEN = 2 * NSA_DH
SLC_LEN = 64
SLC_TOPK = 8
WINDOW = 512
Q_BLOCK = 128

GLA_HEADS = 4
GLA_DK = 128
GLA_DV = 256
GLA_QK = GLA_HEADS * GLA_DK
GLA_V = GLA_HEADS * GLA_DV
GLA_RANK = 16
GLA_TAU = 16.0
GLA_CHUNK = 64

D_FF = 2816
FFN_CONV = 3

EVEN_SPLITS = (2 * GDN_QK + GDN_V, GDN_HEADS, GDN_HEADS, GDN_V, NSA_Q, 6 * NSA_KV, 3 * NSA_HEADS)
ODD_SPLITS = (GLA_QK, GLA_QK, GLA_V, GLA_V, GLA_RANK)

LANE = 128
VMEM_LIMIT = 56 * 1024 * 1024

F32 = jnp.float32
BF16 = jnp.bfloat16


def _round_up(n, m):
    return (n + m - 1) // m * m


def _resident(shape):
    nd = len(shape)
    return pl.BlockSpec(shape, lambda *_: (0,) * nd, pipeline_mode=pl.Buffered(1))


def _rms(x, g):
    ms = jnp.mean(x * x, axis=-1, keepdims=True)
    return x * lax.rsqrt(ms + EPS) * g


PROJ_TM = 256
PROJ_CH = 512


def _proj_kernel(x_ref, g_ref, w_ref, *o_refs):
    h = _rms(x_ref[...], g_ref[...]).astype(BF16)
    col = 0
    for o_ref in o_refs:
        n = o_ref.shape[1]
        for c0 in range(0, n, PROJ_CH):
            cw = min(PROJ_CH, n - c0)
            y = jnp.dot(h, w_ref[:, col + c0:col + c0 + cw], preferred_element_type=F32)
            o_ref[:, c0:c0 + cw] = y.astype(o_ref.dtype)
        col += n


def norm_proj(x2, gain, w, sections):
    T, D = x2.shape
    N = w.shape[1]
    assert N == sum(n for n, _ in sections) and all(n % LANE == 0 for n, _ in sections)
    return pl.pallas_call(
        _proj_kernel,
        grid=(T // PROJ_TM,),
        in_specs=[pl.BlockSpec((PROJ_TM, D), lambda i: (i, 0)), _resident((1, D)), _resident((D, N))],
        out_specs=[pl.BlockSpec((PROJ_TM, n), lambda i: (i, 0)) for n, _ in sections],
        out_shape=[jax.ShapeDtypeStruct((T, n), dt) for n, dt in sections],
        compiler_params=pltpu.CompilerParams(dimension_semantics=("arbitrary",), vmem_limit_bytes=VMEM_LIMIT),
        name="norm_proj",
    )(x2, gain.reshape(1, D), w)


def _outproj_kernel(o_ref, x_ref, g_ref, w_ref, y_ref):
    mix = jnp.dot(o_ref[...].astype(BF16), w_ref[...], preferred_element_type=F32)
    y_ref[...] = x_ref[...] + _rms(mix, g_ref[...])


def out_proj_residual(o2, x2, gain, w):
    T, K = o2.shape
    D = x2.shape[1]
    return pl.pallas_call(
        _outproj_kernel,
        grid=(T // PROJ_TM,),
        in_specs=[pl.BlockSpec((PROJ_TM, K), lambda i: (i, 0)), pl.BlockSpec((PROJ_TM, D), lambda i: (i, 0)),
                  _resident((1, D)), _resident((K, D))],
        out_specs=pl.BlockSpec((PROJ_TM, D), lambda i: (i, 0)),
        out_shape=jax.ShapeDtypeStruct((T, D), F32),
        compiler_params=pltpu.CompilerParams(dimension_semantics=("arbitrary",), vmem_limit_bytes=VMEM_LIMIT),
        name="out_proj_residual",
    )(o2, x2, gain.reshape(1, D), w)


FFN_TM = 256
FFN_HALO = 16
FFN_CH = 256
assert D_FF % FFN_CH == 0


def _ffn_kernel(x_ref, g_in_ref, w_in_ref, cw_ref, cb_ref, w_out_ref, g_out_ref, y_ref, hext_ref, act_ref):
    t = pl.program_id(1)

    @pl.when(t == 0)
    def _():
        hext_ref[0:FFN_HALO, :] = jnp.zeros((FFN_HALO, D_MODEL), BF16)

    x = x_ref[...]
    hext_ref[FFN_HALO:, :] = _rms(x, g_in_ref[...]).astype(BF16)
    hext = hext_ref[...]

    def conv_half(c0):
        u = jnp.dot(hext, w_in_ref[:, c0:c0 + FFN_CH], preferred_element_type=F32)
        u1 = pltpu.roll(u, 1, 0)
        u2 = pltpu.roll(u, 2, 0)
        w = cw_ref[:, c0:c0 + FFN_CH]
        y = u2 * w[0:1] + u1 * w[1:2] + u * w[2:3] + cb_ref[:, c0:c0 + FFN_CH]
        return y[FFN_HALO:]

    for c0 in range(0, D_FF, FFN_CH):
        half_gate = 0.5 * conv_half(c0)
        val = conv_half(D_FF + c0)
        silu = half_gate + half_gate * jnp.tanh(half_gate)
        act_ref[:, c0:c0 + FFN_CH] = (silu * val).astype(BF16)
    down = jnp.dot(act_ref[...], w_out_ref[...], preferred_element_type=F32)
    y_ref[...] = x + _rms(down, g_out_ref[...])
    hext_ref[0:FFN_HALO, :] = hext_ref[FFN_TM:FFN_TM + FFN_HALO, :]


def conv_ffn_residual(x2, g_in, w_in, conv_w, conv_b, w_out, g_out):
    T, D = x2.shape
    nt = SEQ // FFN_TM
    tok = lambda b, t: (b * nt + t, 0)
    return pl.pallas_call(
        _ffn_kernel,
        grid=(BATCH, nt),
        in_specs=[pl.BlockSpec((FFN_TM, D), tok), _resident((1, D)), _resident((D, 2 * D_FF)),
                  _resident((FFN_CONV, 2 * D_FF)), _resident((1, 2 * D_FF)), _resident((D_FF, D)), _resident((1, D))],
        out_specs=pl.BlockSpec((FFN_TM, D), tok),
        out_shape=jax.ShapeDtypeStruct((T, D), F32),
        scratch_shapes=[pltpu.VMEM((FFN_HALO + FFN_TM, D), BF16), pltpu.VMEM((FFN_TM, D_FF), BF16)],
        compiler_params=pltpu.CompilerParams(dimension_semantics=("arbitrary", "arbitrary"),
                                             vmem_limit_bytes=VMEM_LIMIT),
        name="conv_ffn",
    )(x2, g_in.reshape(1, D), w_in, conv_w, conv_b.reshape(1, -1), w_out, g_out.reshape(1, D))


def split_last(x, sizes):
    idx = np.cumsum(sizes)[:-1].tolist()
    return jnp.split(x, idx, axis=-1)


def _dot_nt(a, b):
    return lax.dot_general(a, b, (((1,), (1,)), ((), ())), preferred_element_type=F32)


def _dot_tn(a, b):
    return lax.dot_general(a, b, (((0,), (0,)), ((), ())), preferred_element_type=F32)


def _split3(a):
    hi = a.astype(BF16)
    r = a - hi.astype(F32)
    mid = r.astype(BF16)
    return hi, mid, (r - mid.astype(F32)).astype(BF16)


def _dot_x3(a, b):
    a_hi, a_lo, _ = _split3(a)
    b_hi, b_lo, _ = _split3(b)
    d = lambda p, q: jnp.dot(p, q, preferred_element_type=F32)
    return d(a_hi, b_hi) + (d(a_lo, b_hi) + d(a_hi, b_lo))


def _cumsum_rows(tri, g):
    return sum(jnp.dot(tri, t, preferred_element_type=F32) for t in reversed(_split3(g)))


def _unit_lower_inverses(mats):
    c = mats[0].shape[0]
    eye = jnp.where(lax.broadcasted_iota(jnp.int32, (c, c), 0) == lax.broadcasted_iota(jnp.int32, (c, c), 1), 1.0, 0.0)
    ps = [-a for a in mats]
    ts = [eye + p for p in ps]
    k = 2
    while k < c:
        ps = [_dot_x3(p, p) for p in ps]
        ts = [t + _dot_x3(t, p) for t, p in zip(ts, ps)]
        k *= 2
    return ts


GDN_TC = 256
GDN_HALO = 8
GDN_SMALL_BETA = 0
GDN_SMALL_DECAY = GDN_HEADS
NSA_SMALL_GATE = 2 * GDN_HEADS


def _gdn_kernel(x_ref, gate_ref, small_ref, cw_ref, alog_ref, dtb_ref, onorm_ref, tri_ref, o_ref, xext_ref, state_ref):
    t = pl.program_id(1)
    C = GDN_CHUNK

    @pl.when(t == 0)
    def _():
        xext_ref[0:GDN_HALO, :] = jnp.zeros((GDN_HALO, xext_ref.shape[1]), F32)
        state_ref[...] = jnp.zeros(state_ref.shape, F32)

    xext_ref[GDN_HALO:, :] = x_ref[...]
    small = small_ref[...]
    beta_all = jax.nn.sigmoid(small)
    z = small + dtb_ref[...]
    softplus = jnp.maximum(z, 0.0) + jnp.log(1.0 + jnp.exp(-jnp.abs(z)))
    g_all = -jnp.exp(alog_ref[...]) * softplus
    tri = tri_ref[...]
    row = lax.broadcasted_iota(jnp.int32, (C, C), 0)
    col = lax.broadcasted_iota(jnp.int32, (C, C), 1)

    def conv_silu(c0, width):
        xe = xext_ref[:, c0:c0 + width]
        w = cw_ref[:, c0:c0 + width]
        y = xe * w[GDN_CONV - 1:GDN_CONV]
        for s in range(1, GDN_CONV):
            y = y + pltpu.roll(xe, s, 0) * w[GDN_CONV - 1 - s:GDN_CONV - s]
        y = y[GDN_HALO:]
        return y * jax.nn.sigmoid(y)

    n_chunks = GDN_TC // C
    gsums = [_cumsum_rows(tri, g_all[c * C:(c + 1) * C]) for c in range(n_chunks)]
    gsum_ts = [jnp.transpose(gs) for gs in gsums]

    work = {}
    for h in range(GDN_HEADS):
        q_all = conv_silu(h * GDN_DK, GDN_DK)
        k_all = conv_silu(GDN_QK + h * GDN_DK, GDN_DK)
        v_all = conv_silu(2 * GDN_QK + h * GDN_DV, GDN_DV)
        q_all = q_all * (lax.rsqrt(jnp.sum(q_all * q_all, axis=-1, keepdims=True) + EPS) * GDN_DK ** -0.5)
        k_all = k_all * lax.rsqrt(jnp.sum(k_all * k_all, axis=-1, keepdims=True) + EPS)
        for c in range(n_chunks):
            rows = slice(c * C, (c + 1) * C)
            q, k, v = q_all[rows], k_all[rows], v_all[rows]
            beta = beta_all[rows, GDN_SMALL_BETA + h:GDN_SMALL_BETA + h + 1]
            gcol = gsums[c][:, GDN_SMALL_DECAY + h:GDN_SMALL_DECAY + h + 1]
            grow = gsum_ts[c][GDN_SMALL_DECAY + h:GDN_SMALL_DECAY + h + 1, :]
            glast = gcol[C - 1:C]
            decay = jnp.exp(jnp.where(row >= col, gcol - grow, -jnp.inf))
            kb = k * beta
            kbf = k.astype(BF16)
            eg = jnp.exp(gcol)
            work[h, c] = dict(
                a_kk=jnp.where(row > col, _dot_nt(kb.astype(BF16), kbf) * decay, 0.0),
                a_qk=(_dot_nt(q.astype(BF16), kbf) * decay).astype(BF16),
                rhs=jnp.concatenate([v * beta, kb * eg], axis=-1),
                qg=(q * eg).astype(BF16),
                kd=(k * jnp.exp(glast - gcol)).astype(BF16),
                keep=jnp.exp(glast))
    keys = sorted(work)
    inverses = _unit_lower_inverses([work[key]["a_kk"] for key in keys])
    for key, inv in zip(keys, inverses):
        work[key]["sol"] = _dot_x3(inv, work[key]["rhs"])

    states = [state_ref[h] for h in range(GDN_HEADS)]
    for c in range(n_chunks):
        rows = slice(c * C, (c + 1) * C)
        for h in range(GDN_HEADS):
            wk = work[h, c]
            u, w = wk["sol"][:, :GDN_DV], wk["sol"][:, GDN_DV:]
            sb = states[h].astype(BF16)
            v_new = u - jnp.dot(w.astype(BF16), sb, preferred_element_type=F32)
            vb = v_new.astype(BF16)
            o = (jnp.dot(wk["qg"], sb, preferred_element_type=F32)
                 + jnp.dot(wk["a_qk"], vb, preferred_element_type=F32))
            states[h] = states[h] * wk["keep"] + _dot_tn(wk["kd"], vb)
            gate = gate_ref[rows, h * GDN_DV:(h + 1) * GDN_DV]
            on = o * lax.rsqrt(jnp.mean(o * o, axis=-1, keepdims=True) + EPS) * onorm_ref[...]
            o_ref[rows, h * GDN_DV:(h + 1) * GDN_DV] = (on * (gate * jax.nn.sigmoid(gate))).astype(o_ref.dtype)
    for h in range(GDN_HEADS):
        state_ref[h] = states[h]
    xext_ref[0:GDN_HALO, :] = xext_ref[GDN_TC:GDN_TC + GDN_HALO, :]


def gated_delta_net(p32, conv_w, a_log, dt_bias, o_norm, qkv_col, gate_col, small_col):
    nt = SEQ // GDN_TC
    width = 2 * GDN_QK + GDN_V
    lane_vec = lambda a, off: jnp.zeros((1, LANE), F32).at[0, off:off + GDN_HEADS].set(a.astype(F32))
    tri = jnp.asarray(np.tril(np.ones((GDN_CHUNK, GDN_CHUNK))), BF16)
    tok = lambda blk: (lambda b, t: (b * nt + t, blk))
    return pl.pallas_call(
        _gdn_kernel,
        grid=(BATCH, nt),
        in_specs=[pl.BlockSpec((GDN_TC, width), tok(qkv_col // width)),
                  pl.BlockSpec((GDN_TC, GDN_V), tok(gate_col // GDN_V)),
                  pl.BlockSpec((GDN_TC, LANE), tok(small_col // LANE)),
                  _resident((GDN_CONV, width)), _resident((1, LANE)), _resident((1, LANE)), _resident((1, GDN_DV)),
                  _resident((GDN_CHUNK, GDN_CHUNK))],
        out_specs=pl.BlockSpec((GDN_TC, GDN_V), tok(0)),
        out_shape=jax.ShapeDtypeStruct((TOKENS, GDN_V), BF16),
        scratch_shapes=[pltpu.VMEM((GDN_HALO + GDN_TC, width), F32), pltpu.VMEM((GDN_HEADS, GDN_DK, GDN_DV), F32)],
        compiler_params=pltpu.CompilerParams(dimension_semantics=("arbitrary", "arbitrary"),
                                             vmem_limit_bytes=VMEM_LIMIT),
        name="gated_delta_net",
    )(p32, p32, p32, conv_w, lane_vec(a_log, GDN_SMALL_DECAY), lane_vec(dt_bias, GDN_SMALL_DECAY),
      o_norm.reshape(1, GDN_DV), tri)


N_SLC = SEQ // SLC_LEN
N_CMP = SEQ // CMP_STRIDE
NSA_HG = NSA_HEADS // NSA_GROUPS
NSA_TQ = 128
SEL_STEP = 512
WIN_W = WINDOW + NSA_TQ
NSA_SCALE = NSA_DH ** -0.5


def _cmp_kernel(p_ref, pe_ref, w1_ref, b1_ref, w2_ref, o_ref):
    half = CMP_STRIDE * NSA_DH
    for kv in range(2):
        for g in range(NSA_GROUPS):
            rows = p_ref[0, kv * NSA_GROUPS + g]
            lo = (rows + pe_ref[kv, :, :half]).astype(BF16)
            hi = (rows + pe_ref[kv, :, half:]).astype(BF16)
            a = jnp.dot(lo, w1_ref[kv, :half, :], preferred_element_type=F32)
            b = jnp.dot(hi, w1_ref[kv, half:, :], preferred_element_type=F32)
            hid = a + pltpu.roll(b, N_CMP - 1, 0) + b1_ref[kv]
            act = (hid * jax.nn.sigmoid(hid)).astype(BF16)
            o_ref[0, kv * NSA_GROUPS + g] = jnp.dot(act, w2_ref[kv], preferred_element_type=F32).astype(BF16)


def nsa_compress(strips, pe, w1, b1, w2):
    n = 2 * NSA_GROUPS
    return pl.pallas_call(
        _cmp_kernel,
        grid=(BATCH,),
        in_specs=[pl.BlockSpec((1, n, N_CMP, CMP_STRIDE * NSA_DH), lambda b: (b, 0, 0, 0)),
                  _resident((2, 1, CMP_LEN * NSA_DH)), _resident((2, CMP_LEN * NSA_DH, CMP_HIDDEN)),
                  _resident((2, 1, CMP_HIDDEN)), _resident((2, CMP_HIDDEN, NSA_DH))],
        out_specs=pl.BlockSpec((1, n, N_CMP, NSA_DH), lambda b: (b, 0, 0, 0)),
        out_shape=jax.ShapeDtypeStruct((BATCH, n, N_CMP, NSA_DH), BF16),
        compiler_params=pltpu.CompilerParams(dimension_semantics=("arbitrary",), vmem_limit_bytes=VMEM_LIMIT),
        name="nsa_compress",
    )(strips, pe.reshape(2, 1, -1), w1.astype(BF16), b1.reshape(2, 1, -1), w2.astype(BF16))


def _masked_attention(q, k, v, bias):
    w = k.shape[0]
    s = _dot_nt(q, k).reshape(NSA_HG, NSA_TQ, w) + bias[None]
    s = s.reshape(NSA_HG * NSA_TQ, w)
    p = jnp.exp(s - jnp.max(s, axis=-1, keepdims=True))
    l = jnp.sum(p, axis=-1, keepdims=True)
    return jnp.dot(p.astype(BF16), v, preferred_element_type=F32) / l


def _nsa_kernel(q_ref, ks_ref, vs_ref, kw_ref, vw_ref, cmp_ref, gate_ref, ovl_ref, eye_ref, expand_ref,
                o_ref, sbias_ref, osel_ref):
    i = pl.program_id(1)
    q0 = i * NSA_TQ
    rows = NSA_HG * NSA_TQ
    gates = jax.nn.sigmoid(gate_ref[...])
    tok = q0 + lax.broadcasted_iota(jnp.int32, (NSA_TQ, 1), 0)
    tok_l = q0 + lax.broadcasted_iota(jnp.int32, (1, NSA_TQ), 1)
    w0 = pl.multiple_of(jnp.maximum(q0 + NSA_TQ - WIN_W, 0), NSA_TQ)
    wpos = w0 + lax.broadcasted_iota(jnp.int32, (1, WIN_W), 1)
    wbias = jnp.where((wpos <= tok) & (wpos > tok - WINDOW), 0.0, NEG)

    for g in range(NSA_GROUPS):
        lanes = slice(g * NSA_DH, (g + 1) * NSA_DH)
        heads = [g * NSA_HG + h for h in range(NSA_HG)]
        q = jnp.concatenate([q_ref[:, h * NSA_DH:(h + 1) * NSA_DH] for h in heads], axis=0)
        q = q * jnp.asarray(NSA_SCALE, BF16)

        s = _dot_nt(q, cmp_ref[0, g]).reshape(NSA_HG, NSA_TQ, N_CMP)
        cmp_end = lax.broadcasted_iota(jnp.int32, (1, N_CMP), 1) * CMP_STRIDE + (CMP_LEN - 1)
        cvalid = (cmp_end <= tok)[None]
        sm = jnp.where(cvalid, s, NEG)
        e = jnp.exp(sm - jnp.max(sm, axis=-1, keepdims=True))
        p = jnp.where(cvalid, e / jnp.sum(e, axis=-1, keepdims=True), 0.0)
        o_cmp = jnp.dot(p.reshape(rows, N_CMP).astype(BF16), cmp_ref[0, NSA_GROUPS + g], preferred_element_type=F32)

        imp = _dot_nt(ovl_ref[...], jnp.sum(p, axis=0).astype(BF16))
        blk = lax.broadcasted_iota(jnp.int32, (N_SLC, NSA_TQ), 0)
        cur = tok_l // SLC_LEN
        bvalid = blk <= cur
        forced = (blk == 0) | (blk == cur) | (blk == cur - 1)
        score = jnp.where(bvalid, imp + jnp.where(forced, FORCE, 0.0), NEG)
        rank = jnp.zeros((N_SLC, NSA_TQ), F32)
        for j in range(N_SLC):
            sj = score[j:j + 1, :]
            tie = jnp.where((sj == score) & (blk > j), 1.0, 0.0)
            rank = rank + jnp.where(sj > score, 1.0, tie)
        sel_t = jnp.where((rank < SLC_TOPK) & bvalid, 1.0, 0.0).astype(BF16)
        sel = _dot_nt(eye_ref[...], sel_t).astype(BF16)
        hit = jnp.dot(sel, expand_ref[...], preferred_element_type=F32)
        kpos = lax.broadcasted_iota(jnp.int32, (1, SEQ), 1)
        sbias_ref[...] = jnp.where((hit > 0.5) & (kpos <= tok), 0.0, NEG)

        for n in range(SEQ // SEL_STEP):
            @pl.when(i // (SEL_STEP // NSA_TQ) == n)
            def _():
                w = (n + 1) * SEL_STEP
                osel_ref[...] = _masked_attention(q, ks_ref[0:w, lanes], vs_ref[0:w, lanes], sbias_ref[:, 0:w])

        r = pl.ds(w0, WIN_W)
        o_win = _masked_attention(q, kw_ref[r, lanes], vw_ref[r, lanes], wbias)

        def gate(branch):
            lane = NSA_SMALL_GATE + branch * NSA_HEADS
            return jnp.concatenate([gates[:, lane + h:lane + h + 1] for h in heads], axis=0)

        o = gate(0) * o_cmp + gate(1) * osel_ref[...] + gate(2) * o_win
        for n, h in enumerate(heads):
            o_ref[:, h * NSA_DH:(h + 1) * NSA_DH] = o[n * NSA_TQ:(n + 1) * NSA_TQ].astype(o_ref.dtype)


def _nsa_constants():
    c0 = np.arange(N_CMP)[:, None] * CMP_STRIDE
    s0 = np.arange(N_SLC)[None, :] * SLC_LEN
    overlap = np.clip(np.minimum(c0 + CMP_LEN, s0 + SLC_LEN) - np.maximum(c0, s0), 0, None) / CMP_STRIDE
    overlap[N_CMP - 1] = 0.0
    expand = (np.arange(SEQ)[None, :] // SLC_LEN == np.arange(N_SLC)[:, None]).astype(np.float32)
    return jnp.asarray(overlap.T, BF16), jnp.asarray(np.eye(NSA_TQ), BF16), jnp.asarray(expand, BF16)


def nsa_attention(p16, cmp_kv, p32, q_col, kv_col, small_col):
    nq = SEQ // NSA_TQ
    ovl_t, eye, expand = _nsa_constants()
    kv_spec = lambda n: pl.BlockSpec((SEQ, NSA_KV), lambda b, i: (b, kv_col // NSA_KV + n))
    return pl.pallas_call(
        _nsa_kernel,
        grid=(BATCH, nq),
        in_specs=[pl.BlockSpec((NSA_TQ, NSA_Q), lambda b, i: (b * nq + i, q_col // NSA_Q)),
                  kv_spec(0), kv_spec(1), kv_spec(2), kv_spec(3),
                  pl.BlockSpec((1, 2 * NSA_GROUPS, N_CMP, NSA_DH), lambda b, i: (b, 0, 0, 0)),
                  pl.BlockSpec((NSA_TQ, LANE), lambda b, i: (b * nq + i, small_col // LANE)),
                  _resident(ovl_t.shape), _resident(eye.shape), _resident(expand.shape)],
        out_specs=pl.BlockSpec((NSA_TQ, NSA_Q), lambda b, i: (b * nq + i, 0)),
        out_shape=jax.ShapeDtypeStruct((TOKENS, NSA_Q), BF16),
        scratch_shapes=[pltpu.VMEM((NSA_TQ, SEQ), F32), pltpu.VMEM((NSA_HG * NSA_TQ, NSA_DH), F32)],
        compiler_params=pltpu.CompilerParams(dimension_semantics=("arbitrary", "arbitrary"),
                                             vmem_limit_bytes=VMEM_LIMIT),
        name="nsa_attention",
    )(p16, p16, p16, p16, p16, cmp_kv, p32, ovl_t, eye, expand)


EVEN_QKV_COL = 0
EVEN_GATE_COL = 2 * GDN_QK + GDN_V
EVEN_CMP_COL = EVEN_GATE_COL + GDN_V
EVEN_SMALL_COL = EVEN_CMP_COL + 2 * NSA_KV
EVEN_F32_W = EVEN_SMALL_COL + LANE
EVEN_Q_COL = 0
EVEN_KV_COL = NSA_Q
EVEN_BF16_W = NSA_Q + 4 * NSA_KV
assert NSA_SMALL_GATE + 3 * NSA_HEADS <= LANE


def even_proj_weight(w_in):
    qkv_a, beta, dec, gate_a, q_b, kv_b, gate_b = split_last(w_in, EVEN_SPLITS)
    pad = jnp.zeros((w_in.shape[0], LANE - 2 * GDN_HEADS - 3 * NSA_HEADS), w_in.dtype)
    return jnp.concatenate([qkv_a, gate_a, kv_b[:, :2 * NSA_KV], beta, dec, gate_b, pad, q_b, kv_b[:, 2 * NSA_KV:]],
                           axis=1).astype(BF16)


def even_mixer_core(p32, p16, conv_w, A_log, dt_bias, o_norm, cmp_pe, cmp_w1, cmp_b1, cmp_w2):
    oa = gated_delta_net(p32, conv_w, A_log, dt_bias, o_norm, EVEN_QKV_COL, EVEN_GATE_COL, EVEN_SMALL_COL)
    strips = p32[:, EVEN_CMP_COL:EVEN_CMP_COL + 2 * NSA_KV].reshape(BATCH, N_CMP, CMP_STRIDE, 2 * NSA_GROUPS, NSA_DH)
    strips = strips.transpose(0, 3, 1, 2, 4).reshape(BATCH, 2 * NSA_GROUPS, N_CMP, CMP_STRIDE * NSA_DH)
    cmp_kv = nsa_compress(strips, cmp_pe, cmp_w1, cmp_b1, cmp_w2)
    ob = nsa_attention(p16, cmp_kv, p32, EVEN_Q_COL, EVEN_KV_COL, EVEN_SMALL_COL)
    return jnp.concatenate([oa, ob], axis=-1)


GLA_TC = 256
ODD_Q_COL = 0
ODD_K_COL = GLA_QK
ODD_V_COL = 2 * GLA_QK
ODD_R_COL = ODD_V_COL + GLA_V
ODD_LR_COL = ODD_R_COL + GLA_V
ODD_F32_W = ODD_LR_COL + LANE


def _gla_kernel(q_ref, k_ref, v_ref, r_ref, lr_ref, aw_ref, ab_ref, onorm_ref, tri_ref, o_ref, state_ref):
    t = pl.program_id(1)
    C = GLA_CHUNK

    @pl.when(t == 0)
    def _():
        state_ref[...] = jnp.zeros(state_ref.shape, F32)

    z = jnp.dot(lr_ref[...].astype(BF16), aw_ref[...], preferred_element_type=F32) + ab_ref[...]
    log_a = (jnp.minimum(z, 0.0) - jnp.log(1.0 + jnp.exp(-jnp.abs(z)))) * (1.0 / GLA_TAU)
    tri = tri_ref[...]
    causal = lax.broadcasted_iota(jnp.int32, (C, C), 0) >= lax.broadcasted_iota(jnp.int32, (C, C), 1)

    states = [state_ref[h] for h in range(GLA_HEADS)]
    for c in range(GLA_TC // C):
        rows = slice(c * C, (c + 1) * C)
        b_all = _cumsum_rows(tri, log_a[rows])
        for h in range(GLA_HEADS):
            kl = slice(h * GLA_DK, (h + 1) * GLA_DK)
            vl = slice(h * GLA_DV, (h + 1) * GLA_DV)
            b = b_all[:, kl]
            b_last = b[C - 1:C]
            q = q_ref[rows, kl] * GLA_DK ** -0.5
            k = k_ref[rows, kl]
            v = v_ref[rows, vl].astype(BF16)
            qt = (q * jnp.exp(b)).astype(BF16)
            kt = (k * jnp.exp(-b)).astype(BF16)
            kh = (k * jnp.exp(b_last - b)).astype(BF16)
            intra = jnp.where(causal, _dot_nt(qt, kt), 0.0).astype(BF16)
            o = jnp.dot(intra, v, preferred_element_type=F32) + _dot_nt(qt, states[h].astype(BF16))
            states[h] = states[h] * jnp.exp(b_last) + _dot_tn(v, kh)
            r = r_ref[rows, vl]
            on = o * lax.rsqrt(jnp.mean(o * o, axis=-1, keepdims=True) + EPS) * onorm_ref[...]
            o_ref[rows, vl] = (on * (r * jax.nn.sigmoid(r))).astype(o_ref.dtype)
    for h in range(GLA_HEADS):
        state_ref[h] = states[h]


def gated_linear_attention(p32, a_w2, a_b, o_norm):
    nt = SEQ // GLA_TC
    tok = lambda blk: (lambda b, t: (b * nt + t, blk))
    aw = jnp.pad(a_w2, ((0, LANE - GLA_RANK), (0, 0))).astype(BF16)
    tri = jnp.asarray(np.tril(np.ones((GLA_CHUNK, GLA_CHUNK))), BF16)
    return pl.pallas_call(
        _gla_kernel,
        grid=(BATCH, nt),
        in_specs=[pl.BlockSpec((GLA_TC, GLA_QK), tok(ODD_Q_COL // GLA_QK)),
                  pl.BlockSpec((GLA_TC, GLA_QK), tok(ODD_K_COL // GLA_QK)),
                  pl.BlockSpec((GLA_TC, GLA_V), tok(ODD_V_COL // GLA_V)),
                  pl.BlockSpec((GLA_TC, GLA_V), tok(ODD_R_COL // GLA_V)),
                  pl.BlockSpec((GLA_TC, LANE), tok(ODD_LR_COL // LANE)),
                  _resident((LANE, GLA_QK)), _resident((1, GLA_QK)), _resident((1, GLA_DV)),
                  _resident((GLA_CHUNK, GLA_CHUNK))],
        out_specs=pl.BlockSpec((GLA_TC, GLA_V), tok(0)),
        out_shape=jax.ShapeDtypeStruct((TOKENS, GLA_V), BF16),
        scratch_shapes=[pltpu.VMEM((GLA_HEADS, GLA_DV, GLA_DK), F32)],
        compiler_params=pltpu.CompilerParams(dimension_semantics=("arbitrary", "arbitrary"),
                                             vmem_limit_bytes=VMEM_LIMIT),
        name="gated_linear_attention",
    )(p32, p32, p32, p32, p32, aw, a_b.reshape(1, GLA_QK), o_norm.reshape(1, GLA_DV), tri)


def _pad_cols(w, n):
    return jnp.pad(w, ((0, 0), (0, n - w.shape[1])))


def kernel(x, norm_gains, ffn_w_in, ffn_conv_w, ffn_conv_b, ffn_w_out, hyb_w_in, hyb_w_out, gdn_conv_w, gdn_A_log,
           gdn_dt_bias, gdn_o_norm, nsa_cmp_pe, nsa_cmp_w1, nsa_cmp_b1, nsa_cmp_w2, gla_w_in, gla_a_w2, gla_a_b,
           gla_o_norm, gla_w_out):
    x2 = x.reshape(TOKENS, D_MODEL)
    for layer in range(DEPTH):
        if layer % 2 == 0:
            e = layer // 2
            p32, p16 = norm_proj(x2, norm_gains[layer, 0], even_proj_weight(hyb_w_in[e]),
                                 ((EVEN_F32_W, F32), (EVEN_BF16_W, BF16)))
            o = even_mixer_core(p32, p16, gdn_conv_w[e], gdn_A_log[e], gdn_dt_bias[e], gdn_o_norm[e], nsa_cmp_pe[e],
                                nsa_cmp_w1[e], nsa_cmp_b1[e], nsa_cmp_w2[e])
            w_out = hyb_w_out[e]
        else:
            o_ = layer // 2
            w = _pad_cols(gla_w_in[o_], ODD_F32_W).astype(BF16)
            (p32,) = norm_proj(x2, norm_gains[layer, 0], w, ((ODD_F32_W, F32),))
            o = gated_linear_attention(p32, gla_a_w2[o_], gla_a_b[o_], gla_o_norm[o_])
            w_out = gla_w_out[o_]
        x2 = out_proj_residual(o.reshape(TOKENS, -1), x2, norm_gains[layer, 1], w_out.astype(BF16))
        x2 = conv_ffn_residual(x2, norm_gains[layer, 2], ffn_w_in[layer].astype(BF16), ffn_conv_w[layer],
                               ffn_conv_b[layer], ffn_w_out[layer].astype(BF16), norm_gains[layer, 3])
    return x2.reshape(BATCH, SEQ, D_MODEL)
```

```python
import functools
import math

import jax
import jax.numpy as jnp
import numpy as np
from jax import lax
from jax.experimental import pallas as pl
from jax.experimental.pallas import tpu as pltpu

D_MODEL = 1024
BATCH = 8
SEQ = 2048
DEPTH = 4
TOKENS = BATCH * SEQ

EPS = 1e-6
NEG = -1e30
FORCE = 1e4

GDN_DK = 128
GDN_DV = 128
GDN_HEADS = 4
GDN_QK = GDN_HEADS * GDN_DK
GDN_V = GDN_HEADS * GDN_DV
GDN_CONV = 4
GDN_CHUNK = 64

NSA_DH = 64
NSA_HEADS = 8
NSA_GROUPS = 2
NSA_Q = NSA_HEADS * NSA_DH
NSA_KV = NSA_GROUPS * NSA_DH
CMP_LEN = 32
CMP_STRIDE = 16
CMP_HIDDEN = 2 * NSA_DH
SLC_LEN = 64
SLC_TOPK = 8
WINDOW = 512
Q_BLOCK = 128

GLA_HEADS = 4
GLA_DK = 128
GLA_DV = 256
GLA_QK = GLA_HEADS * GLA_DK
GLA_V = GLA_HEADS * GLA_DV
GLA_RANK = 16
GLA_TAU = 16.0
GLA_CHUNK = 64

D_FF = 2816
FFN_CONV = 3

EVEN_SPLITS = (2 * GDN_QK + GDN_V, GDN_HEADS, GDN_HEADS, GDN_V, NSA_Q, 6 * NSA_KV, 3 * NSA_HEADS)
ODD_SPLITS = (GLA_QK, GLA_QK, GLA_V, GLA_V, GLA_RANK)

LANE = 128
VMEM_LIMIT = 56 * 1024 * 1024

F32 = jnp.float32
BF16 = jnp.bfloat16


def _round_up(n, m):
    return (n + m - 1) // m * m


def _resident(shape):
    nd = len(shape)
    return pl.BlockSpec(shape, lambda *_: (0,) * nd, pipeline_mode=pl.Buffered(1))


def _rms(x, g):
    ms = jnp.mean(x * x, axis=-1, keepdims=True)
    return x * lax.rsqrt(ms + EPS) * g


PROJ_TM = 256
PROJ_CH = 512


def _proj_kernel(x_ref, g_ref, w_ref, *o_refs):
    h = _rms(x_ref[...], g_ref[...]).astype(BF16)
    col = 0
    for o_ref in o_refs:
        n = o_ref.shape[1]
        for c0 in range(0, n, PROJ_CH):
            cw = min(PROJ_CH, n - c0)
            y = jnp.dot(h, w_ref[:, col + c0:col + c0 + cw], preferred_element_type=F32)
            o_ref[:, c0:c0 + cw] = y.astype(o_ref.dtype)
        col += n


def norm_proj(x2, gain, w, sections):
    T, D = x2.shape
    N = w.shape[1]
    assert N == sum(n for n, _ in sections) and all(n % LANE == 0 for n, _ in sections)
    return pl.pallas_call(
        _proj_kernel,
        grid=(T // PROJ_TM,),
        in_specs=[pl.BlockSpec((PROJ_TM, D), lambda i: (i, 0)), _resident((1, D)), _resident((D, N))],
        out_specs=[pl.BlockSpec((PROJ_TM, n), lambda i: (i, 0)) for n, _ in sections],
        out_shape=[jax.ShapeDtypeStruct((T, n), dt) for n, dt in sections],
        compiler_params=pltpu.CompilerParams(dimension_semantics=("arbitrary",), vmem_limit_bytes=VMEM_LIMIT),
        name="norm_proj",
    )(x2, gain.reshape(1, D), w)


def _outproj_kernel(o_ref, x_ref, g_ref, w_ref, y_ref):
    mix = jnp.dot(o_ref[...].astype(BF16), w_ref[...], preferred_element_type=F32)
    y_ref[...] = x_ref[...] + _rms(mix, g_ref[...])


def out_proj_residual(o2, x2, gain, w):
    T, K = o2.shape
    D = x2.shape[1]
    return pl.pallas_call(
        _outproj_kernel,
        grid=(T // PROJ_TM,),
        in_specs=[pl.BlockSpec((PROJ_TM, K), lambda i: (i, 0)), pl.BlockSpec((PROJ_TM, D), lambda i: (i, 0)),
                  _resident((1, D)), _resident((K, D))],
        out_specs=pl.BlockSpec((PROJ_TM, D), lambda i: (i, 0)),
        out_shape=jax.ShapeDtypeStruct((T, D), F32),
        compiler_params=pltpu.CompilerParams(dimension_semantics=("arbitrary",), vmem_limit_bytes=VMEM_LIMIT),
        name="out_proj_residual",
    )(o2, x2, gain.reshape(1, D), w)


FFN_TM = 256
FFN_HALO = 16
FFN_CH = 256
assert D_FF % FFN_CH == 0


def _ffn_kernel(x_ref, g_in_ref, w_in_ref, cw_ref, cb_ref, w_out_ref, g_out_ref, y_ref, hext_ref, act_ref):
    t = pl.program_id(1)

    @pl.when(t == 0)
    def _():
        hext_ref[0:FFN_HALO, :] = jnp.zeros((FFN_HALO, D_MODEL), BF16)

    x = x_ref[...]
    hext_ref[FFN_HALO:, :] = _rms(x, g_in_ref[...]).astype(BF16)
    hext = hext_ref[...]

    def conv_half(c0):
        u = jnp.dot(hext, w_in_ref[:, c0:c0 + FFN_CH], preferred_element_type=F32)
        u1 = pltpu.roll(u, 1, 0)
        u2 = pltpu.roll(u, 2, 0)
        w = cw_ref[:, c0:c0 + FFN_CH]
        y = u2 * w[0:1] + u1 * w[1:2] + u * w[2:3] + cb_ref[:, c0:c0 + FFN_CH]
        return y[FFN_HALO:]

    for c0 in range(0, D_FF, FFN_CH):
        half_gate = 0.5 * conv_half(c0)
        val = conv_half(D_FF + c0)
        silu = half_gate + half_gate * jnp.tanh(half_gate)
        act_ref[:, c0:c0 + FFN_CH] = (silu * val).astype(BF16)
    down = jnp.dot(act_ref[...], w_out_ref[...], preferred_element_type=F32)
    y_ref[...] = x + _rms(down, g_out_ref[...])
    hext_ref[0:FFN_HALO, :] = hext_ref[FFN_TM:FFN_TM + FFN_HALO, :]


def conv_ffn_residual(x2, g_in, w_in, conv_w, conv_b, w_out, g_out):
    T, D = x2.shape
    nt = SEQ // FFN_TM
    tok = lambda b, t: (b * nt + t, 0)
    return pl.pallas_call(
        _ffn_kernel,
        grid=(BATCH, nt),
        in_specs=[pl.BlockSpec((FFN_TM, D), tok), _resident((1, D)), _resident((D, 2 * D_FF)),
                  _resident((FFN_CONV, 2 * D_FF)), _resident((1, 2 * D_FF)), _resident((D_FF, D)), _resident((1, D))],
        out_specs=pl.BlockSpec((FFN_TM, D), tok),
        out_shape=jax.ShapeDtypeStruct((T, D), F32),
        scratch_shapes=[pltpu.VMEM((FFN_HALO + FFN_TM, D), BF16), pltpu.VMEM((FFN_TM, D_FF), BF16)],
        compiler_params=pltpu.CompilerParams(dimension_semantics=("arbitrary", "arbitrary"),
                                             vmem_limit_bytes=VMEM_LIMIT),
        name="conv_ffn",
    )(x2, g_in.reshape(1, D), w_in, conv_w, conv_b.reshape(1, -1), w_out, g_out.reshape(1, D))


def split_last(x, sizes):
    idx = np.cumsum(sizes)[:-1].tolist()
    return jnp.split(x, idx, axis=-1)


def _dot_nt(a, b):
    return lax.dot_general(a, b, (((1,), (1,)), ((), ())), preferred_element_type=F32)


def _dot_tn(a, b):
    return lax.dot_general(a, b, (((0,), (0,)), ((), ())), preferred_element_type=F32)


def _split3(a):
    hi = a.astype(BF16)
    r = a - hi.astype(F32)
    mid = r.astype(BF16)
    return hi, mid, (r - mid.astype(F32)).astype(BF16)


def _dot_x3(a, b):
    a_hi, a_lo, _ = _split3(a)
    b_hi, b_lo, _ = _split3(b)
    d = lambda p, q: jnp.dot(p, q, preferred_element_type=F32)
    return d(a_hi, b_hi) + (d(a_lo, b_hi) + d(a_hi, b_lo))


def _cumsum_rows(tri, g):
    return sum(jnp.dot(tri, t, preferred_element_type=F32) for t in reversed(_split3(g)))


def _unit_lower_inverses(mats):
    c = mats[0].shape[0]
    eye = jnp.where(lax.broadcasted_iota(jnp.int32, (c, c), 0) == lax.broadcasted_iota(jnp.int32, (c, c), 1), 1.0, 0.0)
    ps = [-a for a in mats]
    ts = [eye + p for p in ps]
    k = 2
    while k < c:
        ps = [_dot_x3(p, p) for p in ps]
        ts = [t + _dot_x3(t, p) for t, p in zip(ts, ps)]
        k *= 2
    return ts


GDN_TC = 256
GDN_HALO = 8
GDN_SMALL_BETA = 0
GDN_SMALL_DECAY = GDN_HEADS
NSA_SMALL_GATE = 2 * GDN_HEADS


def _gdn_kernel(x_ref, gate_ref, small_ref, cw_ref, alog_ref, dtb_ref, onorm_ref, tri_ref, o_ref, xext_ref, state_ref):
    t = pl.program_id(1)
    C = GDN_CHUNK

    @pl.when(t == 0)
    def _():
        xext_ref[0:GDN_HALO, :] = jnp.zeros((GDN_HALO, xext_ref.shape[1]), F32)
        state_ref[...] = jnp.zeros(state_ref.shape, F32)

    xext_ref[GDN_HALO:, :] = x_ref[...]
    small = small_ref[...]
    beta_all = jax.nn.sigmoid(small)
    z = small + dtb_ref[...]
    softplus = jnp.maximum(z, 0.0) + jnp.log(1.0 + jnp.exp(-jnp.abs(z)))
    g_all = -jnp.exp(alog_ref[...]) * softplus
    tri = tri_ref[...]
    row = lax.broadcasted_iota(jnp.int32, (C, C), 0)
    col = lax.broadcasted_iota(jnp.int32, (C, C), 1)

    def conv_silu(c0, width):
        xe = xext_ref[:, c0:c0 + width]
        w = cw_ref[:, c0:c0 + width]
        y = xe * w[GDN_CONV - 1:GDN_CONV]
        for s in range(1, GDN_CONV):
            y = y + pltpu.roll(xe, s, 0) * w[GDN_CONV - 1 - s:GDN_CONV - s]
        y = y[GDN_HALO:]
        return y * jax.nn.sigmoid(y)

    n_chunks = GDN_TC // C
    gsums = [_cumsum_rows(tri, g_all[c * C:(c + 1) * C]) for c in range(n_chunks)]
    gsum_ts = [jnp.transpose(gs) for gs in gsums]

    work = {}
    for h in range(GDN_HEADS):
        q_all = conv_silu(h * GDN_DK, GDN_DK)
        k_all = conv_silu(GDN_QK + h * GDN_DK, GDN_DK)
        v_all = conv_silu(2 * GDN_QK + h * GDN_DV, GDN_DV)
        q_all = q_all * (lax.rsqrt(jnp.sum(q_all * q_all, axis=-1, keepdims=True) + EPS) * GDN_DK ** -0.5)
        k_all = k_all * lax.rsqrt(jnp.sum(k_all * k_all, axis=-1, keepdims=True) + EPS)
        for c in range(n_chunks):
            rows = slice(c * C, (c + 1) * C)
            q, k, v = q_all[rows], k_all[rows], v_all[rows]
            beta = beta_all[rows, GDN_SMALL_BETA + h:GDN_SMALL_BETA + h + 1]
            gcol = gsums[c][:, GDN_SMALL_DECAY + h:GDN_SMALL_DECAY + h + 1]
            grow = gsum_ts[c][GDN_SMALL_DECAY + h:GDN_SMALL_DECAY + h + 1, :]
            glast = gcol[C - 1:C]
            decay = jnp.exp(jnp.where(row >= col, gcol - grow, -jnp.inf))
            kb = k * beta
            kbf = k.astype(BF16)
            eg = jnp.exp(gcol)
            work[h, c] = dict(
                a_kk=jnp.where(row > col, _dot_nt(kb.astype(BF16), kbf) * decay, 0.0),
                a_qk=(_dot_nt(q.astype(BF16), kbf) * decay).astype(BF16),
                rhs=jnp.concatenate([v * beta, kb * eg], axis=-1),
                qg=(q * eg).astype(BF16),
                kd=(k * jnp.exp(glast - gcol)).astype(BF16),
                keep=jnp.exp(glast))
    keys = sorted(work)
    inverses = _unit_lower_inverses([work[key]["a_kk"] for key in keys])
    for key, inv in zip(keys, inverses):
        work[key]["sol"] = _dot_x3(inv, work[key]["rhs"])

    states = [state_ref[h] for h in range(GDN_HEADS)]
    for c in range(n_chunks):
        rows = slice(c * C, (c + 1) * C)
        for h in range(GDN_HEADS):
            wk = work[h, c]
            u, w = wk["sol"][:, :GDN_DV], wk["sol"][:, GDN_DV:]
            sb = states[h].astype(BF16)
            v_new = u - jnp.dot(w.astype(BF16), sb, preferred_element_type=F32)
            vb = v_new.astype(BF16)
            o = (jnp.dot(wk["qg"], sb, preferred_element_type=F32)
                 + jnp.dot(wk["a_qk"], vb, preferred_element_type=F32))
            states[h] = states[h] * wk["keep"] + _dot_tn(wk["kd"], vb)
            gate = gate_ref[rows, h * GDN_DV:(h + 1) * GDN_DV]
            on = o * lax.rsqrt(jnp.mean(o * o, axis=-1, keepdims=True) + EPS) * onorm_ref[...]
            o_ref[rows, h * GDN_DV:(h + 1) * GDN_DV] = (on * (gate * jax.nn.sigmoid(gate))).astype(o_ref.dtype)
    for h in range(GDN_HEADS):
        state_ref[h] = states[h]
    xext_ref[0:GDN_HALO, :] = xext_ref[GDN_TC:GDN_TC + GDN_HALO, :]


def gated_delta_net(p32, conv_w, a_log, dt_bias, o_norm, qkv_col, gate_col, small_col):
    nt = SEQ // GDN_TC
    width = 2 * GDN_QK + GDN_V
    lane_vec = lambda a, off: jnp.zeros((1, LANE), F32).at[0, off:off + GDN_HEADS].set(a.astype(F32))
    tri = jnp.asarray(np.tril(np.ones((GDN_CHUNK, GDN_CHUNK))), BF16)
    tok = lambda blk: (lambda b, t: (b * nt + t, blk))
    return pl.pallas_call(
        _gdn_kernel,
        grid=(BATCH, nt),
        in_specs=[pl.BlockSpec((GDN_TC, width), tok(qkv_col // width)),
                  pl.BlockSpec((GDN_TC, GDN_V), tok(gate_col // GDN_V)),
                  pl.BlockSpec((GDN_TC, LANE), tok(small_col // LANE)),
                  _resident((GDN_CONV, width)), _resident((1, LANE)), _resident((1, LANE)), _resident((1, GDN_DV)),
                  _resident((GDN_CHUNK, GDN_CHUNK))],
        out_specs=pl.BlockSpec((GDN_TC, GDN_V), tok(0)),
        out_shape=jax.ShapeDtypeStruct((TOKENS, GDN_V), BF16),
        scratch_shapes=[pltpu.VMEM((GDN_HALO + GDN_TC, width), F32), pltpu.VMEM((GDN_HEADS, GDN_DK, GDN_DV), F32)],
        compiler_params=pltpu.CompilerParams(dimension_semantics=("arbitrary", "arbitrary"),
                                             vmem_limit_bytes=VMEM_LIMIT),
        name="gated_delta_net",
    )(p32, p32, p32, conv_w, lane_vec(a_log, GDN_SMALL_DECAY), lane_vec(dt_bias, GDN_SMALL_DECAY),
      o_norm.reshape(1, GDN_DV), tri)


N_SLC = SEQ // SLC_LEN
N_CMP = SEQ // CMP_STRIDE
NSA_HG = NSA_HEADS // NSA_GROUPS
NSA_TQ = 128
SEL_STEP = 512
WIN_W = WINDOW + NSA_TQ
NSA_SCALE = NSA_DH ** -0.5


def _cmp_kernel(p_ref, pe_ref, w1_ref, b1_ref, w2_ref, o_ref):
    half = CMP_STRIDE * NSA_DH
    for kv in range(2):
        for g in range(NSA_GROUPS):
            rows = p_ref[0, kv * NSA_GROUPS + g]
            lo = (rows + pe_ref[kv, :, :half]).astype(BF16)
            hi = (rows + pe_ref[kv, :, half:]).astype(BF16)
            a = jnp.dot(lo, w1_ref[kv, :half, :], preferred_element_type=F32)
            b = jnp.dot(hi, w1_ref[kv, half:, :], preferred_element_type=F32)
            hid = a + pltpu.roll(b, N_CMP - 1, 0) + b1_ref[kv]
            act = (hid * jax.nn.sigmoid(hid)).astype(BF16)
            o_ref[0, kv * NSA_GROUPS + g] = jnp.dot(act, w2_ref[kv], preferred_element_type=F32).astype(BF16)


def nsa_compress(strips, pe, w1, b1, w2):
    n = 2 * NSA_GROUPS
    return pl.pallas_call(
        _cmp_kernel,
        grid=(BATCH,),
        in_specs=[pl.BlockSpec((1, n, N_CMP, CMP_STRIDE * NSA_DH), lambda b: (b, 0, 0, 0)),
                  _resident((2, 1, CMP_LEN * NSA_DH)), _resident((2, CMP_LEN * NSA_DH, CMP_HIDDEN)),
                  _resident((2, 1, CMP_HIDDEN)), _resident((2, CMP_HIDDEN, NSA_DH))],
        out_specs=pl.BlockSpec((1, n, N_CMP, NSA_DH), lambda b: (b, 0, 0, 0)),
        out_shape=jax.ShapeDtypeStruct((BATCH, n, N_CMP, NSA_DH), BF16),
        compiler_params=pltpu.CompilerParams(dimension_semantics=("arbitrary",), vmem_limit_bytes=VMEM_LIMIT),
        name="nsa_compress",
    )(strips, pe.reshape(2, 1, -1), w1.astype(BF16), b1.reshape(2, 1, -1), w2.astype(BF16))


def _masked_attention(q, k, v, bias):
    w = k.shape[0]
    s = _dot_nt(q, k).reshape(NSA_HG, NSA_TQ, w) + bias[None]
    s = s.reshape(NSA_HG * NSA_TQ, w)
    p = jnp.exp(s - jnp.max(s, axis=-1, keepdims=True))
    l = jnp.sum(p, axis=-1, keepdims=True)
    return jnp.dot(p.astype(BF16), v, preferred_element_type=F32) / l


def _nsa_kernel(q_ref, ks_ref, vs_ref, kw_ref, vw_ref, cmp_ref, gate_ref, ovl_ref, eye_ref, expand_ref,
                o_ref, sbias_ref, qs_ref, ocmp_ref, gsel_ref, gwin_ref):
    i = pl.program_id(1)
    q0 = i * NSA_TQ
    rows = NSA_HG * NSA_TQ
    gates = jax.nn.sigmoid(gate_ref[...])
    tok = q0 + lax.broadcasted_iota(jnp.int32, (NSA_TQ, 1), 0)
    tok_l = q0 + lax.broadcasted_iota(jnp.int32, (1, NSA_TQ), 1)

    for g in range(NSA_GROUPS):
        heads = [g * NSA_HG + h for h in range(NSA_HG)]
        q = jnp.concatenate([q_ref[:, h * NSA_DH:(h + 1) * NSA_DH] for h in heads], axis=0)
        q = q * jnp.asarray(NSA_SCALE, BF16)
        qs_ref[g] = q

        s = _dot_nt(q, cmp_ref[0, g]).reshape(NSA_HG, NSA_TQ, N_CMP)
        cmp_end = lax.broadcasted_iota(jnp.int32, (1, N_CMP), 1) * CMP_STRIDE + (CMP_LEN - 1)
        cvalid = (cmp_end <= tok)[None]
        sm = jnp.where(cvalid, s, NEG)
        e = jnp.exp(sm - jnp.max(sm, axis=-1, keepdims=True))
        p = jnp.where(cvalid, e / jnp.sum(e, axis=-1, keepdims=True), 0.0)
        o_cmp = jnp.dot(p.reshape(rows, N_CMP).astype(BF16), cmp_ref[0, NSA_GROUPS + g], preferred_element_type=F32)

        def gate(branch):
            lane = NSA_SMALL_GATE + branch * NSA_HEADS
            return jnp.concatenate([gates[:, lane + h:lane + h + 1] for h in heads], axis=0)

        ocmp_ref[g] = gate(0) * o_cmp
        gsel_ref[g] = jnp.broadcast_to(gate(1), (rows, NSA_DH))
        gwin_ref[g] = jnp.broadcast_to(gate(2), (rows, NSA_DH))

        imp = _dot_nt(ovl_ref[...], jnp.sum(p, axis=0).astype(BF16))
        blk = lax.broadcasted_iota(jnp.int32, (N_SLC, NSA_TQ), 0)
        cur = tok_l // SLC_LEN
        bvalid = blk <= cur
        forced = (blk == 0) | (blk == cur) | (blk == cur - 1)
        score = jnp.where(bvalid, imp + jnp.where(forced, FORCE, 0.0), NEG)
        rank = jnp.zeros((N_SLC, NSA_TQ), F32)
        for j in range(N_SLC):
            sj = score[j:j + 1, :]
            tie = jnp.where((sj == score) & (blk > j), 1.0, 0.0)
            rank = rank + jnp.where(sj > score, 1.0, tie)
        sel_t = jnp.where((rank < SLC_TOPK) & bvalid, 1.0, 0.0).astype(BF16)
        sel = _dot_nt(eye_ref[...], sel_t).astype(BF16)
        hit = jnp.dot(sel, expand_ref[...], preferred_element_type=F32)
        kpos = lax.broadcasted_iota(jnp.int32, (1, SEQ), 1)
        sbias_ref[g] = jnp.where((hit > 0.5) & (kpos <= tok), 0.0, NEG)

    for n in range(SEQ // SEL_STEP):
        @pl.when(i // (SEL_STEP // NSA_TQ) == n)
        def _():
            w = (n + 1) * SEL_STEP
            w0 = pl.multiple_of(jnp.maximum(q0 + NSA_TQ - WIN_W, 0), NSA_TQ)
            wpos = w0 + lax.broadcasted_iota(jnp.int32, (1, WIN_W), 1)
            wbias = jnp.where((wpos <= tok) & (wpos > tok - WINDOW), 0.0, NEG)
            r = pl.ds(w0, WIN_W)
            for g in range(NSA_GROUPS):
                lanes = slice(g * NSA_DH, (g + 1) * NSA_DH)
                q = qs_ref[g]
                o_sel = _masked_attention(q, ks_ref[0:w, lanes], vs_ref[0:w, lanes], sbias_ref[g, :, 0:w])
                o_win = _masked_attention(q, kw_ref[r, lanes], vw_ref[r, lanes], wbias)
                o = ocmp_ref[g] + gsel_ref[g] * o_sel + gwin_ref[g] * o_win
                for k in range(NSA_HG):
                    h = g * NSA_HG + k
                    o_ref[:, h * NSA_DH:(h + 1) * NSA_DH] = o[k * NSA_TQ:(k + 1) * NSA_TQ].astype(o_ref.dtype)


def _nsa_constants():
    c0 = np.arange(N_CMP)[:, None] * CMP_STRIDE
    s0 = np.arange(N_SLC)[None, :] * SLC_LEN
    overlap = np.clip(np.minimum(c0 + CMP_LEN, s0 + SLC_LEN) - np.maximum(c0, s0), 0, None) / CMP_STRIDE
    overlap[N_CMP - 1] = 0.0
    expand = (np.arange(SEQ)[None, :] // SLC_LEN == np.arange(N_SLC)[:, None]).astype(np.float32)
    return jnp.asarray(overlap.T, BF16), jnp.asarray(np.eye(NSA_TQ), BF16), jnp.asarray(expand, BF16)


def nsa_attention(p16, cmp_kv, p32, q_col, kv_col, small_col):
    nq = SEQ // NSA_TQ
    ovl_t, eye, expand = _nsa_constants()
    kv_spec = lambda n: pl.BlockSpec((SEQ, NSA_KV), lambda b, i: (b, kv_col // NSA_KV + n))
    return pl.pallas_call(
        _nsa_kernel,
        grid=(BATCH, nq),
        in_specs=[pl.BlockSpec((NSA_TQ, NSA_Q), lambda b, i: (b * nq + i, q_col // NSA_Q)),
                  kv_spec(0), kv_spec(1), kv_spec(2), kv_spec(3),
                  pl.BlockSpec((1, 2 * NSA_GROUPS, N_CMP, NSA_DH), lambda b, i: (b, 0, 0, 0)),
                  pl.BlockSpec((NSA_TQ, LANE), lambda b, i: (b * nq + i, small_col // LANE)),
                  _resident(ovl_t.shape), _resident(eye.shape), _resident(expand.shape)],
        out_specs=pl.BlockSpec((NSA_TQ, NSA_Q), lambda b, i: (b * nq + i, 0)),
        out_shape=jax.ShapeDtypeStruct((TOKENS, NSA_Q), BF16),
        scratch_shapes=[pltpu.VMEM((NSA_GROUPS, NSA_TQ, SEQ), F32),
                        pltpu.VMEM((NSA_GROUPS, NSA_HG * NSA_TQ, NSA_DH), BF16)]
        + [pltpu.VMEM((NSA_GROUPS, NSA_HG * NSA_TQ, NSA_DH), F32)] * 3,
        compiler_params=pltpu.CompilerParams(dimension_semantics=("arbitrary", "arbitrary"),
                                             vmem_limit_bytes=VMEM_LIMIT),
        name="nsa_attention",
    )(p16, p16, p16, p16, p16, cmp_kv, p32, ovl_t, eye, expand)


EVEN_QKV_COL = 0
EVEN_GATE_COL = 2 * GDN_QK + GDN_V
EVEN_CMP_COL = EVEN_GATE_COL + GDN_V
EVEN_SMALL_COL = EVEN_CMP_COL + 2 * NSA_KV
EVEN_F32_W = EVEN_SMALL_COL + LANE
EVEN_Q_COL = 0
EVEN_KV_COL = NSA_Q
EVEN_BF16_W = NSA_Q + 4 * NSA_KV
assert NSA_SMALL_GATE + 3 * NSA_HEADS <= LANE


def even_proj_weight(w_in):
    qkv_a, beta, dec, gate_a, q_b, kv_b, gate_b = split_last(w_in, EVEN_SPLITS)
    pad = jnp.zeros((w_in.shape[0], LANE - 2 * GDN_HEADS - 3 * NSA_HEADS), w_in.dtype)
    return jnp.concatenate([qkv_a, gate_a, kv_b[:, :2 * NSA_KV], beta, dec, gate_b, pad, q_b, kv_b[:, 2 * NSA_KV:]],
                           axis=1).astype(BF16)


def even_mixer_core(p32, p16, conv_w, A_log, dt_bias, o_norm, cmp_pe, cmp_w1, cmp_b1, cmp_w2):
    oa = gated_delta_net(p32, conv_w, A_log, dt_bias, o_norm, EVEN_QKV_COL, EVEN_GATE_COL, EVEN_SMALL_COL)
    strips = p32[:, EVEN_CMP_COL:EVEN_CMP_COL + 2 * NSA_KV].reshape(BATCH, N_CMP, CMP_STRIDE, 2 * NSA_GROUPS, NSA_DH)
    strips = strips.transpose(0, 3, 1, 2, 4).reshape(BATCH, 2 * NSA_GROUPS, N_CMP, CMP_STRIDE * NSA_DH)
    cmp_kv = nsa_compress(strips, cmp_pe, cmp_w1, cmp_b1, cmp_w2)
    ob = nsa_attention(p16, cmp_kv, p32, EVEN_Q_COL, EVEN_KV_COL, EVEN_SMALL_COL)
    return jnp.concatenate([oa, ob], axis=-1)


GLA_TC = 256
ODD_Q_COL = 0
ODD_K_COL = GLA_QK
ODD_V_COL = 2 * GLA_QK
ODD_R_COL = ODD_V_COL + GLA_V
ODD_LR_COL = ODD_R_COL + GLA_V
ODD_F32_W = ODD_LR_COL + LANE


def _gla_kernel(q_ref, k_ref, v_ref, r_ref, lr_ref, aw_ref, ab_ref, onorm_ref, tri_ref, o_ref, state_ref):
    t = pl.program_id(1)
    C = GLA_CHUNK

    @pl.when(t == 0)
    def _():
        state_ref[...] = jnp.zeros(state_ref.shape, F32)

    z = jnp.dot(lr_ref[...].astype(BF16), aw_ref[...], preferred_element_type=F32) + ab_ref[...]
    log_a = (jnp.minimum(z, 0.0) - jnp.log(1.0 + jnp.exp(-jnp.abs(z)))) * (1.0 / GLA_TAU)
    tri = tri_ref[...]
    causal = lax.broadcasted_iota(jnp.int32, (C, C), 0) >= lax.broadcasted_iota(jnp.int32, (C, C), 1)

    states = [state_ref[h] for h in range(GLA_HEADS)]
    for c in range(GLA_TC // C):
        rows = slice(c * C, (c + 1) * C)
        b_all = _cumsum_rows(tri, log_a[rows])
        for h in range(GLA_HEADS):
            kl = slice(h * GLA_DK, (h + 1) * GLA_DK)
            vl = slice(h * GLA_DV, (h + 1) * GLA_DV)
            b = b_all[:, kl]
            b_last = b[C - 1:C]
            q = q_ref[rows, kl] * GLA_DK ** -0.5
            k = k_ref[rows, kl]
            v = v_ref[rows, vl].astype(BF16)
            qt = (q * jnp.exp(b)).astype(BF16)
            kt = (k * jnp.exp(-b)).astype(BF16)
            kh = (k * jnp.exp(b_last - b)).astype(BF16)
            intra = jnp.where(causal, _dot_nt(qt, kt), 0.0).astype(BF16)
            o = jnp.dot(intra, v, preferred_element_type=F32) + _dot_nt(qt, states[h].astype(BF16))
            states[h] = states[h] * jnp.exp(b_last) + _dot_tn(v, kh)
            r = r_ref[rows, vl]
            on = o * lax.rsqrt(jnp.mean(o * o, axis=-1, keepdims=True) + EPS) * onorm_ref[...]
            o_ref[rows, vl] = (on * (r * jax.nn.sigmoid(r))).astype(o_ref.dtype)
    for h in range(GLA_HEADS):
        state_ref[h] = states[h]


def gated_linear_attention(p32, a_w2, a_b, o_norm):
    nt = SEQ // GLA_TC
    tok = lambda blk: (lambda b, t: (b * nt + t, blk))
    aw = jnp.pad(a_w2, ((0, LANE - GLA_RANK), (0, 0))).astype(BF16)
    tri = jnp.asarray(np.tril(np.ones((GLA_CHUNK, GLA_CHUNK))), BF16)
    return pl.pallas_call(
        _gla_kernel,
        grid=(BATCH, nt),
        in_specs=[pl.BlockSpec((GLA_TC, GLA_QK), tok(ODD_Q_COL // GLA_QK)),
                  pl.BlockSpec((GLA_TC, GLA_QK), tok(ODD_K_COL // GLA_QK)),
                  pl.BlockSpec((GLA_TC, GLA_V), tok(ODD_V_COL // GLA_V)),
                  pl.BlockSpec((GLA_TC, GLA_V), tok(ODD_R_COL // GLA_V)),
                  pl.BlockSpec((GLA_TC, LANE), tok(ODD_LR_COL // LANE)),
                  _resident((LANE, GLA_QK)), _resident((1, GLA_QK)), _resident((1, GLA_DV)),
                  _resident((GLA_CHUNK, GLA_CHUNK))],
        out_specs=pl.BlockSpec((GLA_TC, GLA_V), tok(0)),
        out_shape=jax.ShapeDtypeStruct((TOKENS, GLA_V), BF16),
        scratch_shapes=[pltpu.VMEM((GLA_HEADS, GLA_DV, GLA_DK), F32)],
        compiler_params=pltpu.CompilerParams(dimension_semantics=("arbitrary", "arbitrary"),
                                             vmem_limit_bytes=VMEM_LIMIT),
        name="gated_linear_attention",
    )(p32, p32, p32, p32, p32, aw, a_b.reshape(1, GLA_QK), o_norm.reshape(1, GLA_DV), tri)


def _pad_cols(w, n):
    return jnp.pad(w, ((0, 0), (0, n - w.shape[1])))


def kernel(x, norm_gains, ffn_w_in, ffn_conv_w, ffn_conv_b, ffn_w_out, hyb_w_in, hyb_w_out, gdn_conv_w, gdn_A_log,
           gdn_dt_bias, gdn_o_norm, nsa_cmp_pe, nsa_cmp_w1, nsa_cmp_b1, nsa_cmp_w2, gla_w_in, gla_a_w2, gla_a_b,
           gla_o_norm, gla_w_out):
    x2 = x.reshape(TOKENS, D_MODEL)
    for layer in range(DEPTH):
        if layer % 2 == 0:
            e = layer // 2
            p32, p16 = norm_proj(x2, norm_gains[layer, 0], even_proj_weight(hyb_w_in[e]),
                                 ((EVEN_F32_W, F32), (EVEN_BF16_W, BF16)))
            o = even_mixer_core(p32, p16, gdn_conv_w[e], gdn_A_log[e], gdn_dt_bias[e], gdn_o_norm[e], nsa_cmp_pe[e],
                                nsa_cmp_w1[e], nsa_cmp_b1[e], nsa_cmp_w2[e])
            w_out = hyb_w_out[e]
        else:
            o_ = layer // 2
            w = _pad_cols(gla_w_in[o_], ODD_F32_W).astype(BF16)
            (p32,) = norm_proj(x2, norm_gains[layer, 0], w, ((ODD_F32_W, F32),))
            o = gated_linear_attention(p32, gla_a_w2[o_], gla_a_b[o_], gla_o_norm[o_])
            w_out = gla_w_out[o_]
        x2 = out_proj_residual(o.reshape(TOKENS, -1), x2, norm_gains[layer, 1], w_out.astype(BF16))
        x2 = conv_ffn_residual(x2, norm_gains[layer, 2], ffn_w_in[layer].astype(BF16), ffn_conv_w[layer],
                               ffn_conv_b[layer], ffn_w_out[layer].astype(BF16), norm_gains[layer, 3])
    return x2.reshape(BATCH, SEQ, D_MODEL)
```

```python
import functools
import math

import jax
import jax.numpy as jnp
import numpy as np
from jax import lax
from jax.experimental import pallas as pl
from jax.experimental.pallas import tpu as pltpu

D_MODEL = 1024
BATCH = 8
SEQ = 2048
DEPTH = 4
TOKENS = BATCH * SEQ

EPS = 1e-6
NEG = -1e30
FORCE = 1e4

GDN_DK = 128
GDN_DV = 128
GDN_HEADS = 4
GDN_QK = GDN_HEADS * GDN_DK
GDN_V = GDN_HEADS * GDN_DV
GDN_CONV = 4
GDN_CHUNK = 64

NSA_DH = 64
NSA_HEADS = 8
NSA_GROUPS = 2
NSA_Q = NSA_HEADS * NSA_DH
NSA_KV = NSA_GROUPS * NSA_DH
CMP_LEN = 32
CMP_STRIDE = 16
CMP_HIDDEN = 2 * NSA_DH
SLC_LEN = 64
SLC_TOPK = 8
WINDOW = 512
Q_BLOCK = 128

GLA_HEADS = 4
GLA_DK = 128
GLA_DV = 256
GLA_QK = GLA_HEADS * GLA_DK
GLA_V = GLA_HEADS * GLA_DV
GLA_RANK = 16
GLA_TAU = 16.0
GLA_CHUNK = 64

D_FF = 2816
FFN_CONV = 3

EVEN_SPLITS = (2 * GDN_QK + GDN_V, GDN_HEADS, GDN_HEADS, GDN_V, NSA_Q, 6 * NSA_KV, 3 * NSA_HEADS)
ODD_SPLITS = (GLA_QK, GLA_QK, GLA_V, GLA_V, GLA_RANK)

LANE = 128
VMEM_LIMIT = 56 * 1024 * 1024

F32 = jnp.float32
BF16 = jnp.bfloat16


def _round_up(n, m):
    return (n + m - 1) // m * m


def _resident(shape):
    nd = len(shape)
    return pl.BlockSpec(shape, lambda *_: (0,) * nd, pipeline_mode=pl.Buffered(1))


def _rms(x, g):
    ms = jnp.mean(x * x, axis=-1, keepdims=True)
    return x * lax.rsqrt(ms + EPS) * g


PROJ_TM = 256
PROJ_CH = 512


def _proj_kernel(x_ref, g_ref, w_ref, *o_refs):
    h = _rms(x_ref[...], g_ref[...]).astype(BF16)
    col = 0
    for o_ref in o_refs:
        n = o_ref.shape[1]
        for c0 in range(0, n, PROJ_CH):
            cw = min(PROJ_CH, n - c0)
            y = jnp.dot(h, w_ref[:, col + c0:col + c0 + cw], preferred_element_type=F32)
            o_ref[:, c0:c0 + cw] = y.astype(o_ref.dtype)
        col += n


def norm_proj(x2, gain, w, sections):
    T, D = x2.shape
    N = w.shape[1]
    assert N == sum(n for n, _ in sections) and all(n % LANE == 0 for n, _ in sections)
    return pl.pallas_call(
        _proj_kernel,
        grid=(T // PROJ_TM,),
        in_specs=[pl.BlockSpec((PROJ_TM, D), lambda i: (i, 0)), _resident((1, D)), _resident((D, N))],
        out_specs=[pl.BlockSpec((PROJ_TM, n), lambda i: (i, 0)) for n, _ in sections],
        out_shape=[jax.ShapeDtypeStruct((T, n), dt) for n, dt in sections],
        compiler_params=pltpu.CompilerParams(dimension_semantics=("arbitrary",), vmem_limit_bytes=VMEM_LIMIT),
        name="norm_proj",
    )(x2, gain.reshape(1, D), w)


FFN_TM = 512
FFN_HALO = 16
FFN_CH = 256
MIX_HALF = D_MODEL // 2
assert D_FF % FFN_CH == 0


def _ffn_kernel(oa_ref, ob_ref, x_ref, g_mix_ref, w_mix_ref, g_in_ref, w_in_ref, cw_ref, cb_ref, w_out_ref, g_out_ref,
                y_ref, hext_ref, act_ref):
    t = pl.program_id(1)

    @pl.when(t == 0)
    def _():
        hext_ref[0:FFN_HALO, :] = jnp.zeros((FFN_HALO, D_MODEL), BF16)

    mix = (jnp.dot(oa_ref[...], w_mix_ref[0:MIX_HALF, :], preferred_element_type=F32)
           + jnp.dot(ob_ref[...], w_mix_ref[MIX_HALF:, :], preferred_element_type=F32))
    x = x_ref[...] + _rms(mix, g_mix_ref[...])
    hext_ref[FFN_HALO:, :] = _rms(x, g_in_ref[...]).astype(BF16)
    hext = hext_ref[...]

    def conv_half(c0):
        u = jnp.dot(hext, w_in_ref[:, c0:c0 + FFN_CH], preferred_element_type=F32)
        u1 = pltpu.roll(u, 1, 0)
        u2 = pltpu.roll(u, 2, 0)
        w = cw_ref[:, c0:c0 + FFN_CH]
        y = u2 * w[0:1] + u1 * w[1:2] + u * w[2:3] + cb_ref[:, c0:c0 + FFN_CH]
        return y[FFN_HALO:]

    for c0 in range(0, D_FF, FFN_CH):
        half_gate = 0.5 * conv_half(c0)
        val = conv_half(D_FF + c0)
        silu = half_gate + half_gate * jnp.tanh(half_gate)
        act_ref[:, c0:c0 + FFN_CH] = (silu * val).astype(BF16)
    down = jnp.dot(act_ref[...], w_out_ref[...], preferred_element_type=F32)
    y_ref[...] = x + _rms(down, g_out_ref[...])
    hext_ref[0:FFN_HALO, :] = hext_ref[FFN_TM:FFN_TM + FFN_HALO, :]


def mix_out_conv_ffn(oa, a_blk, ob, b_blk, x2, g_mix, w_mix, g_in, w_in, conv_w, conv_b, w_out, g_out):
    T, D = x2.shape
    nt = SEQ // FFN_TM
    tok = lambda blk: (lambda b, t: (b * nt + t, blk))
    return pl.pallas_call(
        _ffn_kernel,
        grid=(BATCH, nt),
        in_specs=[pl.BlockSpec((FFN_TM, MIX_HALF), tok(a_blk)), pl.BlockSpec((FFN_TM, MIX_HALF), tok(b_blk)),
                  pl.BlockSpec((FFN_TM, D), tok(0)), _resident((1, D)), _resident((D, D)),
                  _resident((1, D)), _resident((D, 2 * D_FF)),
                  _resident((FFN_CONV, 2 * D_FF)), _resident((1, 2 * D_FF)), _resident((D_FF, D)), _resident((1, D))],
        out_specs=pl.BlockSpec((FFN_TM, D), tok(0)),
        out_shape=jax.ShapeDtypeStruct((T, D), F32),
        scratch_shapes=[pltpu.VMEM((FFN_HALO + FFN_TM, D), BF16), pltpu.VMEM((FFN_TM, D_FF), BF16)],
        compiler_params=pltpu.CompilerParams(dimension_semantics=("arbitrary", "arbitrary"),
                                             vmem_limit_bytes=VMEM_LIMIT),
        name="mix_out_conv_ffn",
    )(oa, ob, x2, g_mix.reshape(1, D), w_mix, g_in.reshape(1, D), w_in, conv_w, conv_b.reshape(1, -1), w_out,
      g_out.reshape(1, D))


def split_last(x, sizes):
    idx = np.cumsum(sizes)[:-1].tolist()
    return jnp.split(x, idx, axis=-1)


def _dot_nt(a, b):
    return lax.dot_general(a, b, (((1,), (1,)), ((), ())), preferred_element_type=F32)


def _dot_tn(a, b):
    return lax.dot_general(a, b, (((0,), (0,)), ((), ())), preferred_element_type=F32)


def _split3(a):
    hi = a.astype(BF16)
    r = a - hi.astype(F32)
    mid = r.astype(BF16)
    return hi, mid, (r - mid.astype(F32)).astype(BF16)


def _dot_x3(a, b):
    a_hi, a_lo, _ = _split3(a)
    b_hi, b_lo, _ = _split3(b)
    d = lambda p, q: jnp.dot(p, q, preferred_element_type=F32)
    return d(a_hi, b_hi) + (d(a_lo, b_hi) + d(a_hi, b_lo))


def _cumsum_rows(tri, g):
    return sum(jnp.dot(tri, t, preferred_element_type=F32) for t in reversed(_split3(g)))


def _unit_lower_inverses(mats):
    c = mats[0].shape[0]
    eye = jnp.where(lax.broadcasted_iota(jnp.int32, (c, c), 0) == lax.broadcasted_iota(jnp.int32, (c, c), 1), 1.0, 0.0)
    ps = [-a for a in mats]
    ts = [eye + p for p in ps]
    k = 2
    while k < c:
        ps = [_dot_x3(p, p) for p in ps]
        ts = [t + _dot_x3(t, p) for t, p in zip(ts, ps)]
        k *= 2
    return ts


GDN_TC = 256
GDN_HALO = 8
GDN_SMALL_BETA = 0
GDN_SMALL_DECAY = GDN_HEADS
NSA_SMALL_GATE = 2 * GDN_HEADS


def _gdn_kernel(x_ref, gate_ref, small_ref, cw_ref, alog_ref, dtb_ref, onorm_ref, tri_ref, o_ref, xext_ref, state_ref):
    t = pl.program_id(1)
    C = GDN_CHUNK

    @pl.when(t == 0)
    def _():
        xext_ref[0:GDN_HALO, :] = jnp.zeros((GDN_HALO, xext_ref.shape[1]), F32)
        state_ref[...] = jnp.zeros(state_ref.shape, F32)

    xext_ref[GDN_HALO:, :] = x_ref[...]
    small = small_ref[...]
    beta_all = jax.nn.sigmoid(small)
    z = small + dtb_ref[...]
    softplus = jnp.maximum(z, 0.0) + jnp.log(1.0 + jnp.exp(-jnp.abs(z)))
    g_all = -jnp.exp(alog_ref[...]) * softplus
    tri = tri_ref[...]
    row = lax.broadcasted_iota(jnp.int32, (C, C), 0)
    col = lax.broadcasted_iota(jnp.int32, (C, C), 1)

    def conv_silu(c0, width):
        xe = xext_ref[:, c0:c0 + width]
        w = cw_ref[:, c0:c0 + width]
        y = xe * w[GDN_CONV - 1:GDN_CONV]
        for s in range(1, GDN_CONV):
            y = y + pltpu.roll(xe, s, 0) * w[GDN_CONV - 1 - s:GDN_CONV - s]
        y = y[GDN_HALO:]
        return y * jax.nn.sigmoid(y)

    n_chunks = GDN_TC // C
    gsums = [_cumsum_rows(tri, g_all[c * C:(c + 1) * C]) for c in range(n_chunks)]
    gsum_ts = [jnp.transpose(gs) for gs in gsums]

    work = {}
    for h in range(GDN_HEADS):
        q_all = conv_silu(h * GDN_DK, GDN_DK)
        k_all = conv_silu(GDN_QK + h * GDN_DK, GDN_DK)
        v_all = conv_silu(2 * GDN_QK + h * GDN_DV, GDN_DV)
        q_all = q_all * (lax.rsqrt(jnp.sum(q_all * q_all, axis=-1, keepdims=True) + EPS) * GDN_DK ** -0.5)
        k_all = k_all * lax.rsqrt(jnp.sum(k_all * k_all, axis=-1, keepdims=True) + EPS)
        for c in range(n_chunks):
            rows = slice(c * C, (c + 1) * C)
            q, k, v = q_all[rows], k_all[rows], v_all[rows]
            beta = beta_all[rows, GDN_SMALL_BETA + h:GDN_SMALL_BETA + h + 1]
            gcol = gsums[c][:, GDN_SMALL_DECAY + h:GDN_SMALL_DECAY + h + 1]
            grow = gsum_ts[c][GDN_SMALL_DECAY + h:GDN_SMALL_DECAY + h + 1, :]
            glast = gcol[C - 1:C]
            decay = jnp.exp(jnp.where(row >= col, gcol - grow, -jnp.inf))
            kb = k * beta
            kbf = k.astype(BF16)
            eg = jnp.exp(gcol)
            work[h, c] = dict(
                a_kk=jnp.where(row > col, _dot_nt(kb.astype(BF16), kbf) * decay, 0.0),
                a_qk=(_dot_nt(q.astype(BF16), kbf) * decay).astype(BF16),
                rhs=jnp.concatenate([v * beta, kb * eg], axis=-1),
                qg=(q * eg).astype(BF16),
                kd=(k * jnp.exp(glast - gcol)).astype(BF16),
                keep=jnp.exp(glast))
    keys = sorted(work)
    inverses = _unit_lower_inverses([work[key]["a_kk"] for key in keys])
    for key, inv in zip(keys, inverses):
        work[key]["sol"] = _dot_x3(inv, work[key]["rhs"])

    states = [state_ref[h] for h in range(GDN_HEADS)]
    for c in range(n_chunks):
        rows = slice(c * C, (c + 1) * C)
        for h in range(GDN_HEADS):
            wk = work[h, c]
            u, w = wk["sol"][:, :GDN_DV], wk["sol"][:, GDN_DV:]
            sb = states[h].astype(BF16)
            v_new = u - jnp.dot(w.astype(BF16), sb, preferred_element_type=F32)
            vb = v_new.astype(BF16)
            o = (jnp.dot(wk["qg"], sb, preferred_element_type=F32)
                 + jnp.dot(wk["a_qk"], vb, preferred_element_type=F32))
            states[h] = states[h] * wk["keep"] + _dot_tn(wk["kd"], vb)
            gate = gate_ref[rows, h * GDN_DV:(h + 1) * GDN_DV]
            on = o * lax.rsqrt(jnp.mean(o * o, axis=-1, keepdims=True) + EPS) * onorm_ref[...]
            o_ref[rows, h * GDN_DV:(h + 1) * GDN_DV] = (on * (gate * jax.nn.sigmoid(gate))).astype(o_ref.dtype)
    for h in range(GDN_HEADS):
        state_ref[h] = states[h]
    xext_ref[0:GDN_HALO, :] = xext_ref[GDN_TC:GDN_TC + GDN_HALO, :]


def gated_delta_net(p32, conv_w, a_log, dt_bias, o_norm, qkv_col, gate_col, small_col):
    nt = SEQ // GDN_TC
    width = 2 * GDN_QK + GDN_V
    lane_vec = lambda a, off: jnp.zeros((1, LANE), F32).at[0, off:off + GDN_HEADS].set(a.astype(F32))
    tri = jnp.asarray(np.tril(np.ones((GDN_CHUNK, GDN_CHUNK))), BF16)
    tok = lambda blk: (lambda b, t: (b * nt + t, blk))
    return pl.pallas_call(
        _gdn_kernel,
        grid=(BATCH, nt),
        in_specs=[pl.BlockSpec((GDN_TC, width), tok(qkv_col // width)),
                  pl.BlockSpec((GDN_TC, GDN_V), tok(gate_col // GDN_V)),
                  pl.BlockSpec((GDN_TC, LANE), tok(small_col // LANE)),
                  _resident((GDN_CONV, width)), _resident((1, LANE)), _resident((1, LANE)), _resident((1, GDN_DV)),
                  _resident((GDN_CHUNK, GDN_CHUNK))],
        out_specs=pl.BlockSpec((GDN_TC, GDN_V), tok(0)),
        out_shape=jax.ShapeDtypeStruct((TOKENS, GDN_V), BF16),
        scratch_shapes=[pltpu.VMEM((GDN_HALO + GDN_TC, width), F32), pltpu.VMEM((GDN_HEADS, GDN_DK, GDN_DV), F32)],
        compiler_params=pltpu.CompilerParams(dimension_semantics=("arbitrary", "arbitrary"),
                                             vmem_limit_bytes=VMEM_LIMIT),
        name="gated_delta_net",
    )(p32, p32, p32, conv_w, lane_vec(a_log, GDN_SMALL_DECAY), lane_vec(dt_bias, GDN_SMALL_DECAY),
      o_norm.reshape(1, GDN_DV), tri)


N_SLC = SEQ // SLC_LEN
N_CMP = SEQ // CMP_STRIDE
NSA_HG = NSA_HEADS // NSA_GROUPS
NSA_TQ = 128
SEL_STEP = 512
WIN_W = WINDOW + NSA_TQ
NSA_SCALE = NSA_DH ** -0.5


def _cmp_kernel(p_ref, pe_ref, w1_ref, b1_ref, w2_ref, o_ref):
    half = CMP_STRIDE * NSA_DH
    for kv in range(2):
        for g in range(NSA_GROUPS):
            rows = p_ref[0, kv * NSA_GROUPS + g]
            lo = (rows + pe_ref[kv, :, :half]).astype(BF16)
            hi = (rows + pe_ref[kv, :, half:]).astype(BF16)
            a = jnp.dot(lo, w1_ref[kv, :half, :], preferred_element_type=F32)
            b = jnp.dot(hi, w1_ref[kv, half:, :], preferred_element_type=F32)
            hid = a + pltpu.roll(b, N_CMP - 1, 0) + b1_ref[kv]
            act = (hid * jax.nn.sigmoid(hid)).astype(BF16)
            o_ref[0, kv * NSA_GROUPS + g] = jnp.dot(act, w2_ref[kv], preferred_element_type=F32).astype(BF16)


def nsa_compress(strips, pe, w1, b1, w2):
    n = 2 * NSA_GROUPS
    return pl.pallas_call(
        _cmp_kernel,
        grid=(BATCH,),
        in_specs=[pl.BlockSpec((1, n, N_CMP, CMP_STRIDE * NSA_DH), lambda b: (b, 0, 0, 0)),
                  _resident((2, 1, CMP_LEN * NSA_DH)), _resident((2, CMP_LEN * NSA_DH, CMP_HIDDEN)),
                  _resident((2, 1, CMP_HIDDEN)), _resident((2, CMP_HIDDEN, NSA_DH))],
        out_specs=pl.BlockSpec((1, n, N_CMP, NSA_DH), lambda b: (b, 0, 0, 0)),
        out_shape=jax.ShapeDtypeStruct((BATCH, n, N_CMP, NSA_DH), BF16),
        compiler_params=pltpu.CompilerParams(dimension_semantics=("arbitrary",), vmem_limit_bytes=VMEM_LIMIT),
        name="nsa_compress",
    )(strips, pe.reshape(2, 1, -1), w1.astype(BF16), b1.reshape(2, 1, -1), w2.astype(BF16))


def _masked_attention(q, k, v, bias):
    w = k.shape[0]
    s = _dot_nt(q, k).reshape(NSA_HG, NSA_TQ, w) + bias[None]
    s = s.reshape(NSA_HG * NSA_TQ, w)
    p = jnp.exp(s - jnp.max(s, axis=-1, keepdims=True))
    l = jnp.sum(p, axis=-1, keepdims=True)
    return jnp.dot(p.astype(BF16), v, preferred_element_type=F32) / l


def _nsa_kernel(q_ref, ks_ref, vs_ref, kw_ref, vw_ref, cmp_ref, gate_ref, ovl_ref, eye_ref, expand_ref,
                o_ref, sbias_ref, qs_ref, ocmp_ref, gsel_ref, gwin_ref):
    i = pl.program_id(1)
    q0 = i * NSA_TQ
    rows = NSA_HG * NSA_TQ
    gates = jax.nn.sigmoid(gate_ref[...])
    tok = q0 + lax.broadcasted_iota(jnp.int32, (NSA_TQ, 1), 0)
    tok_l = q0 + lax.broadcasted_iota(jnp.int32, (1, NSA_TQ), 1)

    for g in range(NSA_GROUPS):
        heads = [g * NSA_HG + h for h in range(NSA_HG)]
        q = jnp.concatenate([q_ref[:, h * NSA_DH:(h + 1) * NSA_DH] for h in heads], axis=0)
        q = q * jnp.asarray(NSA_SCALE, BF16)
        qs_ref[g] = q

        s = _dot_nt(q, cmp_ref[0, g]).reshape(NSA_HG, NSA_TQ, N_CMP)
        cmp_end = lax.broadcasted_iota(jnp.int32, (1, N_CMP), 1) * CMP_STRIDE + (CMP_LEN - 1)
        cvalid = (cmp_end <= tok)[None]
        sm = jnp.where(cvalid, s, NEG)
        e = jnp.exp(sm - jnp.max(sm, axis=-1, keepdims=True))
        p = jnp.where(cvalid, e / jnp.sum(e, axis=-1, keepdims=True), 0.0)
        o_cmp = jnp.dot(p.reshape(rows, N_CMP).astype(BF16), cmp_ref[0, NSA_GROUPS + g], preferred_element_type=F32)

        def gate(branch):
            lane = NSA_SMALL_GATE + branch * NSA_HEADS
            return jnp.concatenate([gates[:, lane + h:lane + h + 1] for h in heads], axis=0)

        ocmp_ref[g] = gate(0) * o_cmp
        gsel_ref[g] = jnp.broadcast_to(gate(1), (rows, NSA_DH))
        gwin_ref[g] = jnp.broadcast_to(gate(2), (rows, NSA_DH))

        imp = _dot_nt(ovl_ref[...], jnp.sum(p, axis=0).astype(BF16))
        blk = lax.broadcasted_iota(jnp.int32, (N_SLC, NSA_TQ), 0)
        cur = tok_l // SLC_LEN
        bvalid = blk <= cur
        forced = (blk == 0) | (blk == cur) | (blk == cur - 1)
        score = jnp.where(bvalid, imp + jnp.where(forced, FORCE, 0.0), NEG)
        rank = jnp.zeros((N_SLC, NSA_TQ), F32)
        for j in range(N_SLC):
            sj = score[j:j + 1, :]
            tie = jnp.where((sj == score) & (blk > j), 1.0, 0.0)
            rank = rank + jnp.where(sj > score, 1.0, tie)
        sel_t = jnp.where((rank < SLC_TOPK) & bvalid, 1.0, 0.0).astype(BF16)
        sel = _dot_nt(eye_ref[...], sel_t).astype(BF16)
        hit = jnp.dot(sel, expand_ref[...], preferred_element_type=F32)
        kpos = lax.broadcasted_iota(jnp.int32, (1, SEQ), 1)
        sbias_ref[g] = jnp.where((hit > 0.5) & (kpos <= tok), 0.0, NEG)

    for n in range(SEQ // SEL_STEP):
        @pl.when(i // (SEL_STEP // NSA_TQ) == n)
        def _():
            w = (n + 1) * SEL_STEP
            w0 = pl.multiple_of(jnp.maximum(q0 + NSA_TQ - WIN_W, 0), NSA_TQ)
            wpos = w0 + lax.broadcasted_iota(jnp.int32, (1, WIN_W), 1)
            wbias = jnp.where((wpos <= tok) & (wpos > tok - WINDOW), 0.0, NEG)
            r = pl.ds(w0, WIN_W)
            for g in range(NSA_GROUPS):
                lanes = slice(g * NSA_DH, (g + 1) * NSA_DH)
                q = qs_ref[g]
                o_sel = _masked_attention(q, ks_ref[0:w, lanes], vs_ref[0:w, lanes], sbias_ref[g, :, 0:w])
                o_win = _masked_attention(q, kw_ref[r, lanes], vw_ref[r, lanes], wbias)
                o = ocmp_ref[g] + gsel_ref[g] * o_sel + gwin_ref[g] * o_win
                for k in range(NSA_HG):
                    h = g * NSA_HG + k
                    o_ref[:, h * NSA_DH:(h + 1) * NSA_DH] = o[k * NSA_TQ:(k + 1) * NSA_TQ].astype(o_ref.dtype)


def _nsa_constants():
    c0 = np.arange(N_CMP)[:, None] * CMP_STRIDE
    s0 = np.arange(N_SLC)[None, :] * SLC_LEN
    overlap = np.clip(np.minimum(c0 + CMP_LEN, s0 + SLC_LEN) - np.maximum(c0, s0), 0, None) / CMP_STRIDE
    overlap[N_CMP - 1] = 0.0
    expand = (np.arange(SEQ)[None, :] // SLC_LEN == np.arange(N_SLC)[:, None]).astype(np.float32)
    return jnp.asarray(overlap.T, BF16), jnp.asarray(np.eye(NSA_TQ), BF16), jnp.asarray(expand, BF16)


def nsa_attention(p16, cmp_kv, p32, q_col, kv_col, small_col):
    nq = SEQ // NSA_TQ
    ovl_t, eye, expand = _nsa_constants()
    kv_spec = lambda n: pl.BlockSpec((SEQ, NSA_KV), lambda b, i: (b, kv_col // NSA_KV + n))
    return pl.pallas_call(
        _nsa_kernel,
        grid=(BATCH, nq),
        in_specs=[pl.BlockSpec((NSA_TQ, NSA_Q), lambda b, i: (b * nq + i, q_col // NSA_Q)),
                  kv_spec(0), kv_spec(1), kv_spec(2), kv_spec(3),
                  pl.BlockSpec((1, 2 * NSA_GROUPS, N_CMP, NSA_DH), lambda b, i: (b, 0, 0, 0)),
                  pl.BlockSpec((NSA_TQ, LANE), lambda b, i: (b * nq + i, small_col // LANE)),
                  _resident(ovl_t.shape), _resident(eye.shape), _resident(expand.shape)],
        out_specs=pl.BlockSpec((NSA_TQ, NSA_Q), lambda b, i: (b * nq + i, 0)),
        out_shape=jax.ShapeDtypeStruct((TOKENS, NSA_Q), BF16),
        scratch_shapes=[pltpu.VMEM((NSA_GROUPS, NSA_TQ, SEQ), F32),
                        pltpu.VMEM((NSA_GROUPS, NSA_HG * NSA_TQ, NSA_DH), BF16)]
        + [pltpu.VMEM((NSA_GROUPS, NSA_HG * NSA_TQ, NSA_DH), F32)] * 3,
        compiler_params=pltpu.CompilerParams(dimension_semantics=("arbitrary", "arbitrary"),
                                             vmem_limit_bytes=VMEM_LIMIT),
        name="nsa_attention",
    )(p16, p16, p16, p16, p16, cmp_kv, p32, ovl_t, eye, expand)


EVEN_QKV_COL = 0
EVEN_GATE_COL = 2 * GDN_QK + GDN_V
EVEN_CMP_COL = EVEN_GATE_COL + GDN_V
EVEN_SMALL_COL = EVEN_CMP_COL + 2 * NSA_KV
EVEN_F32_W = EVEN_SMALL_COL + LANE
EVEN_Q_COL = 0
EVEN_KV_COL = NSA_Q
EVEN_BF16_W = NSA_Q + 4 * NSA_KV
assert NSA_SMALL_GATE + 3 * NSA_HEADS <= LANE


def even_proj_weight(w_in):
    qkv_a, beta, dec, gate_a, q_b, kv_b, gate_b = split_last(w_in, EVEN_SPLITS)
    pad = jnp.zeros((w_in.shape[0], LANE - 2 * GDN_HEADS - 3 * NSA_HEADS), w_in.dtype)
    return jnp.concatenate([qkv_a, gate_a, kv_b[:, :2 * NSA_KV], beta, dec, gate_b, pad, q_b, kv_b[:, 2 * NSA_KV:]],
                           axis=1).astype(BF16)


def even_mixer_core(p32, p16, conv_w, A_log, dt_bias, o_norm, cmp_pe, cmp_w1, cmp_b1, cmp_w2):
    oa = gated_delta_net(p32, conv_w, A_log, dt_bias, o_norm, EVEN_QKV_COL, EVEN_GATE_COL, EVEN_SMALL_COL)
    strips = p32[:, EVEN_CMP_COL:EVEN_CMP_COL + 2 * NSA_KV].reshape(BATCH, N_CMP, CMP_STRIDE, 2 * NSA_GROUPS, NSA_DH)
    strips = strips.transpose(0, 3, 1, 2, 4).reshape(BATCH, 2 * NSA_GROUPS, N_CMP, CMP_STRIDE * NSA_DH)
    cmp_kv = nsa_compress(strips, cmp_pe, cmp_w1, cmp_b1, cmp_w2)
    ob = nsa_attention(p16, cmp_kv, p32, EVEN_Q_COL, EVEN_KV_COL, EVEN_SMALL_COL)
    return oa, ob


GLA_TC = 256
ODD_Q_COL = 0
ODD_K_COL = GLA_QK
ODD_V_COL = 2 * GLA_QK
ODD_R_COL = ODD_V_COL + GLA_V
ODD_LR_COL = ODD_R_COL + GLA_V
ODD_F32_W = ODD_LR_COL + LANE


def _gla_kernel(q_ref, k_ref, v_ref, r_ref, lr_ref, aw_ref, ab_ref, onorm_ref, tri_ref, o_ref, state_ref):
    t = pl.program_id(1)
    C = GLA_CHUNK

    @pl.when(t == 0)
    def _():
        state_ref[...] = jnp.zeros(state_ref.shape, F32)

    z = jnp.dot(lr_ref[...].astype(BF16), aw_ref[...], preferred_element_type=F32) + ab_ref[...]
    log_a = (jnp.minimum(z, 0.0) - jnp.log(1.0 + jnp.exp(-jnp.abs(z)))) * (1.0 / GLA_TAU)
    tri = tri_ref[...]
    causal = lax.broadcasted_iota(jnp.int32, (C, C), 0) >= lax.broadcasted_iota(jnp.int32, (C, C), 1)

    states = [state_ref[h] for h in range(GLA_HEADS)]
    for c in range(GLA_TC // C):
        rows = slice(c * C, (c + 1) * C)
        b_all = _cumsum_rows(tri, log_a[rows])
        for h in range(GLA_HEADS):
            kl = slice(h * GLA_DK, (h + 1) * GLA_DK)
            vl = slice(h * GLA_DV, (h + 1) * GLA_DV)
            b = b_all[:, kl]
            b_last = b[C - 1:C]
            q = q_ref[rows, kl] * GLA_DK ** -0.5
            k = k_ref[rows, kl]
            v = v_ref[rows, vl].astype(BF16)
            qt = (q * jnp.exp(b)).astype(BF16)
            kt = (k * jnp.exp(-b)).astype(BF16)
            kh = (k * jnp.exp(b_last - b)).astype(BF16)
            intra = jnp.where(causal, _dot_nt(qt, kt), 0.0).astype(BF16)
            o = jnp.dot(intra, v, preferred_element_type=F32) + _dot_nt(qt, states[h].astype(BF16))
            states[h] = states[h] * jnp.exp(b_last) + _dot_tn(v, kh)
            r = r_ref[rows, vl]
            on = o * lax.rsqrt(jnp.mean(o * o, axis=-1, keepdims=True) + EPS) * onorm_ref[...]
            o_ref[rows, vl] = (on * (r * jax.nn.sigmoid(r))).astype(o_ref.dtype)
    for h in range(GLA_HEADS):
        state_ref[h] = states[h]


def gated_linear_attention(p32, a_w2, a_b, o_norm):
    nt = SEQ // GLA_TC
    tok = lambda blk: (lambda b, t: (b * nt + t, blk))
    aw = jnp.pad(a_w2, ((0, LANE - GLA_RANK), (0, 0))).astype(BF16)
    tri = jnp.asarray(np.tril(np.ones((GLA_CHUNK, GLA_CHUNK))), BF16)
    return pl.pallas_call(
        _gla_kernel,
        grid=(BATCH, nt),
        in_specs=[pl.BlockSpec((GLA_TC, GLA_QK), tok(ODD_Q_COL // GLA_QK)),
                  pl.BlockSpec((GLA_TC, GLA_QK), tok(ODD_K_COL // GLA_QK)),
                  pl.BlockSpec((GLA_TC, GLA_V), tok(ODD_V_COL // GLA_V)),
                  pl.BlockSpec((GLA_TC, GLA_V), tok(ODD_R_COL // GLA_V)),
                  pl.BlockSpec((GLA_TC, LANE), tok(ODD_LR_COL // LANE)),
                  _resident((LANE, GLA_QK)), _resident((1, GLA_QK)), _resident((1, GLA_DV)),
                  _resident((GLA_CHUNK, GLA_CHUNK))],
        out_specs=pl.BlockSpec((GLA_TC, GLA_V), tok(0)),
        out_shape=jax.ShapeDtypeStruct((TOKENS, GLA_V), BF16),
        scratch_shapes=[pltpu.VMEM((GLA_HEADS, GLA_DV, GLA_DK), F32)],
        compiler_params=pltpu.CompilerParams(dimension_semantics=("arbitrary", "arbitrary"),
                                             vmem_limit_bytes=VMEM_LIMIT),
        name="gated_linear_attention",
    )(p32, p32, p32, p32, p32, aw, a_b.reshape(1, GLA_QK), o_norm.reshape(1, GLA_DV), tri)


def _pad_cols(w, n):
    return jnp.pad(w, ((0, 0), (0, n - w.shape[1])))


def kernel(x, norm_gains, ffn_w_in, ffn_conv_w, ffn_conv_b, ffn_w_out, hyb_w_in, hyb_w_out, gdn_conv_w, gdn_A_log,
           gdn_dt_bias, gdn_o_norm, nsa_cmp_pe, nsa_cmp_w1, nsa_cmp_b1, nsa_cmp_w2, gla_w_in, gla_a_w2, gla_a_b,
           gla_o_norm, gla_w_out):
    x2 = x.reshape(TOKENS, D_MODEL)
    for layer in range(DEPTH):
        if layer % 2 == 0:
            e = layer // 2
            p32, p16 = norm_proj(x2, norm_gains[layer, 0], even_proj_weight(hyb_w_in[e]),
                                 ((EVEN_F32_W, F32), (EVEN_BF16_W, BF16)))
            oa, ob = even_mixer_core(p32, p16, gdn_conv_w[e], gdn_A_log[e], gdn_dt_bias[e], gdn_o_norm[e],
                                     nsa_cmp_pe[e], nsa_cmp_w1[e], nsa_cmp_b1[e], nsa_cmp_w2[e])
            halves = (oa, 0, ob, 0)
            w_mix = hyb_w_out[e]
        else:
            o_ = layer // 2
            w = _pad_cols(gla_w_in[o_], ODD_F32_W).astype(BF16)
            (p32,) = norm_proj(x2, norm_gains[layer, 0], w, ((ODD_F32_W, F32),))
            o = gated_linear_attention(p32, gla_a_w2[o_], gla_a_b[o_], gla_o_norm[o_])
            halves = (o, 0, o, 1)
            w_mix = gla_w_out[o_]
        x2 = mix_out_conv_ffn(*halves, x2, norm_gains[layer, 1], w_mix.astype(BF16), norm_gains[layer, 2],
                              ffn_w_in[layer].astype(BF16), ffn_conv_w[layer], ffn_conv_b[layer],
                              ffn_w_out[layer].astype(BF16), norm_gains[layer, 3])
    return x2.reshape(BATCH, SEQ, D_MODEL)
```

```python
import functools
import math

import jax
import jax.numpy as jnp
import numpy as np
from jax import lax
from jax.experimental import pallas as pl
from jax.experimental.pallas import tpu as pltpu

D_MODEL = 1024
BATCH = 8
SEQ = 2048
DEPTH = 4
TOKENS = BATCH * SEQ

EPS = 1e-6
NEG = -1e30
FORCE = 1e4

GDN_DK = 128
GDN_DV = 128
GDN_HEADS = 4
GDN_QK = GDN_HEADS * GDN_DK
GDN_V = GDN_HEADS * GDN_DV
GDN_CONV = 4
GDN_CHUNK = 64

NSA_DH = 64
NSA_HEADS = 8
NSA_GROUPS = 2
NSA_Q = NSA_HEADS * NSA_DH
NSA_KV = NSA_GROUPS * NSA_DH
CMP_LEN = 32
CMP_STRIDE = 16
CMP_HIDDEN = 2 * NSA_DH
SLC_LEN = 64
SLC_TOPK = 8
WINDOW = 512
Q_BLOCK = 128

GLA_HEADS = 4
GLA_DK = 128
GLA_DV = 256
GLA_QK = GLA_HEADS * GLA_DK
GLA_V = GLA_HEADS * GLA_DV
GLA_RANK = 16
GLA_TAU = 16.0
GLA_CHUNK = 64

D_FF = 2816
FFN_CONV = 3

EVEN_SPLITS = (2 * GDN_QK + GDN_V, GDN_HEADS, GDN_HEADS, GDN_V, NSA_Q, 6 * NSA_KV, 3 * NSA_HEADS)
ODD_SPLITS = (GLA_QK, GLA_QK, GLA_V, GLA_V, GLA_RANK)

LANE = 128
VMEM_LIMIT = 56 * 1024 * 1024

F32 = jnp.float32
BF16 = jnp.bfloat16


def _round_up(n, m):
    return (n + m - 1) // m * m


def _resident(shape):
    nd = len(shape)
    return pl.BlockSpec(shape, lambda *_: (0,) * nd, pipeline_mode=pl.Buffered(1))


def _resident_layer(shape, layer):
    nd = len(shape)
    return pl.BlockSpec((None,) + tuple(shape), lambda *_: (layer,) + (0,) * nd, pipeline_mode=pl.Buffered(1))


def _rms(x, g):
    ms = jnp.mean(x * x, axis=-1, keepdims=True)
    return x * lax.rsqrt(ms + EPS) * g


PROJ_TM = 256
PROJ_CH = 512


def _proj_kernel(x_ref, g_ref, w_ref, *o_refs):
    h = _rms(x_ref[...], g_ref[...]).astype(BF16)
    col = 0
    for o_ref in o_refs:
        n = o_ref.shape[1]
        for c0 in range(0, n, PROJ_CH):
            cw = min(PROJ_CH, n - c0)
            y = jnp.dot(h, w_ref[:, col + c0:col + c0 + cw], preferred_element_type=F32)
            o_ref[:, c0:c0 + cw] = y.astype(o_ref.dtype)
        col += n


def norm_proj(x2, gain, w, sections):
    T, D = x2.shape
    N = w.shape[1]
    assert N == sum(n for n, _ in sections) and all(n % LANE == 0 for n, _ in sections)
    return pl.pallas_call(
        _proj_kernel,
        grid=(T // PROJ_TM,),
        in_specs=[pl.BlockSpec((PROJ_TM, D), lambda i: (i, 0)), _resident((1, D)), _resident((D, N))],
        out_specs=[pl.BlockSpec((PROJ_TM, n), lambda i: (i, 0)) for n, _ in sections],
        out_shape=[jax.ShapeDtypeStruct((T, n), dt) for n, dt in sections],
        compiler_params=pltpu.CompilerParams(dimension_semantics=("arbitrary",), vmem_limit_bytes=VMEM_LIMIT),
        name="norm_proj",
    )(x2, gain.reshape(1, D), w)


FFN_TM = 512
FFN_HALO = 16
FFN_CH = 256
MIX_HALF = D_MODEL // 2
assert D_FF % FFN_CH == 0


def _ffn_kernel(oa_ref, ob_ref, x_ref, g_mix_ref, w_mix_ref, g_in_ref, w_in_ref, cw_ref, cb_ref, w_out_ref, g_out_ref,
                y_ref, hext_ref, act_ref):
    t = pl.program_id(1)

    @pl.when(t == 0)
    def _():
        hext_ref[0:FFN_HALO, :] = jnp.zeros((FFN_HALO, D_MODEL), BF16)

    mix = (jnp.dot(oa_ref[...], w_mix_ref[0:MIX_HALF, :], preferred_element_type=F32)
           + jnp.dot(ob_ref[...], w_mix_ref[MIX_HALF:, :], preferred_element_type=F32))
    x = x_ref[...] + _rms(mix, g_mix_ref[...])
    hext_ref[FFN_HALO:, :] = _rms(x, g_in_ref[...]).astype(BF16)
    hext = hext_ref[...]

    def conv_half(c0):
        u = jnp.dot(hext, w_in_ref[:, c0:c0 + FFN_CH], preferred_element_type=F32)
        u1 = pltpu.roll(u, 1, 0)
        u2 = pltpu.roll(u, 2, 0)
        w = cw_ref[:, c0:c0 + FFN_CH]
        y = u2 * w[0:1] + u1 * w[1:2] + u * w[2:3] + cb_ref[:, c0:c0 + FFN_CH]
        return y[FFN_HALO:]

    for c0 in range(0, D_FF, FFN_CH):
        half_gate = 0.5 * conv_half(c0)
        val = conv_half(D_FF + c0)
        silu = half_gate + half_gate * jnp.tanh(half_gate)
        act_ref[:, c0:c0 + FFN_CH] = (silu * val).astype(BF16)
    down = jnp.dot(act_ref[...], w_out_ref[...], preferred_element_type=F32)
    y_ref[...] = x + _rms(down, g_out_ref[...])
    hext_ref[0:FFN_HALO, :] = hext_ref[FFN_TM:FFN_TM + FFN_HALO, :]


def mix_out_conv_ffn(oa, a_blk, ob, b_blk, x2, g_mix, w_mix, g_in, layer, w_in, conv_w, conv_b, w_out, g_out):
    T, D = x2.shape
    nt = SEQ // FFN_TM
    tok = lambda blk: (lambda b, t: (b * nt + t, blk))
    return pl.pallas_call(
        _ffn_kernel,
        grid=(BATCH, nt),
        in_specs=[pl.BlockSpec((FFN_TM, MIX_HALF), tok(a_blk)), pl.BlockSpec((FFN_TM, MIX_HALF), tok(b_blk)),
                  pl.BlockSpec((FFN_TM, D), tok(0)), _resident((1, D)), _resident((D, D)),
                  _resident((1, D)), _resident_layer((D, 2 * D_FF), layer),
                  _resident_layer((FFN_CONV, 2 * D_FF), layer), _resident_layer((1, 2 * D_FF), layer),
                  _resident_layer((D_FF, D), layer), _resident((1, D))],
        out_specs=pl.BlockSpec((FFN_TM, D), tok(0)),
        out_shape=jax.ShapeDtypeStruct((T, D), F32),
        scratch_shapes=[pltpu.VMEM((FFN_HALO + FFN_TM, D), BF16), pltpu.VMEM((FFN_TM, D_FF), BF16)],
        compiler_params=pltpu.CompilerParams(dimension_semantics=("arbitrary", "arbitrary"),
                                             vmem_limit_bytes=VMEM_LIMIT),
        name="mix_out_conv_ffn",
    )(oa, ob, x2, g_mix.reshape(1, D), w_mix, g_in.reshape(1, D), w_in, conv_w, conv_b.reshape(DEPTH, 1, -1), w_out,
      g_out.reshape(1, D))


def split_last(x, sizes):
    idx = np.cumsum(sizes)[:-1].tolist()
    return jnp.split(x, idx, axis=-1)


def _dot_nt(a, b):
    return lax.dot_general(a, b, (((1,), (1,)), ((), ())), preferred_element_type=F32)


def _dot_tn(a, b):
    return lax.dot_general(a, b, (((0,), (0,)), ((), ())), preferred_element_type=F32)


def _split3(a):
    hi = a.astype(BF16)
    r = a - hi.astype(F32)
    mid = r.astype(BF16)
    return hi, mid, (r - mid.astype(F32)).astype(BF16)


def _dot_x3(a, b):
    a_hi, a_lo, _ = _split3(a)
    b_hi, b_lo, _ = _split3(b)
    d = lambda p, q: jnp.dot(p, q, preferred_element_type=F32)
    return d(a_hi, b_hi) + (d(a_lo, b_hi) + d(a_hi, b_lo))


def _cumsum_rows(tri, g):
    return sum(jnp.dot(tri, t, preferred_element_type=F32) for t in reversed(_split3(g)))


def _unit_lower_inverses(mats):
    c = mats[0].shape[0]
    eye = jnp.where(lax.broadcasted_iota(jnp.int32, (c, c), 0) == lax.broadcasted_iota(jnp.int32, (c, c), 1), 1.0, 0.0)
    ps = [-a for a in mats]
    ts = [eye + p for p in ps]
    k = 2
    while k < c:
        ps = [_dot_x3(p, p) for p in ps]
        ts = [t + _dot_x3(t, p) for t, p in zip(ts, ps)]
        k *= 2
    return ts


GDN_TC = 256
GDN_HALO = 8
GDN_SMALL_BETA = 0
GDN_SMALL_DECAY = GDN_HEADS
NSA_SMALL_GATE = 2 * GDN_HEADS


def _gdn_kernel(x_ref, gate_ref, small_ref, cw_ref, alog_ref, dtb_ref, onorm_ref, tri_ref, o_ref, xext_ref, state_ref):
    t = pl.program_id(1)
    C = GDN_CHUNK

    @pl.when(t == 0)
    def _():
        xext_ref[0:GDN_HALO, :] = jnp.zeros((GDN_HALO, xext_ref.shape[1]), F32)
        state_ref[...] = jnp.zeros(state_ref.shape, F32)

    xext_ref[GDN_HALO:, :] = x_ref[...]
    small = small_ref[...]
    beta_all = jax.nn.sigmoid(small)
    z = small + dtb_ref[...]
    softplus = jnp.maximum(z, 0.0) + jnp.log(1.0 + jnp.exp(-jnp.abs(z)))
    g_all = -jnp.exp(alog_ref[...]) * softplus
    tri = tri_ref[...]
    row = lax.broadcasted_iota(jnp.int32, (C, C), 0)
    col = lax.broadcasted_iota(jnp.int32, (C, C), 1)

    def conv_silu(c0, width):
        xe = xext_ref[:, c0:c0 + width]
        w = cw_ref[:, c0:c0 + width]
        y = xe * w[GDN_CONV - 1:GDN_CONV]
        for s in range(1, GDN_CONV):
            y = y + pltpu.roll(xe, s, 0) * w[GDN_CONV - 1 - s:GDN_CONV - s]
        y = y[GDN_HALO:]
        return y * jax.nn.sigmoid(y)

    n_chunks = GDN_TC // C
    gsums = [_cumsum_rows(tri, g_all[c * C:(c + 1) * C]) for c in range(n_chunks)]
    gsum_ts = [jnp.transpose(gs) for gs in gsums]

    work = {}
    for h in range(GDN_HEADS):
        q_all = conv_silu(h * GDN_DK, GDN_DK)
        k_all = conv_silu(GDN_QK + h * GDN_DK, GDN_DK)
        v_all = conv_silu(2 * GDN_QK + h * GDN_DV, GDN_DV)
        q_all = q_all * (lax.rsqrt(jnp.sum(q_all * q_all, axis=-1, keepdims=True) + EPS) * GDN_DK ** -0.5)
        k_all = k_all * lax.rsqrt(jnp.sum(k_all * k_all, axis=-1, keepdims=True) + EPS)
        for c in range(n_chunks):
            rows = slice(c * C, (c + 1) * C)
            q, k, v = q_all[rows], k_all[rows], v_all[rows]
            beta = beta_all[rows, GDN_SMALL_BETA + h:GDN_SMALL_BETA + h + 1]
            gcol = gsums[c][:, GDN_SMALL_DECAY + h:GDN_SMALL_DECAY + h + 1]
            grow = gsum_ts[c][GDN_SMALL_DECAY + h:GDN_SMALL_DECAY + h + 1, :]
            glast = gcol[C - 1:C]
            decay = jnp.exp(jnp.where(row >= col, gcol - grow, -jnp.inf))
            kb = k * beta
            kbf = k.astype(BF16)
            eg = jnp.exp(gcol)
            work[h, c] = dict(
                a_kk=jnp.where(row > col, _dot_nt(kb.astype(BF16), kbf) * decay, 0.0),
                a_qk=(_dot_nt(q.astype(BF16), kbf) * decay).astype(BF16),
                rhs=jnp.concatenate([v * beta, kb * eg], axis=-1),
                qg=(q * eg).astype(BF16),
                kd=(k * jnp.exp(glast - gcol)).astype(BF16),
                keep=jnp.exp(glast))
    keys = sorted(work)
    inverses = _unit_lower_inverses([work[key]["a_kk"] for key in keys])
    for key, inv in zip(keys, inverses):
        work[key]["sol"] = _dot_x3(inv, work[key]["rhs"])

    states = [state_ref[h] for h in range(GDN_HEADS)]
    for c in range(n_chunks):
        rows = slice(c * C, (c + 1) * C)
        for h in range(GDN_HEADS):
            wk = work[h, c]
            u, w = wk["sol"][:, :GDN_DV], wk["sol"][:, GDN_DV:]
            sb = states[h].astype(BF16)
            v_new = u - jnp.dot(w.astype(BF16), sb, preferred_element_type=F32)
            vb = v_new.astype(BF16)
            o = (jnp.dot(wk["qg"], sb, preferred_element_type=F32)
                 + jnp.dot(wk["a_qk"], vb, preferred_element_type=F32))
            states[h] = states[h] * wk["keep"] + _dot_tn(wk["kd"], vb)
            gate = gate_ref[rows, h * GDN_DV:(h + 1) * GDN_DV]
            on = o * lax.rsqrt(jnp.mean(o * o, axis=-1, keepdims=True) + EPS) * onorm_ref[...]
            o_ref[rows, h * GDN_DV:(h + 1) * GDN_DV] = (on * (gate * jax.nn.sigmoid(gate))).astype(o_ref.dtype)
    for h in range(GDN_HEADS):
        state_ref[h] = states[h]
    xext_ref[0:GDN_HALO, :] = xext_ref[GDN_TC:GDN_TC + GDN_HALO, :]


def gated_delta_net(p32, conv_w, a_log, dt_bias, o_norm, qkv_col, gate_col, small_col):
    nt = SEQ // GDN_TC
    width = 2 * GDN_QK + GDN_V
    lane_vec = lambda a, off: jnp.zeros((1, LANE), F32).at[0, off:off + GDN_HEADS].set(a.astype(F32))
    tri = jnp.asarray(np.tril(np.ones((GDN_CHUNK, GDN_CHUNK))), BF16)
    tok = lambda blk: (lambda b, t: (b * nt + t, blk))
    return pl.pallas_call(
        _gdn_kernel,
        grid=(BATCH, nt),
        in_specs=[pl.BlockSpec((GDN_TC, width), tok(qkv_col // width)),
                  pl.BlockSpec((GDN_TC, GDN_V), tok(gate_col // GDN_V)),
                  pl.BlockSpec((GDN_TC, LANE), tok(small_col // LANE)),
                  _resident((GDN_CONV, width)), _resident((1, LANE)), _resident((1, LANE)), _resident((1, GDN_DV)),
                  _resident((GDN_CHUNK, GDN_CHUNK))],
        out_specs=pl.BlockSpec((GDN_TC, GDN_V), tok(0)),
        out_shape=jax.ShapeDtypeStruct((TOKENS, GDN_V), BF16),
        scratch_shapes=[pltpu.VMEM((GDN_HALO + GDN_TC, width), F32), pltpu.VMEM((GDN_HEADS, GDN_DK, GDN_DV), F32)],
        compiler_params=pltpu.CompilerParams(dimension_semantics=("arbitrary", "arbitrary"),
                                             vmem_limit_bytes=VMEM_LIMIT),
        name="gated_delta_net",
    )(p32, p32, p32, conv_w, lane_vec(a_log, GDN_SMALL_DECAY), lane_vec(dt_bias, GDN_SMALL_DECAY),
      o_norm.reshape(1, GDN_DV), tri)


N_SLC = SEQ // SLC_LEN
N_CMP = SEQ // CMP_STRIDE
NSA_HG = NSA_HEADS // NSA_GROUPS
NSA_TQ = 128
SEL_STEP = 512
WIN_W = WINDOW + NSA_TQ
NSA_SCALE = NSA_DH ** -0.5


def _cmp_kernel(kc_ref, vc_ref, pe_ref, w1_ref, b1_ref, w2_ref, o_ref, strip_ref):
    half = CMP_STRIDE * NSA_DH
    for kv, x_ref in enumerate((kc_ref, vc_ref)):
        for p in range(CMP_STRIDE):
            xp = x_ref[pl.ds(p, N_CMP, stride=CMP_STRIDE), :]
            for g in range(NSA_GROUPS):
                strip_ref[kv * NSA_GROUPS + g, :, p * NSA_DH:(p + 1) * NSA_DH] = xp[:, g * NSA_DH:(g + 1) * NSA_DH]
    for kv in range(2):
        for g in range(NSA_GROUPS):
            rows = strip_ref[kv * NSA_GROUPS + g]
            lo = (rows + pe_ref[kv, :, :half]).astype(BF16)
            hi = (rows + pe_ref[kv, :, half:]).astype(BF16)
            a = jnp.dot(lo, w1_ref[kv, :half, :], preferred_element_type=F32)
            b = jnp.dot(hi, w1_ref[kv, half:, :], preferred_element_type=F32)
            hid = a + pltpu.roll(b, N_CMP - 1, 0) + b1_ref[kv]
            act = (hid * jax.nn.sigmoid(hid)).astype(BF16)
            o_ref[0, kv * NSA_GROUPS + g] = jnp.dot(act, w2_ref[kv], preferred_element_type=F32).astype(BF16)


def nsa_compress(p32, cmp_col, pe, w1, b1, w2):
    n = 2 * NSA_GROUPS
    return pl.pallas_call(
        _cmp_kernel,
        grid=(BATCH,),
        in_specs=[pl.BlockSpec((SEQ, NSA_KV), lambda b: (b, cmp_col // NSA_KV)),
                  pl.BlockSpec((SEQ, NSA_KV), lambda b: (b, cmp_col // NSA_KV + 1)),
                  _resident((2, 1, CMP_LEN * NSA_DH)), _resident((2, CMP_LEN * NSA_DH, CMP_HIDDEN)),
                  _resident((2, 1, CMP_HIDDEN)), _resident((2, CMP_HIDDEN, NSA_DH))],
        out_specs=pl.BlockSpec((1, n, N_CMP, NSA_DH), lambda b: (b, 0, 0, 0)),
        out_shape=jax.ShapeDtypeStruct((BATCH, n, N_CMP, NSA_DH), BF16),
        scratch_shapes=[pltpu.VMEM((n, N_CMP, CMP_STRIDE * NSA_DH), F32)],
        compiler_params=pltpu.CompilerParams(dimension_semantics=("arbitrary",), vmem_limit_bytes=VMEM_LIMIT),
        name="nsa_compress",
    )(p32, p32, pe.reshape(2, 1, -1), w1.astype(BF16), b1.reshape(2, 1, -1), w2.astype(BF16))


def _masked_attention(q, k, v, bias):
    w = k.shape[0]
    s = _dot_nt(q, k).reshape(NSA_HG, NSA_TQ, w) + bias[None]
    s = s.reshape(NSA_HG * NSA_TQ, w)
    p = jnp.exp(s - jnp.max(s, axis=-1, keepdims=True))
    l = jnp.sum(p, axis=-1, keepdims=True)
    return jnp.dot(p.astype(BF16), v, preferred_element_type=F32) / l


def _nsa_kernel(q_ref, ks_ref, vs_ref, kw_ref, vw_ref, cmp_ref, gate_ref, ovl_ref, eye_ref, expand_ref,
                o_ref, sbias_ref, qs_ref, ocmp_ref, gsel_ref, gwin_ref):
    i = pl.program_id(1)
    q0 = i * NSA_TQ
    rows = NSA_HG * NSA_TQ
    gates = jax.nn.sigmoid(gate_ref[...])
    tok = q0 + lax.broadcasted_iota(jnp.int32, (NSA_TQ, 1), 0)
    tok_l = q0 + lax.broadcasted_iota(jnp.int32, (1, NSA_TQ), 1)

    for g in range(NSA_GROUPS):
        heads = [g * NSA_HG + h for h in range(NSA_HG)]
        q = jnp.concatenate([q_ref[:, h * NSA_DH:(h + 1) * NSA_DH] for h in heads], axis=0)
        q = q * jnp.asarray(NSA_SCALE, BF16)
        qs_ref[g] = q

        s = _dot_nt(q, cmp_ref[0, g]).reshape(NSA_HG, NSA_TQ, N_CMP)
        cmp_end = lax.broadcasted_iota(jnp.int32, (1, N_CMP), 1) * CMP_STRIDE + (CMP_LEN - 1)
        cvalid = (cmp_end <= tok)[None]
        sm = jnp.where(cvalid, s, NEG)
        e = jnp.exp(sm - jnp.max(sm, axis=-1, keepdims=True))
        p = jnp.where(cvalid, e / jnp.sum(e, axis=-1, keepdims=True), 0.0)
        o_cmp = jnp.dot(p.reshape(rows, N_CMP).astype(BF16), cmp_ref[0, NSA_GROUPS + g], preferred_element_type=F32)

        def gate(branch):
            lane = NSA_SMALL_GATE + branch * NSA_HEADS
            return jnp.concatenate([gates[:, lane + h:lane + h + 1] for h in heads], axis=0)

        ocmp_ref[g] = gate(0) * o_cmp
        gsel_ref[g] = jnp.broadcast_to(gate(1), (rows, NSA_DH))
        gwin_ref[g] = jnp.broadcast_to(gate(2), (rows, NSA_DH))

        imp = _dot_nt(ovl_ref[...], jnp.sum(p, axis=0).astype(BF16))
        blk = lax.broadcasted_iota(jnp.int32, (N_SLC, NSA_TQ), 0)
        cur = tok_l // SLC_LEN
        bvalid = blk <= cur
        forced = (blk == 0) | (blk == cur) | (blk == cur - 1)
        score = jnp.where(bvalid, imp + jnp.where(forced, FORCE, 0.0), NEG)
        rank = jnp.zeros((N_SLC, NSA_TQ), F32)
        for j in range(N_SLC):
            sj = score[j:j + 1, :]
            tie = jnp.where((sj == score) & (blk > j), 1.0, 0.0)
            rank = rank + jnp.where(sj > score, 1.0, tie)
        sel_t = jnp.where((rank < SLC_TOPK) & bvalid, 1.0, 0.0).astype(BF16)
        sel = _dot_nt(eye_ref[...], sel_t).astype(BF16)
        hit = jnp.dot(sel, expand_ref[...], preferred_element_type=F32)
        kpos = lax.broadcasted_iota(jnp.int32, (1, SEQ), 1)
        sbias_ref[g] = jnp.where((hit > 0.5) & (kpos <= tok), 0.0, NEG)

    for n in range(SEQ // SEL_STEP):
        @pl.when(i // (SEL_STEP // NSA_TQ) == n)
        def _():
            w = (n + 1) * SEL_STEP
            w0 = pl.multiple_of(jnp.maximum(q0 + NSA_TQ - WIN_W, 0), NSA_TQ)
            wpos = w0 + lax.broadcasted_iota(jnp.int32, (1, WIN_W), 1)
            wbias = jnp.where((wpos <= tok) & (wpos > tok - WINDOW), 0.0, NEG)
            r = pl.ds(w0, WIN_W)
            for g in range(NSA_GROUPS):
                lanes = slice(g * NSA_DH, (g + 1) * NSA_DH)
                q = qs_ref[g]
                o_sel = _masked_attention(q, ks_ref[0:w, lanes], vs_ref[0:w, lanes], sbias_ref[g, :, 0:w])
                o_win = _masked_attention(q, kw_ref[r, lanes], vw_ref[r, lanes], wbias)
                o = ocmp_ref[g] + gsel_ref[g] * o_sel + gwin_ref[g] * o_win
                for k in range(NSA_HG):
                    h = g * NSA_HG + k
                    o_ref[:, h * NSA_DH:(h + 1) * NSA_DH] = o[k * NSA_TQ:(k + 1) * NSA_TQ].astype(o_ref.dtype)


def _nsa_constants():
    c0 = np.arange(N_CMP)[:, None] * CMP_STRIDE
    s0 = np.arange(N_SLC)[None, :] * SLC_LEN
    overlap = np.clip(np.minimum(c0 + CMP_LEN, s0 + SLC_LEN) - np.maximum(c0, s0), 0, None) / CMP_STRIDE
    overlap[N_CMP - 1] = 0.0
    expand = (np.arange(SEQ)[None, :] // SLC_LEN == np.arange(N_SLC)[:, None]).astype(np.float32)
    return jnp.asarray(overlap.T, BF16), jnp.asarray(np.eye(NSA_TQ), BF16), jnp.asarray(expand, BF16)


def nsa_attention(p16, cmp_kv, p32, q_col, kv_col, small_col):
    nq = SEQ // NSA_TQ
    ovl_t, eye, expand = _nsa_constants()
    kv_spec = lambda n: pl.BlockSpec((SEQ, NSA_KV), lambda b, i: (b, kv_col // NSA_KV + n))
    return pl.pallas_call(
        _nsa_kernel,
        grid=(BATCH, nq),
        in_specs=[pl.BlockSpec((NSA_TQ, NSA_Q), lambda b, i: (b * nq + i, q_col // NSA_Q)),
                  kv_spec(0), kv_spec(1), kv_spec(2), kv_spec(3),
                  pl.BlockSpec((1, 2 * NSA_GROUPS, N_CMP, NSA_DH), lambda b, i: (b, 0, 0, 0)),
                  pl.BlockSpec((NSA_TQ, LANE), lambda b, i: (b * nq + i, small_col // LANE)),
                  _resident(ovl_t.shape), _resident(eye.shape), _resident(expand.shape)],
        out_specs=pl.BlockSpec((NSA_TQ, NSA_Q), lambda b, i: (b * nq + i, 0)),
        out_shape=jax.ShapeDtypeStruct((TOKENS, NSA_Q), BF16),
        scratch_shapes=[pltpu.VMEM((NSA_GROUPS, NSA_TQ, SEQ), F32),
                        pltpu.VMEM((NSA_GROUPS, NSA_HG * NSA_TQ, NSA_DH), BF16)]
        + [pltpu.VMEM((NSA_GROUPS, NSA_HG * NSA_TQ, NSA_DH), F32)] * 3,
        compiler_params=pltpu.CompilerParams(dimension_semantics=("arbitrary", "arbitrary"),
                                             vmem_limit_bytes=VMEM_LIMIT),
        name="nsa_attention",
    )(p16, p16, p16, p16, p16, cmp_kv, p32, ovl_t, eye, expand)


EVEN_QKV_COL = 0
EVEN_GATE_COL = 2 * GDN_QK + GDN_V
EVEN_CMP_COL = EVEN_GATE_COL + GDN_V
EVEN_SMALL_COL = EVEN_CMP_COL + 2 * NSA_KV
EVEN_F32_W = EVEN_SMALL_COL + LANE
EVEN_Q_COL = 0
EVEN_KV_COL = NSA_Q
EVEN_BF16_W = NSA_Q + 4 * NSA_KV
assert NSA_SMALL_GATE + 3 * NSA_HEADS <= LANE


def even_proj_weight(w_in):
    qkv_a, beta, dec, gate_a, q_b, kv_b, gate_b = split_last(w_in, EVEN_SPLITS)
    pad = jnp.zeros((w_in.shape[0], LANE - 2 * GDN_HEADS - 3 * NSA_HEADS), w_in.dtype)
    return jnp.concatenate([qkv_a, gate_a, kv_b[:, :2 * NSA_KV], beta, dec, gate_b, pad, q_b, kv_b[:, 2 * NSA_KV:]],
                           axis=1).astype(BF16)


def even_mixer_core(p32, p16, conv_w, A_log, dt_bias, o_norm, cmp_pe, cmp_w1, cmp_b1, cmp_w2):
    oa = gated_delta_net(p32, conv_w, A_log, dt_bias, o_norm, EVEN_QKV_COL, EVEN_GATE_COL, EVEN_SMALL_COL)
    cmp_kv = nsa_compress(p32, EVEN_CMP_COL, cmp_pe, cmp_w1, cmp_b1, cmp_w2)
    ob = nsa_attention(p16, cmp_kv, p32, EVEN_Q_COL, EVEN_KV_COL, EVEN_SMALL_COL)
    return oa, ob


GLA_TC = 256
ODD_Q_COL = 0
ODD_K_COL = GLA_QK
ODD_V_COL = 2 * GLA_QK
ODD_R_COL = ODD_V_COL + GLA_V
ODD_LR_COL = ODD_R_COL + GLA_V
ODD_F32_W = ODD_LR_COL + LANE


def _gla_kernel(q_ref, k_ref, v_ref, r_ref, lr_ref, aw_ref, ab_ref, onorm_ref, tri_ref, o_ref, state_ref):
    t = pl.program_id(1)
    C = GLA_CHUNK

    @pl.when(t == 0)
    def _():
        state_ref[...] = jnp.zeros(state_ref.shape, F32)

    z = jnp.dot(lr_ref[...].astype(BF16), aw_ref[...], preferred_element_type=F32) + ab_ref[...]
    log_a = (jnp.minimum(z, 0.0) - jnp.log(1.0 + jnp.exp(-jnp.abs(z)))) * (1.0 / GLA_TAU)
    tri = tri_ref[...]
    causal = lax.broadcasted_iota(jnp.int32, (C, C), 0) >= lax.broadcasted_iota(jnp.int32, (C, C), 1)

    states = [state_ref[h] for h in range(GLA_HEADS)]
    for c in range(GLA_TC // C):
        rows = slice(c * C, (c + 1) * C)
        b_all = _cumsum_rows(tri, log_a[rows])
        for h in range(GLA_HEADS):
            kl = slice(h * GLA_DK, (h + 1) * GLA_DK)
            vl = slice(h * GLA_DV, (h + 1) * GLA_DV)
            b = b_all[:, kl]
            b_last = b[C - 1:C]
            q = q_ref[rows, kl] * GLA_DK ** -0.5
            k = k_ref[rows, kl]
            v = v_ref[rows, vl].astype(BF16)
            qt = (q * jnp.exp(b)).astype(BF16)
            kt = (k * jnp.exp(-b)).astype(BF16)
            kh = (k * jnp.exp(b_last - b)).astype(BF16)
            intra = jnp.where(causal, _dot_nt(qt, kt), 0.0).astype(BF16)
            o = jnp.dot(intra, v, preferred_element_type=F32) + _dot_nt(qt, states[h].astype(BF16))
            states[h] = states[h] * jnp.exp(b_last) + _dot_tn(v, kh)
            r = r_ref[rows, vl]
            on = o * lax.rsqrt(jnp.mean(o * o, axis=-1, keepdims=True) + EPS) * onorm_ref[...]
            o_ref[rows, vl] = (on * (r * jax.nn.sigmoid(r))).astype(o_ref.dtype)
    for h in range(GLA_HEADS):
        state_ref[h] = states[h]


def gated_linear_attention(p32, a_w2, a_b, o_norm):
    nt = SEQ // GLA_TC
    tok = lambda blk: (lambda b, t: (b * nt + t, blk))
    aw = jnp.pad(a_w2, ((0, LANE - GLA_RANK), (0, 0))).astype(BF16)
    tri = jnp.asarray(np.tril(np.ones((GLA_CHUNK, GLA_CHUNK))), BF16)
    return pl.pallas_call(
        _gla_kernel,
        grid=(BATCH, nt),
        in_specs=[pl.BlockSpec((GLA_TC, GLA_QK), tok(ODD_Q_COL // GLA_QK)),
                  pl.BlockSpec((GLA_TC, GLA_QK), tok(ODD_K_COL // GLA_QK)),
                  pl.BlockSpec((GLA_TC, GLA_V), tok(ODD_V_COL // GLA_V)),
                  pl.BlockSpec((GLA_TC, GLA_V), tok(ODD_R_COL // GLA_V)),
                  pl.BlockSpec((GLA_TC, LANE), tok(ODD_LR_COL // LANE)),
                  _resident((LANE, GLA_QK)), _resident((1, GLA_QK)), _resident((1, GLA_DV)),
                  _resident((GLA_CHUNK, GLA_CHUNK))],
        out_specs=pl.BlockSpec((GLA_TC, GLA_V), tok(0)),
        out_shape=jax.ShapeDtypeStruct((TOKENS, GLA_V), BF16),
        scratch_shapes=[pltpu.VMEM((GLA_HEADS, GLA_DV, GLA_DK), F32)],
        compiler_params=pltpu.CompilerParams(dimension_semantics=("arbitrary", "arbitrary"),
                                             vmem_limit_bytes=VMEM_LIMIT),
        name="gated_linear_attention",
    )(p32, p32, p32, p32, p32, aw, a_b.reshape(1, GLA_QK), o_norm.reshape(1, GLA_DV), tri)


def _pad_cols(w, n):
    return jnp.pad(w, ((0, 0), (0, n - w.shape[1])))


def kernel(x, norm_gains, ffn_w_in, ffn_conv_w, ffn_conv_b, ffn_w_out, hyb_w_in, hyb_w_out, gdn_conv_w, gdn_A_log,
           gdn_dt_bias, gdn_o_norm, nsa_cmp_pe, nsa_cmp_w1, nsa_cmp_b1, nsa_cmp_w2, gla_w_in, gla_a_w2, gla_a_b,
           gla_o_norm, gla_w_out):
    x2 = x.reshape(TOKENS, D_MODEL)
    ffn_w_in_bf = ffn_w_in.astype(BF16)
    ffn_w_out_bf = ffn_w_out.astype(BF16)
    for layer in range(DEPTH):
        if layer % 2 == 0:
            e = layer // 2
            p32, p16 = norm_proj(x2, norm_gains[layer, 0], even_proj_weight(hyb_w_in[e]),
                                 ((EVEN_F32_W, F32), (EVEN_BF16_W, BF16)))
            oa, ob = even_mixer_core(p32, p16, gdn_conv_w[e], gdn_A_log[e], gdn_dt_bias[e], gdn_o_norm[e],
                                     nsa_cmp_pe[e], nsa_cmp_w1[e], nsa_cmp_b1[e], nsa_cmp_w2[e])
            halves = (oa, 0, ob, 0)
            w_mix = hyb_w_out[e]
        else:
            o_ = layer // 2
            w = _pad_cols(gla_w_in[o_], ODD_F32_W).astype(BF16)
            (p32,) = norm_proj(x2, norm_gains[layer, 0], w, ((ODD_F32_W, F32),))
            o = gated_linear_attention(p32, gla_a_w2[o_], gla_a_b[o_], gla_o_norm[o_])
            halves = (o, 0, o, 1)
            w_mix = gla_w_out[o_]
        x2 = mix_out_conv_ffn(*halves, x2, norm_gains[layer, 1], w_mix.astype(BF16), norm_gains[layer, 2],
                              layer, ffn_w_in_bf, ffn_conv_w, ffn_conv_b, ffn_w_out_bf, norm_gains[layer, 3])
    return x2.reshape(BATCH, SEQ, D_MODEL)
```

```python
import functools
import math

import jax
import jax.numpy as jnp
import numpy as np
from jax import lax
from jax.experimental import pallas as pl
from jax.experimental.pallas import tpu as pltpu

D_MODEL = 1024
BATCH = 8
SEQ = 2048
DEPTH = 4
TOKENS = BATCH * SEQ

EPS = 1e-6
NEG = -1e30
FORCE = 1e4

GDN_DK = 128
GDN_DV = 128
GDN_HEADS = 4
GDN_QK = GDN_HEADS * GDN_DK
GDN_V = GDN_HEADS * GDN_DV
GDN_CONV = 4
GDN_CHUNK = 64

NSA_DH = 64
NSA_HEADS = 8
NSA_GROUPS = 2
NSA_Q = NSA_HEADS * NSA_DH
NSA_KV = NSA_GROUPS * NSA_DH
CMP_LEN = 32
CMP_STRIDE = 16
CMP_HIDDEN = 2 * NSA_DH
SLC_LEN = 64
SLC_TOPK = 8
WINDOW = 512
Q_BLOCK = 128

GLA_HEADS = 4
GLA_DK = 128
GLA_DV = 256
GLA_QK = GLA_HEADS * GLA_DK
GLA_V = GLA_HEADS * GLA_DV
GLA_RANK = 16
GLA_TAU = 16.0
GLA_CHUNK = 64

D_FF = 2816
FFN_CONV = 3

EVEN_SPLITS = (2 * GDN_QK + GDN_V, GDN_HEADS, GDN_HEADS, GDN_V, NSA_Q, 6 * NSA_KV, 3 * NSA_HEADS)
ODD_SPLITS = (GLA_QK, GLA_QK, GLA_V, GLA_V, GLA_RANK)

LANE = 128
VMEM_LIMIT = 56 * 1024 * 1024

F32 = jnp.float32
BF16 = jnp.bfloat16


def _round_up(n, m):
    return (n + m - 1) // m * m


def _resident(shape):
    nd = len(shape)
    return pl.BlockSpec(shape, lambda *_: (0,) * nd, pipeline_mode=pl.Buffered(1))


def _resident_layer(shape, layer):
    nd = len(shape)
    return pl.BlockSpec((None,) + tuple(shape), lambda *_: (layer,) + (0,) * nd, pipeline_mode=pl.Buffered(1))


def _rms(x, g):
    ms = jnp.mean(x * x, axis=-1, keepdims=True)
    return x * lax.rsqrt(ms + EPS) * g


PROJ_TM = 256
PROJ_CH = 512


def _proj_kernel(x_ref, g_ref, w_ref, *o_refs):
    h = _rms(x_ref[...], g_ref[...]).astype(BF16)
    col = 0
    for o_ref in o_refs:
        n = o_ref.shape[1]
        for c0 in range(0, n, PROJ_CH):
            cw = min(PROJ_CH, n - c0)
            y = jnp.dot(h, w_ref[:, col + c0:col + c0 + cw], preferred_element_type=F32)
            o_ref[:, c0:c0 + cw] = y.astype(o_ref.dtype)
        col += n


def norm_proj(x2, gain, w, sections):
    T, D = x2.shape
    N = w.shape[1]
    assert N == sum(n for n, _ in sections) and all(n % LANE == 0 for n, _ in sections)
    return pl.pallas_call(
        _proj_kernel,
        grid=(T // PROJ_TM,),
        in_specs=[pl.BlockSpec((PROJ_TM, D), lambda i: (i, 0)), _resident((1, D)), _resident((D, N))],
        out_specs=[pl.BlockSpec((PROJ_TM, n), lambda i: (i, 0)) for n, _ in sections],
        out_shape=[jax.ShapeDtypeStruct((T, n), dt) for n, dt in sections],
        compiler_params=pltpu.CompilerParams(dimension_semantics=("arbitrary",), vmem_limit_bytes=VMEM_LIMIT),
        name="norm_proj",
    )(x2, gain.reshape(1, D), w)


FFN_TM = 512
FFN_HALO = 16
FFN_CH = 256
MIX_HALF = D_MODEL // 2
assert D_FF % FFN_CH == 0


def _ffn_kernel(oa_ref, ob_ref, x_ref, g_mix_ref, w_mix_ref, g_in_ref, w_in_ref, cw_ref, cb_ref, w_out_ref, g_out_ref,
                y_ref, hext_ref, act_ref):
    t = pl.program_id(1)

    @pl.when(t == 0)
    def _():
        hext_ref[0:FFN_HALO, :] = jnp.zeros((FFN_HALO, D_MODEL), BF16)

    mix = (jnp.dot(oa_ref[...], w_mix_ref[0:MIX_HALF, :], preferred_element_type=F32)
           + jnp.dot(ob_ref[...], w_mix_ref[MIX_HALF:, :], preferred_element_type=F32))
    x = x_ref[...] + _rms(mix, g_mix_ref[...])
    hext_ref[FFN_HALO:, :] = _rms(x, g_in_ref[...]).astype(BF16)
    hext = hext_ref[...]

    def conv_half(c0):
        u = jnp.dot(hext, w_in_ref[:, c0:c0 + FFN_CH], preferred_element_type=F32)
        u1 = pltpu.roll(u, 1, 0)
        u2 = pltpu.roll(u, 2, 0)
        w = cw_ref[:, c0:c0 + FFN_CH]
        y = u2 * w[0:1] + u1 * w[1:2] + u * w[2:3] + cb_ref[:, c0:c0 + FFN_CH]
        return y[FFN_HALO:]

    for c0 in range(0, D_FF, FFN_CH):
        half_gate = 0.5 * conv_half(c0)
        val = conv_half(D_FF + c0)
        silu = half_gate + half_gate * jnp.tanh(half_gate)
        act_ref[:, c0:c0 + FFN_CH] = (silu * val).astype(BF16)
    down = jnp.dot(act_ref[...], w_out_ref[...], preferred_element_type=F32)
    y_ref[...] = x + _rms(down, g_out_ref[...])
    hext_ref[0:FFN_HALO, :] = hext_ref[FFN_TM:FFN_TM + FFN_HALO, :]


def mix_out_conv_ffn(oa, a_blk, ob, b_blk, x2, g_mix, w_mix, g_in, layer, w_in, conv_w, conv_b, w_out, g_out):
    T, D = x2.shape
    nt = SEQ // FFN_TM
    tok = lambda blk: (lambda b, t: (b * nt + t, blk))
    return pl.pallas_call(
        _ffn_kernel,
        grid=(BATCH, nt),
        in_specs=[pl.BlockSpec((FFN_TM, MIX_HALF), tok(a_blk)), pl.BlockSpec((FFN_TM, MIX_HALF), tok(b_blk)),
                  pl.BlockSpec((FFN_TM, D), tok(0)), _resident((1, D)), _resident((D, D)),
                  _resident((1, D)), _resident_layer((D, 2 * D_FF), layer),
                  _resident_layer((FFN_CONV, 2 * D_FF), layer), _resident_layer((1, 2 * D_FF), layer),
                  _resident_layer((D_FF, D), layer), _resident((1, D))],
        out_specs=pl.BlockSpec((FFN_TM, D), tok(0)),
        out_shape=jax.ShapeDtypeStruct((T, D), F32),
        scratch_shapes=[pltpu.VMEM((FFN_HALO + FFN_TM, D), BF16), pltpu.VMEM((FFN_TM, D_FF), BF16)],
        compiler_params=pltpu.CompilerParams(dimension_semantics=("arbitrary", "arbitrary"),
                                             vmem_limit_bytes=VMEM_LIMIT),
        name="mix_out_conv_ffn",
    )(oa, ob, x2, g_mix.reshape(1, D), w_mix, g_in.reshape(1, D), w_in, conv_w, conv_b.reshape(DEPTH, 1, -1), w_out,
      g_out.reshape(1, D))


def split_last(x, sizes):
    idx = np.cumsum(sizes)[:-1].tolist()
    return jnp.split(x, idx, axis=-1)


def _dot_nt(a, b):
    return lax.dot_general(a, b, (((1,), (1,)), ((), ())), preferred_element_type=F32)


def _dot_tn(a, b):
    return lax.dot_general(a, b, (((0,), (0,)), ((), ())), preferred_element_type=F32)


def _split3(a):
    hi = a.astype(BF16)
    r = a - hi.astype(F32)
    mid = r.astype(BF16)
    return hi, mid, (r - mid.astype(F32)).astype(BF16)


def _dot_x3(a, b):
    a_hi, a_lo, _ = _split3(a)
    b_hi, b_lo, _ = _split3(b)
    d = lambda p, q: jnp.dot(p, q, preferred_element_type=F32)
    return d(a_hi, b_hi) + (d(a_lo, b_hi) + d(a_hi, b_lo))


def _cumsum_rows(tri, g):
    return sum(jnp.dot(tri, t, preferred_element_type=F32) for t in reversed(_split3(g)))


def _unit_lower_inverses(mats):
    c = mats[0].shape[0]
    eye = jnp.where(lax.broadcasted_iota(jnp.int32, (c, c), 0) == lax.broadcasted_iota(jnp.int32, (c, c), 1), 1.0, 0.0)
    ps = [-a for a in mats]
    ts = [eye + p for p in ps]
    k = 2
    while k < c:
        ps = [_dot_x3(p, p) for p in ps]
        ts = [t + _dot_x3(t, p) for t, p in zip(ts, ps)]
        k *= 2
    return ts


GDN_TC = 256
GDN_HALO = 8
GDN_SMALL_BETA = 0
GDN_SMALL_DECAY = GDN_HEADS
NSA_SMALL_GATE = 2 * GDN_HEADS


def _gdn_kernel(x_ref, gate_ref, small_ref, cw_ref, alog_ref, dtb_ref, onorm_ref, tri_ref, o_ref, xext_ref, state_ref):
    t = pl.program_id(1)
    C = GDN_CHUNK

    @pl.when(t == 0)
    def _():
        xext_ref[0:GDN_HALO, :] = jnp.zeros((GDN_HALO, xext_ref.shape[1]), F32)
        state_ref[...] = jnp.zeros(state_ref.shape, F32)

    xext_ref[GDN_HALO:, :] = x_ref[...]
    small = small_ref[...]
    beta_all = jax.nn.sigmoid(small)
    z = small + dtb_ref[...]
    softplus = jnp.maximum(z, 0.0) + jnp.log(1.0 + jnp.exp(-jnp.abs(z)))
    g_all = -jnp.exp(alog_ref[...]) * softplus
    tri = tri_ref[...]
    row = lax.broadcasted_iota(jnp.int32, (C, C), 0)
    col = lax.broadcasted_iota(jnp.int32, (C, C), 1)

    def conv_silu(c0, width):
        xe = xext_ref[:, c0:c0 + width]
        w = cw_ref[:, c0:c0 + width]
        y = xe * w[GDN_CONV - 1:GDN_CONV]
        for s in range(1, GDN_CONV):
            y = y + pltpu.roll(xe, s, 0) * w[GDN_CONV - 1 - s:GDN_CONV - s]
        y = y[GDN_HALO:]
        return y * jax.nn.sigmoid(y)

    n_chunks = GDN_TC // C
    gsums = [_cumsum_rows(tri, g_all[c * C:(c + 1) * C]) for c in range(n_chunks)]
    gsum_ts = [jnp.transpose(gs) for gs in gsums]

    work = {}
    for h in range(GDN_HEADS):
        q_all = conv_silu(h * GDN_DK, GDN_DK)
        k_all = conv_silu(GDN_QK + h * GDN_DK, GDN_DK)
        v_all = conv_silu(2 * GDN_QK + h * GDN_DV, GDN_DV)
        q_all = q_all * (lax.rsqrt(jnp.sum(q_all * q_all, axis=-1, keepdims=True) + EPS) * GDN_DK ** -0.5)
        k_all = k_all * lax.rsqrt(jnp.sum(k_all * k_all, axis=-1, keepdims=True) + EPS)
        for c in range(n_chunks):
            rows = slice(c * C, (c + 1) * C)
            q, k, v = q_all[rows], k_all[rows], v_all[rows]
            beta = beta_all[rows, GDN_SMALL_BETA + h:GDN_SMALL_BETA + h + 1]
            gcol = gsums[c][:, GDN_SMALL_DECAY + h:GDN_SMALL_DECAY + h + 1]
            grow = gsum_ts[c][GDN_SMALL_DECAY + h:GDN_SMALL_DECAY + h + 1, :]
            glast = gcol[C - 1:C]
            decay = jnp.exp(jnp.where(row >= col, gcol - grow, -jnp.inf))
            kb = k * beta
            kbf = k.astype(BF16)
            eg = jnp.exp(gcol)
            work[h, c] = dict(
                a_kk=jnp.where(row > col, _dot_nt(kb.astype(BF16), kbf) * decay, 0.0),
                a_qk=(_dot_nt(q.astype(BF16), kbf) * decay).astype(BF16),
                rhs=jnp.concatenate([v * beta, kb * eg], axis=-1),
                qg=(q * eg).astype(BF16),
                kd=(k * jnp.exp(glast - gcol)).astype(BF16),
                keep=jnp.exp(glast))
    keys = sorted(work)
    inverses = _unit_lower_inverses([work[key]["a_kk"] for key in keys])
    for key, inv in zip(keys, inverses):
        work[key]["sol"] = _dot_x3(inv, work[key]["rhs"])

    states = [state_ref[h] for h in range(GDN_HEADS)]
    for c in range(n_chunks):
        rows = slice(c * C, (c + 1) * C)
        for h in range(GDN_HEADS):
            wk = work[h, c]
            u, w = wk["sol"][:, :GDN_DV], wk["sol"][:, GDN_DV:]
            sb = states[h].astype(BF16)
            v_new = u - jnp.dot(w.astype(BF16), sb, preferred_element_type=F32)
            vb = v_new.astype(BF16)
            o = (jnp.dot(wk["qg"], sb, preferred_element_type=F32)
                 + jnp.dot(wk["a_qk"], vb, preferred_element_type=F32))
            states[h] = states[h] * wk["keep"] + _dot_tn(wk["kd"], vb)
            gate = gate_ref[rows, h * GDN_DV:(h + 1) * GDN_DV]
            on = o * lax.rsqrt(jnp.mean(o * o, axis=-1, keepdims=True) + EPS) * onorm_ref[...]
            o_ref[rows, h * GDN_DV:(h + 1) * GDN_DV] = (on * (gate * jax.nn.sigmoid(gate))).astype(o_ref.dtype)
    for h in range(GDN_HEADS):
        state_ref[h] = states[h]
    xext_ref[0:GDN_HALO, :] = xext_ref[GDN_TC:GDN_TC + GDN_HALO, :]


def gated_delta_net(p32, conv_w, a_log, dt_bias, o_norm, qkv_col, gate_col, small_col):
    nt = SEQ // GDN_TC
    width = 2 * GDN_QK + GDN_V
    lane_vec = lambda a, off: jnp.zeros((1, LANE), F32).at[0, off:off + GDN_HEADS].set(a.astype(F32))
    tri = jnp.asarray(np.tril(np.ones((GDN_CHUNK, GDN_CHUNK))), BF16)
    tok = lambda blk: (lambda b, t: (b * nt + t, blk))
    return pl.pallas_call(
        _gdn_kernel,
        grid=(BATCH, nt),
        in_specs=[pl.BlockSpec((GDN_TC, width), tok(qkv_col // width)),
                  pl.BlockSpec((GDN_TC, GDN_V), tok(gate_col // GDN_V)),
                  pl.BlockSpec((GDN_TC, LANE), tok(small_col // LANE)),
                  _resident((GDN_CONV, width)), _resident((1, LANE)), _resident((1, LANE)), _resident((1, GDN_DV)),
                  _resident((GDN_CHUNK, GDN_CHUNK))],
        out_specs=pl.BlockSpec((GDN_TC, GDN_V), tok(0)),
        out_shape=jax.ShapeDtypeStruct((TOKENS, GDN_V), BF16),
        scratch_shapes=[pltpu.VMEM((GDN_HALO + GDN_TC, width), F32), pltpu.VMEM((GDN_HEADS, GDN_DK, GDN_DV), F32)],
        compiler_params=pltpu.CompilerParams(dimension_semantics=("arbitrary", "arbitrary"),
                                             vmem_limit_bytes=VMEM_LIMIT),
        name="gated_delta_net",
    )(p32, p32, p32, conv_w, lane_vec(a_log, GDN_SMALL_DECAY), lane_vec(dt_bias, GDN_SMALL_DECAY),
      o_norm.reshape(1, GDN_DV), tri)


N_SLC = SEQ // SLC_LEN
N_CMP = SEQ // CMP_STRIDE
NSA_HG = NSA_HEADS // NSA_GROUPS
NSA_TQ = 128
SEL_STEP = 512
WIN_W = WINDOW + NSA_TQ
NSA_SCALE = NSA_DH ** -0.5


def _cmp_kernel(kc_ref, vc_ref, pe_ref, w1_ref, b1_ref, w2_ref, o_ref, strip_ref):
    half = CMP_STRIDE * NSA_DH
    for kv, x_ref in enumerate((kc_ref, vc_ref)):
        for p in range(CMP_STRIDE):
            xp = x_ref[pl.ds(p, N_CMP, stride=CMP_STRIDE), :]
            for g in range(NSA_GROUPS):
                strip_ref[kv * NSA_GROUPS + g, :, p * NSA_DH:(p + 1) * NSA_DH] = xp[:, g * NSA_DH:(g + 1) * NSA_DH]
    for kv in range(2):
        for g in range(NSA_GROUPS):
            rows = strip_ref[kv * NSA_GROUPS + g]
            lo = (rows + pe_ref[kv, :, :half]).astype(BF16)
            hi = (rows + pe_ref[kv, :, half:]).astype(BF16)
            a = jnp.dot(lo, w1_ref[kv, :half, :], preferred_element_type=F32)
            b = jnp.dot(hi, w1_ref[kv, half:, :], preferred_element_type=F32)
            hid = a + pltpu.roll(b, N_CMP - 1, 0) + b1_ref[kv]
            act = (hid * jax.nn.sigmoid(hid)).astype(BF16)
            o_ref[0, kv * NSA_GROUPS + g] = jnp.dot(act, w2_ref[kv], preferred_element_type=F32).astype(BF16)


def nsa_compress(p32, cmp_col, pe, w1, b1, w2):
    n = 2 * NSA_GROUPS
    return pl.pallas_call(
        _cmp_kernel,
        grid=(BATCH,),
        in_specs=[pl.BlockSpec((SEQ, NSA_KV), lambda b: (b, cmp_col // NSA_KV)),
                  pl.BlockSpec((SEQ, NSA_KV), lambda b: (b, cmp_col // NSA_KV + 1)),
                  _resident((2, 1, CMP_LEN * NSA_DH)), _resident((2, CMP_LEN * NSA_DH, CMP_HIDDEN)),
                  _resident((2, 1, CMP_HIDDEN)), _resident((2, CMP_HIDDEN, NSA_DH))],
        out_specs=pl.BlockSpec((1, n, N_CMP, NSA_DH), lambda b: (b, 0, 0, 0)),
        out_shape=jax.ShapeDtypeStruct((BATCH, n, N_CMP, NSA_DH), BF16),
        scratch_shapes=[pltpu.VMEM((n, N_CMP, CMP_STRIDE * NSA_DH), F32)],
        compiler_params=pltpu.CompilerParams(dimension_semantics=("arbitrary",), vmem_limit_bytes=VMEM_LIMIT),
        name="nsa_compress",
    )(p32, p32, pe.reshape(2, 1, -1), w1.astype(BF16), b1.reshape(2, 1, -1), w2.astype(BF16))


def _masked_attention(q, k, v_ones, bias):
    w = k.shape[0]
    s = _dot_nt(q, k).reshape(NSA_HG, NSA_TQ, w) + bias[None]
    s = s.reshape(NSA_HG * NSA_TQ, w)
    p = jnp.exp((s - jnp.max(s, axis=-1, keepdims=True)).astype(BF16))
    acc = jnp.dot(p, v_ones, preferred_element_type=F32)
    return acc[:, :NSA_DH] / acc[:, NSA_DH:NSA_DH + 1]


def _with_ones_column(v):
    lane = lax.broadcasted_iota(jnp.int32, v.shape, 1)
    return jnp.concatenate([v, jnp.where(lane == 0, 1.0, 0.0).astype(v.dtype)], axis=-1)


def _nsa_kernel(q_ref, ks_ref, vs_ref, kw_ref, vw_ref, cmp_ref, gate_ref, ovl_ref, eye_ref, expand_ref,
                o_ref, sbias_ref, qs_ref, ocmp_ref, gsel_ref, gwin_ref, vs1_ref, vw1_ref):
    i = pl.program_id(1)
    q0 = i * NSA_TQ
    rows = NSA_HG * NSA_TQ

    @pl.when(i == 0)
    def _():
        for g in range(NSA_GROUPS):
            lanes = slice(g * NSA_DH, (g + 1) * NSA_DH)
            vs1_ref[g] = _with_ones_column(vs_ref[:, lanes])
            vw1_ref[g] = _with_ones_column(vw_ref[:, lanes])

    gates = jax.nn.sigmoid(gate_ref[...])
    tok = q0 + lax.broadcasted_iota(jnp.int32, (NSA_TQ, 1), 0)
    tok_l = q0 + lax.broadcasted_iota(jnp.int32, (1, NSA_TQ), 1)

    for g in range(NSA_GROUPS):
        heads = [g * NSA_HG + h for h in range(NSA_HG)]
        q = jnp.concatenate([q_ref[:, h * NSA_DH:(h + 1) * NSA_DH] for h in heads], axis=0)
        q = q * jnp.asarray(NSA_SCALE, BF16)
        qs_ref[g] = q

        s = _dot_nt(q, cmp_ref[0, g]).reshape(NSA_HG, NSA_TQ, N_CMP)
        cmp_end = lax.broadcasted_iota(jnp.int32, (1, N_CMP), 1) * CMP_STRIDE + (CMP_LEN - 1)
        cvalid = (cmp_end <= tok)[None]
        sm = jnp.where(cvalid, s, NEG)
        e = jnp.exp(sm - jnp.max(sm, axis=-1, keepdims=True))
        p = jnp.where(cvalid, e / jnp.sum(e, axis=-1, keepdims=True), 0.0)
        o_cmp = jnp.dot(p.reshape(rows, N_CMP).astype(BF16), cmp_ref[0, NSA_GROUPS + g], preferred_element_type=F32)

        def gate(branch):
            lane = NSA_SMALL_GATE + branch * NSA_HEADS
            return jnp.concatenate([gates[:, lane + h:lane + h + 1] for h in heads], axis=0)

        ocmp_ref[g] = gate(0) * o_cmp
        gsel_ref[g] = jnp.broadcast_to(gate(1), (rows, NSA_DH))
        gwin_ref[g] = jnp.broadcast_to(gate(2), (rows, NSA_DH))

        imp = _dot_nt(ovl_ref[...], jnp.sum(p, axis=0).astype(BF16))
        blk = lax.broadcasted_iota(jnp.int32, (N_SLC, NSA_TQ), 0)
        cur = tok_l // SLC_LEN
        bvalid = blk <= cur
        forced = (blk == 0) | (blk == cur) | (blk == cur - 1)
        score = jnp.where(bvalid, imp + jnp.where(forced, FORCE, 0.0), NEG)
        rank = jnp.zeros((N_SLC, NSA_TQ), F32)
        for j in range(N_SLC):
            sj = score[j:j + 1, :]
            tie = jnp.where((sj == score) & (blk > j), 1.0, 0.0)
            rank = rank + jnp.where(sj > score, 1.0, tie)
        sel_t = jnp.where((rank < SLC_TOPK) & bvalid, 1.0, 0.0).astype(BF16)
        sel = _dot_nt(eye_ref[...], sel_t).astype(BF16)
        hit = jnp.dot(sel, expand_ref[...], preferred_element_type=F32)
        kpos = lax.broadcasted_iota(jnp.int32, (1, SEQ), 1)
        sbias_ref[g] = jnp.where((hit > 0.5) & (kpos <= tok), 0.0, NEG)

    for n in range(SEQ // SEL_STEP):
        @pl.when(i // (SEL_STEP // NSA_TQ) == n)
        def _():
            w = (n + 1) * SEL_STEP
            w0 = pl.multiple_of(jnp.maximum(q0 + NSA_TQ - WIN_W, 0), NSA_TQ)
            wpos = w0 + lax.broadcasted_iota(jnp.int32, (1, WIN_W), 1)
            wbias = jnp.where((wpos <= tok) & (wpos > tok - WINDOW), 0.0, NEG)
            r = pl.ds(w0, WIN_W)
            for g in range(NSA_GROUPS):
                lanes = slice(g * NSA_DH, (g + 1) * NSA_DH)
                q = qs_ref[g]
                o_sel = _masked_attention(q, ks_ref[0:w, lanes], vs1_ref[g, 0:w, :], sbias_ref[g, :, 0:w])
                o_win = _masked_attention(q, kw_ref[r, lanes], vw1_ref[g, r, :], wbias)
                o = ocmp_ref[g] + gsel_ref[g] * o_sel + gwin_ref[g] * o_win
                for k in range(NSA_HG):
                    h = g * NSA_HG + k
                    o_ref[:, h * NSA_DH:(h + 1) * NSA_DH] = o[k * NSA_TQ:(k + 1) * NSA_TQ].astype(o_ref.dtype)


def _nsa_constants():
    c0 = np.arange(N_CMP)[:, None] * CMP_STRIDE
    s0 = np.arange(N_SLC)[None, :] * SLC_LEN
    overlap = np.clip(np.minimum(c0 + CMP_LEN, s0 + SLC_LEN) - np.maximum(c0, s0), 0, None) / CMP_STRIDE
    overlap[N_CMP - 1] = 0.0
    expand = (np.arange(SEQ)[None, :] // SLC_LEN == np.arange(N_SLC)[:, None]).astype(np.float32)
    return jnp.asarray(overlap.T, BF16), jnp.asarray(np.eye(NSA_TQ), BF16), jnp.asarray(expand, BF16)


def nsa_attention(p16, cmp_kv, p32, q_col, kv_col, small_col):
    nq = SEQ // NSA_TQ
    ovl_t, eye, expand = _nsa_constants()
    kv_spec = lambda n: pl.BlockSpec((SEQ, NSA_KV), lambda b, i: (b, kv_col // NSA_KV + n))
    return pl.pallas_call(
        _nsa_kernel,
        grid=(BATCH, nq),
        in_specs=[pl.BlockSpec((NSA_TQ, NSA_Q), lambda b, i: (b * nq + i, q_col // NSA_Q)),
                  kv_spec(0), kv_spec(1), kv_spec(2), kv_spec(3),
                  pl.BlockSpec((1, 2 * NSA_GROUPS, N_CMP, NSA_DH), lambda b, i: (b, 0, 0, 0)),
                  pl.BlockSpec((NSA_TQ, LANE), lambda b, i: (b * nq + i, small_col // LANE)),
                  _resident(ovl_t.shape), _resident(eye.shape), _resident(expand.shape)],
        out_specs=pl.BlockSpec((NSA_TQ, NSA_Q), lambda b, i: (b * nq + i, 0)),
        out_shape=jax.ShapeDtypeStruct((TOKENS, NSA_Q), BF16),
        scratch_shapes=[pltpu.VMEM((NSA_GROUPS, NSA_TQ, SEQ), F32),
                        pltpu.VMEM((NSA_GROUPS, NSA_HG * NSA_TQ, NSA_DH), BF16)]
        + [pltpu.VMEM((NSA_GROUPS, NSA_HG * NSA_TQ, NSA_DH), F32)] * 3
        + [pltpu.VMEM((NSA_GROUPS, SEQ, 2 * NSA_DH), BF16)] * 2,
        compiler_params=pltpu.CompilerParams(dimension_semantics=("arbitrary", "arbitrary"),
                                             vmem_limit_bytes=VMEM_LIMIT),
        name="nsa_attention",
    )(p16, p16, p16, p16, p16, cmp_kv, p32, ovl_t, eye, expand)


EVEN_QKV_COL = 0
EVEN_GATE_COL = 2 * GDN_QK + GDN_V
EVEN_CMP_COL = EVEN_GATE_COL + GDN_V
EVEN_SMALL_COL = EVEN_CMP_COL + 2 * NSA_KV
EVEN_F32_W = EVEN_SMALL_COL + LANE
EVEN_Q_COL = 0
EVEN_KV_COL = NSA_Q
EVEN_BF16_W = NSA_Q + 4 * NSA_KV
assert NSA_SMALL_GATE + 3 * NSA_HEADS <= LANE


def even_proj_weight(w_in):
    qkv_a, beta, dec, gate_a, q_b, kv_b, gate_b = split_last(w_in, EVEN_SPLITS)
    pad = jnp.zeros((w_in.shape[0], LANE - 2 * GDN_HEADS - 3 * NSA_HEADS), w_in.dtype)
    return jnp.concatenate([qkv_a, gate_a, kv_b[:, :2 * NSA_KV], beta, dec, gate_b, pad, q_b, kv_b[:, 2 * NSA_KV:]],
                           axis=1).astype(BF16)


def even_mixer_core(p32, p16, conv_w, A_log, dt_bias, o_norm, cmp_pe, cmp_w1, cmp_b1, cmp_w2):
    oa = gated_delta_net(p32, conv_w, A_log, dt_bias, o_norm, EVEN_QKV_COL, EVEN_GATE_COL, EVEN_SMALL_COL)
    cmp_kv = nsa_compress(p32, EVEN_CMP_COL, cmp_pe, cmp_w1, cmp_b1, cmp_w2)
    ob = nsa_attention(p16, cmp_kv, p32, EVEN_Q_COL, EVEN_KV_COL, EVEN_SMALL_COL)
    return oa, ob


GLA_TC = 256
ODD_Q_COL = 0
ODD_K_COL = GLA_QK
ODD_V_COL = 2 * GLA_QK
ODD_R_COL = ODD_V_COL + GLA_V
ODD_LR_COL = ODD_R_COL + GLA_V
ODD_F32_W = ODD_LR_COL + LANE


def _gla_kernel(q_ref, k_ref, v_ref, r_ref, lr_ref, aw_ref, ab_ref, onorm_ref, tri_ref, o_ref, state_ref):
    t = pl.program_id(1)
    C = GLA_CHUNK

    @pl.when(t == 0)
    def _():
        state_ref[...] = jnp.zeros(state_ref.shape, F32)

    z = jnp.dot(lr_ref[...].astype(BF16), aw_ref[...], preferred_element_type=F32) + ab_ref[...]
    log_a = (jnp.minimum(z, 0.0) - jnp.log(1.0 + jnp.exp(-jnp.abs(z)))) * (1.0 / GLA_TAU)
    tri = tri_ref[...]
    causal = lax.broadcasted_iota(jnp.int32, (C, C), 0) >= lax.broadcasted_iota(jnp.int32, (C, C), 1)

    states = [state_ref[h] for h in range(GLA_HEADS)]
    for c in range(GLA_TC // C):
        rows = slice(c * C, (c + 1) * C)
        b_all = _cumsum_rows(tri, log_a[rows])
        for h in range(GLA_HEADS):
            kl = slice(h * GLA_DK, (h + 1) * GLA_DK)
            vl = slice(h * GLA_DV, (h + 1) * GLA_DV)
            b = b_all[:, kl]
            b_last = b[C - 1:C]
            q = q_ref[rows, kl] * GLA_DK ** -0.5
            k = k_ref[rows, kl]
            v = v_ref[rows, vl].astype(BF16)
            qt = (q * jnp.exp(b)).astype(BF16)
            kt = (k * jnp.exp(-b)).astype(BF16)
            kh = (k * jnp.exp(b_last - b)).astype(BF16)
            intra = jnp.where(causal, _dot_nt(qt, kt), 0.0).astype(BF16)
            o = jnp.dot(intra, v, preferred_element_type=F32) + _dot_nt(qt, states[h].astype(BF16))
            states[h] = states[h] * jnp.exp(b_last) + _dot_tn(v, kh)
            r = r_ref[rows, vl]
            on = o * lax.rsqrt(jnp.mean(o * o, axis=-1, keepdims=True) + EPS) * onorm_ref[...]
            o_ref[rows, vl] = (on * (r * jax.nn.sigmoid(r))).astype(o_ref.dtype)
    for h in range(GLA_HEADS):
        state_ref[h] = states[h]


def gated_linear_attention(p32, a_w2, a_b, o_norm):
    nt = SEQ // GLA_TC
    tok = lambda blk: (lambda b, t: (b * nt + t, blk))
    aw = jnp.pad(a_w2, ((0, LANE - GLA_RANK), (0, 0))).astype(BF16)
    tri = jnp.asarray(np.tril(np.ones((GLA_CHUNK, GLA_CHUNK))), BF16)
    return pl.pallas_call(
        _gla_kernel,
        grid=(BATCH, nt),
        in_specs=[pl.BlockSpec((GLA_TC, GLA_QK), tok(ODD_Q_COL // GLA_QK)),
                  pl.BlockSpec((GLA_TC, GLA_QK), tok(ODD_K_COL // GLA_QK)),
                  pl.BlockSpec((GLA_TC, GLA_V), tok(ODD_V_COL // GLA_V)),
                  pl.BlockSpec((GLA_TC, GLA_V), tok(ODD_R_COL // GLA_V)),
                  pl.BlockSpec((GLA_TC, LANE), tok(ODD_LR_COL // LANE)),
                  _resident((LANE, GLA_QK)), _resident((1, GLA_QK)), _resident((1, GLA_DV)),
                  _resident((GLA_CHUNK, GLA_CHUNK))],
        out_specs=pl.BlockSpec((GLA_TC, GLA_V), tok(0)),
        out_shape=jax.ShapeDtypeStruct((TOKENS, GLA_V), BF16),
        scratch_shapes=[pltpu.VMEM((GLA_HEADS, GLA_DV, GLA_DK), F32)],
        compiler_params=pltpu.CompilerParams(dimension_semantics=("arbitrary", "arbitrary"),
                                             vmem_limit_bytes=VMEM_LIMIT),
        name="gated_linear_attention",
    )(p32, p32, p32, p32, p32, aw, a_b.reshape(1, GLA_QK), o_norm.reshape(1, GLA_DV), tri)


def _pad_cols(w, n):
    return jnp.pad(w, ((0, 0), (0, n - w.shape[1])))


def kernel(x, norm_gains, ffn_w_in, ffn_conv_w, ffn_conv_b, ffn_w_out, hyb_w_in, hyb_w_out, gdn_conv_w, gdn_A_log,
           gdn_dt_bias, gdn_o_norm, nsa_cmp_pe, nsa_cmp_w1, nsa_cmp_b1, nsa_cmp_w2, gla_w_in, gla_a_w2, gla_a_b,
           gla_o_norm, gla_w_out):
    x2 = x.reshape(TOKENS, D_MODEL)
    ffn_w_in_bf = ffn_w_in.astype(BF16)
    ffn_w_out_bf = ffn_w_out.astype(BF16)
    for layer in range(DEPTH):
        if layer % 2 == 0:
            e = layer // 2
            p32, p16 = norm_proj(x2, norm_gains[layer, 0], even_proj_weight(hyb_w_in[e]),
                                 ((EVEN_F32_W, F32), (EVEN_BF16_W, BF16)))
            oa, ob = even_mixer_core(p32, p16, gdn_conv_w[e], gdn_A_log[e], gdn_dt_bias[e], gdn_o_norm[e],
                                     nsa_cmp_pe[e], nsa_cmp_w1[e], nsa_cmp_b1[e], nsa_cmp_w2[e])
            halves = (oa, 0, ob, 0)
            w_mix = hyb_w_out[e]
        else:
            o_ = layer // 2
            w = _pad_cols(gla_w_in[o_], ODD_F32_W).astype(BF16)
            (p32,) = norm_proj(x2, norm_gains[layer, 0], w, ((ODD_F32_W, F32),))
            o = gated_linear_attention(p32, gla_a_w2[o_], gla_a_b[o_], gla_o_norm[o_])
            halves = (o, 0, o, 1)
            w_mix = gla_w_out[o_]
        x2 = mix_out_conv_ffn(*halves, x2, norm_gains[layer, 1], w_mix.astype(BF16), norm_gains[layer, 2],
                              layer, ffn_w_in_bf, ffn_conv_w, ffn_conv_b, ffn_w_out_bf, norm_gains[layer, 3])
    return x2.reshape(BATCH, SEQ, D_MODEL)
```

```python
import functools
import math

import jax
import jax.numpy as jnp
import numpy as np
from jax import lax
from jax.experimental import pallas as pl
from jax.experimental.pallas import tpu as pltpu

D_MODEL = 1024
BATCH = 8
SEQ = 2048
DEPTH = 4
TOKENS = BATCH * SEQ

EPS = 1e-6
NEG = -1e30
FORCE = 1e4

GDN_DK = 128
GDN_DV = 128
GDN_HEADS = 4
GDN_QK = GDN_HEADS * GDN_DK
GDN_V = GDN_HEADS * GDN_DV
GDN_CONV = 4
GDN_CHUNK = 64

NSA_DH = 64
NSA_HEADS = 8
NSA_GROUPS = 2
NSA_Q = NSA_HEADS * NSA_DH
NSA_KV = NSA_GROUPS * NSA_DH
CMP_LEN = 32
CMP_STRIDE = 16
CMP_HIDDEN = 2 * NSA_DH
SLC_LEN = 64
SLC_TOPK = 8
WINDOW = 512
Q_BLOCK = 128

GLA_HEADS = 4
GLA_DK = 128
GLA_DV = 256
GLA_QK = GLA_HEADS * GLA_DK
GLA_V = GLA_HEADS * GLA_DV
GLA_RANK = 16
GLA_TAU = 16.0
GLA_CHUNK = 64

D_FF = 2816
FFN_CONV = 3

EVEN_SPLITS = (2 * GDN_QK + GDN_V, GDN_HEADS, GDN_HEADS, GDN_V, NSA_Q, 6 * NSA_KV, 3 * NSA_HEADS)
ODD_SPLITS = (GLA_QK, GLA_QK, GLA_V, GLA_V, GLA_RANK)

LANE = 128
VMEM_LIMIT = 56 * 1024 * 1024

F32 = jnp.float32
BF16 = jnp.bfloat16


def _round_up(n, m):
    return (n + m - 1) // m * m


def _resident(shape):
    nd = len(shape)
    return pl.BlockSpec(shape, lambda *_: (0,) * nd, pipeline_mode=pl.Buffered(1))


def _resident_layer(shape, layer):
    nd = len(shape)
    return pl.BlockSpec((None,) + tuple(shape), lambda *_: (layer,) + (0,) * nd, pipeline_mode=pl.Buffered(1))


def _rms(x, g):
    ms = jnp.mean(x * x, axis=-1, keepdims=True)
    return x * lax.rsqrt(ms + EPS) * g


PROJ_TM = 256
PROJ_CH = 512


def _proj_kernel(x_ref, g_ref, w_ref, *o_refs):
    h = _rms(x_ref[...], g_ref[...]).astype(BF16)
    col = 0
    for o_ref in o_refs:
        n = o_ref.shape[1]
        for c0 in range(0, n, PROJ_CH):
            cw = min(PROJ_CH, n - c0)
            y = jnp.dot(h, w_ref[:, col + c0:col + c0 + cw], preferred_element_type=F32)
            o_ref[:, c0:c0 + cw] = y.astype(o_ref.dtype)
        col += n


def norm_proj(x2, gain, w, sections):
    T, D = x2.shape
    N = w.shape[1]
    assert N == sum(n for n, _ in sections) and all(n % LANE == 0 for n, _ in sections)
    return pl.pallas_call(
        _proj_kernel,
        grid=(T // PROJ_TM,),
        in_specs=[pl.BlockSpec((PROJ_TM, D), lambda i: (i, 0)), _resident((1, D)), _resident((D, N))],
        out_specs=[pl.BlockSpec((PROJ_TM, n), lambda i: (i, 0)) for n, _ in sections],
        out_shape=[jax.ShapeDtypeStruct((T, n), dt) for n, dt in sections],
        compiler_params=pltpu.CompilerParams(dimension_semantics=("arbitrary",), vmem_limit_bytes=VMEM_LIMIT),
        name="norm_proj",
    )(x2, gain.reshape(1, D), w)


FFN_TM = 512
FFN_HALO = 16
FFN_CH = 256
MIX_HALF = D_MODEL // 2
assert D_FF % FFN_CH == 0


def _ffn_kernel(oa_ref, ob_ref, x_ref, g_mix_ref, w_mix_ref, g_in_ref, w_in_ref, cw_ref, cb_ref, w_out_ref, g_out_ref,
                y_ref, hext_ref, act_ref):
    t = pl.program_id(1)

    @pl.when(t == 0)
    def _():
        hext_ref[0:FFN_HALO, :] = jnp.zeros((FFN_HALO, D_MODEL), BF16)

    mix = (jnp.dot(oa_ref[...], w_mix_ref[0:MIX_HALF, :], preferred_element_type=F32)
           + jnp.dot(ob_ref[...], w_mix_ref[MIX_HALF:, :], preferred_element_type=F32))
    x = x_ref[...] + _rms(mix, g_mix_ref[...])
    hext_ref[FFN_HALO:, :] = _rms(x, g_in_ref[...]).astype(BF16)
    hext = hext_ref[...]

    def conv_half(c0):
        u = jnp.dot(hext, w_in_ref[:, c0:c0 + FFN_CH], preferred_element_type=F32)
        u1 = pltpu.roll(u, 1, 0)
        u2 = pltpu.roll(u, 2, 0)
        w = cw_ref[:, c0:c0 + FFN_CH]
        y = u2 * w[0:1] + u1 * w[1:2] + u * w[2:3] + cb_ref[:, c0:c0 + FFN_CH]
        return y[FFN_HALO:]

    for c0 in range(0, D_FF, FFN_CH):
        half_gate = 0.5 * conv_half(c0)
        val = conv_half(D_FF + c0)
        silu = half_gate + half_gate * jnp.tanh(half_gate)
        act_ref[:, c0:c0 + FFN_CH] = (silu * val).astype(BF16)
    down = jnp.dot(act_ref[...], w_out_ref[...], preferred_element_type=F32)
    y_ref[...] = x + _rms(down, g_out_ref[...])
    hext_ref[0:FFN_HALO, :] = hext_ref[FFN_TM:FFN_TM + FFN_HALO, :]


def mix_out_conv_ffn(oa, a_blk, ob, b_blk, x2, g_mix, w_mix, g_in, layer, w_in, conv_w, conv_b, w_out, g_out):
    T, D = x2.shape
    nt = SEQ // FFN_TM
    tok = lambda blk: (lambda b, t: (b * nt + t, blk))
    return pl.pallas_call(
        _ffn_kernel,
        grid=(BATCH, nt),
        in_specs=[pl.BlockSpec((FFN_TM, MIX_HALF), tok(a_blk)), pl.BlockSpec((FFN_TM, MIX_HALF), tok(b_blk)),
                  pl.BlockSpec((FFN_TM, D), tok(0)), _resident((1, D)), _resident((D, D)),
                  _resident((1, D)), _resident_layer((D, 2 * D_FF), layer),
                  _resident_layer((FFN_CONV, 2 * D_FF), layer), _resident_layer((1, 2 * D_FF), layer),
                  _resident_layer((D_FF, D), layer), _resident((1, D))],
        out_specs=pl.BlockSpec((FFN_TM, D), tok(0)),
        out_shape=jax.ShapeDtypeStruct((T, D), F32),
        scratch_shapes=[pltpu.VMEM((FFN_HALO + FFN_TM, D), BF16), pltpu.VMEM((FFN_TM, D_FF), BF16)],
        compiler_params=pltpu.CompilerParams(dimension_semantics=("arbitrary", "arbitrary"),
                                             vmem_limit_bytes=VMEM_LIMIT),
        name="mix_out_conv_ffn",
    )(oa, ob, x2, g_mix.reshape(1, D), w_mix, g_in.reshape(1, D), w_in, conv_w, conv_b.reshape(DEPTH, 1, -1), w_out,
      g_out.reshape(1, D))


def split_last(x, sizes):
    idx = np.cumsum(sizes)[:-1].tolist()
    return jnp.split(x, idx, axis=-1)


def _dot_nt(a, b):
    return lax.dot_general(a, b, (((1,), (1,)), ((), ())), preferred_element_type=F32)


def _dot_tn(a, b):
    return lax.dot_general(a, b, (((0,), (0,)), ((), ())), preferred_element_type=F32)


def _split3(a):
    hi = a.astype(BF16)
    r = a - hi.astype(F32)
    mid = r.astype(BF16)
    return hi, mid, (r - mid.astype(F32)).astype(BF16)


def _dot_x3(a, b):
    a_hi, a_lo, _ = _split3(a)
    b_hi, b_lo, _ = _split3(b)
    d = lambda p, q: jnp.dot(p, q, preferred_element_type=F32)
    return d(a_hi, b_hi) + (d(a_lo, b_hi) + d(a_hi, b_lo))


def _cumsum_rows(tri, g):
    return sum(jnp.dot(tri, t, preferred_element_type=F32) for t in reversed(_split3(g)))


def _unit_lower_inverses(mats):
    c = mats[0].shape[0]
    eye = jnp.where(lax.broadcasted_iota(jnp.int32, (c, c), 0) == lax.broadcasted_iota(jnp.int32, (c, c), 1), 1.0, 0.0)
    ps = [-a for a in mats]
    ts = [eye + p for p in ps]
    k = 2
    while k < c:
        ps = [_dot_x3(p, p) for p in ps]
        ts = [t + _dot_x3(t, p) for t, p in zip(ts, ps)]
        k *= 2
    return ts


GDN_TC = 256
GDN_HALO = 8
GDN_SMALL_BETA = 0
GDN_SMALL_DECAY = GDN_HEADS
NSA_SMALL_GATE = 2 * GDN_HEADS


def _gdn_kernel(x_ref, gate_ref, small_ref, cw_ref, alog_ref, dtb_ref, onorm_ref, tri_ref, o_ref, xext_ref, state_ref):
    t = pl.program_id(1)
    C = GDN_CHUNK

    @pl.when(t == 0)
    def _():
        xext_ref[0:GDN_HALO, :] = jnp.zeros((GDN_HALO, xext_ref.shape[1]), F32)
        state_ref[...] = jnp.zeros(state_ref.shape, F32)

    xext_ref[GDN_HALO:, :] = x_ref[...]
    small = small_ref[...]
    beta_all = jax.nn.sigmoid(small)
    z = small + dtb_ref[...]
    softplus = jnp.maximum(z, 0.0) + jnp.log(1.0 + jnp.exp(-jnp.abs(z)))
    g_all = -jnp.exp(alog_ref[...]) * softplus
    tri = tri_ref[...]
    row = lax.broadcasted_iota(jnp.int32, (C, C), 0)
    col = lax.broadcasted_iota(jnp.int32, (C, C), 1)

    def conv_silu(c0, width):
        xe = xext_ref[:, c0:c0 + width]
        w = cw_ref[:, c0:c0 + width]
        y = xe * w[GDN_CONV - 1:GDN_CONV]
        for s in range(1, GDN_CONV):
            y = y + pltpu.roll(xe, s, 0) * w[GDN_CONV - 1 - s:GDN_CONV - s]
        y = y[GDN_HALO:]
        return y * jax.nn.sigmoid(y)

    n_chunks = GDN_TC // C
    gsums = [_cumsum_rows(tri, g_all[c * C:(c + 1) * C]) for c in range(n_chunks)]
    gsum_ts = [jnp.transpose(gs) for gs in gsums]

    work = {}
    for h in range(GDN_HEADS):
        q_all = conv_silu(h * GDN_DK, GDN_DK)
        k_all = conv_silu(GDN_QK + h * GDN_DK, GDN_DK)
        v_all = conv_silu(2 * GDN_QK + h * GDN_DV, GDN_DV)
        q_all = q_all * (lax.rsqrt(jnp.sum(q_all * q_all, axis=-1, keepdims=True) + EPS) * GDN_DK ** -0.5)
        k_all = k_all * lax.rsqrt(jnp.sum(k_all * k_all, axis=-1, keepdims=True) + EPS)
        for c in range(n_chunks):
            rows = slice(c * C, (c + 1) * C)
            q, k, v = q_all[rows], k_all[rows], v_all[rows]
            beta = beta_all[rows, GDN_SMALL_BETA + h:GDN_SMALL_BETA + h + 1]
            gcol = gsums[c][:, GDN_SMALL_DECAY + h:GDN_SMALL_DECAY + h + 1]
            grow = gsum_ts[c][GDN_SMALL_DECAY + h:GDN_SMALL_DECAY + h + 1, :]
            glast = gcol[C - 1:C]
            decay = jnp.exp(jnp.where(row >= col, gcol - grow, -jnp.inf))
            kb = k * beta
            kbf = k.astype(BF16)
            eg = jnp.exp(gcol)
            work[h, c] = dict(
                a_kk=jnp.where(row > col, _dot_nt(kb.astype(BF16), kbf) * decay, 0.0),
                a_qk=(_dot_nt(q.astype(BF16), kbf) * decay).astype(BF16),
                rhs=jnp.concatenate([v * beta, kb * eg], axis=-1),
                qg=(q * eg).astype(BF16),
                kd=(k * jnp.exp(glast - gcol)).astype(BF16),
                keep=jnp.exp(glast))
    keys = sorted(work)
    inverses = _unit_lower_inverses([work[key]["a_kk"] for key in keys])
    for key, inv in zip(keys, inverses):
        work[key]["sol"] = _dot_x3(inv, work[key]["rhs"])

    states = [state_ref[h] for h in range(GDN_HEADS)]
    for c in range(n_chunks):
        rows = slice(c * C, (c + 1) * C)
        for h in range(GDN_HEADS):
            wk = work[h, c]
            u, w = wk["sol"][:, :GDN_DV], wk["sol"][:, GDN_DV:]
            sb = states[h].astype(BF16)
            v_new = u - jnp.dot(w.astype(BF16), sb, preferred_element_type=F32)
            vb = v_new.astype(BF16)
            o = (jnp.dot(wk["qg"], sb, preferred_element_type=F32)
                 + jnp.dot(wk["a_qk"], vb, preferred_element_type=F32))
            states[h] = states[h] * wk["keep"] + _dot_tn(wk["kd"], vb)
            gate = gate_ref[rows, h * GDN_DV:(h + 1) * GDN_DV]
            on = o * lax.rsqrt(jnp.mean(o * o, axis=-1, keepdims=True) + EPS) * onorm_ref[...]
            o_ref[rows, h * GDN_DV:(h + 1) * GDN_DV] = (on * (gate * jax.nn.sigmoid(gate))).astype(o_ref.dtype)
    for h in range(GDN_HEADS):
        state_ref[h] = states[h]
    xext_ref[0:GDN_HALO, :] = xext_ref[GDN_TC:GDN_TC + GDN_HALO, :]


def gated_delta_net(p32, conv_w, a_log, dt_bias, o_norm, qkv_col, gate_col, small_col):
    nt = SEQ // GDN_TC
    width = 2 * GDN_QK + GDN_V
    lane_vec = lambda a, off: jnp.zeros((1, LANE), F32).at[0, off:off + GDN_HEADS].set(a.astype(F32))
    tri = jnp.asarray(np.tril(np.ones((GDN_CHUNK, GDN_CHUNK))), BF16)
    tok = lambda blk: (lambda b, t: (b * nt + t, blk))
    return pl.pallas_call(
        _gdn_kernel,
        grid=(BATCH, nt),
        in_specs=[pl.BlockSpec((GDN_TC, width), tok(qkv_col // width)),
                  pl.BlockSpec((GDN_TC, GDN_V), tok(gate_col // GDN_V)),
                  pl.BlockSpec((GDN_TC, LANE), tok(small_col // LANE)),
                  _resident((GDN_CONV, width)), _resident((1, LANE)), _resident((1, LANE)), _resident((1, GDN_DV)),
                  _resident((GDN_CHUNK, GDN_CHUNK))],
        out_specs=pl.BlockSpec((GDN_TC, GDN_V), tok(0)),
        out_shape=jax.ShapeDtypeStruct((TOKENS, GDN_V), BF16),
        scratch_shapes=[pltpu.VMEM((GDN_HALO + GDN_TC, width), F32), pltpu.VMEM((GDN_HEADS, GDN_DK, GDN_DV), F32)],
        compiler_params=pltpu.CompilerParams(dimension_semantics=("arbitrary", "arbitrary"),
                                             vmem_limit_bytes=VMEM_LIMIT),
        name="gated_delta_net",
    )(p32, p32, p32, conv_w, lane_vec(a_log, GDN_SMALL_DECAY), lane_vec(dt_bias, GDN_SMALL_DECAY),
      o_norm.reshape(1, GDN_DV), tri)


N_SLC = SEQ // SLC_LEN
N_CMP = SEQ // CMP_STRIDE
NSA_HG = NSA_HEADS // NSA_GROUPS
NSA_TQ = 128
SEL_STEP = 512
WIN_W = WINDOW + NSA_TQ
NSA_SCALE = NSA_DH ** -0.5


def _cmp_kernel(kc_ref, vc_ref, pe_ref, w1_ref, b1_ref, w2_ref, o_ref, strip_ref):
    half = CMP_STRIDE * NSA_DH
    for kv, x_ref in enumerate((kc_ref, vc_ref)):
        for p in range(CMP_STRIDE):
            xp = x_ref[pl.ds(p, N_CMP, stride=CMP_STRIDE), :]
            for g in range(NSA_GROUPS):
                strip_ref[kv * NSA_GROUPS + g, :, p * NSA_DH:(p + 1) * NSA_DH] = xp[:, g * NSA_DH:(g + 1) * NSA_DH]
    for kv in range(2):
        for g in range(NSA_GROUPS):
            rows = strip_ref[kv * NSA_GROUPS + g]
            lo = (rows + pe_ref[kv, :, :half]).astype(BF16)
            hi = (rows + pe_ref[kv, :, half:]).astype(BF16)
            a = jnp.dot(lo, w1_ref[kv, :half, :], preferred_element_type=F32)
            b = jnp.dot(hi, w1_ref[kv, half:, :], preferred_element_type=F32)
            hid = a + pltpu.roll(b, N_CMP - 1, 0) + b1_ref[kv]
            act = (hid * jax.nn.sigmoid(hid)).astype(BF16)
            o_ref[0, kv * NSA_GROUPS + g] = jnp.dot(act, w2_ref[kv], preferred_element_type=F32).astype(BF16)


def nsa_compress(p32, cmp_col, pe, w1, b1, w2):
    n = 2 * NSA_GROUPS
    return pl.pallas_call(
        _cmp_kernel,
        grid=(BATCH,),
        in_specs=[pl.BlockSpec((SEQ, NSA_KV), lambda b: (b, cmp_col // NSA_KV)),
                  pl.BlockSpec((SEQ, NSA_KV), lambda b: (b, cmp_col // NSA_KV + 1)),
                  _resident((2, 1, CMP_LEN * NSA_DH)), _resident((2, CMP_LEN * NSA_DH, CMP_HIDDEN)),
                  _resident((2, 1, CMP_HIDDEN)), _resident((2, CMP_HIDDEN, NSA_DH))],
        out_specs=pl.BlockSpec((1, n, N_CMP, NSA_DH), lambda b: (b, 0, 0, 0)),
        out_shape=jax.ShapeDtypeStruct((BATCH, n, N_CMP, NSA_DH), BF16),
        scratch_shapes=[pltpu.VMEM((n, N_CMP, CMP_STRIDE * NSA_DH), F32)],
        compiler_params=pltpu.CompilerParams(dimension_semantics=("arbitrary",), vmem_limit_bytes=VMEM_LIMIT),
        name="nsa_compress",
    )(p32, p32, pe.reshape(2, 1, -1), w1.astype(BF16), b1.reshape(2, 1, -1), w2.astype(BF16))


V1_ROWS = NSA_DH + 16
NSA_PAIR = 2


def _masked_attention_t(k, q_t, v1_t, bias_t):
    s = jnp.dot(k, q_t, preferred_element_type=F32) + jnp.concatenate([bias_t] * NSA_HG, axis=1)
    p = jnp.exp((s - jnp.max(s, axis=0, keepdims=True)).astype(BF16))
    acc = jnp.dot(v1_t, p, preferred_element_type=F32)
    return acc[:NSA_DH] / acc[NSA_DH:NSA_DH + 1]


def _transposed_with_ones_row(eye, v):
    v_t = _dot_nt(eye, v).astype(v.dtype)
    row = lax.broadcasted_iota(jnp.int32, (V1_ROWS - NSA_DH, v.shape[0]), 0)
    return jnp.concatenate([v_t, jnp.where(row == 0, 1.0, 0.0).astype(v.dtype)], axis=0)


def _nsa_kernel(q_ref, ks_ref, vs_ref, kw_ref, vw_ref, cmp_ref, gate_ref, ovl_ref, eye_ref, expand_ref,
                o_ref, sbias_ref, qt_ref, ocmp_ref, gsel_ref, gwin_ref, vs1_ref, vw1_ref):
    i = pl.program_id(1)
    q0 = i * NSA_TQ
    rows = NSA_HG * NSA_TQ
    eye = eye_ref[...]

    @pl.when(i == 0)
    def _():
        for g in range(NSA_GROUPS):
            lanes = slice(g * NSA_DH, (g + 1) * NSA_DH)
            vs1_ref[g] = _transposed_with_ones_row(eye, vs_ref[:, lanes])
            vw1_ref[g] = _transposed_with_ones_row(eye, vw_ref[:, lanes])

    gates = jax.nn.sigmoid(gate_ref[...])
    gates_t = jnp.transpose(gates)
    tok = q0 + lax.broadcasted_iota(jnp.int32, (NSA_TQ, 1), 0)
    tok_l = q0 + lax.broadcasted_iota(jnp.int32, (1, NSA_TQ), 1)

    for g in range(NSA_GROUPS):
        heads = [g * NSA_HG + h for h in range(NSA_HG)]
        q = jnp.concatenate([q_ref[:, h * NSA_DH:(h + 1) * NSA_DH] for h in heads], axis=0)
        q = q * jnp.asarray(NSA_SCALE, BF16)
        qt_ref[g] = _dot_nt(eye, q).astype(BF16)

        s = _dot_nt(q, cmp_ref[0, g]).reshape(NSA_HG, NSA_TQ, N_CMP)
        cmp_end = lax.broadcasted_iota(jnp.int32, (1, N_CMP), 1) * CMP_STRIDE + (CMP_LEN - 1)
        cvalid = (cmp_end <= tok)[None]
        sm = jnp.where(cvalid, s, NEG)
        e = jnp.exp(sm - jnp.max(sm, axis=-1, keepdims=True))
        p = jnp.where(cvalid, e / jnp.sum(e, axis=-1, keepdims=True), 0.0)
        o_cmp = jnp.dot(p.reshape(rows, N_CMP).astype(BF16), cmp_ref[0, NSA_GROUPS + g], preferred_element_type=F32)

        def gate_col(branch):
            lane = NSA_SMALL_GATE + branch * NSA_HEADS
            return jnp.concatenate([gates[:, lane + h:lane + h + 1] for h in heads], axis=0)

        def gate_row(branch):
            lane = NSA_SMALL_GATE + branch * NSA_HEADS
            return jnp.concatenate([gates_t[lane + h:lane + h + 1, :] for h in heads], axis=1)

        ocmp_ref[g] = gate_col(0) * o_cmp
        gsel_ref[g] = gate_row(1)
        gwin_ref[g] = gate_row(2)

        imp = _dot_nt(ovl_ref[...], jnp.sum(p, axis=0).astype(BF16))
        blk = lax.broadcasted_iota(jnp.int32, (N_SLC, NSA_TQ), 0)
        cur = tok_l // SLC_LEN
        bvalid = blk <= cur
        forced = (blk == 0) | (blk == cur) | (blk == cur - 1)
        score = jnp.where(bvalid, imp + jnp.where(forced, FORCE, 0.0), NEG)
        rank = jnp.zeros((N_SLC, NSA_TQ), F32)
        for j in range(N_SLC):
            sj = score[j:j + 1, :]
            tie = jnp.where((sj == score) & (blk > j), 1.0, 0.0)
            rank = rank + jnp.where(sj > score, 1.0, tie)
        sel_t = jnp.where((rank < SLC_TOPK) & bvalid, 1.0, 0.0).astype(BF16)
        hit_t = jnp.dot(expand_ref[...], sel_t, preferred_element_type=F32)
        kpos = lax.broadcasted_iota(jnp.int32, (SEQ, 1), 0)
        sbias_ref[g] = jnp.where((hit_t > 0.5) & (kpos <= tok_l), 0.0, NEG)

    for n in range(SEQ // SEL_STEP):
        @pl.when(i // (SEL_STEP // NSA_TQ) == n)
        def _():
            w = (n + 1) * SEL_STEP
            w0 = pl.multiple_of(jnp.maximum(q0 + NSA_TQ - WIN_W, 0), NSA_TQ)
            wpos = w0 + lax.broadcasted_iota(jnp.int32, (WIN_W, 1), 0)
            wbias = jnp.where((wpos <= tok_l) & (wpos > tok_l - WINDOW), 0.0, NEG)
            r = pl.ds(w0, WIN_W)
            problems = []
            for g in range(NSA_GROUPS):
                lanes = slice(g * NSA_DH, (g + 1) * NSA_DH)
                for half in range(NSA_HG // NSA_PAIR):
                    cols = slice(half * NSA_PAIR * NSA_TQ, (half + 1) * NSA_PAIR * NSA_TQ)
                    problems.append((g, cols, 0, lambda lanes=lanes: ks_ref[0:w, lanes],
                                     lambda g=g: vs1_ref[g, :, 0:w], lambda g=g: sbias_ref[g, 0:w, :]))
                    problems.append((g, cols, 1, lambda lanes=lanes: kw_ref[r, lanes],
                                     lambda g=g: vw1_ref[g, :, r], lambda: wbias))

            def scores(prob):
                g, cols, _, k, _, bias_t = prob
                return (jnp.dot(k(), qt_ref[g, :, cols], preferred_element_type=F32)
                        + jnp.concatenate([bias_t()] * NSA_PAIR, axis=1))

            def probs(s):
                return jnp.exp((s - jnp.max(s, axis=0, keepdims=True)).astype(BF16))

            def values(prob, p):
                acc = jnp.dot(prob[4](), p, preferred_element_type=F32)
                return acc[:NSA_DH] / acc[NSA_DH:NSA_DH + 1]

            problems = problems[1:2] + problems[0:1] + problems[2:]
            outs = {}
            s_next = scores(problems[0])
            for idx, prob in enumerate(problems):
                s_cur = s_next
                if idx + 1 < len(problems):
                    s_next = scores(problems[idx + 1])
                outs[prob[0], prob[1].start, prob[2]] = values(prob, probs(s_cur))

            for g in range(NSA_GROUPS):
                for half in range(NSA_HG // NSA_PAIR):
                    c0 = half * NSA_PAIR * NSA_TQ
                    cols = slice(c0, c0 + NSA_PAIR * NSA_TQ)
                    o_t = gsel_ref[g, :, cols] * outs[g, c0, 0] + gwin_ref[g, :, cols] * outs[g, c0, 1]
                    for k in range(NSA_PAIR):
                        h = g * NSA_HG + half * NSA_PAIR + k
                        tq = slice(c0 + k * NSA_TQ, c0 + (k + 1) * NSA_TQ)
                        o = ocmp_ref[g, tq, :] + jnp.transpose(o_t[:, k * NSA_TQ:(k + 1) * NSA_TQ])
                        o_ref[:, h * NSA_DH:(h + 1) * NSA_DH] = o.astype(o_ref.dtype)


def _nsa_constants():
    c0 = np.arange(N_CMP)[:, None] * CMP_STRIDE
    s0 = np.arange(N_SLC)[None, :] * SLC_LEN
    overlap = np.clip(np.minimum(c0 + CMP_LEN, s0 + SLC_LEN) - np.maximum(c0, s0), 0, None) / CMP_STRIDE
    overlap[N_CMP - 1] = 0.0
    expand_t = (np.arange(SEQ)[:, None] // SLC_LEN == np.arange(N_SLC)[None, :]).astype(np.float32)
    return jnp.asarray(overlap.T, BF16), jnp.asarray(np.eye(NSA_DH), BF16), jnp.asarray(expand_t, BF16)


def nsa_attention(p16, cmp_kv, p32, q_col, kv_col, small_col):
    nq = SEQ // NSA_TQ
    ovl_t, eye, expand = _nsa_constants()
    kv_spec = lambda n: pl.BlockSpec((SEQ, NSA_KV), lambda b, i: (b, kv_col // NSA_KV + n))
    return pl.pallas_call(
        _nsa_kernel,
        grid=(BATCH, nq),
        in_specs=[pl.BlockSpec((NSA_TQ, NSA_Q), lambda b, i: (b * nq + i, q_col // NSA_Q)),
                  kv_spec(0), kv_spec(1), kv_spec(2), kv_spec(3),
                  pl.BlockSpec((1, 2 * NSA_GROUPS, N_CMP, NSA_DH), lambda b, i: (b, 0, 0, 0)),
                  pl.BlockSpec((NSA_TQ, LANE), lambda b, i: (b * nq + i, small_col // LANE)),
                  _resident(ovl_t.shape), _resident(eye.shape), _resident(expand.shape)],
        out_specs=pl.BlockSpec((NSA_TQ, NSA_Q), lambda b, i: (b * nq + i, 0)),
        out_shape=jax.ShapeDtypeStruct((TOKENS, NSA_Q), BF16),
        scratch_shapes=[pltpu.VMEM((NSA_GROUPS, SEQ, NSA_TQ), F32),
                        pltpu.VMEM((NSA_GROUPS, NSA_DH, NSA_HG * NSA_TQ), BF16),
                        pltpu.VMEM((NSA_GROUPS, NSA_HG * NSA_TQ, NSA_DH), F32)]
        + [pltpu.VMEM((NSA_GROUPS, 1, NSA_HG * NSA_TQ), F32)] * 2
        + [pltpu.VMEM((NSA_GROUPS, V1_ROWS, SEQ), BF16)] * 2,
        compiler_params=pltpu.CompilerParams(dimension_semantics=("arbitrary", "arbitrary"),
                                             vmem_limit_bytes=VMEM_LIMIT),
        name="nsa_attention",
    )(p16, p16, p16, p16, p16, cmp_kv, p32, ovl_t, eye, expand)


EVEN_QKV_COL = 0
EVEN_GATE_COL = 2 * GDN_QK + GDN_V
EVEN_CMP_COL = EVEN_GATE_COL + GDN_V
EVEN_SMALL_COL = EVEN_CMP_COL + 2 * NSA_KV
EVEN_F32_W = EVEN_SMALL_COL + LANE
EVEN_Q_COL = 0
EVEN_KV_COL = NSA_Q
EVEN_BF16_W = NSA_Q + 4 * NSA_KV
assert NSA_SMALL_GATE + 3 * NSA_HEADS <= LANE


def even_proj_weight(w_in):
    qkv_a, beta, dec, gate_a, q_b, kv_b, gate_b = split_last(w_in, EVEN_SPLITS)
    pad = jnp.zeros((w_in.shape[0], LANE - 2 * GDN_HEADS - 3 * NSA_HEADS), w_in.dtype)
    return jnp.concatenate([qkv_a, gate_a, kv_b[:, :2 * NSA_KV], beta, dec, gate_b, pad, q_b, kv_b[:, 2 * NSA_KV:]],
                           axis=1).astype(BF16)


def even_mixer_core(p32, p16, conv_w, A_log, dt_bias, o_norm, cmp_pe, cmp_w1, cmp_b1, cmp_w2):
    oa = gated_delta_net(p32, conv_w, A_log, dt_bias, o_norm, EVEN_QKV_COL, EVEN_GATE_COL, EVEN_SMALL_COL)
    cmp_kv = nsa_compress(p32, EVEN_CMP_COL, cmp_pe, cmp_w1, cmp_b1, cmp_w2)
    ob = nsa_attention(p16, cmp_kv, p32, EVEN_Q_COL, EVEN_KV_COL, EVEN_SMALL_COL)
    return oa, ob


GLA_TC = 256
ODD_Q_COL = 0
ODD_K_COL = GLA_QK
ODD_V_COL = 2 * GLA_QK
ODD_R_COL = ODD_V_COL + GLA_V
ODD_LR_COL = ODD_R_COL + GLA_V
ODD_F32_W = ODD_LR_COL + LANE


def _gla_kernel(q_ref, k_ref, v_ref, r_ref, lr_ref, aw_ref, ab_ref, onorm_ref, tri_ref, o_ref, state_ref):
    t = pl.program_id(1)
    C = GLA_CHUNK

    @pl.when(t == 0)
    def _():
        state_ref[...] = jnp.zeros(state_ref.shape, F32)

    z = jnp.dot(lr_ref[...].astype(BF16), aw_ref[...], preferred_element_type=F32) + ab_ref[...]
    log_a = (jnp.minimum(z, 0.0) - jnp.log(1.0 + jnp.exp(-jnp.abs(z)))) * (1.0 / GLA_TAU)
    tri = tri_ref[...]
    causal = lax.broadcasted_iota(jnp.int32, (C, C), 0) >= lax.broadcasted_iota(jnp.int32, (C, C), 1)

    states = [state_ref[h] for h in range(GLA_HEADS)]
    for c in range(GLA_TC // C):
        rows = slice(c * C, (c + 1) * C)
        b_all = _cumsum_rows(tri, log_a[rows])
        for h in range(GLA_HEADS):
            kl = slice(h * GLA_DK, (h + 1) * GLA_DK)
            vl = slice(h * GLA_DV, (h + 1) * GLA_DV)
            b = b_all[:, kl]
            b_last = b[C - 1:C]
            q = q_ref[rows, kl] * GLA_DK ** -0.5
            k = k_ref[rows, kl]
            v = v_ref[rows, vl].astype(BF16)
            qt = (q * jnp.exp(b)).astype(BF16)
            kt = (k * jnp.exp(-b)).astype(BF16)
            kh = (k * jnp.exp(b_last - b)).astype(BF16)
            intra = jnp.where(causal, _dot_nt(qt, kt), 0.0).astype(BF16)
            o = jnp.dot(intra, v, preferred_element_type=F32) + _dot_nt(qt, states[h].astype(BF16))
            states[h] = states[h] * jnp.exp(b_last) + _dot_tn(v, kh)
            r = r_ref[rows, vl]
            on = o * lax.rsqrt(jnp.mean(o * o, axis=-1, keepdims=True) + EPS) * onorm_ref[...]
            o_ref[rows, vl] = (on * (r * jax.nn.sigmoid(r))).astype(o_ref.dtype)
    for h in range(GLA_HEADS):
        state_ref[h] = states[h]


def gated_linear_attention(p32, a_w2, a_b, o_norm):
    nt = SEQ // GLA_TC
    tok = lambda blk: (lambda b, t: (b * nt + t, blk))
    aw = jnp.pad(a_w2, ((0, LANE - GLA_RANK), (0, 0))).astype(BF16)
    tri = jnp.asarray(np.tril(np.ones((GLA_CHUNK, GLA_CHUNK))), BF16)
    return pl.pallas_call(
        _gla_kernel,
        grid=(BATCH, nt),
        in_specs=[pl.BlockSpec((GLA_TC, GLA_QK), tok(ODD_Q_COL // GLA_QK)),
                  pl.BlockSpec((GLA_TC, GLA_QK), tok(ODD_K_COL // GLA_QK)),
                  pl.BlockSpec((GLA_TC, GLA_V), tok(ODD_V_COL // GLA_V)),
                  pl.BlockSpec((GLA_TC, GLA_V), tok(ODD_R_COL // GLA_V)),
                  pl.BlockSpec((GLA_TC, LANE), tok(ODD_LR_COL // LANE)),
                  _resident((LANE, GLA_QK)), _resident((1, GLA_QK)), _resident((1, GLA_DV)),
                  _resident((GLA_CHUNK, GLA_CHUNK))],
        out_specs=pl.BlockSpec((GLA_TC, GLA_V), tok(0)),
        out_shape=jax.ShapeDtypeStruct((TOKENS, GLA_V), BF16),
        scratch_shapes=[pltpu.VMEM((GLA_HEADS, GLA_DV, GLA_DK), F32)],
        compiler_params=pltpu.CompilerParams(dimension_semantics=("arbitrary", "arbitrary"),
                                             vmem_limit_bytes=VMEM_LIMIT),
        name="gated_linear_attention",
    )(p32, p32, p32, p32, p32, aw, a_b.reshape(1, GLA_QK), o_norm.reshape(1, GLA_DV), tri)


def _pad_cols(w, n):
    return jnp.pad(w, ((0, 0), (0, n - w.shape[1])))


def kernel(x, norm_gains, ffn_w_in, ffn_conv_w, ffn_conv_b, ffn_w_out, hyb_w_in, hyb_w_out, gdn_conv_w, gdn_A_log,
           gdn_dt_bias, gdn_o_norm, nsa_cmp_pe, nsa_cmp_w1, nsa_cmp_b1, nsa_cmp_w2, gla_w_in, gla_a_w2, gla_a_b,
           gla_o_norm, gla_w_out):
    x2 = x.reshape(TOKENS, D_MODEL)
    ffn_w_in_bf = ffn_w_in.astype(BF16)
    ffn_w_out_bf = ffn_w_out.astype(BF16)
    for layer in range(DEPTH):
        if layer % 2 == 0:
            e = layer // 2
            p32, p16 = norm_proj(x2, norm_gains[layer, 0], even_proj_weight(hyb_w_in[e]),
                                 ((EVEN_F32_W, F32), (EVEN_BF16_W, BF16)))
            oa, ob = even_mixer_core(p32, p16, gdn_conv_w[e], gdn_A_log[e], gdn_dt_bias[e], gdn_o_norm[e],
                                     nsa_cmp_pe[e], nsa_cmp_w1[e], nsa_cmp_b1[e], nsa_cmp_w2[e])
            halves = (oa, 0, ob, 0)
            w_mix = hyb_w_out[e]
        else:
            o_ = layer // 2
            w = _pad_cols(gla_w_in[o_], ODD_F32_W).astype(BF16)
            (p32,) = norm_proj(x2, norm_gains[layer, 0], w, ((ODD_F32_W, F32),))
            o = gated_linear_attention(p32, gla_a_w2[o_], gla_a_b[o_], gla_o_norm[o_])
            halves = (o, 0, o, 1)
            w_mix = gla_w_out[o_]
        x2 = mix_out_conv_ffn(*halves, x2, norm_gains[layer, 1], w_mix.astype(BF16), norm_gains[layer, 2],
                              layer, ffn_w_in_bf, ffn_conv_w, ffn_conv_b, ffn_w_out_bf, norm_gains[layer, 3])
    return x2.reshape(BATCH, SEQ, D_MODEL)
```

```python
import functools
import math

import jax
import jax.numpy as jnp
import numpy as np
from jax import lax
from jax.experimental import pallas as pl
from jax.experimental.pallas import tpu as pltpu

D_MODEL = 1024
BATCH = 8
SEQ = 2048
DEPTH = 4
TOKENS = BATCH * SEQ

EPS = 1e-6
NEG = -1e30
FORCE = 1e4

GDN_DK = 128
GDN_DV = 128
GDN_HEADS = 4
GDN_QK = GDN_HEADS * GDN_DK
GDN_V = GDN_HEADS * GDN_DV
GDN_CONV = 4
GDN_CHUNK = 64

NSA_DH = 64
NSA_HEADS = 8
NSA_GROUPS = 2
NSA_Q = NSA_HEADS * NSA_DH
NSA_KV = NSA_GROUPS * NSA_DH
CMP_LEN = 32
CMP_STRIDE = 16
CMP_HIDDEN = 2 * NSA_DH
SLC_LEN = 64
SLC_TOPK = 8
WINDOW = 512
Q_BLOCK = 128

GLA_HEADS = 4
GLA_DK = 128
GLA_DV = 256
GLA_QK = GLA_HEADS * GLA_DK
GLA_V = GLA_HEADS * GLA_DV
GLA_RANK = 16
GLA_TAU = 16.0
GLA_CHUNK = 64

D_FF = 2816
FFN_CONV = 3

EVEN_SPLITS = (2 * GDN_QK + GDN_V, GDN_HEADS, GDN_HEADS, GDN_V, NSA_Q, 6 * NSA_KV, 3 * NSA_HEADS)
ODD_SPLITS = (GLA_QK, GLA_QK, GLA_V, GLA_V, GLA_RANK)

LANE = 128
VMEM_LIMIT = 56 * 1024 * 1024

F32 = jnp.float32
BF16 = jnp.bfloat16


def _round_up(n, m):
    return (n + m - 1) // m * m


def _resident(shape):
    nd = len(shape)
    return pl.BlockSpec(shape, lambda *_: (0,) * nd, pipeline_mode=pl.Buffered(1))


def _resident_layer(shape, layer):
    nd = len(shape)
    return pl.BlockSpec((None,) + tuple(shape), lambda *_: (layer,) + (0,) * nd, pipeline_mode=pl.Buffered(1))


def _rms(x, g):
    ms = jnp.mean(x * x, axis=-1, keepdims=True)
    return x * lax.rsqrt(ms + EPS) * g


PROJ_TM = 256
PROJ_CH = 512


def _proj_kernel(x_ref, g_ref, w_ref, *o_refs):
    h = _rms(x_ref[...], g_ref[...]).astype(BF16)
    col = 0
    for o_ref in o_refs:
        n = o_ref.shape[1]
        for c0 in range(0, n, PROJ_CH):
            cw = min(PROJ_CH, n - c0)
            y = jnp.dot(h, w_ref[:, col + c0:col + c0 + cw], preferred_element_type=F32)
            o_ref[:, c0:c0 + cw] = y.astype(o_ref.dtype)
        col += n


def norm_proj(x2, gain, w, sections):
    T, D = x2.shape
    N = w.shape[1]
    assert N == sum(n for n, _ in sections) and all(n % LANE == 0 for n, _ in sections)
    return pl.pallas_call(
        _proj_kernel,
        grid=(T // PROJ_TM,),
        in_specs=[pl.BlockSpec((PROJ_TM, D), lambda i: (i, 0)), _resident((1, D)), _resident((D, N))],
        out_specs=[pl.BlockSpec((PROJ_TM, n), lambda i: (i, 0)) for n, _ in sections],
        out_shape=[jax.ShapeDtypeStruct((T, n), dt) for n, dt in sections],
        compiler_params=pltpu.CompilerParams(dimension_semantics=("arbitrary",), vmem_limit_bytes=VMEM_LIMIT),
        name="norm_proj",
    )(x2, gain.reshape(1, D), w)


FFN_TM = 512
FFN_HALO = 16
FFN_CH = 256
MIX_HALF = D_MODEL // 2
assert D_FF % FFN_CH == 0


def _ffn_kernel(oa_ref, ob_ref, x_ref, g_mix_ref, w_mix_ref, g_in_ref, w_in_ref, cw_ref, cb_ref, w_out_ref, g_out_ref,
                y_ref, hext_ref, act_ref):
    t = pl.program_id(1)

    @pl.when(t == 0)
    def _():
        hext_ref[0:FFN_HALO, :] = jnp.zeros((FFN_HALO, D_MODEL), BF16)

    mix = (jnp.dot(oa_ref[...], w_mix_ref[0:MIX_HALF, :], preferred_element_type=F32)
           + jnp.dot(ob_ref[...], w_mix_ref[MIX_HALF:, :], preferred_element_type=F32))
    x = x_ref[...] + _rms(mix, g_mix_ref[...])
    hext_ref[FFN_HALO:, :] = _rms(x, g_in_ref[...]).astype(BF16)
    hext = hext_ref[...]

    def conv_half(c0):
        u = jnp.dot(hext, w_in_ref[:, c0:c0 + FFN_CH], preferred_element_type=F32)
        u1 = pltpu.roll(u, 1, 0)
        u2 = pltpu.roll(u, 2, 0)
        w = cw_ref[:, c0:c0 + FFN_CH]
        y = u2 * w[0:1] + u1 * w[1:2] + u * w[2:3] + cb_ref[:, c0:c0 + FFN_CH]
        return y[FFN_HALO:]

    for c0 in range(0, D_FF, FFN_CH):
        half_gate = 0.5 * conv_half(c0)
        val = conv_half(D_FF + c0)
        silu = half_gate + half_gate * jnp.tanh(half_gate)
        act_ref[:, c0:c0 + FFN_CH] = (silu * val).astype(BF16)
    down = jnp.dot(act_ref[...], w_out_ref[...], preferred_element_type=F32)
    y_ref[...] = x + _rms(down, g_out_ref[...])
    hext_ref[0:FFN_HALO, :] = hext_ref[FFN_TM:FFN_TM + FFN_HALO, :]


def mix_out_conv_ffn(oa, a_blk, ob, b_blk, x2, g_mix, w_mix, g_in, layer, w_in, conv_w, conv_b, w_out, g_out):
    T, D = x2.shape
    nt = SEQ // FFN_TM
    tok = lambda blk: (lambda b, t: (b * nt + t, blk))
    return pl.pallas_call(
        _ffn_kernel,
        grid=(BATCH, nt),
        in_specs=[pl.BlockSpec((FFN_TM, MIX_HALF), tok(a_blk)), pl.BlockSpec((FFN_TM, MIX_HALF), tok(b_blk)),
                  pl.BlockSpec((FFN_TM, D), tok(0)), _resident((1, D)), _resident((D, D)),
                  _resident((1, D)), _resident_layer((D, 2 * D_FF), layer),
                  _resident_layer((FFN_CONV, 2 * D_FF), layer), _resident_layer((1, 2 * D_FF), layer),
                  _resident_layer((D_FF, D), layer), _resident((1, D))],
        out_specs=pl.BlockSpec((FFN_TM, D), tok(0)),
        out_shape=jax.ShapeDtypeStruct((T, D), F32),
        scratch_shapes=[pltpu.VMEM((FFN_HALO + FFN_TM, D), BF16), pltpu.VMEM((FFN_TM, D_FF), BF16)],
        compiler_params=pltpu.CompilerParams(dimension_semantics=("arbitrary", "arbitrary"),
                                             vmem_limit_bytes=VMEM_LIMIT),
        name="mix_out_conv_ffn",
    )(oa, ob, x2, g_mix.reshape(1, D), w_mix, g_in.reshape(1, D), w_in, conv_w, conv_b.reshape(DEPTH, 1, -1), w_out,
      g_out.reshape(1, D))


def split_last(x, sizes):
    idx = np.cumsum(sizes)[:-1].tolist()
    return jnp.split(x, idx, axis=-1)


def _dot_nt(a, b):
    return lax.dot_general(a, b, (((1,), (1,)), ((), ())), preferred_element_type=F32)


def _dot_tn(a, b):
    return lax.dot_general(a, b, (((0,), (0,)), ((), ())), preferred_element_type=F32)


def _split3(a):
    hi = a.astype(BF16)
    r = a - hi.astype(F32)
    mid = r.astype(BF16)
    return hi, mid, (r - mid.astype(F32)).astype(BF16)


def _dot_x3(a, b):
    a_hi, a_lo, _ = _split3(a)
    b_hi, b_lo, _ = _split3(b)
    d = lambda p, q: jnp.dot(p, q, preferred_element_type=F32)
    return d(a_hi, b_hi) + (d(a_lo, b_hi) + d(a_hi, b_lo))


def _cumsum_rows(tri, g):
    return sum(jnp.dot(tri, t, preferred_element_type=F32) for t in reversed(_split3(g)))


def _unit_lower_inverses(mats):
    c = mats[0].shape[0]
    eye = jnp.where(lax.broadcasted_iota(jnp.int32, (c, c), 0) == lax.broadcasted_iota(jnp.int32, (c, c), 1), 1.0, 0.0)
    dot = lambda a, b: jnp.dot(a.astype(BF16), b.astype(BF16), preferred_element_type=F32)
    ps = [-a for a in mats]
    ts = [eye + p for p in ps]
    k = 2
    while k < c:
        ps = [dot(p, p) for p in ps]
        ts = [t + dot(t, p) for t, p in zip(ts, ps)]
        k *= 2
    res = [(eye - t) - _dot_x3(a, t) for a, t in zip(mats, ts)]
    return [t + dot(t, r) for t, r in zip(ts, res)]


GDN_TC = 256
GDN_HALO = 8
GDN_SMALL_BETA = 0
GDN_SMALL_DECAY = GDN_HEADS
NSA_SMALL_GATE = 2 * GDN_HEADS


def _gdn_kernel(x_ref, gate_ref, small_ref, cw_ref, alog_ref, dtb_ref, onorm_ref, tri_ref, o_ref, xext_ref, state_ref):
    t = pl.program_id(1)
    C = GDN_CHUNK

    @pl.when(t == 0)
    def _():
        xext_ref[0:GDN_HALO, :] = jnp.zeros((GDN_HALO, xext_ref.shape[1]), F32)
        state_ref[...] = jnp.zeros(state_ref.shape, F32)

    xext_ref[GDN_HALO:, :] = x_ref[...]
    small = small_ref[...]
    beta_all = jax.nn.sigmoid(small)
    z = small + dtb_ref[...]
    softplus = jnp.maximum(z, 0.0) + jnp.log(1.0 + jnp.exp(-jnp.abs(z)))
    g_all = -jnp.exp(alog_ref[...]) * softplus
    tri = tri_ref[...]
    row = lax.broadcasted_iota(jnp.int32, (C, C), 0)
    col = lax.broadcasted_iota(jnp.int32, (C, C), 1)

    def conv_silu(c0, width):
        xe = xext_ref[:, c0:c0 + width]
        w = cw_ref[:, c0:c0 + width]
        y = xe * w[GDN_CONV - 1:GDN_CONV]
        for s in range(1, GDN_CONV):
            y = y + pltpu.roll(xe, s, 0) * w[GDN_CONV - 1 - s:GDN_CONV - s]
        y = y[GDN_HALO:]
        return y * jax.nn.sigmoid(y)

    n_chunks = GDN_TC // C
    gsums = [_cumsum_rows(tri, g_all[c * C:(c + 1) * C]) for c in range(n_chunks)]
    gsum_ts = [jnp.transpose(gs) for gs in gsums]

    work = {}
    for h in range(GDN_HEADS):
        q_all = conv_silu(h * GDN_DK, GDN_DK)
        k_all = conv_silu(GDN_QK + h * GDN_DK, GDN_DK)
        v_all = conv_silu(2 * GDN_QK + h * GDN_DV, GDN_DV)
        q_all = q_all * (lax.rsqrt(jnp.sum(q_all * q_all, axis=-1, keepdims=True) + EPS) * GDN_DK ** -0.5)
        k_all = k_all * lax.rsqrt(jnp.sum(k_all * k_all, axis=-1, keepdims=True) + EPS)
        for c in range(n_chunks):
            rows = slice(c * C, (c + 1) * C)
            q, k, v = q_all[rows], k_all[rows], v_all[rows]
            beta = beta_all[rows, GDN_SMALL_BETA + h:GDN_SMALL_BETA + h + 1]
            gcol = gsums[c][:, GDN_SMALL_DECAY + h:GDN_SMALL_DECAY + h + 1]
            grow = gsum_ts[c][GDN_SMALL_DECAY + h:GDN_SMALL_DECAY + h + 1, :]
            glast = gcol[C - 1:C]
            decay = jnp.exp(jnp.where(row >= col, gcol - grow, -jnp.inf))
            kb = k * beta
            kbf = k.astype(BF16)
            eg = jnp.exp(gcol)
            work[h, c] = dict(
                a_kk=jnp.where(row > col, _dot_nt(kb.astype(BF16), kbf) * decay, 0.0),
                a_qk=(_dot_nt(q.astype(BF16), kbf) * decay).astype(BF16),
                rhs=jnp.concatenate([v * beta, kb * eg], axis=-1),
                qg=q * eg,
                kd_t=jnp.transpose(k * jnp.exp(glast - gcol)).astype(BF16),
                keep=jnp.exp(glast))
    keys = sorted(work)
    inverses = _unit_lower_inverses([work[key]["a_kk"] for key in keys])
    wks = [work[key] for key in keys]
    sols = [_dot_x3(inv, wk["rhs"]) for wk, inv in zip(wks, inverses)]
    us = [sol[:, :GDN_DV].astype(BF16) for sol in sols]
    ws = [sol[:, GDN_DV:].astype(BF16) for sol in sols]
    dot = lambda a, b: jnp.dot(a, b, preferred_element_type=F32)
    for wk, s_in in zip(wks, [dot(wk["kd_t"], u) for wk, u in zip(wks, us)]):
        wk["s_in"] = s_in
    for wk, s_mix in zip(wks, [dot(wk["kd_t"], w) for wk, w in zip(wks, ws)]):
        wk["s_mix"] = s_mix.astype(BF16)
    for wk, o_in in zip(wks, [dot(wk["a_qk"], u) for wk, u in zip(wks, us)]):
        wk["o_in"] = o_in
    for wk, aw in zip(wks, [dot(wk["a_qk"], w) for wk, w in zip(wks, ws)]):
        wk["o_mix"] = (wk["qg"] - aw).astype(BF16)

    states = [state_ref[h] for h in range(GDN_HEADS)]
    for c in range(n_chunks):
        rows = slice(c * C, (c + 1) * C)
        for h in range(GDN_HEADS):
            wk = work[h, c]
            sb = states[h].astype(BF16)
            o = wk["o_in"] + jnp.dot(wk["o_mix"], sb, preferred_element_type=F32)
            states[h] = (states[h] * wk["keep"] + wk["s_in"]) - jnp.dot(wk["s_mix"], sb, preferred_element_type=F32)
            gate = gate_ref[rows, h * GDN_DV:(h + 1) * GDN_DV]
            on = o * lax.rsqrt(jnp.mean(o * o, axis=-1, keepdims=True) + EPS) * onorm_ref[...]
            o_ref[rows, h * GDN_DV:(h + 1) * GDN_DV] = (on * (gate * jax.nn.sigmoid(gate))).astype(o_ref.dtype)
    for h in range(GDN_HEADS):
        state_ref[h] = states[h]
    xext_ref[0:GDN_HALO, :] = xext_ref[GDN_TC:GDN_TC + GDN_HALO, :]


def gated_delta_net(p32, conv_w, a_log, dt_bias, o_norm, qkv_col, gate_col, small_col):
    nt = SEQ // GDN_TC
    width = 2 * GDN_QK + GDN_V
    lane_vec = lambda a, off: jnp.zeros((1, LANE), F32).at[0, off:off + GDN_HEADS].set(a.astype(F32))
    tri = jnp.asarray(np.tril(np.ones((GDN_CHUNK, GDN_CHUNK))), BF16)
    tok = lambda blk: (lambda b, t: (b * nt + t, blk))
    return pl.pallas_call(
        _gdn_kernel,
        grid=(BATCH, nt),
        in_specs=[pl.BlockSpec((GDN_TC, width), tok(qkv_col // width)),
                  pl.BlockSpec((GDN_TC, GDN_V), tok(gate_col // GDN_V)),
                  pl.BlockSpec((GDN_TC, LANE), tok(small_col // LANE)),
                  _resident((GDN_CONV, width)), _resident((1, LANE)), _resident((1, LANE)), _resident((1, GDN_DV)),
                  _resident((GDN_CHUNK, GDN_CHUNK))],
        out_specs=pl.BlockSpec((GDN_TC, GDN_V), tok(0)),
        out_shape=jax.ShapeDtypeStruct((TOKENS, GDN_V), BF16),
        scratch_shapes=[pltpu.VMEM((GDN_HALO + GDN_TC, width), F32), pltpu.VMEM((GDN_HEADS, GDN_DK, GDN_DV), F32)],
        compiler_params=pltpu.CompilerParams(dimension_semantics=("arbitrary", "arbitrary"),
                                             vmem_limit_bytes=VMEM_LIMIT),
        name="gated_delta_net",
    )(p32, p32, p32, conv_w, lane_vec(a_log, GDN_SMALL_DECAY), lane_vec(dt_bias, GDN_SMALL_DECAY),
      o_norm.reshape(1, GDN_DV), tri)


N_SLC = SEQ // SLC_LEN
N_CMP = SEQ // CMP_STRIDE
NSA_HG = NSA_HEADS // NSA_GROUPS
NSA_TQ = 128
SEL_STEP = 512
WIN_W = WINDOW + NSA_TQ
NSA_SCALE = NSA_DH ** -0.5


def _cmp_kernel(kc_ref, vc_ref, pe_ref, w1_ref, b1_ref, w2_ref, o_ref, strip_ref):
    half = CMP_STRIDE * NSA_DH
    for kv, x_ref in enumerate((kc_ref, vc_ref)):
        for p in range(CMP_STRIDE):
            xp = x_ref[pl.ds(p, N_CMP, stride=CMP_STRIDE), :]
            for g in range(NSA_GROUPS):
                strip_ref[kv * NSA_GROUPS + g, :, p * NSA_DH:(p + 1) * NSA_DH] = xp[:, g * NSA_DH:(g + 1) * NSA_DH]
    for kv in range(2):
        for g in range(NSA_GROUPS):
            rows = strip_ref[kv * NSA_GROUPS + g]
            lo = (rows + pe_ref[kv, :, :half]).astype(BF16)
            hi = (rows + pe_ref[kv, :, half:]).astype(BF16)
            a = jnp.dot(lo, w1_ref[kv, :half, :], preferred_element_type=F32)
            b = jnp.dot(hi, w1_ref[kv, half:, :], preferred_element_type=F32)
            hid = a + pltpu.roll(b, N_CMP - 1, 0) + b1_ref[kv]
            act = (hid * jax.nn.sigmoid(hid)).astype(BF16)
            o_ref[0, kv * NSA_GROUPS + g] = jnp.dot(act, w2_ref[kv], preferred_element_type=F32).astype(BF16)


def nsa_compress(p32, cmp_col, pe, w1, b1, w2):
    n = 2 * NSA_GROUPS
    return pl.pallas_call(
        _cmp_kernel,
        grid=(BATCH,),
        in_specs=[pl.BlockSpec((SEQ, NSA_KV), lambda b: (b, cmp_col // NSA_KV)),
                  pl.BlockSpec((SEQ, NSA_KV), lambda b: (b, cmp_col // NSA_KV + 1)),
                  _resident((2, 1, CMP_LEN * NSA_DH)), _resident((2, CMP_LEN * NSA_DH, CMP_HIDDEN)),
                  _resident((2, 1, CMP_HIDDEN)), _resident((2, CMP_HIDDEN, NSA_DH))],
        out_specs=pl.BlockSpec((1, n, N_CMP, NSA_DH), lambda b: (b, 0, 0, 0)),
        out_shape=jax.ShapeDtypeStruct((BATCH, n, N_CMP, NSA_DH), BF16),
        scratch_shapes=[pltpu.VMEM((n, N_CMP, CMP_STRIDE * NSA_DH), F32)],
        compiler_params=pltpu.CompilerParams(dimension_semantics=("arbitrary",), vmem_limit_bytes=VMEM_LIMIT),
        name="nsa_compress",
    )(p32, p32, pe.reshape(2, 1, -1), w1.astype(BF16), b1.reshape(2, 1, -1), w2.astype(BF16))


V1_ROWS = NSA_DH + 16
NSA_PAIR = 2


def _masked_attention_t(k, q_t, v1_t, bias_t):
    s = jnp.dot(k, q_t, preferred_element_type=F32) + jnp.concatenate([bias_t] * NSA_HG, axis=1)
    p = jnp.exp((s - jnp.max(s, axis=0, keepdims=True)).astype(BF16))
    acc = jnp.dot(v1_t, p, preferred_element_type=F32)
    return acc[:NSA_DH] / acc[NSA_DH:NSA_DH + 1]


def _transposed_with_ones_row(eye, v):
    v_t = _dot_nt(eye, v).astype(v.dtype)
    row = lax.broadcasted_iota(jnp.int32, (V1_ROWS - NSA_DH, v.shape[0]), 0)
    return jnp.concatenate([v_t, jnp.where(row == 0, 1.0, 0.0).astype(v.dtype)], axis=0)


def _nsa_kernel(q_ref, ks_ref, vs_ref, kw_ref, vw_ref, cmp_ref, gate_ref, ovl_ref, eye_ref, expand_ref,
                o_ref, sbias_ref, qt_ref, ocmp_ref, gsel_ref, gwin_ref, vs1_ref, vw1_ref):
    i = pl.program_id(1)
    q0 = i * NSA_TQ
    rows = NSA_HG * NSA_TQ
    eye = eye_ref[...]

    @pl.when(i == 0)
    def _():
        for g in range(NSA_GROUPS):
            lanes = slice(g * NSA_DH, (g + 1) * NSA_DH)
            vs1_ref[g] = _transposed_with_ones_row(eye, vs_ref[:, lanes])
            vw1_ref[g] = _transposed_with_ones_row(eye, vw_ref[:, lanes])

    gates = jax.nn.sigmoid(gate_ref[...])
    gates_t = jnp.transpose(gates)
    tok = q0 + lax.broadcasted_iota(jnp.int32, (NSA_TQ, 1), 0)
    tok_l = q0 + lax.broadcasted_iota(jnp.int32, (1, NSA_TQ), 1)

    for g in range(NSA_GROUPS):
        heads = [g * NSA_HG + h for h in range(NSA_HG)]
        q = jnp.concatenate([q_ref[:, h * NSA_DH:(h + 1) * NSA_DH] for h in heads], axis=0)
        q = q * jnp.asarray(NSA_SCALE, BF16)
        qt_ref[g] = _dot_nt(eye, q).astype(BF16)

        s = _dot_nt(q, cmp_ref[0, g]).reshape(NSA_HG, NSA_TQ, N_CMP)
        cmp_end = lax.broadcasted_iota(jnp.int32, (1, N_CMP), 1) * CMP_STRIDE + (CMP_LEN - 1)
        cvalid = (cmp_end <= tok)[None]
        sm = jnp.where(cvalid, s, NEG)
        e = jnp.exp(sm - jnp.max(sm, axis=-1, keepdims=True))
        p = jnp.where(cvalid, e / jnp.sum(e, axis=-1, keepdims=True), 0.0)
        o_cmp = jnp.dot(p.reshape(rows, N_CMP).astype(BF16), cmp_ref[0, NSA_GROUPS + g], preferred_element_type=F32)

        def gate_col(branch):
            lane = NSA_SMALL_GATE + branch * NSA_HEADS
            return jnp.concatenate([gates[:, lane + h:lane + h + 1] for h in heads], axis=0)

        def gate_row(branch):
            lane = NSA_SMALL_GATE + branch * NSA_HEADS
            return jnp.concatenate([gates_t[lane + h:lane + h + 1, :] for h in heads], axis=1)

        ocmp_ref[g] = gate_col(0) * o_cmp
        gsel_ref[g] = gate_row(1)
        gwin_ref[g] = gate_row(2)

        imp = _dot_nt(ovl_ref[...], jnp.sum(p, axis=0).astype(BF16))
        blk = lax.broadcasted_iota(jnp.int32, (N_SLC, NSA_TQ), 0)
        cur = tok_l // SLC_LEN
        bvalid = blk <= cur
        forced = (blk == 0) | (blk == cur) | (blk == cur - 1)
        score = jnp.where(bvalid, imp + jnp.where(forced, FORCE, 0.0), NEG)
        rank = jnp.zeros((N_SLC, NSA_TQ), F32)
        for j in range(N_SLC):
            sj = score[j:j + 1, :]
            tie = jnp.where((sj == score) & (blk > j), 1.0, 0.0)
            rank = rank + jnp.where(sj > score, 1.0, tie)
        sel_t = jnp.where((rank < SLC_TOPK) & bvalid, 1.0, 0.0).astype(BF16)
        hit_t = jnp.dot(expand_ref[...], sel_t, preferred_element_type=F32)
        kpos = lax.broadcasted_iota(jnp.int32, (SEQ, 1), 0)
        sbias_ref[g] = jnp.where((hit_t > 0.5) & (kpos <= tok_l), 0.0, NEG)

    for n in range(SEQ // SEL_STEP):
        @pl.when(i // (SEL_STEP // NSA_TQ) == n)
        def _():
            w = (n + 1) * SEL_STEP
            w0 = pl.multiple_of(jnp.maximum(q0 + NSA_TQ - WIN_W, 0), NSA_TQ)
            wpos = w0 + lax.broadcasted_iota(jnp.int32, (WIN_W, 1), 0)
            wbias = jnp.where((wpos <= tok_l) & (wpos > tok_l - WINDOW), 0.0, NEG)
            r = pl.ds(w0, WIN_W)
            problems = []
            for g in range(NSA_GROUPS):
                lanes = slice(g * NSA_DH, (g + 1) * NSA_DH)
                for half in range(NSA_HG // NSA_PAIR):
                    cols = slice(half * NSA_PAIR * NSA_TQ, (half + 1) * NSA_PAIR * NSA_TQ)
                    problems.append((g, cols, 0, lambda lanes=lanes: ks_ref[0:w, lanes],
                                     lambda g=g: vs1_ref[g, :, 0:w], lambda g=g: sbias_ref[g, 0:w, :]))
                    problems.append((g, cols, 1, lambda lanes=lanes: kw_ref[r, lanes],
                                     lambda g=g: vw1_ref[g, :, r], lambda: wbias))

            def scores(prob):
                g, cols, _, k, _, bias_t = prob
                return (jnp.dot(k(), qt_ref[g, :, cols], preferred_element_type=F32)
                        + jnp.concatenate([bias_t()] * NSA_PAIR, axis=1))

            def probs(s):
                return jnp.exp((s - jnp.max(s, axis=0, keepdims=True)).astype(BF16))

            def values(prob, p):
                acc = jnp.dot(prob[4](), p, preferred_element_type=F32)
                return acc[:NSA_DH] / acc[NSA_DH:NSA_DH + 1]

            problems = problems[1:2] + problems[0:1] + problems[2:]
            outs = {}
            s_next = scores(problems[0])
            for idx, prob in enumerate(problems):
                s_cur = s_next
                if idx + 1 < len(problems):
                    s_next = scores(problems[idx + 1])
                outs[prob[0], prob[1].start, prob[2]] = values(prob, probs(s_cur))

            for g in range(NSA_GROUPS):
                for half in range(NSA_HG // NSA_PAIR):
                    c0 = half * NSA_PAIR * NSA_TQ
                    cols = slice(c0, c0 + NSA_PAIR * NSA_TQ)
                    o_t = gsel_ref[g, :, cols] * outs[g, c0, 0] + gwin_ref[g, :, cols] * outs[g, c0, 1]
                    for k in range(NSA_PAIR):
                        h = g * NSA_HG + half * NSA_PAIR + k
                        tq = slice(c0 + k * NSA_TQ, c0 + (k + 1) * NSA_TQ)
                        o = ocmp_ref[g, tq, :] + jnp.transpose(o_t[:, k * NSA_TQ:(k + 1) * NSA_TQ])
                        o_ref[:, h * NSA_DH:(h + 1) * NSA_DH] = o.astype(o_ref.dtype)


def _nsa_constants():
    c0 = np.arange(N_CMP)[:, None] * CMP_STRIDE
    s0 = np.arange(N_SLC)[None, :] * SLC_LEN
    overlap = np.clip(np.minimum(c0 + CMP_LEN, s0 + SLC_LEN) - np.maximum(c0, s0), 0, None) / CMP_STRIDE
    overlap[N_CMP - 1] = 0.0
    expand_t = (np.arange(SEQ)[:, None] // SLC_LEN == np.arange(N_SLC)[None, :]).astype(np.float32)
    return jnp.asarray(overlap.T, BF16), jnp.asarray(np.eye(NSA_DH), BF16), jnp.asarray(expand_t, BF16)


def nsa_attention(p16, cmp_kv, p32, q_col, kv_col, small_col):
    nq = SEQ // NSA_TQ
    ovl_t, eye, expand = _nsa_constants()
    kv_spec = lambda n: pl.BlockSpec((SEQ, NSA_KV), lambda b, i: (b, kv_col // NSA_KV + n))
    return pl.pallas_call(
        _nsa_kernel,
        grid=(BATCH, nq),
        in_specs=[pl.BlockSpec((NSA_TQ, NSA_Q), lambda b, i: (b * nq + i, q_col // NSA_Q)),
                  kv_spec(0), kv_spec(1), kv_spec(2), kv_spec(3),
                  pl.BlockSpec((1, 2 * NSA_GROUPS, N_CMP, NSA_DH), lambda b, i: (b, 0, 0, 0)),
                  pl.BlockSpec((NSA_TQ, LANE), lambda b, i: (b * nq + i, small_col // LANE)),
                  _resident(ovl_t.shape), _resident(eye.shape), _resident(expand.shape)],
        out_specs=pl.BlockSpec((NSA_TQ, NSA_Q), lambda b, i: (b * nq + i, 0)),
        out_shape=jax.ShapeDtypeStruct((TOKENS, NSA_Q), BF16),
        scratch_shapes=[pltpu.VMEM((NSA_GROUPS, SEQ, NSA_TQ), F32),
                        pltpu.VMEM((NSA_GROUPS, NSA_DH, NSA_HG * NSA_TQ), BF16),
                        pltpu.VMEM((NSA_GROUPS, NSA_HG * NSA_TQ, NSA_DH), F32)]
        + [pltpu.VMEM((NSA_GROUPS, 1, NSA_HG * NSA_TQ), F32)] * 2
        + [pltpu.VMEM((NSA_GROUPS, V1_ROWS, SEQ), BF16)] * 2,
        compiler_params=pltpu.CompilerParams(dimension_semantics=("arbitrary", "arbitrary"),
                                             vmem_limit_bytes=VMEM_LIMIT),
        name="nsa_attention",
    )(p16, p16, p16, p16, p16, cmp_kv, p32, ovl_t, eye, expand)


EVEN_QKV_COL = 0
EVEN_GATE_COL = 2 * GDN_QK + GDN_V
EVEN_CMP_COL = EVEN_GATE_COL + GDN_V
EVEN_SMALL_COL = EVEN_CMP_COL + 2 * NSA_KV
EVEN_F32_W = EVEN_SMALL_COL + LANE
EVEN_Q_COL = 0
EVEN_KV_COL = NSA_Q
EVEN_BF16_W = NSA_Q + 4 * NSA_KV
assert NSA_SMALL_GATE + 3 * NSA_HEADS <= LANE


def even_proj_weight(w_in):
    qkv_a, beta, dec, gate_a, q_b, kv_b, gate_b = split_last(w_in, EVEN_SPLITS)
    pad = jnp.zeros((w_in.shape[0], LANE - 2 * GDN_HEADS - 3 * NSA_HEADS), w_in.dtype)
    return jnp.concatenate([qkv_a, gate_a, kv_b[:, :2 * NSA_KV], beta, dec, gate_b, pad, q_b, kv_b[:, 2 * NSA_KV:]],
                           axis=1).astype(BF16)


def even_mixer_core(p32, p16, conv_w, A_log, dt_bias, o_norm, cmp_pe, cmp_w1, cmp_b1, cmp_w2):
    oa = gated_delta_net(p32, conv_w, A_log, dt_bias, o_norm, EVEN_QKV_COL, EVEN_GATE_COL, EVEN_SMALL_COL)
    cmp_kv = nsa_compress(p32, EVEN_CMP_COL, cmp_pe, cmp_w1, cmp_b1, cmp_w2)
    ob = nsa_attention(p16, cmp_kv, p32, EVEN_Q_COL, EVEN_KV_COL, EVEN_SMALL_COL)
    return oa, ob


GLA_TC = 256
ODD_Q_COL = 0
ODD_K_COL = GLA_QK
ODD_V_COL = 2 * GLA_QK
ODD_R_COL = ODD_V_COL + GLA_V
ODD_LR_COL = ODD_R_COL + GLA_V
ODD_F32_W = ODD_LR_COL + LANE


def _gla_kernel(q_ref, k_ref, v_ref, r_ref, lr_ref, aw_ref, ab_ref, onorm_ref, tri_ref, o_ref, state_ref):
    t = pl.program_id(1)
    C = GLA_CHUNK

    @pl.when(t == 0)
    def _():
        state_ref[...] = jnp.zeros(state_ref.shape, F32)

    z = jnp.dot(lr_ref[...].astype(BF16), aw_ref[...], preferred_element_type=F32) + ab_ref[...]
    log_a = (jnp.minimum(z, 0.0) - jnp.log(1.0 + jnp.exp(-jnp.abs(z)))) * (1.0 / GLA_TAU)
    tri = tri_ref[...]
    causal = lax.broadcasted_iota(jnp.int32, (C, C), 0) >= lax.broadcasted_iota(jnp.int32, (C, C), 1)

    states = [state_ref[h] for h in range(GLA_HEADS)]
    for c in range(GLA_TC // C):
        rows = slice(c * C, (c + 1) * C)
        b_all = _cumsum_rows(tri, log_a[rows])
        for h in range(GLA_HEADS):
            kl = slice(h * GLA_DK, (h + 1) * GLA_DK)
            vl = slice(h * GLA_DV, (h + 1) * GLA_DV)
            b = b_all[:, kl]
            b_last = b[C - 1:C]
            q = q_ref[rows, kl] * GLA_DK ** -0.5
            k = k_ref[rows, kl]
            v = v_ref[rows, vl].astype(BF16)
            qt = (q * jnp.exp(b)).astype(BF16)
            kt = (k * jnp.exp(-b)).astype(BF16)
            kh = (k * jnp.exp(b_last - b)).astype(BF16)
            intra = jnp.where(causal, _dot_nt(qt, kt), 0.0).astype(BF16)
            o = jnp.dot(intra, v, preferred_element_type=F32) + _dot_nt(qt, states[h].astype(BF16))
            states[h] = states[h] * jnp.exp(b_last) + _dot_tn(v, kh)
            r = r_ref[rows, vl]
            on = o * lax.rsqrt(jnp.mean(o * o, axis=-1, keepdims=True) + EPS) * onorm_ref[...]
            o_ref[rows, vl] = (on * (r * jax.nn.sigmoid(r))).astype(o_ref.dtype)
    for h in range(GLA_HEADS):
        state_ref[h] = states[h]


def gated_linear_attention(p32, a_w2, a_b, o_norm):
    nt = SEQ // GLA_TC
    tok = lambda blk: (lambda b, t: (b * nt + t, blk))
    aw = jnp.pad(a_w2, ((0, LANE - GLA_RANK), (0, 0))).astype(BF16)
    tri = jnp.asarray(np.tril(np.ones((GLA_CHUNK, GLA_CHUNK))), BF16)
    return pl.pallas_call(
        _gla_kernel,
        grid=(BATCH, nt),
        in_specs=[pl.BlockSpec((GLA_TC, GLA_QK), tok(ODD_Q_COL // GLA_QK)),
                  pl.BlockSpec((GLA_TC, GLA_QK), tok(ODD_K_COL // GLA_QK)),
                  pl.BlockSpec((GLA_TC, GLA_V), tok(ODD_V_COL // GLA_V)),
                  pl.BlockSpec((GLA_TC, GLA_V), tok(ODD_R_COL // GLA_V)),
                  pl.BlockSpec((GLA_TC, LANE), tok(ODD_LR_COL // LANE)),
                  _resident((LANE, GLA_QK)), _resident((1, GLA_QK)), _resident((1, GLA_DV)),
                  _resident((GLA_CHUNK, GLA_CHUNK))],
        out_specs=pl.BlockSpec((GLA_TC, GLA_V), tok(0)),
        out_shape=jax.ShapeDtypeStruct((TOKENS, GLA_V), BF16),
        scratch_shapes=[pltpu.VMEM((GLA_HEADS, GLA_DV, GLA_DK), F32)],
        compiler_params=pltpu.CompilerParams(dimension_semantics=("arbitrary", "arbitrary"),
                                             vmem_limit_bytes=VMEM_LIMIT),
        name="gated_linear_attention",
    )(p32, p32, p32, p32, p32, aw, a_b.reshape(1, GLA_QK), o_norm.reshape(1, GLA_DV), tri)


def _pad_cols(w, n):
    return jnp.pad(w, ((0, 0), (0, n - w.shape[1])))


def kernel(x, norm_gains, ffn_w_in, ffn_conv_w, ffn_conv_b, ffn_w_out, hyb_w_in, hyb_w_out, gdn_conv_w, gdn_A_log,
           gdn_dt_bias, gdn_o_norm, nsa_cmp_pe, nsa_cmp_w1, nsa_cmp_b1, nsa_cmp_w2, gla_w_in, gla_a_w2, gla_a_b,
           gla_o_norm, gla_w_out):
    x2 = x.reshape(TOKENS, D_MODEL)
    ffn_w_in_bf = ffn_w_in.astype(BF16)
    ffn_w_out_bf = ffn_w_out.astype(BF16)
    for layer in range(DEPTH):
        if layer % 2 == 0:
            e = layer // 2
            p32, p16 = norm_proj(x2, norm_gains[layer, 0], even_proj_weight(hyb_w_in[e]),
                                 ((EVEN_F32_W, F32), (EVEN_BF16_W, BF16)))
            oa, ob = even_mixer_core(p32, p16, gdn_conv_w[e], gdn_A_log[e], gdn_dt_bias[e], gdn_o_norm[e],
                                     nsa_cmp_pe[e], nsa_cmp_w1[e], nsa_cmp_b1[e], nsa_cmp_w2[e])
            halves = (oa, 0, ob, 0)
            w_mix = hyb_w_out[e]
        else:
            o_ = layer // 2
            w = _pad_cols(gla_w_in[o_], ODD_F32_W).astype(BF16)
            (p32,) = norm_proj(x2, norm_gains[layer, 0], w, ((ODD_F32_W, F32),))
            o = gated_linear_attention(p32, gla_a_w2[o_], gla_a_b[o_], gla_o_norm[o_])
            halves = (o, 0, o, 1)
            w_mix = gla_w_out[o_]
        x2 = mix_out_conv_ffn(*halves, x2, norm_gains[layer, 1], w_mix.astype(BF16), norm_gains[layer, 2],
                              layer, ffn_w_in_bf, ffn_conv_w, ffn_conv_b, ffn_w_out_bf, norm_gains[layer, 3])
    return x2.reshape(BATCH, SEQ, D_MODEL)
```

```python
import functools
import math

import jax
import jax.numpy as jnp
import numpy as np
from jax import lax
from jax.experimental import pallas as pl
from jax.experimental.pallas import tpu as pltpu

D_MODEL = 1024
BATCH = 8
SEQ = 2048
DEPTH = 4
TOKENS = BATCH * SEQ

EPS = 1e-6
NEG = -1e30
FORCE = 1e4

GDN_DK = 128
GDN_DV = 128
GDN_HEADS = 4
GDN_QK = GDN_HEADS * GDN_DK
GDN_V = GDN_HEADS * GDN_DV
GDN_CONV = 4
GDN_CHUNK = 64

NSA_DH = 64
NSA_HEADS = 8
NSA_GROUPS = 2
NSA_Q = NSA_HEADS * NSA_DH
NSA_KV = NSA_GROUPS * NSA_DH
CMP_LEN = 32
CMP_STRIDE = 16
CMP_HIDDEN = 2 * NSA_DH
SLC_LEN = 64
SLC_TOPK = 8
WINDOW = 512
Q_BLOCK = 128

GLA_HEADS = 4
GLA_DK = 128
GLA_DV = 256
GLA_QK = GLA_HEADS * GLA_DK
GLA_V = GLA_HEADS * GLA_DV
GLA_RANK = 16
GLA_TAU = 16.0
GLA_CHUNK = 64

D_FF = 2816
FFN_CONV = 3

EVEN_SPLITS = (2 * GDN_QK + GDN_V, GDN_HEADS, GDN_HEADS, GDN_V, NSA_Q, 6 * NSA_KV, 3 * NSA_HEADS)
ODD_SPLITS = (GLA_QK, GLA_QK, GLA_V, GLA_V, GLA_RANK)

LANE = 128
VMEM_LIMIT = 56 * 1024 * 1024

F32 = jnp.float32
BF16 = jnp.bfloat16


def _round_up(n, m):
    return (n + m - 1) // m * m


def _resident(shape):
    nd = len(shape)
    return pl.BlockSpec(shape, lambda *_: (0,) * nd, pipeline_mode=pl.Buffered(1))


def _resident_layer(shape, layer):
    nd = len(shape)
    return pl.BlockSpec((None,) + tuple(shape), lambda *_: (layer,) + (0,) * nd, pipeline_mode=pl.Buffered(1))


def _rms(x, g):
    ms = jnp.mean(x * x, axis=-1, keepdims=True)
    return x * lax.rsqrt(ms + EPS) * g


PROJ_TM = 512
PROJ_CH = 512


def _proj_kernel(x_ref, g_ref, w_ref, *o_refs):
    h = _rms(x_ref[...], g_ref[...]).astype(BF16)
    col = 0
    for o_ref in o_refs:
        n = o_ref.shape[1]
        for c0 in range(0, n, PROJ_CH):
            cw = min(PROJ_CH, n - c0)
            y = jnp.dot(h, w_ref[:, col + c0:col + c0 + cw], preferred_element_type=F32)
            o_ref[:, c0:c0 + cw] = y.astype(o_ref.dtype)
        col += n


def norm_proj(x2, gain, w, sections):
    T, D = x2.shape
    N = w.shape[1]
    assert N == sum(n for n, _ in sections) and all(n % LANE == 0 for n, _ in sections)
    return pl.pallas_call(
        _proj_kernel,
        grid=(T // PROJ_TM,),
        in_specs=[pl.BlockSpec((PROJ_TM, D), lambda i: (i, 0)), _resident((1, D)), _resident((D, N))],
        out_specs=[pl.BlockSpec((PROJ_TM, n), lambda i: (i, 0)) for n, _ in sections],
        out_shape=[jax.ShapeDtypeStruct((T, n), dt) for n, dt in sections],
        compiler_params=pltpu.CompilerParams(dimension_semantics=("arbitrary",), vmem_limit_bytes=VMEM_LIMIT),
        name="norm_proj",
    )(x2, gain.reshape(1, D), w)


FFN_TM = 512
FFN_HALO = 16
FFN_CH = 256
MIX_HALF = D_MODEL // 2
assert D_FF % FFN_CH == 0


def _ffn_kernel(oa_ref, ob_ref, x_ref, g_mix_ref, w_mix_ref, g_in_ref, w_in_ref, cw_ref, cb_ref, w_out_ref, g_out_ref,
                y_ref, hext_ref, act_ref):
    t = pl.program_id(1)

    @pl.when(t == 0)
    def _():
        hext_ref[0:FFN_HALO, :] = jnp.zeros((FFN_HALO, D_MODEL), BF16)

    mix = (jnp.dot(oa_ref[...], w_mix_ref[0:MIX_HALF, :], preferred_element_type=F32)
           + jnp.dot(ob_ref[...], w_mix_ref[MIX_HALF:, :], preferred_element_type=F32))
    x = x_ref[...] + _rms(mix, g_mix_ref[...])
    hext_ref[FFN_HALO:, :] = _rms(x, g_in_ref[...]).astype(BF16)
    hext = hext_ref[...]

    def conv_half(c0):
        u = jnp.dot(hext, w_in_ref[:, c0:c0 + FFN_CH], preferred_element_type=F32)
        u1 = pltpu.roll(u, 1, 0)
        u2 = pltpu.roll(u, 2, 0)
        w = cw_ref[:, c0:c0 + FFN_CH]
        y = u2 * w[0:1] + u1 * w[1:2] + u * w[2:3] + cb_ref[:, c0:c0 + FFN_CH]
        return y[FFN_HALO:]

    for c0 in range(0, D_FF, FFN_CH):
        half_gate = 0.5 * conv_half(c0)
        val = conv_half(D_FF + c0)
        silu = half_gate + half_gate * jnp.tanh(half_gate)
        act_ref[:, c0:c0 + FFN_CH] = (silu * val).astype(BF16)
    down = jnp.dot(act_ref[...], w_out_ref[...], preferred_element_type=F32)
    y_ref[...] = x + _rms(down, g_out_ref[...])
    hext_ref[0:FFN_HALO, :] = hext_ref[FFN_TM:FFN_TM + FFN_HALO, :]


def mix_out_conv_ffn(oa, a_blk, ob, b_blk, x2, g_mix, w_mix, g_in, layer, w_in, conv_w, conv_b, w_out, g_out):
    T, D = x2.shape
    nt = SEQ // FFN_TM
    tok = lambda blk: (lambda b, t: (b * nt + t, blk))
    return pl.pallas_call(
        _ffn_kernel,
        grid=(BATCH, nt),
        in_specs=[pl.BlockSpec((FFN_TM, MIX_HALF), tok(a_blk)), pl.BlockSpec((FFN_TM, MIX_HALF), tok(b_blk)),
                  pl.BlockSpec((FFN_TM, D), tok(0)), _resident((1, D)), _resident((D, D)),
                  _resident((1, D)), _resident_layer((D, 2 * D_FF), layer),
                  _resident_layer((FFN_CONV, 2 * D_FF), layer), _resident_layer((1, 2 * D_FF), layer),
                  _resident_layer((D_FF, D), layer), _resident((1, D))],
        out_specs=pl.BlockSpec((FFN_TM, D), tok(0)),
        out_shape=jax.ShapeDtypeStruct((T, D), F32),
        scratch_shapes=[pltpu.VMEM((FFN_HALO + FFN_TM, D), BF16), pltpu.VMEM((FFN_TM, D_FF), BF16)],
        compiler_params=pltpu.CompilerParams(dimension_semantics=("arbitrary", "arbitrary"),
                                             vmem_limit_bytes=VMEM_LIMIT),
        name="mix_out_conv_ffn",
    )(oa, ob, x2, g_mix.reshape(1, D), w_mix, g_in.reshape(1, D), w_in, conv_w, conv_b.reshape(DEPTH, 1, -1), w_out,
      g_out.reshape(1, D))


def split_last(x, sizes):
    idx = np.cumsum(sizes)[:-1].tolist()
    return jnp.split(x, idx, axis=-1)


def _dot_nt(a, b):
    return lax.dot_general(a, b, (((1,), (1,)), ((), ())), preferred_element_type=F32)


def _dot_tn(a, b):
    return lax.dot_general(a, b, (((0,), (0,)), ((), ())), preferred_element_type=F32)


def _split3(a):
    hi = a.astype(BF16)
    r = a - hi.astype(F32)
    mid = r.astype(BF16)
    return hi, mid, (r - mid.astype(F32)).astype(BF16)


def _dot_x3(a, b):
    a_hi, a_lo, _ = _split3(a)
    b_hi, b_lo, _ = _split3(b)
    d = lambda p, q: jnp.dot(p, q, preferred_element_type=F32)
    return d(a_hi, b_hi) + (d(a_lo, b_hi) + d(a_hi, b_lo))


def _cumsum_rows(tri, g):
    return sum(jnp.dot(tri, t, preferred_element_type=F32) for t in reversed(_split3(g)))


def _unit_lower_inverses(mats):
    c = mats[0].shape[0]
    eye = jnp.where(lax.broadcasted_iota(jnp.int32, (c, c), 0) == lax.broadcasted_iota(jnp.int32, (c, c), 1), 1.0, 0.0)
    dot = lambda a, b: jnp.dot(a.astype(BF16), b.astype(BF16), preferred_element_type=F32)
    ps = [-a for a in mats]
    ts = [eye + p for p in ps]
    k = 2
    while k < c:
        ps = [dot(p, p) for p in ps]
        ts = [t + dot(t, p) for t, p in zip(ts, ps)]
        k *= 2
    res = [(eye - t) - _dot_x3(a, t) for a, t in zip(mats, ts)]
    return [t + dot(t, r) for t, r in zip(ts, res)]


GDN_TC = 256
GDN_HALO = 8
GDN_SMALL_BETA = 0
GDN_SMALL_DECAY = GDN_HEADS
NSA_SMALL_GATE = 2 * GDN_HEADS


def _gdn_kernel(x_ref, gate_ref, small_ref, cw_ref, alog_ref, dtb_ref, onorm_ref, tri_ref, o_ref, xext_ref, state_ref):
    t = pl.program_id(1)
    C = GDN_CHUNK

    @pl.when(t == 0)
    def _():
        xext_ref[0:GDN_HALO, :] = jnp.zeros((GDN_HALO, xext_ref.shape[1]), F32)
        state_ref[...] = jnp.zeros(state_ref.shape, F32)

    xext_ref[GDN_HALO:, :] = x_ref[...]
    small = small_ref[...]
    beta_all = jax.nn.sigmoid(small)
    z = small + dtb_ref[...]
    softplus = jnp.maximum(z, 0.0) + jnp.log(1.0 + jnp.exp(-jnp.abs(z)))
    g_all = -jnp.exp(alog_ref[...]) * softplus
    tri = tri_ref[...]
    row = lax.broadcasted_iota(jnp.int32, (C, C), 0)
    col = lax.broadcasted_iota(jnp.int32, (C, C), 1)

    def conv_silu(c0, width):
        xe = xext_ref[:, c0:c0 + width]
        w = cw_ref[:, c0:c0 + width]
        y = xe * w[GDN_CONV - 1:GDN_CONV]
        for s in range(1, GDN_CONV):
            y = y + pltpu.roll(xe, s, 0) * w[GDN_CONV - 1 - s:GDN_CONV - s]
        y = y[GDN_HALO:]
        return y * jax.nn.sigmoid(y)

    n_chunks = GDN_TC // C
    gsums = [_cumsum_rows(tri, g_all[c * C:(c + 1) * C]) for c in range(n_chunks)]
    gsum_ts = [jnp.transpose(gs) for gs in gsums]

    work = {}
    for h in range(GDN_HEADS):
        q_all = conv_silu(h * GDN_DK, GDN_DK)
        k_all = conv_silu(GDN_QK + h * GDN_DK, GDN_DK)
        v_all = conv_silu(2 * GDN_QK + h * GDN_DV, GDN_DV)
        q_all = q_all * (lax.rsqrt(jnp.sum(q_all * q_all, axis=-1, keepdims=True) + EPS) * GDN_DK ** -0.5)
        k_all = k_all * lax.rsqrt(jnp.sum(k_all * k_all, axis=-1, keepdims=True) + EPS)
        for c in range(n_chunks):
            rows = slice(c * C, (c + 1) * C)
            q, k, v = q_all[rows], k_all[rows], v_all[rows]
            beta = beta_all[rows, GDN_SMALL_BETA + h:GDN_SMALL_BETA + h + 1]
            gcol = gsums[c][:, GDN_SMALL_DECAY + h:GDN_SMALL_DECAY + h + 1]
            grow = gsum_ts[c][GDN_SMALL_DECAY + h:GDN_SMALL_DECAY + h + 1, :]
            glast = gcol[C - 1:C]
            decay = jnp.exp(jnp.where(row >= col, gcol - grow, -jnp.inf))
            kb = k * beta
            kbf = k.astype(BF16)
            eg = jnp.exp(gcol)
            work[h, c] = dict(
                a_kk=jnp.where(row > col, _dot_nt(kb.astype(BF16), kbf) * decay, 0.0),
                a_qk=(_dot_nt(q.astype(BF16), kbf) * decay).astype(BF16),
                rhs=jnp.concatenate([v * beta, kb * eg], axis=-1),
                qg=q * eg,
                kd_t=jnp.transpose(k * jnp.exp(glast - gcol)).astype(BF16),
                keep=jnp.exp(glast))
    keys = sorted(work)
    inverses = _unit_lower_inverses([work[key]["a_kk"] for key in keys])
    wks = [work[key] for key in keys]
    sols = [_dot_x3(inv, wk["rhs"]) for wk, inv in zip(wks, inverses)]
    us = [sol[:, :GDN_DV].astype(BF16) for sol in sols]
    ws = [sol[:, GDN_DV:].astype(BF16) for sol in sols]
    dot = lambda a, b: jnp.dot(a, b, preferred_element_type=F32)
    for wk, s_in in zip(wks, [dot(wk["kd_t"], u) for wk, u in zip(wks, us)]):
        wk["s_in"] = s_in
    for wk, s_mix in zip(wks, [dot(wk["kd_t"], w) for wk, w in zip(wks, ws)]):
        wk["s_mix"] = s_mix.astype(BF16)
    for wk, o_in in zip(wks, [dot(wk["a_qk"], u) for wk, u in zip(wks, us)]):
        wk["o_in"] = o_in
    for wk, aw in zip(wks, [dot(wk["a_qk"], w) for wk, w in zip(wks, ws)]):
        wk["o_mix"] = (wk["qg"] - aw).astype(BF16)

    states = [state_ref[h] for h in range(GDN_HEADS)]
    for c in range(n_chunks):
        rows = slice(c * C, (c + 1) * C)
        for h in range(GDN_HEADS):
            wk = work[h, c]
            sb = states[h].astype(BF16)
            o = wk["o_in"] + jnp.dot(wk["o_mix"], sb, preferred_element_type=F32)
            states[h] = (states[h] * wk["keep"] + wk["s_in"]) - jnp.dot(wk["s_mix"], sb, preferred_element_type=F32)
            gate = gate_ref[rows, h * GDN_DV:(h + 1) * GDN_DV]
            on = o * lax.rsqrt(jnp.mean(o * o, axis=-1, keepdims=True) + EPS) * onorm_ref[...]
            o_ref[rows, h * GDN_DV:(h + 1) * GDN_DV] = (on * (gate * jax.nn.sigmoid(gate))).astype(o_ref.dtype)
    for h in range(GDN_HEADS):
        state_ref[h] = states[h]
    xext_ref[0:GDN_HALO, :] = xext_ref[GDN_TC:GDN_TC + GDN_HALO, :]


def gated_delta_net(p32, conv_w, a_log, dt_bias, o_norm, qkv_col, gate_col, small_col):
    nt = SEQ // GDN_TC
    width = 2 * GDN_QK + GDN_V
    lane_vec = lambda a, off: jnp.zeros((1, LANE), F32).at[0, off:off + GDN_HEADS].set(a.astype(F32))
    tri = jnp.asarray(np.tril(np.ones((GDN_CHUNK, GDN_CHUNK))), BF16)
    tok = lambda blk: (lambda b, t: (b * nt + t, blk))
    return pl.pallas_call(
        _gdn_kernel,
        grid=(BATCH, nt),
        in_specs=[pl.BlockSpec((GDN_TC, width), tok(qkv_col // width)),
                  pl.BlockSpec((GDN_TC, GDN_V), tok(gate_col // GDN_V)),
                  pl.BlockSpec((GDN_TC, LANE), tok(small_col // LANE)),
                  _resident((GDN_CONV, width)), _resident((1, LANE)), _resident((1, LANE)), _resident((1, GDN_DV)),
                  _resident((GDN_CHUNK, GDN_CHUNK))],
        out_specs=pl.BlockSpec((GDN_TC, GDN_V), tok(0)),
        out_shape=jax.ShapeDtypeStruct((TOKENS, GDN_V), BF16),
        scratch_shapes=[pltpu.VMEM((GDN_HALO + GDN_TC, width), F32), pltpu.VMEM((GDN_HEADS, GDN_DK, GDN_DV), F32)],
        compiler_params=pltpu.CompilerParams(dimension_semantics=("arbitrary", "arbitrary"),
                                             vmem_limit_bytes=VMEM_LIMIT),
        name="gated_delta_net",
    )(p32, p32, p32, conv_w, lane_vec(a_log, GDN_SMALL_DECAY), lane_vec(dt_bias, GDN_SMALL_DECAY),
      o_norm.reshape(1, GDN_DV), tri)


N_SLC = SEQ // SLC_LEN
N_CMP = SEQ // CMP_STRIDE
NSA_HG = NSA_HEADS // NSA_GROUPS
NSA_TQ = 128
SEL_STEP = 512
WIN_W = WINDOW + NSA_TQ
NSA_SCALE = NSA_DH ** -0.5


def _cmp_kernel(kc_ref, vc_ref, pe_ref, w1_ref, b1_ref, w2_ref, o_ref, strip_ref):
    half = CMP_STRIDE * NSA_DH
    for kv, x_ref in enumerate((kc_ref, vc_ref)):
        for p in range(CMP_STRIDE):
            xp = x_ref[pl.ds(p, N_CMP, stride=CMP_STRIDE), :]
            for g in range(NSA_GROUPS):
                strip_ref[kv * NSA_GROUPS + g, :, p * NSA_DH:(p + 1) * NSA_DH] = xp[:, g * NSA_DH:(g + 1) * NSA_DH]
    for kv in range(2):
        for g in range(NSA_GROUPS):
            rows = strip_ref[kv * NSA_GROUPS + g]
            lo = (rows + pe_ref[kv, :, :half]).astype(BF16)
            hi = (rows + pe_ref[kv, :, half:]).astype(BF16)
            a = jnp.dot(lo, w1_ref[kv, :half, :], preferred_element_type=F32)
            b = jnp.dot(hi, w1_ref[kv, half:, :], preferred_element_type=F32)
            hid = a + pltpu.roll(b, N_CMP - 1, 0) + b1_ref[kv]
            act = (hid * jax.nn.sigmoid(hid)).astype(BF16)
            o_ref[0, kv * NSA_GROUPS + g] = jnp.dot(act, w2_ref[kv], preferred_element_type=F32).astype(BF16)


def nsa_compress(p32, cmp_col, pe, w1, b1, w2):
    n = 2 * NSA_GROUPS
    return pl.pallas_call(
        _cmp_kernel,
        grid=(BATCH,),
        in_specs=[pl.BlockSpec((SEQ, NSA_KV), lambda b: (b, cmp_col // NSA_KV)),
                  pl.BlockSpec((SEQ, NSA_KV), lambda b: (b, cmp_col // NSA_KV + 1)),
                  _resident((2, 1, CMP_LEN * NSA_DH)), _resident((2, CMP_LEN * NSA_DH, CMP_HIDDEN)),
                  _resident((2, 1, CMP_HIDDEN)), _resident((2, CMP_HIDDEN, NSA_DH))],
        out_specs=pl.BlockSpec((1, n, N_CMP, NSA_DH), lambda b: (b, 0, 0, 0)),
        out_shape=jax.ShapeDtypeStruct((BATCH, n, N_CMP, NSA_DH), BF16),
        scratch_shapes=[pltpu.VMEM((n, N_CMP, CMP_STRIDE * NSA_DH), F32)],
        compiler_params=pltpu.CompilerParams(dimension_semantics=("arbitrary",), vmem_limit_bytes=VMEM_LIMIT),
        name="nsa_compress",
    )(p32, p32, pe.reshape(2, 1, -1), w1.astype(BF16), b1.reshape(2, 1, -1), w2.astype(BF16))


V1_ROWS = NSA_DH + 16
NSA_PAIR = 2


def _masked_attention_t(k, q_t, v1_t, bias_t):
    s = jnp.dot(k, q_t, preferred_element_type=F32) + jnp.concatenate([bias_t] * NSA_HG, axis=1)
    p = jnp.exp((s - jnp.max(s, axis=0, keepdims=True)).astype(BF16))
    acc = jnp.dot(v1_t, p, preferred_element_type=F32)
    return acc[:NSA_DH] / acc[NSA_DH:NSA_DH + 1]


def _transposed_with_ones_row(eye, v):
    v_t = _dot_nt(eye, v).astype(v.dtype)
    row = lax.broadcasted_iota(jnp.int32, (V1_ROWS - NSA_DH, v.shape[0]), 0)
    return jnp.concatenate([v_t, jnp.where(row == 0, 1.0, 0.0).astype(v.dtype)], axis=0)


def _nsa_kernel(q_ref, ks_ref, vs_ref, kw_ref, vw_ref, cmp_ref, gate_ref, ovl_ref, eye_ref, expand_ref,
                o_ref, sbias_ref, qt_ref, ocmp_ref, gsel_ref, gwin_ref, vs1_ref, vw1_ref):
    i = pl.program_id(1)
    q0 = i * NSA_TQ
    rows = NSA_HG * NSA_TQ
    eye = eye_ref[...]

    @pl.when(i == 0)
    def _():
        for g in range(NSA_GROUPS):
            lanes = slice(g * NSA_DH, (g + 1) * NSA_DH)
            vs1_ref[g] = _transposed_with_ones_row(eye, vs_ref[:, lanes])
            vw1_ref[g] = _transposed_with_ones_row(eye, vw_ref[:, lanes])

    gates = jax.nn.sigmoid(gate_ref[...])
    gates_t = jnp.transpose(gates)
    tok = q0 + lax.broadcasted_iota(jnp.int32, (NSA_TQ, 1), 0)
    tok_l = q0 + lax.broadcasted_iota(jnp.int32, (1, NSA_TQ), 1)

    for g in range(NSA_GROUPS):
        heads = [g * NSA_HG + h for h in range(NSA_HG)]
        q = jnp.concatenate([q_ref[:, h * NSA_DH:(h + 1) * NSA_DH] for h in heads], axis=0)
        q = q * jnp.asarray(NSA_SCALE, BF16)
        qt_ref[g] = _dot_nt(eye, q).astype(BF16)

        s = _dot_nt(q, cmp_ref[0, g]).reshape(NSA_HG, NSA_TQ, N_CMP)
        cmp_end = lax.broadcasted_iota(jnp.int32, (1, N_CMP), 1) * CMP_STRIDE + (CMP_LEN - 1)
        cvalid = (cmp_end <= tok)[None]
        sm = jnp.where(cvalid, s, NEG)
        e = jnp.exp(sm - jnp.max(sm, axis=-1, keepdims=True))
        p = jnp.where(cvalid, e / jnp.sum(e, axis=-1, keepdims=True), 0.0)
        o_cmp = jnp.dot(p.reshape(rows, N_CMP).astype(BF16), cmp_ref[0, NSA_GROUPS + g], preferred_element_type=F32)

        def gate_col(branch):
            lane = NSA_SMALL_GATE + branch * NSA_HEADS
            return jnp.concatenate([gates[:, lane + h:lane + h + 1] for h in heads], axis=0)

        def gate_row(branch):
            lane = NSA_SMALL_GATE + branch * NSA_HEADS
            return jnp.concatenate([gates_t[lane + h:lane + h + 1, :] for h in heads], axis=1)

        ocmp_ref[g] = gate_col(0) * o_cmp
        gsel_ref[g] = gate_row(1)
        gwin_ref[g] = gate_row(2)

        imp = _dot_nt(ovl_ref[...], jnp.sum(p, axis=0).astype(BF16))
        blk = lax.broadcasted_iota(jnp.int32, (N_SLC, NSA_TQ), 0)
        cur = tok_l // SLC_LEN
        bvalid = blk <= cur
        forced = (blk == 0) | (blk == cur) | (blk == cur - 1)
        score = jnp.where(bvalid, imp + jnp.where(forced, FORCE, 0.0), NEG)
        rank = jnp.zeros((N_SLC, NSA_TQ), F32)
        for j in range(N_SLC):
            sj = score[j:j + 1, :]
            tie = jnp.where((sj == score) & (blk > j), 1.0, 0.0)
            rank = rank + jnp.where(sj > score, 1.0, tie)
        sel_t = jnp.where((rank < SLC_TOPK) & bvalid, 1.0, 0.0).astype(BF16)
        hit_t = jnp.dot(expand_ref[...], sel_t, preferred_element_type=F32)
        kpos = lax.broadcasted_iota(jnp.int32, (SEQ, 1), 0)
        sbias_ref[g] = jnp.where((hit_t > 0.5) & (kpos <= tok_l), 0.0, NEG)

    for n in range(SEQ // SEL_STEP):
        @pl.when(i // (SEL_STEP // NSA_TQ) == n)
        def _():
            w = (n + 1) * SEL_STEP
            w0 = pl.multiple_of(jnp.maximum(q0 + NSA_TQ - WIN_W, 0), NSA_TQ)
            wpos = w0 + lax.broadcasted_iota(jnp.int32, (WIN_W, 1), 0)
            wbias = jnp.where((wpos <= tok_l) & (wpos > tok_l - WINDOW), 0.0, NEG)
            r = pl.ds(w0, WIN_W)
            problems = []
            for g in range(NSA_GROUPS):
                lanes = slice(g * NSA_DH, (g + 1) * NSA_DH)
                for half in range(NSA_HG // NSA_PAIR):
                    cols = slice(half * NSA_PAIR * NSA_TQ, (half + 1) * NSA_PAIR * NSA_TQ)
                    problems.append((g, cols, 0, lambda lanes=lanes: ks_ref[0:w, lanes],
                                     lambda g=g: vs1_ref[g, :, 0:w], lambda g=g: sbias_ref[g, 0:w, :]))
                    problems.append((g, cols, 1, lambda lanes=lanes: kw_ref[r, lanes],
                                     lambda g=g: vw1_ref[g, :, r], lambda: wbias))

            def scores(prob):
                g, cols, _, k, _, bias_t = prob
                return (jnp.dot(k(), qt_ref[g, :, cols], preferred_element_type=F32)
                        + jnp.concatenate([bias_t()] * NSA_PAIR, axis=1))

            def probs(s):
                return jnp.exp((s - jnp.max(s, axis=0, keepdims=True)).astype(BF16))

            def values(prob, p):
                acc = jnp.dot(prob[4](), p, preferred_element_type=F32)
                return acc[:NSA_DH] / acc[NSA_DH:NSA_DH + 1]

            problems = problems[1:2] + problems[0:1] + problems[2:]
            outs = {}
            s_next = scores(problems[0])
            for idx, prob in enumerate(problems):
                s_cur = s_next
                if idx + 1 < len(problems):
                    s_next = scores(problems[idx + 1])
                outs[prob[0], prob[1].start, prob[2]] = values(prob, probs(s_cur))

            for g in range(NSA_GROUPS):
                for half in range(NSA_HG // NSA_PAIR):
                    c0 = half * NSA_PAIR * NSA_TQ
                    cols = slice(c0, c0 + NSA_PAIR * NSA_TQ)
                    o_t = gsel_ref[g, :, cols] * outs[g, c0, 0] + gwin_ref[g, :, cols] * outs[g, c0, 1]
                    for k in range(NSA_PAIR):
                        h = g * NSA_HG + half * NSA_PAIR + k
                        tq = slice(c0 + k * NSA_TQ, c0 + (k + 1) * NSA_TQ)
                        o = ocmp_ref[g, tq, :] + jnp.transpose(o_t[:, k * NSA_TQ:(k + 1) * NSA_TQ])
                        o_ref[:, h * NSA_DH:(h + 1) * NSA_DH] = o.astype(o_ref.dtype)


def _nsa_constants():
    c0 = np.arange(N_CMP)[:, None] * CMP_STRIDE
    s0 = np.arange(N_SLC)[None, :] * SLC_LEN
    overlap = np.clip(np.minimum(c0 + CMP_LEN, s0 + SLC_LEN) - np.maximum(c0, s0), 0, None) / CMP_STRIDE
    overlap[N_CMP - 1] = 0.0
    expand_t = (np.arange(SEQ)[:, None] // SLC_LEN == np.arange(N_SLC)[None, :]).astype(np.float32)
    return jnp.asarray(overlap.T, BF16), jnp.asarray(np.eye(NSA_DH), BF16), jnp.asarray(expand_t, BF16)


def nsa_attention(p16, cmp_kv, p32, q_col, kv_col, small_col):
    nq = SEQ // NSA_TQ
    ovl_t, eye, expand = _nsa_constants()
    kv_spec = lambda n: pl.BlockSpec((SEQ, NSA_KV), lambda b, i: (b, kv_col // NSA_KV + n))
    return pl.pallas_call(
        _nsa_kernel,
        grid=(BATCH, nq),
        in_specs=[pl.BlockSpec((NSA_TQ, NSA_Q), lambda b, i: (b * nq + i, q_col // NSA_Q)),
                  kv_spec(0), kv_spec(1), kv_spec(2), kv_spec(3),
                  pl.BlockSpec((1, 2 * NSA_GROUPS, N_CMP, NSA_DH), lambda b, i: (b, 0, 0, 0)),
                  pl.BlockSpec((NSA_TQ, LANE), lambda b, i: (b * nq + i, small_col // LANE)),
                  _resident(ovl_t.shape), _resident(eye.shape), _resident(expand.shape)],
        out_specs=pl.BlockSpec((NSA_TQ, NSA_Q), lambda b, i: (b * nq + i, 0)),
        out_shape=jax.ShapeDtypeStruct((TOKENS, NSA_Q), BF16),
        scratch_shapes=[pltpu.VMEM((NSA_GROUPS, SEQ, NSA_TQ), F32),
                        pltpu.VMEM((NSA_GROUPS, NSA_DH, NSA_HG * NSA_TQ), BF16),
                        pltpu.VMEM((NSA_GROUPS, NSA_HG * NSA_TQ, NSA_DH), F32)]
        + [pltpu.VMEM((NSA_GROUPS, 1, NSA_HG * NSA_TQ), F32)] * 2
        + [pltpu.VMEM((NSA_GROUPS, V1_ROWS, SEQ), BF16)] * 2,
        compiler_params=pltpu.CompilerParams(dimension_semantics=("arbitrary", "arbitrary"),
                                             vmem_limit_bytes=VMEM_LIMIT),
        name="nsa_attention",
    )(p16, p16, p16, p16, p16, cmp_kv, p32, ovl_t, eye, expand)


EVEN_QKV_COL = 0
EVEN_GATE_COL = 2 * GDN_QK + GDN_V
EVEN_CMP_COL = EVEN_GATE_COL + GDN_V
EVEN_SMALL_COL = EVEN_CMP_COL + 2 * NSA_KV
EVEN_F32_W = EVEN_SMALL_COL + LANE
EVEN_Q_COL = 0
EVEN_KV_COL = NSA_Q
EVEN_BF16_W = NSA_Q + 4 * NSA_KV
assert NSA_SMALL_GATE + 3 * NSA_HEADS <= LANE


def even_proj_weight(w_in):
    qkv_a, beta, dec, gate_a, q_b, kv_b, gate_b = split_last(w_in, EVEN_SPLITS)
    pad = jnp.zeros((w_in.shape[0], LANE - 2 * GDN_HEADS - 3 * NSA_HEADS), w_in.dtype)
    return jnp.concatenate([qkv_a, gate_a, kv_b[:, :2 * NSA_KV], beta, dec, gate_b, pad, q_b, kv_b[:, 2 * NSA_KV:]],
                           axis=1).astype(BF16)


def even_mixer_core(p32, p16, conv_w, A_log, dt_bias, o_norm, cmp_pe, cmp_w1, cmp_b1, cmp_w2):
    oa = gated_delta_net(p32, conv_w, A_log, dt_bias, o_norm, EVEN_QKV_COL, EVEN_GATE_COL, EVEN_SMALL_COL)
    cmp_kv = nsa_compress(p32, EVEN_CMP_COL, cmp_pe, cmp_w1, cmp_b1, cmp_w2)
    ob = nsa_attention(p16, cmp_kv, p32, EVEN_Q_COL, EVEN_KV_COL, EVEN_SMALL_COL)
    return oa, ob


GLA_TC = 256
ODD_Q_COL = 0
ODD_K_COL = GLA_QK
ODD_V_COL = 2 * GLA_QK
ODD_R_COL = ODD_V_COL + GLA_V
ODD_LR_COL = ODD_R_COL + GLA_V
ODD_F32_W = ODD_LR_COL + LANE


def _gla_kernel(q_ref, k_ref, v_ref, r_ref, lr_ref, aw_ref, ab_ref, onorm_ref, tri_ref, o_ref, state_ref):
    t = pl.program_id(1)
    C = GLA_CHUNK

    @pl.when(t == 0)
    def _():
        state_ref[...] = jnp.zeros(state_ref.shape, F32)

    z = jnp.dot(lr_ref[...].astype(BF16), aw_ref[...], preferred_element_type=F32) + ab_ref[...]
    log_a = (jnp.minimum(z, 0.0) - jnp.log(1.0 + jnp.exp(-jnp.abs(z)))) * (1.0 / GLA_TAU)
    tri = tri_ref[...]
    causal = lax.broadcasted_iota(jnp.int32, (C, C), 0) >= lax.broadcasted_iota(jnp.int32, (C, C), 1)

    n_chunks = GLA_TC // C
    b_alls = [_cumsum_rows(tri, log_a[c * C:(c + 1) * C]) for c in range(n_chunks)]
    work = {}
    for c in range(n_chunks):
        rows = slice(c * C, (c + 1) * C)
        for h in range(GLA_HEADS):
            kl = slice(h * GLA_DK, (h + 1) * GLA_DK)
            b = b_alls[c][:, kl]
            b_last = b[C - 1:C]
            k = k_ref[rows, kl]
            work[c, h] = dict(
                rows=rows, vl=slice(h * GLA_DV, (h + 1) * GLA_DV),
                v=v_ref[rows, h * GLA_DV:(h + 1) * GLA_DV].astype(BF16),
                qt=(q_ref[rows, kl] * GLA_DK ** -0.5 * jnp.exp(b)).astype(BF16),
                kt=(k * jnp.exp(-b)).astype(BF16),
                kh=(k * jnp.exp(b_last - b)).astype(BF16),
                keep=jnp.exp(b_last))
    wks = [work[key] for key in sorted(work)]
    intras = [jnp.where(causal, _dot_nt(wk["qt"], wk["kt"]), 0.0).astype(BF16) for wk in wks]
    o_intras = [jnp.dot(intra, wk["v"], preferred_element_type=F32) for wk, intra in zip(wks, intras)]
    updates = [_dot_tn(wk["v"], wk["kh"]) for wk in wks]

    states = [state_ref[h] for h in range(GLA_HEADS)]
    starts = []
    for (c, h), wk, upd in zip(sorted(work), wks, updates):
        starts.append(states[h].astype(BF16))
        states[h] = states[h] * wk["keep"] + upd
    for h in range(GLA_HEADS):
        state_ref[h] = states[h]

    o_inters = [_dot_nt(wk["qt"], s0) for wk, s0 in zip(wks, starts)]
    for wk, o_intra, o_inter in zip(wks, o_intras, o_inters):
        o = o_intra + o_inter
        r = r_ref[wk["rows"], wk["vl"]]
        on = o * lax.rsqrt(jnp.mean(o * o, axis=-1, keepdims=True) + EPS) * onorm_ref[...]
        o_ref[wk["rows"], wk["vl"]] = (on * (r * jax.nn.sigmoid(r))).astype(o_ref.dtype)


def gated_linear_attention(p32, a_w2, a_b, o_norm):
    nt = SEQ // GLA_TC
    tok = lambda blk: (lambda b, t: (b * nt + t, blk))
    aw = jnp.pad(a_w2, ((0, LANE - GLA_RANK), (0, 0))).astype(BF16)
    tri = jnp.asarray(np.tril(np.ones((GLA_CHUNK, GLA_CHUNK))), BF16)
    return pl.pallas_call(
        _gla_kernel,
        grid=(BATCH, nt),
        in_specs=[pl.BlockSpec((GLA_TC, GLA_QK), tok(ODD_Q_COL // GLA_QK)),
                  pl.BlockSpec((GLA_TC, GLA_QK), tok(ODD_K_COL // GLA_QK)),
                  pl.BlockSpec((GLA_TC, GLA_V), tok(ODD_V_COL // GLA_V)),
                  pl.BlockSpec((GLA_TC, GLA_V), tok(ODD_R_COL // GLA_V)),
                  pl.BlockSpec((GLA_TC, LANE), tok(ODD_LR_COL // LANE)),
                  _resident((LANE, GLA_QK)), _resident((1, GLA_QK)), _resident((1, GLA_DV)),
                  _resident((GLA_CHUNK, GLA_CHUNK))],
        out_specs=pl.BlockSpec((GLA_TC, GLA_V), tok(0)),
        out_shape=jax.ShapeDtypeStruct((TOKENS, GLA_V), BF16),
        scratch_shapes=[pltpu.VMEM((GLA_HEADS, GLA_DV, GLA_DK), F32)],
        compiler_params=pltpu.CompilerParams(dimension_semantics=("arbitrary", "arbitrary"),
                                             vmem_limit_bytes=VMEM_LIMIT),
        name="gated_linear_attention",
    )(p32, p32, p32, p32, p32, aw, a_b.reshape(1, GLA_QK), o_norm.reshape(1, GLA_DV), tri)


def _pad_cols(w, n):
    return jnp.pad(w, ((0, 0), (0, n - w.shape[1])))


def kernel(x, norm_gains, ffn_w_in, ffn_conv_w, ffn_conv_b, ffn_w_out, hyb_w_in, hyb_w_out, gdn_conv_w, gdn_A_log,
           gdn_dt_bias, gdn_o_norm, nsa_cmp_pe, nsa_cmp_w1, nsa_cmp_b1, nsa_cmp_w2, gla_w_in, gla_a_w2, gla_a_b,
           gla_o_norm, gla_w_out):
    x2 = x.reshape(TOKENS, D_MODEL)
    ffn_w_in_bf = ffn_w_in.astype(BF16)
    ffn_w_out_bf = ffn_w_out.astype(BF16)
    for layer in range(DEPTH):
        if layer % 2 == 0:
            e = layer // 2
            p32, p16 = norm_proj(x2, norm_gains[layer, 0], even_proj_weight(hyb_w_in[e]),
                                 ((EVEN_F32_W, F32), (EVEN_BF16_W, BF16)))
            oa, ob = even_mixer_core(p32, p16, gdn_conv_w[e], gdn_A_log[e], gdn_dt_bias[e], gdn_o_norm[e],
                                     nsa_cmp_pe[e], nsa_cmp_w1[e], nsa_cmp_b1[e], nsa_cmp_w2[e])
            halves = (oa, 0, ob, 0)
            w_mix = hyb_w_out[e]
        else:
            o_ = layer // 2
            w = _pad_cols(gla_w_in[o_], ODD_F32_W).astype(BF16)
            (p32,) = norm_proj(x2, norm_gains[layer, 0], w, ((ODD_F32_W, F32),))
            o = gated_linear_attention(p32, gla_a_w2[o_], gla_a_b[o_], gla_o_norm[o_])
            halves = (o, 0, o, 1)
            w_mix = gla_w_out[o_]
        x2 = mix_out_conv_ffn(*halves, x2, norm_gains[layer, 1], w_mix.astype(BF16), norm_gains[layer, 2],
                              layer, ffn_w_in_bf, ffn_conv_w, ffn_conv_b, ffn_w_out_bf, norm_gains[layer, 3])
    return x2.reshape(BATCH, SEQ, D_MODEL)
```

```python
import functools
import math

import jax
import jax.numpy as jnp
import numpy as np
from jax import lax
from jax.experimental import pallas as pl
from jax.experimental.pallas import tpu as pltpu

D_MODEL = 1024
BATCH = 8
SEQ = 2048
DEPTH = 4
TOKENS = BATCH * SEQ

EPS = 1e-6
NEG = -1e30
FORCE = 1e4

GDN_DK = 128
GDN_DV = 128
GDN_HEADS = 4
GDN_QK = GDN_HEADS * GDN_DK
GDN_V = GDN_HEADS * GDN_DV
GDN_CONV = 4
GDN_CHUNK = 64

NSA_DH = 64
NSA_HEADS = 8
NSA_GROUPS = 2
NSA_Q = NSA_HEADS * NSA_DH
NSA_KV = NSA_GROUPS * NSA_DH
CMP_LEN = 32
CMP_STRIDE = 16
CMP_HIDDEN = 2 * NSA_DH
SLC_LEN = 64
SLC_TOPK = 8
WINDOW = 512
Q_BLOCK = 128

GLA_HEADS = 4
GLA_DK = 128
GLA_DV = 256
GLA_QK = GLA_HEADS * GLA_DK
GLA_V = GLA_HEADS * GLA_DV
GLA_RANK = 16
GLA_TAU = 16.0
GLA_CHUNK = 64

D_FF = 2816
FFN_CONV = 3

EVEN_SPLITS = (2 * GDN_QK + GDN_V, GDN_HEADS, GDN_HEADS, GDN_V, NSA_Q, 6 * NSA_KV, 3 * NSA_HEADS)
ODD_SPLITS = (GLA_QK, GLA_QK, GLA_V, GLA_V, GLA_RANK)

LANE = 128
VMEM_LIMIT = 56 * 1024 * 1024

F32 = jnp.float32
BF16 = jnp.bfloat16


def _round_up(n, m):
    return (n + m - 1) // m * m


def _resident(shape):
    nd = len(shape)
    return pl.BlockSpec(shape, lambda *_: (0,) * nd, pipeline_mode=pl.Buffered(1))


def _resident_layer(shape, layer):
    nd = len(shape)
    return pl.BlockSpec((None,) + tuple(shape), lambda *_: (layer,) + (0,) * nd, pipeline_mode=pl.Buffered(1))


def _rms(x, g):
    ms = jnp.mean(x * x, axis=-1, keepdims=True)
    return x * lax.rsqrt(ms + EPS) * g


PROJ_TM = 512
PROJ_CH = 512


def _proj_kernel(x_ref, g_ref, w_ref, *o_refs):
    h = _rms(x_ref[...], g_ref[...]).astype(BF16)
    col = 0
    for o_ref in o_refs:
        n = o_ref.shape[1]
        for c0 in range(0, n, PROJ_CH):
            cw = min(PROJ_CH, n - c0)
            y = jnp.dot(h, w_ref[:, col + c0:col + c0 + cw], preferred_element_type=F32)
            o_ref[:, c0:c0 + cw] = y.astype(o_ref.dtype)
        col += n


def norm_proj(x2, gain, w, sections):
    T, D = x2.shape
    N = w.shape[1]
    assert N == sum(n for n, _ in sections) and all(n % LANE == 0 for n, _ in sections)
    return pl.pallas_call(
        _proj_kernel,
        grid=(T // PROJ_TM,),
        in_specs=[pl.BlockSpec((PROJ_TM, D), lambda i: (i, 0)), _resident((1, D)), _resident((D, N))],
        out_specs=[pl.BlockSpec((PROJ_TM, n), lambda i: (i, 0)) for n, _ in sections],
        out_shape=[jax.ShapeDtypeStruct((T, n), dt) for n, dt in sections],
        compiler_params=pltpu.CompilerParams(dimension_semantics=("arbitrary",), vmem_limit_bytes=VMEM_LIMIT),
        name="norm_proj",
    )(x2, gain.reshape(1, D), w)


FFN_TM = 512
FFN_HALO = 16
FFN_CH = 256
MIX_HALF = D_MODEL // 2
assert D_FF % FFN_CH == 0


def _ffn_kernel(oa_ref, ob_ref, x_ref, g_mix_ref, w_mix_ref, g_in_ref, w_in_ref, cw_ref, cb_ref, w_out_ref, g_out_ref,
                y_ref, hext_ref, act_ref):
    t = pl.program_id(1)

    @pl.when(t == 0)
    def _():
        hext_ref[0:FFN_HALO, :] = jnp.zeros((FFN_HALO, D_MODEL), BF16)

    mix = (jnp.dot(oa_ref[...], w_mix_ref[0:MIX_HALF, :], preferred_element_type=F32)
           + jnp.dot(ob_ref[...], w_mix_ref[MIX_HALF:, :], preferred_element_type=F32))
    x = x_ref[...] + _rms(mix, g_mix_ref[...])
    hext_ref[FFN_HALO:, :] = _rms(x, g_in_ref[...]).astype(BF16)
    hext = hext_ref[...]

    def conv_half(c0):
        u = jnp.dot(hext, w_in_ref[:, c0:c0 + FFN_CH], preferred_element_type=F32)
        u1 = pltpu.roll(u, 1, 0)
        u2 = pltpu.roll(u, 2, 0)
        w = cw_ref[:, c0:c0 + FFN_CH]
        y = u2 * w[0:1] + u1 * w[1:2] + u * w[2:3] + cb_ref[:, c0:c0 + FFN_CH]
        return y[FFN_HALO:]

    for c0 in range(0, D_FF, FFN_CH):
        half_gate = 0.5 * conv_half(c0)
        val = conv_half(D_FF + c0)
        silu = half_gate + half_gate * jnp.tanh(half_gate)
        act_ref[:, c0:c0 + FFN_CH] = (silu * val).astype(BF16)
    down = jnp.dot(act_ref[...], w_out_ref[...], preferred_element_type=F32)
    y_ref[...] = x + _rms(down, g_out_ref[...])
    hext_ref[0:FFN_HALO, :] = hext_ref[FFN_TM:FFN_TM + FFN_HALO, :]


def mix_out_conv_ffn(oa, a_blk, ob, b_blk, x2, g_mix, w_mix, g_in, layer, w_in, conv_w, conv_b, w_out, g_out):
    T, D = x2.shape
    nt = SEQ // FFN_TM
    tok = lambda blk: (lambda b, t: (b * nt + t, blk))
    return pl.pallas_call(
        _ffn_kernel,
        grid=(BATCH, nt),
        in_specs=[pl.BlockSpec((FFN_TM, MIX_HALF), tok(a_blk)), pl.BlockSpec((FFN_TM, MIX_HALF), tok(b_blk)),
                  pl.BlockSpec((FFN_TM, D), tok(0)), _resident((1, D)), _resident((D, D)),
                  _resident((1, D)), _resident_layer((D, 2 * D_FF), layer),
                  _resident_layer((FFN_CONV, 2 * D_FF), layer), _resident_layer((1, 2 * D_FF), layer),
                  _resident_layer((D_FF, D), layer), _resident((1, D))],
        out_specs=pl.BlockSpec((FFN_TM, D), tok(0)),
        out_shape=jax.ShapeDtypeStruct((T, D), F32),
        scratch_shapes=[pltpu.VMEM((FFN_HALO + FFN_TM, D), BF16), pltpu.VMEM((FFN_TM, D_FF), BF16)],
        compiler_params=pltpu.CompilerParams(dimension_semantics=("arbitrary", "arbitrary"),
                                             vmem_limit_bytes=VMEM_LIMIT),
        name="mix_out_conv_ffn",
    )(oa, ob, x2, g_mix.reshape(1, D), w_mix, g_in.reshape(1, D), w_in, conv_w, conv_b.reshape(DEPTH, 1, -1), w_out,
      g_out.reshape(1, D))


def split_last(x, sizes):
    idx = np.cumsum(sizes)[:-1].tolist()
    return jnp.split(x, idx, axis=-1)


def _dot_nt(a, b):
    return lax.dot_general(a, b, (((1,), (1,)), ((), ())), preferred_element_type=F32)


def _dot_tn(a, b):
    return lax.dot_general(a, b, (((0,), (0,)), ((), ())), preferred_element_type=F32)


def _split3(a):
    hi = a.astype(BF16)
    r = a - hi.astype(F32)
    mid = r.astype(BF16)
    return hi, mid, (r - mid.astype(F32)).astype(BF16)


def _dot_x3(a, b):
    a_hi, a_lo, _ = _split3(a)
    b_hi, b_lo, _ = _split3(b)
    d = lambda p, q: jnp.dot(p, q, preferred_element_type=F32)
    return d(a_hi, b_hi) + (d(a_lo, b_hi) + d(a_hi, b_lo))


def _cumsum_rows(tri, g):
    return sum(jnp.dot(tri, t, preferred_element_type=F32) for t in reversed(_split3(g)))


def _unit_lower_inverses(mats):
    c = mats[0].shape[0]
    eye = jnp.where(lax.broadcasted_iota(jnp.int32, (c, c), 0) == lax.broadcasted_iota(jnp.int32, (c, c), 1), 1.0, 0.0)
    dot = lambda a, b: jnp.dot(a.astype(BF16), b.astype(BF16), preferred_element_type=F32)
    ps = [-a for a in mats]
    ts = [eye + p for p in ps]
    k = 2
    while k < c:
        ps = [dot(p, p) for p in ps]
        ts = [t + dot(t, p) for t, p in zip(ts, ps)]
        k *= 2
    res = [(eye - t) - _dot_x3(a, t) for a, t in zip(mats, ts)]
    return [t + dot(t, r) for t, r in zip(ts, res)]


GDN_TC = 256
GDN_HALO = 8
GDN_SMALL_BETA = 0
GDN_SMALL_DECAY = GDN_HEADS
NSA_SMALL_GATE = 2 * GDN_HEADS


def _gdn_kernel(x_ref, gate_ref, small_ref, cw_ref, alog_ref, dtb_ref, onorm_ref, tri_ref, o_ref, xext_ref, state_ref):
    t = pl.program_id(1)
    C = GDN_CHUNK

    @pl.when(t == 0)
    def _():
        xext_ref[0:GDN_HALO, :] = jnp.zeros((GDN_HALO, xext_ref.shape[1]), F32)
        state_ref[...] = jnp.zeros(state_ref.shape, F32)

    xext_ref[GDN_HALO:, :] = x_ref[...]
    small = small_ref[...]
    beta_all = jax.nn.sigmoid(small)
    z = small + dtb_ref[...]
    softplus = jnp.maximum(z, 0.0) + jnp.log(1.0 + jnp.exp(-jnp.abs(z)))
    g_all = -jnp.exp(alog_ref[...]) * softplus
    tri = tri_ref[...]
    row = lax.broadcasted_iota(jnp.int32, (C, C), 0)
    col = lax.broadcasted_iota(jnp.int32, (C, C), 1)

    def conv_silu(c0, width):
        xe = xext_ref[:, c0:c0 + width]
        w = cw_ref[:, c0:c0 + width]
        y = xe * w[GDN_CONV - 1:GDN_CONV]
        for s in range(1, GDN_CONV):
            y = y + pltpu.roll(xe, s, 0) * w[GDN_CONV - 1 - s:GDN_CONV - s]
        y = y[GDN_HALO:]
        return y * jax.nn.sigmoid(y)

    n_chunks = GDN_TC // C
    gsums = [_cumsum_rows(tri, g_all[c * C:(c + 1) * C]) for c in range(n_chunks)]
    gsum_ts = [jnp.transpose(gs) for gs in gsums]

    work = {}
    for h in range(GDN_HEADS):
        q_all = conv_silu(h * GDN_DK, GDN_DK)
        k_all = conv_silu(GDN_QK + h * GDN_DK, GDN_DK)
        v_all = conv_silu(2 * GDN_QK + h * GDN_DV, GDN_DV)
        q_all = q_all * (lax.rsqrt(jnp.sum(q_all * q_all, axis=-1, keepdims=True) + EPS) * GDN_DK ** -0.5)
        k_all = k_all * lax.rsqrt(jnp.sum(k_all * k_all, axis=-1, keepdims=True) + EPS)
        for c in range(n_chunks):
            rows = slice(c * C, (c + 1) * C)
            q, k, v = q_all[rows], k_all[rows], v_all[rows]
            beta = beta_all[rows, GDN_SMALL_BETA + h:GDN_SMALL_BETA + h + 1]
            gcol = gsums[c][:, GDN_SMALL_DECAY + h:GDN_SMALL_DECAY + h + 1]
            grow = gsum_ts[c][GDN_SMALL_DECAY + h:GDN_SMALL_DECAY + h + 1, :]
            glast = gcol[C - 1:C]
            decay = jnp.exp(jnp.where(row >= col, gcol - grow, -jnp.inf))
            kb = k * beta
            kbf = k.astype(BF16)
            eg = jnp.exp(gcol)
            work[h, c] = dict(
                a_kk=jnp.where(row > col, _dot_nt(kb.astype(BF16), kbf) * decay, 0.0),
                a_qk=(_dot_nt(q.astype(BF16), kbf) * decay).astype(BF16),
                rhs=jnp.concatenate([v * beta, kb * eg], axis=-1),
                qg=q * eg,
                kd_t=jnp.transpose(k * jnp.exp(glast - gcol)).astype(BF16),
                keep=jnp.exp(glast))
    keys = sorted(work)
    inverses = _unit_lower_inverses([work[key]["a_kk"] for key in keys])
    wks = [work[key] for key in keys]
    sols = [_dot_x3(inv, wk["rhs"]) for wk, inv in zip(wks, inverses)]
    us = [sol[:, :GDN_DV].astype(BF16) for sol in sols]
    ws = [sol[:, GDN_DV:].astype(BF16) for sol in sols]
    dot = lambda a, b: jnp.dot(a, b, preferred_element_type=F32)
    for wk, s_in in zip(wks, [dot(wk["kd_t"], u) for wk, u in zip(wks, us)]):
        wk["s_in"] = s_in
    for wk, s_mix in zip(wks, [dot(wk["kd_t"], w) for wk, w in zip(wks, ws)]):
        wk["s_mix"] = s_mix.astype(BF16)
    for wk, o_in in zip(wks, [dot(wk["a_qk"], u) for wk, u in zip(wks, us)]):
        wk["o_in"] = o_in
    for wk, aw in zip(wks, [dot(wk["a_qk"], w) for wk, w in zip(wks, ws)]):
        wk["o_mix"] = (wk["qg"] - aw).astype(BF16)

    states = [state_ref[h] for h in range(GDN_HEADS)]
    for c in range(n_chunks):
        rows = slice(c * C, (c + 1) * C)
        for h in range(GDN_HEADS):
            wk = work[h, c]
            sb = states[h].astype(BF16)
            o = wk["o_in"] + jnp.dot(wk["o_mix"], sb, preferred_element_type=F32)
            states[h] = (states[h] * wk["keep"] + wk["s_in"]) - jnp.dot(wk["s_mix"], sb, preferred_element_type=F32)
            gate = gate_ref[rows, h * GDN_DV:(h + 1) * GDN_DV]
            on = o * lax.rsqrt(jnp.mean(o * o, axis=-1, keepdims=True) + EPS) * onorm_ref[...]
            o_ref[rows, h * GDN_DV:(h + 1) * GDN_DV] = (on * (gate * jax.nn.sigmoid(gate))).astype(o_ref.dtype)
    for h in range(GDN_HEADS):
        state_ref[h] = states[h]
    xext_ref[0:GDN_HALO, :] = xext_ref[GDN_TC:GDN_TC + GDN_HALO, :]


def gated_delta_net(p32, conv_w, a_log, dt_bias, o_norm, qkv_col, gate_col, small_col):
    nt = SEQ // GDN_TC
    width = 2 * GDN_QK + GDN_V
    lane_vec = lambda a, off: jnp.zeros((1, LANE), F32).at[0, off:off + GDN_HEADS].set(a.astype(F32))
    tri = jnp.asarray(np.tril(np.ones((GDN_CHUNK, GDN_CHUNK))), BF16)
    tok = lambda blk: (lambda b, t: (b * nt + t, blk))
    return pl.pallas_call(
        _gdn_kernel,
        grid=(BATCH, nt),
        in_specs=[pl.BlockSpec((GDN_TC, width), tok(qkv_col // width)),
                  pl.BlockSpec((GDN_TC, GDN_V), tok(gate_col // GDN_V)),
                  pl.BlockSpec((GDN_TC, LANE), tok(small_col // LANE)),
                  _resident((GDN_CONV, width)), _resident((1, LANE)), _resident((1, LANE)), _resident((1, GDN_DV)),
                  _resident((GDN_CHUNK, GDN_CHUNK))],
        out_specs=pl.BlockSpec((GDN_TC, GDN_V), tok(0)),
        out_shape=jax.ShapeDtypeStruct((TOKENS, GDN_V), BF16),
        scratch_shapes=[pltpu.VMEM((GDN_HALO + GDN_TC, width), F32), pltpu.VMEM((GDN_HEADS, GDN_DK, GDN_DV), F32)],
        compiler_params=pltpu.CompilerParams(dimension_semantics=("arbitrary", "arbitrary"),
                                             vmem_limit_bytes=VMEM_LIMIT),
        name="gated_delta_net",
    )(p32, p32, p32, conv_w, lane_vec(a_log, GDN_SMALL_DECAY), lane_vec(dt_bias, GDN_SMALL_DECAY),
      o_norm.reshape(1, GDN_DV), tri)


N_SLC = SEQ // SLC_LEN
N_CMP = SEQ // CMP_STRIDE
NSA_HG = NSA_HEADS // NSA_GROUPS
NSA_TQ = 128
SEL_STEP = 512
WIN_W = WINDOW + NSA_TQ
NSA_SCALE = NSA_DH ** -0.5


def _cmp_kernel(kc_ref, vc_ref, pe_ref, w1_ref, b1_ref, w2_ref, o_ref, strip_ref):
    half = CMP_STRIDE * NSA_DH
    for kv, x_ref in enumerate((kc_ref, vc_ref)):
        for p in range(CMP_STRIDE):
            xp = x_ref[pl.ds(p, N_CMP, stride=CMP_STRIDE), :]
            for g in range(NSA_GROUPS):
                strip_ref[kv * NSA_GROUPS + g, :, p * NSA_DH:(p + 1) * NSA_DH] = xp[:, g * NSA_DH:(g + 1) * NSA_DH]
    for kv in range(2):
        for g in range(NSA_GROUPS):
            rows = strip_ref[kv * NSA_GROUPS + g]
            lo = (rows + pe_ref[kv, :, :half]).astype(BF16)
            hi = (rows + pe_ref[kv, :, half:]).astype(BF16)
            a = jnp.dot(lo, w1_ref[kv, :half, :], preferred_element_type=F32)
            b = jnp.dot(hi, w1_ref[kv, half:, :], preferred_element_type=F32)
            hid = a + pltpu.roll(b, N_CMP - 1, 0) + b1_ref[kv]
            act = (hid * jax.nn.sigmoid(hid)).astype(BF16)
            o_ref[0, kv * NSA_GROUPS + g] = jnp.dot(act, w2_ref[kv], preferred_element_type=F32).astype(BF16)


def nsa_compress(p32, cmp_col, pe, w1, b1, w2):
    n = 2 * NSA_GROUPS
    return pl.pallas_call(
        _cmp_kernel,
        grid=(BATCH,),
        in_specs=[pl.BlockSpec((SEQ, NSA_KV), lambda b: (b, cmp_col // NSA_KV)),
                  pl.BlockSpec((SEQ, NSA_KV), lambda b: (b, cmp_col // NSA_KV + 1)),
                  _resident((2, 1, CMP_LEN * NSA_DH)), _resident((2, CMP_LEN * NSA_DH, CMP_HIDDEN)),
                  _resident((2, 1, CMP_HIDDEN)), _resident((2, CMP_HIDDEN, NSA_DH))],
        out_specs=pl.BlockSpec((1, n, N_CMP, NSA_DH), lambda b: (b, 0, 0, 0)),
        out_shape=jax.ShapeDtypeStruct((BATCH, n, N_CMP, NSA_DH), BF16),
        scratch_shapes=[pltpu.VMEM((n, N_CMP, CMP_STRIDE * NSA_DH), F32)],
        compiler_params=pltpu.CompilerParams(dimension_semantics=("arbitrary",), vmem_limit_bytes=VMEM_LIMIT),
        name="nsa_compress",
    )(p32, p32, pe.reshape(2, 1, -1), w1.astype(BF16), b1.reshape(2, 1, -1), w2.astype(BF16))


V1_ROWS = NSA_DH + 16
NSA_PAIR = 2


def _masked_attention_t(k, q_t, v1_t, bias_t):
    s = jnp.dot(k, q_t, preferred_element_type=F32) + jnp.concatenate([bias_t] * NSA_HG, axis=1)
    p = jnp.exp((s - jnp.max(s, axis=0, keepdims=True)).astype(BF16))
    acc = jnp.dot(v1_t, p, preferred_element_type=F32)
    return acc[:NSA_DH] / acc[NSA_DH:NSA_DH + 1]


def _transposed_with_ones_row(eye, v):
    v_t = _dot_nt(eye, v).astype(v.dtype)
    row = lax.broadcasted_iota(jnp.int32, (V1_ROWS - NSA_DH, v.shape[0]), 0)
    return jnp.concatenate([v_t, jnp.where(row == 0, 1.0, 0.0).astype(v.dtype)], axis=0)


def _nsa_kernel(q_ref, ks_ref, vs_ref, kw_ref, vw_ref, cmp_ref, gate_ref, ovl_ref, eye_ref, expand_ref,
                o_ref, selt_ref, qt_ref, ocmp_ref, gsel_ref, gwin_ref, vs1_ref, vw1_ref, vct_ref):
    i = pl.program_id(1)
    q0 = i * NSA_TQ
    rows = NSA_HG * NSA_TQ
    eye = eye_ref[...]
    eye_dh = eye[0:NSA_DH, 0:NSA_DH]

    @pl.when(i == 0)
    def _():
        for g in range(NSA_GROUPS):
            lanes = slice(g * NSA_DH, (g + 1) * NSA_DH)
            vs1_ref[g] = _transposed_with_ones_row(eye_dh, vs_ref[:, lanes])
            vw1_ref[g] = _transposed_with_ones_row(eye_dh, vw_ref[:, lanes])
            vct_ref[g] = _dot_nt(eye_dh, cmp_ref[0, NSA_GROUPS + g]).astype(BF16)

    gates = jax.nn.sigmoid(gate_ref[...])
    gates_t = jnp.transpose(gates)
    tok = q0 + lax.broadcasted_iota(jnp.int32, (NSA_TQ, 1), 0)
    tok_l = q0 + lax.broadcasted_iota(jnp.int32, (1, NSA_TQ), 1)

    groups = range(NSA_GROUPS)
    scale = jnp.asarray(NSA_SCALE, BF16)
    for pair in range(NSA_HEADS // 2):
        pair_t = _dot_nt(eye, q_ref[:, pair * LANE:(pair + 1) * LANE] * scale).astype(BF16)
        for k in range(2):
            g, hk = divmod(2 * pair + k, NSA_HG)
            qt_ref[g, :, hk * NSA_TQ:(hk + 1) * NSA_TQ] = pair_t[k * NSA_DH:(k + 1) * NSA_DH]

    def gate_row(g, branch):
        lane = NSA_SMALL_GATE + branch * NSA_HEADS + g * NSA_HG
        return jnp.concatenate([gates_t[lane + hk:lane + hk + 1, :] for hk in range(NSA_HG)], axis=1)

    cmp_end = lax.broadcasted_iota(jnp.int32, (N_CMP, 1), 0) * CMP_STRIDE + (CMP_LEN - 1)
    cvalid = cmp_end <= jnp.concatenate([tok_l] * NSA_HG, axis=1)
    ss = [jnp.dot(cmp_ref[0, g], qt_ref[g], preferred_element_type=F32) for g in groups]
    sms = [jnp.where(cvalid, s, NEG) for s in ss]
    es = [jnp.exp(sm - jnp.max(sm, axis=0, keepdims=True)) for sm in sms]
    ps = [jnp.where(cvalid, e / jnp.sum(e, axis=0, keepdims=True), 0.0) for e in es]
    for g in groups:
        ocmp_ref[g] = gate_row(g, 0) * jnp.dot(vct_ref[g], ps[g].astype(BF16), preferred_element_type=F32)
        gsel_ref[g] = gate_row(g, 1)
        gwin_ref[g] = gate_row(g, 2)

    psums = [sum(p[:, hk * NSA_TQ:(hk + 1) * NSA_TQ] for hk in range(NSA_HG)) for p in ps]
    imps = [jnp.dot(ovl_ref[...], psum.astype(BF16), preferred_element_type=F32) for psum in psums]
    blk = lax.broadcasted_iota(jnp.int32, (N_SLC, NSA_TQ), 0)
    cur = tok_l // SLC_LEN
    bvalid = blk <= cur
    forced = (blk == 0) | (blk == cur) | (blk == cur - 1)
    scores_ = [jnp.where(bvalid, imp + jnp.where(forced, FORCE, 0.0), NEG) for imp in imps]
    ranks = [jnp.zeros((N_SLC, NSA_TQ), F32) for _ in groups]
    for j in range(N_SLC):
        for g in groups:
            sj = scores_[g][j:j + 1, :]
            tie = jnp.where((sj == scores_[g]) & (blk > j), 1.0, 0.0)
            ranks[g] = ranks[g] + jnp.where(sj > scores_[g], 1.0, tie)
    for g in groups:
        selt_ref[g] = jnp.where((ranks[g] < SLC_TOPK) & bvalid, 1.0, 0.0).astype(BF16)

    for n in range(SEQ // SEL_STEP):
        @pl.when(i // (SEL_STEP // NSA_TQ) == n)
        def _():
            w = (n + 1) * SEL_STEP
            w0 = pl.multiple_of(jnp.maximum(q0 + NSA_TQ - WIN_W, 0), NSA_TQ)
            wpos = w0 + lax.broadcasted_iota(jnp.int32, (WIN_W, 1), 0)
            wbias = jnp.where((wpos <= tok_l) & (wpos > tok_l - WINDOW), 0.0, NEG)
            r = pl.ds(w0, WIN_W)
            kpos = lax.broadcasted_iota(jnp.int32, (w, 1), 0)
            sel_bias = {}

            def selected_bias(g):
                if g not in sel_bias:
                    hit_t = jnp.dot(expand_ref[0:w, :], selt_ref[g], preferred_element_type=F32)
                    sel_bias[g] = jnp.where((hit_t > 0.5) & (kpos <= tok_l), 0.0, NEG)
                return sel_bias[g]

            problems = []
            for g in range(NSA_GROUPS):
                lanes = slice(g * NSA_DH, (g + 1) * NSA_DH)
                for half in range(NSA_HG // NSA_PAIR):
                    cols = slice(half * NSA_PAIR * NSA_TQ, (half + 1) * NSA_PAIR * NSA_TQ)
                    problems.append((g, cols, 0, lambda lanes=lanes: ks_ref[0:w, lanes],
                                     lambda g=g: vs1_ref[g, :, 0:w], lambda g=g: selected_bias(g)))
                    problems.append((g, cols, 1, lambda lanes=lanes: kw_ref[r, lanes],
                                     lambda g=g: vw1_ref[g, :, r], lambda: wbias))

            def scores(prob):
                g, cols, _, k, _, bias_t = prob
                return (jnp.dot(k(), qt_ref[g, :, cols], preferred_element_type=F32)
                        + jnp.concatenate([bias_t()] * NSA_PAIR, axis=1))

            def probs(s):
                return jnp.exp((s - jnp.max(s, axis=0, keepdims=True)).astype(BF16))

            def values(prob, p):
                acc = jnp.dot(prob[4](), p, preferred_element_type=F32)
                return acc[:NSA_DH] / acc[NSA_DH:NSA_DH + 1]

            problems = problems[1:2] + problems[0:1] + problems[2:]
            outs = {}
            s_next = scores(problems[0])
            for idx, prob in enumerate(problems):
                s_cur = s_next
                if idx + 1 < len(problems):
                    s_next = scores(problems[idx + 1])
                outs[prob[0], prob[1].start, prob[2]] = values(prob, probs(s_cur))

            for g in range(NSA_GROUPS):
                for half in range(NSA_HG // NSA_PAIR):
                    c0 = half * NSA_PAIR * NSA_TQ
                    cols = slice(c0, c0 + NSA_PAIR * NSA_TQ)
                    o_t = (ocmp_ref[g, :, cols] + gsel_ref[g, :, cols] * outs[g, c0, 0]
                           + gwin_ref[g, :, cols] * outs[g, c0, 1])
                    for k in range(NSA_PAIR):
                        h = g * NSA_HG + half * NSA_PAIR + k
                        o = jnp.transpose(o_t[:, k * NSA_TQ:(k + 1) * NSA_TQ])
                        o_ref[:, h * NSA_DH:(h + 1) * NSA_DH] = o.astype(o_ref.dtype)


def _nsa_constants():
    c0 = np.arange(N_CMP)[:, None] * CMP_STRIDE
    s0 = np.arange(N_SLC)[None, :] * SLC_LEN
    overlap = np.clip(np.minimum(c0 + CMP_LEN, s0 + SLC_LEN) - np.maximum(c0, s0), 0, None) / CMP_STRIDE
    overlap[N_CMP - 1] = 0.0
    expand_t = (np.arange(SEQ)[:, None] // SLC_LEN == np.arange(N_SLC)[None, :]).astype(np.float32)
    return jnp.asarray(overlap.T, BF16), jnp.asarray(np.eye(LANE), BF16), jnp.asarray(expand_t, BF16)


def nsa_attention(p16, cmp_kv, p32, q_col, kv_col, small_col):
    nq = SEQ // NSA_TQ
    ovl_t, eye, expand = _nsa_constants()
    kv_spec = lambda n: pl.BlockSpec((SEQ, NSA_KV), lambda b, i: (b, kv_col // NSA_KV + n))
    return pl.pallas_call(
        _nsa_kernel,
        grid=(BATCH, nq),
        in_specs=[pl.BlockSpec((NSA_TQ, NSA_Q), lambda b, i: (b * nq + i, q_col // NSA_Q)),
                  kv_spec(0), kv_spec(1), kv_spec(2), kv_spec(3),
                  pl.BlockSpec((1, 2 * NSA_GROUPS, N_CMP, NSA_DH), lambda b, i: (b, 0, 0, 0)),
                  pl.BlockSpec((NSA_TQ, LANE), lambda b, i: (b * nq + i, small_col // LANE)),
                  _resident(ovl_t.shape), _resident(eye.shape), _resident(expand.shape)],
        out_specs=pl.BlockSpec((NSA_TQ, NSA_Q), lambda b, i: (b * nq + i, 0)),
        out_shape=jax.ShapeDtypeStruct((TOKENS, NSA_Q), BF16),
        scratch_shapes=[pltpu.VMEM((NSA_GROUPS, N_SLC, NSA_TQ), BF16),
                        pltpu.VMEM((NSA_GROUPS, NSA_DH, NSA_HG * NSA_TQ), BF16),
                        pltpu.VMEM((NSA_GROUPS, NSA_DH, NSA_HG * NSA_TQ), F32)]
        + [pltpu.VMEM((NSA_GROUPS, 1, NSA_HG * NSA_TQ), F32)] * 2
        + [pltpu.VMEM((NSA_GROUPS, V1_ROWS, SEQ), BF16)] * 2
        + [pltpu.VMEM((NSA_GROUPS, NSA_DH, N_CMP), BF16)],
        compiler_params=pltpu.CompilerParams(dimension_semantics=("arbitrary", "arbitrary"),
                                             vmem_limit_bytes=VMEM_LIMIT),
        name="nsa_attention",
    )(p16, p16, p16, p16, p16, cmp_kv, p32, ovl_t, eye, expand)


EVEN_QKV_COL = 0
EVEN_GATE_COL = 2 * GDN_QK + GDN_V
EVEN_CMP_COL = EVEN_GATE_COL + GDN_V
EVEN_SMALL_COL = EVEN_CMP_COL + 2 * NSA_KV
EVEN_F32_W = EVEN_SMALL_COL + LANE
EVEN_Q_COL = 0
EVEN_KV_COL = NSA_Q
EVEN_BF16_W = NSA_Q + 4 * NSA_KV
assert NSA_SMALL_GATE + 3 * NSA_HEADS <= LANE


def even_proj_weight(w_in):
    qkv_a, beta, dec, gate_a, q_b, kv_b, gate_b = split_last(w_in, EVEN_SPLITS)
    pad = jnp.zeros((w_in.shape[0], LANE - 2 * GDN_HEADS - 3 * NSA_HEADS), w_in.dtype)
    return jnp.concatenate([qkv_a, gate_a, kv_b[:, :2 * NSA_KV], beta, dec, gate_b, pad, q_b, kv_b[:, 2 * NSA_KV:]],
                           axis=1).astype(BF16)


def even_mixer_core(p32, p16, conv_w, A_log, dt_bias, o_norm, cmp_pe, cmp_w1, cmp_b1, cmp_w2):
    oa = gated_delta_net(p32, conv_w, A_log, dt_bias, o_norm, EVEN_QKV_COL, EVEN_GATE_COL, EVEN_SMALL_COL)
    cmp_kv = nsa_compress(p32, EVEN_CMP_COL, cmp_pe, cmp_w1, cmp_b1, cmp_w2)
    ob = nsa_attention(p16, cmp_kv, p32, EVEN_Q_COL, EVEN_KV_COL, EVEN_SMALL_COL)
    return oa, ob


GLA_TC = 256
ODD_Q_COL = 0
ODD_K_COL = GLA_QK
ODD_V_COL = 2 * GLA_QK
ODD_R_COL = ODD_V_COL + GLA_V
ODD_LR_COL = ODD_R_COL + GLA_V
ODD_F32_W = ODD_LR_COL + LANE


def _gla_kernel(q_ref, k_ref, v_ref, r_ref, lr_ref, aw_ref, ab_ref, onorm_ref, tri_ref, o_ref, state_ref):
    t = pl.program_id(1)
    C = GLA_CHUNK

    @pl.when(t == 0)
    def _():
        state_ref[...] = jnp.zeros(state_ref.shape, F32)

    z = jnp.dot(lr_ref[...].astype(BF16), aw_ref[...], preferred_element_type=F32) + ab_ref[...]
    log_a = (jnp.minimum(z, 0.0) - jnp.log(1.0 + jnp.exp(-jnp.abs(z)))) * (1.0 / GLA_TAU)
    tri = tri_ref[...]
    causal = lax.broadcasted_iota(jnp.int32, (C, C), 0) >= lax.broadcasted_iota(jnp.int32, (C, C), 1)

    n_chunks = GLA_TC // C
    b_alls = [_cumsum_rows(tri, log_a[c * C:(c + 1) * C]) for c in range(n_chunks)]
    work = {}
    for c in range(n_chunks):
        rows = slice(c * C, (c + 1) * C)
        for h in range(GLA_HEADS):
            kl = slice(h * GLA_DK, (h + 1) * GLA_DK)
            b = b_alls[c][:, kl]
            b_last = b[C - 1:C]
            k = k_ref[rows, kl]
            work[c, h] = dict(
                rows=rows, vl=slice(h * GLA_DV, (h + 1) * GLA_DV),
                v=v_ref[rows, h * GLA_DV:(h + 1) * GLA_DV].astype(BF16),
                qt=(q_ref[rows, kl] * GLA_DK ** -0.5 * jnp.exp(b)).astype(BF16),
                kt=(k * jnp.exp(-b)).astype(BF16),
                kh=(k * jnp.exp(b_last - b)).astype(BF16),
                keep=jnp.exp(b_last))
    wks = [work[key] for key in sorted(work)]
    intras = [jnp.where(causal, _dot_nt(wk["qt"], wk["kt"]), 0.0).astype(BF16) for wk in wks]
    o_intras = [jnp.dot(intra, wk["v"], preferred_element_type=F32) for wk, intra in zip(wks, intras)]
    updates = [_dot_tn(wk["v"], wk["kh"]) for wk in wks]

    states = [state_ref[h] for h in range(GLA_HEADS)]
    starts = []
    for (c, h), wk, upd in zip(sorted(work), wks, updates):
        starts.append(states[h].astype(BF16))
        states[h] = states[h] * wk["keep"] + upd
    for h in range(GLA_HEADS):
        state_ref[h] = states[h]

    o_inters = [_dot_nt(wk["qt"], s0) for wk, s0 in zip(wks, starts)]
    for wk, o_intra, o_inter in zip(wks, o_intras, o_inters):
        o = o_intra + o_inter
        r = r_ref[wk["rows"], wk["vl"]]
        on = o * lax.rsqrt(jnp.mean(o * o, axis=-1, keepdims=True) + EPS) * onorm_ref[...]
        o_ref[wk["rows"], wk["vl"]] = (on * (r * jax.nn.sigmoid(r))).astype(o_ref.dtype)


def gated_linear_attention(p32, a_w2, a_b, o_norm):
    nt = SEQ // GLA_TC
    tok = lambda blk: (lambda b, t: (b * nt + t, blk))
    aw = jnp.pad(a_w2, ((0, LANE - GLA_RANK), (0, 0))).astype(BF16)
    tri = jnp.asarray(np.tril(np.ones((GLA_CHUNK, GLA_CHUNK))), BF16)
    return pl.pallas_call(
        _gla_kernel,
        grid=(BATCH, nt),
        in_specs=[pl.BlockSpec((GLA_TC, GLA_QK), tok(ODD_Q_COL // GLA_QK)),
                  pl.BlockSpec((GLA_TC, GLA_QK), tok(ODD_K_COL // GLA_QK)),
                  pl.BlockSpec((GLA_TC, GLA_V), tok(ODD_V_COL // GLA_V)),
                  pl.BlockSpec((GLA_TC, GLA_V), tok(ODD_R_COL // GLA_V)),
                  pl.BlockSpec((GLA_TC, LANE), tok(ODD_LR_COL // LANE)),
                  _resident((LANE, GLA_QK)), _resident((1, GLA_QK)), _resident((1, GLA_DV)),
                  _resident((GLA_CHUNK, GLA_CHUNK))],
        out_specs=pl.BlockSpec((GLA_TC, GLA_V), tok(0)),
        out_shape=jax.ShapeDtypeStruct((TOKENS, GLA_V), BF16),
        scratch_shapes=[pltpu.VMEM((GLA_HEADS, GLA_DV, GLA_DK), F32)],
        compiler_params=pltpu.CompilerParams(dimension_semantics=("arbitrary", "arbitrary"),
                                             vmem_limit_bytes=VMEM_LIMIT),
        name="gated_linear_attention",
    )(p32, p32, p32, p32, p32, aw, a_b.reshape(1, GLA_QK), o_norm.reshape(1, GLA_DV), tri)


def _pad_cols(w, n):
    return jnp.pad(w, ((0, 0), (0, n - w.shape[1])))


def kernel(x, norm_gains, ffn_w_in, ffn_conv_w, ffn_conv_b, ffn_w_out, hyb_w_in, hyb_w_out, gdn_conv_w, gdn_A_log,
           gdn_dt_bias, gdn_o_norm, nsa_cmp_pe, nsa_cmp_w1, nsa_cmp_b1, nsa_cmp_w2, gla_w_in, gla_a_w2, gla_a_b,
           gla_o_norm, gla_w_out):
    x2 = x.reshape(TOKENS, D_MODEL)
    ffn_w_in_bf = ffn_w_in.astype(BF16)
    ffn_w_out_bf = ffn_w_out.astype(BF16)
    for layer in range(DEPTH):
        if layer % 2 == 0:
            e = layer // 2
            p32, p16 = norm_proj(x2, norm_gains[layer, 0], even_proj_weight(hyb_w_in[e]),
                                 ((EVEN_F32_W, F32), (EVEN_BF16_W, BF16)))
            oa, ob = even_mixer_core(p32, p16, gdn_conv_w[e], gdn_A_log[e], gdn_dt_bias[e], gdn_o_norm[e],
                                     nsa_cmp_pe[e], nsa_cmp_w1[e], nsa_cmp_b1[e], nsa_cmp_w2[e])
            halves = (oa, 0, ob, 0)
            w_mix = hyb_w_out[e]
        else:
            o_ = layer // 2
            w = _pad_cols(gla_w_in[o_], ODD_F32_W).astype(BF16)
            (p32,) = norm_proj(x2, norm_gains[layer, 0], w, ((ODD_F32_W, F32),))
            o = gated_linear_attention(p32, gla_a_w2[o_], gla_a_b[o_], gla_o_norm[o_])
            halves = (o, 0, o, 1)
            w_mix = gla_w_out[o_]
        x2 = mix_out_conv_ffn(*halves, x2, norm_gains[layer, 1], w_mix.astype(BF16), norm_gains[layer, 2],
                              layer, ffn_w_in_bf, ffn_conv_w, ffn_conv_b, ffn_w_out_bf, norm_gains[layer, 3])
    return x2.reshape(BATCH, SEQ, D_MODEL)
```

```python
import functools
import math

import jax
import jax.numpy as jnp
import numpy as np
from jax import lax
from jax.experimental import pallas as pl
from jax.experimental.pallas import tpu as pltpu

D_MODEL = 1024
BATCH = 8
SEQ = 2048
DEPTH = 4
TOKENS = BATCH * SEQ

EPS = 1e-6
NEG = -1e30
FORCE = 1e4

GDN_DK = 128
GDN_DV = 128
GDN_HEADS = 4
GDN_QK = GDN_HEADS * GDN_DK
GDN_V = GDN_HEADS * GDN_DV
GDN_CONV = 4
GDN_CHUNK = 64

NSA_DH = 64
NSA_HEADS = 8
NSA_GROUPS = 2
NSA_Q = NSA_HEADS * NSA_DH
NSA_KV = NSA_GROUPS * NSA_DH
CMP_LEN = 32
CMP_STRIDE = 16
CMP_HIDDEN = 2 * NSA_DH
SLC_LEN = 64
SLC_TOPK = 8
WINDOW = 512
Q_BLOCK = 128

GLA_HEADS = 4
GLA_DK = 128
GLA_DV = 256
GLA_QK = GLA_HEADS * GLA_DK
GLA_V = GLA_HEADS * GLA_DV
GLA_RANK = 16
GLA_TAU = 16.0
GLA_CHUNK = 64

D_FF = 2816
FFN_CONV = 3

EVEN_SPLITS = (2 * GDN_QK + GDN_V, GDN_HEADS, GDN_HEADS, GDN_V, NSA_Q, 6 * NSA_KV, 3 * NSA_HEADS)
ODD_SPLITS = (GLA_QK, GLA_QK, GLA_V, GLA_V, GLA_RANK)

LANE = 128
VMEM_LIMIT = 56 * 1024 * 1024

F32 = jnp.float32
BF16 = jnp.bfloat16


def _round_up(n, m):
    return (n + m - 1) // m * m


def _resident(shape):
    nd = len(shape)
    return pl.BlockSpec(shape, lambda *_: (0,) * nd, pipeline_mode=pl.Buffered(1))


def _resident_layer(shape, layer):
    nd = len(shape)
    return pl.BlockSpec((None,) + tuple(shape), lambda *_: (layer,) + (0,) * nd, pipeline_mode=pl.Buffered(1))


def _rms(x, g):
    ms = jnp.mean(x * x, axis=-1, keepdims=True)
    return x * lax.rsqrt(ms + EPS) * g


PROJ_TM = 512
PROJ_CH = 512


def _proj_kernel(x_ref, g_ref, w_ref, *o_refs):
    h = _rms(x_ref[...], g_ref[...]).astype(BF16)
    col = 0
    for o_ref in o_refs:
        n = o_ref.shape[1]
        for c0 in range(0, n, PROJ_CH):
            cw = min(PROJ_CH, n - c0)
            y = jnp.dot(h, w_ref[:, col + c0:col + c0 + cw], preferred_element_type=F32)
            o_ref[:, c0:c0 + cw] = y.astype(o_ref.dtype)
        col += n


def norm_proj(x2, gain, w, sections):
    T, D = x2.shape
    N = w.shape[1]
    assert N == sum(n for n, _ in sections) and all(n % LANE == 0 for n, _ in sections)
    return pl.pallas_call(
        _proj_kernel,
        grid=(T // PROJ_TM,),
        in_specs=[pl.BlockSpec((PROJ_TM, D), lambda i: (i, 0)), _resident((1, D)), _resident((D, N))],
        out_specs=[pl.BlockSpec((PROJ_TM, n), lambda i: (i, 0)) for n, _ in sections],
        out_shape=[jax.ShapeDtypeStruct((T, n), dt) for n, dt in sections],
        compiler_params=pltpu.CompilerParams(dimension_semantics=("arbitrary",), vmem_limit_bytes=VMEM_LIMIT),
        name="norm_proj",
    )(x2, gain.reshape(1, D), w)


FFN_TM = 512
FFN_HALO = 16
FFN_CH = 256
MIX_HALF = D_MODEL // 2
assert D_FF % FFN_CH == 0


def _ffn_kernel(oa_ref, ob_ref, x_ref, g_mix_ref, w_mix_ref, g_in_ref, w_in_ref, cw_ref, cb_ref, w_out_ref, g_out_ref,
                y_ref, hext_ref, act_ref):
    t = pl.program_id(1)

    @pl.when(t == 0)
    def _():
        hext_ref[0:FFN_HALO, :] = jnp.zeros((FFN_HALO, D_MODEL), BF16)

    mix = (jnp.dot(oa_ref[...], w_mix_ref[0:MIX_HALF, :], preferred_element_type=F32)
           + jnp.dot(ob_ref[...], w_mix_ref[MIX_HALF:, :], preferred_element_type=F32))
    x = x_ref[...] + _rms(mix, g_mix_ref[...])
    hext_ref[FFN_HALO:, :] = _rms(x, g_in_ref[...]).astype(BF16)
    hext = hext_ref[...]

    def conv_half(c0):
        u = jnp.dot(hext, w_in_ref[:, c0:c0 + FFN_CH], preferred_element_type=F32)
        u1 = pltpu.roll(u, 1, 0)
        u2 = pltpu.roll(u, 2, 0)
        w = cw_ref[:, c0:c0 + FFN_CH]
        y = u2 * w[0:1] + u1 * w[1:2] + u * w[2:3] + cb_ref[:, c0:c0 + FFN_CH]
        return y[FFN_HALO:]

    for c0 in range(0, D_FF, FFN_CH):
        half_gate = 0.5 * conv_half(c0)
        val = conv_half(D_FF + c0)
        silu = half_gate + half_gate * jnp.tanh(half_gate)
        act_ref[:, c0:c0 + FFN_CH] = (silu * val).astype(BF16)
    down = jnp.dot(act_ref[...], w_out_ref[...], preferred_element_type=F32)
    y_ref[...] = x + _rms(down, g_out_ref[...])
    hext_ref[0:FFN_HALO, :] = hext_ref[FFN_TM:FFN_TM + FFN_HALO, :]


def mix_out_conv_ffn(oa, a_blk, ob, b_blk, x2, g_mix, w_mix, g_in, layer, w_in, conv_w, conv_b, w_out, g_out):
    T, D = x2.shape
    nt = SEQ // FFN_TM
    tok = lambda blk: (lambda b, t: (b * nt + t, blk))
    return pl.pallas_call(
        _ffn_kernel,
        grid=(BATCH, nt),
        in_specs=[pl.BlockSpec((FFN_TM, MIX_HALF), tok(a_blk)), pl.BlockSpec((FFN_TM, MIX_HALF), tok(b_blk)),
                  pl.BlockSpec((FFN_TM, D), tok(0)), _resident((1, D)), _resident((D, D)),
                  _resident((1, D)), _resident_layer((D, 2 * D_FF), layer),
                  _resident_layer((FFN_CONV, 2 * D_FF), layer), _resident_layer((1, 2 * D_FF), layer),
                  _resident_layer((D_FF, D), layer), _resident((1, D))],
        out_specs=pl.BlockSpec((FFN_TM, D), tok(0)),
        out_shape=jax.ShapeDtypeStruct((T, D), F32),
        scratch_shapes=[pltpu.VMEM((FFN_HALO + FFN_TM, D), BF16), pltpu.VMEM((FFN_TM, D_FF), BF16)],
        compiler_params=pltpu.CompilerParams(dimension_semantics=("arbitrary", "arbitrary"),
                                             vmem_limit_bytes=VMEM_LIMIT),
        name="mix_out_conv_ffn",
    )(oa, ob, x2, g_mix.reshape(1, D), w_mix, g_in.reshape(1, D), w_in, conv_w, conv_b.reshape(DEPTH, 1, -1), w_out,
      g_out.reshape(1, D))


def split_last(x, sizes):
    idx = np.cumsum(sizes)[:-1].tolist()
    return jnp.split(x, idx, axis=-1)


def _dot_nt(a, b):
    return lax.dot_general(a, b, (((1,), (1,)), ((), ())), preferred_element_type=F32)


def _dot_tn(a, b):
    return lax.dot_general(a, b, (((0,), (0,)), ((), ())), preferred_element_type=F32)


def _split3(a):
    hi = a.astype(BF16)
    r = a - hi.astype(F32)
    mid = r.astype(BF16)
    return hi, mid, (r - mid.astype(F32)).astype(BF16)


def _dot_x3(a, b):
    a_hi, a_lo, _ = _split3(a)
    b_hi, b_lo, _ = _split3(b)
    d = lambda p, q: jnp.dot(p, q, preferred_element_type=F32)
    return d(a_hi, b_hi) + (d(a_lo, b_hi) + d(a_hi, b_lo))


def _cumsum_rows(tri, g):
    return sum(jnp.dot(tri, t, preferred_element_type=F32) for t in reversed(_split3(g)))


def _unit_lower_inverses(mats):
    c = mats[0].shape[0]
    eye = jnp.where(lax.broadcasted_iota(jnp.int32, (c, c), 0) == lax.broadcasted_iota(jnp.int32, (c, c), 1), 1.0, 0.0)
    dot = lambda a, b: jnp.dot(a.astype(BF16), b.astype(BF16), preferred_element_type=F32)
    ps = [-a for a in mats]
    ts = [eye + p for p in ps]
    k = 2
    while k < c:
        ps = [dot(p, p) for p in ps]
        ts = [t + dot(t, p) for t, p in zip(ts, ps)]
        k *= 2
    res = [(eye - t) - _dot_x3(a, t) for a, t in zip(mats, ts)]
    return [t + dot(t, r) for t, r in zip(ts, res)]


GDN_TC = 256
GDN_HALO = 8
GDN_SMALL_BETA = 0
GDN_SMALL_DECAY = GDN_HEADS
NSA_SMALL_GATE = 2 * GDN_HEADS


def _gdn_kernel(x_ref, gate_ref, small_ref, cw_ref, alog_ref, dtb_ref, onorm_ref, tri_ref, o_ref, xext_ref, state_ref):
    t = pl.program_id(1)
    C = GDN_CHUNK

    @pl.when(t == 0)
    def _():
        xext_ref[0:GDN_HALO, :] = jnp.zeros((GDN_HALO, xext_ref.shape[1]), F32)
        state_ref[...] = jnp.zeros(state_ref.shape, F32)

    xext_ref[GDN_HALO:, :] = x_ref[...]
    small = small_ref[...]
    beta_all = jax.nn.sigmoid(small)
    z = small + dtb_ref[...]
    softplus = jnp.maximum(z, 0.0) + jnp.log(1.0 + jnp.exp(-jnp.abs(z)))
    g_all = -jnp.exp(alog_ref[...]) * softplus
    tri = tri_ref[...]
    row = lax.broadcasted_iota(jnp.int32, (C, C), 0)
    col = lax.broadcasted_iota(jnp.int32, (C, C), 1)

    def conv_silu(c0, width):
        xe = xext_ref[:, c0:c0 + width]
        w = cw_ref[:, c0:c0 + width]
        y = xe * w[GDN_CONV - 1:GDN_CONV]
        for s in range(1, GDN_CONV):
            y = y + pltpu.roll(xe, s, 0) * w[GDN_CONV - 1 - s:GDN_CONV - s]
        y = y[GDN_HALO:]
        return y * jax.nn.sigmoid(y)

    n_chunks = GDN_TC // C
    gsums = [_cumsum_rows(tri, g_all[c * C:(c + 1) * C]) for c in range(n_chunks)]
    gsum_ts = [jnp.transpose(gs) for gs in gsums]

    work = {}
    for h in range(GDN_HEADS):
        q_all = conv_silu(h * GDN_DK, GDN_DK)
        k_all = conv_silu(GDN_QK + h * GDN_DK, GDN_DK)
        v_all = conv_silu(2 * GDN_QK + h * GDN_DV, GDN_DV)
        q_all = q_all * (lax.rsqrt(jnp.sum(q_all * q_all, axis=-1, keepdims=True) + EPS) * GDN_DK ** -0.5)
        k_all = k_all * lax.rsqrt(jnp.sum(k_all * k_all, axis=-1, keepdims=True) + EPS)
        for c in range(n_chunks):
            rows = slice(c * C, (c + 1) * C)
            q, k, v = q_all[rows], k_all[rows], v_all[rows]
            beta = beta_all[rows, GDN_SMALL_BETA + h:GDN_SMALL_BETA + h + 1]
            gcol = gsums[c][:, GDN_SMALL_DECAY + h:GDN_SMALL_DECAY + h + 1]
            grow = gsum_ts[c][GDN_SMALL_DECAY + h:GDN_SMALL_DECAY + h + 1, :]
            glast = gcol[C - 1:C]
            decay = jnp.exp(jnp.where(row >= col, gcol - grow, -jnp.inf))
            kb = k * beta
            kbf = k.astype(BF16)
            eg = jnp.exp(gcol)
            work[h, c] = dict(
                a_kk=jnp.where(row > col, _dot_nt(kb.astype(BF16), kbf) * decay, 0.0),
                a_qk=(_dot_nt(q.astype(BF16), kbf) * decay).astype(BF16),
                rhs=jnp.concatenate([v * beta, kb * eg], axis=-1),
                qg=q * eg,
                kd_t=jnp.transpose(k * jnp.exp(glast - gcol)).astype(BF16),
                keep=jnp.exp(glast))
    keys = sorted(work)
    inverses = _unit_lower_inverses([work[key]["a_kk"] for key in keys])
    wks = [work[key] for key in keys]
    sols = [_dot_x3(inv, wk["rhs"]) for wk, inv in zip(wks, inverses)]
    us = [sol[:, :GDN_DV].astype(BF16) for sol in sols]
    ws = [sol[:, GDN_DV:].astype(BF16) for sol in sols]
    dot = lambda a, b: jnp.dot(a, b, preferred_element_type=F32)
    for wk, s_in in zip(wks, [dot(wk["kd_t"], u) for wk, u in zip(wks, us)]):
        wk["s_in"] = s_in
    for wk, s_mix in zip(wks, [dot(wk["kd_t"], w) for wk, w in zip(wks, ws)]):
        wk["s_mix"] = s_mix.astype(BF16)
    for wk, o_in in zip(wks, [dot(wk["a_qk"], u) for wk, u in zip(wks, us)]):
        wk["o_in"] = o_in
    for wk, aw in zip(wks, [dot(wk["a_qk"], w) for wk, w in zip(wks, ws)]):
        wk["o_mix"] = (wk["qg"] - aw).astype(BF16)

    states = [state_ref[h] for h in range(GDN_HEADS)]
    for c in range(n_chunks):
        rows = slice(c * C, (c + 1) * C)
        for h in range(GDN_HEADS):
            wk = work[h, c]
            sb = states[h].astype(BF16)
            o = wk["o_in"] + jnp.dot(wk["o_mix"], sb, preferred_element_type=F32)
            states[h] = (states[h] * wk["keep"] + wk["s_in"]) - jnp.dot(wk["s_mix"], sb, preferred_element_type=F32)
            gate = gate_ref[rows, h * GDN_DV:(h + 1) * GDN_DV]
            on = o * lax.rsqrt(jnp.mean(o * o, axis=-1, keepdims=True) + EPS) * onorm_ref[...]
            o_ref[rows, h * GDN_DV:(h + 1) * GDN_DV] = (on * (gate * jax.nn.sigmoid(gate))).astype(o_ref.dtype)
    for h in range(GDN_HEADS):
        state_ref[h] = states[h]
    xext_ref[0:GDN_HALO, :] = xext_ref[GDN_TC:GDN_TC + GDN_HALO, :]


def gated_delta_net(p32, conv_w, a_log, dt_bias, o_norm, qkv_col, gate_col, small_col):
    nt = SEQ // GDN_TC
    width = 2 * GDN_QK + GDN_V
    lane_vec = lambda a, off: jnp.zeros((1, LANE), F32).at[0, off:off + GDN_HEADS].set(a.astype(F32))
    tri = jnp.asarray(np.tril(np.ones((GDN_CHUNK, GDN_CHUNK))), BF16)
    tok = lambda blk: (lambda b, t: (b * nt + t, blk))
    return pl.pallas_call(
        _gdn_kernel,
        grid=(BATCH, nt),
        in_specs=[pl.BlockSpec((GDN_TC, width), tok(qkv_col // width)),
                  pl.BlockSpec((GDN_TC, GDN_V), tok(gate_col // GDN_V)),
                  pl.BlockSpec((GDN_TC, LANE), tok(small_col // LANE)),
                  _resident((GDN_CONV, width)), _resident((1, LANE)), _resident((1, LANE)), _resident((1, GDN_DV)),
                  _resident((GDN_CHUNK, GDN_CHUNK))],
        out_specs=pl.BlockSpec((GDN_TC, GDN_V), tok(0)),
        out_shape=jax.ShapeDtypeStruct((TOKENS, GDN_V), BF16),
        scratch_shapes=[pltpu.VMEM((GDN_HALO + GDN_TC, width), F32), pltpu.VMEM((GDN_HEADS, GDN_DK, GDN_DV), F32)],
        compiler_params=pltpu.CompilerParams(dimension_semantics=("arbitrary", "arbitrary"),
                                             vmem_limit_bytes=VMEM_LIMIT),
        name="gated_delta_net",
    )(p32, p32, p32, conv_w, lane_vec(a_log, GDN_SMALL_DECAY), lane_vec(dt_bias, GDN_SMALL_DECAY),
      o_norm.reshape(1, GDN_DV), tri)


N_SLC = SEQ // SLC_LEN
N_CMP = SEQ // CMP_STRIDE
NSA_HG = NSA_HEADS // NSA_GROUPS
NSA_TQ = 128
SEL_STEP = 512
WIN_W = WINDOW + NSA_TQ
NSA_SCALE = NSA_DH ** -0.5


def _cmp_kernel(kc_ref, vc_ref, pe_ref, w1_ref, b1_ref, w2_ref, o_ref, strip_ref):
    half = CMP_STRIDE * NSA_DH
    for kv, x_ref in enumerate((kc_ref, vc_ref)):
        for p in range(CMP_STRIDE):
            xp = x_ref[pl.ds(p, N_CMP, stride=CMP_STRIDE), :]
            for g in range(NSA_GROUPS):
                strip_ref[kv * NSA_GROUPS + g, :, p * NSA_DH:(p + 1) * NSA_DH] = xp[:, g * NSA_DH:(g + 1) * NSA_DH]
    for kv in range(2):
        for g in range(NSA_GROUPS):
            rows = strip_ref[kv * NSA_GROUPS + g]
            lo = (rows + pe_ref[kv, :, :half]).astype(BF16)
            hi = (rows + pe_ref[kv, :, half:]).astype(BF16)
            a = jnp.dot(lo, w1_ref[kv, :half, :], preferred_element_type=F32)
            b = jnp.dot(hi, w1_ref[kv, half:, :], preferred_element_type=F32)
            hid = a + pltpu.roll(b, N_CMP - 1, 0) + b1_ref[kv]
            act = (hid * jax.nn.sigmoid(hid)).astype(BF16)
            o_ref[0, kv * NSA_GROUPS + g] = jnp.dot(act, w2_ref[kv], preferred_element_type=F32).astype(BF16)


def nsa_compress(p32, cmp_col, pe, w1, b1, w2):
    n = 2 * NSA_GROUPS
    return pl.pallas_call(
        _cmp_kernel,
        grid=(BATCH,),
        in_specs=[pl.BlockSpec((SEQ, NSA_KV), lambda b: (b, cmp_col // NSA_KV)),
                  pl.BlockSpec((SEQ, NSA_KV), lambda b: (b, cmp_col // NSA_KV + 1)),
                  _resident((2, 1, CMP_LEN * NSA_DH)), _resident((2, CMP_LEN * NSA_DH, CMP_HIDDEN)),
                  _resident((2, 1, CMP_HIDDEN)), _resident((2, CMP_HIDDEN, NSA_DH))],
        out_specs=pl.BlockSpec((1, n, N_CMP, NSA_DH), lambda b: (b, 0, 0, 0)),
        out_shape=jax.ShapeDtypeStruct((BATCH, n, N_CMP, NSA_DH), BF16),
        scratch_shapes=[pltpu.VMEM((n, N_CMP, CMP_STRIDE * NSA_DH), F32)],
        compiler_params=pltpu.CompilerParams(dimension_semantics=("arbitrary",), vmem_limit_bytes=VMEM_LIMIT),
        name="nsa_compress",
    )(p32, p32, pe.reshape(2, 1, -1), w1.astype(BF16), b1.reshape(2, 1, -1), w2.astype(BF16))


V1_ROWS = NSA_DH + 16
NSA_PAIR = 2
MASK_BIG = 2.0 ** 100


def _masked_attention_t(k, q_t, v1_t, bias_t):
    s = jnp.dot(k, q_t, preferred_element_type=F32) + jnp.concatenate([bias_t] * NSA_HG, axis=1)
    p = jnp.exp((s - jnp.max(s, axis=0, keepdims=True)).astype(BF16))
    acc = jnp.dot(v1_t, p, preferred_element_type=F32)
    return acc[:NSA_DH] / acc[NSA_DH:NSA_DH + 1]


def _transposed_with_ones_row(eye, v):
    v_t = _dot_nt(eye, v).astype(v.dtype)
    row = lax.broadcasted_iota(jnp.int32, (V1_ROWS - NSA_DH, v.shape[0]), 0)
    return jnp.concatenate([v_t, jnp.where(row == 0, 1.0, 0.0).astype(v.dtype)], axis=0)


def _nsa_kernel(q_ref, ks_ref, vs_ref, kw_ref, vw_ref, cmp_ref, gate_ref, ovl_ref, eye_ref, expand_ref,
                o_ref, ksx_ref, qt_ref, ocmp_ref, gsel_ref, gwin_ref, vs1_ref, vw1_ref, vct_ref):
    i = pl.program_id(1)
    q0 = i * NSA_TQ
    rows = NSA_HG * NSA_TQ
    eye = eye_ref[...]
    eye_dh = eye[0:NSA_DH, 0:NSA_DH]

    @pl.when(i == 0)
    def _():
        for g in range(NSA_GROUPS):
            lanes = slice(g * NSA_DH, (g + 1) * NSA_DH)
            vs1_ref[g] = _transposed_with_ones_row(eye_dh, vs_ref[:, lanes])
            vw1_ref[g] = _transposed_with_ones_row(eye_dh, vw_ref[:, lanes])
            vct_ref[g] = _dot_nt(eye_dh, cmp_ref[0, NSA_GROUPS + g]).astype(BF16)
            ksx_ref[g] = jnp.concatenate(
                [ks_ref[:, lanes], expand_ref[...], jnp.zeros((SEQ, LANE - NSA_DH - N_SLC), BF16)], axis=1)
            qt_ref[g, NSA_DH + N_SLC:, :] = jnp.zeros((LANE - NSA_DH - N_SLC, NSA_HG * NSA_TQ), BF16)

    gates = jax.nn.sigmoid(gate_ref[...])
    gates_t = jnp.transpose(gates)
    tok = q0 + lax.broadcasted_iota(jnp.int32, (NSA_TQ, 1), 0)
    tok_l = q0 + lax.broadcasted_iota(jnp.int32, (1, NSA_TQ), 1)

    groups = range(NSA_GROUPS)
    scale = jnp.asarray(NSA_SCALE, BF16)
    for pair in range(NSA_HEADS // 2):
        pair_t = _dot_nt(eye, q_ref[:, pair * LANE:(pair + 1) * LANE] * scale).astype(BF16)
        for k in range(2):
            g, hk = divmod(2 * pair + k, NSA_HG)
            qt_ref[g, 0:NSA_DH, hk * NSA_TQ:(hk + 1) * NSA_TQ] = pair_t[k * NSA_DH:(k + 1) * NSA_DH]

    def gate_row(g, branch):
        lane = NSA_SMALL_GATE + branch * NSA_HEADS + g * NSA_HG
        return jnp.concatenate([gates_t[lane + hk:lane + hk + 1, :] for hk in range(NSA_HG)], axis=1)

    cmp_end = lax.broadcasted_iota(jnp.int32, (N_CMP, 1), 0) * CMP_STRIDE + (CMP_LEN - 1)
    cvalid = cmp_end <= jnp.concatenate([tok_l] * NSA_HG, axis=1)
    ss = [jnp.dot(cmp_ref[0, g], qt_ref[g, 0:NSA_DH, :], preferred_element_type=F32) for g in groups]
    sms = [jnp.where(cvalid, s, NEG) for s in ss]
    es = [jnp.exp(sm - jnp.max(sm, axis=0, keepdims=True)) for sm in sms]
    ps = [jnp.where(cvalid, e / jnp.sum(e, axis=0, keepdims=True), 0.0) for e in es]
    for g in groups:
        ocmp_ref[g] = gate_row(g, 0) * jnp.dot(vct_ref[g], ps[g].astype(BF16), preferred_element_type=F32)
        gsel_ref[g] = gate_row(g, 1)
        gwin_ref[g] = gate_row(g, 2)

    psums = [sum(p[:, hk * NSA_TQ:(hk + 1) * NSA_TQ] for hk in range(NSA_HG)) for p in ps]
    imps = [jnp.dot(ovl_ref[...], psum.astype(BF16), preferred_element_type=F32) for psum in psums]
    blk = lax.broadcasted_iota(jnp.int32, (N_SLC, NSA_TQ), 0)
    cur = tok_l // SLC_LEN
    bvalid = blk <= cur
    forced = (blk == 0) | (blk == cur) | (blk == cur - 1)
    scores_ = [jnp.where(bvalid, imp + jnp.where(forced, FORCE, 0.0), NEG) for imp in imps]
    ranks = [jnp.zeros((N_SLC, NSA_TQ), F32) for _ in groups]
    for j in range(N_SLC):
        for g in groups:
            sj = scores_[g][j:j + 1, :]
            tie = jnp.where((sj == scores_[g]) & (blk > j), 1.0, 0.0)
            ranks[g] = ranks[g] + jnp.where(sj > scores_[g], 1.0, tie)
    for g in groups:
        off = jnp.where((ranks[g] < SLC_TOPK) & bvalid, 0.0, -MASK_BIG).astype(BF16)
        qt_ref[g, NSA_DH:NSA_DH + N_SLC, :] = jnp.concatenate([off] * NSA_HG, axis=1)

    for n in range(SEQ // SEL_STEP):
        @pl.when(i // (SEL_STEP // NSA_TQ) == n)
        def _():
            w = (n + 1) * SEL_STEP
            w0 = pl.multiple_of(jnp.maximum(q0 + NSA_TQ - WIN_W, 0), NSA_TQ)
            r = pl.ds(w0, WIN_W)

            def bias(pos, lo_excl):
                ok = (pos <= tok_l) if lo_excl is None else ((pos <= tok_l) & (pos > lo_excl))
                return jnp.concatenate([jnp.where(ok, 0.0, NEG)] * NSA_PAIR, axis=1)

            sel_causal = bias((w - SEL_STEP) + lax.broadcasted_iota(jnp.int32, (SEL_STEP, 1), 0), None)
            wpos = w0 + lax.broadcasted_iota(jnp.int32, (WIN_W, 1), 0)
            if n == 0:
                win_edges = [(0, WIN_W, bias(wpos, tok_l - WINDOW))]
            else:
                win_edges = [(0, NSA_TQ, bias(wpos[:NSA_TQ], tok_l - WINDOW)),
                             (WIN_W - NSA_TQ, WIN_W, bias(wpos[WIN_W - NSA_TQ:], None))]

            def add_bias(s, edges):
                pieces, at = [], 0
                for lo, hi, b in edges:
                    pieces += [s[at:lo], s[lo:hi] + b]
                    at = hi
                return jnp.concatenate([p for p in pieces + [s[at:]] if p.shape[0]], axis=0)

            problems = []
            for g in range(NSA_GROUPS):
                lanes = slice(g * NSA_DH, (g + 1) * NSA_DH)
                for half in range(NSA_HG // NSA_PAIR):
                    cols = slice(half * NSA_PAIR * NSA_TQ, (half + 1) * NSA_PAIR * NSA_TQ)
                    problems.append((g, cols, 0, lambda g=g: ksx_ref[g, 0:w, :], lambda g=g: vs1_ref[g, :, 0:w],
                                     lambda g=g, cols=cols: qt_ref[g, :, cols], [(w - SEL_STEP, w, sel_causal)]))
                    problems.append((g, cols, 1, lambda lanes=lanes: kw_ref[r, lanes], lambda g=g: vw1_ref[g, :, r],
                                     lambda g=g, cols=cols: qt_ref[g, 0:NSA_DH, cols], win_edges))

            def scores(prob):
                _, _, _, k, _, q_t, edges = prob
                return add_bias(jnp.dot(k(), q_t(), preferred_element_type=F32), edges)

            def probs(s):
                return jnp.exp((s - jnp.max(s, axis=0, keepdims=True)).astype(BF16))

            def values(prob, p):
                acc = jnp.dot(prob[4](), p, preferred_element_type=F32)
                return acc[:NSA_DH] / acc[NSA_DH:NSA_DH + 1]

            problems = problems[1:2] + problems[0:1] + problems[2:]
            outs = {}
            s_next = scores(problems[0])
            for idx, prob in enumerate(problems):
                s_cur = s_next
                if idx + 1 < len(problems):
                    s_next = scores(problems[idx + 1])
                outs[prob[0], prob[1].start, prob[2]] = values(prob, probs(s_cur))

            for g in range(NSA_GROUPS):
                for half in range(NSA_HG // NSA_PAIR):
                    c0 = half * NSA_PAIR * NSA_TQ
                    cols = slice(c0, c0 + NSA_PAIR * NSA_TQ)
                    o_t = (ocmp_ref[g, :, cols] + gsel_ref[g, :, cols] * outs[g, c0, 0]
                           + gwin_ref[g, :, cols] * outs[g, c0, 1])
                    for k in range(NSA_PAIR):
                        h = g * NSA_HG + half * NSA_PAIR + k
                        o = jnp.transpose(o_t[:, k * NSA_TQ:(k + 1) * NSA_TQ])
                        o_ref[:, h * NSA_DH:(h + 1) * NSA_DH] = o.astype(o_ref.dtype)


def _nsa_constants():
    c0 = np.arange(N_CMP)[:, None] * CMP_STRIDE
    s0 = np.arange(N_SLC)[None, :] * SLC_LEN
    overlap = np.clip(np.minimum(c0 + CMP_LEN, s0 + SLC_LEN) - np.maximum(c0, s0), 0, None) / CMP_STRIDE
    overlap[N_CMP - 1] = 0.0
    expand_t = (np.arange(SEQ)[:, None] // SLC_LEN == np.arange(N_SLC)[None, :]).astype(np.float32)
    return jnp.asarray(overlap.T, BF16), jnp.asarray(np.eye(LANE), BF16), jnp.asarray(expand_t, BF16)


def nsa_attention(p16, cmp_kv, p32, q_col, kv_col, small_col):
    nq = SEQ // NSA_TQ
    ovl_t, eye, expand = _nsa_constants()
    kv_spec = lambda n: pl.BlockSpec((SEQ, NSA_KV), lambda b, i: (b, kv_col // NSA_KV + n))
    return pl.pallas_call(
        _nsa_kernel,
        grid=(BATCH, nq),
        in_specs=[pl.BlockSpec((NSA_TQ, NSA_Q), lambda b, i: (b * nq + i, q_col // NSA_Q)),
                  kv_spec(0), kv_spec(1), kv_spec(2), kv_spec(3),
                  pl.BlockSpec((1, 2 * NSA_GROUPS, N_CMP, NSA_DH), lambda b, i: (b, 0, 0, 0)),
                  pl.BlockSpec((NSA_TQ, LANE), lambda b, i: (b * nq + i, small_col // LANE)),
                  _resident(ovl_t.shape), _resident(eye.shape), _resident(expand.shape)],
        out_specs=pl.BlockSpec((NSA_TQ, NSA_Q), lambda b, i: (b * nq + i, 0)),
        out_shape=jax.ShapeDtypeStruct((TOKENS, NSA_Q), BF16),
        scratch_shapes=[pltpu.VMEM((NSA_GROUPS, SEQ, LANE), BF16),
                        pltpu.VMEM((NSA_GROUPS, LANE, NSA_HG * NSA_TQ), BF16),
                        pltpu.VMEM((NSA_GROUPS, NSA_DH, NSA_HG * NSA_TQ), F32)]
        + [pltpu.VMEM((NSA_GROUPS, 1, NSA_HG * NSA_TQ), F32)] * 2
        + [pltpu.VMEM((NSA_GROUPS, V1_ROWS, SEQ), BF16)] * 2
        + [pltpu.VMEM((NSA_GROUPS, NSA_DH, N_CMP), BF16)],
        compiler_params=pltpu.CompilerParams(dimension_semantics=("arbitrary", "arbitrary"),
                                             vmem_limit_bytes=VMEM_LIMIT),
        name="nsa_attention",
    )(p16, p16, p16, p16, p16, cmp_kv, p32, ovl_t, eye, expand)


EVEN_QKV_COL = 0
EVEN_GATE_COL = 2 * GDN_QK + GDN_V
EVEN_CMP_COL = EVEN_GATE_COL + GDN_V
EVEN_SMALL_COL = EVEN_CMP_COL + 2 * NSA_KV
EVEN_F32_W = EVEN_SMALL_COL + LANE
EVEN_Q_COL = 0
EVEN_KV_COL = NSA_Q
EVEN_BF16_W = NSA_Q + 4 * NSA_KV
assert NSA_SMALL_GATE + 3 * NSA_HEADS <= LANE


def even_proj_weight(w_in):
    qkv_a, beta, dec, gate_a, q_b, kv_b, gate_b = split_last(w_in, EVEN_SPLITS)
    pad = jnp.zeros((w_in.shape[0], LANE - 2 * GDN_HEADS - 3 * NSA_HEADS), w_in.dtype)
    return jnp.concatenate([qkv_a, gate_a, kv_b[:, :2 * NSA_KV], beta, dec, gate_b, pad, q_b, kv_b[:, 2 * NSA_KV:]],
                           axis=1).astype(BF16)


def even_mixer_core(p32, p16, conv_w, A_log, dt_bias, o_norm, cmp_pe, cmp_w1, cmp_b1, cmp_w2):
    oa = gated_delta_net(p32, conv_w, A_log, dt_bias, o_norm, EVEN_QKV_COL, EVEN_GATE_COL, EVEN_SMALL_COL)
    cmp_kv = nsa_compress(p32, EVEN_CMP_COL, cmp_pe, cmp_w1, cmp_b1, cmp_w2)
    ob = nsa_attention(p16, cmp_kv, p32, EVEN_Q_COL, EVEN_KV_COL, EVEN_SMALL_COL)
    return oa, ob


GLA_TC = 256
ODD_Q_COL = 0
ODD_K_COL = GLA_QK
ODD_V_COL = 2 * GLA_QK
ODD_R_COL = ODD_V_COL + GLA_V
ODD_LR_COL = ODD_R_COL + GLA_V
ODD_F32_W = ODD_LR_COL + LANE


def _gla_kernel(q_ref, k_ref, v_ref, r_ref, lr_ref, aw_ref, ab_ref, onorm_ref, tri_ref, o_ref, state_ref):
    t = pl.program_id(1)
    C = GLA_CHUNK

    @pl.when(t == 0)
    def _():
        state_ref[...] = jnp.zeros(state_ref.shape, F32)

    z = jnp.dot(lr_ref[...].astype(BF16), aw_ref[...], preferred_element_type=F32) + ab_ref[...]
    log_a = (jnp.minimum(z, 0.0) - jnp.log(1.0 + jnp.exp(-jnp.abs(z)))) * (1.0 / GLA_TAU)
    tri = tri_ref[...]
    causal = lax.broadcasted_iota(jnp.int32, (C, C), 0) >= lax.broadcasted_iota(jnp.int32, (C, C), 1)

    n_chunks = GLA_TC // C
    b_alls = [_cumsum_rows(tri, log_a[c * C:(c + 1) * C]) for c in range(n_chunks)]
    work = {}
    for c in range(n_chunks):
        rows = slice(c * C, (c + 1) * C)
        for h in range(GLA_HEADS):
            kl = slice(h * GLA_DK, (h + 1) * GLA_DK)
            b = b_alls[c][:, kl]
            b_last = b[C - 1:C]
            k = k_ref[rows, kl]
            work[c, h] = dict(
                rows=rows, vl=slice(h * GLA_DV, (h + 1) * GLA_DV),
                v=v_ref[rows, h * GLA_DV:(h + 1) * GLA_DV].astype(BF16),
                qt=(q_ref[rows, kl] * GLA_DK ** -0.5 * jnp.exp(b)).astype(BF16),
                kt=(k * jnp.exp(-b)).astype(BF16),
                kh=(k * jnp.exp(b_last - b)).astype(BF16),
                keep=jnp.exp(b_last))
    wks = [work[key] for key in sorted(work)]
    intras = [jnp.where(causal, _dot_nt(wk["qt"], wk["kt"]), 0.0).astype(BF16) for wk in wks]
    o_intras = [jnp.dot(intra, wk["v"], preferred_element_type=F32) for wk, intra in zip(wks, intras)]
    updates = [_dot_tn(wk["v"], wk["kh"]) for wk in wks]

    states = [state_ref[h] for h in range(GLA_HEADS)]
    starts = []
    for (c, h), wk, upd in zip(sorted(work), wks, updates):
        starts.append(states[h].astype(BF16))
        states[h] = states[h] * wk["keep"] + upd
    for h in range(GLA_HEADS):
        state_ref[h] = states[h]

    o_inters = [_dot_nt(wk["qt"], s0) for wk, s0 in zip(wks, starts)]
    for wk, o_intra, o_inter in zip(wks, o_intras, o_inters):
        o = o_intra + o_inter
        r = r_ref[wk["rows"], wk["vl"]]
        on = o * lax.rsqrt(jnp.mean(o * o, axis=-1, keepdims=True) + EPS) * onorm_ref[...]
        o_ref[wk["rows"], wk["vl"]] = (on * (r * jax.nn.sigmoid(r))).astype(o_ref.dtype)


def gated_linear_attention(p32, a_w2, a_b, o_norm):
    nt = SEQ // GLA_TC
    tok = lambda blk: (lambda b, t: (b * nt + t, blk))
    aw = jnp.pad(a_w2, ((0, LANE - GLA_RANK), (0, 0))).astype(BF16)
    tri = jnp.asarray(np.tril(np.ones((GLA_CHUNK, GLA_CHUNK))), BF16)
    return pl.pallas_call(
        _gla_kernel,
        grid=(BATCH, nt),
        in_specs=[pl.BlockSpec((GLA_TC, GLA_QK), tok(ODD_Q_COL // GLA_QK)),
                  pl.BlockSpec((GLA_TC, GLA_QK), tok(ODD_K_COL // GLA_QK)),
                  pl.BlockSpec((GLA_TC, GLA_V), tok(ODD_V_COL // GLA_V)),
                  pl.BlockSpec((GLA_TC, GLA_V), tok(ODD_R_COL // GLA_V)),
                  pl.BlockSpec((GLA_TC, LANE), tok(ODD_LR_COL // LANE)),
                  _resident((LANE, GLA_QK)), _resident((1, GLA_QK)), _resident((1, GLA_DV)),
                  _resident((GLA_CHUNK, GLA_CHUNK))],
        out_specs=pl.BlockSpec((GLA_TC, GLA_V), tok(0)),
        out_shape=jax.ShapeDtypeStruct((TOKENS, GLA_V), BF16),
        scratch_shapes=[pltpu.VMEM((GLA_HEADS, GLA_DV, GLA_DK), F32)],
        compiler_params=pltpu.CompilerParams(dimension_semantics=("arbitrary", "arbitrary"),
                                             vmem_limit_bytes=VMEM_LIMIT),
        name="gated_linear_attention",
    )(p32, p32, p32, p32, p32, aw, a_b.reshape(1, GLA_QK), o_norm.reshape(1, GLA_DV), tri)


def _pad_cols(w, n):
    return jnp.pad(w, ((0, 0), (0, n - w.shape[1])))


def kernel(x, norm_gains, ffn_w_in, ffn_conv_w, ffn_conv_b, ffn_w_out, hyb_w_in, hyb_w_out, gdn_conv_w, gdn_A_log,
           gdn_dt_bias, gdn_o_norm, nsa_cmp_pe, nsa_cmp_w1, nsa_cmp_b1, nsa_cmp_w2, gla_w_in, gla_a_w2, gla_a_b,
           gla_o_norm, gla_w_out):
    x2 = x.reshape(TOKENS, D_MODEL)
    ffn_w_in_bf = ffn_w_in.astype(BF16)
    ffn_w_out_bf = ffn_w_out.astype(BF16)
    for layer in range(DEPTH):
        if layer % 2 == 0:
            e = layer // 2
            p32, p16 = norm_proj(x2, norm_gains[layer, 0], even_proj_weight(hyb_w_in[e]),
                                 ((EVEN_F32_W, F32), (EVEN_BF16_W, BF16)))
            oa, ob = even_mixer_core(p32, p16, gdn_conv_w[e], gdn_A_log[e], gdn_dt_bias[e], gdn_o_norm[e],
                                     nsa_cmp_pe[e], nsa_cmp_w1[e], nsa_cmp_b1[e], nsa_cmp_w2[e])
            halves = (oa, 0, ob, 0)
            w_mix = hyb_w_out[e]
        else:
            o_ = layer // 2
            w = _pad_cols(gla_w_in[o_], ODD_F32_W).astype(BF16)
            (p32,) = norm_proj(x2, norm_gains[layer, 0], w, ((ODD_F32_W, F32),))
            o = gated_linear_attention(p32, gla_a_w2[o_], gla_a_b[o_], gla_o_norm[o_])
            halves = (o, 0, o, 1)
            w_mix = gla_w_out[o_]
        x2 = mix_out_conv_ffn(*halves, x2, norm_gains[layer, 1], w_mix.astype(BF16), norm_gains[layer, 2],
                              layer, ffn_w_in_bf, ffn_conv_w, ffn_conv_b, ffn_w_out_bf, norm_gains[layer, 3])
    return x2.reshape(BATCH, SEQ, D_MODEL)
```

```python
import functools
import math

import jax
import jax.numpy as jnp
import numpy as np
from jax import lax
from jax.experimental import pallas as pl
from jax.experimental.pallas import tpu as pltpu

D_MODEL = 1024
BATCH = 8
SEQ = 2048
DEPTH = 4
TOKENS = BATCH * SEQ

EPS = 1e-6
NEG = -1e30
FORCE = 1e4

GDN_DK = 128
GDN_DV = 128
GDN_HEADS = 4
GDN_QK = GDN_HEADS * GDN_DK
GDN_V = GDN_HEADS * GDN_DV
GDN_CONV = 4
GDN_CHUNK = 64

NSA_DH = 64
NSA_HEADS = 8
NSA_GROUPS = 2
NSA_Q = NSA_HEADS * NSA_DH
NSA_KV = NSA_GROUPS * NSA_DH
CMP_LEN = 32
CMP_STRIDE = 16
CMP_HIDDEN = 2 * NSA_DH
SLC_LEN = 64
SLC_TOPK = 8
WINDOW = 512
Q_BLOCK = 128

GLA_HEADS = 4
GLA_DK = 128
GLA_DV = 256
GLA_QK = GLA_HEADS * GLA_DK
GLA_V = GLA_HEADS * GLA_DV
GLA_RANK = 16
GLA_TAU = 16.0
GLA_CHUNK = 64

D_FF = 2816
FFN_CONV = 3

EVEN_SPLITS = (2 * GDN_QK + GDN_V, GDN_HEADS, GDN_HEADS, GDN_V, NSA_Q, 6 * NSA_KV, 3 * NSA_HEADS)
ODD_SPLITS = (GLA_QK, GLA_QK, GLA_V, GLA_V, GLA_RANK)

LANE = 128
VMEM_LIMIT = 56 * 1024 * 1024

F32 = jnp.float32
BF16 = jnp.bfloat16


def _round_up(n, m):
    return (n + m - 1) // m * m


def _resident(shape):
    nd = len(shape)
    return pl.BlockSpec(shape, lambda *_: (0,) * nd, pipeline_mode=pl.Buffered(1))


def _resident_layer(shape, layer):
    nd = len(shape)
    return pl.BlockSpec((None,) + tuple(shape), lambda *_: (layer,) + (0,) * nd, pipeline_mode=pl.Buffered(1))


def _rms(x, g):
    ms = jnp.mean(x * x, axis=-1, keepdims=True)
    return x * lax.rsqrt(ms + EPS) * g


PROJ_TM = 512
PROJ_CH = 512


def _proj_kernel(x_ref, g_ref, w_ref, *o_refs):
    h = _rms(x_ref[...], g_ref[...]).astype(BF16)
    col = 0
    for o_ref in o_refs:
        n = o_ref.shape[1]
        for c0 in range(0, n, PROJ_CH):
            cw = min(PROJ_CH, n - c0)
            y = jnp.dot(h, w_ref[:, col + c0:col + c0 + cw], preferred_element_type=F32)
            o_ref[:, c0:c0 + cw] = y.astype(o_ref.dtype)
        col += n


def norm_proj(x2, gain, w, sections):
    T, D = x2.shape
    N = w.shape[1]
    assert N == sum(n for n, _ in sections) and all(n % LANE == 0 for n, _ in sections)
    return pl.pallas_call(
        _proj_kernel,
        grid=(T // PROJ_TM,),
        in_specs=[pl.BlockSpec((PROJ_TM, D), lambda i: (i, 0)), _resident((1, D)), _resident((D, N))],
        out_specs=[pl.BlockSpec((PROJ_TM, n), lambda i: (i, 0)) for n, _ in sections],
        out_shape=[jax.ShapeDtypeStruct((T, n), dt) for n, dt in sections],
        compiler_params=pltpu.CompilerParams(dimension_semantics=("arbitrary",), vmem_limit_bytes=VMEM_LIMIT),
        name="norm_proj",
    )(x2, gain.reshape(1, D), w)


FFN_TM = 512
FFN_TAIL = 8
FFN_CH = 256
MIX_HALF = D_MODEL // 2
assert D_FF % FFN_CH == 0


def _ffn_kernel(oa_ref, ob_ref, x_ref, g_mix_ref, w_mix_ref, g_in_ref, w_in_ref, cw_ref, cb_ref, w_out_ref, g_out_ref,
                y_ref, tail_ref, act_ref):
    t = pl.program_id(1)

    @pl.when(t == 0)
    def _():
        tail_ref[...] = jnp.zeros(tail_ref.shape, F32)

    mix = (jnp.dot(oa_ref[...], w_mix_ref[0:MIX_HALF, :], preferred_element_type=F32)
           + jnp.dot(ob_ref[...], w_mix_ref[MIX_HALF:, :], preferred_element_type=F32))
    x = x_ref[...] + _rms(mix, g_mix_ref[...])
    h = _rms(x, g_in_ref[...]).astype(BF16)

    def conv_half(c0):
        j = c0 // FFN_CH
        u = jnp.dot(h, w_in_ref[:, c0:c0 + FFN_CH], preferred_element_type=F32)
        ext = jnp.concatenate([tail_ref[j], u], axis=0)
        tail_ref[j] = u[FFN_TM - FFN_TAIL:]
        w = cw_ref[:, c0:c0 + FFN_CH]
        y = pltpu.roll(ext, 2, 0) * w[0:1] + pltpu.roll(ext, 1, 0) * w[1:2] + ext * w[2:3] + cb_ref[:, c0:c0 + FFN_CH]
        return y[FFN_TAIL:]

    for c0 in range(0, D_FF, FFN_CH):
        half_gate = (0.5 * conv_half(c0)).astype(BF16)
        val = conv_half(D_FF + c0).astype(BF16)
        silu = half_gate + half_gate * jnp.tanh(half_gate)
        act_ref[:, c0:c0 + FFN_CH] = silu * val
    down = jnp.dot(act_ref[...], w_out_ref[...], preferred_element_type=F32)
    y_ref[...] = x + _rms(down, g_out_ref[...])


def mix_out_conv_ffn(oa, a_blk, ob, b_blk, x2, g_mix, w_mix, g_in, layer, w_in, conv_w, conv_b, w_out, g_out):
    T, D = x2.shape
    nt = SEQ // FFN_TM
    tok = lambda blk: (lambda b, t: (b * nt + t, blk))
    return pl.pallas_call(
        _ffn_kernel,
        grid=(BATCH, nt),
        in_specs=[pl.BlockSpec((FFN_TM, MIX_HALF), tok(a_blk)), pl.BlockSpec((FFN_TM, MIX_HALF), tok(b_blk)),
                  pl.BlockSpec((FFN_TM, D), tok(0)), _resident((1, D)), _resident((D, D)),
                  _resident((1, D)), _resident_layer((D, 2 * D_FF), layer),
                  _resident_layer((FFN_CONV, 2 * D_FF), layer), _resident_layer((1, 2 * D_FF), layer),
                  _resident_layer((D_FF, D), layer), _resident((1, D))],
        out_specs=pl.BlockSpec((FFN_TM, D), tok(0)),
        out_shape=jax.ShapeDtypeStruct((T, D), F32),
        scratch_shapes=[pltpu.VMEM((2 * D_FF // FFN_CH, FFN_TAIL, FFN_CH), F32), pltpu.VMEM((FFN_TM, D_FF), BF16)],
        compiler_params=pltpu.CompilerParams(dimension_semantics=("arbitrary", "arbitrary"),
                                             vmem_limit_bytes=VMEM_LIMIT),
        name="mix_out_conv_ffn",
    )(oa, ob, x2, g_mix.reshape(1, D), w_mix, g_in.reshape(1, D), w_in, conv_w, conv_b.reshape(DEPTH, 1, -1), w_out,
      g_out.reshape(1, D))


def split_last(x, sizes):
    idx = np.cumsum(sizes)[:-1].tolist()
    return jnp.split(x, idx, axis=-1)


def _dot_nt(a, b):
    return lax.dot_general(a, b, (((1,), (1,)), ((), ())), preferred_element_type=F32)


def _dot_tn(a, b):
    return lax.dot_general(a, b, (((0,), (0,)), ((), ())), preferred_element_type=F32)


def _split3(a):
    hi = a.astype(BF16)
    r = a - hi.astype(F32)
    mid = r.astype(BF16)
    return hi, mid, (r - mid.astype(F32)).astype(BF16)


def _dot_x3(a, b):
    a_hi, a_lo, _ = _split3(a)
    b_hi, b_lo, _ = _split3(b)
    d = lambda p, q: jnp.dot(p, q, preferred_element_type=F32)
    return d(a_hi, b_hi) + (d(a_lo, b_hi) + d(a_hi, b_lo))


def _cumsum_rows(tri, g):
    return sum(jnp.dot(tri, t, preferred_element_type=F32) for t in reversed(_split3(g)))


def _unit_lower_inverses(mats):
    c = mats[0].shape[0]
    eye = jnp.where(lax.broadcasted_iota(jnp.int32, (c, c), 0) == lax.broadcasted_iota(jnp.int32, (c, c), 1), 1.0, 0.0)
    dot = lambda a, b: jnp.dot(a.astype(BF16), b.astype(BF16), preferred_element_type=F32)
    ps = [-a for a in mats]
    ts = [eye + p for p in ps]
    k = 2
    while k < c:
        ps = [dot(p, p) for p in ps]
        ts = [t + dot(t, p) for t, p in zip(ts, ps)]
        k *= 2
    res = [(eye - t) - _dot_x3(a, t) for a, t in zip(mats, ts)]
    return [t + dot(t, r) for t, r in zip(ts, res)]


GDN_TC = 256
GDN_HALO = 8
GDN_SMALL_BETA = 0
GDN_SMALL_DECAY = GDN_HEADS
NSA_SMALL_GATE = 2 * GDN_HEADS


def _gdn_kernel(x_ref, gate_ref, small_ref, cw_ref, alog_ref, dtb_ref, onorm_ref, tri_ref, o_ref, xext_ref, state_ref):
    t = pl.program_id(1)
    C = GDN_CHUNK

    @pl.when(t == 0)
    def _():
        xext_ref[0:GDN_HALO, :] = jnp.zeros((GDN_HALO, xext_ref.shape[1]), F32)
        state_ref[...] = jnp.zeros(state_ref.shape, F32)

    xext_ref[GDN_HALO:, :] = x_ref[...]
    small = small_ref[...]
    beta_all = jax.nn.sigmoid(small)
    z = small + dtb_ref[...]
    softplus = jnp.maximum(z, 0.0) + jnp.log(1.0 + jnp.exp(-jnp.abs(z)))
    g_all = -jnp.exp(alog_ref[...]) * softplus
    tri = tri_ref[...]
    row = lax.broadcasted_iota(jnp.int32, (C, C), 0)
    col = lax.broadcasted_iota(jnp.int32, (C, C), 1)

    def conv_silu(c0, width):
        xe = xext_ref[:, c0:c0 + width]
        w = cw_ref[:, c0:c0 + width]
        y = xe * w[GDN_CONV - 1:GDN_CONV]
        for s in range(1, GDN_CONV):
            y = y + pltpu.roll(xe, s, 0) * w[GDN_CONV - 1 - s:GDN_CONV - s]
        y = y[GDN_HALO:]
        return y * jax.nn.sigmoid(y)

    n_chunks = GDN_TC // C
    gsums = [_cumsum_rows(tri, g_all[c * C:(c + 1) * C]) for c in range(n_chunks)]
    gsum_ts = [jnp.transpose(gs) for gs in gsums]

    work = {}
    for h in range(GDN_HEADS):
        q_all = conv_silu(h * GDN_DK, GDN_DK)
        k_all = conv_silu(GDN_QK + h * GDN_DK, GDN_DK)
        v_all = conv_silu(2 * GDN_QK + h * GDN_DV, GDN_DV)
        q_all = q_all * (lax.rsqrt(jnp.sum(q_all * q_all, axis=-1, keepdims=True) + EPS) * GDN_DK ** -0.5)
        k_all = k_all * lax.rsqrt(jnp.sum(k_all * k_all, axis=-1, keepdims=True) + EPS)
        for c in range(n_chunks):
            rows = slice(c * C, (c + 1) * C)
            q, k, v = q_all[rows], k_all[rows], v_all[rows]
            beta = beta_all[rows, GDN_SMALL_BETA + h:GDN_SMALL_BETA + h + 1]
            gcol = gsums[c][:, GDN_SMALL_DECAY + h:GDN_SMALL_DECAY + h + 1]
            grow = gsum_ts[c][GDN_SMALL_DECAY + h:GDN_SMALL_DECAY + h + 1, :]
            glast = gcol[C - 1:C]
            decay = jnp.exp(jnp.where(row >= col, gcol - grow, -jnp.inf))
            kb = k * beta
            kbf = k.astype(BF16)
            eg = jnp.exp(gcol)
            work[h, c] = dict(
                a_kk=jnp.where(row > col, _dot_nt(kb.astype(BF16), kbf) * decay, 0.0),
                a_qk=(_dot_nt(q.astype(BF16), kbf) * decay).astype(BF16),
                rhs=jnp.concatenate([v * beta, kb * eg], axis=-1),
                qg=q * eg,
                kd_t=jnp.transpose(k * jnp.exp(glast - gcol)).astype(BF16),
                keep=jnp.exp(glast))
    keys = sorted(work)
    inverses = _unit_lower_inverses([work[key]["a_kk"] for key in keys])
    wks = [work[key] for key in keys]
    sols = [_dot_x3(inv, wk["rhs"]) for wk, inv in zip(wks, inverses)]
    us = [sol[:, :GDN_DV].astype(BF16) for sol in sols]
    ws = [sol[:, GDN_DV:].astype(BF16) for sol in sols]
    dot = lambda a, b: jnp.dot(a, b, preferred_element_type=F32)
    for wk, s_in in zip(wks, [dot(wk["kd_t"], u) for wk, u in zip(wks, us)]):
        wk["s_in"] = s_in
    for wk, s_mix in zip(wks, [dot(wk["kd_t"], w) for wk, w in zip(wks, ws)]):
        wk["s_mix"] = s_mix.astype(BF16)
    for wk, o_in in zip(wks, [dot(wk["a_qk"], u) for wk, u in zip(wks, us)]):
        wk["o_in"] = o_in
    for wk, aw in zip(wks, [dot(wk["a_qk"], w) for wk, w in zip(wks, ws)]):
        wk["o_mix"] = (wk["qg"] - aw).astype(BF16)

    states = [state_ref[h] for h in range(GDN_HEADS)]
    for c in range(n_chunks):
        rows = slice(c * C, (c + 1) * C)
        for h in range(GDN_HEADS):
            wk = work[h, c]
            sb = states[h].astype(BF16)
            o = wk["o_in"] + jnp.dot(wk["o_mix"], sb, preferred_element_type=F32)
            states[h] = (states[h] * wk["keep"] + wk["s_in"]) - jnp.dot(wk["s_mix"], sb, preferred_element_type=F32)
            gate = gate_ref[rows, h * GDN_DV:(h + 1) * GDN_DV]
            on = o * lax.rsqrt(jnp.mean(o * o, axis=-1, keepdims=True) + EPS) * onorm_ref[...]
            o_ref[rows, h * GDN_DV:(h + 1) * GDN_DV] = (on * (gate * jax.nn.sigmoid(gate))).astype(o_ref.dtype)
    for h in range(GDN_HEADS):
        state_ref[h] = states[h]
    xext_ref[0:GDN_HALO, :] = xext_ref[GDN_TC:GDN_TC + GDN_HALO, :]


def gated_delta_net(p32, conv_w, a_log, dt_bias, o_norm, qkv_col, gate_col, small_col):
    nt = SEQ // GDN_TC
    width = 2 * GDN_QK + GDN_V
    lane_vec = lambda a, off: jnp.zeros((1, LANE), F32).at[0, off:off + GDN_HEADS].set(a.astype(F32))
    tri = jnp.asarray(np.tril(np.ones((GDN_CHUNK, GDN_CHUNK))), BF16)
    tok = lambda blk: (lambda b, t: (b * nt + t, blk))
    return pl.pallas_call(
        _gdn_kernel,
        grid=(BATCH, nt),
        in_specs=[pl.BlockSpec((GDN_TC, width), tok(qkv_col // width)),
                  pl.BlockSpec((GDN_TC, GDN_V), tok(gate_col // GDN_V)),
                  pl.BlockSpec((GDN_TC, LANE), tok(small_col // LANE)),
                  _resident((GDN_CONV, width)), _resident((1, LANE)), _resident((1, LANE)), _resident((1, GDN_DV)),
                  _resident((GDN_CHUNK, GDN_CHUNK))],
        out_specs=pl.BlockSpec((GDN_TC, GDN_V), tok(0)),
        out_shape=jax.ShapeDtypeStruct((TOKENS, GDN_V), BF16),
        scratch_shapes=[pltpu.VMEM((GDN_HALO + GDN_TC, width), F32), pltpu.VMEM((GDN_HEADS, GDN_DK, GDN_DV), F32)],
        compiler_params=pltpu.CompilerParams(dimension_semantics=("arbitrary", "arbitrary"),
                                             vmem_limit_bytes=VMEM_LIMIT),
        name="gated_delta_net",
    )(p32, p32, p32, conv_w, lane_vec(a_log, GDN_SMALL_DECAY), lane_vec(dt_bias, GDN_SMALL_DECAY),
      o_norm.reshape(1, GDN_DV), tri)


N_SLC = SEQ // SLC_LEN
N_CMP = SEQ // CMP_STRIDE
NSA_HG = NSA_HEADS // NSA_GROUPS
NSA_TQ = 128
SEL_STEP = 512
WIN_W = WINDOW + NSA_TQ
NSA_SCALE = NSA_DH ** -0.5


def _cmp_kernel(kc_ref, vc_ref, pe_ref, w1_ref, b1_ref, w2_ref, o_ref, strip_ref):
    half = CMP_STRIDE * NSA_DH
    for kv, x_ref in enumerate((kc_ref, vc_ref)):
        for p in range(CMP_STRIDE):
            xp = x_ref[pl.ds(p, N_CMP, stride=CMP_STRIDE), :]
            for g in range(NSA_GROUPS):
                strip_ref[kv * NSA_GROUPS + g, :, p * NSA_DH:(p + 1) * NSA_DH] = xp[:, g * NSA_DH:(g + 1) * NSA_DH]
    for kv in range(2):
        for g in range(NSA_GROUPS):
            rows = strip_ref[kv * NSA_GROUPS + g]
            lo = (rows + pe_ref[kv, :, :half]).astype(BF16)
            hi = (rows + pe_ref[kv, :, half:]).astype(BF16)
            a = jnp.dot(lo, w1_ref[kv, :half, :], preferred_element_type=F32)
            b = jnp.dot(hi, w1_ref[kv, half:, :], preferred_element_type=F32)
            hid = a + pltpu.roll(b, N_CMP - 1, 0) + b1_ref[kv]
            act = (hid * jax.nn.sigmoid(hid)).astype(BF16)
            o_ref[0, kv * NSA_GROUPS + g] = jnp.dot(act, w2_ref[kv], preferred_element_type=F32).astype(BF16)


def nsa_compress(p32, cmp_col, pe, w1, b1, w2):
    n = 2 * NSA_GROUPS
    return pl.pallas_call(
        _cmp_kernel,
        grid=(BATCH,),
        in_specs=[pl.BlockSpec((SEQ, NSA_KV), lambda b: (b, cmp_col // NSA_KV)),
                  pl.BlockSpec((SEQ, NSA_KV), lambda b: (b, cmp_col // NSA_KV + 1)),
                  _resident((2, 1, CMP_LEN * NSA_DH)), _resident((2, CMP_LEN * NSA_DH, CMP_HIDDEN)),
                  _resident((2, 1, CMP_HIDDEN)), _resident((2, CMP_HIDDEN, NSA_DH))],
        out_specs=pl.BlockSpec((1, n, N_CMP, NSA_DH), lambda b: (b, 0, 0, 0)),
        out_shape=jax.ShapeDtypeStruct((BATCH, n, N_CMP, NSA_DH), BF16),
        scratch_shapes=[pltpu.VMEM((n, N_CMP, CMP_STRIDE * NSA_DH), F32)],
        compiler_params=pltpu.CompilerParams(dimension_semantics=("arbitrary",), vmem_limit_bytes=VMEM_LIMIT),
        name="nsa_compress",
    )(p32, p32, pe.reshape(2, 1, -1), w1.astype(BF16), b1.reshape(2, 1, -1), w2.astype(BF16))


V1_ROWS = NSA_DH + 16
NSA_PAIR = 2
MASK_BIG = 2.0 ** 100


def _masked_attention_t(k, q_t, v1_t, bias_t):
    s = jnp.dot(k, q_t, preferred_element_type=F32) + jnp.concatenate([bias_t] * NSA_HG, axis=1)
    p = jnp.exp((s - jnp.max(s, axis=0, keepdims=True)).astype(BF16))
    acc = jnp.dot(v1_t, p, preferred_element_type=F32)
    return acc[:NSA_DH] / acc[NSA_DH:NSA_DH + 1]


def _transposed_with_ones_row(eye, v):
    v_t = _dot_nt(eye, v).astype(v.dtype)
    row = lax.broadcasted_iota(jnp.int32, (V1_ROWS - NSA_DH, v.shape[0]), 0)
    return jnp.concatenate([v_t, jnp.where(row == 0, 1.0, 0.0).astype(v.dtype)], axis=0)


def _nsa_kernel(q_ref, ks_ref, vs_ref, kw_ref, vw_ref, cmp_ref, gate_ref, ovl_ref, eye_ref, expand_ref,
                o_ref, ksx_ref, qt_ref, ocmp_ref, gsel_ref, gwin_ref, vs1_ref, vw1_ref, vct_ref):
    i = pl.program_id(1)
    q0 = i * NSA_TQ
    rows = NSA_HG * NSA_TQ
    eye = eye_ref[...]
    eye_dh = eye[0:NSA_DH, 0:NSA_DH]

    @pl.when(i == 0)
    def _():
        for g in range(NSA_GROUPS):
            lanes = slice(g * NSA_DH, (g + 1) * NSA_DH)
            vs1_ref[g] = _transposed_with_ones_row(eye_dh, vs_ref[:, lanes])
            vw1_ref[g] = _transposed_with_ones_row(eye_dh, vw_ref[:, lanes])
            vct_ref[g] = _dot_nt(eye_dh, cmp_ref[0, NSA_GROUPS + g]).astype(BF16)
            ksx_ref[g] = jnp.concatenate(
                [ks_ref[:, lanes], expand_ref[...], jnp.zeros((SEQ, LANE - NSA_DH - N_SLC), BF16)], axis=1)
            qt_ref[g, NSA_DH + N_SLC:, :] = jnp.zeros((LANE - NSA_DH - N_SLC, NSA_HG * NSA_TQ), BF16)

    gates = jax.nn.sigmoid(gate_ref[...])
    gates_t = jnp.transpose(gates)
    tok = q0 + lax.broadcasted_iota(jnp.int32, (NSA_TQ, 1), 0)
    tok_l = q0 + lax.broadcasted_iota(jnp.int32, (1, NSA_TQ), 1)

    groups = range(NSA_GROUPS)
    scale = jnp.asarray(NSA_SCALE, BF16)
    for pair in range(NSA_HEADS // 2):
        pair_t = _dot_nt(eye, q_ref[:, pair * LANE:(pair + 1) * LANE] * scale).astype(BF16)
        for k in range(2):
            g, hk = divmod(2 * pair + k, NSA_HG)
            qt_ref[g, 0:NSA_DH, hk * NSA_TQ:(hk + 1) * NSA_TQ] = pair_t[k * NSA_DH:(k + 1) * NSA_DH]

    def gate_row(g, branch):
        lane = NSA_SMALL_GATE + branch * NSA_HEADS + g * NSA_HG
        return jnp.concatenate([gates_t[lane + hk:lane + hk + 1, :] for hk in range(NSA_HG)], axis=1)

    cmp_end = lax.broadcasted_iota(jnp.int32, (N_CMP, 1), 0) * CMP_STRIDE + (CMP_LEN - 1)
    cvalid = cmp_end <= jnp.concatenate([tok_l] * NSA_HG, axis=1)
    ss = [jnp.dot(cmp_ref[0, g], qt_ref[g, 0:NSA_DH, :], preferred_element_type=F32) for g in groups]
    sms = [jnp.where(cvalid, s, NEG) for s in ss]
    es = [jnp.exp(sm - jnp.max(sm, axis=0, keepdims=True)) for sm in sms]
    ps = [jnp.where(cvalid, e / jnp.sum(e, axis=0, keepdims=True), 0.0) for e in es]
    for g in groups:
        ocmp_ref[g] = gate_row(g, 0) * jnp.dot(vct_ref[g], ps[g].astype(BF16), preferred_element_type=F32)
        gsel_ref[g] = gate_row(g, 1)
        gwin_ref[g] = gate_row(g, 2)

    psums = [sum(p[:, hk * NSA_TQ:(hk + 1) * NSA_TQ] for hk in range(NSA_HG)) for p in ps]
    imps = [jnp.dot(ovl_ref[...], psum.astype(BF16), preferred_element_type=F32) for psum in psums]
    blk = lax.broadcasted_iota(jnp.int32, (N_SLC, NSA_TQ), 0)
    cur = tok_l // SLC_LEN
    bvalid = blk <= cur
    forced = (blk == 0) | (blk == cur) | (blk == cur - 1)
    scores_ = [jnp.where(bvalid, imp + jnp.where(forced, FORCE, 0.0), NEG) for imp in imps]
    ranks = [jnp.zeros((N_SLC, NSA_TQ), F32) for _ in groups]
    for j in range(N_SLC):
        for g in groups:
            sj = scores_[g][j:j + 1, :]
            tie = jnp.where((sj == scores_[g]) & (blk > j), 1.0, 0.0)
            ranks[g] = ranks[g] + jnp.where(sj > scores_[g], 1.0, tie)
    for g in groups:
        off = jnp.where((ranks[g] < SLC_TOPK) & bvalid, 0.0, -MASK_BIG).astype(BF16)
        qt_ref[g, NSA_DH:NSA_DH + N_SLC, :] = jnp.concatenate([off] * NSA_HG, axis=1)

    for n in range(SEQ // SEL_STEP):
        @pl.when(i // (SEL_STEP // NSA_TQ) == n)
        def _():
            w = (n + 1) * SEL_STEP
            w0 = pl.multiple_of(jnp.maximum(q0 + NSA_TQ - WIN_W, 0), NSA_TQ)
            r = pl.ds(w0, WIN_W)

            def bias(pos, lo_excl):
                ok = (pos <= tok_l) if lo_excl is None else ((pos <= tok_l) & (pos > lo_excl))
                return jnp.concatenate([jnp.where(ok, 0.0, NEG)] * NSA_PAIR, axis=1)

            sel_causal = bias((w - SEL_STEP) + lax.broadcasted_iota(jnp.int32, (SEL_STEP, 1), 0), None)
            wpos = w0 + lax.broadcasted_iota(jnp.int32, (WIN_W, 1), 0)
            if n == 0:
                win_edges = [(0, WIN_W, bias(wpos, tok_l - WINDOW))]
            else:
                win_edges = [(0, NSA_TQ, bias(wpos[:NSA_TQ], tok_l - WINDOW)),
                             (WIN_W - NSA_TQ, WIN_W, bias(wpos[WIN_W - NSA_TQ:], None))]

            def add_bias(s, edges):
                pieces, at = [], 0
                for lo, hi, b in edges:
                    pieces += [s[at:lo], s[lo:hi] + b]
                    at = hi
                return jnp.concatenate([p for p in pieces + [s[at:]] if p.shape[0]], axis=0)

            problems = []
            for g in range(NSA_GROUPS):
                lanes = slice(g * NSA_DH, (g + 1) * NSA_DH)
                for half in range(NSA_HG // NSA_PAIR):
                    cols = slice(half * NSA_PAIR * NSA_TQ, (half + 1) * NSA_PAIR * NSA_TQ)
                    problems.append((g, cols, 0, lambda g=g: ksx_ref[g, 0:w, :], lambda g=g: vs1_ref[g, :, 0:w],
                                     lambda g=g, cols=cols: qt_ref[g, :, cols], [(w - SEL_STEP, w, sel_causal)]))
                    problems.append((g, cols, 1, lambda lanes=lanes: kw_ref[r, lanes], lambda g=g: vw1_ref[g, :, r],
                                     lambda g=g, cols=cols: qt_ref[g, 0:NSA_DH, cols], win_edges))

            def scores(prob):
                _, _, _, k, _, q_t, edges = prob
                return add_bias(jnp.dot(k(), q_t(), preferred_element_type=F32), edges)

            def probs(s):
                return jnp.exp((s - jnp.max(s, axis=0, keepdims=True)).astype(BF16))

            def values(prob, p):
                acc = jnp.dot(prob[4](), p, preferred_element_type=F32)
                return acc[:NSA_DH] / acc[NSA_DH:NSA_DH + 1]

            problems = problems[1:2] + problems[0:1] + problems[2:]
            outs = {}
            s_next = scores(problems[0])
            for idx, prob in enumerate(problems):
                s_cur = s_next
                if idx + 1 < len(problems):
                    s_next = scores(problems[idx + 1])
                outs[prob[0], prob[1].start, prob[2]] = values(prob, probs(s_cur))

            for g in range(NSA_GROUPS):
                for half in range(NSA_HG // NSA_PAIR):
                    c0 = half * NSA_PAIR * NSA_TQ
                    cols = slice(c0, c0 + NSA_PAIR * NSA_TQ)
                    o_t = (ocmp_ref[g, :, cols] + gsel_ref[g, :, cols] * outs[g, c0, 0]
                           + gwin_ref[g, :, cols] * outs[g, c0, 1])
                    for k in range(NSA_PAIR):
                        h = g * NSA_HG + half * NSA_PAIR + k
                        o = jnp.transpose(o_t[:, k * NSA_TQ:(k + 1) * NSA_TQ])
                        o_ref[:, h * NSA_DH:(h + 1) * NSA_DH] = o.astype(o_ref.dtype)


def _nsa_constants():
    c0 = np.arange(N_CMP)[:, None] * CMP_STRIDE
    s0 = np.arange(N_SLC)[None, :] * SLC_LEN
    overlap = np.clip(np.minimum(c0 + CMP_LEN, s0 + SLC_LEN) - np.maximum(c0, s0), 0, None) / CMP_STRIDE
    overlap[N_CMP - 1] = 0.0
    expand_t = (np.arange(SEQ)[:, None] // SLC_LEN == np.arange(N_SLC)[None, :]).astype(np.float32)
    return jnp.asarray(overlap.T, BF16), jnp.asarray(np.eye(LANE), BF16), jnp.asarray(expand_t, BF16)


def nsa_attention(p16, cmp_kv, p32, q_col, kv_col, small_col):
    nq = SEQ // NSA_TQ
    ovl_t, eye, expand = _nsa_constants()
    kv_spec = lambda n: pl.BlockSpec((SEQ, NSA_KV), lambda b, i: (b, kv_col // NSA_KV + n))
    return pl.pallas_call(
        _nsa_kernel,
        grid=(BATCH, nq),
        in_specs=[pl.BlockSpec((NSA_TQ, NSA_Q), lambda b, i: (b * nq + i, q_col // NSA_Q)),
                  kv_spec(0), kv_spec(1), kv_spec(2), kv_spec(3),
                  pl.BlockSpec((1, 2 * NSA_GROUPS, N_CMP, NSA_DH), lambda b, i: (b, 0, 0, 0)),
                  pl.BlockSpec((NSA_TQ, LANE), lambda b, i: (b * nq + i, small_col // LANE)),
                  _resident(ovl_t.shape), _resident(eye.shape), _resident(expand.shape)],
        out_specs=pl.BlockSpec((NSA_TQ, NSA_Q), lambda b, i: (b * nq + i, 0)),
        out_shape=jax.ShapeDtypeStruct((TOKENS, NSA_Q), BF16),
        scratch_shapes=[pltpu.VMEM((NSA_GROUPS, SEQ, LANE), BF16),
                        pltpu.VMEM((NSA_GROUPS, LANE, NSA_HG * NSA_TQ), BF16),
                        pltpu.VMEM((NSA_GROUPS, NSA_DH, NSA_HG * NSA_TQ), F32)]
        + [pltpu.VMEM((NSA_GROUPS, 1, NSA_HG * NSA_TQ), F32)] * 2
        + [pltpu.VMEM((NSA_GROUPS, V1_ROWS, SEQ), BF16)] * 2
        + [pltpu.VMEM((NSA_GROUPS, NSA_DH, N_CMP), BF16)],
        compiler_params=pltpu.CompilerParams(dimension_semantics=("arbitrary", "arbitrary"),
                                             vmem_limit_bytes=VMEM_LIMIT),
        name="nsa_attention",
    )(p16, p16, p16, p16, p16, cmp_kv, p32, ovl_t, eye, expand)


EVEN_QKV_COL = 0
EVEN_GATE_COL = 2 * GDN_QK + GDN_V
EVEN_CMP_COL = EVEN_GATE_COL + GDN_V
EVEN_SMALL_COL = EVEN_CMP_COL + 2 * NSA_KV
EVEN_F32_W = EVEN_SMALL_COL + LANE
EVEN_Q_COL = 0
EVEN_KV_COL = NSA_Q
EVEN_BF16_W = NSA_Q + 4 * NSA_KV
assert NSA_SMALL_GATE + 3 * NSA_HEADS <= LANE


def even_proj_weight(w_in):
    qkv_a, beta, dec, gate_a, q_b, kv_b, gate_b = split_last(w_in, EVEN_SPLITS)
    pad = jnp.zeros((w_in.shape[0], LANE - 2 * GDN_HEADS - 3 * NSA_HEADS), w_in.dtype)
    return jnp.concatenate([qkv_a, gate_a, kv_b[:, :2 * NSA_KV], beta, dec, gate_b, pad, q_b, kv_b[:, 2 * NSA_KV:]],
                           axis=1).astype(BF16)


def even_mixer_core(p32, p16, conv_w, A_log, dt_bias, o_norm, cmp_pe, cmp_w1, cmp_b1, cmp_w2):
    oa = gated_delta_net(p32, conv_w, A_log, dt_bias, o_norm, EVEN_QKV_COL, EVEN_GATE_COL, EVEN_SMALL_COL)
    cmp_kv = nsa_compress(p32, EVEN_CMP_COL, cmp_pe, cmp_w1, cmp_b1, cmp_w2)
    ob = nsa_attention(p16, cmp_kv, p32, EVEN_Q_COL, EVEN_KV_COL, EVEN_SMALL_COL)
    return oa, ob


GLA_TC = 256
ODD_Q_COL = 0
ODD_K_COL = GLA_QK
ODD_V_COL = 2 * GLA_QK
ODD_R_COL = ODD_V_COL + GLA_V
ODD_LR_COL = ODD_R_COL + GLA_V
ODD_F32_W = ODD_LR_COL + LANE


def _gla_kernel(q_ref, k_ref, v_ref, r_ref, lr_ref, aw_ref, ab_ref, onorm_ref, tri_ref, o_ref, state_ref):
    t = pl.program_id(1)
    C = GLA_CHUNK

    @pl.when(t == 0)
    def _():
        state_ref[...] = jnp.zeros(state_ref.shape, F32)

    z = jnp.dot(lr_ref[...].astype(BF16), aw_ref[...], preferred_element_type=F32) + ab_ref[...]
    log_a = (jnp.minimum(z, 0.0) - jnp.log(1.0 + jnp.exp(-jnp.abs(z)))) * (1.0 / GLA_TAU)
    tri = tri_ref[...]
    causal = lax.broadcasted_iota(jnp.int32, (C, C), 0) >= lax.broadcasted_iota(jnp.int32, (C, C), 1)

    n_chunks = GLA_TC // C
    b_alls = [_cumsum_rows(tri, log_a[c * C:(c + 1) * C]) for c in range(n_chunks)]
    work = {}
    for c in range(n_chunks):
        rows = slice(c * C, (c + 1) * C)
        for h in range(GLA_HEADS):
            kl = slice(h * GLA_DK, (h + 1) * GLA_DK)
            b = b_alls[c][:, kl]
            b_last = b[C - 1:C]
            k = k_ref[rows, kl]
            work[c, h] = dict(
                rows=rows, vl=slice(h * GLA_DV, (h + 1) * GLA_DV),
                v=v_ref[rows, h * GLA_DV:(h + 1) * GLA_DV].astype(BF16),
                qt=(q_ref[rows, kl] * GLA_DK ** -0.5 * jnp.exp(b)).astype(BF16),
                kt=(k * jnp.exp(-b)).astype(BF16),
                kh=(k * jnp.exp(b_last - b)).astype(BF16),
                keep=jnp.exp(b_last))
    wks = [work[key] for key in sorted(work)]
    intras = [jnp.where(causal, _dot_nt(wk["qt"], wk["kt"]), 0.0).astype(BF16) for wk in wks]
    o_intras = [jnp.dot(intra, wk["v"], preferred_element_type=F32) for wk, intra in zip(wks, intras)]
    updates = [_dot_tn(wk["v"], wk["kh"]) for wk in wks]

    states = [state_ref[h] for h in range(GLA_HEADS)]
    starts = []
    for (c, h), wk, upd in zip(sorted(work), wks, updates):
        starts.append(states[h].astype(BF16))
        states[h] = states[h] * wk["keep"] + upd
    for h in range(GLA_HEADS):
        state_ref[h] = states[h]

    o_inters = [_dot_nt(wk["qt"], s0) for wk, s0 in zip(wks, starts)]
    for wk, o_intra, o_inter in zip(wks, o_intras, o_inters):
        o = o_intra + o_inter
        r = r_ref[wk["rows"], wk["vl"]]
        on = o * lax.rsqrt(jnp.mean(o * o, axis=-1, keepdims=True) + EPS) * onorm_ref[...]
        o_ref[wk["rows"], wk["vl"]] = (on * (r * jax.nn.sigmoid(r))).astype(o_ref.dtype)


def gated_linear_attention(p32, a_w2, a_b, o_norm):
    nt = SEQ // GLA_TC
    tok = lambda blk: (lambda b, t: (b * nt + t, blk))
    aw = jnp.pad(a_w2, ((0, LANE - GLA_RANK), (0, 0))).astype(BF16)
    tri = jnp.asarray(np.tril(np.ones((GLA_CHUNK, GLA_CHUNK))), BF16)
    return pl.pallas_call(
        _gla_kernel,
        grid=(BATCH, nt),
        in_specs=[pl.BlockSpec((GLA_TC, GLA_QK), tok(ODD_Q_COL // GLA_QK)),
                  pl.BlockSpec((GLA_TC, GLA_QK), tok(ODD_K_COL // GLA_QK)),
                  pl.BlockSpec((GLA_TC, GLA_V), tok(ODD_V_COL // GLA_V)),
                  pl.BlockSpec((GLA_TC, GLA_V), tok(ODD_R_COL // GLA_V)),
                  pl.BlockSpec((GLA_TC, LANE), tok(ODD_LR_COL // LANE)),
                  _resident((LANE, GLA_QK)), _resident((1, GLA_QK)), _resident((1, GLA_DV)),
                  _resident((GLA_CHUNK, GLA_CHUNK))],
        out_specs=pl.BlockSpec((GLA_TC, GLA_V), tok(0)),
        out_shape=jax.ShapeDtypeStruct((TOKENS, GLA_V), BF16),
        scratch_shapes=[pltpu.VMEM((GLA_HEADS, GLA_DV, GLA_DK), F32)],
        compiler_params=pltpu.CompilerParams(dimension_semantics=("arbitrary", "arbitrary"),
                                             vmem_limit_bytes=VMEM_LIMIT),
        name="gated_linear_attention",
    )(p32, p32, p32, p32, p32, aw, a_b.reshape(1, GLA_QK), o_norm.reshape(1, GLA_DV), tri)


def _pad_cols(w, n):
    return jnp.pad(w, ((0, 0), (0, n - w.shape[1])))


def kernel(x, norm_gains, ffn_w_in, ffn_conv_w, ffn_conv_b, ffn_w_out, hyb_w_in, hyb_w_out, gdn_conv_w, gdn_A_log,
           gdn_dt_bias, gdn_o_norm, nsa_cmp_pe, nsa_cmp_w1, nsa_cmp_b1, nsa_cmp_w2, gla_w_in, gla_a_w2, gla_a_b,
           gla_o_norm, gla_w_out):
    x2 = x.reshape(TOKENS, D_MODEL)
    ffn_w_in_bf = ffn_w_in.astype(BF16)
    ffn_w_out_bf = ffn_w_out.astype(BF16)
    for layer in range(DEPTH):
        if layer % 2 == 0:
            e = layer // 2
            p32, p16 = norm_proj(x2, norm_gains[layer, 0], even_proj_weight(hyb_w_in[e]),
                                 ((EVEN_F32_W, F32), (EVEN_BF16_W, BF16)))
            oa, ob = even_mixer_core(p32, p16, gdn_conv_w[e], gdn_A_log[e], gdn_dt_bias[e], gdn_o_norm[e],
                                     nsa_cmp_pe[e], nsa_cmp_w1[e], nsa_cmp_b1[e], nsa_cmp_w2[e])
            halves = (oa, 0, ob, 0)
            w_mix = hyb_w_out[e]
        else:
            o_ = layer // 2
            w = _pad_cols(gla_w_in[o_], ODD_F32_W).astype(BF16)
            (p32,) = norm_proj(x2, norm_gains[layer, 0], w, ((ODD_F32_W, F32),))
            o = gated_linear_attention(p32, gla_a_w2[o_], gla_a_b[o_], gla_o_norm[o_])
            halves = (o, 0, o, 1)
            w_mix = gla_w_out[o_]
        x2 = mix_out_conv_ffn(*halves, x2, norm_gains[layer, 1], w_mix.astype(BF16), norm_gains[layer, 2],
                              layer, ffn_w_in_bf, ffn_conv_w, ffn_conv_b, ffn_w_out_bf, norm_gains[layer, 3])
    return x2.reshape(BATCH, SEQ, D_MODEL)
```

```python
import functools
import math

import jax
import jax.numpy as jnp
import numpy as np
from jax import lax
from jax.experimental import pallas as pl
from jax.experimental.pallas import tpu as pltpu

D_MODEL = 1024
BATCH = 8
SEQ = 2048
DEPTH = 4
TOKENS = BATCH * SEQ

EPS = 1e-6
NEG = -1e30
FORCE = 1e4

GDN_DK = 128
GDN_DV = 128
GDN_HEADS = 4
GDN_QK = GDN_HEADS * GDN_DK
GDN_V = GDN_HEADS * GDN_DV
GDN_CONV = 4
GDN_CHUNK = 64

NSA_DH = 64
NSA_HEADS = 8
NSA_GROUPS = 2
NSA_Q = NSA_HEADS * NSA_DH
NSA_KV = NSA_GROUPS * NSA_DH
CMP_LEN = 32
CMP_STRIDE = 16
CMP_HIDDEN = 2 * NSA_DH
SLC_LEN = 64
SLC_TOPK = 8
WINDOW = 512
Q_BLOCK = 128

GLA_HEADS = 4
GLA_DK = 128
GLA_DV = 256
GLA_QK = GLA_HEADS * GLA_DK
GLA_V = GLA_HEADS * GLA_DV
GLA_RANK = 16
GLA_TAU = 16.0
GLA_CHUNK = 64

D_FF = 2816
FFN_CONV = 3

EVEN_SPLITS = (2 * GDN_QK + GDN_V, GDN_HEADS, GDN_HEADS, GDN_V, NSA_Q, 6 * NSA_KV, 3 * NSA_HEADS)
ODD_SPLITS = (GLA_QK, GLA_QK, GLA_V, GLA_V, GLA_RANK)

LANE = 128
VMEM_LIMIT = 56 * 1024 * 1024

F32 = jnp.float32
BF16 = jnp.bfloat16


def _round_up(n, m):
    return (n + m - 1) // m * m


def _resident(shape):
    nd = len(shape)
    return pl.BlockSpec(shape, lambda *_: (0,) * nd, pipeline_mode=pl.Buffered(1))


def _resident_layer(shape, layer):
    nd = len(shape)
    return pl.BlockSpec((None,) + tuple(shape), lambda *_: (layer,) + (0,) * nd, pipeline_mode=pl.Buffered(1))


def _rms(x, g):
    ms = jnp.mean(x * x, axis=-1, keepdims=True)
    return x * lax.rsqrt(ms + EPS) * g


PROJ_TM = 512
PROJ_CH = 512


def _proj_kernel(x_ref, g_ref, w_ref, *o_refs):
    h = _rms(x_ref[...], g_ref[...]).astype(BF16)
    col = 0
    for o_ref in o_refs:
        n = o_ref.shape[1]
        for c0 in range(0, n, PROJ_CH):
            cw = min(PROJ_CH, n - c0)
            y = jnp.dot(h, w_ref[:, col + c0:col + c0 + cw], preferred_element_type=F32)
            o_ref[:, c0:c0 + cw] = y.astype(o_ref.dtype)
        col += n


def norm_proj(x2, gain, w, sections):
    T, D = x2.shape
    N = w.shape[1]
    assert N == sum(n for n, _ in sections) and all(n % LANE == 0 for n, _ in sections)
    return pl.pallas_call(
        _proj_kernel,
        grid=(T // PROJ_TM,),
        in_specs=[pl.BlockSpec((PROJ_TM, D), lambda i: (i, 0)), _resident((1, D)), _resident((D, N))],
        out_specs=[pl.BlockSpec((PROJ_TM, n), lambda i: (i, 0)) for n, _ in sections],
        out_shape=[jax.ShapeDtypeStruct((T, n), dt) for n, dt in sections],
        compiler_params=pltpu.CompilerParams(dimension_semantics=("arbitrary",), vmem_limit_bytes=VMEM_LIMIT),
        name="norm_proj",
    )(x2, gain.reshape(1, D), w)


FFN_TM = 512
FFN_TAIL = 8
FFN_CH = 256
MIX_HALF = D_MODEL // 2
assert D_FF % FFN_CH == 0


def _ffn_kernel(oa_ref, ob_ref, x_ref, g_mix_ref, w_mix_ref, g_in_ref, w_in_ref, cw_ref, cb_ref, w_out_ref, g_out_ref,
                y_ref, tail_ref, act_ref):
    t = pl.program_id(1)

    @pl.when(t == 0)
    def _():
        tail_ref[...] = jnp.zeros(tail_ref.shape, F32)

    mix = (jnp.dot(oa_ref[...], w_mix_ref[0:MIX_HALF, :], preferred_element_type=F32)
           + jnp.dot(ob_ref[...], w_mix_ref[MIX_HALF:, :], preferred_element_type=F32))
    x = x_ref[...] + _rms(mix, g_mix_ref[...])
    h = _rms(x, g_in_ref[...]).astype(BF16)

    def conv_half(c0):
        j = c0 // FFN_CH
        u = jnp.dot(h, w_in_ref[:, c0:c0 + FFN_CH], preferred_element_type=F32)
        ext = jnp.concatenate([tail_ref[j], u], axis=0)
        tail_ref[j] = u[FFN_TM - FFN_TAIL:]
        w = cw_ref[:, c0:c0 + FFN_CH]
        y = pltpu.roll(ext, 2, 0) * w[0:1] + pltpu.roll(ext, 1, 0) * w[1:2] + ext * w[2:3] + cb_ref[:, c0:c0 + FFN_CH]
        return y[FFN_TAIL:]

    for c0 in range(0, D_FF, FFN_CH):
        half_gate = (0.5 * conv_half(c0)).astype(BF16)
        val = conv_half(D_FF + c0).astype(BF16)
        silu = half_gate + half_gate * jnp.tanh(half_gate)
        act_ref[:, c0:c0 + FFN_CH] = silu * val
    down = jnp.dot(act_ref[...], w_out_ref[...], preferred_element_type=F32)
    y_ref[...] = x + _rms(down, g_out_ref[...])


def mix_out_conv_ffn(oa, a_blk, ob, b_blk, x2, g_mix, w_mix, g_in, layer, w_in, conv_w, conv_b, w_out, g_out):
    T, D = x2.shape
    nt = SEQ // FFN_TM
    tok = lambda blk: (lambda b, t: (b * nt + t, blk))
    return pl.pallas_call(
        _ffn_kernel,
        grid=(BATCH, nt),
        in_specs=[pl.BlockSpec((FFN_TM, MIX_HALF), tok(a_blk)), pl.BlockSpec((FFN_TM, MIX_HALF), tok(b_blk)),
                  pl.BlockSpec((FFN_TM, D), tok(0)), _resident((1, D)), _resident((D, D)),
                  _resident((1, D)), _resident_layer((D, 2 * D_FF), layer),
                  _resident_layer((FFN_CONV, 2 * D_FF), layer), _resident_layer((1, 2 * D_FF), layer),
                  _resident_layer((D_FF, D), layer), _resident((1, D))],
        out_specs=pl.BlockSpec((FFN_TM, D), tok(0)),
        out_shape=jax.ShapeDtypeStruct((T, D), F32),
        scratch_shapes=[pltpu.VMEM((2 * D_FF // FFN_CH, FFN_TAIL, FFN_CH), F32), pltpu.VMEM((FFN_TM, D_FF), BF16)],
        compiler_params=pltpu.CompilerParams(dimension_semantics=("arbitrary", "arbitrary"),
                                             vmem_limit_bytes=VMEM_LIMIT),
        name="mix_out_conv_ffn",
    )(oa, ob, x2, g_mix.reshape(1, D), w_mix, g_in.reshape(1, D), w_in, conv_w, conv_b.reshape(DEPTH, 1, -1), w_out,
      g_out.reshape(1, D))


def split_last(x, sizes):
    idx = np.cumsum(sizes)[:-1].tolist()
    return jnp.split(x, idx, axis=-1)


def _dot_nt(a, b):
    return lax.dot_general(a, b, (((1,), (1,)), ((), ())), preferred_element_type=F32)


def _dot_tn(a, b):
    return lax.dot_general(a, b, (((0,), (0,)), ((), ())), preferred_element_type=F32)


def _split3(a):
    hi = a.astype(BF16)
    r = a - hi.astype(F32)
    mid = r.astype(BF16)
    return hi, mid, (r - mid.astype(F32)).astype(BF16)


def _dot_x3(a, b):
    a_hi, a_lo, _ = _split3(a)
    b_hi, b_lo, _ = _split3(b)
    d = lambda p, q: jnp.dot(p, q, preferred_element_type=F32)
    return d(a_hi, b_hi) + (d(a_lo, b_hi) + d(a_hi, b_lo))


def _cumsum_rows(tri, g):
    return sum(jnp.dot(tri, t, preferred_element_type=F32) for t in reversed(_split3(g)))


def _unit_lower_inverses(mats):
    c = mats[0].shape[0]
    eye = jnp.where(lax.broadcasted_iota(jnp.int32, (c, c), 0) == lax.broadcasted_iota(jnp.int32, (c, c), 1), 1.0, 0.0)
    dot = lambda a, b: jnp.dot(a.astype(BF16), b.astype(BF16), preferred_element_type=F32)
    ps = [-a for a in mats]
    ts = [eye + p for p in ps]
    k = 2
    while k < c:
        ps = [dot(p, p) for p in ps]
        ts = [t + dot(t, p) for t, p in zip(ts, ps)]
        k *= 2
    res = [(eye - t) - _dot_x3(a, t) for a, t in zip(mats, ts)]
    return [t + dot(t, r) for t, r in zip(ts, res)]


GDN_TC = 512
GDN_HALO = 8
GDN_SMALL_BETA = 0
GDN_SMALL_DECAY = GDN_HEADS
NSA_SMALL_GATE = 2 * GDN_HEADS


def _gdn_kernel(x_ref, gate_ref, small_ref, cw_ref, alog_ref, dtb_ref, onorm_ref, tri_ref, o_ref, xext_ref, state_ref):
    t = pl.program_id(1)
    C = GDN_CHUNK

    @pl.when(t == 0)
    def _():
        xext_ref[0:GDN_HALO, :] = jnp.zeros((GDN_HALO, xext_ref.shape[1]), F32)
        state_ref[...] = jnp.zeros(state_ref.shape, F32)

    xext_ref[GDN_HALO:, :] = x_ref[...]
    small = small_ref[...]
    beta_all = jax.nn.sigmoid(small)
    z = small + dtb_ref[...]
    softplus = jnp.maximum(z, 0.0) + jnp.log(1.0 + jnp.exp(-jnp.abs(z)))
    g_all = -jnp.exp(alog_ref[...]) * softplus
    tri = tri_ref[...]
    row = lax.broadcasted_iota(jnp.int32, (C, C), 0)
    col = lax.broadcasted_iota(jnp.int32, (C, C), 1)

    def conv_silu(c0, width):
        w = cw_ref[:, c0:c0 + width]
        y = xext_ref[GDN_HALO:, c0:c0 + width] * w[GDN_CONV - 1:GDN_CONV]
        for s in range(1, GDN_CONV):
            y = y + xext_ref[GDN_HALO - s:GDN_HALO - s + GDN_TC, c0:c0 + width] * w[GDN_CONV - 1 - s:GDN_CONV - s]
        return y * jax.nn.sigmoid(y)

    n_chunks = GDN_TC // C
    gsums = [_cumsum_rows(tri, g_all[c * C:(c + 1) * C]) for c in range(n_chunks)]
    gsum_ts = [jnp.transpose(gs) for gs in gsums]

    work = {}
    for h in range(GDN_HEADS):
        q_all = conv_silu(h * GDN_DK, GDN_DK)
        k_all = conv_silu(GDN_QK + h * GDN_DK, GDN_DK)
        v_all = conv_silu(2 * GDN_QK + h * GDN_DV, GDN_DV)
        q_all = q_all * (lax.rsqrt(jnp.sum(q_all * q_all, axis=-1, keepdims=True) + EPS) * GDN_DK ** -0.5)
        k_all = k_all * lax.rsqrt(jnp.sum(k_all * k_all, axis=-1, keepdims=True) + EPS)
        for c in range(n_chunks):
            rows = slice(c * C, (c + 1) * C)
            q, k, v = q_all[rows], k_all[rows], v_all[rows]
            beta = beta_all[rows, GDN_SMALL_BETA + h:GDN_SMALL_BETA + h + 1]
            gcol = gsums[c][:, GDN_SMALL_DECAY + h:GDN_SMALL_DECAY + h + 1]
            grow = gsum_ts[c][GDN_SMALL_DECAY + h:GDN_SMALL_DECAY + h + 1, :]
            glast = gcol[C - 1:C]
            decay = jnp.exp(jnp.where(row >= col, gcol - grow, -jnp.inf))
            kb = k * beta
            kbf = k.astype(BF16)
            eg = jnp.exp(gcol)
            work[h, c] = dict(
                a_kk=jnp.where(row > col, _dot_nt(kb.astype(BF16), kbf) * decay, 0.0),
                a_qk=(_dot_nt(q.astype(BF16), kbf) * decay).astype(BF16),
                rhs=jnp.concatenate([v * beta, kb * eg], axis=-1),
                qg=q * eg,
                kd_t=jnp.transpose(k * jnp.exp(glast - gcol)).astype(BF16),
                keep=jnp.exp(glast))
    keys = sorted(work)
    inverses = _unit_lower_inverses([work[key]["a_kk"] for key in keys])
    wks = [work[key] for key in keys]
    sols = [_dot_x3(inv, wk["rhs"]) for wk, inv in zip(wks, inverses)]
    us = [sol[:, :GDN_DV].astype(BF16) for sol in sols]
    ws = [sol[:, GDN_DV:].astype(BF16) for sol in sols]
    dot = lambda a, b: jnp.dot(a, b, preferred_element_type=F32)
    for wk, s_in in zip(wks, [dot(wk["kd_t"], u) for wk, u in zip(wks, us)]):
        wk["s_in"] = s_in
    for wk, s_mix in zip(wks, [dot(wk["kd_t"], w) for wk, w in zip(wks, ws)]):
        wk["s_mix"] = s_mix.astype(BF16)
    for wk, o_in in zip(wks, [dot(wk["a_qk"], u) for wk, u in zip(wks, us)]):
        wk["o_in"] = o_in
    for wk, aw in zip(wks, [dot(wk["a_qk"], w) for wk, w in zip(wks, ws)]):
        wk["o_mix"] = (wk["qg"] - aw).astype(BF16)

    states = [state_ref[h] for h in range(GDN_HEADS)]
    for c in range(n_chunks):
        rows = slice(c * C, (c + 1) * C)
        for h in range(GDN_HEADS):
            wk = work[h, c]
            sb = states[h].astype(BF16)
            o = wk["o_in"] + jnp.dot(wk["o_mix"], sb, preferred_element_type=F32)
            states[h] = (states[h] * wk["keep"] + wk["s_in"]) - jnp.dot(wk["s_mix"], sb, preferred_element_type=F32)
            gate = gate_ref[rows, h * GDN_DV:(h + 1) * GDN_DV]
            on = o * lax.rsqrt(jnp.mean(o * o, axis=-1, keepdims=True) + EPS) * onorm_ref[...]
            o_ref[rows, h * GDN_DV:(h + 1) * GDN_DV] = (on * (gate * jax.nn.sigmoid(gate))).astype(o_ref.dtype)
    for h in range(GDN_HEADS):
        state_ref[h] = states[h]
    xext_ref[0:GDN_HALO, :] = xext_ref[GDN_TC:GDN_TC + GDN_HALO, :]


def gated_delta_net(p32, conv_w, a_log, dt_bias, o_norm, qkv_col, gate_col, small_col):
    nt = SEQ // GDN_TC
    width = 2 * GDN_QK + GDN_V
    lane_vec = lambda a, off: jnp.zeros((1, LANE), F32).at[0, off:off + GDN_HEADS].set(a.astype(F32))
    tri = jnp.asarray(np.tril(np.ones((GDN_CHUNK, GDN_CHUNK))), BF16)
    tok = lambda blk: (lambda b, t: (b * nt + t, blk))
    return pl.pallas_call(
        _gdn_kernel,
        grid=(BATCH, nt),
        in_specs=[pl.BlockSpec((GDN_TC, width), tok(qkv_col // width)),
                  pl.BlockSpec((GDN_TC, GDN_V), tok(gate_col // GDN_V)),
                  pl.BlockSpec((GDN_TC, LANE), tok(small_col // LANE)),
                  _resident((GDN_CONV, width)), _resident((1, LANE)), _resident((1, LANE)), _resident((1, GDN_DV)),
                  _resident((GDN_CHUNK, GDN_CHUNK))],
        out_specs=pl.BlockSpec((GDN_TC, GDN_V), tok(0)),
        out_shape=jax.ShapeDtypeStruct((TOKENS, GDN_V), BF16),
        scratch_shapes=[pltpu.VMEM((GDN_HALO + GDN_TC, width), F32), pltpu.VMEM((GDN_HEADS, GDN_DK, GDN_DV), F32)],
        compiler_params=pltpu.CompilerParams(dimension_semantics=("arbitrary", "arbitrary"),
                                             vmem_limit_bytes=VMEM_LIMIT),
        name="gated_delta_net",
    )(p32, p32, p32, conv_w, lane_vec(a_log, GDN_SMALL_DECAY), lane_vec(dt_bias, GDN_SMALL_DECAY),
      o_norm.reshape(1, GDN_DV), tri)


N_SLC = SEQ // SLC_LEN
N_CMP = SEQ // CMP_STRIDE
NSA_HG = NSA_HEADS // NSA_GROUPS
NSA_TQ = 128
SEL_STEP = 512
WIN_W = WINDOW + NSA_TQ
NSA_SCALE = NSA_DH ** -0.5


def _cmp_kernel(kc_ref, vc_ref, pe_ref, w1_ref, b1_ref, w2_ref, o_ref, strip_ref):
    half = CMP_STRIDE * NSA_DH
    for kv, x_ref in enumerate((kc_ref, vc_ref)):
        for p in range(CMP_STRIDE):
            xp = x_ref[pl.ds(p, N_CMP, stride=CMP_STRIDE), :]
            for g in range(NSA_GROUPS):
                strip_ref[kv * NSA_GROUPS + g, :, p * NSA_DH:(p + 1) * NSA_DH] = xp[:, g * NSA_DH:(g + 1) * NSA_DH]
    for kv in range(2):
        for g in range(NSA_GROUPS):
            rows = strip_ref[kv * NSA_GROUPS + g]
            lo = (rows + pe_ref[kv, :, :half]).astype(BF16)
            hi = (rows + pe_ref[kv, :, half:]).astype(BF16)
            a = jnp.dot(lo, w1_ref[kv, :half, :], preferred_element_type=F32)
            b = jnp.dot(hi, w1_ref[kv, half:, :], preferred_element_type=F32)
            hid = a + pltpu.roll(b, N_CMP - 1, 0) + b1_ref[kv]
            act = (hid * jax.nn.sigmoid(hid)).astype(BF16)
            o_ref[0, kv * NSA_GROUPS + g] = jnp.dot(act, w2_ref[kv], preferred_element_type=F32).astype(BF16)


def nsa_compress(p32, cmp_col, pe, w1, b1, w2):
    n = 2 * NSA_GROUPS
    return pl.pallas_call(
        _cmp_kernel,
        grid=(BATCH,),
        in_specs=[pl.BlockSpec((SEQ, NSA_KV), lambda b: (b, cmp_col // NSA_KV)),
                  pl.BlockSpec((SEQ, NSA_KV), lambda b: (b, cmp_col // NSA_KV + 1)),
                  _resident((2, 1, CMP_LEN * NSA_DH)), _resident((2, CMP_LEN * NSA_DH, CMP_HIDDEN)),
                  _resident((2, 1, CMP_HIDDEN)), _resident((2, CMP_HIDDEN, NSA_DH))],
        out_specs=pl.BlockSpec((1, n, N_CMP, NSA_DH), lambda b: (b, 0, 0, 0)),
        out_shape=jax.ShapeDtypeStruct((BATCH, n, N_CMP, NSA_DH), BF16),
        scratch_shapes=[pltpu.VMEM((n, N_CMP, CMP_STRIDE * NSA_DH), F32)],
        compiler_params=pltpu.CompilerParams(dimension_semantics=("arbitrary",), vmem_limit_bytes=VMEM_LIMIT),
        name="nsa_compress",
    )(p32, p32, pe.reshape(2, 1, -1), w1.astype(BF16), b1.reshape(2, 1, -1), w2.astype(BF16))


V1_ROWS = NSA_DH + 16
NSA_PAIR = 2
MASK_BIG = 2.0 ** 100


def _masked_attention_t(k, q_t, v1_t, bias_t):
    s = jnp.dot(k, q_t, preferred_element_type=F32) + jnp.concatenate([bias_t] * NSA_HG, axis=1)
    p = jnp.exp((s - jnp.max(s, axis=0, keepdims=True)).astype(BF16))
    acc = jnp.dot(v1_t, p, preferred_element_type=F32)
    return acc[:NSA_DH] / acc[NSA_DH:NSA_DH + 1]


def _transposed_with_ones_row(eye, v):
    v_t = _dot_nt(eye, v).astype(v.dtype)
    row = lax.broadcasted_iota(jnp.int32, (V1_ROWS - NSA_DH, v.shape[0]), 0)
    return jnp.concatenate([v_t, jnp.where(row == 0, 1.0, 0.0).astype(v.dtype)], axis=0)


def _nsa_kernel(q_ref, ks_ref, vs_ref, kw_ref, vw_ref, cmp_ref, gate_ref, ovl_ref, eye_ref, expand_ref,
                o_ref, ksx_ref, qt_ref, ocmp_ref, gsel_ref, gwin_ref, vs1_ref, vw1_ref, vct_ref):
    i = pl.program_id(1)
    q0 = i * NSA_TQ
    rows = NSA_HG * NSA_TQ
    eye = eye_ref[...]
    eye_dh = eye[0:NSA_DH, 0:NSA_DH]

    @pl.when(i == 0)
    def _():
        for g in range(NSA_GROUPS):
            lanes = slice(g * NSA_DH, (g + 1) * NSA_DH)
            vs1_ref[g] = _transposed_with_ones_row(eye_dh, vs_ref[:, lanes])
            vw1_ref[g] = _transposed_with_ones_row(eye_dh, vw_ref[:, lanes])
            vct_ref[g] = _dot_nt(eye_dh, cmp_ref[0, NSA_GROUPS + g]).astype(BF16)
            ksx_ref[g] = jnp.concatenate(
                [ks_ref[:, lanes], expand_ref[...], jnp.zeros((SEQ, LANE - NSA_DH - N_SLC), BF16)], axis=1)
            qt_ref[g, NSA_DH + N_SLC:, :] = jnp.zeros((LANE - NSA_DH - N_SLC, NSA_HG * NSA_TQ), BF16)

    gates = jax.nn.sigmoid(gate_ref[...])
    gates_t = jnp.transpose(gates)
    tok = q0 + lax.broadcasted_iota(jnp.int32, (NSA_TQ, 1), 0)
    tok_l = q0 + lax.broadcasted_iota(jnp.int32, (1, NSA_TQ), 1)

    groups = range(NSA_GROUPS)
    scale = jnp.asarray(NSA_SCALE, BF16)
    for pair in range(NSA_HEADS // 2):
        pair_t = _dot_nt(eye, q_ref[:, pair * LANE:(pair + 1) * LANE] * scale).astype(BF16)
        for k in range(2):
            g, hk = divmod(2 * pair + k, NSA_HG)
            qt_ref[g, 0:NSA_DH, hk * NSA_TQ:(hk + 1) * NSA_TQ] = pair_t[k * NSA_DH:(k + 1) * NSA_DH]

    def gate_row(g, branch):
        lane = NSA_SMALL_GATE + branch * NSA_HEADS + g * NSA_HG
        return jnp.concatenate([gates_t[lane + hk:lane + hk + 1, :] for hk in range(NSA_HG)], axis=1)

    cmp_end = lax.broadcasted_iota(jnp.int32, (N_CMP, 1), 0) * CMP_STRIDE + (CMP_LEN - 1)
    cvalid = cmp_end <= jnp.concatenate([tok_l] * NSA_HG, axis=1)
    ss = [jnp.dot(cmp_ref[0, g], qt_ref[g, 0:NSA_DH, :], preferred_element_type=F32) for g in groups]
    sms = [jnp.where(cvalid, s, NEG) for s in ss]
    es = [jnp.exp(sm - jnp.max(sm, axis=0, keepdims=True)) for sm in sms]
    ps = [jnp.where(cvalid, e / jnp.sum(e, axis=0, keepdims=True), 0.0) for e in es]
    for g in groups:
        ocmp_ref[g] = gate_row(g, 0) * jnp.dot(vct_ref[g], ps[g].astype(BF16), preferred_element_type=F32)
        gsel_ref[g] = gate_row(g, 1)
        gwin_ref[g] = gate_row(g, 2)

    psums = [sum(p[:, hk * NSA_TQ:(hk + 1) * NSA_TQ] for hk in range(NSA_HG)) for p in ps]
    imps = [jnp.dot(ovl_ref[...], psum.astype(BF16), preferred_element_type=F32) for psum in psums]
    blk = lax.broadcasted_iota(jnp.int32, (N_SLC, NSA_TQ), 0)
    cur = tok_l // SLC_LEN
    bvalid = blk <= cur
    forced = (blk == 0) | (blk == cur) | (blk == cur - 1)
    scores_ = [jnp.where(bvalid, imp + jnp.where(forced, FORCE, 0.0), NEG) for imp in imps]
    ranks = [jnp.zeros((N_SLC, NSA_TQ), F32) for _ in groups]
    for j in range(N_SLC):
        for g in groups:
            sj = scores_[g][j:j + 1, :]
            tie = jnp.where((sj == scores_[g]) & (blk > j), 1.0, 0.0)
            ranks[g] = ranks[g] + jnp.where(sj > scores_[g], 1.0, tie)
    for g in groups:
        off = jnp.where((ranks[g] < SLC_TOPK) & bvalid, 0.0, -MASK_BIG).astype(BF16)
        qt_ref[g, NSA_DH:NSA_DH + N_SLC, :] = jnp.concatenate([off] * NSA_HG, axis=1)

    for n in range(SEQ // SEL_STEP):
        @pl.when(i // (SEL_STEP // NSA_TQ) == n)
        def _():
            w = (n + 1) * SEL_STEP
            w0 = pl.multiple_of(jnp.maximum(q0 + NSA_TQ - WIN_W, 0), NSA_TQ)
            r = pl.ds(w0, WIN_W)

            def bias(pos, lo_excl):
                ok = (pos <= tok_l) if lo_excl is None else ((pos <= tok_l) & (pos > lo_excl))
                return jnp.concatenate([jnp.where(ok, 0.0, NEG)] * NSA_PAIR, axis=1)

            sel_causal = bias((w - SEL_STEP) + lax.broadcasted_iota(jnp.int32, (SEL_STEP, 1), 0), None)
            wpos = w0 + lax.broadcasted_iota(jnp.int32, (WIN_W, 1), 0)
            if n == 0:
                win_edges = [(0, WIN_W, bias(wpos, tok_l - WINDOW))]
            else:
                win_edges = [(0, NSA_TQ, bias(wpos[:NSA_TQ], tok_l - WINDOW)),
                             (WIN_W - NSA_TQ, WIN_W, bias(wpos[WIN_W - NSA_TQ:], None))]

            def add_bias(s, edges):
                pieces, at = [], 0
                for lo, hi, b in edges:
                    pieces += [s[at:lo], s[lo:hi] + b]
                    at = hi
                return jnp.concatenate([p for p in pieces + [s[at:]] if p.shape[0]], axis=0)

            problems = []
            for g in range(NSA_GROUPS):
                lanes = slice(g * NSA_DH, (g + 1) * NSA_DH)
                for half in range(NSA_HG // NSA_PAIR):
                    cols = slice(half * NSA_PAIR * NSA_TQ, (half + 1) * NSA_PAIR * NSA_TQ)
                    problems.append((g, cols, 0, lambda g=g: ksx_ref[g, 0:w, :], lambda g=g: vs1_ref[g, :, 0:w],
                                     lambda g=g, cols=cols: qt_ref[g, :, cols], [(w - SEL_STEP, w, sel_causal)]))
                    problems.append((g, cols, 1, lambda lanes=lanes: kw_ref[r, lanes], lambda g=g: vw1_ref[g, :, r],
                                     lambda g=g, cols=cols: qt_ref[g, 0:NSA_DH, cols], win_edges))

            def scores(prob):
                _, _, _, k, _, q_t, edges = prob
                return add_bias(jnp.dot(k(), q_t(), preferred_element_type=F32), edges)

            def probs(s):
                return jnp.exp((s - jnp.max(s, axis=0, keepdims=True)).astype(BF16))

            def values(prob, p):
                acc = jnp.dot(prob[4](), p, preferred_element_type=F32)
                return acc[:NSA_DH] / acc[NSA_DH:NSA_DH + 1]

            problems = problems[1:2] + problems[0:1] + problems[2:]
            outs = {}
            s_next = scores(problems[0])
            for idx, prob in enumerate(problems):
                s_cur = s_next
                if idx + 1 < len(problems):
                    s_next = scores(problems[idx + 1])
                outs[prob[0], prob[1].start, prob[2]] = values(prob, probs(s_cur))

            for g in range(NSA_GROUPS):
                for half in range(NSA_HG // NSA_PAIR):
                    c0 = half * NSA_PAIR * NSA_TQ
                    cols = slice(c0, c0 + NSA_PAIR * NSA_TQ)
                    o_t = (ocmp_ref[g, :, cols] + gsel_ref[g, :, cols] * outs[g, c0, 0]
                           + gwin_ref[g, :, cols] * outs[g, c0, 1])
                    for k in range(NSA_PAIR):
                        h = g * NSA_HG + half * NSA_PAIR + k
                        o = jnp.transpose(o_t[:, k * NSA_TQ:(k + 1) * NSA_TQ])
                        o_ref[:, h * NSA_DH:(h + 1) * NSA_DH] = o.astype(o_ref.dtype)


def _nsa_constants():
    c0 = np.arange(N_CMP)[:, None] * CMP_STRIDE
    s0 = np.arange(N_SLC)[None, :] * SLC_LEN
    overlap = np.clip(np.minimum(c0 + CMP_LEN, s0 + SLC_LEN) - np.maximum(c0, s0), 0, None) / CMP_STRIDE
    overlap[N_CMP - 1] = 0.0
    expand_t = (np.arange(SEQ)[:, None] // SLC_LEN == np.arange(N_SLC)[None, :]).astype(np.float32)
    return jnp.asarray(overlap.T, BF16), jnp.asarray(np.eye(LANE), BF16), jnp.asarray(expand_t, BF16)


def nsa_attention(p16, cmp_kv, p32, q_col, kv_col, small_col):
    nq = SEQ // NSA_TQ
    ovl_t, eye, expand = _nsa_constants()
    kv_spec = lambda n: pl.BlockSpec((SEQ, NSA_KV), lambda b, i: (b, kv_col // NSA_KV + n))
    return pl.pallas_call(
        _nsa_kernel,
        grid=(BATCH, nq),
        in_specs=[pl.BlockSpec((NSA_TQ, NSA_Q), lambda b, i: (b * nq + i, q_col // NSA_Q)),
                  kv_spec(0), kv_spec(1), kv_spec(2), kv_spec(3),
                  pl.BlockSpec((1, 2 * NSA_GROUPS, N_CMP, NSA_DH), lambda b, i: (b, 0, 0, 0)),
                  pl.BlockSpec((NSA_TQ, LANE), lambda b, i: (b * nq + i, small_col // LANE)),
                  _resident(ovl_t.shape), _resident(eye.shape), _resident(expand.shape)],
        out_specs=pl.BlockSpec((NSA_TQ, NSA_Q), lambda b, i: (b * nq + i, 0)),
        out_shape=jax.ShapeDtypeStruct((TOKENS, NSA_Q), BF16),
        scratch_shapes=[pltpu.VMEM((NSA_GROUPS, SEQ, LANE), BF16),
                        pltpu.VMEM((NSA_GROUPS, LANE, NSA_HG * NSA_TQ), BF16),
                        pltpu.VMEM((NSA_GROUPS, NSA_DH, NSA_HG * NSA_TQ), F32)]
        + [pltpu.VMEM((NSA_GROUPS, 1, NSA_HG * NSA_TQ), F32)] * 2
        + [pltpu.VMEM((NSA_GROUPS, V1_ROWS, SEQ), BF16)] * 2
        + [pltpu.VMEM((NSA_GROUPS, NSA_DH, N_CMP), BF16)],
        compiler_params=pltpu.CompilerParams(dimension_semantics=("arbitrary", "arbitrary"),
                                             vmem_limit_bytes=VMEM_LIMIT),
        name="nsa_attention",
    )(p16, p16, p16, p16, p16, cmp_kv, p32, ovl_t, eye, expand)


EVEN_QKV_COL = 0
EVEN_GATE_COL = 2 * GDN_QK + GDN_V
EVEN_CMP_COL = EVEN_GATE_COL + GDN_V
EVEN_SMALL_COL = EVEN_CMP_COL + 2 * NSA_KV
EVEN_F32_W = EVEN_SMALL_COL + LANE
EVEN_Q_COL = 0
EVEN_KV_COL = NSA_Q
EVEN_BF16_W = NSA_Q + 4 * NSA_KV
assert NSA_SMALL_GATE + 3 * NSA_HEADS <= LANE


def even_proj_weight(w_in):
    qkv_a, beta, dec, gate_a, q_b, kv_b, gate_b = split_last(w_in, EVEN_SPLITS)
    pad = jnp.zeros((w_in.shape[0], LANE - 2 * GDN_HEADS - 3 * NSA_HEADS), w_in.dtype)
    return jnp.concatenate([qkv_a, gate_a, kv_b[:, :2 * NSA_KV], beta, dec, gate_b, pad, q_b, kv_b[:, 2 * NSA_KV:]],
                           axis=1).astype(BF16)


def even_mixer_core(p32, p16, conv_w, A_log, dt_bias, o_norm, cmp_pe, cmp_w1, cmp_b1, cmp_w2):
    oa = gated_delta_net(p32, conv_w, A_log, dt_bias, o_norm, EVEN_QKV_COL, EVEN_GATE_COL, EVEN_SMALL_COL)
    cmp_kv = nsa_compress(p32, EVEN_CMP_COL, cmp_pe, cmp_w1, cmp_b1, cmp_w2)
    ob = nsa_attention(p16, cmp_kv, p32, EVEN_Q_COL, EVEN_KV_COL, EVEN_SMALL_COL)
    return oa, ob


GLA_TC = 512
ODD_Q_COL = 0
ODD_K_COL = GLA_QK
ODD_V_COL = 2 * GLA_QK
ODD_R_COL = ODD_V_COL + GLA_V
ODD_LR_COL = ODD_R_COL + GLA_V
ODD_F32_W = ODD_LR_COL + LANE


def _gla_kernel(q_ref, k_ref, v_ref, r_ref, lr_ref, aw_ref, ab_ref, onorm_ref, tri_ref, o_ref, state_ref):
    t = pl.program_id(1)
    C = GLA_CHUNK

    @pl.when(t == 0)
    def _():
        state_ref[...] = jnp.zeros(state_ref.shape, F32)

    z = jnp.dot(lr_ref[...].astype(BF16), aw_ref[...], preferred_element_type=F32) + ab_ref[...]
    log_a = (jnp.minimum(z, 0.0) - jnp.log(1.0 + jnp.exp(-jnp.abs(z)))) * (1.0 / GLA_TAU)
    tri = tri_ref[...]
    causal = lax.broadcasted_iota(jnp.int32, (C, C), 0) >= lax.broadcasted_iota(jnp.int32, (C, C), 1)

    n_chunks = GLA_TC // C
    b_alls = [_cumsum_rows(tri, log_a[c * C:(c + 1) * C]) for c in range(n_chunks)]
    work = {}
    for c in range(n_chunks):
        rows = slice(c * C, (c + 1) * C)
        for h in range(GLA_HEADS):
            kl = slice(h * GLA_DK, (h + 1) * GLA_DK)
            b = b_alls[c][:, kl]
            b_last = b[C - 1:C]
            k = k_ref[rows, kl]
            work[c, h] = dict(
                rows=rows, vl=slice(h * GLA_DV, (h + 1) * GLA_DV),
                v=v_ref[rows, h * GLA_DV:(h + 1) * GLA_DV].astype(BF16),
                qt=(q_ref[rows, kl] * GLA_DK ** -0.5 * jnp.exp(b)).astype(BF16),
                kt=(k * jnp.exp(-b)).astype(BF16),
                kh=(k * jnp.exp(b_last - b)).astype(BF16),
                keep=jnp.exp(b_last))
    wks = [work[key] for key in sorted(work)]
    intras = [jnp.where(causal, _dot_nt(wk["qt"], wk["kt"]), 0.0).astype(BF16) for wk in wks]
    o_intras = [jnp.dot(intra, wk["v"], preferred_element_type=F32) for wk, intra in zip(wks, intras)]
    updates = [_dot_tn(wk["v"], wk["kh"]) for wk in wks]

    states = [state_ref[h] for h in range(GLA_HEADS)]
    starts = []
    for (c, h), wk, upd in zip(sorted(work), wks, updates):
        starts.append(states[h].astype(BF16))
        states[h] = states[h] * wk["keep"] + upd
    for h in range(GLA_HEADS):
        state_ref[h] = states[h]

    o_inters = [_dot_nt(wk["qt"], s0) for wk, s0 in zip(wks, starts)]
    for wk, o_intra, o_inter in zip(wks, o_intras, o_inters):
        o = o_intra + o_inter
        r = r_ref[wk["rows"], wk["vl"]]
        on = o * lax.rsqrt(jnp.mean(o * o, axis=-1, keepdims=True) + EPS) * onorm_ref[...]
        o_ref[wk["rows"], wk["vl"]] = (on * (r * jax.nn.sigmoid(r))).astype(o_ref.dtype)


def gated_linear_attention(p32, a_w2, a_b, o_norm):
    nt = SEQ // GLA_TC
    tok = lambda blk: (lambda b, t: (b * nt + t, blk))
    aw = jnp.pad(a_w2, ((0, LANE - GLA_RANK), (0, 0))).astype(BF16)
    tri = jnp.asarray(np.tril(np.ones((GLA_CHUNK, GLA_CHUNK))), BF16)
    return pl.pallas_call(
        _gla_kernel,
        grid=(BATCH, nt),
        in_specs=[pl.BlockSpec((GLA_TC, GLA_QK), tok(ODD_Q_COL // GLA_QK)),
                  pl.BlockSpec((GLA_TC, GLA_QK), tok(ODD_K_COL // GLA_QK)),
                  pl.BlockSpec((GLA_TC, GLA_V), tok(ODD_V_COL // GLA_V)),
                  pl.BlockSpec((GLA_TC, GLA_V), tok(ODD_R_COL // GLA_V)),
                  pl.BlockSpec((GLA_TC, LANE), tok(ODD_LR_COL // LANE)),
                  _resident((LANE, GLA_QK)), _resident((1, GLA_QK)), _resident((1, GLA_DV)),
                  _resident((GLA_CHUNK, GLA_CHUNK))],
        out_specs=pl.BlockSpec((GLA_TC, GLA_V), tok(0)),
        out_shape=jax.ShapeDtypeStruct((TOKENS, GLA_V), BF16),
        scratch_shapes=[pltpu.VMEM((GLA_HEADS, GLA_DV, GLA_DK), F32)],
        compiler_params=pltpu.CompilerParams(dimension_semantics=("arbitrary", "arbitrary"),
                                             vmem_limit_bytes=VMEM_LIMIT),
        name="gated_linear_attention",
    )(p32, p32, p32, p32, p32, aw, a_b.reshape(1, GLA_QK), o_norm.reshape(1, GLA_DV), tri)


def _pad_cols(w, n):
    return jnp.pad(w, ((0, 0), (0, n - w.shape[1])))


def kernel(x, norm_gains, ffn_w_in, ffn_conv_w, ffn_conv_b, ffn_w_out, hyb_w_in, hyb_w_out, gdn_conv_w, gdn_A_log,
           gdn_dt_bias, gdn_o_norm, nsa_cmp_pe, nsa_cmp_w1, nsa_cmp_b1, nsa_cmp_w2, gla_w_in, gla_a_w2, gla_a_b,
           gla_o_norm, gla_w_out):
    x2 = x.reshape(TOKENS, D_MODEL)
    ffn_w_in_bf = ffn_w_in.astype(BF16)
    ffn_w_out_bf = ffn_w_out.astype(BF16)
    for layer in range(DEPTH):
        if layer % 2 == 0:
            e = layer // 2
            p32, p16 = norm_proj(x2, norm_gains[layer, 0], even_proj_weight(hyb_w_in[e]),
                                 ((EVEN_F32_W, F32), (EVEN_BF16_W, BF16)))
            oa, ob = even_mixer_core(p32, p16, gdn_conv_w[e], gdn_A_log[e], gdn_dt_bias[e], gdn_o_norm[e],
                                     nsa_cmp_pe[e], nsa_cmp_w1[e], nsa_cmp_b1[e], nsa_cmp_w2[e])
            halves = (oa, 0, ob, 0)
            w_mix = hyb_w_out[e]
        else:
            o_ = layer // 2
            w = _pad_cols(gla_w_in[o_], ODD_F32_W).astype(BF16)
            (p32,) = norm_proj(x2, norm_gains[layer, 0], w, ((ODD_F32_W, F32),))
            o = gated_linear_attention(p32, gla_a_w2[o_], gla_a_b[o_], gla_o_norm[o_])
            halves = (o, 0, o, 1)
            w_mix = gla_w_out[o_]
        x2 = mix_out_conv_ffn(*halves, x2, norm_gains[layer, 1], w_mix.astype(BF16), norm_gains[layer, 2],
                              layer, ffn_w_in_bf, ffn_conv_w, ffn_conv_b, ffn_w_out_bf, norm_gains[layer, 3])
    return x2.reshape(BATCH, SEQ, D_MODEL)
```

```python
import functools
import math

import jax
import jax.numpy as jnp
import numpy as np
from jax import lax
from jax.experimental import pallas as pl
from jax.experimental.pallas import tpu as pltpu

D_MODEL = 1024
BATCH = 8
SEQ = 2048
DEPTH = 4
TOKENS = BATCH * SEQ

EPS = 1e-6
NEG = -1e30
FORCE = 1e4

GDN_DK = 128
GDN_DV = 128
GDN_HEADS = 4
GDN_QK = GDN_HEADS * GDN_DK
GDN_V = GDN_HEADS * GDN_DV
GDN_CONV = 4
GDN_CHUNK = 64

NSA_DH = 64
NSA_HEADS = 8
NSA_GROUPS = 2
NSA_Q = NSA_HEADS * NSA_DH
NSA_KV = NSA_GROUPS * NSA_DH
CMP_LEN = 32
CMP_STRIDE = 16
CMP_HIDDEN = 2 * NSA_DH
SLC_LEN = 64
SLC_TOPK = 8
WINDOW = 512
Q_BLOCK = 128

GLA_HEADS = 4
GLA_DK = 128
GLA_DV = 256
GLA_QK = GLA_HEADS * GLA_DK
GLA_V = GLA_HEADS * GLA_DV
GLA_RANK = 16
GLA_TAU = 16.0
GLA_CHUNK = 64

D_FF = 2816
FFN_CONV = 3

EVEN_SPLITS = (2 * GDN_QK + GDN_V, GDN_HEADS, GDN_HEADS, GDN_V, NSA_Q, 6 * NSA_KV, 3 * NSA_HEADS)
ODD_SPLITS = (GLA_QK, GLA_QK, GLA_V, GLA_V, GLA_RANK)

LANE = 128
VMEM_LIMIT = 56 * 1024 * 1024

F32 = jnp.float32
BF16 = jnp.bfloat16


def _round_up(n, m):
    return (n + m - 1) // m * m


def _resident(shape):
    nd = len(shape)
    return pl.BlockSpec(shape, lambda *_: (0,) * nd, pipeline_mode=pl.Buffered(1))


def _resident_layer(shape, layer):
    nd = len(shape)
    return pl.BlockSpec((None,) + tuple(shape), lambda *_: (layer,) + (0,) * nd, pipeline_mode=pl.Buffered(1))


def _rms(x, g):
    ms = jnp.mean(x * x, axis=-1, keepdims=True)
    return x * lax.rsqrt(ms + EPS) * g


PROJ_TM = 512
PROJ_CH = 512


def _proj_kernel(x_ref, g_ref, w_ref, *o_refs):
    h = _rms(x_ref[...], g_ref[...]).astype(BF16)
    col = 0
    for o_ref in o_refs:
        n = o_ref.shape[1]
        for c0 in range(0, n, PROJ_CH):
            cw = min(PROJ_CH, n - c0)
            y = jnp.dot(h, w_ref[:, col + c0:col + c0 + cw], preferred_element_type=F32)
            o_ref[:, c0:c0 + cw] = y.astype(o_ref.dtype)
        col += n


def norm_proj(x2, gain, w, sections):
    T, D = x2.shape
    N = w.shape[1]
    assert N == sum(n for n, _ in sections) and all(n % LANE == 0 for n, _ in sections)
    return pl.pallas_call(
        _proj_kernel,
        grid=(T // PROJ_TM,),
        in_specs=[pl.BlockSpec((PROJ_TM, D), lambda i: (i, 0)), _resident((1, D)), _resident((D, N))],
        out_specs=[pl.BlockSpec((PROJ_TM, n), lambda i: (i, 0)) for n, _ in sections],
        out_shape=[jax.ShapeDtypeStruct((T, n), dt) for n, dt in sections],
        compiler_params=pltpu.CompilerParams(dimension_semantics=("arbitrary",), vmem_limit_bytes=VMEM_LIMIT),
        name="norm_proj",
    )(x2, gain.reshape(1, D), w)


FFN_TM = 512
FFN_TAIL = 8
FFN_CH = 256
MIX_HALF = D_MODEL // 2
assert D_FF % FFN_CH == 0


def _ffn_kernel(oa_ref, ob_ref, x_ref, g_mix_ref, w_mix_ref, g_in_ref, w_in_ref, cw_ref, cb_ref, w_out_ref, g_out_ref,
                y_ref, tail_ref, act_ref):
    t = pl.program_id(1)

    @pl.when(t == 0)
    def _():
        tail_ref[...] = jnp.zeros(tail_ref.shape, F32)

    mix = (jnp.dot(oa_ref[...], w_mix_ref[0:MIX_HALF, :], preferred_element_type=F32)
           + jnp.dot(ob_ref[...], w_mix_ref[MIX_HALF:, :], preferred_element_type=F32))
    x = x_ref[...] + _rms(mix, g_mix_ref[...])
    h = _rms(x, g_in_ref[...]).astype(BF16)

    def conv_half(c0):
        j = c0 // FFN_CH
        u = jnp.dot(h, w_in_ref[:, c0:c0 + FFN_CH], preferred_element_type=F32)
        ext = jnp.concatenate([tail_ref[j], u], axis=0)
        tail_ref[j] = u[FFN_TM - FFN_TAIL:]
        w = cw_ref[:, c0:c0 + FFN_CH]
        y = pltpu.roll(ext, 2, 0) * w[0:1] + pltpu.roll(ext, 1, 0) * w[1:2] + ext * w[2:3] + cb_ref[:, c0:c0 + FFN_CH]
        return y[FFN_TAIL:]

    for c0 in range(0, D_FF, FFN_CH):
        half_gate = (0.5 * conv_half(c0)).astype(BF16)
        val = conv_half(D_FF + c0).astype(BF16)
        silu = half_gate + half_gate * jnp.tanh(half_gate)
        act_ref[:, c0:c0 + FFN_CH] = silu * val
    down = jnp.dot(act_ref[...], w_out_ref[...], preferred_element_type=F32)
    y_ref[...] = x + _rms(down, g_out_ref[...])


def mix_out_conv_ffn(oa, a_blk, ob, b_blk, x2, g_mix, w_mix, g_in, layer, w_in, conv_w, conv_b, w_out, g_out):
    T, D = x2.shape
    nt = SEQ // FFN_TM
    tok = lambda blk: (lambda b, t: (b * nt + t, blk))
    return pl.pallas_call(
        _ffn_kernel,
        grid=(BATCH, nt),
        in_specs=[pl.BlockSpec((FFN_TM, MIX_HALF), tok(a_blk)), pl.BlockSpec((FFN_TM, MIX_HALF), tok(b_blk)),
                  pl.BlockSpec((FFN_TM, D), tok(0)), _resident((1, D)), _resident((D, D)),
                  _resident((1, D)), _resident_layer((D, 2 * D_FF), layer),
                  _resident_layer((FFN_CONV, 2 * D_FF), layer), _resident_layer((1, 2 * D_FF), layer),
                  _resident_layer((D_FF, D), layer), _resident((1, D))],
        out_specs=pl.BlockSpec((FFN_TM, D), tok(0)),
        out_shape=jax.ShapeDtypeStruct((T, D), F32),
        scratch_shapes=[pltpu.VMEM((2 * D_FF // FFN_CH, FFN_TAIL, FFN_CH), F32), pltpu.VMEM((FFN_TM, D_FF), BF16)],
        compiler_params=pltpu.CompilerParams(dimension_semantics=("arbitrary", "arbitrary"),
                                             vmem_limit_bytes=VMEM_LIMIT),
        name="mix_out_conv_ffn",
    )(oa, ob, x2, g_mix.reshape(1, D), w_mix, g_in.reshape(1, D), w_in, conv_w, conv_b.reshape(DEPTH, 1, -1), w_out,
      g_out.reshape(1, D))


def split_last(x, sizes):
    idx = np.cumsum(sizes)[:-1].tolist()
    return jnp.split(x, idx, axis=-1)


def _dot_nt(a, b):
    return lax.dot_general(a, b, (((1,), (1,)), ((), ())), preferred_element_type=F32)


def _dot_tn(a, b):
    return lax.dot_general(a, b, (((0,), (0,)), ((), ())), preferred_element_type=F32)


def _split3(a):
    hi = a.astype(BF16)
    r = a - hi.astype(F32)
    mid = r.astype(BF16)
    return hi, mid, (r - mid.astype(F32)).astype(BF16)


def _dot_x3(a, b):
    a_hi, a_lo, _ = _split3(a)
    b_hi, b_lo, _ = _split3(b)
    d = lambda p, q: jnp.dot(p, q, preferred_element_type=F32)
    return d(a_hi, b_hi) + (d(a_lo, b_hi) + d(a_hi, b_lo))


def _cumsum_rows(tri, g):
    return sum(jnp.dot(tri, t, preferred_element_type=F32) for t in reversed(_split3(g)))


def _unit_lower_inverses(mats):
    c = mats[0].shape[0]
    eye = jnp.where(lax.broadcasted_iota(jnp.int32, (c, c), 0) == lax.broadcasted_iota(jnp.int32, (c, c), 1), 1.0, 0.0)
    dot = lambda a, b: jnp.dot(a.astype(BF16), b.astype(BF16), preferred_element_type=F32)
    ps = [-a for a in mats]
    ts = [eye + p for p in ps]
    k = 2
    while k < c:
        ps = [dot(p, p) for p in ps]
        ts = [t + dot(t, p) for t, p in zip(ts, ps)]
        k *= 2
    res = [(eye - t) - _dot_x3(a, t) for a, t in zip(mats, ts)]
    return [t + dot(t, r) for t, r in zip(ts, res)]


GDN_TC = 512
GDN_HALO = 8
GDN_SMALL_BETA = 0
GDN_SMALL_DECAY = GDN_HEADS
NSA_SMALL_GATE = 2 * GDN_HEADS


def _gdn_kernel(x_ref, gate_ref, small_ref, cw_ref, alog_ref, dtb_ref, onorm_ref, tri_ref, o_ref, xext_ref, state_ref):
    t = pl.program_id(1)
    C = GDN_CHUNK

    @pl.when(t == 0)
    def _():
        xext_ref[0:GDN_HALO, :] = jnp.zeros((GDN_HALO, xext_ref.shape[1]), F32)
        state_ref[...] = jnp.zeros(state_ref.shape, F32)

    xext_ref[GDN_HALO:, :] = x_ref[...]
    small = small_ref[...]
    beta_all = jax.nn.sigmoid(small)
    z = small + dtb_ref[...]
    softplus = jnp.maximum(z, 0.0) + jnp.log(1.0 + jnp.exp(-jnp.abs(z)))
    g_all = -jnp.exp(alog_ref[...]) * softplus
    tri = tri_ref[...]
    row = lax.broadcasted_iota(jnp.int32, (C, C), 0)
    col = lax.broadcasted_iota(jnp.int32, (C, C), 1)

    def conv_silu(c0, width):
        w = cw_ref[:, c0:c0 + width]
        y = xext_ref[GDN_HALO:, c0:c0 + width] * w[GDN_CONV - 1:GDN_CONV]
        for s in range(1, GDN_CONV):
            y = y + xext_ref[GDN_HALO - s:GDN_HALO - s + GDN_TC, c0:c0 + width] * w[GDN_CONV - 1 - s:GDN_CONV - s]
        return y * jax.nn.sigmoid(y)

    n_chunks = GDN_TC // C
    gsums = [_cumsum_rows(tri, g_all[c * C:(c + 1) * C]) for c in range(n_chunks)]
    gsum_ts = [jnp.transpose(gs) for gs in gsums]

    work = {}
    for h in range(GDN_HEADS):
        q_all = conv_silu(h * GDN_DK, GDN_DK)
        k_all = conv_silu(GDN_QK + h * GDN_DK, GDN_DK)
        v_all = conv_silu(2 * GDN_QK + h * GDN_DV, GDN_DV)
        q_all = q_all * (lax.rsqrt(jnp.sum(q_all * q_all, axis=-1, keepdims=True) + EPS) * GDN_DK ** -0.5)
        k_all = k_all * lax.rsqrt(jnp.sum(k_all * k_all, axis=-1, keepdims=True) + EPS)
        for c in range(n_chunks):
            rows = slice(c * C, (c + 1) * C)
            q, k, v = q_all[rows], k_all[rows], v_all[rows]
            beta = beta_all[rows, GDN_SMALL_BETA + h:GDN_SMALL_BETA + h + 1]
            gcol = gsums[c][:, GDN_SMALL_DECAY + h:GDN_SMALL_DECAY + h + 1]
            grow = gsum_ts[c][GDN_SMALL_DECAY + h:GDN_SMALL_DECAY + h + 1, :]
            glast = gcol[C - 1:C]
            decay = jnp.exp(jnp.where(row >= col, gcol - grow, -jnp.inf))
            kb = k * beta
            kbf = k.astype(BF16)
            eg = jnp.exp(gcol)
            work[h, c] = dict(
                a_kk=jnp.where(row > col, _dot_nt(kb.astype(BF16), kbf) * decay, 0.0),
                a_qk=(_dot_nt(q.astype(BF16), kbf) * decay).astype(BF16),
                rhs=jnp.concatenate([v * beta, kb * eg], axis=-1),
                qg=q * eg,
                kd_t=jnp.transpose(k * jnp.exp(glast - gcol)).astype(BF16),
                keep=jnp.exp(glast))
    keys = sorted(work)
    inverses = _unit_lower_inverses([work[key]["a_kk"] for key in keys])
    wks = [work[key] for key in keys]
    sols = [_dot_x3(inv, wk["rhs"]) for wk, inv in zip(wks, inverses)]
    us = [sol[:, :GDN_DV].astype(BF16) for sol in sols]
    ws = [sol[:, GDN_DV:].astype(BF16) for sol in sols]
    dot = lambda a, b: jnp.dot(a, b, preferred_element_type=F32)
    for wk, s_in in zip(wks, [dot(wk["kd_t"], u) for wk, u in zip(wks, us)]):
        wk["s_in"] = s_in
    for wk, s_mix in zip(wks, [dot(wk["kd_t"], w) for wk, w in zip(wks, ws)]):
        wk["s_mix"] = s_mix.astype(BF16)
    for wk, o_in in zip(wks, [dot(wk["a_qk"], u) for wk, u in zip(wks, us)]):
        wk["o_in"] = o_in
    for wk, aw in zip(wks, [dot(wk["a_qk"], w) for wk, w in zip(wks, ws)]):
        wk["o_mix"] = (wk["qg"] - aw).astype(BF16)

    states = [state_ref[h] for h in range(GDN_HEADS)]
    for c in range(n_chunks):
        rows = slice(c * C, (c + 1) * C)
        for h in range(GDN_HEADS):
            wk = work[h, c]
            sb = states[h].astype(BF16)
            o = wk["o_in"] + jnp.dot(wk["o_mix"], sb, preferred_element_type=F32)
            states[h] = (states[h] * wk["keep"] + wk["s_in"]) - jnp.dot(wk["s_mix"], sb, preferred_element_type=F32)
            gate = gate_ref[rows, h * GDN_DV:(h + 1) * GDN_DV]
            on = o * lax.rsqrt(jnp.mean(o * o, axis=-1, keepdims=True) + EPS) * onorm_ref[...]
            o_ref[rows, h * GDN_DV:(h + 1) * GDN_DV] = (on * (gate * jax.nn.sigmoid(gate))).astype(o_ref.dtype)
    for h in range(GDN_HEADS):
        state_ref[h] = states[h]
    xext_ref[0:GDN_HALO, :] = xext_ref[GDN_TC:GDN_TC + GDN_HALO, :]


def gated_delta_net(p32, conv_w, a_log, dt_bias, o_norm, qkv_col, gate_col, small_col):
    nt = SEQ // GDN_TC
    width = 2 * GDN_QK + GDN_V
    lane_vec = lambda a, off: jnp.zeros((1, LANE), F32).at[0, off:off + GDN_HEADS].set(a.astype(F32))
    tri = jnp.asarray(np.tril(np.ones((GDN_CHUNK, GDN_CHUNK))), BF16)
    tok = lambda blk: (lambda b, t: (b * nt + t, blk))
    return pl.pallas_call(
        _gdn_kernel,
        grid=(BATCH, nt),
        in_specs=[pl.BlockSpec((GDN_TC, width), tok(qkv_col // width)),
                  pl.BlockSpec((GDN_TC, GDN_V), tok(gate_col // GDN_V)),
                  pl.BlockSpec((GDN_TC, LANE), tok(small_col // LANE)),
                  _resident((GDN_CONV, width)), _resident((1, LANE)), _resident((1, LANE)), _resident((1, GDN_DV)),
                  _resident((GDN_CHUNK, GDN_CHUNK))],
        out_specs=pl.BlockSpec((GDN_TC, GDN_V), tok(0)),
        out_shape=jax.ShapeDtypeStruct((TOKENS, GDN_V), BF16),
        scratch_shapes=[pltpu.VMEM((GDN_HALO + GDN_TC, width), F32), pltpu.VMEM((GDN_HEADS, GDN_DK, GDN_DV), F32)],
        compiler_params=pltpu.CompilerParams(dimension_semantics=("arbitrary", "arbitrary"),
                                             vmem_limit_bytes=VMEM_LIMIT),
        name="gated_delta_net",
    )(p32, p32, p32, conv_w, lane_vec(a_log, GDN_SMALL_DECAY), lane_vec(dt_bias, GDN_SMALL_DECAY),
      o_norm.reshape(1, GDN_DV), tri)


N_SLC = SEQ // SLC_LEN
N_CMP = SEQ // CMP_STRIDE
NSA_HG = NSA_HEADS // NSA_GROUPS
NSA_TQ = 128
SEL_STEP = 512
WIN_W = WINDOW + NSA_TQ
NSA_SCALE = NSA_DH ** -0.5


def _cmp_kernel(kc_ref, vc_ref, pe_ref, w1_ref, b1_ref, w2_ref, o_ref, strip_ref):
    half = CMP_STRIDE * NSA_DH
    for kv, x_ref in enumerate((kc_ref, vc_ref)):
        for p in range(CMP_STRIDE):
            xp = x_ref[pl.ds(p, N_CMP, stride=CMP_STRIDE), :]
            for g in range(NSA_GROUPS):
                strip_ref[kv * NSA_GROUPS + g, :, p * NSA_DH:(p + 1) * NSA_DH] = xp[:, g * NSA_DH:(g + 1) * NSA_DH]
    for kv in range(2):
        for g in range(NSA_GROUPS):
            rows = strip_ref[kv * NSA_GROUPS + g]
            lo = (rows + pe_ref[kv, :, :half]).astype(BF16)
            hi = (rows + pe_ref[kv, :, half:]).astype(BF16)
            a = jnp.dot(lo, w1_ref[kv, :half, :], preferred_element_type=F32)
            b = jnp.dot(hi, w1_ref[kv, half:, :], preferred_element_type=F32)
            hid = a + pltpu.roll(b, N_CMP - 1, 0) + b1_ref[kv]
            act = (hid * jax.nn.sigmoid(hid)).astype(BF16)
            o_ref[0, kv * NSA_GROUPS + g] = jnp.dot(act, w2_ref[kv], preferred_element_type=F32).astype(BF16)


def nsa_compress(p32, cmp_col, pe, w1, b1, w2):
    n = 2 * NSA_GROUPS
    return pl.pallas_call(
        _cmp_kernel,
        grid=(BATCH,),
        in_specs=[pl.BlockSpec((SEQ, NSA_KV), lambda b: (b, cmp_col // NSA_KV)),
                  pl.BlockSpec((SEQ, NSA_KV), lambda b: (b, cmp_col // NSA_KV + 1)),
                  _resident((2, 1, CMP_LEN * NSA_DH)), _resident((2, CMP_LEN * NSA_DH, CMP_HIDDEN)),
                  _resident((2, 1, CMP_HIDDEN)), _resident((2, CMP_HIDDEN, NSA_DH))],
        out_specs=pl.BlockSpec((1, n, N_CMP, NSA_DH), lambda b: (b, 0, 0, 0)),
        out_shape=jax.ShapeDtypeStruct((BATCH, n, N_CMP, NSA_DH), BF16),
        scratch_shapes=[pltpu.VMEM((n, N_CMP, CMP_STRIDE * NSA_DH), F32)],
        compiler_params=pltpu.CompilerParams(dimension_semantics=("arbitrary",), vmem_limit_bytes=VMEM_LIMIT),
        name="nsa_compress",
    )(p32, p32, pe.reshape(2, 1, -1), w1.astype(BF16), b1.reshape(2, 1, -1), w2.astype(BF16))


V1_ROWS = NSA_DH + 16
NSA_PAIR = 2
NSA_AHEAD = 3
MASK_BIG = 2.0 ** 100


def _masked_attention_t(k, q_t, v1_t, bias_t):
    s = jnp.dot(k, q_t, preferred_element_type=F32) + jnp.concatenate([bias_t] * NSA_HG, axis=1)
    p = jnp.exp((s - jnp.max(s, axis=0, keepdims=True)).astype(BF16))
    acc = jnp.dot(v1_t, p, preferred_element_type=F32)
    return acc[:NSA_DH] / acc[NSA_DH:NSA_DH + 1]


def _transposed_with_ones_row(eye, v):
    v_t = _dot_nt(eye, v).astype(v.dtype)
    row = lax.broadcasted_iota(jnp.int32, (V1_ROWS - NSA_DH, v.shape[0]), 0)
    return jnp.concatenate([v_t, jnp.where(row == 0, 1.0, 0.0).astype(v.dtype)], axis=0)


def _nsa_kernel(q_ref, ks_ref, vs_ref, kw_ref, vw_ref, cmp_ref, gate_ref, ovl_ref, eye_ref, expand_ref,
                o_ref, ksx_ref, qt_ref, ocmp_ref, gsel_ref, gwin_ref, vs1_ref, vw1_ref, vct_ref):
    i = pl.program_id(1)
    q0 = i * NSA_TQ
    rows = NSA_HG * NSA_TQ
    eye = eye_ref[...]
    eye_dh = eye[0:NSA_DH, 0:NSA_DH]

    @pl.when(i == 0)
    def _():
        for g in range(NSA_GROUPS):
            lanes = slice(g * NSA_DH, (g + 1) * NSA_DH)
            vs1_ref[g] = _transposed_with_ones_row(eye_dh, vs_ref[:, lanes])
            vw1_ref[g] = _transposed_with_ones_row(eye_dh, vw_ref[:, lanes])
            vct_ref[g] = _dot_nt(eye_dh, cmp_ref[0, NSA_GROUPS + g]).astype(BF16)
            ksx_ref[g] = jnp.concatenate(
                [ks_ref[:, lanes], expand_ref[...], jnp.zeros((SEQ, LANE - NSA_DH - N_SLC), BF16)], axis=1)
            qt_ref[g, NSA_DH + N_SLC:, :] = jnp.zeros((LANE - NSA_DH - N_SLC, NSA_HG * NSA_TQ), BF16)

    gates = jax.nn.sigmoid(gate_ref[...])
    gates_t = jnp.transpose(gates)
    tok = q0 + lax.broadcasted_iota(jnp.int32, (NSA_TQ, 1), 0)
    tok_l = q0 + lax.broadcasted_iota(jnp.int32, (1, NSA_TQ), 1)

    groups = range(NSA_GROUPS)
    scale = jnp.asarray(NSA_SCALE, BF16)
    for pair in range(NSA_HEADS // 2):
        pair_t = _dot_nt(eye, q_ref[:, pair * LANE:(pair + 1) * LANE] * scale).astype(BF16)
        for k in range(2):
            g, hk = divmod(2 * pair + k, NSA_HG)
            qt_ref[g, 0:NSA_DH, hk * NSA_TQ:(hk + 1) * NSA_TQ] = pair_t[k * NSA_DH:(k + 1) * NSA_DH]

    def gate_row(g, branch):
        lane = NSA_SMALL_GATE + branch * NSA_HEADS + g * NSA_HG
        return jnp.concatenate([gates_t[lane + hk:lane + hk + 1, :] for hk in range(NSA_HG)], axis=1)

    cmp_end = lax.broadcasted_iota(jnp.int32, (N_CMP, 1), 0) * CMP_STRIDE + (CMP_LEN - 1)
    cvalid = cmp_end <= jnp.concatenate([tok_l] * NSA_HG, axis=1)
    ss = [jnp.dot(cmp_ref[0, g], qt_ref[g, 0:NSA_DH, :], preferred_element_type=F32) for g in groups]
    sms = [jnp.where(cvalid, s, NEG) for s in ss]
    es = [jnp.exp(sm - jnp.max(sm, axis=0, keepdims=True)) for sm in sms]
    ps = [jnp.where(cvalid, e / jnp.sum(e, axis=0, keepdims=True), 0.0) for e in es]
    for g in groups:
        ocmp_ref[g] = gate_row(g, 0) * jnp.dot(vct_ref[g], ps[g].astype(BF16), preferred_element_type=F32)
        gsel_ref[g] = gate_row(g, 1)
        gwin_ref[g] = gate_row(g, 2)

    psums = [sum(p[:, hk * NSA_TQ:(hk + 1) * NSA_TQ] for hk in range(NSA_HG)) for p in ps]
    imps = [jnp.dot(ovl_ref[...], psum.astype(BF16), preferred_element_type=F32) for psum in psums]
    blk = lax.broadcasted_iota(jnp.int32, (N_SLC, NSA_TQ), 0)
    cur = tok_l // SLC_LEN
    bvalid = blk <= cur
    forced = (blk == 0) | (blk == cur) | (blk == cur - 1)
    scores_ = [jnp.where(bvalid, imp + jnp.where(forced, FORCE, 0.0), NEG) for imp in imps]
    ranks = [jnp.zeros((N_SLC, NSA_TQ), F32) for _ in groups]
    for j in range(N_SLC):
        for g in groups:
            sj = scores_[g][j:j + 1, :]
            tie = jnp.where((sj == scores_[g]) & (blk > j), 1.0, 0.0)
            ranks[g] = ranks[g] + jnp.where(sj > scores_[g], 1.0, tie)
    for g in groups:
        off = jnp.where((ranks[g] < SLC_TOPK) & bvalid, 0.0, -MASK_BIG).astype(BF16)
        qt_ref[g, NSA_DH:NSA_DH + N_SLC, :] = jnp.concatenate([off] * NSA_HG, axis=1)

    for n in range(SEQ // SEL_STEP):
        @pl.when(i // (SEL_STEP // NSA_TQ) == n)
        def _():
            w = (n + 1) * SEL_STEP
            w0 = pl.multiple_of(jnp.maximum(q0 + NSA_TQ - WIN_W, 0), NSA_TQ)
            r = pl.ds(w0, WIN_W)

            def bias(pos, lo_excl):
                ok = (pos <= tok_l) if lo_excl is None else ((pos <= tok_l) & (pos > lo_excl))
                return jnp.concatenate([jnp.where(ok, 0.0, NEG)] * NSA_PAIR, axis=1)

            sel_causal = bias((w - SEL_STEP) + lax.broadcasted_iota(jnp.int32, (SEL_STEP, 1), 0), None)
            wpos = w0 + lax.broadcasted_iota(jnp.int32, (WIN_W, 1), 0)
            if n == 0:
                win_edges = [(0, WIN_W, bias(wpos, tok_l - WINDOW))]
            else:
                win_edges = [(0, NSA_TQ, bias(wpos[:NSA_TQ], tok_l - WINDOW)),
                             (WIN_W - NSA_TQ, WIN_W, bias(wpos[WIN_W - NSA_TQ:], None))]

            def add_bias(s, edges):
                pieces, at = [], 0
                for lo, hi, b in edges:
                    pieces += [s[at:lo], s[lo:hi] + b]
                    at = hi
                return jnp.concatenate([p for p in pieces + [s[at:]] if p.shape[0]], axis=0)

            problems = []
            for g in range(NSA_GROUPS):
                lanes = slice(g * NSA_DH, (g + 1) * NSA_DH)
                for half in range(NSA_HG // NSA_PAIR):
                    cols = slice(half * NSA_PAIR * NSA_TQ, (half + 1) * NSA_PAIR * NSA_TQ)
                    problems.append((g, cols, 0, lambda g=g: ksx_ref[g, 0:w, :], lambda g=g: vs1_ref[g, :, 0:w],
                                     lambda g=g, cols=cols: qt_ref[g, :, cols], [(w - SEL_STEP, w, sel_causal)]))
                    problems.append((g, cols, 1, lambda lanes=lanes: kw_ref[r, lanes], lambda g=g: vw1_ref[g, :, r],
                                     lambda g=g, cols=cols: qt_ref[g, 0:NSA_DH, cols], win_edges))

            def scores(prob):
                _, _, _, k, _, q_t, edges = prob
                return add_bias(jnp.dot(k(), q_t(), preferred_element_type=F32), edges)

            def probs(s):
                return jnp.exp((s - jnp.max(s, axis=0, keepdims=True)).astype(BF16))

            def values(prob, p):
                acc = jnp.dot(prob[4](), p, preferred_element_type=F32)
                return acc[:NSA_DH] / acc[NSA_DH:NSA_DH + 1]

            problems = problems[1:2] + problems[0:1] + problems[2:]
            outs = {}
            ahead = [scores(prob) for prob in problems[:NSA_AHEAD]]
            for idx, prob in enumerate(problems):
                s_cur = ahead.pop(0)
                if idx + NSA_AHEAD < len(problems):
                    ahead.append(scores(problems[idx + NSA_AHEAD]))
                outs[prob[0], prob[1].start, prob[2]] = values(prob, probs(s_cur))

            for g in range(NSA_GROUPS):
                for half in range(NSA_HG // NSA_PAIR):
                    c0 = half * NSA_PAIR * NSA_TQ
                    cols = slice(c0, c0 + NSA_PAIR * NSA_TQ)
                    o_t = (ocmp_ref[g, :, cols] + gsel_ref[g, :, cols] * outs[g, c0, 0]
                           + gwin_ref[g, :, cols] * outs[g, c0, 1])
                    for k in range(NSA_PAIR):
                        h = g * NSA_HG + half * NSA_PAIR + k
                        o = jnp.transpose(o_t[:, k * NSA_TQ:(k + 1) * NSA_TQ])
                        o_ref[:, h * NSA_DH:(h + 1) * NSA_DH] = o.astype(o_ref.dtype)


def _nsa_constants():
    c0 = np.arange(N_CMP)[:, None] * CMP_STRIDE
    s0 = np.arange(N_SLC)[None, :] * SLC_LEN
    overlap = np.clip(np.minimum(c0 + CMP_LEN, s0 + SLC_LEN) - np.maximum(c0, s0), 0, None) / CMP_STRIDE
    overlap[N_CMP - 1] = 0.0
    expand_t = (np.arange(SEQ)[:, None] // SLC_LEN == np.arange(N_SLC)[None, :]).astype(np.float32)
    return jnp.asarray(overlap.T, BF16), jnp.asarray(np.eye(LANE), BF16), jnp.asarray(expand_t, BF16)


def nsa_attention(p16, cmp_kv, p32, q_col, kv_col, small_col):
    nq = SEQ // NSA_TQ
    ovl_t, eye, expand = _nsa_constants()
    kv_spec = lambda n: pl.BlockSpec((SEQ, NSA_KV), lambda b, i: (b, kv_col // NSA_KV + n))
    return pl.pallas_call(
        _nsa_kernel,
        grid=(BATCH, nq),
        in_specs=[pl.BlockSpec((NSA_TQ, NSA_Q), lambda b, i: (b * nq + i, q_col // NSA_Q)),
                  kv_spec(0), kv_spec(1), kv_spec(2), kv_spec(3),
                  pl.BlockSpec((1, 2 * NSA_GROUPS, N_CMP, NSA_DH), lambda b, i: (b, 0, 0, 0)),
                  pl.BlockSpec((NSA_TQ, LANE), lambda b, i: (b * nq + i, small_col // LANE)),
                  _resident(ovl_t.shape), _resident(eye.shape), _resident(expand.shape)],
        out_specs=pl.BlockSpec((NSA_TQ, NSA_Q), lambda b, i: (b * nq + i, 0)),
        out_shape=jax.ShapeDtypeStruct((TOKENS, NSA_Q), BF16),
        scratch_shapes=[pltpu.VMEM((NSA_GROUPS, SEQ, LANE), BF16),
                        pltpu.VMEM((NSA_GROUPS, LANE, NSA_HG * NSA_TQ), BF16),
                        pltpu.VMEM((NSA_GROUPS, NSA_DH, NSA_HG * NSA_TQ), F32)]
        + [pltpu.VMEM((NSA_GROUPS, 1, NSA_HG * NSA_TQ), F32)] * 2
        + [pltpu.VMEM((NSA_GROUPS, V1_ROWS, SEQ), BF16)] * 2
        + [pltpu.VMEM((NSA_GROUPS, NSA_DH, N_CMP), BF16)],
        compiler_params=pltpu.CompilerParams(dimension_semantics=("arbitrary", "arbitrary"),
                                             vmem_limit_bytes=VMEM_LIMIT),
        name="nsa_attention",
    )(p16, p16, p16, p16, p16, cmp_kv, p32, ovl_t, eye, expand)


EVEN_QKV_COL = 0
EVEN_GATE_COL = 2 * GDN_QK + GDN_V
EVEN_CMP_COL = EVEN_GATE_COL + GDN_V
EVEN_SMALL_COL = EVEN_CMP_COL + 2 * NSA_KV
EVEN_F32_W = EVEN_SMALL_COL + LANE
EVEN_Q_COL = 0
EVEN_KV_COL = NSA_Q
EVEN_BF16_W = NSA_Q + 4 * NSA_KV
assert NSA_SMALL_GATE + 3 * NSA_HEADS <= LANE


def even_proj_weight(w_in):
    qkv_a, beta, dec, gate_a, q_b, kv_b, gate_b = split_last(w_in, EVEN_SPLITS)
    pad = jnp.zeros((w_in.shape[0], LANE - 2 * GDN_HEADS - 3 * NSA_HEADS), w_in.dtype)
    return jnp.concatenate([qkv_a, gate_a, kv_b[:, :2 * NSA_KV], beta, dec, gate_b, pad, q_b, kv_b[:, 2 * NSA_KV:]],
                           axis=1).astype(BF16)


def even_mixer_core(p32, p16, conv_w, A_log, dt_bias, o_norm, cmp_pe, cmp_w1, cmp_b1, cmp_w2):
    oa = gated_delta_net(p32, conv_w, A_log, dt_bias, o_norm, EVEN_QKV_COL, EVEN_GATE_COL, EVEN_SMALL_COL)
    cmp_kv = nsa_compress(p32, EVEN_CMP_COL, cmp_pe, cmp_w1, cmp_b1, cmp_w2)
    ob = nsa_attention(p16, cmp_kv, p32, EVEN_Q_COL, EVEN_KV_COL, EVEN_SMALL_COL)
    return oa, ob


GLA_TC = 512
ODD_Q_COL = 0
ODD_K_COL = GLA_QK
ODD_V_COL = 2 * GLA_QK
ODD_R_COL = ODD_V_COL + GLA_V
ODD_LR_COL = ODD_R_COL + GLA_V
ODD_F32_W = ODD_LR_COL + LANE


def _gla_kernel(q_ref, k_ref, v_ref, r_ref, lr_ref, aw_ref, ab_ref, onorm_ref, tri_ref, o_ref, state_ref):
    t = pl.program_id(1)
    C = GLA_CHUNK

    @pl.when(t == 0)
    def _():
        state_ref[...] = jnp.zeros(state_ref.shape, F32)

    z = jnp.dot(lr_ref[...].astype(BF16), aw_ref[...], preferred_element_type=F32) + ab_ref[...]
    log_a = (jnp.minimum(z, 0.0) - jnp.log(1.0 + jnp.exp(-jnp.abs(z)))) * (1.0 / GLA_TAU)
    tri = tri_ref[...]
    causal = lax.broadcasted_iota(jnp.int32, (C, C), 0) >= lax.broadcasted_iota(jnp.int32, (C, C), 1)

    n_chunks = GLA_TC // C
    b_alls = [_cumsum_rows(tri, log_a[c * C:(c + 1) * C]) for c in range(n_chunks)]
    work = {}
    for c in range(n_chunks):
        rows = slice(c * C, (c + 1) * C)
        for h in range(GLA_HEADS):
            kl = slice(h * GLA_DK, (h + 1) * GLA_DK)
            b = b_alls[c][:, kl]
            b_last = b[C - 1:C]
            k = k_ref[rows, kl]
            work[c, h] = dict(
                rows=rows, vl=slice(h * GLA_DV, (h + 1) * GLA_DV),
                v=v_ref[rows, h * GLA_DV:(h + 1) * GLA_DV].astype(BF16),
                qt=(q_ref[rows, kl] * GLA_DK ** -0.5 * jnp.exp(b)).astype(BF16),
                kt=(k * jnp.exp(-b)).astype(BF16),
                kh=(k * jnp.exp(b_last - b)).astype(BF16),
                keep=jnp.exp(b_last))
    wks = [work[key] for key in sorted(work)]
    intras = [jnp.where(causal, _dot_nt(wk["qt"], wk["kt"]), 0.0).astype(BF16) for wk in wks]
    o_intras = [jnp.dot(intra, wk["v"], preferred_element_type=F32) for wk, intra in zip(wks, intras)]
    updates = [_dot_tn(wk["v"], wk["kh"]) for wk in wks]

    states = [state_ref[h] for h in range(GLA_HEADS)]
    starts = []
    for (c, h), wk, upd in zip(sorted(work), wks, updates):
        starts.append(states[h].astype(BF16))
        states[h] = states[h] * wk["keep"] + upd
    for h in range(GLA_HEADS):
        state_ref[h] = states[h]

    o_inters = [_dot_nt(wk["qt"], s0) for wk, s0 in zip(wks, starts)]
    for wk, o_intra, o_inter in zip(wks, o_intras, o_inters):
        o = o_intra + o_inter
        r = r_ref[wk["rows"], wk["vl"]]
        on = o * lax.rsqrt(jnp.mean(o * o, axis=-1, keepdims=True) + EPS) * onorm_ref[...]
        o_ref[wk["rows"], wk["vl"]] = (on * (r * jax.nn.sigmoid(r))).astype(o_ref.dtype)


def gated_linear_attention(p32, a_w2, a_b, o_norm):
    nt = SEQ // GLA_TC
    tok = lambda blk: (lambda b, t: (b * nt + t, blk))
    aw = jnp.pad(a_w2, ((0, LANE - GLA_RANK), (0, 0))).astype(BF16)
    tri = jnp.asarray(np.tril(np.ones((GLA_CHUNK, GLA_CHUNK))), BF16)
    return pl.pallas_call(
        _gla_kernel,
        grid=(BATCH, nt),
        in_specs=[pl.BlockSpec((GLA_TC, GLA_QK), tok(ODD_Q_COL // GLA_QK)),
                  pl.BlockSpec((GLA_TC, GLA_QK), tok(ODD_K_COL // GLA_QK)),
                  pl.BlockSpec((GLA_TC, GLA_V), tok(ODD_V_COL // GLA_V)),
                  pl.BlockSpec((GLA_TC, GLA_V), tok(ODD_R_COL // GLA_V)),
                  pl.BlockSpec((GLA_TC, LANE), tok(ODD_LR_COL // LANE)),
                  _resident((LANE, GLA_QK)), _resident((1, GLA_QK)), _resident((1, GLA_DV)),
                  _resident((GLA_CHUNK, GLA_CHUNK))],
        out_specs=pl.BlockSpec((GLA_TC, GLA_V), tok(0)),
        out_shape=jax.ShapeDtypeStruct((TOKENS, GLA_V), BF16),
        scratch_shapes=[pltpu.VMEM((GLA_HEADS, GLA_DV, GLA_DK), F32)],
        compiler_params=pltpu.CompilerParams(dimension_semantics=("arbitrary", "arbitrary"),
                                             vmem_limit_bytes=VMEM_LIMIT),
        name="gated_linear_attention",
    )(p32, p32, p32, p32, p32, aw, a_b.reshape(1, GLA_QK), o_norm.reshape(1, GLA_DV), tri)


def _pad_cols(w, n):
    return jnp.pad(w, ((0, 0), (0, n - w.shape[1])))


def kernel(x, norm_gains, ffn_w_in, ffn_conv_w, ffn_conv_b, ffn_w_out, hyb_w_in, hyb_w_out, gdn_conv_w, gdn_A_log,
           gdn_dt_bias, gdn_o_norm, nsa_cmp_pe, nsa_cmp_w1, nsa_cmp_b1, nsa_cmp_w2, gla_w_in, gla_a_w2, gla_a_b,
           gla_o_norm, gla_w_out):
    x2 = x.reshape(TOKENS, D_MODEL)
    ffn_w_in_bf = ffn_w_in.astype(BF16)
    ffn_w_out_bf = ffn_w_out.astype(BF16)
    for layer in range(DEPTH):
        if layer % 2 == 0:
            e = layer // 2
            p32, p16 = norm_proj(x2, norm_gains[layer, 0], even_proj_weight(hyb_w_in[e]),
                                 ((EVEN_F32_W, F32), (EVEN_BF16_W, BF16)))
            oa, ob = even_mixer_core(p32, p16, gdn_conv_w[e], gdn_A_log[e], gdn_dt_bias[e], gdn_o_norm[e],
                                     nsa_cmp_pe[e], nsa_cmp_w1[e], nsa_cmp_b1[e], nsa_cmp_w2[e])
            halves = (oa, 0, ob, 0)
            w_mix = hyb_w_out[e]
        else:
            o_ = layer // 2
            w = _pad_cols(gla_w_in[o_], ODD_F32_W).astype(BF16)
            (p32,) = norm_proj(x2, norm_gains[layer, 0], w, ((ODD_F32_W, F32),))
            o = gated_linear_attention(p32, gla_a_w2[o_], gla_a_b[o_], gla_o_norm[o_])
            halves = (o, 0, o, 1)
            w_mix = gla_w_out[o_]
        x2 = mix_out_conv_ffn(*halves, x2, norm_gains[layer, 1], w_mix.astype(BF16), norm_gains[layer, 2],
                              layer, ffn_w_in_bf, ffn_conv_w, ffn_conv_b, ffn_w_out_bf, norm_gains[layer, 3])
    return x2.reshape(BATCH, SEQ, D_MODEL)
```

```python
import functools
import math

import jax
import jax.numpy as jnp
import numpy as np
from jax import lax
from jax.experimental import pallas as pl
from jax.experimental.pallas import tpu as pltpu

D_MODEL = 1024
BATCH = 8
SEQ = 2048
DEPTH = 4
TOKENS = BATCH * SEQ

EPS = 1e-6
NEG = -1e30
FORCE = 1e4

GDN_DK = 128
GDN_DV = 128
GDN_HEADS = 4
GDN_QK = GDN_HEADS * GDN_DK
GDN_V = GDN_HEADS * GDN_DV
GDN_CONV = 4
GDN_CHUNK = 64

NSA_DH = 64
NSA_HEADS = 8
NSA_GROUPS = 2
NSA_Q = NSA_HEADS * NSA_DH
NSA_KV = NSA_GROUPS * NSA_DH
CMP_LEN = 32
CMP_STRIDE = 16
CMP_HIDDEN = 2 * NSA_DH
SLC_LEN = 64
SLC_TOPK = 8
WINDOW = 512
Q_BLOCK = 128

GLA_HEADS = 4
GLA_DK = 128
GLA_DV = 256
GLA_QK = GLA_HEADS * GLA_DK
GLA_V = GLA_HEADS * GLA_DV
GLA_RANK = 16
GLA_TAU = 16.0
GLA_CHUNK = 64

D_FF = 2816
FFN_CONV = 3

EVEN_SPLITS = (2 * GDN_QK + GDN_V, GDN_HEADS, GDN_HEADS, GDN_V, NSA_Q, 6 * NSA_KV, 3 * NSA_HEADS)
ODD_SPLITS = (GLA_QK, GLA_QK, GLA_V, GLA_V, GLA_RANK)

LANE = 128
VMEM_LIMIT = 56 * 1024 * 1024

F32 = jnp.float32
BF16 = jnp.bfloat16


def _round_up(n, m):
    return (n + m - 1) // m * m


def _resident(shape):
    nd = len(shape)
    return pl.BlockSpec(shape, lambda *_: (0,) * nd, pipeline_mode=pl.Buffered(1))


def _resident_layer(shape, layer):
    nd = len(shape)
    return pl.BlockSpec((None,) + tuple(shape), lambda *_: (layer,) + (0,) * nd, pipeline_mode=pl.Buffered(1))


def _rms(x, g):
    ms = jnp.mean(x * x, axis=-1, keepdims=True)
    return x * lax.rsqrt(ms + EPS) * g


PROJ_TM = 512
PROJ_CH = 512


def _proj_kernel(x_ref, g_ref, w_ref, *o_refs):
    h = _rms(x_ref[...], g_ref[...]).astype(BF16)
    col = 0
    for o_ref in o_refs:
        n = o_ref.shape[1]
        for c0 in range(0, n, PROJ_CH):
            cw = min(PROJ_CH, n - c0)
            y = jnp.dot(h, w_ref[:, col + c0:col + c0 + cw], preferred_element_type=F32)
            o_ref[:, c0:c0 + cw] = y.astype(o_ref.dtype)
        col += n


def norm_proj(x2, gain, w, sections):
    T, D = x2.shape
    N = w.shape[1]
    assert N == sum(n for n, _ in sections) and all(n % LANE == 0 for n, _ in sections)
    return pl.pallas_call(
        _proj_kernel,
        grid=(T // PROJ_TM,),
        in_specs=[pl.BlockSpec((PROJ_TM, D), lambda i: (i, 0)), _resident((1, D)), _resident((D, N))],
        out_specs=[pl.BlockSpec((PROJ_TM, n), lambda i: (i, 0)) for n, _ in sections],
        out_shape=[jax.ShapeDtypeStruct((T, n), dt) for n, dt in sections],
        compiler_params=pltpu.CompilerParams(dimension_semantics=("arbitrary",), vmem_limit_bytes=VMEM_LIMIT),
        name="norm_proj",
    )(x2, gain.reshape(1, D), w)


FFN_TM = 512
FFN_TAIL = 8
FFN_CH = 256
MIX_HALF = D_MODEL // 2
assert D_FF % FFN_CH == 0


def _ffn_kernel(oa_ref, ob_ref, x_ref, g_mix_ref, w_mix_ref, g_in_ref, w_in_ref, cw_ref, cb_ref, w_out_ref, g_out_ref,
                y_ref, tail_ref, act_ref):
    t = pl.program_id(1)

    @pl.when(t == 0)
    def _():
        tail_ref[...] = jnp.zeros(tail_ref.shape, F32)

    mix = (jnp.dot(oa_ref[...], w_mix_ref[0:MIX_HALF, :], preferred_element_type=F32)
           + jnp.dot(ob_ref[...], w_mix_ref[MIX_HALF:, :], preferred_element_type=F32))
    x = x_ref[...] + _rms(mix, g_mix_ref[...])
    h = _rms(x, g_in_ref[...]).astype(BF16)

    def conv_half(c0):
        j = c0 // FFN_CH
        u = jnp.dot(h, w_in_ref[:, c0:c0 + FFN_CH], preferred_element_type=F32)
        ext = jnp.concatenate([tail_ref[j], u], axis=0)
        tail_ref[j] = u[FFN_TM - FFN_TAIL:]
        w = cw_ref[:, c0:c0 + FFN_CH]
        y = pltpu.roll(ext, 2, 0) * w[0:1] + pltpu.roll(ext, 1, 0) * w[1:2] + ext * w[2:3] + cb_ref[:, c0:c0 + FFN_CH]
        return y[FFN_TAIL:]

    for c0 in range(0, D_FF, FFN_CH):
        half_gate = (0.5 * conv_half(c0)).astype(BF16)
        val = conv_half(D_FF + c0).astype(BF16)
        silu = half_gate + half_gate * jnp.tanh(half_gate)
        act_ref[:, c0:c0 + FFN_CH] = silu * val
    down = jnp.dot(act_ref[...], w_out_ref[...], preferred_element_type=F32)
    y_ref[...] = x + _rms(down, g_out_ref[...])


def mix_out_conv_ffn(oa, a_blk, ob, b_blk, x2, g_mix, w_mix, g_in, layer, w_in, conv_w, conv_b, w_out, g_out):
    T, D = x2.shape
    nt = SEQ // FFN_TM
    tok = lambda blk: (lambda b, t: (b * nt + t, blk))
    return pl.pallas_call(
        _ffn_kernel,
        grid=(BATCH, nt),
        in_specs=[pl.BlockSpec((FFN_TM, MIX_HALF), tok(a_blk)), pl.BlockSpec((FFN_TM, MIX_HALF), tok(b_blk)),
                  pl.BlockSpec((FFN_TM, D), tok(0)), _resident((1, D)), _resident((D, D)),
                  _resident((1, D)), _resident_layer((D, 2 * D_FF), layer),
                  _resident_layer((FFN_CONV, 2 * D_FF), layer), _resident_layer((1, 2 * D_FF), layer),
                  _resident_layer((D_FF, D), layer), _resident((1, D))],
        out_specs=pl.BlockSpec((FFN_TM, D), tok(0)),
        out_shape=jax.ShapeDtypeStruct((T, D), F32),
        scratch_shapes=[pltpu.VMEM((2 * D_FF // FFN_CH, FFN_TAIL, FFN_CH), F32), pltpu.VMEM((FFN_TM, D_FF), BF16)],
        compiler_params=pltpu.CompilerParams(dimension_semantics=("arbitrary", "arbitrary"),
                                             vmem_limit_bytes=VMEM_LIMIT),
        name="mix_out_conv_ffn",
    )(oa, ob, x2, g_mix.reshape(1, D), w_mix, g_in.reshape(1, D), w_in, conv_w, conv_b.reshape(DEPTH, 1, -1), w_out,
      g_out.reshape(1, D))


def split_last(x, sizes):
    idx = np.cumsum(sizes)[:-1].tolist()
    return jnp.split(x, idx, axis=-1)


def _dot_nt(a, b):
    return lax.dot_general(a, b, (((1,), (1,)), ((), ())), preferred_element_type=F32)


def _dot_tn(a, b):
    return lax.dot_general(a, b, (((0,), (0,)), ((), ())), preferred_element_type=F32)


def _split3(a):
    hi = a.astype(BF16)
    r = a - hi.astype(F32)
    mid = r.astype(BF16)
    return hi, mid, (r - mid.astype(F32)).astype(BF16)


def _dot_x3(a, b):
    a_hi, a_lo, _ = _split3(a)
    b_hi, b_lo, _ = _split3(b)
    d = lambda p, q: jnp.dot(p, q, preferred_element_type=F32)
    return d(a_hi, b_hi) + (d(a_lo, b_hi) + d(a_hi, b_lo))


def _cumsum_rows(tri, g):
    return sum(jnp.dot(tri, t, preferred_element_type=F32) for t in reversed(_split3(g)))


def _unit_lower_inverses(mats):
    c = mats[0].shape[0]
    eye = jnp.where(lax.broadcasted_iota(jnp.int32, (c, c), 0) == lax.broadcasted_iota(jnp.int32, (c, c), 1), 1.0, 0.0)
    dot = lambda a, b: jnp.dot(a.astype(BF16), b.astype(BF16), preferred_element_type=F32)
    ps = [-a for a in mats]
    ts = [eye + p for p in ps]
    k = 2
    while k < c:
        ps = [dot(p, p) for p in ps]
        ts = [t + dot(t, p) for t, p in zip(ts, ps)]
        k *= 2
    res = [(eye - t) - _dot_x3(a, t) for a, t in zip(mats, ts)]
    return [t + dot(t, r) for t, r in zip(ts, res)]


GDN_TC = 512
GDN_HALO = 8
GDN_SMALL_BETA = 0
GDN_SMALL_DECAY = GDN_HEADS
NSA_SMALL_GATE = 2 * GDN_HEADS


def _gdn_kernel(x_ref, gate_ref, small_ref, cw_ref, alog_ref, dtb_ref, onorm_ref, tri_ref, o_ref, xext_ref, state_ref):
    t = pl.program_id(1)
    C = GDN_CHUNK

    @pl.when(t == 0)
    def _():
        xext_ref[0:GDN_HALO, :] = jnp.zeros((GDN_HALO, xext_ref.shape[1]), F32)
        state_ref[...] = jnp.zeros(state_ref.shape, F32)

    xext_ref[GDN_HALO:, :] = x_ref[...]
    small = small_ref[...]
    beta_all = jax.nn.sigmoid(small)
    z = small + dtb_ref[...]
    softplus = jnp.maximum(z, 0.0) + jnp.log(1.0 + jnp.exp(-jnp.abs(z)))
    g_all = -jnp.exp(alog_ref[...]) * softplus
    tri = tri_ref[...]
    row = lax.broadcasted_iota(jnp.int32, (C, C), 0)
    col = lax.broadcasted_iota(jnp.int32, (C, C), 1)

    def conv_silu(c0, width):
        w = cw_ref[:, c0:c0 + width]
        y = xext_ref[GDN_HALO:, c0:c0 + width] * w[GDN_CONV - 1:GDN_CONV]
        for s in range(1, GDN_CONV):
            y = y + xext_ref[GDN_HALO - s:GDN_HALO - s + GDN_TC, c0:c0 + width] * w[GDN_CONV - 1 - s:GDN_CONV - s]
        return y * jax.nn.sigmoid(y)

    n_chunks = GDN_TC // C
    gsums = [_cumsum_rows(tri, g_all[c * C:(c + 1) * C]) for c in range(n_chunks)]
    gsum_ts = [jnp.transpose(gs) for gs in gsums]

    work = {}
    for h in range(GDN_HEADS):
        q_all = conv_silu(h * GDN_DK, GDN_DK)
        k_all = conv_silu(GDN_QK + h * GDN_DK, GDN_DK)
        v_all = conv_silu(2 * GDN_QK + h * GDN_DV, GDN_DV)
        q_all = q_all * (lax.rsqrt(jnp.sum(q_all * q_all, axis=-1, keepdims=True) + EPS) * GDN_DK ** -0.5)
        k_all = k_all * lax.rsqrt(jnp.sum(k_all * k_all, axis=-1, keepdims=True) + EPS)
        for c in range(n_chunks):
            rows = slice(c * C, (c + 1) * C)
            q, k, v = q_all[rows], k_all[rows], v_all[rows]
            beta = beta_all[rows, GDN_SMALL_BETA + h:GDN_SMALL_BETA + h + 1]
            gcol = gsums[c][:, GDN_SMALL_DECAY + h:GDN_SMALL_DECAY + h + 1]
            grow = gsum_ts[c][GDN_SMALL_DECAY + h:GDN_SMALL_DECAY + h + 1, :]
            glast = gcol[C - 1:C]
            decay = jnp.exp(jnp.where(row >= col, gcol - grow, -jnp.inf))
            kb = k * beta
            kbf = k.astype(BF16)
            eg = jnp.exp(gcol)
            work[h, c] = dict(
                a_kk=jnp.where(row > col, _dot_nt(kb.astype(BF16), kbf) * decay, 0.0),
                a_qk=(_dot_nt(q.astype(BF16), kbf) * decay).astype(BF16),
                rhs=jnp.concatenate([v * beta, kb * eg], axis=-1),
                qg=q * eg,
                kd_t=jnp.transpose(k * jnp.exp(glast - gcol)).astype(BF16),
                keep=jnp.exp(glast))
    keys = sorted(work)
    inverses = _unit_lower_inverses([work[key]["a_kk"] for key in keys])
    wks = [work[key] for key in keys]
    sols = [_dot_x3(inv, wk["rhs"]) for wk, inv in zip(wks, inverses)]
    us = [sol[:, :GDN_DV].astype(BF16) for sol in sols]
    ws = [sol[:, GDN_DV:].astype(BF16) for sol in sols]
    dot = lambda a, b: jnp.dot(a, b, preferred_element_type=F32)
    for wk, s_in in zip(wks, [dot(wk["kd_t"], u) for wk, u in zip(wks, us)]):
        wk["s_in"] = s_in
    for wk, s_mix in zip(wks, [dot(wk["kd_t"], w) for wk, w in zip(wks, ws)]):
        wk["s_mix"] = s_mix.astype(BF16)
    for wk, o_in in zip(wks, [dot(wk["a_qk"], u) for wk, u in zip(wks, us)]):
        wk["o_in"] = o_in
    for wk, aw in zip(wks, [dot(wk["a_qk"], w) for wk, w in zip(wks, ws)]):
        wk["o_mix"] = (wk["qg"] - aw).astype(BF16)

    states = [state_ref[h] for h in range(GDN_HEADS)]
    for c in range(n_chunks):
        rows = slice(c * C, (c + 1) * C)
        for h in range(GDN_HEADS):
            wk = work[h, c]
            sb = states[h].astype(BF16)
            o = wk["o_in"] + jnp.dot(wk["o_mix"], sb, preferred_element_type=F32)
            states[h] = (states[h] * wk["keep"] + wk["s_in"]) - jnp.dot(wk["s_mix"], sb, preferred_element_type=F32)
            gate = gate_ref[rows, h * GDN_DV:(h + 1) * GDN_DV]
            on = o * lax.rsqrt(jnp.mean(o * o, axis=-1, keepdims=True) + EPS) * onorm_ref[...]
            o_ref[rows, h * GDN_DV:(h + 1) * GDN_DV] = (on * (gate * jax.nn.sigmoid(gate))).astype(o_ref.dtype)
    for h in range(GDN_HEADS):
        state_ref[h] = states[h]
    xext_ref[0:GDN_HALO, :] = xext_ref[GDN_TC:GDN_TC + GDN_HALO, :]


def gated_delta_net(p32, conv_w, a_log, dt_bias, o_norm, qkv_col, gate_col, small_col):
    nt = SEQ // GDN_TC
    width = 2 * GDN_QK + GDN_V
    lane_vec = lambda a, off: jnp.zeros((1, LANE), F32).at[0, off:off + GDN_HEADS].set(a.astype(F32))
    tri = jnp.asarray(np.tril(np.ones((GDN_CHUNK, GDN_CHUNK))), BF16)
    tok = lambda blk: (lambda b, t: (b * nt + t, blk))
    return pl.pallas_call(
        _gdn_kernel,
        grid=(BATCH, nt),
        in_specs=[pl.BlockSpec((GDN_TC, width), tok(qkv_col // width)),
                  pl.BlockSpec((GDN_TC, GDN_V), tok(gate_col // GDN_V)),
                  pl.BlockSpec((GDN_TC, LANE), tok(small_col // LANE)),
                  _resident((GDN_CONV, width)), _resident((1, LANE)), _resident((1, LANE)), _resident((1, GDN_DV)),
                  _resident((GDN_CHUNK, GDN_CHUNK))],
        out_specs=pl.BlockSpec((GDN_TC, GDN_V), tok(0)),
        out_shape=jax.ShapeDtypeStruct((TOKENS, GDN_V), BF16),
        scratch_shapes=[pltpu.VMEM((GDN_HALO + GDN_TC, width), F32), pltpu.VMEM((GDN_HEADS, GDN_DK, GDN_DV), F32)],
        compiler_params=pltpu.CompilerParams(dimension_semantics=("arbitrary", "arbitrary"),
                                             vmem_limit_bytes=VMEM_LIMIT),
        name="gated_delta_net",
    )(p32, p32, p32, conv_w, lane_vec(a_log, GDN_SMALL_DECAY), lane_vec(dt_bias, GDN_SMALL_DECAY),
      o_norm.reshape(1, GDN_DV), tri)


N_SLC = SEQ // SLC_LEN
N_CMP = SEQ // CMP_STRIDE
NSA_HG = NSA_HEADS // NSA_GROUPS
NSA_TQ = 128
SEL_STEP = 256
WIN_W = WINDOW + NSA_TQ
NSA_SCALE = NSA_DH ** -0.5


def _cmp_kernel(kc_ref, vc_ref, pe_ref, w1_ref, b1_ref, w2_ref, o_ref, strip_ref):
    half = CMP_STRIDE * NSA_DH
    for kv, x_ref in enumerate((kc_ref, vc_ref)):
        for p in range(CMP_STRIDE):
            xp = x_ref[pl.ds(p, N_CMP, stride=CMP_STRIDE), :]
            for g in range(NSA_GROUPS):
                strip_ref[kv * NSA_GROUPS + g, :, p * NSA_DH:(p + 1) * NSA_DH] = xp[:, g * NSA_DH:(g + 1) * NSA_DH]
    for kv in range(2):
        for g in range(NSA_GROUPS):
            rows = strip_ref[kv * NSA_GROUPS + g]
            lo = (rows + pe_ref[kv, :, :half]).astype(BF16)
            hi = (rows + pe_ref[kv, :, half:]).astype(BF16)
            a = jnp.dot(lo, w1_ref[kv, :half, :], preferred_element_type=F32)
            b = jnp.dot(hi, w1_ref[kv, half:, :], preferred_element_type=F32)
            hid = a + pltpu.roll(b, N_CMP - 1, 0) + b1_ref[kv]
            act = (hid * jax.nn.sigmoid(hid)).astype(BF16)
            o_ref[0, kv * NSA_GROUPS + g] = jnp.dot(act, w2_ref[kv], preferred_element_type=F32).astype(BF16)


def nsa_compress(p32, cmp_col, pe, w1, b1, w2):
    n = 2 * NSA_GROUPS
    return pl.pallas_call(
        _cmp_kernel,
        grid=(BATCH,),
        in_specs=[pl.BlockSpec((SEQ, NSA_KV), lambda b: (b, cmp_col // NSA_KV)),
                  pl.BlockSpec((SEQ, NSA_KV), lambda b: (b, cmp_col // NSA_KV + 1)),
                  _resident((2, 1, CMP_LEN * NSA_DH)), _resident((2, CMP_LEN * NSA_DH, CMP_HIDDEN)),
                  _resident((2, 1, CMP_HIDDEN)), _resident((2, CMP_HIDDEN, NSA_DH))],
        out_specs=pl.BlockSpec((1, n, N_CMP, NSA_DH), lambda b: (b, 0, 0, 0)),
        out_shape=jax.ShapeDtypeStruct((BATCH, n, N_CMP, NSA_DH), BF16),
        scratch_shapes=[pltpu.VMEM((n, N_CMP, CMP_STRIDE * NSA_DH), F32)],
        compiler_params=pltpu.CompilerParams(dimension_semantics=("arbitrary",), vmem_limit_bytes=VMEM_LIMIT),
        name="nsa_compress",
    )(p32, p32, pe.reshape(2, 1, -1), w1.astype(BF16), b1.reshape(2, 1, -1), w2.astype(BF16))


V1_ROWS = NSA_DH + 16
NSA_PAIR = 2
NSA_AHEAD = 3
MASK_BIG = 2.0 ** 100


def _masked_attention_t(k, q_t, v1_t, bias_t):
    s = jnp.dot(k, q_t, preferred_element_type=F32) + jnp.concatenate([bias_t] * NSA_HG, axis=1)
    p = jnp.exp((s - jnp.max(s, axis=0, keepdims=True)).astype(BF16))
    acc = jnp.dot(v1_t, p, preferred_element_type=F32)
    return acc[:NSA_DH] / acc[NSA_DH:NSA_DH + 1]


def _transposed_with_ones_row(eye, v):
    v_t = _dot_nt(eye, v).astype(v.dtype)
    row = lax.broadcasted_iota(jnp.int32, (V1_ROWS - NSA_DH, v.shape[0]), 0)
    return jnp.concatenate([v_t, jnp.where(row == 0, 1.0, 0.0).astype(v.dtype)], axis=0)


def _nsa_kernel(q_ref, ks_ref, vs_ref, kw_ref, vw_ref, cmp_ref, gate_ref, ovl_ref, eye_ref, expand_ref,
                o_ref, ksx_ref, qt_ref, ocmp_ref, gsel_ref, gwin_ref, vs1_ref, vw1_ref, vct_ref):
    i = pl.program_id(1)
    q0 = i * NSA_TQ
    rows = NSA_HG * NSA_TQ
    eye = eye_ref[...]
    eye_dh = eye[0:NSA_DH, 0:NSA_DH]

    @pl.when(i == 0)
    def _():
        for g in range(NSA_GROUPS):
            lanes = slice(g * NSA_DH, (g + 1) * NSA_DH)
            vs1_ref[g] = _transposed_with_ones_row(eye_dh, vs_ref[:, lanes])
            vw1_ref[g] = _transposed_with_ones_row(eye_dh, vw_ref[:, lanes])
            vct_ref[g] = _dot_nt(eye_dh, cmp_ref[0, NSA_GROUPS + g]).astype(BF16)
            ksx_ref[g] = jnp.concatenate(
                [ks_ref[:, lanes], expand_ref[...], jnp.zeros((SEQ, LANE - NSA_DH - N_SLC), BF16)], axis=1)
            qt_ref[g, NSA_DH + N_SLC:, :] = jnp.zeros((LANE - NSA_DH - N_SLC, NSA_HG * NSA_TQ), BF16)

    gates = jax.nn.sigmoid(gate_ref[...])
    gates_t = jnp.transpose(gates)
    tok = q0 + lax.broadcasted_iota(jnp.int32, (NSA_TQ, 1), 0)
    tok_l = q0 + lax.broadcasted_iota(jnp.int32, (1, NSA_TQ), 1)

    groups = range(NSA_GROUPS)
    scale = jnp.asarray(NSA_SCALE, BF16)
    for pair in range(NSA_HEADS // 2):
        pair_t = _dot_nt(eye, q_ref[:, pair * LANE:(pair + 1) * LANE] * scale).astype(BF16)
        for k in range(2):
            g, hk = divmod(2 * pair + k, NSA_HG)
            qt_ref[g, 0:NSA_DH, hk * NSA_TQ:(hk + 1) * NSA_TQ] = pair_t[k * NSA_DH:(k + 1) * NSA_DH]

    def gate_row(g, branch):
        lane = NSA_SMALL_GATE + branch * NSA_HEADS + g * NSA_HG
        return jnp.concatenate([gates_t[lane + hk:lane + hk + 1, :] for hk in range(NSA_HG)], axis=1)

    cmp_end = lax.broadcasted_iota(jnp.int32, (N_CMP, 1), 0) * CMP_STRIDE + (CMP_LEN - 1)
    cvalid = cmp_end <= jnp.concatenate([tok_l] * NSA_HG, axis=1)
    ss = [jnp.dot(cmp_ref[0, g], qt_ref[g, 0:NSA_DH, :], preferred_element_type=F32) for g in groups]
    sms = [jnp.where(cvalid, s, NEG) for s in ss]
    es = [jnp.exp(sm - jnp.max(sm, axis=0, keepdims=True)) for sm in sms]
    ps = [jnp.where(cvalid, e / jnp.sum(e, axis=0, keepdims=True), 0.0) for e in es]
    for g in groups:
        ocmp_ref[g] = gate_row(g, 0) * jnp.dot(vct_ref[g], ps[g].astype(BF16), preferred_element_type=F32)
        gsel_ref[g] = gate_row(g, 1)
        gwin_ref[g] = gate_row(g, 2)

    psums = [sum(p[:, hk * NSA_TQ:(hk + 1) * NSA_TQ] for hk in range(NSA_HG)) for p in ps]
    imps = [jnp.dot(ovl_ref[...], psum.astype(BF16), preferred_element_type=F32) for psum in psums]
    blk = lax.broadcasted_iota(jnp.int32, (N_SLC, NSA_TQ), 0)
    cur = tok_l // SLC_LEN
    bvalid = blk <= cur
    forced = (blk == 0) | (blk == cur) | (blk == cur - 1)
    scores_ = [jnp.where(bvalid, imp + jnp.where(forced, FORCE, 0.0), NEG) for imp in imps]
    ranks = [jnp.zeros((N_SLC, NSA_TQ), F32) for _ in groups]
    for j in range(N_SLC):
        for g in groups:
            sj = scores_[g][j:j + 1, :]
            tie = jnp.where((sj == scores_[g]) & (blk > j), 1.0, 0.0)
            ranks[g] = ranks[g] + jnp.where(sj > scores_[g], 1.0, tie)
    for g in groups:
        off = jnp.where((ranks[g] < SLC_TOPK) & bvalid, 0.0, -MASK_BIG).astype(BF16)
        qt_ref[g, NSA_DH:NSA_DH + N_SLC, :] = jnp.concatenate([off] * NSA_HG, axis=1)

    for n in range(SEQ // SEL_STEP):
        @pl.when(i // (SEL_STEP // NSA_TQ) == n)
        def _():
            w = (n + 1) * SEL_STEP
            w0 = pl.multiple_of(jnp.maximum(q0 + NSA_TQ - WIN_W, 0), NSA_TQ)
            r = pl.ds(w0, WIN_W)

            def bias(pos, lo_excl):
                ok = (pos <= tok_l) if lo_excl is None else ((pos <= tok_l) & (pos > lo_excl))
                return jnp.concatenate([jnp.where(ok, 0.0, NEG)] * NSA_PAIR, axis=1)

            sel_causal = bias((w - SEL_STEP) + lax.broadcasted_iota(jnp.int32, (SEL_STEP, 1), 0), None)
            wpos = w0 + lax.broadcasted_iota(jnp.int32, (WIN_W, 1), 0)
            if n < WINDOW // SEL_STEP:
                win_edges = [(0, WIN_W, bias(wpos, tok_l - WINDOW))]
            else:
                win_edges = [(0, NSA_TQ, bias(wpos[:NSA_TQ], tok_l - WINDOW)),
                             (WIN_W - NSA_TQ, WIN_W, bias(wpos[WIN_W - NSA_TQ:], None))]

            def add_bias(s, edges):
                pieces, at = [], 0
                for lo, hi, b in edges:
                    pieces += [s[at:lo], s[lo:hi] + b]
                    at = hi
                return jnp.concatenate([p for p in pieces + [s[at:]] if p.shape[0]], axis=0)

            problems = []
            for g in range(NSA_GROUPS):
                lanes = slice(g * NSA_DH, (g + 1) * NSA_DH)
                for half in range(NSA_HG // NSA_PAIR):
                    cols = slice(half * NSA_PAIR * NSA_TQ, (half + 1) * NSA_PAIR * NSA_TQ)
                    problems.append((g, cols, 0, lambda g=g: ksx_ref[g, 0:w, :], lambda g=g: vs1_ref[g, :, 0:w],
                                     lambda g=g, cols=cols: qt_ref[g, :, cols], [(w - SEL_STEP, w, sel_causal)]))
                    problems.append((g, cols, 1, lambda lanes=lanes: kw_ref[r, lanes], lambda g=g: vw1_ref[g, :, r],
                                     lambda g=g, cols=cols: qt_ref[g, 0:NSA_DH, cols], win_edges))

            def scores(prob):
                _, _, _, k, _, q_t, edges = prob
                return add_bias(jnp.dot(k(), q_t(), preferred_element_type=F32), edges)

            def probs(s):
                return jnp.exp((s - jnp.max(s, axis=0, keepdims=True)).astype(BF16))

            def values(prob, p):
                acc = jnp.dot(prob[4](), p, preferred_element_type=F32)
                return acc[:NSA_DH] / acc[NSA_DH:NSA_DH + 1]

            problems = problems[1:2] + problems[0:1] + problems[2:]
            outs = {}
            ahead = [scores(prob) for prob in problems[:NSA_AHEAD]]
            for idx, prob in enumerate(problems):
                s_cur = ahead.pop(0)
                if idx + NSA_AHEAD < len(problems):
                    ahead.append(scores(problems[idx + NSA_AHEAD]))
                outs[prob[0], prob[1].start, prob[2]] = values(prob, probs(s_cur))

            for g in range(NSA_GROUPS):
                for half in range(NSA_HG // NSA_PAIR):
                    c0 = half * NSA_PAIR * NSA_TQ
                    cols = slice(c0, c0 + NSA_PAIR * NSA_TQ)
                    o_t = (ocmp_ref[g, :, cols] + gsel_ref[g, :, cols] * outs[g, c0, 0]
                           + gwin_ref[g, :, cols] * outs[g, c0, 1])
                    for k in range(NSA_PAIR):
                        h = g * NSA_HG + half * NSA_PAIR + k
                        o = jnp.transpose(o_t[:, k * NSA_TQ:(k + 1) * NSA_TQ])
                        o_ref[:, h * NSA_DH:(h + 1) * NSA_DH] = o.astype(o_ref.dtype)


def _nsa_constants():
    c0 = np.arange(N_CMP)[:, None] * CMP_STRIDE
    s0 = np.arange(N_SLC)[None, :] * SLC_LEN
    overlap = np.clip(np.minimum(c0 + CMP_LEN, s0 + SLC_LEN) - np.maximum(c0, s0), 0, None) / CMP_STRIDE
    overlap[N_CMP - 1] = 0.0
    expand_t = (np.arange(SEQ)[:, None] // SLC_LEN == np.arange(N_SLC)[None, :]).astype(np.float32)
    return jnp.asarray(overlap.T, BF16), jnp.asarray(np.eye(LANE), BF16), jnp.asarray(expand_t, BF16)


def nsa_attention(p16, cmp_kv, p32, q_col, kv_col, small_col):
    nq = SEQ // NSA_TQ
    ovl_t, eye, expand = _nsa_constants()
    kv_spec = lambda n: pl.BlockSpec((SEQ, NSA_KV), lambda b, i: (b, kv_col // NSA_KV + n))
    return pl.pallas_call(
        _nsa_kernel,
        grid=(BATCH, nq),
        in_specs=[pl.BlockSpec((NSA_TQ, NSA_Q), lambda b, i: (b * nq + i, q_col // NSA_Q)),
                  kv_spec(0), kv_spec(1), kv_spec(2), kv_spec(3),
                  pl.BlockSpec((1, 2 * NSA_GROUPS, N_CMP, NSA_DH), lambda b, i: (b, 0, 0, 0)),
                  pl.BlockSpec((NSA_TQ, LANE), lambda b, i: (b * nq + i, small_col // LANE)),
                  _resident(ovl_t.shape), _resident(eye.shape), _resident(expand.shape)],
        out_specs=pl.BlockSpec((NSA_TQ, NSA_Q), lambda b, i: (b * nq + i, 0)),
        out_shape=jax.ShapeDtypeStruct((TOKENS, NSA_Q), BF16),
        scratch_shapes=[pltpu.VMEM((NSA_GROUPS, SEQ, LANE), BF16),
                        pltpu.VMEM((NSA_GROUPS, LANE, NSA_HG * NSA_TQ), BF16),
                        pltpu.VMEM((NSA_GROUPS, NSA_DH, NSA_HG * NSA_TQ), F32)]
        + [pltpu.VMEM((NSA_GROUPS, 1, NSA_HG * NSA_TQ), F32)] * 2
        + [pltpu.VMEM((NSA_GROUPS, V1_ROWS, SEQ), BF16)] * 2
        + [pltpu.VMEM((NSA_GROUPS, NSA_DH, N_CMP), BF16)],
        compiler_params=pltpu.CompilerParams(dimension_semantics=("arbitrary", "arbitrary"),
                                             vmem_limit_bytes=VMEM_LIMIT),
        name="nsa_attention",
    )(p16, p16, p16, p16, p16, cmp_kv, p32, ovl_t, eye, expand)


EVEN_QKV_COL = 0
EVEN_GATE_COL = 2 * GDN_QK + GDN_V
EVEN_CMP_COL = EVEN_GATE_COL + GDN_V
EVEN_SMALL_COL = EVEN_CMP_COL + 2 * NSA_KV
EVEN_F32_W = EVEN_SMALL_COL + LANE
EVEN_Q_COL = 0
EVEN_KV_COL = NSA_Q
EVEN_BF16_W = NSA_Q + 4 * NSA_KV
assert NSA_SMALL_GATE + 3 * NSA_HEADS <= LANE


def even_proj_weight(w_in):
    qkv_a, beta, dec, gate_a, q_b, kv_b, gate_b = split_last(w_in, EVEN_SPLITS)
    pad = jnp.zeros((w_in.shape[0], LANE - 2 * GDN_HEADS - 3 * NSA_HEADS), w_in.dtype)
    return jnp.concatenate([qkv_a, gate_a, kv_b[:, :2 * NSA_KV], beta, dec, gate_b, pad, q_b, kv_b[:, 2 * NSA_KV:]],
                           axis=1).astype(BF16)


def even_mixer_core(p32, p16, conv_w, A_log, dt_bias, o_norm, cmp_pe, cmp_w1, cmp_b1, cmp_w2):
    oa = gated_delta_net(p32, conv_w, A_log, dt_bias, o_norm, EVEN_QKV_COL, EVEN_GATE_COL, EVEN_SMALL_COL)
    cmp_kv = nsa_compress(p32, EVEN_CMP_COL, cmp_pe, cmp_w1, cmp_b1, cmp_w2)
    ob = nsa_attention(p16, cmp_kv, p32, EVEN_Q_COL, EVEN_KV_COL, EVEN_SMALL_COL)
    return oa, ob


GLA_TC = 512
ODD_Q_COL = 0
ODD_K_COL = GLA_QK
ODD_V_COL = 2 * GLA_QK
ODD_R_COL = ODD_V_COL + GLA_V
ODD_LR_COL = ODD_R_COL + GLA_V
ODD_F32_W = ODD_LR_COL + LANE


def _gla_kernel(q_ref, k_ref, v_ref, r_ref, lr_ref, aw_ref, ab_ref, onorm_ref, tri_ref, o_ref, state_ref):
    t = pl.program_id(1)
    C = GLA_CHUNK

    @pl.when(t == 0)
    def _():
        state_ref[...] = jnp.zeros(state_ref.shape, F32)

    z = jnp.dot(lr_ref[...].astype(BF16), aw_ref[...], preferred_element_type=F32) + ab_ref[...]
    log_a = (jnp.minimum(z, 0.0) - jnp.log(1.0 + jnp.exp(-jnp.abs(z)))) * (1.0 / GLA_TAU)
    tri = tri_ref[...]
    causal = lax.broadcasted_iota(jnp.int32, (C, C), 0) >= lax.broadcasted_iota(jnp.int32, (C, C), 1)

    n_chunks = GLA_TC // C
    b_alls = [_cumsum_rows(tri, log_a[c * C:(c + 1) * C]) for c in range(n_chunks)]
    work = {}
    for c in range(n_chunks):
        rows = slice(c * C, (c + 1) * C)
        for h in range(GLA_HEADS):
            kl = slice(h * GLA_DK, (h + 1) * GLA_DK)
            b = b_alls[c][:, kl]
            b_last = b[C - 1:C]
            k = k_ref[rows, kl]
            work[c, h] = dict(
                rows=rows, vl=slice(h * GLA_DV, (h + 1) * GLA_DV),
                v=v_ref[rows, h * GLA_DV:(h + 1) * GLA_DV].astype(BF16),
                qt=(q_ref[rows, kl] * GLA_DK ** -0.5 * jnp.exp(b)).astype(BF16),
                kt=(k * jnp.exp(-b)).astype(BF16),
                kh=(k * jnp.exp(b_last - b)).astype(BF16),
                keep=jnp.exp(b_last))
    wks = [work[key] for key in sorted(work)]
    intras = [jnp.where(causal, _dot_nt(wk["qt"], wk["kt"]), 0.0).astype(BF16) for wk in wks]
    o_intras = [jnp.dot(intra, wk["v"], preferred_element_type=F32) for wk, intra in zip(wks, intras)]
    updates = [_dot_tn(wk["v"], wk["kh"]) for wk in wks]

    states = [state_ref[h] for h in range(GLA_HEADS)]
    starts = []
    for (c, h), wk, upd in zip(sorted(work), wks, updates):
        starts.append(states[h].astype(BF16))
        states[h] = states[h] * wk["keep"] + upd
    for h in range(GLA_HEADS):
        state_ref[h] = states[h]

    o_inters = [_dot_nt(wk["qt"], s0) for wk, s0 in zip(wks, starts)]
    for wk, o_intra, o_inter in zip(wks, o_intras, o_inters):
        o = o_intra + o_inter
        r = r_ref[wk["rows"], wk["vl"]]
        on = o * lax.rsqrt(jnp.mean(o * o, axis=-1, keepdims=True) + EPS) * onorm_ref[...]
        o_ref[wk["rows"], wk["vl"]] = (on * (r * jax.nn.sigmoid(r))).astype(o_ref.dtype)


def gated_linear_attention(p32, a_w2, a_b, o_norm):
    nt = SEQ // GLA_TC
    tok = lambda blk: (lambda b, t: (b * nt + t, blk))
    aw = jnp.pad(a_w2, ((0, LANE - GLA_RANK), (0, 0))).astype(BF16)
    tri = jnp.asarray(np.tril(np.ones((GLA_CHUNK, GLA_CHUNK))), BF16)
    return pl.pallas_call(
        _gla_kernel,
        grid=(BATCH, nt),
        in_specs=[pl.BlockSpec((GLA_TC, GLA_QK), tok(ODD_Q_COL // GLA_QK)),
                  pl.BlockSpec((GLA_TC, GLA_QK), tok(ODD_K_COL // GLA_QK)),
                  pl.BlockSpec((GLA_TC, GLA_V), tok(ODD_V_COL // GLA_V)),
                  pl.BlockSpec((GLA_TC, GLA_V), tok(ODD_R_COL // GLA_V)),
                  pl.BlockSpec((GLA_TC, LANE), tok(ODD_LR_COL // LANE)),
                  _resident((LANE, GLA_QK)), _resident((1, GLA_QK)), _resident((1, GLA_DV)),
                  _resident((GLA_CHUNK, GLA_CHUNK))],
        out_specs=pl.BlockSpec((GLA_TC, GLA_V), tok(0)),
        out_shape=jax.ShapeDtypeStruct((TOKENS, GLA_V), BF16),
        scratch_shapes=[pltpu.VMEM((GLA_HEADS, GLA_DV, GLA_DK), F32)],
        compiler_params=pltpu.CompilerParams(dimension_semantics=("arbitrary", "arbitrary"),
                                             vmem_limit_bytes=VMEM_LIMIT),
        name="gated_linear_attention",
    )(p32, p32, p32, p32, p32, aw, a_b.reshape(1, GLA_QK), o_norm.reshape(1, GLA_DV), tri)


def _pad_cols(w, n):
    return jnp.pad(w, ((0, 0), (0, n - w.shape[1])))


def kernel(x, norm_gains, ffn_w_in, ffn_conv_w, ffn_conv_b, ffn_w_out, hyb_w_in, hyb_w_out, gdn_conv_w, gdn_A_log,
           gdn_dt_bias, gdn_o_norm, nsa_cmp_pe, nsa_cmp_w1, nsa_cmp_b1, nsa_cmp_w2, gla_w_in, gla_a_w2, gla_a_b,
           gla_o_norm, gla_w_out):
    x2 = x.reshape(TOKENS, D_MODEL)
    ffn_w_in_bf = ffn_w_in.astype(BF16)
    ffn_w_out_bf = ffn_w_out.astype(BF16)
    for layer in range(DEPTH):
        if layer % 2 == 0:
            e = layer // 2
            p32, p16 = norm_proj(x2, norm_gains[layer, 0], even_proj_weight(hyb_w_in[e]),
                                 ((EVEN_F32_W, F32), (EVEN_BF16_W, BF16)))
            oa, ob = even_mixer_core(p32, p16, gdn_conv_w[e], gdn_A_log[e], gdn_dt_bias[e], gdn_o_norm[e],
                                     nsa_cmp_pe[e], nsa_cmp_w1[e], nsa_cmp_b1[e], nsa_cmp_w2[e])
            halves = (oa, 0, ob, 0)
            w_mix = hyb_w_out[e]
        else:
            o_ = layer // 2
            w = _pad_cols(gla_w_in[o_], ODD_F32_W).astype(BF16)
            (p32,) = norm_proj(x2, norm_gains[layer, 0], w, ((ODD_F32_W, F32),))
            o = gated_linear_attention(p32, gla_a_w2[o_], gla_a_b[o_], gla_o_norm[o_])
            halves = (o, 0, o, 1)
            w_mix = gla_w_out[o_]
        x2 = mix_out_conv_ffn(*halves, x2, norm_gains[layer, 1], w_mix.astype(BF16), norm_gains[layer, 2],
                              layer, ffn_w_in_bf, ffn_conv_w, ffn_conv_b, ffn_w_out_bf, norm_gains[layer, 3])
    return x2.reshape(BATCH, SEQ, D_MODEL)
```

```python
import jax
import jax.numpy as jnp
import numpy as np
from jax import lax
from jax.experimental import pallas as pl
from jax.experimental.pallas import tpu as pltpu

D_MODEL = 1024
BATCH = 8
SEQ = 2048
DEPTH = 4
TOKENS = BATCH * SEQ

EPS = 1e-6
NEG = -1e30
FORCE = 1e4

GDN_DK = 128
GDN_DV = 128
GDN_HEADS = 4
GDN_QK = GDN_HEADS * GDN_DK
GDN_V = GDN_HEADS * GDN_DV
GDN_CONV = 4
GDN_CHUNK = 64

NSA_DH = 64
NSA_HEADS = 8
NSA_GROUPS = 2
NSA_Q = NSA_HEADS * NSA_DH
NSA_KV = NSA_GROUPS * NSA_DH
CMP_LEN = 32
CMP_STRIDE = 16
CMP_HIDDEN = 2 * NSA_DH
SLC_LEN = 64
SLC_TOPK = 8
WINDOW = 512

GLA_HEADS = 4
GLA_DK = 128
GLA_DV = 256
GLA_QK = GLA_HEADS * GLA_DK
GLA_V = GLA_HEADS * GLA_DV
GLA_RANK = 16
GLA_TAU = 16.0
GLA_CHUNK = 64

D_FF = 2816
FFN_CONV = 3

EVEN_SPLITS = (2 * GDN_QK + GDN_V, GDN_HEADS, GDN_HEADS, GDN_V, NSA_Q, 6 * NSA_KV, 3 * NSA_HEADS)
ODD_SPLITS = (GLA_QK, GLA_QK, GLA_V, GLA_V, GLA_RANK)

LANE = 128
VMEM_LIMIT = 56 * 1024 * 1024

F32 = jnp.float32
BF16 = jnp.bfloat16


def _resident(shape):
    nd = len(shape)
    return pl.BlockSpec(shape, lambda *_: (0,) * nd, pipeline_mode=pl.Buffered(1))


def _resident_layer(shape, layer):
    nd = len(shape)
    return pl.BlockSpec((None,) + tuple(shape), lambda *_: (layer,) + (0,) * nd, pipeline_mode=pl.Buffered(1))


def _rms(x, g):
    ms = jnp.mean(x * x, axis=-1, keepdims=True)
    return x * lax.rsqrt(ms + EPS) * g


PROJ_TM = 512
PROJ_CH = 512


def _proj_kernel(x_ref, g_ref, w_ref, *o_refs):
    h = _rms(x_ref[...], g_ref[...]).astype(BF16)
    col = 0
    for o_ref in o_refs:
        n = o_ref.shape[1]
        for c0 in range(0, n, PROJ_CH):
            cw = min(PROJ_CH, n - c0)
            y = jnp.dot(h, w_ref[:, col + c0:col + c0 + cw], preferred_element_type=F32)
            o_ref[:, c0:c0 + cw] = y.astype(o_ref.dtype)
        col += n


def norm_proj(x2, gain, w, sections):
    T, D = x2.shape
    N = w.shape[1]
    assert N == sum(n for n, _ in sections) and all(n % LANE == 0 for n, _ in sections)
    return pl.pallas_call(
        _proj_kernel,
        grid=(T // PROJ_TM,),
        in_specs=[pl.BlockSpec((PROJ_TM, D), lambda i: (i, 0)), _resident((1, D)), _resident((D, N))],
        out_specs=[pl.BlockSpec((PROJ_TM, n), lambda i: (i, 0)) for n, _ in sections],
        out_shape=[jax.ShapeDtypeStruct((T, n), dt) for n, dt in sections],
        compiler_params=pltpu.CompilerParams(dimension_semantics=("arbitrary",), vmem_limit_bytes=VMEM_LIMIT),
        name="norm_proj",
    )(x2, gain.reshape(1, D), w)


FFN_TM = 512
FFN_TAIL = 8
FFN_CH = 256
MIX_HALF = D_MODEL // 2
assert D_FF % FFN_CH == 0


def _ffn_kernel(oa_ref, ob_ref, x_ref, g_mix_ref, w_mix_ref, g_in_ref, w_in_ref, cw_ref, cb_ref, w_out_ref, g_out_ref,
                y_ref, tail_ref, act_ref):
    t = pl.program_id(1)

    @pl.when(t == 0)
    def _():
        tail_ref[...] = jnp.zeros(tail_ref.shape, F32)

    mix = (jnp.dot(oa_ref[...], w_mix_ref[0:MIX_HALF, :], preferred_element_type=F32)
           + jnp.dot(ob_ref[...], w_mix_ref[MIX_HALF:, :], preferred_element_type=F32))
    x = x_ref[...] + _rms(mix, g_mix_ref[...])
    h = _rms(x, g_in_ref[...]).astype(BF16)

    def conv_half(c0):
        j = c0 // FFN_CH
        u = jnp.dot(h, w_in_ref[:, c0:c0 + FFN_CH], preferred_element_type=F32)
        ext = jnp.concatenate([tail_ref[j], u], axis=0)
        tail_ref[j] = u[FFN_TM - FFN_TAIL:]
        w = cw_ref[:, c0:c0 + FFN_CH]
        y = pltpu.roll(ext, 2, 0) * w[0:1] + pltpu.roll(ext, 1, 0) * w[1:2] + ext * w[2:3] + cb_ref[:, c0:c0 + FFN_CH]
        return y[FFN_TAIL:]

    for c0 in range(0, D_FF, FFN_CH):
        half_gate = (0.5 * conv_half(c0)).astype(BF16)
        val = conv_half(D_FF + c0).astype(BF16)
        silu = half_gate + half_gate * jnp.tanh(half_gate)
        act_ref[:, c0:c0 + FFN_CH] = silu * val
    down = jnp.dot(act_ref[...], w_out_ref[...], preferred_element_type=F32)
    y_ref[...] = x + _rms(down, g_out_ref[...])


def mix_out_conv_ffn(oa, a_blk, ob, b_blk, x2, g_mix, w_mix, g_in, layer, w_in, conv_w, conv_b, w_out, g_out):
    T, D = x2.shape
    nt = SEQ // FFN_TM
    tok = lambda blk: (lambda b, t: (b * nt + t, blk))
    return pl.pallas_call(
        _ffn_kernel,
        grid=(BATCH, nt),
        in_specs=[pl.BlockSpec((FFN_TM, MIX_HALF), tok(a_blk)), pl.BlockSpec((FFN_TM, MIX_HALF), tok(b_blk)),
                  pl.BlockSpec((FFN_TM, D), tok(0)), _resident((1, D)), _resident((D, D)),
                  _resident((1, D)), _resident_layer((D, 2 * D_FF), layer),
                  _resident_layer((FFN_CONV, 2 * D_FF), layer), _resident_layer((1, 2 * D_FF), layer),
                  _resident_layer((D_FF, D), layer), _resident((1, D))],
        out_specs=pl.BlockSpec((FFN_TM, D), tok(0)),
        out_shape=jax.ShapeDtypeStruct((T, D), F32),
        scratch_shapes=[pltpu.VMEM((2 * D_FF // FFN_CH, FFN_TAIL, FFN_CH), F32), pltpu.VMEM((FFN_TM, D_FF), BF16)],
        compiler_params=pltpu.CompilerParams(dimension_semantics=("arbitrary", "arbitrary"),
                                             vmem_limit_bytes=VMEM_LIMIT),
        name="mix_out_conv_ffn",
    )(oa, ob, x2, g_mix.reshape(1, D), w_mix, g_in.reshape(1, D), w_in, conv_w, conv_b.reshape(DEPTH, 1, -1), w_out,
      g_out.reshape(1, D))


def split_last(x, sizes):
    idx = np.cumsum(sizes)[:-1].tolist()
    return jnp.split(x, idx, axis=-1)


def _dot_nt(a, b):
    return lax.dot_general(a, b, (((1,), (1,)), ((), ())), preferred_element_type=F32)


def _dot_tn(a, b):
    return lax.dot_general(a, b, (((0,), (0,)), ((), ())), preferred_element_type=F32)


def _split3(a):
    hi = a.astype(BF16)
    r = a - hi.astype(F32)
    mid = r.astype(BF16)
    return hi, mid, (r - mid.astype(F32)).astype(BF16)


def _dot_x3(a, b):
    a_hi, a_lo, _ = _split3(a)
    b_hi, b_lo, _ = _split3(b)
    d = lambda p, q: jnp.dot(p, q, preferred_element_type=F32)
    return d(a_hi, b_hi) + (d(a_lo, b_hi) + d(a_hi, b_lo))


def _cumsum_rows(tri, g):
    return sum(jnp.dot(tri, t, preferred_element_type=F32) for t in reversed(_split3(g)))


def _unit_lower_inverses(mats):
    c = mats[0].shape[0]
    eye = jnp.where(lax.broadcasted_iota(jnp.int32, (c, c), 0) == lax.broadcasted_iota(jnp.int32, (c, c), 1), 1.0, 0.0)
    dot = lambda a, b: jnp.dot(a.astype(BF16), b.astype(BF16), preferred_element_type=F32)
    ps = [-a for a in mats]
    ts = [eye + p for p in ps]
    k = 2
    while k < c:
        ps = [dot(p, p) for p in ps]
        ts = [t + dot(t, p) for t, p in zip(ts, ps)]
        k *= 2
    res = [(eye - t) - _dot_x3(a, t) for a, t in zip(mats, ts)]
    return [t + dot(t, r) for t, r in zip(ts, res)]


GDN_TC = 512
GDN_HALO = 8
GDN_SMALL_BETA = 0
GDN_SMALL_DECAY = GDN_HEADS
NSA_SMALL_GATE = 2 * GDN_HEADS


def _gdn_kernel(x_ref, gate_ref, small_ref, cw_ref, alog_ref, dtb_ref, onorm_ref, tri_ref, o_ref, xext_ref, state_ref):
    t = pl.program_id(1)
    C = GDN_CHUNK

    @pl.when(t == 0)
    def _():
        xext_ref[0:GDN_HALO, :] = jnp.zeros((GDN_HALO, xext_ref.shape[1]), F32)
        state_ref[...] = jnp.zeros(state_ref.shape, F32)

    xext_ref[GDN_HALO:, :] = x_ref[...]
    small = small_ref[...]
    beta_all = jax.nn.sigmoid(small)
    z = small + dtb_ref[...]
    softplus = jnp.maximum(z, 0.0) + jnp.log(1.0 + jnp.exp(-jnp.abs(z)))
    g_all = -jnp.exp(alog_ref[...]) * softplus
    tri = tri_ref[...]
    row = lax.broadcasted_iota(jnp.int32, (C, C), 0)
    col = lax.broadcasted_iota(jnp.int32, (C, C), 1)

    def conv_silu(c0, width):
        w = cw_ref[:, c0:c0 + width]
        y = xext_ref[GDN_HALO:, c0:c0 + width] * w[GDN_CONV - 1:GDN_CONV]
        for s in range(1, GDN_CONV):
            y = y + xext_ref[GDN_HALO - s:GDN_HALO - s + GDN_TC, c0:c0 + width] * w[GDN_CONV - 1 - s:GDN_CONV - s]
        return y * jax.nn.sigmoid(y)

    n_chunks = GDN_TC // C
    gsums = [_cumsum_rows(tri, g_all[c * C:(c + 1) * C]) for c in range(n_chunks)]
    gsum_ts = [jnp.transpose(gs) for gs in gsums]

    work = {}
    for h in range(GDN_HEADS):
        q_all = conv_silu(h * GDN_DK, GDN_DK)
        k_all = conv_silu(GDN_QK + h * GDN_DK, GDN_DK)
        v_all = conv_silu(2 * GDN_QK + h * GDN_DV, GDN_DV)
        q_all = q_all * (lax.rsqrt(jnp.sum(q_all * q_all, axis=-1, keepdims=True) + EPS) * GDN_DK ** -0.5)
        k_all = k_all * lax.rsqrt(jnp.sum(k_all * k_all, axis=-1, keepdims=True) + EPS)
        for c in range(n_chunks):
            rows = slice(c * C, (c + 1) * C)
            q, k, v = q_all[rows], k_all[rows], v_all[rows]
            beta = beta_all[rows, GDN_SMALL_BETA + h:GDN_SMALL_BETA + h + 1]
            gcol = gsums[c][:, GDN_SMALL_DECAY + h:GDN_SMALL_DECAY + h + 1]
            grow = gsum_ts[c][GDN_SMALL_DECAY + h:GDN_SMALL_DECAY + h + 1, :]
            glast = gcol[C - 1:C]
            decay = jnp.exp(jnp.where(row >= col, gcol - grow, -jnp.inf))
            kb = k * beta
            kbf = k.astype(BF16)
            eg = jnp.exp(gcol)
            work[h, c] = dict(
                a_kk=jnp.where(row > col, _dot_nt(kb.astype(BF16), kbf) * decay, 0.0),
                a_qk=(_dot_nt(q.astype(BF16), kbf) * decay).astype(BF16),
                rhs=jnp.concatenate([v * beta, kb * eg], axis=-1),
                qg=q * eg,
                kd_t=jnp.transpose(k * jnp.exp(glast - gcol)).astype(BF16),
                keep=jnp.exp(glast))
    keys = sorted(work)
    inverses = _unit_lower_inverses([work[key]["a_kk"] for key in keys])
    wks = [work[key] for key in keys]
    dot = lambda a, b: jnp.dot(a, b, preferred_element_type=F32)
    sols = [dot(inv.astype(BF16), wk["rhs"].astype(BF16)) for wk, inv in zip(wks, inverses)]
    us = [sol[:, :GDN_DV].astype(BF16) for sol in sols]
    ws = [sol[:, GDN_DV:].astype(BF16) for sol in sols]
    for wk, s_in in zip(wks, [dot(wk["kd_t"], u) for wk, u in zip(wks, us)]):
        wk["s_in"] = s_in
    for wk, s_mix in zip(wks, [dot(wk["kd_t"], w) for wk, w in zip(wks, ws)]):
        wk["s_mix"] = s_mix.astype(BF16)
    for wk, o_in in zip(wks, [dot(wk["a_qk"], u) for wk, u in zip(wks, us)]):
        wk["o_in"] = o_in
    for wk, aw in zip(wks, [dot(wk["a_qk"], w) for wk, w in zip(wks, ws)]):
        wk["o_mix"] = (wk["qg"] - aw).astype(BF16)

    states = [state_ref[h] for h in range(GDN_HEADS)]
    for c in range(n_chunks):
        rows = slice(c * C, (c + 1) * C)
        for h in range(GDN_HEADS):
            wk = work[h, c]
            sb = states[h].astype(BF16)
            o = wk["o_in"] + jnp.dot(wk["o_mix"], sb, preferred_element_type=F32)
            states[h] = (states[h] * wk["keep"] + wk["s_in"]) - jnp.dot(wk["s_mix"], sb, preferred_element_type=F32)
            gate = gate_ref[rows, h * GDN_DV:(h + 1) * GDN_DV]
            on = o * lax.rsqrt(jnp.mean(o * o, axis=-1, keepdims=True) + EPS) * onorm_ref[...]
            o_ref[rows, h * GDN_DV:(h + 1) * GDN_DV] = (on * (gate * jax.nn.sigmoid(gate))).astype(o_ref.dtype)
    for h in range(GDN_HEADS):
        state_ref[h] = states[h]
    xext_ref[0:GDN_HALO, :] = xext_ref[GDN_TC:GDN_TC + GDN_HALO, :]


def gated_delta_net(p32, conv_w, a_log, dt_bias, o_norm, qkv_col, gate_col, small_col):
    nt = SEQ // GDN_TC
    width = 2 * GDN_QK + GDN_V
    lane_vec = lambda a, off: jnp.zeros((1, LANE), F32).at[0, off:off + GDN_HEADS].set(a.astype(F32))
    tri = jnp.asarray(np.tril(np.ones((GDN_CHUNK, GDN_CHUNK))), BF16)
    tok = lambda blk: (lambda b, t: (b * nt + t, blk))
    return pl.pallas_call(
        _gdn_kernel,
        grid=(BATCH, nt),
        in_specs=[pl.BlockSpec((GDN_TC, width), tok(qkv_col // width)),
                  pl.BlockSpec((GDN_TC, GDN_V), tok(gate_col // GDN_V)),
                  pl.BlockSpec((GDN_TC, LANE), tok(small_col // LANE)),
                  _resident((GDN_CONV, width)), _resident((1, LANE)), _resident((1, LANE)), _resident((1, GDN_DV)),
                  _resident((GDN_CHUNK, GDN_CHUNK))],
        out_specs=pl.BlockSpec((GDN_TC, GDN_V), tok(0)),
        out_shape=jax.ShapeDtypeStruct((TOKENS, GDN_V), BF16),
        scratch_shapes=[pltpu.VMEM((GDN_HALO + GDN_TC, width), F32), pltpu.VMEM((GDN_HEADS, GDN_DK, GDN_DV), F32)],
        compiler_params=pltpu.CompilerParams(dimension_semantics=("arbitrary", "arbitrary"),
                                             vmem_limit_bytes=VMEM_LIMIT),
        name="gated_delta_net",
    )(p32, p32, p32, conv_w, lane_vec(a_log, GDN_SMALL_DECAY), lane_vec(dt_bias, GDN_SMALL_DECAY),
      o_norm.reshape(1, GDN_DV), tri)


N_SLC = SEQ // SLC_LEN
N_CMP = SEQ // CMP_STRIDE
NSA_HG = NSA_HEADS // NSA_GROUPS
NSA_TQ = 128
SEL_STEP = 256
WIN_W = WINDOW + NSA_TQ
NSA_SCALE = NSA_DH ** -0.5


def _cmp_kernel(kc_ref, vc_ref, pe_ref, w1_ref, b1_ref, w2_ref, o_ref, strip_ref):
    half = CMP_STRIDE * NSA_DH
    for kv, x_ref in enumerate((kc_ref, vc_ref)):
        for p in range(CMP_STRIDE):
            xp = x_ref[pl.ds(p, N_CMP, stride=CMP_STRIDE), :]
            for g in range(NSA_GROUPS):
                strip_ref[kv * NSA_GROUPS + g, :, p * NSA_DH:(p + 1) * NSA_DH] = xp[:, g * NSA_DH:(g + 1) * NSA_DH]
    for kv in range(2):
        for g in range(NSA_GROUPS):
            rows = strip_ref[kv * NSA_GROUPS + g]
            lo = (rows + pe_ref[kv, :, :half]).astype(BF16)
            hi = (rows + pe_ref[kv, :, half:]).astype(BF16)
            a = jnp.dot(lo, w1_ref[kv, :half, :], preferred_element_type=F32)
            b = jnp.dot(hi, w1_ref[kv, half:, :], preferred_element_type=F32)
            hid = a + pltpu.roll(b, N_CMP - 1, 0) + b1_ref[kv]
            act = (hid * jax.nn.sigmoid(hid)).astype(BF16)
            o_ref[0, kv * NSA_GROUPS + g] = jnp.dot(act, w2_ref[kv], preferred_element_type=F32).astype(BF16)


def nsa_compress(p32, cmp_col, pe, w1, b1, w2):
    n = 2 * NSA_GROUPS
    return pl.pallas_call(
        _cmp_kernel,
        grid=(BATCH,),
        in_specs=[pl.BlockSpec((SEQ, NSA_KV), lambda b: (b, cmp_col // NSA_KV)),
                  pl.BlockSpec((SEQ, NSA_KV), lambda b: (b, cmp_col // NSA_KV + 1)),
                  _resident((2, 1, CMP_LEN * NSA_DH)), _resident((2, CMP_LEN * NSA_DH, CMP_HIDDEN)),
                  _resident((2, 1, CMP_HIDDEN)), _resident((2, CMP_HIDDEN, NSA_DH))],
        out_specs=pl.BlockSpec((1, n, N_CMP, NSA_DH), lambda b: (b, 0, 0, 0)),
        out_shape=jax.ShapeDtypeStruct((BATCH, n, N_CMP, NSA_DH), BF16),
        scratch_shapes=[pltpu.VMEM((n, N_CMP, CMP_STRIDE * NSA_DH), F32)],
        compiler_params=pltpu.CompilerParams(dimension_semantics=("arbitrary",), vmem_limit_bytes=VMEM_LIMIT),
        name="nsa_compress",
    )(p32, p32, pe.reshape(2, 1, -1), w1.astype(BF16), b1.reshape(2, 1, -1), w2.astype(BF16))


V1_ROWS = NSA_DH + 16
NSA_PAIR = 2
NSA_AHEAD = 3
MASK_BIG = 2.0 ** 100


def _masked_attention_t(k, q_t, v1_t, bias_t):
    s = jnp.dot(k, q_t, preferred_element_type=F32) + jnp.concatenate([bias_t] * NSA_HG, axis=1)
    p = jnp.exp((s - jnp.max(s, axis=0, keepdims=True)).astype(BF16))
    acc = jnp.dot(v1_t, p, preferred_element_type=F32)
    return acc[:NSA_DH] / acc[NSA_DH:NSA_DH + 1]


def _transposed_with_ones_row(eye, v):
    v_t = _dot_nt(eye, v).astype(v.dtype)
    row = lax.broadcasted_iota(jnp.int32, (V1_ROWS - NSA_DH, v.shape[0]), 0)
    return jnp.concatenate([v_t, jnp.where(row == 0, 1.0, 0.0).astype(v.dtype)], axis=0)


def _nsa_kernel(q_ref, ks_ref, vs_ref, kw_ref, vw_ref, cmp_ref, gate_ref, ovl_ref, eye_ref, expand_ref,
                o_ref, ksx_ref, qt_ref, ocmp_ref, gsel_ref, gwin_ref, vs1_ref, vw1_ref, vct_ref):
    i = pl.program_id(1)
    q0 = i * NSA_TQ
    eye = eye_ref[...]
    eye_dh = eye[0:NSA_DH, 0:NSA_DH]

    @pl.when(i == 0)
    def _():
        for g in range(NSA_GROUPS):
            lanes = slice(g * NSA_DH, (g + 1) * NSA_DH)
            vs1_ref[g] = _transposed_with_ones_row(eye_dh, vs_ref[:, lanes])
            vw1_ref[g] = _transposed_with_ones_row(eye_dh, vw_ref[:, lanes])
            vct_ref[g] = _dot_nt(eye_dh, cmp_ref[0, NSA_GROUPS + g]).astype(BF16)
            ksx_ref[g] = jnp.concatenate(
                [ks_ref[:, lanes], expand_ref[...], jnp.zeros((SEQ, LANE - NSA_DH - N_SLC), BF16)], axis=1)
            qt_ref[g, NSA_DH + N_SLC:, :] = jnp.zeros((LANE - NSA_DH - N_SLC, NSA_HG * NSA_TQ), BF16)

    gates = jax.nn.sigmoid(gate_ref[...])
    gates_t = jnp.transpose(gates)
    tok_l = q0 + lax.broadcasted_iota(jnp.int32, (1, NSA_TQ), 1)

    groups = range(NSA_GROUPS)
    scale = jnp.asarray(NSA_SCALE, BF16)
    for pair in range(NSA_HEADS // 2):
        pair_t = _dot_nt(eye, q_ref[:, pair * LANE:(pair + 1) * LANE] * scale).astype(BF16)
        for k in range(2):
            g, hk = divmod(2 * pair + k, NSA_HG)
            qt_ref[g, 0:NSA_DH, hk * NSA_TQ:(hk + 1) * NSA_TQ] = pair_t[k * NSA_DH:(k + 1) * NSA_DH]

    def gate_row(g, branch):
        lane = NSA_SMALL_GATE + branch * NSA_HEADS + g * NSA_HG
        return jnp.concatenate([gates_t[lane + hk:lane + hk + 1, :] for hk in range(NSA_HG)], axis=1)

    cmp_end = lax.broadcasted_iota(jnp.int32, (N_CMP, 1), 0) * CMP_STRIDE + (CMP_LEN - 1)
    cvalid = cmp_end <= jnp.concatenate([tok_l] * NSA_HG, axis=1)
    ss = [jnp.dot(cmp_ref[0, g], qt_ref[g, 0:NSA_DH, :], preferred_element_type=F32) for g in groups]
    sms = [jnp.where(cvalid, s, NEG) for s in ss]
    es = [jnp.exp(sm - jnp.max(sm, axis=0, keepdims=True)) for sm in sms]
    ps = [jnp.where(cvalid, e / jnp.sum(e, axis=0, keepdims=True), 0.0) for e in es]
    for g in groups:
        ocmp_ref[g] = gate_row(g, 0) * jnp.dot(vct_ref[g], ps[g].astype(BF16), preferred_element_type=F32)
        gsel_ref[g] = gate_row(g, 1)
        gwin_ref[g] = gate_row(g, 2)

    psums = [sum(p[:, hk * NSA_TQ:(hk + 1) * NSA_TQ] for hk in range(NSA_HG)) for p in ps]
    imps = [jnp.dot(ovl_ref[...], psum.astype(BF16), preferred_element_type=F32) for psum in psums]
    blk = lax.broadcasted_iota(jnp.int32, (N_SLC, NSA_TQ), 0)
    cur = tok_l // SLC_LEN
    bvalid = blk <= cur
    forced = (blk == 0) | (blk == cur) | (blk == cur - 1)
    scores_ = [jnp.where(bvalid, imp + jnp.where(forced, FORCE, 0.0), NEG) for imp in imps]
    ranks = [jnp.zeros((N_SLC, NSA_TQ), F32) for _ in groups]
    for j in range(N_SLC):
        for g in groups:
            sj = scores_[g][j:j + 1, :]
            tie = jnp.where((sj == scores_[g]) & (blk > j), 1.0, 0.0)
            ranks[g] = ranks[g] + jnp.where(sj > scores_[g], 1.0, tie)
    for g in groups:
        off = jnp.where((ranks[g] < SLC_TOPK) & bvalid, 0.0, -MASK_BIG).astype(BF16)
        qt_ref[g, NSA_DH:NSA_DH + N_SLC, :] = jnp.concatenate([off] * NSA_HG, axis=1)

    for n in range(SEQ // SEL_STEP):
        @pl.when(i // (SEL_STEP // NSA_TQ) == n)
        def _():
            w = (n + 1) * SEL_STEP
            w0 = pl.multiple_of(jnp.maximum(q0 + NSA_TQ - WIN_W, 0), NSA_TQ)
            r = pl.ds(w0, WIN_W)

            def bias(pos, lo_excl):
                ok = (pos <= tok_l) if lo_excl is None else ((pos <= tok_l) & (pos > lo_excl))
                return jnp.concatenate([jnp.where(ok, 0.0, NEG)] * NSA_PAIR, axis=1)

            sel_causal = bias((w - SEL_STEP) + lax.broadcasted_iota(jnp.int32, (SEL_STEP, 1), 0), None)
            wpos = w0 + lax.broadcasted_iota(jnp.int32, (WIN_W, 1), 0)
            if n < WINDOW // SEL_STEP:
                win_edges = [(0, WIN_W, bias(wpos, tok_l - WINDOW))]
            else:
                win_edges = [(0, NSA_TQ, bias(wpos[:NSA_TQ], tok_l - WINDOW)),
                             (WIN_W - NSA_TQ, WIN_W, bias(wpos[WIN_W - NSA_TQ:], None))]

            def add_bias(s, edges):
                pieces, at = [], 0
                for lo, hi, b in edges:
                    pieces += [s[at:lo], s[lo:hi] + b]
                    at = hi
                return jnp.concatenate([p for p in pieces + [s[at:]] if p.shape[0]], axis=0)

            problems = []
            for g in range(NSA_GROUPS):
                lanes = slice(g * NSA_DH, (g + 1) * NSA_DH)
                for half in range(NSA_HG // NSA_PAIR):
                    cols = slice(half * NSA_PAIR * NSA_TQ, (half + 1) * NSA_PAIR * NSA_TQ)
                    problems.append((g, cols, 0, lambda g=g: ksx_ref[g, 0:w, :], lambda g=g: vs1_ref[g, :, 0:w],
                                     lambda g=g, cols=cols: qt_ref[g, :, cols], [(w - SEL_STEP, w, sel_causal)]))
                    problems.append((g, cols, 1, lambda lanes=lanes: kw_ref[r, lanes], lambda g=g: vw1_ref[g, :, r],
                                     lambda g=g, cols=cols: qt_ref[g, 0:NSA_DH, cols], win_edges))

            def scores(prob):
                _, _, _, k, _, q_t, edges = prob
                return add_bias(jnp.dot(k(), q_t(), preferred_element_type=F32), edges)

            def probs(s):
                return jnp.exp((s - jnp.max(s, axis=0, keepdims=True)).astype(BF16))

            def values(prob, p):
                acc = jnp.dot(prob[4](), p, preferred_element_type=F32)
                return acc[:NSA_DH] / acc[NSA_DH:NSA_DH + 1]

            problems = problems[1:2] + problems[0:1] + problems[2:]
            outs = {}
            ahead = [scores(prob) for prob in problems[:NSA_AHEAD]]
            for idx, prob in enumerate(problems):
                s_cur = ahead.pop(0)
                if idx + NSA_AHEAD < len(problems):
                    ahead.append(scores(problems[idx + NSA_AHEAD]))
                outs[prob[0], prob[1].start, prob[2]] = values(prob, probs(s_cur))

            for g in range(NSA_GROUPS):
                for half in range(NSA_HG // NSA_PAIR):
                    c0 = half * NSA_PAIR * NSA_TQ
                    cols = slice(c0, c0 + NSA_PAIR * NSA_TQ)
                    o_t = (ocmp_ref[g, :, cols] + gsel_ref[g, :, cols] * outs[g, c0, 0]
                           + gwin_ref[g, :, cols] * outs[g, c0, 1])
                    for k in range(NSA_PAIR):
                        h = g * NSA_HG + half * NSA_PAIR + k
                        o = jnp.transpose(o_t[:, k * NSA_TQ:(k + 1) * NSA_TQ])
                        o_ref[:, h * NSA_DH:(h + 1) * NSA_DH] = o.astype(o_ref.dtype)


def _nsa_constants():
    c0 = np.arange(N_CMP)[:, None] * CMP_STRIDE
    s0 = np.arange(N_SLC)[None, :] * SLC_LEN
    overlap = np.clip(np.minimum(c0 + CMP_LEN, s0 + SLC_LEN) - np.maximum(c0, s0), 0, None) / CMP_STRIDE
    overlap[N_CMP - 1] = 0.0
    expand_t = (np.arange(SEQ)[:, None] // SLC_LEN == np.arange(N_SLC)[None, :]).astype(np.float32)
    return jnp.asarray(overlap.T, BF16), jnp.asarray(np.eye(LANE), BF16), jnp.asarray(expand_t, BF16)


def nsa_attention(p16, cmp_kv, p32, q_col, kv_col, small_col):
    nq = SEQ // NSA_TQ
    ovl_t, eye, expand = _nsa_constants()
    kv_spec = lambda n: pl.BlockSpec((SEQ, NSA_KV), lambda b, i: (b, kv_col // NSA_KV + n))
    return pl.pallas_call(
        _nsa_kernel,
        grid=(BATCH, nq),
        in_specs=[pl.BlockSpec((NSA_TQ, NSA_Q), lambda b, i: (b * nq + i, q_col // NSA_Q)),
                  kv_spec(0), kv_spec(1), kv_spec(2), kv_spec(3),
                  pl.BlockSpec((1, 2 * NSA_GROUPS, N_CMP, NSA_DH), lambda b, i: (b, 0, 0, 0)),
                  pl.BlockSpec((NSA_TQ, LANE), lambda b, i: (b * nq + i, small_col // LANE)),
                  _resident(ovl_t.shape), _resident(eye.shape), _resident(expand.shape)],
        out_specs=pl.BlockSpec((NSA_TQ, NSA_Q), lambda b, i: (b * nq + i, 0)),
        out_shape=jax.ShapeDtypeStruct((TOKENS, NSA_Q), BF16),
        scratch_shapes=[pltpu.VMEM((NSA_GROUPS, SEQ, LANE), BF16),
                        pltpu.VMEM((NSA_GROUPS, LANE, NSA_HG * NSA_TQ), BF16),
                        pltpu.VMEM((NSA_GROUPS, NSA_DH, NSA_HG * NSA_TQ), F32)]
        + [pltpu.VMEM((NSA_GROUPS, 1, NSA_HG * NSA_TQ), F32)] * 2
        + [pltpu.VMEM((NSA_GROUPS, V1_ROWS, SEQ), BF16)] * 2
        + [pltpu.VMEM((NSA_GROUPS, NSA_DH, N_CMP), BF16)],
        compiler_params=pltpu.CompilerParams(dimension_semantics=("arbitrary", "arbitrary"),
                                             vmem_limit_bytes=VMEM_LIMIT),
        name="nsa_attention",
    )(p16, p16, p16, p16, p16, cmp_kv, p32, ovl_t, eye, expand)


EVEN_QKV_COL = 0
EVEN_GATE_COL = 2 * GDN_QK + GDN_V
EVEN_CMP_COL = EVEN_GATE_COL + GDN_V
EVEN_SMALL_COL = EVEN_CMP_COL + 2 * NSA_KV
EVEN_F32_W = EVEN_SMALL_COL + LANE
EVEN_Q_COL = 0
EVEN_KV_COL = NSA_Q
EVEN_BF16_W = NSA_Q + 4 * NSA_KV
assert NSA_SMALL_GATE + 3 * NSA_HEADS <= LANE


def even_proj_weight(w_in):
    qkv_a, beta, dec, gate_a, q_b, kv_b, gate_b = split_last(w_in, EVEN_SPLITS)
    pad = jnp.zeros((w_in.shape[0], LANE - 2 * GDN_HEADS - 3 * NSA_HEADS), w_in.dtype)
    return jnp.concatenate([qkv_a, gate_a, kv_b[:, :2 * NSA_KV], beta, dec, gate_b, pad, q_b, kv_b[:, 2 * NSA_KV:]],
                           axis=1).astype(BF16)


def even_mixer_core(p32, p16, conv_w, A_log, dt_bias, o_norm, cmp_pe, cmp_w1, cmp_b1, cmp_w2):
    oa = gated_delta_net(p32, conv_w, A_log, dt_bias, o_norm, EVEN_QKV_COL, EVEN_GATE_COL, EVEN_SMALL_COL)
    cmp_kv = nsa_compress(p32, EVEN_CMP_COL, cmp_pe, cmp_w1, cmp_b1, cmp_w2)
    ob = nsa_attention(p16, cmp_kv, p32, EVEN_Q_COL, EVEN_KV_COL, EVEN_SMALL_COL)
    return oa, ob


GLA_TC = 512
ODD_Q_COL = 0
ODD_K_COL = GLA_QK
ODD_V_COL = 2 * GLA_QK
ODD_R_COL = ODD_V_COL + GLA_V
ODD_LR_COL = ODD_R_COL + GLA_V
ODD_F32_W = ODD_LR_COL + LANE


def _gla_kernel(q_ref, k_ref, v_ref, r_ref, lr_ref, aw_ref, ab_ref, onorm_ref, tri_ref, o_ref, state_ref):
    t = pl.program_id(1)
    C = GLA_CHUNK

    @pl.when(t == 0)
    def _():
        state_ref[...] = jnp.zeros(state_ref.shape, F32)

    z = jnp.dot(lr_ref[...].astype(BF16), aw_ref[...], preferred_element_type=F32) + ab_ref[...]
    log_a = (jnp.minimum(z, 0.0) - jnp.log(1.0 + jnp.exp(-jnp.abs(z)))) * (1.0 / GLA_TAU)
    tri = tri_ref[...]
    causal = lax.broadcasted_iota(jnp.int32, (C, C), 0) >= lax.broadcasted_iota(jnp.int32, (C, C), 1)

    n_chunks = GLA_TC // C
    b_alls = [_cumsum_rows(tri, log_a[c * C:(c + 1) * C]) for c in range(n_chunks)]
    work = {}
    for c in range(n_chunks):
        rows = slice(c * C, (c + 1) * C)
        for h in range(GLA_HEADS):
            kl = slice(h * GLA_DK, (h + 1) * GLA_DK)
            b = b_alls[c][:, kl]
            b_last = b[C - 1:C]
            k = k_ref[rows, kl]
            work[c, h] = dict(
                rows=rows, vl=slice(h * GLA_DV, (h + 1) * GLA_DV),
                v=v_ref[rows, h * GLA_DV:(h + 1) * GLA_DV].astype(BF16),
                qt=(q_ref[rows, kl] * GLA_DK ** -0.5 * jnp.exp(b)).astype(BF16),
                kt=(k * jnp.exp(-b)).astype(BF16),
                kh=(k * jnp.exp(b_last - b)).astype(BF16),
                keep=jnp.exp(b_last))
    wks = [work[key] for key in sorted(work)]
    intras = [jnp.where(causal, _dot_nt(wk["qt"], wk["kt"]), 0.0).astype(BF16) for wk in wks]
    o_intras = [jnp.dot(intra, wk["v"], preferred_element_type=F32) for wk, intra in zip(wks, intras)]
    updates = [_dot_tn(wk["v"], wk["kh"]) for wk in wks]

    states = [state_ref[h] for h in range(GLA_HEADS)]
    starts = []
    for (c, h), wk, upd in zip(sorted(work), wks, updates):
        starts.append(states[h].astype(BF16))
        states[h] = states[h] * wk["keep"] + upd
    for h in range(GLA_HEADS):
        state_ref[h] = states[h]

    o_inters = [_dot_nt(wk["qt"], s0) for wk, s0 in zip(wks, starts)]
    for wk, o_intra, o_inter in zip(wks, o_intras, o_inters):
        o = o_intra + o_inter
        r = r_ref[wk["rows"], wk["vl"]]
        on = o * lax.rsqrt(jnp.mean(o * o, axis=-1, keepdims=True) + EPS) * onorm_ref[...]
        o_ref[wk["rows"], wk["vl"]] = (on * (r * jax.nn.sigmoid(r))).astype(o_ref.dtype)


def gated_linear_attention(p32, a_w2, a_b, o_norm):
    nt = SEQ // GLA_TC
    tok = lambda blk: (lambda b, t: (b * nt + t, blk))
    aw = jnp.pad(a_w2, ((0, LANE - GLA_RANK), (0, 0))).astype(BF16)
    tri = jnp.asarray(np.tril(np.ones((GLA_CHUNK, GLA_CHUNK))), BF16)
    return pl.pallas_call(
        _gla_kernel,
        grid=(BATCH, nt),
        in_specs=[pl.BlockSpec((GLA_TC, GLA_QK), tok(ODD_Q_COL // GLA_QK)),
                  pl.BlockSpec((GLA_TC, GLA_QK), tok(ODD_K_COL // GLA_QK)),
                  pl.BlockSpec((GLA_TC, GLA_V), tok(ODD_V_COL // GLA_V)),
                  pl.BlockSpec((GLA_TC, GLA_V), tok(ODD_R_COL // GLA_V)),
                  pl.BlockSpec((GLA_TC, LANE), tok(ODD_LR_COL // LANE)),
                  _resident((LANE, GLA_QK)), _resident((1, GLA_QK)), _resident((1, GLA_DV)),
                  _resident((GLA_CHUNK, GLA_CHUNK))],
        out_specs=pl.BlockSpec((GLA_TC, GLA_V), tok(0)),
        out_shape=jax.ShapeDtypeStruct((TOKENS, GLA_V), BF16),
        scratch_shapes=[pltpu.VMEM((GLA_HEADS, GLA_DV, GLA_DK), F32)],
        compiler_params=pltpu.CompilerParams(dimension_semantics=("arbitrary", "arbitrary"),
                                             vmem_limit_bytes=VMEM_LIMIT),
        name="gated_linear_attention",
    )(p32, p32, p32, p32, p32, aw, a_b.reshape(1, GLA_QK), o_norm.reshape(1, GLA_DV), tri)


def _pad_cols(w, n):
    return jnp.pad(w, ((0, 0), (0, n - w.shape[1])))


def kernel(x, norm_gains, ffn_w_in, ffn_conv_w, ffn_conv_b, ffn_w_out, hyb_w_in, hyb_w_out, gdn_conv_w, gdn_A_log,
           gdn_dt_bias, gdn_o_norm, nsa_cmp_pe, nsa_cmp_w1, nsa_cmp_b1, nsa_cmp_w2, gla_w_in, gla_a_w2, gla_a_b,
           gla_o_norm, gla_w_out):
    x2 = x.reshape(TOKENS, D_MODEL)
    ffn_w_in_bf = ffn_w_in.astype(BF16)
    ffn_w_out_bf = ffn_w_out.astype(BF16)
    for layer in range(DEPTH):
        if layer % 2 == 0:
            e = layer // 2
            p32, p16 = norm_proj(x2, norm_gains[layer, 0], even_proj_weight(hyb_w_in[e]),
                                 ((EVEN_F32_W, F32), (EVEN_BF16_W, BF16)))
            oa, ob = even_mixer_core(p32, p16, gdn_conv_w[e], gdn_A_log[e], gdn_dt_bias[e], gdn_o_norm[e],
                                     nsa_cmp_pe[e], nsa_cmp_w1[e], nsa_cmp_b1[e], nsa_cmp_w2[e])
            halves = (oa, 0, ob, 0)
            w_mix = hyb_w_out[e]
        else:
            o_ = layer // 2
            w = _pad_cols(gla_w_in[o_], ODD_F32_W).astype(BF16)
            (p32,) = norm_proj(x2, norm_gains[layer, 0], w, ((ODD_F32_W, F32),))
            o = gated_linear_attention(p32, gla_a_w2[o_], gla_a_b[o_], gla_o_norm[o_])
            halves = (o, 0, o, 1)
            w_mix = gla_w_out[o_]
        x2 = mix_out_conv_ffn(*halves, x2, norm_gains[layer, 1], w_mix.astype(BF16), norm_gains[layer, 2],
                              layer, ffn_w_in_bf, ffn_conv_w, ffn_conv_b, ffn_w_out_bf, norm_gains[layer, 3])
    return x2.reshape(BATCH, SEQ, D_MODEL)
```

```python
import jax
import jax.numpy as jnp
import numpy as np
from jax import lax
from jax.experimental import pallas as pl
from jax.experimental.pallas import tpu as pltpu

D_MODEL = 1024
BATCH = 8
SEQ = 2048
DEPTH = 4
TOKENS = BATCH * SEQ

EPS = 1e-6
NEG = -1e30
FORCE = 1e4

GDN_DK = 128
GDN_DV = 128
GDN_HEADS = 4
GDN_QK = GDN_HEADS * GDN_DK
GDN_V = GDN_HEADS * GDN_DV
GDN_CONV = 4
GDN_CHUNK = 64

NSA_DH = 64
NSA_HEADS = 8
NSA_GROUPS = 2
NSA_Q = NSA_HEADS * NSA_DH
NSA_KV = NSA_GROUPS * NSA_DH
CMP_LEN = 32
CMP_STRIDE = 16
CMP_HIDDEN = 2 * NSA_DH
SLC_LEN = 64
SLC_TOPK = 8
WINDOW = 512

GLA_HEADS = 4
GLA_DK = 128
GLA_DV = 256
GLA_QK = GLA_HEADS * GLA_DK
GLA_V = GLA_HEADS * GLA_DV
GLA_RANK = 16
GLA_TAU = 16.0
GLA_CHUNK = 64

D_FF = 2816
FFN_CONV = 3

EVEN_SPLITS = (2 * GDN_QK + GDN_V, GDN_HEADS, GDN_HEADS, GDN_V, NSA_Q, 6 * NSA_KV, 3 * NSA_HEADS)
ODD_SPLITS = (GLA_QK, GLA_QK, GLA_V, GLA_V, GLA_RANK)

LANE = 128
VMEM_LIMIT = 56 * 1024 * 1024

F32 = jnp.float32
BF16 = jnp.bfloat16


def _resident(shape):
    nd = len(shape)
    return pl.BlockSpec(shape, lambda *_: (0,) * nd, pipeline_mode=pl.Buffered(1))


def _resident_layer(shape, layer):
    nd = len(shape)
    return pl.BlockSpec((None,) + tuple(shape), lambda *_: (layer,) + (0,) * nd, pipeline_mode=pl.Buffered(1))


def _rms(x, g):
    ms = jnp.mean(x * x, axis=-1, keepdims=True)
    return x * lax.rsqrt(ms + EPS) * g


PROJ_TM = 512
PROJ_CH = 512


def _proj_kernel(x_ref, g_ref, w_ref, *o_refs):
    h = _rms(x_ref[...], g_ref[...]).astype(BF16)
    col = 0
    for o_ref in o_refs:
        n = o_ref.shape[1]
        for c0 in range(0, n, PROJ_CH):
            cw = min(PROJ_CH, n - c0)
            y = jnp.dot(h, w_ref[:, col + c0:col + c0 + cw], preferred_element_type=F32)
            o_ref[:, c0:c0 + cw] = y.astype(o_ref.dtype)
        col += n


def norm_proj(x2, gain, w, sections):
    T, D = x2.shape
    N = w.shape[1]
    assert N == sum(n for n, _ in sections) and all(n % LANE == 0 for n, _ in sections)
    return pl.pallas_call(
        _proj_kernel,
        grid=(T // PROJ_TM,),
        in_specs=[pl.BlockSpec((PROJ_TM, D), lambda i: (i, 0)), _resident((1, D)), _resident((D, N))],
        out_specs=[pl.BlockSpec((PROJ_TM, n), lambda i: (i, 0)) for n, _ in sections],
        out_shape=[jax.ShapeDtypeStruct((T, n), dt) for n, dt in sections],
        compiler_params=pltpu.CompilerParams(dimension_semantics=("arbitrary",), vmem_limit_bytes=VMEM_LIMIT),
        name="norm_proj",
    )(x2, gain.reshape(1, D), w)


FFN_TM = 512
FFN_TAIL = 8
FFN_CH = 256
MIX_HALF = D_MODEL // 2
assert D_FF % FFN_CH == 0


def _ffn_kernel(oa_ref, ob_ref, x_ref, g_mix_ref, w_mix_ref, g_in_ref, w_in_ref, cw_ref, cb_ref, w_out_ref, g_out_ref,
                y_ref, tail_ref, act_ref):
    t = pl.program_id(1)

    @pl.when(t == 0)
    def _():
        tail_ref[...] = jnp.zeros(tail_ref.shape, F32)

    mix = (jnp.dot(oa_ref[...], w_mix_ref[0:MIX_HALF, :], preferred_element_type=F32)
           + jnp.dot(ob_ref[...], w_mix_ref[MIX_HALF:, :], preferred_element_type=F32))
    x = x_ref[...] + _rms(mix, g_mix_ref[...])
    h = _rms(x, g_in_ref[...]).astype(BF16)

    def conv_half(c0):
        j = c0 // FFN_CH
        u = jnp.dot(h, w_in_ref[:, c0:c0 + FFN_CH], preferred_element_type=F32)
        ext = jnp.concatenate([tail_ref[j], u], axis=0)
        tail_ref[j] = u[FFN_TM - FFN_TAIL:]
        w = cw_ref[:, c0:c0 + FFN_CH]
        y = pltpu.roll(ext, 2, 0) * w[0:1] + pltpu.roll(ext, 1, 0) * w[1:2] + ext * w[2:3] + cb_ref[:, c0:c0 + FFN_CH]
        return y[FFN_TAIL:]

    for c0 in range(0, D_FF, FFN_CH):
        half_gate = (0.5 * conv_half(c0)).astype(BF16)
        val = conv_half(D_FF + c0).astype(BF16)
        silu = half_gate + half_gate * jnp.tanh(half_gate)
        act_ref[:, c0:c0 + FFN_CH] = silu * val
    down = jnp.dot(act_ref[...], w_out_ref[...], preferred_element_type=F32)
    y_ref[...] = x + _rms(down, g_out_ref[...])


def mix_out_conv_ffn(oa, a_blk, ob, b_blk, x2, g_mix, w_mix, g_in, layer, w_in, conv_w, conv_b, w_out, g_out):
    T, D = x2.shape
    nt = SEQ // FFN_TM
    tok = lambda blk: (lambda b, t: (b * nt + t, blk))
    return pl.pallas_call(
        _ffn_kernel,
        grid=(BATCH, nt),
        in_specs=[pl.BlockSpec((FFN_TM, MIX_HALF), tok(a_blk)), pl.BlockSpec((FFN_TM, MIX_HALF), tok(b_blk)),
                  pl.BlockSpec((FFN_TM, D), tok(0)), _resident((1, D)), _resident((D, D)),
                  _resident((1, D)), _resident_layer((D, 2 * D_FF), layer),
                  _resident_layer((FFN_CONV, 2 * D_FF), layer), _resident_layer((1, 2 * D_FF), layer),
                  _resident_layer((D_FF, D), layer), _resident((1, D))],
        out_specs=pl.BlockSpec((FFN_TM, D), tok(0)),
        out_shape=jax.ShapeDtypeStruct((T, D), F32),
        scratch_shapes=[pltpu.VMEM((2 * D_FF // FFN_CH, FFN_TAIL, FFN_CH), F32), pltpu.VMEM((FFN_TM, D_FF), BF16)],
        compiler_params=pltpu.CompilerParams(dimension_semantics=("arbitrary", "arbitrary"),
                                             vmem_limit_bytes=VMEM_LIMIT),
        name="mix_out_conv_ffn",
    )(oa, ob, x2, g_mix.reshape(1, D), w_mix, g_in.reshape(1, D), w_in, conv_w, conv_b.reshape(DEPTH, 1, -1), w_out,
      g_out.reshape(1, D))


def split_last(x, sizes):
    idx = np.cumsum(sizes)[:-1].tolist()
    return jnp.split(x, idx, axis=-1)


def _dot_nt(a, b):
    return lax.dot_general(a, b, (((1,), (1,)), ((), ())), preferred_element_type=F32)


def _dot_tn(a, b):
    return lax.dot_general(a, b, (((0,), (0,)), ((), ())), preferred_element_type=F32)


def _split3(a):
    hi = a.astype(BF16)
    r = a - hi.astype(F32)
    mid = r.astype(BF16)
    return hi, mid, (r - mid.astype(F32)).astype(BF16)


def _dot_x3(a, b):
    a_hi, a_lo, _ = _split3(a)
    b_hi, b_lo, _ = _split3(b)
    d = lambda p, q: jnp.dot(p, q, preferred_element_type=F32)
    return d(a_hi, b_hi) + (d(a_lo, b_hi) + d(a_hi, b_lo))


def _cumsum_rows(tri, g):
    return sum(jnp.dot(tri, t, preferred_element_type=F32) for t in reversed(_split3(g)))


def _unit_lower_inverses(mats):
    c = mats[0].shape[0]
    eye = jnp.where(lax.broadcasted_iota(jnp.int32, (c, c), 0) == lax.broadcasted_iota(jnp.int32, (c, c), 1), 1.0, 0.0)
    dot = lambda a, b: jnp.dot(a.astype(BF16), b.astype(BF16), preferred_element_type=F32)
    ps = [-a for a in mats]
    ts = [eye + p for p in ps]
    k = 2
    while k < c:
        ps = [dot(p, p) for p in ps]
        ts = [t + dot(t, p) for t, p in zip(ts, ps)]
        k *= 2
    res = [(eye - t) - _dot_x3(a, t) for a, t in zip(mats, ts)]
    return [t + dot(t, r) for t, r in zip(ts, res)]


GDN_TC = 512
GDN_HALO = 8
GDN_SMALL_BETA = 0
GDN_SMALL_DECAY = GDN_HEADS
NSA_SMALL_GATE = 2 * GDN_HEADS


def _gdn_kernel(x_ref, gate_ref, small_ref, cw_ref, alog_ref, dtb_ref, onorm_ref, tri_ref, o_ref, xext_ref, state_ref):
    t = pl.program_id(1)
    C = GDN_CHUNK

    @pl.when(t == 0)
    def _():
        xext_ref[0:GDN_HALO, :] = jnp.zeros((GDN_HALO, xext_ref.shape[1]), F32)
        state_ref[...] = jnp.zeros(state_ref.shape, F32)

    xext_ref[GDN_HALO:, :] = x_ref[...]
    small = small_ref[...]
    beta_all = jax.nn.sigmoid(small)
    z = small + dtb_ref[...]
    softplus = jnp.maximum(z, 0.0) + jnp.log(1.0 + jnp.exp(-jnp.abs(z)))
    g_all = -jnp.exp(alog_ref[...]) * softplus
    tri = tri_ref[...]
    row = lax.broadcasted_iota(jnp.int32, (C, C), 0)
    col = lax.broadcasted_iota(jnp.int32, (C, C), 1)

    def conv_silu(c0, width):
        w = cw_ref[:, c0:c0 + width]
        y = xext_ref[GDN_HALO:, c0:c0 + width] * w[GDN_CONV - 1:GDN_CONV]
        for s in range(1, GDN_CONV):
            y = y + xext_ref[GDN_HALO - s:GDN_HALO - s + GDN_TC, c0:c0 + width] * w[GDN_CONV - 1 - s:GDN_CONV - s]
        return y * jax.nn.sigmoid(y)

    n_chunks = GDN_TC // C
    gsums = [_cumsum_rows(tri, g_all[c * C:(c + 1) * C]) for c in range(n_chunks)]
    gsum_ts = [jnp.transpose(gs) for gs in gsums]

    work = {}
    for h in range(GDN_HEADS):
        q_all = conv_silu(h * GDN_DK, GDN_DK)
        k_all = conv_silu(GDN_QK + h * GDN_DK, GDN_DK)
        v_all = conv_silu(2 * GDN_QK + h * GDN_DV, GDN_DV)
        q_all = q_all * (lax.rsqrt(jnp.sum(q_all * q_all, axis=-1, keepdims=True) + EPS) * GDN_DK ** -0.5)
        k_all = k_all * lax.rsqrt(jnp.sum(k_all * k_all, axis=-1, keepdims=True) + EPS)
        for c in range(n_chunks):
            rows = slice(c * C, (c + 1) * C)
            q, k, v = q_all[rows], k_all[rows], v_all[rows]
            beta = beta_all[rows, GDN_SMALL_BETA + h:GDN_SMALL_BETA + h + 1]
            gcol = gsums[c][:, GDN_SMALL_DECAY + h:GDN_SMALL_DECAY + h + 1]
            grow = gsum_ts[c][GDN_SMALL_DECAY + h:GDN_SMALL_DECAY + h + 1, :]
            glast = gcol[C - 1:C]
            decay = jnp.exp(jnp.where(row >= col, gcol - grow, -jnp.inf))
            kb = k * beta
            kbf = k.astype(BF16)
            eg = jnp.exp(gcol)
            work[h, c] = dict(
                a_kk=jnp.where(row > col, _dot_nt(kb.astype(BF16), kbf) * decay, 0.0),
                a_qk=(_dot_nt(q.astype(BF16), kbf) * decay).astype(BF16),
                rhs=jnp.concatenate([v * beta, kb * eg], axis=-1),
                qg=q * eg,
                kd_t=jnp.transpose(k * jnp.exp(glast - gcol)).astype(BF16),
                keep=jnp.exp(glast))
    keys = sorted(work)
    inverses = _unit_lower_inverses([work[key]["a_kk"] for key in keys])
    wks = [work[key] for key in keys]
    dot = lambda a, b: jnp.dot(a, b, preferred_element_type=F32)
    sols = [dot(inv.astype(BF16), wk["rhs"].astype(BF16)) for wk, inv in zip(wks, inverses)]
    us = [sol[:, :GDN_DV].astype(BF16) for sol in sols]
    ws = [sol[:, GDN_DV:].astype(BF16) for sol in sols]
    for wk, s_in in zip(wks, [dot(wk["kd_t"], u) for wk, u in zip(wks, us)]):
        wk["s_in"] = s_in
    for wk, s_mix in zip(wks, [dot(wk["kd_t"], w) for wk, w in zip(wks, ws)]):
        wk["s_mix"] = s_mix.astype(BF16)
    for wk, o_in in zip(wks, [dot(wk["a_qk"], u) for wk, u in zip(wks, us)]):
        wk["o_in"] = o_in
    for wk, aw in zip(wks, [dot(wk["a_qk"], w) for wk, w in zip(wks, ws)]):
        wk["o_mix"] = (wk["qg"] - aw).astype(BF16)

    states = [state_ref[h] for h in range(GDN_HEADS)]
    for c in range(n_chunks):
        rows = slice(c * C, (c + 1) * C)
        for h in range(GDN_HEADS):
            wk = work[h, c]
            sb = states[h].astype(BF16)
            o = wk["o_in"] + jnp.dot(wk["o_mix"], sb, preferred_element_type=F32)
            states[h] = (states[h] * wk["keep"] + wk["s_in"]) - jnp.dot(wk["s_mix"], sb, preferred_element_type=F32)
            gate = gate_ref[rows, h * GDN_DV:(h + 1) * GDN_DV]
            on = o * lax.rsqrt(jnp.mean(o * o, axis=-1, keepdims=True) + EPS) * onorm_ref[...]
            o_ref[rows, h * GDN_DV:(h + 1) * GDN_DV] = (on * (gate * jax.nn.sigmoid(gate))).astype(o_ref.dtype)
    for h in range(GDN_HEADS):
        state_ref[h] = states[h]
    xext_ref[0:GDN_HALO, :] = xext_ref[GDN_TC:GDN_TC + GDN_HALO, :]


def gated_delta_net(p32, conv_w, a_log, dt_bias, o_norm, qkv_col, gate_col, small_col):
    nt = SEQ // GDN_TC
    width = 2 * GDN_QK + GDN_V
    lane_vec = lambda a, off: jnp.zeros((1, LANE), F32).at[0, off:off + GDN_HEADS].set(a.astype(F32))
    tri = jnp.asarray(np.tril(np.ones((GDN_CHUNK, GDN_CHUNK))), BF16)
    tok = lambda blk: (lambda b, t: (b * nt + t, blk))
    return pl.pallas_call(
        _gdn_kernel,
        grid=(BATCH, nt),
        in_specs=[pl.BlockSpec((GDN_TC, width), tok(qkv_col // width)),
                  pl.BlockSpec((GDN_TC, GDN_V), tok(gate_col // GDN_V)),
                  pl.BlockSpec((GDN_TC, LANE), tok(small_col // LANE)),
                  _resident((GDN_CONV, width)), _resident((1, LANE)), _resident((1, LANE)), _resident((1, GDN_DV)),
                  _resident((GDN_CHUNK, GDN_CHUNK))],
        out_specs=pl.BlockSpec((GDN_TC, GDN_V), tok(0)),
        out_shape=jax.ShapeDtypeStruct((TOKENS, GDN_V), BF16),
        scratch_shapes=[pltpu.VMEM((GDN_HALO + GDN_TC, width), F32), pltpu.VMEM((GDN_HEADS, GDN_DK, GDN_DV), F32)],
        compiler_params=pltpu.CompilerParams(dimension_semantics=("arbitrary", "arbitrary"),
                                             vmem_limit_bytes=VMEM_LIMIT),
        name="gated_delta_net",
    )(p32, p32, p32, conv_w, lane_vec(a_log, GDN_SMALL_DECAY), lane_vec(dt_bias, GDN_SMALL_DECAY),
      o_norm.reshape(1, GDN_DV), tri)


N_SLC = SEQ // SLC_LEN
N_CMP = SEQ // CMP_STRIDE
NSA_HG = NSA_HEADS // NSA_GROUPS
NSA_TQ = 128
SEL_STEP = 256
WIN_W = WINDOW + NSA_TQ
NSA_SCALE = NSA_DH ** -0.5


def _cmp_kernel(kc_ref, vc_ref, pe_ref, w1_ref, b1_ref, w2_ref, o_ref, strip_ref):
    half = CMP_STRIDE * NSA_DH
    for kv, x_ref in enumerate((kc_ref, vc_ref)):
        for p in range(CMP_STRIDE):
            xp = x_ref[pl.ds(p, N_CMP, stride=CMP_STRIDE), :]
            for g in range(NSA_GROUPS):
                strip_ref[kv * NSA_GROUPS + g, :, p * NSA_DH:(p + 1) * NSA_DH] = xp[:, g * NSA_DH:(g + 1) * NSA_DH]
    for kv in range(2):
        for g in range(NSA_GROUPS):
            rows = strip_ref[kv * NSA_GROUPS + g]
            lo = (rows + pe_ref[kv, :, :half]).astype(BF16)
            hi = (rows + pe_ref[kv, :, half:]).astype(BF16)
            a = jnp.dot(lo, w1_ref[kv, :half, :], preferred_element_type=F32)
            b = jnp.dot(hi, w1_ref[kv, half:, :], preferred_element_type=F32)
            hid = a + pltpu.roll(b, N_CMP - 1, 0) + b1_ref[kv]
            act = (hid * jax.nn.sigmoid(hid)).astype(BF16)
            o_ref[0, kv * NSA_GROUPS + g] = jnp.dot(act, w2_ref[kv], preferred_element_type=F32).astype(BF16)


def nsa_compress(p32, cmp_col, pe, w1, b1, w2):
    n = 2 * NSA_GROUPS
    return pl.pallas_call(
        _cmp_kernel,
        grid=(BATCH,),
        in_specs=[pl.BlockSpec((SEQ, NSA_KV), lambda b: (b, cmp_col // NSA_KV)),
                  pl.BlockSpec((SEQ, NSA_KV), lambda b: (b, cmp_col // NSA_KV + 1)),
                  _resident((2, 1, CMP_LEN * NSA_DH)), _resident((2, CMP_LEN * NSA_DH, CMP_HIDDEN)),
                  _resident((2, 1, CMP_HIDDEN)), _resident((2, CMP_HIDDEN, NSA_DH))],
        out_specs=pl.BlockSpec((1, n, N_CMP, NSA_DH), lambda b: (b, 0, 0, 0)),
        out_shape=jax.ShapeDtypeStruct((BATCH, n, N_CMP, NSA_DH), BF16),
        scratch_shapes=[pltpu.VMEM((n, N_CMP, CMP_STRIDE * NSA_DH), F32)],
        compiler_params=pltpu.CompilerParams(dimension_semantics=("arbitrary",), vmem_limit_bytes=VMEM_LIMIT),
        name="nsa_compress",
    )(p32, p32, pe.reshape(2, 1, -1), w1.astype(BF16), b1.reshape(2, 1, -1), w2.astype(BF16))


V1_ROWS = NSA_DH + 16
NSA_PAIR = 2
RANK_ROWS = 8
NSA_AHEAD = 4
MASK_BIG = 2.0 ** 100


def _masked_attention_t(k, q_t, v1_t, bias_t):
    s = jnp.dot(k, q_t, preferred_element_type=F32) + jnp.concatenate([bias_t] * NSA_HG, axis=1)
    p = jnp.exp((s - jnp.max(s, axis=0, keepdims=True)).astype(BF16))
    acc = jnp.dot(v1_t, p, preferred_element_type=F32)
    return acc[:NSA_DH] / acc[NSA_DH:NSA_DH + 1]


def _transposed_with_ones_row(eye, v):
    v_t = _dot_nt(eye, v).astype(v.dtype)
    row = lax.broadcasted_iota(jnp.int32, (V1_ROWS - NSA_DH, v.shape[0]), 0)
    return jnp.concatenate([v_t, jnp.where(row == 0, 1.0, 0.0).astype(v.dtype)], axis=0)


def _nsa_kernel(q_ref, ks_ref, vs_ref, kw_ref, vw_ref, cmp_ref, gate_ref, ovl_ref, eye_ref, expand_ref,
                o_ref, ksx_ref, qt_ref, ocmp_ref, gsel_ref, gwin_ref, vs1_ref, vw1_ref, vct_ref):
    i = pl.program_id(1)
    q0 = i * NSA_TQ
    eye = eye_ref[...]
    eye_dh = eye[0:NSA_DH, 0:NSA_DH]

    @pl.when(i == 0)
    def _():
        for g in range(NSA_GROUPS):
            lanes = slice(g * NSA_DH, (g + 1) * NSA_DH)
            vs1_ref[g] = _transposed_with_ones_row(eye_dh, vs_ref[:, lanes])
            vw1_ref[g] = _transposed_with_ones_row(eye_dh, vw_ref[:, lanes])
            vct_ref[g] = _dot_nt(eye_dh, cmp_ref[0, NSA_GROUPS + g]).astype(BF16)
            ksx_ref[g] = jnp.concatenate(
                [ks_ref[:, lanes], expand_ref[...], jnp.zeros((SEQ, LANE - NSA_DH - N_SLC), BF16)], axis=1)
            qt_ref[g, NSA_DH + N_SLC:, :] = jnp.zeros((LANE - NSA_DH - N_SLC, NSA_HG * NSA_TQ), BF16)

    gates = jax.nn.sigmoid(gate_ref[...])
    gates_t = jnp.transpose(gates)
    tok_l = q0 + lax.broadcasted_iota(jnp.int32, (1, NSA_TQ), 1)

    groups = range(NSA_GROUPS)
    scale = jnp.asarray(NSA_SCALE, BF16)
    for pair in range(NSA_HEADS // 2):
        pair_t = _dot_nt(eye, q_ref[:, pair * LANE:(pair + 1) * LANE] * scale).astype(BF16)
        for k in range(2):
            g, hk = divmod(2 * pair + k, NSA_HG)
            qt_ref[g, 0:NSA_DH, hk * NSA_TQ:(hk + 1) * NSA_TQ] = pair_t[k * NSA_DH:(k + 1) * NSA_DH]

    def gate_row(g, branch):
        lane = NSA_SMALL_GATE + branch * NSA_HEADS + g * NSA_HG
        return jnp.concatenate([gates_t[lane + hk:lane + hk + 1, :] for hk in range(NSA_HG)], axis=1)

    cmp_end = lax.broadcasted_iota(jnp.int32, (N_CMP, 1), 0) * CMP_STRIDE + (CMP_LEN - 1)
    cvalid = cmp_end <= jnp.concatenate([tok_l] * NSA_HG, axis=1)
    ss = [jnp.dot(cmp_ref[0, g], qt_ref[g, 0:NSA_DH, :], preferred_element_type=F32) for g in groups]
    sms = [jnp.where(cvalid, s, NEG) for s in ss]
    es = [jnp.exp(sm - jnp.max(sm, axis=0, keepdims=True)) for sm in sms]
    ps = [jnp.where(cvalid, e / jnp.sum(e, axis=0, keepdims=True), 0.0) for e in es]
    for g in groups:
        ocmp_ref[g] = gate_row(g, 0) * jnp.dot(vct_ref[g], ps[g].astype(BF16), preferred_element_type=F32)
        gsel_ref[g] = gate_row(g, 1)
        gwin_ref[g] = gate_row(g, 2)

    psums = [sum(p[:, hk * NSA_TQ:(hk + 1) * NSA_TQ] for hk in range(NSA_HG)) for p in ps]
    imps = [jnp.dot(ovl_ref[...], psum.astype(BF16), preferred_element_type=F32) for psum in psums]
    blk = lax.broadcasted_iota(jnp.int32, (N_SLC, NSA_TQ), 0)
    cur = tok_l // SLC_LEN
    bvalid = blk <= cur
    forced = (blk == 0) | (blk == cur) | (blk == cur - 1)
    scores_ = [jnp.where(bvalid, imp + jnp.where(forced, FORCE, 0.0), NEG) for imp in imps]
    tiles = [slice(r, r + RANK_ROWS) for r in range(0, N_SLC, RANK_ROWS)]
    rank_tiles = [[jnp.zeros((RANK_ROWS, NSA_TQ), F32) for _ in tiles] for _ in groups]
    for j in range(N_SLC):
        for g in groups:
            sj = scores_[g][j:j + 1, :]
            for n, rows in enumerate(tiles):
                s_tile = scores_[g][rows]
                if j < rows.start:
                    before = jnp.where(sj >= s_tile, 1.0, 0.0)
                elif j >= rows.stop - 1:
                    before = jnp.where(sj > s_tile, 1.0, 0.0)
                else:
                    tie = jnp.where((sj == s_tile) & (blk[rows] > j), 1.0, 0.0)
                    before = jnp.where(sj > s_tile, 1.0, tie)
                rank_tiles[g][n] = rank_tiles[g][n] + before
    ranks = [jnp.concatenate(rank_tiles[g], axis=0) for g in groups]
    for g in groups:
        off = jnp.where((ranks[g] < SLC_TOPK) & bvalid, 0.0, -MASK_BIG).astype(BF16)
        qt_ref[g, NSA_DH:NSA_DH + N_SLC, :] = jnp.concatenate([off] * NSA_HG, axis=1)

    for n in range(SEQ // SEL_STEP):
        @pl.when(i // (SEL_STEP // NSA_TQ) == n)
        def _():
            w = (n + 1) * SEL_STEP
            w0 = pl.multiple_of(jnp.maximum(q0 + NSA_TQ - WIN_W, 0), NSA_TQ)
            r = pl.ds(w0, WIN_W)

            def bias(pos, lo_excl):
                ok = (pos <= tok_l) if lo_excl is None else ((pos <= tok_l) & (pos > lo_excl))
                return jnp.concatenate([jnp.where(ok, 0.0, NEG)] * NSA_PAIR, axis=1)

            sel_causal = bias((w - SEL_STEP) + lax.broadcasted_iota(jnp.int32, (SEL_STEP, 1), 0), None)
            wpos = w0 + lax.broadcasted_iota(jnp.int32, (WIN_W, 1), 0)
            if n < WINDOW // SEL_STEP:
                win_edges = [(0, WIN_W, bias(wpos, tok_l - WINDOW))]
            else:
                win_edges = [(0, NSA_TQ, bias(wpos[:NSA_TQ], tok_l - WINDOW)),
                             (WIN_W - NSA_TQ, WIN_W, bias(wpos[WIN_W - NSA_TQ:], None))]

            def add_bias(s, edges):
                pieces, at = [], 0
                for lo, hi, b in edges:
                    pieces += [s[at:lo], s[lo:hi] + b]
                    at = hi
                return jnp.concatenate([p for p in pieces + [s[at:]] if p.shape[0]], axis=0)

            problems = []
            for g in range(NSA_GROUPS):
                lanes = slice(g * NSA_DH, (g + 1) * NSA_DH)
                for half in range(NSA_HG // NSA_PAIR):
                    cols = slice(half * NSA_PAIR * NSA_TQ, (half + 1) * NSA_PAIR * NSA_TQ)
                    problems.append((g, cols, 0, lambda g=g: ksx_ref[g, 0:w, :], lambda g=g: vs1_ref[g, :, 0:w],
                                     lambda g=g, cols=cols: qt_ref[g, :, cols], [(w - SEL_STEP, w, sel_causal)]))
                    problems.append((g, cols, 1, lambda lanes=lanes: kw_ref[r, lanes], lambda g=g: vw1_ref[g, :, r],
                                     lambda g=g, cols=cols: qt_ref[g, 0:NSA_DH, cols], win_edges))

            def scores(prob):
                _, _, _, k, _, q_t, edges = prob
                return add_bias(jnp.dot(k(), q_t(), preferred_element_type=F32), edges)

            def probs(s):
                return jnp.exp((s - jnp.max(s, axis=0, keepdims=True)).astype(BF16))

            def values(prob, p):
                acc = jnp.dot(prob[4](), p, preferred_element_type=F32)
                return acc[:NSA_DH] / acc[NSA_DH:NSA_DH + 1]

            problems = problems[1:2] + problems[0:1] + problems[2:]
            outs = {}
            ahead = [scores(prob) for prob in problems[:NSA_AHEAD]]
            for idx, prob in enumerate(problems):
                s_cur = ahead.pop(0)
                if idx + NSA_AHEAD < len(problems):
                    ahead.append(scores(problems[idx + NSA_AHEAD]))
                outs[prob[0], prob[1].start, prob[2]] = values(prob, probs(s_cur))

            for g in range(NSA_GROUPS):
                for half in range(NSA_HG // NSA_PAIR):
                    c0 = half * NSA_PAIR * NSA_TQ
                    cols = slice(c0, c0 + NSA_PAIR * NSA_TQ)
                    o_t = (ocmp_ref[g, :, cols] + gsel_ref[g, :, cols] * outs[g, c0, 0]
                           + gwin_ref[g, :, cols] * outs[g, c0, 1])
                    for k in range(NSA_PAIR):
                        h = g * NSA_HG + half * NSA_PAIR + k
                        o = jnp.transpose(o_t[:, k * NSA_TQ:(k + 1) * NSA_TQ])
                        o_ref[:, h * NSA_DH:(h + 1) * NSA_DH] = o.astype(o_ref.dtype)


def _nsa_constants():
    c0 = np.arange(N_CMP)[:, None] * CMP_STRIDE
    s0 = np.arange(N_SLC)[None, :] * SLC_LEN
    overlap = np.clip(np.minimum(c0 + CMP_LEN, s0 + SLC_LEN) - np.maximum(c0, s0), 0, None) / CMP_STRIDE
    overlap[N_CMP - 1] = 0.0
    expand_t = (np.arange(SEQ)[:, None] // SLC_LEN == np.arange(N_SLC)[None, :]).astype(np.float32)
    return jnp.asarray(overlap.T, BF16), jnp.asarray(np.eye(LANE), BF16), jnp.asarray(expand_t, BF16)


def nsa_attention(p16, cmp_kv, p32, q_col, kv_col, small_col):
    nq = SEQ // NSA_TQ
    ovl_t, eye, expand = _nsa_constants()
    kv_spec = lambda n: pl.BlockSpec((SEQ, NSA_KV), lambda b, i: (b, kv_col // NSA_KV + n))
    return pl.pallas_call(
        _nsa_kernel,
        grid=(BATCH, nq),
        in_specs=[pl.BlockSpec((NSA_TQ, NSA_Q), lambda b, i: (b * nq + i, q_col // NSA_Q)),
                  kv_spec(0), kv_spec(1), kv_spec(2), kv_spec(3),
                  pl.BlockSpec((1, 2 * NSA_GROUPS, N_CMP, NSA_DH), lambda b, i: (b, 0, 0, 0)),
                  pl.BlockSpec((NSA_TQ, LANE), lambda b, i: (b * nq + i, small_col // LANE)),
                  _resident(ovl_t.shape), _resident(eye.shape), _resident(expand.shape)],
        out_specs=pl.BlockSpec((NSA_TQ, NSA_Q), lambda b, i: (b * nq + i, 0)),
        out_shape=jax.ShapeDtypeStruct((TOKENS, NSA_Q), BF16),
        scratch_shapes=[pltpu.VMEM((NSA_GROUPS, SEQ, LANE), BF16),
                        pltpu.VMEM((NSA_GROUPS, LANE, NSA_HG * NSA_TQ), BF16),
                        pltpu.VMEM((NSA_GROUPS, NSA_DH, NSA_HG * NSA_TQ), F32)]
        + [pltpu.VMEM((NSA_GROUPS, 1, NSA_HG * NSA_TQ), F32)] * 2
        + [pltpu.VMEM((NSA_GROUPS, V1_ROWS, SEQ), BF16)] * 2
        + [pltpu.VMEM((NSA_GROUPS, NSA_DH, N_CMP), BF16)],
        compiler_params=pltpu.CompilerParams(dimension_semantics=("arbitrary", "arbitrary"),
                                             vmem_limit_bytes=VMEM_LIMIT),
        name="nsa_attention",
    )(p16, p16, p16, p16, p16, cmp_kv, p32, ovl_t, eye, expand)


EVEN_QKV_COL = 0
EVEN_GATE_COL = 2 * GDN_QK + GDN_V
EVEN_CMP_COL = EVEN_GATE_COL + GDN_V
EVEN_SMALL_COL = EVEN_CMP_COL + 2 * NSA_KV
EVEN_F32_W = EVEN_SMALL_COL + LANE
EVEN_Q_COL = 0
EVEN_KV_COL = NSA_Q
EVEN_BF16_W = NSA_Q + 4 * NSA_KV
assert NSA_SMALL_GATE + 3 * NSA_HEADS <= LANE


def even_proj_weight(w_in):
    qkv_a, beta, dec, gate_a, q_b, kv_b, gate_b = split_last(w_in, EVEN_SPLITS)
    pad = jnp.zeros((w_in.shape[0], LANE - 2 * GDN_HEADS - 3 * NSA_HEADS), w_in.dtype)
    return jnp.concatenate([qkv_a, gate_a, kv_b[:, :2 * NSA_KV], beta, dec, gate_b, pad, q_b, kv_b[:, 2 * NSA_KV:]],
                           axis=1).astype(BF16)


def even_mixer_core(p32, p16, conv_w, A_log, dt_bias, o_norm, cmp_pe, cmp_w1, cmp_b1, cmp_w2):
    oa = gated_delta_net(p32, conv_w, A_log, dt_bias, o_norm, EVEN_QKV_COL, EVEN_GATE_COL, EVEN_SMALL_COL)
    cmp_kv = nsa_compress(p32, EVEN_CMP_COL, cmp_pe, cmp_w1, cmp_b1, cmp_w2)
    ob = nsa_attention(p16, cmp_kv, p32, EVEN_Q_COL, EVEN_KV_COL, EVEN_SMALL_COL)
    return oa, ob


GLA_TC = 512
ODD_Q_COL = 0
ODD_K_COL = GLA_QK
ODD_V_COL = 2 * GLA_QK
ODD_R_COL = ODD_V_COL + GLA_V
ODD_LR_COL = ODD_R_COL + GLA_V
ODD_F32_W = ODD_LR_COL + LANE


def _gla_kernel(q_ref, k_ref, v_ref, r_ref, lr_ref, aw_ref, ab_ref, onorm_ref, tri_ref, o_ref, state_ref):
    t = pl.program_id(1)
    C = GLA_CHUNK

    @pl.when(t == 0)
    def _():
        state_ref[...] = jnp.zeros(state_ref.shape, F32)

    z = jnp.dot(lr_ref[...].astype(BF16), aw_ref[...], preferred_element_type=F32) + ab_ref[...]
    log_a = (jnp.minimum(z, 0.0) - jnp.log(1.0 + jnp.exp(-jnp.abs(z)))) * (1.0 / GLA_TAU)
    tri = tri_ref[...]
    causal = lax.broadcasted_iota(jnp.int32, (C, C), 0) >= lax.broadcasted_iota(jnp.int32, (C, C), 1)

    n_chunks = GLA_TC // C
    b_alls = [_cumsum_rows(tri, log_a[c * C:(c + 1) * C]) for c in range(n_chunks)]
    work = {}
    for c in range(n_chunks):
        rows = slice(c * C, (c + 1) * C)
        for h in range(GLA_HEADS):
            kl = slice(h * GLA_DK, (h + 1) * GLA_DK)
            b = b_alls[c][:, kl]
            b_last = b[C - 1:C]
            k = k_ref[rows, kl]
            work[c, h] = dict(
                rows=rows, vl=slice(h * GLA_DV, (h + 1) * GLA_DV),
                v=v_ref[rows, h * GLA_DV:(h + 1) * GLA_DV].astype(BF16),
                qt=(q_ref[rows, kl] * GLA_DK ** -0.5 * jnp.exp(b)).astype(BF16),
                kt=(k * jnp.exp(-b)).astype(BF16),
                kh=(k * jnp.exp(b_last - b)).astype(BF16),
                keep=jnp.exp(b_last))
    wks = [work[key] for key in sorted(work)]
    intras = [jnp.where(causal, _dot_nt(wk["qt"], wk["kt"]), 0.0).astype(BF16) for wk in wks]
    o_intras = [jnp.dot(intra, wk["v"], preferred_element_type=F32) for wk, intra in zip(wks, intras)]
    updates = [_dot_tn(wk["v"], wk["kh"]) for wk in wks]

    states = [state_ref[h] for h in range(GLA_HEADS)]
    starts = []
    for (c, h), wk, upd in zip(sorted(work), wks, updates):
        starts.append(states[h].astype(BF16))
        states[h] = states[h] * wk["keep"] + upd
    for h in range(GLA_HEADS):
        state_ref[h] = states[h]

    o_inters = [_dot_nt(wk["qt"], s0) for wk, s0 in zip(wks, starts)]
    for wk, o_intra, o_inter in zip(wks, o_intras, o_inters):
        o = o_intra + o_inter
        r = r_ref[wk["rows"], wk["vl"]]
        on = o * lax.rsqrt(jnp.mean(o * o, axis=-1, keepdims=True) + EPS) * onorm_ref[...]
        o_ref[wk["rows"], wk["vl"]] = (on * (r * jax.nn.sigmoid(r))).astype(o_ref.dtype)


def gated_linear_attention(p32, a_w2, a_b, o_norm):
    nt = SEQ // GLA_TC
    tok = lambda blk: (lambda b, t: (b * nt + t, blk))
    aw = jnp.pad(a_w2, ((0, LANE - GLA_RANK), (0, 0))).astype(BF16)
    tri = jnp.asarray(np.tril(np.ones((GLA_CHUNK, GLA_CHUNK))), BF16)
    return pl.pallas_call(
        _gla_kernel,
        grid=(BATCH, nt),
        in_specs=[pl.BlockSpec((GLA_TC, GLA_QK), tok(ODD_Q_COL // GLA_QK)),
                  pl.BlockSpec((GLA_TC, GLA_QK), tok(ODD_K_COL // GLA_QK)),
                  pl.BlockSpec((GLA_TC, GLA_V), tok(ODD_V_COL // GLA_V)),
                  pl.BlockSpec((GLA_TC, GLA_V), tok(ODD_R_COL // GLA_V)),
                  pl.BlockSpec((GLA_TC, LANE), tok(ODD_LR_COL // LANE)),
                  _resident((LANE, GLA_QK)), _resident((1, GLA_QK)), _resident((1, GLA_DV)),
                  _resident((GLA_CHUNK, GLA_CHUNK))],
        out_specs=pl.BlockSpec((GLA_TC, GLA_V), tok(0)),
        out_shape=jax.ShapeDtypeStruct((TOKENS, GLA_V), BF16),
        scratch_shapes=[pltpu.VMEM((GLA_HEADS, GLA_DV, GLA_DK), F32)],
        compiler_params=pltpu.CompilerParams(dimension_semantics=("arbitrary", "arbitrary"),
                                             vmem_limit_bytes=VMEM_LIMIT),
        name="gated_linear_attention",
    )(p32, p32, p32, p32, p32, aw, a_b.reshape(1, GLA_QK), o_norm.reshape(1, GLA_DV), tri)


def _pad_cols(w, n):
    return jnp.pad(w, ((0, 0), (0, n - w.shape[1])))


def kernel(x, norm_gains, ffn_w_in, ffn_conv_w, ffn_conv_b, ffn_w_out, hyb_w_in, hyb_w_out, gdn_conv_w, gdn_A_log,
           gdn_dt_bias, gdn_o_norm, nsa_cmp_pe, nsa_cmp_w1, nsa_cmp_b1, nsa_cmp_w2, gla_w_in, gla_a_w2, gla_a_b,
           gla_o_norm, gla_w_out):
    x2 = x.reshape(TOKENS, D_MODEL)
    ffn_w_in_bf = ffn_w_in.astype(BF16)
    ffn_w_out_bf = ffn_w_out.astype(BF16)
    for layer in range(DEPTH):
        if layer % 2 == 0:
            e = layer // 2
            p32, p16 = norm_proj(x2, norm_gains[layer, 0], even_proj_weight(hyb_w_in[e]),
                                 ((EVEN_F32_W, F32), (EVEN_BF16_W, BF16)))
            oa, ob = even_mixer_core(p32, p16, gdn_conv_w[e], gdn_A_log[e], gdn_dt_bias[e], gdn_o_norm[e],
                                     nsa_cmp_pe[e], nsa_cmp_w1[e], nsa_cmp_b1[e], nsa_cmp_w2[e])
            halves = (oa, 0, ob, 0)
            w_mix = hyb_w_out[e]
        else:
            o_ = layer // 2
            w = _pad_cols(gla_w_in[o_], ODD_F32_W).astype(BF16)
            (p32,) = norm_proj(x2, norm_gains[layer, 0], w, ((ODD_F32_W, F32),))
            o = gated_linear_attention(p32, gla_a_w2[o_], gla_a_b[o_], gla_o_norm[o_])
            halves = (o, 0, o, 1)
            w_mix = gla_w_out[o_]
        x2 = mix_out_conv_ffn(*halves, x2, norm_gains[layer, 1], w_mix.astype(BF16), norm_gains[layer, 2],
                              layer, ffn_w_in_bf, ffn_conv_w, ffn_conv_b, ffn_w_out_bf, norm_gains[layer, 3])
    return x2.reshape(BATCH, SEQ, D_MODEL)
```

```python
import jax
import jax.numpy as jnp
import numpy as np
from jax import lax
from jax.experimental import pallas as pl
from jax.experimental.pallas import tpu as pltpu

D_MODEL = 1024
BATCH = 8
SEQ = 2048
DEPTH = 4
TOKENS = BATCH * SEQ

EPS = 1e-6
NEG = -1e30
FORCE = 1e4

GDN_DK = 128
GDN_DV = 128
GDN_HEADS = 4
GDN_QK = GDN_HEADS * GDN_DK
GDN_V = GDN_HEADS * GDN_DV
GDN_CONV = 4
GDN_CHUNK = 64

NSA_DH = 64
NSA_HEADS = 8
NSA_GROUPS = 2
NSA_Q = NSA_HEADS * NSA_DH
NSA_KV = NSA_GROUPS * NSA_DH
CMP_LEN = 32
CMP_STRIDE = 16
CMP_HIDDEN = 2 * NSA_DH
SLC_LEN = 64
SLC_TOPK = 8
WINDOW = 512

GLA_HEADS = 4
GLA_DK = 128
GLA_DV = 256
GLA_QK = GLA_HEADS * GLA_DK
GLA_V = GLA_HEADS * GLA_DV
GLA_RANK = 16
GLA_TAU = 16.0
GLA_CHUNK = 64

D_FF = 2816
FFN_CONV = 3

EVEN_SPLITS = (2 * GDN_QK + GDN_V, GDN_HEADS, GDN_HEADS, GDN_V, NSA_Q, 6 * NSA_KV, 3 * NSA_HEADS)
ODD_SPLITS = (GLA_QK, GLA_QK, GLA_V, GLA_V, GLA_RANK)

LANE = 128
VMEM_LIMIT = 56 * 1024 * 1024

F32 = jnp.float32
BF16 = jnp.bfloat16


def _resident(shape):
    nd = len(shape)
    return pl.BlockSpec(shape, lambda *_: (0,) * nd, pipeline_mode=pl.Buffered(1))


def _resident_layer(shape, layer):
    nd = len(shape)
    return pl.BlockSpec((None,) + tuple(shape), lambda *_: (layer,) + (0,) * nd, pipeline_mode=pl.Buffered(1))


def _silu(x):
    half = 0.5 * x
    return half + half * jnp.tanh(half)


def _rms(x, g):
    ms = jnp.mean(x * x, axis=-1, keepdims=True)
    return x * lax.rsqrt(ms + EPS) * g


PROJ_TM = 512
PROJ_CH = 512


def _proj_kernel(x_ref, g_ref, w_ref, *o_refs):
    h = _rms(x_ref[...], g_ref[...]).astype(BF16)
    col = 0
    for o_ref in o_refs:
        n = o_ref.shape[1]
        for c0 in range(0, n, PROJ_CH):
            cw = min(PROJ_CH, n - c0)
            y = jnp.dot(h, w_ref[:, col + c0:col + c0 + cw], preferred_element_type=F32)
            o_ref[:, c0:c0 + cw] = y.astype(o_ref.dtype)
        col += n


def norm_proj(x2, gain, w, sections):
    T, D = x2.shape
    N = w.shape[1]
    assert N == sum(n for n, _ in sections) and all(n % LANE == 0 for n, _ in sections)
    return pl.pallas_call(
        _proj_kernel,
        grid=(T // PROJ_TM,),
        in_specs=[pl.BlockSpec((PROJ_TM, D), lambda i: (i, 0)), _resident((1, D)), _resident((D, N))],
        out_specs=[pl.BlockSpec((PROJ_TM, n), lambda i: (i, 0)) for n, _ in sections],
        out_shape=[jax.ShapeDtypeStruct((T, n), dt) for n, dt in sections],
        compiler_params=pltpu.CompilerParams(dimension_semantics=("arbitrary",), vmem_limit_bytes=VMEM_LIMIT),
        name="norm_proj",
    )(x2, gain.reshape(1, D), w)


FFN_TM = 512
FFN_TAIL = 8
FFN_CH = 256
MIX_HALF = D_MODEL // 2
assert D_FF % FFN_CH == 0


def _ffn_kernel(oa_ref, ob_ref, x_ref, g_mix_ref, w_mix_ref, g_in_ref, w_in_ref, cw_ref, cb_ref, w_out_ref, g_out_ref,
                y_ref, tail_ref, act_ref):
    t = pl.program_id(1)

    @pl.when(t == 0)
    def _():
        tail_ref[...] = jnp.zeros(tail_ref.shape, F32)

    mix = (jnp.dot(oa_ref[...], w_mix_ref[0:MIX_HALF, :], preferred_element_type=F32)
           + jnp.dot(ob_ref[...], w_mix_ref[MIX_HALF:, :], preferred_element_type=F32))
    x = x_ref[...] + _rms(mix, g_mix_ref[...])
    h = _rms(x, g_in_ref[...]).astype(BF16)

    def conv_half(c0):
        j = c0 // FFN_CH
        u = jnp.dot(h, w_in_ref[:, c0:c0 + FFN_CH], preferred_element_type=F32)
        ext = jnp.concatenate([tail_ref[j], u], axis=0)
        tail_ref[j] = u[FFN_TM - FFN_TAIL:]
        w = cw_ref[:, c0:c0 + FFN_CH]
        y = pltpu.roll(ext, 2, 0) * w[0:1] + pltpu.roll(ext, 1, 0) * w[1:2] + ext * w[2:3] + cb_ref[:, c0:c0 + FFN_CH]
        return y[FFN_TAIL:]

    for c0 in range(0, D_FF, FFN_CH):
        gate = conv_half(c0).astype(BF16)
        val = conv_half(D_FF + c0).astype(BF16)
        act_ref[:, c0:c0 + FFN_CH] = _silu(gate) * val
    down = jnp.dot(act_ref[...], w_out_ref[...], preferred_element_type=F32)
    y_ref[...] = x + _rms(down, g_out_ref[...])


def mix_out_conv_ffn(oa, a_blk, ob, b_blk, x2, g_mix, w_mix, g_in, layer, w_in, conv_w, conv_b, w_out, g_out):
    T, D = x2.shape
    nt = SEQ // FFN_TM
    tok = lambda blk: (lambda b, t: (b * nt + t, blk))
    return pl.pallas_call(
        _ffn_kernel,
        grid=(BATCH, nt),
        in_specs=[pl.BlockSpec((FFN_TM, MIX_HALF), tok(a_blk)), pl.BlockSpec((FFN_TM, MIX_HALF), tok(b_blk)),
                  pl.BlockSpec((FFN_TM, D), tok(0)), _resident((1, D)), _resident((D, D)),
                  _resident((1, D)), _resident_layer((D, 2 * D_FF), layer),
                  _resident_layer((FFN_CONV, 2 * D_FF), layer), _resident_layer((1, 2 * D_FF), layer),
                  _resident_layer((D_FF, D), layer), _resident((1, D))],
        out_specs=pl.BlockSpec((FFN_TM, D), tok(0)),
        out_shape=jax.ShapeDtypeStruct((T, D), F32),
        scratch_shapes=[pltpu.VMEM((2 * D_FF // FFN_CH, FFN_TAIL, FFN_CH), F32), pltpu.VMEM((FFN_TM, D_FF), BF16)],
        compiler_params=pltpu.CompilerParams(dimension_semantics=("arbitrary", "arbitrary"),
                                             vmem_limit_bytes=VMEM_LIMIT),
        name="mix_out_conv_ffn",
    )(oa, ob, x2, g_mix.reshape(1, D), w_mix, g_in.reshape(1, D), w_in, conv_w, conv_b.reshape(DEPTH, 1, -1), w_out,
      g_out.reshape(1, D))


def split_last(x, sizes):
    idx = np.cumsum(sizes)[:-1].tolist()
    return jnp.split(x, idx, axis=-1)


def _dot_nt(a, b):
    return lax.dot_general(a, b, (((1,), (1,)), ((), ())), preferred_element_type=F32)


def _dot_tn(a, b):
    return lax.dot_general(a, b, (((0,), (0,)), ((), ())), preferred_element_type=F32)


def _split3(a):
    hi = a.astype(BF16)
    r = a - hi.astype(F32)
    mid = r.astype(BF16)
    return hi, mid, (r - mid.astype(F32)).astype(BF16)


def _dot_x3(a, b):
    a_hi, a_lo, _ = _split3(a)
    b_hi, b_lo, _ = _split3(b)
    d = lambda p, q: jnp.dot(p, q, preferred_element_type=F32)
    return d(a_hi, b_hi) + (d(a_lo, b_hi) + d(a_hi, b_lo))


def _cumsum_rows(tri, g):
    return sum(jnp.dot(tri, t, preferred_element_type=F32) for t in reversed(_split3(g)))


def _unit_lower_inverses(mats):
    c = mats[0].shape[0]
    eye = jnp.where(lax.broadcasted_iota(jnp.int32, (c, c), 0) == lax.broadcasted_iota(jnp.int32, (c, c), 1), 1.0, 0.0)
    dot = lambda a, b: jnp.dot(a.astype(BF16), b.astype(BF16), preferred_element_type=F32)
    ps = [-a for a in mats]
    ts = [eye + p for p in ps]
    k = 2
    while k < c:
        ps = [dot(p, p) for p in ps]
        ts = [t + dot(t, p) for t, p in zip(ts, ps)]
        k *= 2
    res = [(eye - t) - _dot_x3(a, t) for a, t in zip(mats, ts)]
    return [t + dot(t, r) for t, r in zip(ts, res)]


GDN_TC = 512
GDN_HALO = 8
GDN_SMALL_BETA = 0
GDN_SMALL_DECAY = GDN_HEADS
NSA_SMALL_GATE = 2 * GDN_HEADS


def _gdn_kernel(x_ref, gate_ref, small_ref, cw_ref, alog_ref, dtb_ref, onorm_ref, tri_ref, o_ref, xext_ref, state_ref):
    t = pl.program_id(1)
    C = GDN_CHUNK

    @pl.when(t == 0)
    def _():
        xext_ref[0:GDN_HALO, :] = jnp.zeros((GDN_HALO, xext_ref.shape[1]), F32)
        state_ref[...] = jnp.zeros(state_ref.shape, F32)

    xext_ref[GDN_HALO:, :] = x_ref[...]
    small = small_ref[...]
    beta_all = jax.nn.sigmoid(small)
    z = small + dtb_ref[...]
    softplus = jnp.maximum(z, 0.0) + jnp.log(1.0 + jnp.exp(-jnp.abs(z)))
    g_all = -jnp.exp(alog_ref[...]) * softplus
    tri = tri_ref[...]
    row = lax.broadcasted_iota(jnp.int32, (C, C), 0)
    col = lax.broadcasted_iota(jnp.int32, (C, C), 1)

    def conv_silu(c0, width):
        w = cw_ref[:, c0:c0 + width]
        y = xext_ref[GDN_HALO:, c0:c0 + width] * w[GDN_CONV - 1:GDN_CONV]
        for s in range(1, GDN_CONV):
            y = y + xext_ref[GDN_HALO - s:GDN_HALO - s + GDN_TC, c0:c0 + width] * w[GDN_CONV - 1 - s:GDN_CONV - s]
        return _silu(y)

    n_chunks = GDN_TC // C
    gsums = [_cumsum_rows(tri, g_all[c * C:(c + 1) * C]) for c in range(n_chunks)]
    gsum_ts = [jnp.transpose(gs) for gs in gsums]

    work = {}
    for h in range(GDN_HEADS):
        q_all = conv_silu(h * GDN_DK, GDN_DK)
        k_all = conv_silu(GDN_QK + h * GDN_DK, GDN_DK)
        v_all = conv_silu(2 * GDN_QK + h * GDN_DV, GDN_DV)
        q_all = q_all * (lax.rsqrt(jnp.sum(q_all * q_all, axis=-1, keepdims=True) + EPS) * GDN_DK ** -0.5)
        k_all = k_all * lax.rsqrt(jnp.sum(k_all * k_all, axis=-1, keepdims=True) + EPS)
        for c in range(n_chunks):
            rows = slice(c * C, (c + 1) * C)
            q, k, v = q_all[rows], k_all[rows], v_all[rows]
            beta = beta_all[rows, GDN_SMALL_BETA + h:GDN_SMALL_BETA + h + 1]
            gcol = gsums[c][:, GDN_SMALL_DECAY + h:GDN_SMALL_DECAY + h + 1]
            grow = gsum_ts[c][GDN_SMALL_DECAY + h:GDN_SMALL_DECAY + h + 1, :]
            glast = gcol[C - 1:C]
            decay = jnp.exp(jnp.where(row >= col, gcol - grow, -jnp.inf))
            kb = k * beta
            kbf = k.astype(BF16)
            eg = jnp.exp(gcol)
            work[h, c] = dict(
                a_kk=jnp.where(row > col, _dot_nt(kb.astype(BF16), kbf) * decay, 0.0),
                a_qk=(_dot_nt(q.astype(BF16), kbf) * decay).astype(BF16),
                rhs=jnp.concatenate([v * beta, kb * eg], axis=-1),
                qg=q * eg,
                kd_t=jnp.transpose(k * jnp.exp(glast - gcol)).astype(BF16),
                keep=jnp.exp(glast))
    keys = sorted(work)
    inverses = _unit_lower_inverses([work[key]["a_kk"] for key in keys])
    wks = [work[key] for key in keys]
    dot = lambda a, b: jnp.dot(a, b, preferred_element_type=F32)
    sols = [dot(inv.astype(BF16), wk["rhs"].astype(BF16)) for wk, inv in zip(wks, inverses)]
    us = [sol[:, :GDN_DV].astype(BF16) for sol in sols]
    ws = [sol[:, GDN_DV:].astype(BF16) for sol in sols]
    for wk, s_in in zip(wks, [dot(wk["kd_t"], u) for wk, u in zip(wks, us)]):
        wk["s_in"] = s_in
    for wk, s_mix in zip(wks, [dot(wk["kd_t"], w) for wk, w in zip(wks, ws)]):
        wk["s_mix"] = s_mix.astype(BF16)
    for wk, o_in in zip(wks, [dot(wk["a_qk"], u) for wk, u in zip(wks, us)]):
        wk["o_in"] = o_in
    for wk, aw in zip(wks, [dot(wk["a_qk"], w) for wk, w in zip(wks, ws)]):
        wk["o_mix"] = (wk["qg"] - aw).astype(BF16)

    states = [state_ref[h] for h in range(GDN_HEADS)]
    for c in range(n_chunks):
        rows = slice(c * C, (c + 1) * C)
        for h in range(GDN_HEADS):
            wk = work[h, c]
            sb = states[h].astype(BF16)
            o = wk["o_in"] + jnp.dot(wk["o_mix"], sb, preferred_element_type=F32)
            states[h] = (states[h] * wk["keep"] + wk["s_in"]) - jnp.dot(wk["s_mix"], sb, preferred_element_type=F32)
            gate = gate_ref[rows, h * GDN_DV:(h + 1) * GDN_DV]
            on = o * lax.rsqrt(jnp.mean(o * o, axis=-1, keepdims=True) + EPS) * onorm_ref[...]
            o_ref[rows, h * GDN_DV:(h + 1) * GDN_DV] = (on * _silu(gate)).astype(o_ref.dtype)
    for h in range(GDN_HEADS):
        state_ref[h] = states[h]
    xext_ref[0:GDN_HALO, :] = xext_ref[GDN_TC:GDN_TC + GDN_HALO, :]


def gated_delta_net(p32, conv_w, a_log, dt_bias, o_norm, qkv_col, gate_col, small_col):
    nt = SEQ // GDN_TC
    width = 2 * GDN_QK + GDN_V
    lane_vec = lambda a, off: jnp.zeros((1, LANE), F32).at[0, off:off + GDN_HEADS].set(a.astype(F32))
    tri = jnp.asarray(np.tril(np.ones((GDN_CHUNK, GDN_CHUNK))), BF16)
    tok = lambda blk: (lambda b, t: (b * nt + t, blk))
    return pl.pallas_call(
        _gdn_kernel,
        grid=(BATCH, nt),
        in_specs=[pl.BlockSpec((GDN_TC, width), tok(qkv_col // width)),
                  pl.BlockSpec((GDN_TC, GDN_V), tok(gate_col // GDN_V)),
                  pl.BlockSpec((GDN_TC, LANE), tok(small_col // LANE)),
                  _resident((GDN_CONV, width)), _resident((1, LANE)), _resident((1, LANE)), _resident((1, GDN_DV)),
                  _resident((GDN_CHUNK, GDN_CHUNK))],
        out_specs=pl.BlockSpec((GDN_TC, GDN_V), tok(0)),
        out_shape=jax.ShapeDtypeStruct((TOKENS, GDN_V), BF16),
        scratch_shapes=[pltpu.VMEM((GDN_HALO + GDN_TC, width), F32), pltpu.VMEM((GDN_HEADS, GDN_DK, GDN_DV), F32)],
        compiler_params=pltpu.CompilerParams(dimension_semantics=("arbitrary", "arbitrary"),
                                             vmem_limit_bytes=VMEM_LIMIT),
        name="gated_delta_net",
    )(p32, p32, p32, conv_w, lane_vec(a_log, GDN_SMALL_DECAY), lane_vec(dt_bias, GDN_SMALL_DECAY),
      o_norm.reshape(1, GDN_DV), tri)


N_SLC = SEQ // SLC_LEN
N_CMP = SEQ // CMP_STRIDE
NSA_HG = NSA_HEADS // NSA_GROUPS
NSA_TQ = 128
SEL_STEP = 256
WIN_W = WINDOW + NSA_TQ
NSA_SCALE = NSA_DH ** -0.5


def _cmp_kernel(kc_ref, vc_ref, pe_ref, w1_ref, b1_ref, w2_ref, o_ref, strip_ref):
    half = CMP_STRIDE * NSA_DH
    for kv, x_ref in enumerate((kc_ref, vc_ref)):
        for p in range(CMP_STRIDE):
            xp = x_ref[pl.ds(p, N_CMP, stride=CMP_STRIDE), :]
            for g in range(NSA_GROUPS):
                strip_ref[kv * NSA_GROUPS + g, :, p * NSA_DH:(p + 1) * NSA_DH] = xp[:, g * NSA_DH:(g + 1) * NSA_DH]
    for kv in range(2):
        for g in range(NSA_GROUPS):
            rows = strip_ref[kv * NSA_GROUPS + g]
            lo = (rows + pe_ref[kv, :, :half]).astype(BF16)
            hi = (rows + pe_ref[kv, :, half:]).astype(BF16)
            a = jnp.dot(lo, w1_ref[kv, :half, :], preferred_element_type=F32)
            b = jnp.dot(hi, w1_ref[kv, half:, :], preferred_element_type=F32)
            hid = a + pltpu.roll(b, N_CMP - 1, 0) + b1_ref[kv]
            act = _silu(hid).astype(BF16)
            o_ref[0, kv * NSA_GROUPS + g] = jnp.dot(act, w2_ref[kv], preferred_element_type=F32).astype(BF16)


def nsa_compress(p32, cmp_col, pe, w1, b1, w2):
    n = 2 * NSA_GROUPS
    return pl.pallas_call(
        _cmp_kernel,
        grid=(BATCH,),
        in_specs=[pl.BlockSpec((SEQ, NSA_KV), lambda b: (b, cmp_col // NSA_KV)),
                  pl.BlockSpec((SEQ, NSA_KV), lambda b: (b, cmp_col // NSA_KV + 1)),
                  _resident((2, 1, CMP_LEN * NSA_DH)), _resident((2, CMP_LEN * NSA_DH, CMP_HIDDEN)),
                  _resident((2, 1, CMP_HIDDEN)), _resident((2, CMP_HIDDEN, NSA_DH))],
        out_specs=pl.BlockSpec((1, n, N_CMP, NSA_DH), lambda b: (b, 0, 0, 0)),
        out_shape=jax.ShapeDtypeStruct((BATCH, n, N_CMP, NSA_DH), BF16),
        scratch_shapes=[pltpu.VMEM((n, N_CMP, CMP_STRIDE * NSA_DH), F32)],
        compiler_params=pltpu.CompilerParams(dimension_semantics=("arbitrary",), vmem_limit_bytes=VMEM_LIMIT),
        name="nsa_compress",
    )(p32, p32, pe.reshape(2, 1, -1), w1.astype(BF16), b1.reshape(2, 1, -1), w2.astype(BF16))


V1_ROWS = NSA_DH + 16
NSA_PAIR = 2
RANK_ROWS = 8
NSA_AHEAD = 4
MASK_BIG = 2.0 ** 100


def _masked_attention_t(k, q_t, v1_t, bias_t):
    s = jnp.dot(k, q_t, preferred_element_type=F32) + jnp.concatenate([bias_t] * NSA_HG, axis=1)
    p = jnp.exp((s - jnp.max(s, axis=0, keepdims=True)).astype(BF16))
    acc = jnp.dot(v1_t, p, preferred_element_type=F32)
    return acc[:NSA_DH] / acc[NSA_DH:NSA_DH + 1]


def _transposed_with_ones_row(eye, v):
    v_t = _dot_nt(eye, v).astype(v.dtype)
    row = lax.broadcasted_iota(jnp.int32, (V1_ROWS - NSA_DH, v.shape[0]), 0)
    return jnp.concatenate([v_t, jnp.where(row == 0, 1.0, 0.0).astype(v.dtype)], axis=0)


def _nsa_kernel(q_ref, ks_ref, vs_ref, kw_ref, vw_ref, cmp_ref, gate_ref, ovl_ref, eye_ref, expand_ref,
                o_ref, ksx_ref, qt_ref, ocmp_ref, gsel_ref, gwin_ref, vs1_ref, vw1_ref, vct_ref):
    i = pl.program_id(1)
    q0 = i * NSA_TQ
    eye = eye_ref[...]
    eye_dh = eye[0:NSA_DH, 0:NSA_DH]

    @pl.when(i == 0)
    def _():
        for g in range(NSA_GROUPS):
            lanes = slice(g * NSA_DH, (g + 1) * NSA_DH)
            vs1_ref[g] = _transposed_with_ones_row(eye_dh, vs_ref[:, lanes])
            vw1_ref[g] = _transposed_with_ones_row(eye_dh, vw_ref[:, lanes])
            vct_ref[g] = _dot_nt(eye_dh, cmp_ref[0, NSA_GROUPS + g]).astype(BF16)
            ksx_ref[g] = jnp.concatenate(
                [ks_ref[:, lanes], expand_ref[...], jnp.zeros((SEQ, LANE - NSA_DH - N_SLC), BF16)], axis=1)
            qt_ref[g, NSA_DH + N_SLC:, :] = jnp.zeros((LANE - NSA_DH - N_SLC, NSA_HG * NSA_TQ), BF16)

    gates = jax.nn.sigmoid(gate_ref[...])
    gates_t = jnp.transpose(gates)
    tok_l = q0 + lax.broadcasted_iota(jnp.int32, (1, NSA_TQ), 1)

    groups = range(NSA_GROUPS)
    scale = jnp.asarray(NSA_SCALE, BF16)
    for pair in range(NSA_HEADS // 2):
        pair_t = _dot_nt(eye, q_ref[:, pair * LANE:(pair + 1) * LANE] * scale).astype(BF16)
        for k in range(2):
            g, hk = divmod(2 * pair + k, NSA_HG)
            qt_ref[g, 0:NSA_DH, hk * NSA_TQ:(hk + 1) * NSA_TQ] = pair_t[k * NSA_DH:(k + 1) * NSA_DH]

    def gate_row(g, branch):
        lane = NSA_SMALL_GATE + branch * NSA_HEADS + g * NSA_HG
        return jnp.concatenate([gates_t[lane + hk:lane + hk + 1, :] for hk in range(NSA_HG)], axis=1)

    cmp_end = lax.broadcasted_iota(jnp.int32, (N_CMP, 1), 0) * CMP_STRIDE + (CMP_LEN - 1)
    cvalid = cmp_end <= jnp.concatenate([tok_l] * NSA_HG, axis=1)
    ss = [jnp.dot(cmp_ref[0, g], qt_ref[g, 0:NSA_DH, :], preferred_element_type=F32) for g in groups]
    sms = [jnp.where(cvalid, s, NEG) for s in ss]
    es = [jnp.exp(sm - jnp.max(sm, axis=0, keepdims=True)) for sm in sms]
    ps = [jnp.where(cvalid, e / jnp.sum(e, axis=0, keepdims=True), 0.0) for e in es]
    for g in groups:
        ocmp_ref[g] = gate_row(g, 0) * jnp.dot(vct_ref[g], ps[g].astype(BF16), preferred_element_type=F32)
        gsel_ref[g] = gate_row(g, 1)
        gwin_ref[g] = gate_row(g, 2)

    psums = [sum(p[:, hk * NSA_TQ:(hk + 1) * NSA_TQ] for hk in range(NSA_HG)) for p in ps]
    imps = [jnp.dot(ovl_ref[...], psum.astype(BF16), preferred_element_type=F32) for psum in psums]
    blk = lax.broadcasted_iota(jnp.int32, (N_SLC, NSA_TQ), 0)
    cur = tok_l // SLC_LEN
    bvalid = blk <= cur
    forced = (blk == 0) | (blk == cur) | (blk == cur - 1)
    scores_ = [jnp.where(bvalid, imp + jnp.where(forced, FORCE, 0.0), NEG) for imp in imps]
    tiles = [slice(r, r + RANK_ROWS) for r in range(0, N_SLC, RANK_ROWS)]
    rank_tiles = [[jnp.zeros((RANK_ROWS, NSA_TQ), F32) for _ in tiles] for _ in groups]
    for j in range(N_SLC):
        for g in groups:
            sj = scores_[g][j:j + 1, :]
            for n, rows in enumerate(tiles):
                s_tile = scores_[g][rows]
                if j < rows.start:
                    before = jnp.where(sj >= s_tile, 1.0, 0.0)
                elif j >= rows.stop - 1:
                    before = jnp.where(sj > s_tile, 1.0, 0.0)
                else:
                    tie = jnp.where((sj == s_tile) & (blk[rows] > j), 1.0, 0.0)
                    before = jnp.where(sj > s_tile, 1.0, tie)
                rank_tiles[g][n] = rank_tiles[g][n] + before
    ranks = [jnp.concatenate(rank_tiles[g], axis=0) for g in groups]
    for g in groups:
        off = jnp.where((ranks[g] < SLC_TOPK) & bvalid, 0.0, -MASK_BIG).astype(BF16)
        qt_ref[g, NSA_DH:NSA_DH + N_SLC, :] = jnp.concatenate([off] * NSA_HG, axis=1)

    for n in range(SEQ // SEL_STEP):
        @pl.when(i // (SEL_STEP // NSA_TQ) == n)
        def _():
            w = (n + 1) * SEL_STEP
            w0 = pl.multiple_of(jnp.maximum(q0 + NSA_TQ - WIN_W, 0), NSA_TQ)
            r = pl.ds(w0, WIN_W)

            def bias(pos, lo_excl):
                ok = (pos <= tok_l) if lo_excl is None else ((pos <= tok_l) & (pos > lo_excl))
                return jnp.concatenate([jnp.where(ok, 0.0, NEG)] * NSA_PAIR, axis=1)

            sel_causal = bias((w - SEL_STEP) + lax.broadcasted_iota(jnp.int32, (SEL_STEP, 1), 0), None)
            wpos = w0 + lax.broadcasted_iota(jnp.int32, (WIN_W, 1), 0)
            if n < WINDOW // SEL_STEP:
                win_edges = [(0, WIN_W, bias(wpos, tok_l - WINDOW))]
            else:
                win_edges = [(0, NSA_TQ, bias(wpos[:NSA_TQ], tok_l - WINDOW)),
                             (WIN_W - NSA_TQ, WIN_W, bias(wpos[WIN_W - NSA_TQ:], None))]

            def add_bias(s, edges):
                pieces, at = [], 0
                for lo, hi, b in edges:
                    pieces += [s[at:lo], s[lo:hi] + b]
                    at = hi
                return jnp.concatenate([p for p in pieces + [s[at:]] if p.shape[0]], axis=0)

            problems = []
            for g in range(NSA_GROUPS):
                lanes = slice(g * NSA_DH, (g + 1) * NSA_DH)
                for half in range(NSA_HG // NSA_PAIR):
                    cols = slice(half * NSA_PAIR * NSA_TQ, (half + 1) * NSA_PAIR * NSA_TQ)
                    problems.append((g, cols, 0, lambda g=g: ksx_ref[g, 0:w, :], lambda g=g: vs1_ref[g, :, 0:w],
                                     lambda g=g, cols=cols: qt_ref[g, :, cols], [(w - SEL_STEP, w, sel_causal)]))
                    problems.append((g, cols, 1, lambda lanes=lanes: kw_ref[r, lanes], lambda g=g: vw1_ref[g, :, r],
                                     lambda g=g, cols=cols: qt_ref[g, 0:NSA_DH, cols], win_edges))

            def scores(prob):
                _, _, _, k, _, q_t, edges = prob
                return add_bias(jnp.dot(k(), q_t(), preferred_element_type=F32), edges)

            def probs(s):
                return jnp.exp((s - jnp.max(s, axis=0, keepdims=True)).astype(BF16))

            def values(prob, p):
                acc = jnp.dot(prob[4](), p, preferred_element_type=F32)
                return acc[:NSA_DH] / acc[NSA_DH:NSA_DH + 1]

            problems = problems[1:2] + problems[0:1] + problems[2:]
            outs = {}
            ahead = [scores(prob) for prob in problems[:NSA_AHEAD]]
            for idx, prob in enumerate(problems):
                s_cur = ahead.pop(0)
                if idx + NSA_AHEAD < len(problems):
                    ahead.append(scores(problems[idx + NSA_AHEAD]))
                outs[prob[0], prob[1].start, prob[2]] = values(prob, probs(s_cur))

            for g in range(NSA_GROUPS):
                for half in range(NSA_HG // NSA_PAIR):
                    c0 = half * NSA_PAIR * NSA_TQ
                    cols = slice(c0, c0 + NSA_PAIR * NSA_TQ)
                    o_t = (ocmp_ref[g, :, cols] + gsel_ref[g, :, cols] * outs[g, c0, 0]
                           + gwin_ref[g, :, cols] * outs[g, c0, 1])
                    for k in range(NSA_PAIR):
                        h = g * NSA_HG + half * NSA_PAIR + k
                        o = jnp.transpose(o_t[:, k * NSA_TQ:(k + 1) * NSA_TQ])
                        o_ref[:, h * NSA_DH:(h + 1) * NSA_DH] = o.astype(o_ref.dtype)


def _nsa_constants():
    c0 = np.arange(N_CMP)[:, None] * CMP_STRIDE
    s0 = np.arange(N_SLC)[None, :] * SLC_LEN
    overlap = np.clip(np.minimum(c0 + CMP_LEN, s0 + SLC_LEN) - np.maximum(c0, s0), 0, None) / CMP_STRIDE
    overlap[N_CMP - 1] = 0.0
    expand_t = (np.arange(SEQ)[:, None] // SLC_LEN == np.arange(N_SLC)[None, :]).astype(np.float32)
    return jnp.asarray(overlap.T, BF16), jnp.asarray(np.eye(LANE), BF16), jnp.asarray(expand_t, BF16)


def nsa_attention(p16, cmp_kv, p32, q_col, kv_col, small_col):
    nq = SEQ // NSA_TQ
    ovl_t, eye, expand = _nsa_constants()
    kv_spec = lambda n: pl.BlockSpec((SEQ, NSA_KV), lambda b, i: (b, kv_col // NSA_KV + n))
    return pl.pallas_call(
        _nsa_kernel,
        grid=(BATCH, nq),
        in_specs=[pl.BlockSpec((NSA_TQ, NSA_Q), lambda b, i: (b * nq + i, q_col // NSA_Q)),
                  kv_spec(0), kv_spec(1), kv_spec(2), kv_spec(3),
                  pl.BlockSpec((1, 2 * NSA_GROUPS, N_CMP, NSA_DH), lambda b, i: (b, 0, 0, 0)),
                  pl.BlockSpec((NSA_TQ, LANE), lambda b, i: (b * nq + i, small_col // LANE)),
                  _resident(ovl_t.shape), _resident(eye.shape), _resident(expand.shape)],
        out_specs=pl.BlockSpec((NSA_TQ, NSA_Q), lambda b, i: (b * nq + i, 0)),
        out_shape=jax.ShapeDtypeStruct((TOKENS, NSA_Q), BF16),
        scratch_shapes=[pltpu.VMEM((NSA_GROUPS, SEQ, LANE), BF16),
                        pltpu.VMEM((NSA_GROUPS, LANE, NSA_HG * NSA_TQ), BF16),
                        pltpu.VMEM((NSA_GROUPS, NSA_DH, NSA_HG * NSA_TQ), F32)]
        + [pltpu.VMEM((NSA_GROUPS, 1, NSA_HG * NSA_TQ), F32)] * 2
        + [pltpu.VMEM((NSA_GROUPS, V1_ROWS, SEQ), BF16)] * 2
        + [pltpu.VMEM((NSA_GROUPS, NSA_DH, N_CMP), BF16)],
        compiler_params=pltpu.CompilerParams(dimension_semantics=("arbitrary", "arbitrary"),
                                             vmem_limit_bytes=VMEM_LIMIT),
        name="nsa_attention",
    )(p16, p16, p16, p16, p16, cmp_kv, p32, ovl_t, eye, expand)


EVEN_QKV_COL = 0
EVEN_GATE_COL = 2 * GDN_QK + GDN_V
EVEN_CMP_COL = EVEN_GATE_COL + GDN_V
EVEN_SMALL_COL = EVEN_CMP_COL + 2 * NSA_KV
EVEN_F32_W = EVEN_SMALL_COL + LANE
EVEN_Q_COL = 0
EVEN_KV_COL = NSA_Q
EVEN_BF16_W = NSA_Q + 4 * NSA_KV
assert NSA_SMALL_GATE + 3 * NSA_HEADS <= LANE


def even_proj_weight(w_in):
    qkv_a, beta, dec, gate_a, q_b, kv_b, gate_b = split_last(w_in, EVEN_SPLITS)
    pad = jnp.zeros((w_in.shape[0], LANE - 2 * GDN_HEADS - 3 * NSA_HEADS), w_in.dtype)
    return jnp.concatenate([qkv_a, gate_a, kv_b[:, :2 * NSA_KV], beta, dec, gate_b, pad, q_b, kv_b[:, 2 * NSA_KV:]],
                           axis=1).astype(BF16)


def even_mixer_core(p32, p16, conv_w, A_log, dt_bias, o_norm, cmp_pe, cmp_w1, cmp_b1, cmp_w2):
    oa = gated_delta_net(p32, conv_w, A_log, dt_bias, o_norm, EVEN_QKV_COL, EVEN_GATE_COL, EVEN_SMALL_COL)
    cmp_kv = nsa_compress(p32, EVEN_CMP_COL, cmp_pe, cmp_w1, cmp_b1, cmp_w2)
    ob = nsa_attention(p16, cmp_kv, p32, EVEN_Q_COL, EVEN_KV_COL, EVEN_SMALL_COL)
    return oa, ob


GLA_TC = 512
ODD_Q_COL = 0
ODD_K_COL = GLA_QK
ODD_V_COL = 2 * GLA_QK
ODD_R_COL = ODD_V_COL + GLA_V
ODD_LR_COL = ODD_R_COL + GLA_V
ODD_F32_W = ODD_LR_COL + LANE


def _gla_kernel(q_ref, k_ref, v_ref, r_ref, lr_ref, aw_ref, ab_ref, onorm_ref, tri_ref, o_ref, state_ref):
    t = pl.program_id(1)
    C = GLA_CHUNK

    @pl.when(t == 0)
    def _():
        state_ref[...] = jnp.zeros(state_ref.shape, F32)

    z = jnp.dot(lr_ref[...].astype(BF16), aw_ref[...], preferred_element_type=F32) + ab_ref[...]
    log_a = (jnp.minimum(z, 0.0) - jnp.log(1.0 + jnp.exp(-jnp.abs(z)))) * (1.0 / GLA_TAU)
    tri = tri_ref[...]
    causal = lax.broadcasted_iota(jnp.int32, (C, C), 0) >= lax.broadcasted_iota(jnp.int32, (C, C), 1)

    n_chunks = GLA_TC // C
    b_alls = [_cumsum_rows(tri, log_a[c * C:(c + 1) * C]) for c in range(n_chunks)]
    work = {}
    for c in range(n_chunks):
        rows = slice(c * C, (c + 1) * C)
        for h in range(GLA_HEADS):
            kl = slice(h * GLA_DK, (h + 1) * GLA_DK)
            b = b_alls[c][:, kl]
            b_last = b[C - 1:C]
            k = k_ref[rows, kl]
            work[c, h] = dict(
                rows=rows, vl=slice(h * GLA_DV, (h + 1) * GLA_DV),
                v=v_ref[rows, h * GLA_DV:(h + 1) * GLA_DV].astype(BF16),
                qt=(q_ref[rows, kl] * GLA_DK ** -0.5 * jnp.exp(b)).astype(BF16),
                kt=(k * jnp.exp(-b)).astype(BF16),
                kh=(k * jnp.exp(b_last - b)).astype(BF16),
                keep=jnp.exp(b_last))
    wks = [work[key] for key in sorted(work)]
    intras = [jnp.where(causal, _dot_nt(wk["qt"], wk["kt"]), 0.0).astype(BF16) for wk in wks]
    o_intras = [jnp.dot(intra, wk["v"], preferred_element_type=F32) for wk, intra in zip(wks, intras)]
    updates = [_dot_tn(wk["v"], wk["kh"]) for wk in wks]

    states = [state_ref[h] for h in range(GLA_HEADS)]
    starts = []
    for (c, h), wk, upd in zip(sorted(work), wks, updates):
        starts.append(states[h].astype(BF16))
        states[h] = states[h] * wk["keep"] + upd
    for h in range(GLA_HEADS):
        state_ref[h] = states[h]

    o_inters = [_dot_nt(wk["qt"], s0) for wk, s0 in zip(wks, starts)]
    for wk, o_intra, o_inter in zip(wks, o_intras, o_inters):
        o = o_intra + o_inter
        r = r_ref[wk["rows"], wk["vl"]]
        on = o * lax.rsqrt(jnp.mean(o * o, axis=-1, keepdims=True) + EPS) * onorm_ref[...]
        o_ref[wk["rows"], wk["vl"]] = (on * _silu(r)).astype(o_ref.dtype)


def gated_linear_attention(p32, a_w2, a_b, o_norm):
    nt = SEQ // GLA_TC
    tok = lambda blk: (lambda b, t: (b * nt + t, blk))
    aw = jnp.pad(a_w2, ((0, LANE - GLA_RANK), (0, 0))).astype(BF16)
    tri = jnp.asarray(np.tril(np.ones((GLA_CHUNK, GLA_CHUNK))), BF16)
    return pl.pallas_call(
        _gla_kernel,
        grid=(BATCH, nt),
        in_specs=[pl.BlockSpec((GLA_TC, GLA_QK), tok(ODD_Q_COL // GLA_QK)),
                  pl.BlockSpec((GLA_TC, GLA_QK), tok(ODD_K_COL // GLA_QK)),
                  pl.BlockSpec((GLA_TC, GLA_V), tok(ODD_V_COL // GLA_V)),
                  pl.BlockSpec((GLA_TC, GLA_V), tok(ODD_R_COL // GLA_V)),
                  pl.BlockSpec((GLA_TC, LANE), tok(ODD_LR_COL // LANE)),
                  _resident((LANE, GLA_QK)), _resident((1, GLA_QK)), _resident((1, GLA_DV)),
                  _resident((GLA_CHUNK, GLA_CHUNK))],
        out_specs=pl.BlockSpec((GLA_TC, GLA_V), tok(0)),
        out_shape=jax.ShapeDtypeStruct((TOKENS, GLA_V), BF16),
        scratch_shapes=[pltpu.VMEM((GLA_HEADS, GLA_DV, GLA_DK), F32)],
        compiler_params=pltpu.CompilerParams(dimension_semantics=("arbitrary", "arbitrary"),
                                             vmem_limit_bytes=VMEM_LIMIT),
        name="gated_linear_attention",
    )(p32, p32, p32, p32, p32, aw, a_b.reshape(1, GLA_QK), o_norm.reshape(1, GLA_DV), tri)


def _pad_cols(w, n):
    return jnp.pad(w, ((0, 0), (0, n - w.shape[1])))


def kernel(x, norm_gains, ffn_w_in, ffn_conv_w, ffn_conv_b, ffn_w_out, hyb_w_in, hyb_w_out, gdn_conv_w, gdn_A_log,
           gdn_dt_bias, gdn_o_norm, nsa_cmp_pe, nsa_cmp_w1, nsa_cmp_b1, nsa_cmp_w2, gla_w_in, gla_a_w2, gla_a_b,
           gla_o_norm, gla_w_out):
    x2 = x.reshape(TOKENS, D_MODEL)
    ffn_w_in_bf = ffn_w_in.astype(BF16)
    ffn_w_out_bf = ffn_w_out.astype(BF16)
    for layer in range(DEPTH):
        if layer % 2 == 0:
            e = layer // 2
            p32, p16 = norm_proj(x2, norm_gains[layer, 0], even_proj_weight(hyb_w_in[e]),
                                 ((EVEN_F32_W, F32), (EVEN_BF16_W, BF16)))
            oa, ob = even_mixer_core(p32, p16, gdn_conv_w[e], gdn_A_log[e], gdn_dt_bias[e], gdn_o_norm[e],
                                     nsa_cmp_pe[e], nsa_cmp_w1[e], nsa_cmp_b1[e], nsa_cmp_w2[e])
            halves = (oa, 0, ob, 0)
            w_mix = hyb_w_out[e]
        else:
            o_ = layer // 2
            w = _pad_cols(gla_w_in[o_], ODD_F32_W).astype(BF16)
            (p32,) = norm_proj(x2, norm_gains[layer, 0], w, ((ODD_F32_W, F32),))
            o = gated_linear_attention(p32, gla_a_w2[o_], gla_a_b[o_], gla_o_norm[o_])
            halves = (o, 0, o, 1)
            w_mix = gla_w_out[o_]
        x2 = mix_out_conv_ffn(*halves, x2, norm_gains[layer, 1], w_mix.astype(BF16), norm_gains[layer, 2],
                              layer, ffn_w_in_bf, ffn_conv_w, ffn_conv_b, ffn_w_out_bf, norm_gains[layer, 3])
    return x2.reshape(BATCH, SEQ, D_MODEL)
```

```python
import jax
import jax.numpy as jnp
import numpy as np
from jax import lax
from jax.experimental import pallas as pl
from jax.experimental.pallas import tpu as pltpu

D_MODEL = 1024
BATCH = 8
SEQ = 2048
DEPTH = 4
TOKENS = BATCH * SEQ

EPS = 1e-6
NEG = -1e30
FORCE = 1e4

GDN_DK = 128
GDN_DV = 128
GDN_HEADS = 4
GDN_QK = GDN_HEADS * GDN_DK
GDN_V = GDN_HEADS * GDN_DV
GDN_CONV = 4
GDN_CHUNK = 64

NSA_DH = 64
NSA_HEADS = 8
NSA_GROUPS = 2
NSA_Q = NSA_HEADS * NSA_DH
NSA_KV = NSA_GROUPS * NSA_DH
CMP_LEN = 32
CMP_STRIDE = 16
CMP_HIDDEN = 2 * NSA_DH
SLC_LEN = 64
SLC_TOPK = 8
WINDOW = 512

GLA_HEADS = 4
GLA_DK = 128
GLA_DV = 256
GLA_QK = GLA_HEADS * GLA_DK
GLA_V = GLA_HEADS * GLA_DV
GLA_RANK = 16
GLA_TAU = 16.0
GLA_CHUNK = 64

D_FF = 2816
FFN_CONV = 3

EVEN_SPLITS = (2 * GDN_QK + GDN_V, GDN_HEADS, GDN_HEADS, GDN_V, NSA_Q, 6 * NSA_KV, 3 * NSA_HEADS)
ODD_SPLITS = (GLA_QK, GLA_QK, GLA_V, GLA_V, GLA_RANK)

LANE = 128
VMEM_LIMIT = 56 * 1024 * 1024

F32 = jnp.float32
BF16 = jnp.bfloat16


def _resident(shape):
    nd = len(shape)
    return pl.BlockSpec(shape, lambda *_: (0,) * nd, pipeline_mode=pl.Buffered(1))


def _resident_layer(shape, layer):
    nd = len(shape)
    return pl.BlockSpec((None,) + tuple(shape), lambda *_: (layer,) + (0,) * nd, pipeline_mode=pl.Buffered(1))


def _silu(x):
    half = 0.5 * x
    return half + half * jnp.tanh(half)


def _rms(x, g):
    ms = jnp.mean(x * x, axis=-1, keepdims=True)
    return x * lax.rsqrt(ms + EPS) * g


PROJ_TM = 512
PROJ_CH = 512


def _proj_kernel(x_ref, g_ref, w_ref, *o_refs):
    h = _rms(x_ref[...], g_ref[...]).astype(BF16)
    col = 0
    for o_ref in o_refs:
        n = o_ref.shape[1]
        for c0 in range(0, n, PROJ_CH):
            cw = min(PROJ_CH, n - c0)
            y = jnp.dot(h, w_ref[:, col + c0:col + c0 + cw], preferred_element_type=F32)
            o_ref[:, c0:c0 + cw] = y.astype(o_ref.dtype)
        col += n


def norm_proj(x2, gain, w, sections):
    T, D = x2.shape
    N = w.shape[1]
    assert N == sum(n for n, _ in sections) and all(n % LANE == 0 for n, _ in sections)
    return pl.pallas_call(
        _proj_kernel,
        grid=(T // PROJ_TM,),
        in_specs=[pl.BlockSpec((PROJ_TM, D), lambda i: (i, 0)), _resident((1, D)), _resident((D, N))],
        out_specs=[pl.BlockSpec((PROJ_TM, n), lambda i: (i, 0)) for n, _ in sections],
        out_shape=[jax.ShapeDtypeStruct((T, n), dt) for n, dt in sections],
        compiler_params=pltpu.CompilerParams(dimension_semantics=("arbitrary",), vmem_limit_bytes=VMEM_LIMIT),
        name="norm_proj",
    )(x2, gain.reshape(1, D), w)


FFN_TM = 512
FFN_TAIL = 8
FFN_CH = 256
MIX_HALF = D_MODEL // 2
assert D_FF % FFN_CH == 0


def _ffn_kernel(oa_ref, ob_ref, x_ref, g_mix_ref, w_mix_ref, g_in_ref, w_in_ref, cw_ref, cb_ref, w_out_ref, g_out_ref,
                y_ref, tail_ref, act_ref):
    t = pl.program_id(1)

    @pl.when(t == 0)
    def _():
        tail_ref[...] = jnp.zeros(tail_ref.shape, F32)

    mix = (jnp.dot(oa_ref[...], w_mix_ref[0:MIX_HALF, :], preferred_element_type=F32)
           + jnp.dot(ob_ref[...], w_mix_ref[MIX_HALF:, :], preferred_element_type=F32))
    x = x_ref[...] + _rms(mix, g_mix_ref[...])
    h = _rms(x, g_in_ref[...]).astype(BF16)

    def conv_half(c0):
        j = c0 // FFN_CH
        u = jnp.dot(h, w_in_ref[:, c0:c0 + FFN_CH], preferred_element_type=F32)
        ext = jnp.concatenate([tail_ref[j], u], axis=0)
        tail_ref[j] = u[FFN_TM - FFN_TAIL:]
        w = cw_ref[:, c0:c0 + FFN_CH]
        y = pltpu.roll(ext, 2, 0) * w[0:1] + pltpu.roll(ext, 1, 0) * w[1:2] + ext * w[2:3] + cb_ref[:, c0:c0 + FFN_CH]
        return y[FFN_TAIL:]

    for c0 in range(0, D_FF, FFN_CH):
        gate = conv_half(c0).astype(BF16)
        val = conv_half(D_FF + c0).astype(BF16)
        act_ref[:, c0:c0 + FFN_CH] = _silu(gate) * val
    down = jnp.dot(act_ref[...], w_out_ref[...], preferred_element_type=F32)
    y_ref[...] = x + _rms(down, g_out_ref[...])


def mix_out_conv_ffn(oa, a_blk, ob, b_blk, x2, g_mix, w_mix, g_in, layer, w_in, conv_w, conv_b, w_out, g_out):
    T, D = x2.shape
    nt = SEQ // FFN_TM
    tok = lambda blk: (lambda b, t: (b * nt + t, blk))
    return pl.pallas_call(
        _ffn_kernel,
        grid=(BATCH, nt),
        in_specs=[pl.BlockSpec((FFN_TM, MIX_HALF), tok(a_blk)), pl.BlockSpec((FFN_TM, MIX_HALF), tok(b_blk)),
                  pl.BlockSpec((FFN_TM, D), tok(0)), _resident((1, D)), _resident((D, D)),
                  _resident((1, D)), _resident_layer((D, 2 * D_FF), layer),
                  _resident_layer((FFN_CONV, 2 * D_FF), layer), _resident_layer((1, 2 * D_FF), layer),
                  _resident_layer((D_FF, D), layer), _resident((1, D))],
        out_specs=pl.BlockSpec((FFN_TM, D), tok(0)),
        out_shape=jax.ShapeDtypeStruct((T, D), F32),
        scratch_shapes=[pltpu.VMEM((2 * D_FF // FFN_CH, FFN_TAIL, FFN_CH), F32), pltpu.VMEM((FFN_TM, D_FF), BF16)],
        compiler_params=pltpu.CompilerParams(dimension_semantics=("arbitrary", "arbitrary"),
                                             vmem_limit_bytes=VMEM_LIMIT),
        name="mix_out_conv_ffn",
    )(oa, ob, x2, g_mix.reshape(1, D), w_mix, g_in.reshape(1, D), w_in, conv_w, conv_b.reshape(DEPTH, 1, -1), w_out,
      g_out.reshape(1, D))


def split_last(x, sizes):
    idx = np.cumsum(sizes)[:-1].tolist()
    return jnp.split(x, idx, axis=-1)


def _dot_nt(a, b):
    return lax.dot_general(a, b, (((1,), (1,)), ((), ())), preferred_element_type=F32)


def _dot_tn(a, b):
    return lax.dot_general(a, b, (((0,), (0,)), ((), ())), preferred_element_type=F32)


def _split3(a):
    hi = a.astype(BF16)
    r = a - hi.astype(F32)
    mid = r.astype(BF16)
    return hi, mid, (r - mid.astype(F32)).astype(BF16)


def _dot_x3(a, b):
    a_hi, a_lo, _ = _split3(a)
    b_hi, b_lo, _ = _split3(b)
    d = lambda p, q: jnp.dot(p, q, preferred_element_type=F32)
    return d(a_hi, b_hi) + (d(a_lo, b_hi) + d(a_hi, b_lo))


def _cumsum_rows(tri, g):
    return sum(jnp.dot(tri, t, preferred_element_type=F32) for t in reversed(_split3(g)))


def _unit_lower_inverses(mats):
    c = mats[0].shape[0]
    eye = jnp.where(lax.broadcasted_iota(jnp.int32, (c, c), 0) == lax.broadcasted_iota(jnp.int32, (c, c), 1), 1.0, 0.0)
    dot = lambda a, b: jnp.dot(a.astype(BF16), b.astype(BF16), preferred_element_type=F32)
    ps = [-a for a in mats]
    ts = [eye + p for p in ps]
    k = 2
    while k < c:
        ps = [dot(p, p) for p in ps]
        ts = [t + dot(t, p) for t, p in zip(ts, ps)]
        k *= 2
    res = [(eye - t) - _dot_x3(a, t) for a, t in zip(mats, ts)]
    return [t + dot(t, r) for t, r in zip(ts, res)]


GDN_TC = 512
GDN_HALO = 8
GDN_SMALL_BETA = 0
GDN_SMALL_DECAY = GDN_HEADS
NSA_SMALL_GATE = 2 * GDN_HEADS


def _gdn_kernel(x_ref, gate_ref, small_ref, cw_ref, alog_ref, dtb_ref, onorm_ref, tri_ref, o_ref, xext_ref, state_ref):
    t = pl.program_id(1)
    C = GDN_CHUNK

    @pl.when(t == 0)
    def _():
        xext_ref[0:GDN_HALO, :] = jnp.zeros((GDN_HALO, xext_ref.shape[1]), F32)
        state_ref[...] = jnp.zeros(state_ref.shape, F32)

    xext_ref[GDN_HALO:, :] = x_ref[...]
    small = small_ref[...]
    beta_all = jax.nn.sigmoid(small)
    z = small + dtb_ref[...]
    softplus = jnp.maximum(z, 0.0) + jnp.log(1.0 + jnp.exp(-jnp.abs(z)))
    g_all = -jnp.exp(alog_ref[...]) * softplus
    tri = tri_ref[...]
    row = lax.broadcasted_iota(jnp.int32, (C, C), 0)
    col = lax.broadcasted_iota(jnp.int32, (C, C), 1)

    def conv_silu(c0, width):
        w = cw_ref[:, c0:c0 + width]
        y = xext_ref[GDN_HALO:, c0:c0 + width] * w[GDN_CONV - 1:GDN_CONV]
        for s in range(1, GDN_CONV):
            y = y + xext_ref[GDN_HALO - s:GDN_HALO - s + GDN_TC, c0:c0 + width] * w[GDN_CONV - 1 - s:GDN_CONV - s]
        return _silu(y)

    n_chunks = GDN_TC // C
    gsums = [_cumsum_rows(tri, g_all[c * C:(c + 1) * C]) for c in range(n_chunks)]
    gsum_ts = [jnp.transpose(gs) for gs in gsums]

    work = {}
    for h in range(GDN_HEADS):
        q_all = conv_silu(h * GDN_DK, GDN_DK)
        k_all = conv_silu(GDN_QK + h * GDN_DK, GDN_DK)
        v_all = conv_silu(2 * GDN_QK + h * GDN_DV, GDN_DV)
        q_all = q_all * (lax.rsqrt(jnp.sum(q_all * q_all, axis=-1, keepdims=True) + EPS) * GDN_DK ** -0.5)
        k_all = k_all * lax.rsqrt(jnp.sum(k_all * k_all, axis=-1, keepdims=True) + EPS)
        for c in range(n_chunks):
            rows = slice(c * C, (c + 1) * C)
            q, k, v = q_all[rows], k_all[rows], v_all[rows]
            beta = beta_all[rows, GDN_SMALL_BETA + h:GDN_SMALL_BETA + h + 1]
            gcol = gsums[c][:, GDN_SMALL_DECAY + h:GDN_SMALL_DECAY + h + 1]
            grow = gsum_ts[c][GDN_SMALL_DECAY + h:GDN_SMALL_DECAY + h + 1, :]
            glast = gcol[C - 1:C]
            decay = jnp.exp(jnp.where(row >= col, gcol - grow, -jnp.inf))
            kb = k * beta
            kbf = k.astype(BF16)
            eg = jnp.exp(gcol)
            work[h, c] = dict(
                a_kk=jnp.where(row > col, _dot_nt(kb.astype(BF16), kbf) * decay, 0.0),
                a_qk=(_dot_nt(q.astype(BF16), kbf) * decay).astype(BF16),
                rhs=jnp.concatenate([v * beta, kb * eg], axis=-1),
                qg=q * eg,
                kd_t=jnp.transpose(k * jnp.exp(glast - gcol)).astype(BF16),
                keep=jnp.exp(glast))
    keys = sorted(work)
    inverses = _unit_lower_inverses([work[key]["a_kk"] for key in keys])
    wks = [work[key] for key in keys]
    dot = lambda a, b: jnp.dot(a, b, preferred_element_type=F32)
    sols = [dot(inv.astype(BF16), wk["rhs"].astype(BF16)) for wk, inv in zip(wks, inverses)]
    us = [sol[:, :GDN_DV].astype(BF16) for sol in sols]
    ws = [sol[:, GDN_DV:].astype(BF16) for sol in sols]
    for wk, s_in in zip(wks, [dot(wk["kd_t"], u) for wk, u in zip(wks, us)]):
        wk["s_in"] = s_in
    for wk, s_mix in zip(wks, [dot(wk["kd_t"], w) for wk, w in zip(wks, ws)]):
        wk["s_mix"] = s_mix.astype(BF16)
    for wk, o_in in zip(wks, [dot(wk["a_qk"], u) for wk, u in zip(wks, us)]):
        wk["o_in"] = o_in
    for wk, aw in zip(wks, [dot(wk["a_qk"], w) for wk, w in zip(wks, ws)]):
        wk["o_mix"] = (wk["qg"] - aw).astype(BF16)

    states = [state_ref[h] for h in range(GDN_HEADS)]
    for c in range(n_chunks):
        rows = slice(c * C, (c + 1) * C)
        for h in range(GDN_HEADS):
            wk = work[h, c]
            sb = states[h].astype(BF16)
            o = wk["o_in"] + jnp.dot(wk["o_mix"], sb, preferred_element_type=F32)
            states[h] = (states[h] * wk["keep"] + wk["s_in"]) - jnp.dot(wk["s_mix"], sb, preferred_element_type=F32)
            gate = gate_ref[rows, h * GDN_DV:(h + 1) * GDN_DV]
            on = o * lax.rsqrt(jnp.mean(o * o, axis=-1, keepdims=True) + EPS) * onorm_ref[...]
            o_ref[rows, h * GDN_DV:(h + 1) * GDN_DV] = (on * _silu(gate)).astype(o_ref.dtype)
    for h in range(GDN_HEADS):
        state_ref[h] = states[h]
    xext_ref[0:GDN_HALO, :] = xext_ref[GDN_TC:GDN_TC + GDN_HALO, :]


def gated_delta_net(p32, conv_w, a_log, dt_bias, o_norm, qkv_col, gate_col, small_col):
    nt = SEQ // GDN_TC
    width = 2 * GDN_QK + GDN_V
    lane_vec = lambda a, off: jnp.zeros((1, LANE), F32).at[0, off:off + GDN_HEADS].set(a.astype(F32))
    tri = jnp.asarray(np.tril(np.ones((GDN_CHUNK, GDN_CHUNK))), BF16)
    tok = lambda blk: (lambda b, t: (b * nt + t, blk))
    return pl.pallas_call(
        _gdn_kernel,
        grid=(BATCH, nt),
        in_specs=[pl.BlockSpec((GDN_TC, width), tok(qkv_col // width)),
                  pl.BlockSpec((GDN_TC, GDN_V), tok(gate_col // GDN_V)),
                  pl.BlockSpec((GDN_TC, LANE), tok(small_col // LANE)),
                  _resident((GDN_CONV, width)), _resident((1, LANE)), _resident((1, LANE)), _resident((1, GDN_DV)),
                  _resident((GDN_CHUNK, GDN_CHUNK))],
        out_specs=pl.BlockSpec((GDN_TC, GDN_V), tok(0)),
        out_shape=jax.ShapeDtypeStruct((TOKENS, GDN_V), BF16),
        scratch_shapes=[pltpu.VMEM((GDN_HALO + GDN_TC, width), F32), pltpu.VMEM((GDN_HEADS, GDN_DK, GDN_DV), F32)],
        compiler_params=pltpu.CompilerParams(dimension_semantics=("arbitrary", "arbitrary"),
                                             vmem_limit_bytes=VMEM_LIMIT),
        name="gated_delta_net",
    )(p32, p32, p32, conv_w, lane_vec(a_log, GDN_SMALL_DECAY), lane_vec(dt_bias, GDN_SMALL_DECAY),
      o_norm.reshape(1, GDN_DV), tri)


N_SLC = SEQ // SLC_LEN
N_CMP = SEQ // CMP_STRIDE
NSA_HG = NSA_HEADS // NSA_GROUPS
NSA_TQ = 128
SEL_STEP = 256
WIN_W = WINDOW + NSA_TQ
NSA_SCALE = NSA_DH ** -0.5


def _cmp_kernel(kc_ref, vc_ref, pe_ref, w1_ref, b1_ref, w2_ref, o_ref, strip_ref):
    half = CMP_STRIDE * NSA_DH
    for kv, x_ref in enumerate((kc_ref, vc_ref)):
        for p in range(CMP_STRIDE):
            xp = x_ref[pl.ds(p, N_CMP, stride=CMP_STRIDE), :]
            for g in range(NSA_GROUPS):
                strip_ref[kv * NSA_GROUPS + g, :, p * NSA_DH:(p + 1) * NSA_DH] = xp[:, g * NSA_DH:(g + 1) * NSA_DH]
    for kv in range(2):
        for g in range(NSA_GROUPS):
            rows = strip_ref[kv * NSA_GROUPS + g]
            lo = (rows + pe_ref[kv, :, :half]).astype(BF16)
            hi = (rows + pe_ref[kv, :, half:]).astype(BF16)
            a = jnp.dot(lo, w1_ref[kv, :half, :], preferred_element_type=F32)
            b = jnp.dot(hi, w1_ref[kv, half:, :], preferred_element_type=F32)
            hid = a + pltpu.roll(b, N_CMP - 1, 0) + b1_ref[kv]
            act = _silu(hid).astype(BF16)
            o_ref[0, kv * NSA_GROUPS + g] = jnp.dot(act, w2_ref[kv], preferred_element_type=F32).astype(BF16)


def nsa_compress(p32, cmp_col, pe, w1, b1, w2):
    n = 2 * NSA_GROUPS
    return pl.pallas_call(
        _cmp_kernel,
        grid=(BATCH,),
        in_specs=[pl.BlockSpec((SEQ, NSA_KV), lambda b: (b, cmp_col // NSA_KV)),
                  pl.BlockSpec((SEQ, NSA_KV), lambda b: (b, cmp_col // NSA_KV + 1)),
                  _resident((2, 1, CMP_LEN * NSA_DH)), _resident((2, CMP_LEN * NSA_DH, CMP_HIDDEN)),
                  _resident((2, 1, CMP_HIDDEN)), _resident((2, CMP_HIDDEN, NSA_DH))],
        out_specs=pl.BlockSpec((1, n, N_CMP, NSA_DH), lambda b: (b, 0, 0, 0)),
        out_shape=jax.ShapeDtypeStruct((BATCH, n, N_CMP, NSA_DH), BF16),
        scratch_shapes=[pltpu.VMEM((n, N_CMP, CMP_STRIDE * NSA_DH), F32)],
        compiler_params=pltpu.CompilerParams(dimension_semantics=("arbitrary",), vmem_limit_bytes=VMEM_LIMIT),
        name="nsa_compress",
    )(p32, p32, pe.reshape(2, 1, -1), w1.astype(BF16), b1.reshape(2, 1, -1), w2.astype(BF16))


V1_ROWS = NSA_DH + 16
NSA_PAIR = 2
RANK_ROWS = 8
NSA_AHEAD = 4
MASK_BIG = 2.0 ** 100


def _transposed_with_ones_row(eye, v):
    v_t = _dot_nt(eye, v).astype(v.dtype)
    row = lax.broadcasted_iota(jnp.int32, (V1_ROWS - NSA_DH, v.shape[0]), 0)
    return jnp.concatenate([v_t, jnp.where(row == 0, 1.0, 0.0).astype(v.dtype)], axis=0)


def _nsa_kernel(q_ref, ks_ref, vs_ref, kw_ref, vw_ref, cmp_ref, gate_ref, ovl_ref, eye_ref, expand_ref,
                o_ref, ksx_ref, qt_ref, ocmp_ref, gsel_ref, gwin_ref, vs1_ref, vw1_ref, vct_ref):
    i = pl.program_id(1)
    q0 = i * NSA_TQ
    eye = eye_ref[...]
    eye_dh = eye[0:NSA_DH, 0:NSA_DH]

    @pl.when(i == 0)
    def _():
        for g in range(NSA_GROUPS):
            lanes = slice(g * NSA_DH, (g + 1) * NSA_DH)
            vs1_ref[g] = _transposed_with_ones_row(eye_dh, vs_ref[:, lanes])
            vw1_ref[g] = _transposed_with_ones_row(eye_dh, vw_ref[:, lanes])
            vct_ref[g] = _dot_nt(eye_dh, cmp_ref[0, NSA_GROUPS + g]).astype(BF16)
            ksx_ref[g] = jnp.concatenate(
                [ks_ref[:, lanes], expand_ref[...], jnp.zeros((SEQ, LANE - NSA_DH - N_SLC), BF16)], axis=1)
            qt_ref[g, NSA_DH + N_SLC:, :] = jnp.zeros((LANE - NSA_DH - N_SLC, NSA_HG * NSA_TQ), BF16)

    gates = jax.nn.sigmoid(gate_ref[...])
    gates_t = jnp.transpose(gates)
    tok_l = q0 + lax.broadcasted_iota(jnp.int32, (1, NSA_TQ), 1)

    groups = range(NSA_GROUPS)
    scale = jnp.asarray(NSA_SCALE, BF16)
    for pair in range(NSA_HEADS // 2):
        pair_t = _dot_nt(eye, q_ref[:, pair * LANE:(pair + 1) * LANE] * scale).astype(BF16)
        for k in range(2):
            g, hk = divmod(2 * pair + k, NSA_HG)
            qt_ref[g, 0:NSA_DH, hk * NSA_TQ:(hk + 1) * NSA_TQ] = pair_t[k * NSA_DH:(k + 1) * NSA_DH]

    def gate_row(g, branch):
        lane = NSA_SMALL_GATE + branch * NSA_HEADS + g * NSA_HG
        return jnp.concatenate([gates_t[lane + hk:lane + hk + 1, :] for hk in range(NSA_HG)], axis=1)

    cmp_end = lax.broadcasted_iota(jnp.int32, (N_CMP, 1), 0) * CMP_STRIDE + (CMP_LEN - 1)
    cvalid = cmp_end <= jnp.concatenate([tok_l] * NSA_HG, axis=1)
    ss = [jnp.dot(cmp_ref[0, g], qt_ref[g, 0:NSA_DH, :], preferred_element_type=F32) for g in groups]
    sms = [jnp.where(cvalid, s, NEG) for s in ss]
    es = [jnp.exp(sm - jnp.max(sm, axis=0, keepdims=True)) for sm in sms]
    ps = [jnp.where(cvalid, e / jnp.sum(e, axis=0, keepdims=True), 0.0) for e in es]
    for g in groups:
        ocmp_ref[g] = gate_row(g, 0) * jnp.dot(vct_ref[g], ps[g].astype(BF16), preferred_element_type=F32)
        gsel_ref[g] = gate_row(g, 1)
        gwin_ref[g] = gate_row(g, 2)

    psums = [sum(p[:, hk * NSA_TQ:(hk + 1) * NSA_TQ] for hk in range(NSA_HG)) for p in ps]
    imps = [jnp.dot(ovl_ref[...], psum.astype(BF16), preferred_element_type=F32) for psum in psums]
    blk = lax.broadcasted_iota(jnp.int32, (N_SLC, NSA_TQ), 0)
    cur = tok_l // SLC_LEN
    bvalid = blk <= cur
    forced = (blk == 0) | (blk == cur) | (blk == cur - 1)
    scores_ = [jnp.where(bvalid, imp + jnp.where(forced, FORCE, 0.0), NEG) for imp in imps]
    tiles = [slice(r, r + RANK_ROWS) for r in range(0, N_SLC, RANK_ROWS)]
    rank_tiles = [[jnp.zeros((RANK_ROWS, NSA_TQ), F32) for _ in tiles] for _ in groups]
    for j in range(N_SLC):
        for g in groups:
            sj = scores_[g][j:j + 1, :]
            for n, rows in enumerate(tiles):
                s_tile = scores_[g][rows]
                if j < rows.start:
                    before = jnp.where(sj >= s_tile, 1.0, 0.0)
                elif j >= rows.stop - 1:
                    before = jnp.where(sj > s_tile, 1.0, 0.0)
                else:
                    tie = jnp.where((sj == s_tile) & (blk[rows] > j), 1.0, 0.0)
                    before = jnp.where(sj > s_tile, 1.0, tie)
                rank_tiles[g][n] = rank_tiles[g][n] + before
    ranks = [jnp.concatenate(rank_tiles[g], axis=0) for g in groups]
    for g in groups:
        off = jnp.where((ranks[g] < SLC_TOPK) & bvalid, 0.0, -MASK_BIG).astype(BF16)
        qt_ref[g, NSA_DH:NSA_DH + N_SLC, :] = jnp.concatenate([off] * NSA_HG, axis=1)

    for n in range(SEQ // SEL_STEP):
        @pl.when(i // (SEL_STEP // NSA_TQ) == n)
        def _():
            w = (n + 1) * SEL_STEP
            w0 = pl.multiple_of(jnp.maximum(q0 + NSA_TQ - WIN_W, 0), NSA_TQ)
            r = pl.ds(w0, WIN_W)

            def bias(pos, lo_excl):
                ok = (pos <= tok_l) if lo_excl is None else ((pos <= tok_l) & (pos > lo_excl))
                return jnp.concatenate([jnp.where(ok, 0.0, NEG)] * NSA_PAIR, axis=1)

            sel_causal = bias((w - SEL_STEP) + lax.broadcasted_iota(jnp.int32, (SEL_STEP, 1), 0), None)
            wpos = w0 + lax.broadcasted_iota(jnp.int32, (WIN_W, 1), 0)
            if n < WINDOW // SEL_STEP:
                win_edges = [(0, WIN_W, bias(wpos, tok_l - WINDOW))]
            else:
                win_edges = [(0, NSA_TQ, bias(wpos[:NSA_TQ], tok_l - WINDOW)),
                             (WIN_W - NSA_TQ, WIN_W, bias(wpos[WIN_W - NSA_TQ:], None))]

            def add_bias(s, edges):
                pieces, at = [], 0
                for lo, hi, b in edges:
                    pieces += [s[at:lo], s[lo:hi] + b]
                    at = hi
                return jnp.concatenate([p for p in pieces + [s[at:]] if p.shape[0]], axis=0)

            problems = []
            for g in range(NSA_GROUPS):
                lanes = slice(g * NSA_DH, (g + 1) * NSA_DH)
                for half in range(NSA_HG // NSA_PAIR):
                    cols = slice(half * NSA_PAIR * NSA_TQ, (half + 1) * NSA_PAIR * NSA_TQ)
                    problems.append((g, cols, 0, lambda g=g: ksx_ref[g, 0:w, :], lambda g=g: vs1_ref[g, :, 0:w],
                                     lambda g=g, cols=cols: qt_ref[g, :, cols], [(w - SEL_STEP, w, sel_causal)]))
                    problems.append((g, cols, 1, lambda lanes=lanes: kw_ref[r, lanes], lambda g=g: vw1_ref[g, :, r],
                                     lambda g=g, cols=cols: qt_ref[g, 0:NSA_DH, cols], win_edges))

            def scores(prob):
                _, _, _, k, _, q_t, edges = prob
                return add_bias(jnp.dot(k(), q_t(), preferred_element_type=F32), edges)

            def probs(s):
                return jnp.exp((s - jnp.max(s, axis=0, keepdims=True)).astype(BF16))

            def values(prob, p):
                acc = jnp.dot(prob[4](), p, preferred_element_type=F32)
                return acc[:NSA_DH] / acc[NSA_DH:NSA_DH + 1]

            problems = problems[1:2] + problems[0:1] + problems[2:]
            outs = {}
            ahead = [scores(prob) for prob in problems[:NSA_AHEAD]]
            for idx, prob in enumerate(problems):
                s_cur = ahead.pop(0)
                if idx + NSA_AHEAD < len(problems):
                    ahead.append(scores(problems[idx + NSA_AHEAD]))
                outs[prob[0], prob[1].start, prob[2]] = values(prob, probs(s_cur))

            for g in range(NSA_GROUPS):
                for half in range(NSA_HG // NSA_PAIR):
                    c0 = half * NSA_PAIR * NSA_TQ
                    cols = slice(c0, c0 + NSA_PAIR * NSA_TQ)
                    o_t = (ocmp_ref[g, :, cols] + gsel_ref[g, :, cols] * outs[g, c0, 0]
                           + gwin_ref[g, :, cols] * outs[g, c0, 1])
                    for k in range(NSA_PAIR):
                        h = g * NSA_HG + half * NSA_PAIR + k
                        o = jnp.transpose(o_t[:, k * NSA_TQ:(k + 1) * NSA_TQ])
                        o_ref[:, h * NSA_DH:(h + 1) * NSA_DH] = o.astype(o_ref.dtype)


def _nsa_constants():
    c0 = np.arange(N_CMP)[:, None] * CMP_STRIDE
    s0 = np.arange(N_SLC)[None, :] * SLC_LEN
    overlap = np.clip(np.minimum(c0 + CMP_LEN, s0 + SLC_LEN) - np.maximum(c0, s0), 0, None) / CMP_STRIDE
    overlap[N_CMP - 1] = 0.0
    expand_t = (np.arange(SEQ)[:, None] // SLC_LEN == np.arange(N_SLC)[None, :]).astype(np.float32)
    return jnp.asarray(overlap.T, BF16), jnp.asarray(np.eye(LANE), BF16), jnp.asarray(expand_t, BF16)


def nsa_attention(p16, cmp_kv, p32, q_col, kv_col, small_col):
    nq = SEQ // NSA_TQ
    ovl_t, eye, expand = _nsa_constants()
    kv_spec = lambda n: pl.BlockSpec((SEQ, NSA_KV), lambda b, i: (b, kv_col // NSA_KV + n))
    return pl.pallas_call(
        _nsa_kernel,
        grid=(BATCH, nq),
        in_specs=[pl.BlockSpec((NSA_TQ, NSA_Q), lambda b, i: (b * nq + i, q_col // NSA_Q)),
                  kv_spec(0), kv_spec(1), kv_spec(2), kv_spec(3),
                  pl.BlockSpec((1, 2 * NSA_GROUPS, N_CMP, NSA_DH), lambda b, i: (b, 0, 0, 0)),
                  pl.BlockSpec((NSA_TQ, LANE), lambda b, i: (b * nq + i, small_col // LANE)),
                  _resident(ovl_t.shape), _resident(eye.shape), _resident(expand.shape)],
        out_specs=pl.BlockSpec((NSA_TQ, NSA_Q), lambda b, i: (b * nq + i, 0)),
        out_shape=jax.ShapeDtypeStruct((TOKENS, NSA_Q), BF16),
        scratch_shapes=[pltpu.VMEM((NSA_GROUPS, SEQ, LANE), BF16),
                        pltpu.VMEM((NSA_GROUPS, LANE, NSA_HG * NSA_TQ), BF16),
                        pltpu.VMEM((NSA_GROUPS, NSA_DH, NSA_HG * NSA_TQ), F32)]
        + [pltpu.VMEM((NSA_GROUPS, 1, NSA_HG * NSA_TQ), F32)] * 2
        + [pltpu.VMEM((NSA_GROUPS, V1_ROWS, SEQ), BF16)] * 2
        + [pltpu.VMEM((NSA_GROUPS, NSA_DH, N_CMP), BF16)],
        compiler_params=pltpu.CompilerParams(dimension_semantics=("arbitrary", "arbitrary"),
                                             vmem_limit_bytes=VMEM_LIMIT),
        name="nsa_attention",
    )(p16, p16, p16, p16, p16, cmp_kv, p32, ovl_t, eye, expand)


EVEN_QKV_COL = 0
EVEN_GATE_COL = 2 * GDN_QK + GDN_V
EVEN_CMP_COL = EVEN_GATE_COL + GDN_V
EVEN_SMALL_COL = EVEN_CMP_COL + 2 * NSA_KV
EVEN_F32_W = EVEN_SMALL_COL + LANE
EVEN_Q_COL = 0
EVEN_KV_COL = NSA_Q
EVEN_BF16_W = NSA_Q + 4 * NSA_KV
assert NSA_SMALL_GATE + 3 * NSA_HEADS <= LANE


def even_proj_weight(w_in):
    qkv_a, beta, dec, gate_a, q_b, kv_b, gate_b = split_last(w_in, EVEN_SPLITS)
    pad = jnp.zeros((w_in.shape[0], LANE - 2 * GDN_HEADS - 3 * NSA_HEADS), w_in.dtype)
    return jnp.concatenate([qkv_a, gate_a, kv_b[:, :2 * NSA_KV], beta, dec, gate_b, pad, q_b, kv_b[:, 2 * NSA_KV:]],
                           axis=1).astype(BF16)


def even_mixer_core(p32, p16, conv_w, A_log, dt_bias, o_norm, cmp_pe, cmp_w1, cmp_b1, cmp_w2):
    oa = gated_delta_net(p32, conv_w, A_log, dt_bias, o_norm, EVEN_QKV_COL, EVEN_GATE_COL, EVEN_SMALL_COL)
    cmp_kv = nsa_compress(p32, EVEN_CMP_COL, cmp_pe, cmp_w1, cmp_b1, cmp_w2)
    ob = nsa_attention(p16, cmp_kv, p32, EVEN_Q_COL, EVEN_KV_COL, EVEN_SMALL_COL)
    return oa, ob


GLA_TC = 512
ODD_Q_COL = 0
ODD_K_COL = GLA_QK
ODD_V_COL = 2 * GLA_QK
ODD_R_COL = ODD_V_COL + GLA_V
ODD_LR_COL = ODD_R_COL + GLA_V
ODD_F32_W = ODD_LR_COL + LANE


def _gla_kernel(q_ref, k_ref, v_ref, r_ref, lr_ref, aw_ref, ab_ref, onorm_ref, tri_ref, o_ref, state_ref):
    t = pl.program_id(1)
    C = GLA_CHUNK

    @pl.when(t == 0)
    def _():
        state_ref[...] = jnp.zeros(state_ref.shape, F32)

    z = jnp.dot(lr_ref[...].astype(BF16), aw_ref[...], preferred_element_type=F32) + ab_ref[...]
    log_a = (jnp.minimum(z, 0.0) - jnp.log(1.0 + jnp.exp(-jnp.abs(z)))) * (1.0 / GLA_TAU)
    tri = tri_ref[...]
    causal = lax.broadcasted_iota(jnp.int32, (C, C), 0) >= lax.broadcasted_iota(jnp.int32, (C, C), 1)

    n_chunks = GLA_TC // C
    b_alls = [_cumsum_rows(tri, log_a[c * C:(c + 1) * C]) for c in range(n_chunks)]
    work = {}
    for c in range(n_chunks):
        rows = slice(c * C, (c + 1) * C)
        for h in range(GLA_HEADS):
            kl = slice(h * GLA_DK, (h + 1) * GLA_DK)
            b = b_alls[c][:, kl]
            b_last = b[C - 1:C]
            k = k_ref[rows, kl]
            work[c, h] = dict(
                rows=rows, vl=slice(h * GLA_DV, (h + 1) * GLA_DV),
                v=v_ref[rows, h * GLA_DV:(h + 1) * GLA_DV].astype(BF16),
                qt=(q_ref[rows, kl] * GLA_DK ** -0.5 * jnp.exp(b)).astype(BF16),
                kt=(k * jnp.exp(-b)).astype(BF16),
                kh=(k * jnp.exp(b_last - b)).astype(BF16),
                keep=jnp.exp(b_last))
    wks = [work[key] for key in sorted(work)]
    intras = [jnp.where(causal, _dot_nt(wk["qt"], wk["kt"]), 0.0).astype(BF16) for wk in wks]
    o_intras = [jnp.dot(intra, wk["v"], preferred_element_type=F32) for wk, intra in zip(wks, intras)]
    updates = [_dot_tn(wk["v"], wk["kh"]) for wk in wks]

    states = [state_ref[h] for h in range(GLA_HEADS)]
    starts = []
    for (c, h), wk, upd in zip(sorted(work), wks, updates):
        starts.append(states[h].astype(BF16))
        states[h] = states[h] * wk["keep"] + upd
    for h in range(GLA_HEADS):
        state_ref[h] = states[h]

    o_inters = [_dot_nt(wk["qt"], s0) for wk, s0 in zip(wks, starts)]
    for wk, o_intra, o_inter in zip(wks, o_intras, o_inters):
        o = o_intra + o_inter
        r = r_ref[wk["rows"], wk["vl"]]
        on = o * lax.rsqrt(jnp.mean(o * o, axis=-1, keepdims=True) + EPS) * onorm_ref[...]
        o_ref[wk["rows"], wk["vl"]] = (on * _silu(r)).astype(o_ref.dtype)


def gated_linear_attention(p32, a_w2, a_b, o_norm):
    nt = SEQ // GLA_TC
    tok = lambda blk: (lambda b, t: (b * nt + t, blk))
    aw = jnp.pad(a_w2, ((0, LANE - GLA_RANK), (0, 0))).astype(BF16)
    tri = jnp.asarray(np.tril(np.ones((GLA_CHUNK, GLA_CHUNK))), BF16)
    return pl.pallas_call(
        _gla_kernel,
        grid=(BATCH, nt),
        in_specs=[pl.BlockSpec((GLA_TC, GLA_QK), tok(ODD_Q_COL // GLA_QK)),
                  pl.BlockSpec((GLA_TC, GLA_QK), tok(ODD_K_COL // GLA_QK)),
                  pl.BlockSpec((GLA_TC, GLA_V), tok(ODD_V_COL // GLA_V)),
                  pl.BlockSpec((GLA_TC, GLA_V), tok(ODD_R_COL // GLA_V)),
                  pl.BlockSpec((GLA_TC, LANE), tok(ODD_LR_COL // LANE)),
                  _resident((LANE, GLA_QK)), _resident((1, GLA_QK)), _resident((1, GLA_DV)),
                  _resident((GLA_CHUNK, GLA_CHUNK))],
        out_specs=pl.BlockSpec((GLA_TC, GLA_V), tok(0)),
        out_shape=jax.ShapeDtypeStruct((TOKENS, GLA_V), BF16),
        scratch_shapes=[pltpu.VMEM((GLA_HEADS, GLA_DV, GLA_DK), F32)],
        compiler_params=pltpu.CompilerParams(dimension_semantics=("arbitrary", "arbitrary"),
                                             vmem_limit_bytes=VMEM_LIMIT),
        name="gated_linear_attention",
    )(p32, p32, p32, p32, p32, aw, a_b.reshape(1, GLA_QK), o_norm.reshape(1, GLA_DV), tri)


def _pad_cols(w, n):
    return jnp.pad(w, ((0, 0), (0, n - w.shape[1])))


def kernel(x, norm_gains, ffn_w_in, ffn_conv_w, ffn_conv_b, ffn_w_out, hyb_w_in, hyb_w_out, gdn_conv_w, gdn_A_log,
           gdn_dt_bias, gdn_o_norm, nsa_cmp_pe, nsa_cmp_w1, nsa_cmp_b1, nsa_cmp_w2, gla_w_in, gla_a_w2, gla_a_b,
           gla_o_norm, gla_w_out):
    x2 = x.reshape(TOKENS, D_MODEL)
    ffn_w_in_bf = ffn_w_in.astype(BF16)
    ffn_w_out_bf = ffn_w_out.astype(BF16)
    for layer in range(DEPTH):
        if layer % 2 == 0:
            e = layer // 2
            p32, p16 = norm_proj(x2, norm_gains[layer, 0], even_proj_weight(hyb_w_in[e]),
                                 ((EVEN_F32_W, F32), (EVEN_BF16_W, BF16)))
            oa, ob = even_mixer_core(p32, p16, gdn_conv_w[e], gdn_A_log[e], gdn_dt_bias[e], gdn_o_norm[e],
                                     nsa_cmp_pe[e], nsa_cmp_w1[e], nsa_cmp_b1[e], nsa_cmp_w2[e])
            halves = (oa, 0, ob, 0)
            w_mix = hyb_w_out[e]
        else:
            o_ = layer // 2
            w = _pad_cols(gla_w_in[o_], ODD_F32_W).astype(BF16)
            (p32,) = norm_proj(x2, norm_gains[layer, 0], w, ((ODD_F32_W, F32),))
            o = gated_linear_attention(p32, gla_a_w2[o_], gla_a_b[o_], gla_o_norm[o_])
            halves = (o, 0, o, 1)
            w_mix = gla_w_out[o_]
        x2 = mix_out_conv_ffn(*halves, x2, norm_gains[layer, 1], w_mix.astype(BF16), norm_gains[layer, 2],
                              layer, ffn_w_in_bf, ffn_conv_w, ffn_conv_b, ffn_w_out_bf, norm_gains[layer, 3])
    return x2.reshape(BATCH, SEQ, D_MODEL)
```
